```python
import math
import numpy as np
import jax
import jax.numpy as jnp
from jax import lax

D_MODEL = 2048
BATCH = 8
SEQ = 2048
DEPTH = 1

RWKV_HEAD_DIM = 64
RWKV_WIDTH = D_MODEL // 2
RWKV_HEADS = RWKV_WIDTH // RWKV_HEAD_DIM
DECAY_LORA = 64
ICLR_LORA = 64
GATE_LORA = 160
RWKV_GN_EPS = 64e-5

NSA_HEAD_DIM = 64
NSA_WIDTH = D_MODEL // 2
NSA_HEADS = NSA_WIDTH // NSA_HEAD_DIM
NSA_KV_HEADS = 4
NSA_GROUP = NSA_HEADS // NSA_KV_HEADS
NSA_KV_WIDTH = NSA_KV_HEADS * NSA_HEAD_DIM
CMP_BLOCK = 32
CMP_STRIDE = 16
SEL_BLOCK = 64
N_SEL = 8
WINDOW = 512
QUERY_BLOCK = 128

REL_BUCKETS = 32
REL_MAX_DIST = 128

D_FF = 4 * D_MODEL
NORM_EPS = 1e-6
NEG_INF = -1e30
FORCE_SCORE = 1e4

RWKV_COLS = 3 * RWKV_WIDTH + DECAY_LORA + ICLR_LORA + GATE_LORA
NSA_COLS = NSA_WIDTH + 6 * NSA_KV_WIDTH + 3 * NSA_HEADS
MERGE_COLS = 2 * D_MODEL
IN_COLS = RWKV_COLS + NSA_COLS + MERGE_COLS

kernel_name = 'rwkv7_nsa_hybrid_block'


def rms_norm(x, g, eps=NORM_EPS):
    xf = x.astype(jnp.float32)
    y = xf * lax.rsqrt(jnp.mean(xf * xf, axis=-1, keepdims=True) + eps)
    return (y * g.astype(jnp.float32)).astype(x.dtype)


def token_shift(p):
    return jnp.pad(p, ((0, 0), (1, 0), (0, 0)))[:, :-1]


def rel_bucket(rel):
    n = jnp.maximum(rel, 0)
    max_exact = REL_BUCKETS // 2
    nf = jnp.maximum(n, max_exact).astype(jnp.float32)
    large = max_exact + (jnp.log(nf / max_exact) / math.log(REL_MAX_DIST / max_exact)
                         * (REL_BUCKETS - max_exact)).astype(jnp.int32)
    large = jnp.minimum(large, REL_BUCKETS - 1)
    return jnp.where(n < max_exact, n, large)


def masked_softmax(s, mask):
    s = jnp.where(mask, s.astype(jnp.float32), NEG_INF)
    p = jax.nn.softmax(s, axis=-1)
    return jnp.where(mask, p, 0.0)


def cmp_to_sel_matrix(T):
    nc = T // CMP_STRIDE - CMP_BLOCK // CMP_STRIDE + 1
    ns = T // SEL_BLOCK
    cs = np.arange(nc) * CMP_STRIDE
    ss = np.arange(ns) * SEL_BLOCK
    lo = np.maximum(cs[:, None], ss[None, :])
    hi = np.minimum(cs[:, None] + CMP_BLOCK, ss[None, :] + SEL_BLOCK)
    return (np.maximum(hi - lo, 0) / CMP_BLOCK).astype(np.float32)


def compress(x, pe, w1, w2):
    B, T, G, hd = x.shape
    n_sub = CMP_BLOCK // CMP_STRIDE
    nc = T // CMP_STRIDE - n_sub + 1
    sub = x.reshape(B, T // CMP_STRIDE, CMP_STRIDE, G, hd)
    blocks = jnp.concatenate([sub[:, j:j + nc] for j in range(n_sub)], axis=2)
    blocks = blocks + pe[:, None, :]
    flat = jnp.moveaxis(blocks, 3, 2).reshape(B, nc, G, CMP_BLOCK * hd)
    return jax.nn.gelu(flat @ w1) @ w2


def rwkv7_mix(p, mu, w0, w2, a0, a2, g2, k_k, k_a, r_k, ln_w, ln_b):
    B, T, _ = p.shape
    H, N, C = RWKV_HEADS, RWKV_HEAD_DIM, RWKV_WIDTH
    p = (p + (token_shift(p) - p) * mu).astype(jnp.float32)
    cuts = np.cumsum([C, C, C, DECAY_LORA, ICLR_LORA]).tolist()
    r, k, v, xw, xa, xg = jnp.split(p, cuts, axis=-1)
    w = -jax.nn.softplus(-(w0 + jnp.tanh(xw) @ w2)) - 0.5
    a = jax.nn.sigmoid(a0 + xa @ a2)
    g = jax.nn.sigmoid(xg) @ g2
    heads = lambda z: z.reshape(B, T, H, N)
    kk = heads(k * k_k)
    kk = kk / jnp.maximum(jnp.sqrt(jnp.sum(kk * kk, axis=-1, keepdims=True)), 1e-12)
    k = heads(k * (1.0 + (a - 1.0) * k_a))
    r, v, a = heads(r), heads(v), heads(a)
    decay = jnp.exp(-jnp.exp(heads(w)))
    seq = tuple(jnp.moveaxis(z, 1, 0) for z in (r, decay, k, v, -kk, kk * a))

    def step(S, inp):
        r_t, w_t, k_t, v_t, a_t, b_t = inp
        sa = jnp.einsum('bhij,bhj->bhi', S, a_t)
        S = S * w_t[:, :, None, :] + sa[..., None] * b_t[:, :, None, :] + v_t[..., None] * k_t[:, :, None, :]
        return S, jnp.einsum('bhij,bhj->bhi', S, r_t)

    S0 = jnp.zeros((B, H, N, N), jnp.float32)
    _, y = lax.scan(step, S0, seq)
    y = jnp.moveaxis(y, 0, 1)
    mean = jnp.mean(y, axis=-1, keepdims=True)
    var = jnp.mean(jnp.square(y - mean), axis=-1, keepdims=True)
    y = ((y - mean) * lax.rsqrt(var + RWKV_GN_EPS)).reshape(B, T, C) * ln_w + ln_b
    bonus = (jnp.sum(r * k * r_k, axis=-1, keepdims=True) * v).reshape(B, T, C)
    return (y + bonus) * g


def nsa_mix(p, pe_k, w1_k, w2_k, pe_v, w1_v, w2_v, q_g, k_g, rel_bias):
    B, T, _ = p.shape
    G, Hg, hd = NSA_KV_HEADS, NSA_GROUP, NSA_HEAD_DIM
    QB = QUERY_BLOCK
    q = p[..., :NSA_WIDTH].reshape(B, T, G, Hg, hd)
    kv = p[..., NSA_WIDTH:NSA_WIDTH + 6 * NSA_KV_WIDTH].reshape(B, T, 6, G, hd)
    gates = jax.nn.sigmoid(p[..., NSA_WIDTH + 6 * NSA_KV_WIDTH:]).reshape(B, T, G, Hg, 3)
    q = rms_norm(q, q_g) * (hd ** -0.5)
    ns = T // SEL_BLOCK
    kc = rms_norm(compress(kv[:, :, 0], pe_k, w1_k, w2_k), k_g[0])
    vc = compress(kv[:, :, 1], pe_v, w1_v, w2_v)
    nc = kc.shape[1]
    ks = rms_norm(kv[:, :, 2], k_g[1]).reshape(B, ns, SEL_BLOCK, G, hd).transpose(0, 3, 1, 2, 4)
    vs = kv[:, :, 3].reshape(B, ns, SEL_BLOCK, G, hd).transpose(0, 3, 1, 2, 4)
    pad = ((0, 0), (WINDOW, 0), (0, 0), (0, 0))
    kw = jnp.pad(rms_norm(kv[:, :, 4], k_g[2]), pad)
    vw = jnp.pad(kv[:, :, 5], pad)
    table = rel_bias.reshape(REL_BUCKETS, G, Hg)
    table_g = jnp.transpose(table, (1, 0, 2))
    cmp_end = jnp.arange(nc) * CMP_STRIDE + CMP_BLOCK - 1
    sel_m = jnp.asarray(cmp_to_sel_matrix(T))
    k_sel = min(N_SEL, ns)
    b_ix = jnp.arange(B)[:, None, None, None]
    g_ix = jnp.arange(G)[None, :, None, None]
    blk = jnp.arange(ns)

    def query_block(i):
        t0 = i * QB
        t = t0 + jnp.arange(QB)
        qb = lax.dynamic_slice_in_dim(q, t0, QB, axis=1)
        gb = lax.dynamic_slice_in_dim(gates, t0, QB, axis=1)
        rel_c = t[:, None] - cmp_end[None, :]
        bias_c = jnp.transpose(table[rel_bucket(rel_c)], (2, 3, 0, 1))
        s_c = jnp.einsum('btghd,bngd->bghtn', qb, kc) + bias_c
        p_c = masked_softmax(s_c, rel_c >= 0)
        o_c = jnp.einsum('bghtn,bngd->btghd', p_c.astype(vc.dtype), vc)
        imp = jnp.einsum('bghtn,nj->bgtj', p_c, sel_m)
        cur = t[:, None] // SEL_BLOCK
        allowed = blk[None, :] <= cur
        forced = (blk[None, :] == 0) | (blk[None, :] == cur) | (blk[None, :] == cur - 1)
        score = jnp.where(forced, FORCE_SCORE, jnp.where(allowed, imp, -1.0))
        _, idx = lax.top_k(score, k_sel)
        kg = ks[b_ix, g_ix, idx]
        vg = vs[b_ix, g_ix, idx]
        pos = idx[..., None] * SEL_BLOCK + jnp.arange(SEL_BLOCK)
        rel_s = t[None, None, :, None, None] - pos
        bias_s = jnp.moveaxis(table_g[g_ix[..., None], rel_bucket(rel_s)], -1, 2)
        s_s = jnp.einsum('btghd,bgtksd->bghtks', qb, kg) + bias_s
        mask_s = (rel_s >= 0)[:, :, None].reshape(B, G, 1, QB, k_sel * SEL_BLOCK)
        p_s = masked_softmax(s_s.reshape(B, G, Hg, QB, k_sel * SEL_BLOCK), mask_s)
        p_s = p_s.reshape(B, G, Hg, QB, k_sel, SEL_BLOCK).astype(vg.dtype)
        o_s = jnp.einsum('bghtks,bgtksd->btghd', p_s, vg)
        kwb = lax.dynamic_slice_in_dim(kw, t0, QB + WINDOW, axis=1)
        vwb = lax.dynamic_slice_in_dim(vw, t0, QB + WINDOW, axis=1)
        kpos = t0 - WINDOW + jnp.arange(QB + WINDOW)
        rel_w = t[:, None] - kpos[None, :]
        mask_w = (rel_w >= 0) & (rel_w < WINDOW) & (kpos[None, :] >= 0)
        bias_w = jnp.transpose(table[rel_bucket(rel_w)], (2, 3, 0, 1))
        s_w = jnp.einsum('btghd,bsgd->bghts', qb, kwb) + bias_w
        p_w = masked_softmax(s_w, mask_w).astype(vwb.dtype)
        o_w = jnp.einsum('bghts,bsgd->btghd', p_w, vwb)
        return gb[..., 0:1] * o_c + gb[..., 1:2] * o_s + gb[..., 2:3] * o_w

    outs = lax.map(query_block, jnp.arange(T // QB))
    return jnp.moveaxis(outs, 0, 1).reshape(B, T, NSA_WIDTH)


def setup_inputs(seed: int = 0) -> dict:
    key = jax.random.key(seed)
    keys = iter(jax.random.split(key, 40))
    nrm = lambda shape, scale: scale * jax.random.normal(next(keys), shape, jnp.float32)
    L, D, hd = DEPTH, D_MODEL, NSA_HEAD_DIM
    return {
        'x': nrm((BATCH, SEQ, D), 1.0),
        'c': nrm((BATCH, D), 1.0),
        'w_ada': nrm((L, D, 6 * D), 0.5 * D ** -0.5),
        'b_ada': nrm((L, 6 * D), 0.01),
        'norm1_g': 1.0 + nrm((L, D), 0.02),
        'norm2_g': 1.0 + nrm((L, D), 0.02),
        'w_in': nrm((L, D, IN_COLS), D ** -0.5),
        'rwkv_mu': jax.random.uniform(next(keys), (L, RWKV_COLS), jnp.float32),
        'rwkv_w0': jax.random.uniform(next(keys), (L, RWKV_WIDTH), jnp.float32, -6.0, -1.0),
        'rwkv_w2': nrm((L, DECAY_LORA, RWKV_WIDTH), 0.1 * DECAY_LORA ** -0.5),
        'rwkv_a0': nrm((L, RWKV_WIDTH), 0.1),
        'rwkv_a2': nrm((L, ICLR_LORA, RWKV_WIDTH), 0.1 * ICLR_LORA ** -0.5),
        'rwkv_g2': nrm((L, GATE_LORA, RWKV_WIDTH), GATE_LORA ** -0.5),
        'rwkv_k_k': 0.85 + nrm((L, RWKV_WIDTH), 0.05),
        'rwkv_k_a': 1.0 + nrm((L, RWKV_WIDTH), 0.05),
        'rwkv_r_k': nrm((L, RWKV_HEADS, RWKV_HEAD_DIM), 0.1),
        'rwkv_ln_w': 1.0 + nrm((L, RWKV_WIDTH), 0.02),
        'rwkv_ln_b': nrm((L, RWKV_WIDTH), 0.01),
        'cmp_pe_k': nrm((L, CMP_BLOCK, hd), 0.02),
        'cmp_w1_k': nrm((L, CMP_BLOCK * hd, hd), (CMP_BLOCK * hd) ** -0.5),
        'cmp_w2_k': nrm((L, hd, hd), hd ** -0.5),
        'cmp_pe_v': nrm((L, CMP_BLOCK, hd), 0.02),
        'cmp_w1_v': nrm((L, CMP_BLOCK * hd, hd), (CMP_BLOCK * hd) ** -0.5),
        'cmp_w2_v': nrm((L, hd, hd), hd ** -0.5),
        'q_norm_g': 1.0 + nrm((L, hd), 0.02),
        'k_norm_g': 1.0 + nrm((L, 3, hd), 0.02),
        'rel_bias': nrm((REL_BUCKETS, NSA_HEADS), 0.5),
        'w_o_rwkv': nrm((L, RWKV_WIDTH, D), RWKV_WIDTH ** -0.5),
        'w_o_nsa': nrm((L, NSA_WIDTH, D), NSA_WIDTH ** -0.5),
        'w_out': nrm((L, D, D), D ** -0.5),
        'w_up': nrm((L, D, D_FF), D ** -0.5),
        'w_down': nrm((L, D_FF, D), D_FF ** -0.5),
    }


def reference(x, c, w_ada, b_ada, norm1_g, norm2_g, w_in, rwkv_mu, rwkv_w0, rwkv_w2, rwkv_a0, rwkv_a2,
              rwkv_g2, rwkv_k_k, rwkv_k_a, rwkv_r_k, rwkv_ln_w, rwkv_ln_b, cmp_pe_k, cmp_w1_k, cmp_w2_k,
              cmp_pe_v, cmp_w1_v, cmp_w2_v, q_norm_g, k_norm_g, rel_bias, w_o_rwkv, w_o_nsa, w_out,
              w_up, w_down):
    D = D_MODEL
    for l in range(DEPTH):
        mod = jnp.einsum('bd,de->be', jax.nn.silu(c), w_ada[l]) + b_ada[l]
        sh1, sc1, gt1, sh2, sc2, gt2 = jnp.split(mod[:, None, :], 6, axis=-1)
        h = rms_norm(x, norm1_g[l]) * (1.0 + sc1) + sh1
        proj = jnp.einsum('btd,dc->btc', h, w_in[l])
        p_rwkv = proj[..., :RWKV_COLS]
        p_nsa = proj[..., RWKV_COLS:RWKV_COLS + NSA_COLS]
        merge_g = jax.nn.sigmoid(proj[..., RWKV_COLS + NSA_COLS:])
        o_a = rwkv7_mix(p_rwkv, rwkv_mu[l], rwkv_w0[l], rwkv_w2[l], rwkv_a0[l], rwkv_a2[l], rwkv_g2[l],
                        rwkv_k_k[l], rwkv_k_a[l], rwkv_r_k[l], rwkv_ln_w[l], rwkv_ln_b[l]).astype(x.dtype)
        o_b = nsa_mix(p_nsa, cmp_pe_k[l], cmp_w1_k[l], cmp_w2_k[l], cmp_pe_v[l], cmp_w1_v[l], cmp_w2_v[l],
                      q_norm_g[l], k_norm_g[l], rel_bias)
        y_a = o_a @ w_o_rwkv[l]
        y_b = o_b @ w_o_nsa[l]
        mixed = merge_g[..., :D] * y_a + merge_g[..., D:] * y_b
        x = x + gt1 * (mixed @ w_out[l])
        h = rms_norm(x, norm2_g[l]) * (1.0 + sc2) + sh2
        x = x + gt2 * (jnp.square(jax.nn.relu(h @ w_up[l])) @ w_down[l])
    return x
```

```python
import functools
import math

import numpy as np
import jax
import jax.numpy as jnp
from jax import lax
from jax.experimental import pallas as pl
from jax.experimental.pallas import tpu as pltpu

F32 = jnp.float32
BF16 = jnp.bfloat16
HIGHEST = lax.Precision.HIGHEST

D_MODEL = 2048
HEAD_DIM = 64
RWKV_WIDTH = D_MODEL // 2
DECAY_LORA = 64
ICLR_LORA = 64
GATE_LORA = 160
RWKV_GN_EPS = 64e-5
NSA_WIDTH = D_MODEL // 2
NSA_HEADS = NSA_WIDTH // HEAD_DIM
NSA_KV_HEADS = 4
NSA_GROUP = NSA_HEADS // NSA_KV_HEADS
NSA_KV_WIDTH = NSA_KV_HEADS * HEAD_DIM
CMP_BLOCK = 32
CMP_STRIDE = 16
SEL_BLOCK = 64
N_SEL = 8
WINDOW = 512
QUERY_BLOCK = 128
REL_BUCKETS = 32
REL_MAX_DIST = 128
D_FF = 4 * D_MODEL
NORM_EPS = 1e-6
NEG_INF = -1e30
FORCE_SCORE = 1e4

RWKV_COLS = 3 * RWKV_WIDTH + DECAY_LORA + ICLR_LORA + GATE_LORA
NSA_COLS = NSA_WIDTH + 6 * NSA_KV_WIDTH + 3 * NSA_HEADS

V7X_VMEM_BYTES = 64 * 1024 * 1024
LANES = 128

COL_RKV = 0
COL_Q = 3 * RWKV_WIDTH
COL_MERGE = COL_Q + NSA_WIDTH
COL_KV = COL_MERGE + 2 * D_MODEL
COL_SMALL = COL_KV + 6 * NSA_KV_WIDTH
SMALL_W = 512
PROJ_COLS = COL_SMALL + SMALL_W
SM_XG = DECAY_LORA + ICLR_LORA
SM_GATES = SM_XG + GATE_LORA


def _vmem_limit(nbytes):
    return int(min(nbytes * 5 // 4 + (4 << 20), V7X_VMEM_BYTES - (8 << 20)))


def _cparams(sem, vmem_bytes):
    return pltpu.CompilerParams(dimension_semantics=sem, vmem_limit_bytes=_vmem_limit(vmem_bytes))


def _ada_kernel(c_ref, w_ref, b_ref, o_ref):
    c = c_ref[...]
    s = c * jax.nn.sigmoid(c)
    o_ref[...] = jnp.dot(s.astype(BF16), w_ref[...].astype(BF16), preferred_element_type=F32) + b_ref[...]


def _ada_mod(c, w_ada, b_ada):
    B, D = c.shape
    N = w_ada.shape[1]
    tn = 1024
    return pl.pallas_call(
        _ada_kernel,
        grid=(N // tn,),
        in_specs=[
            pl.BlockSpec((B, D), lambda j: (0, 0)),
            pl.BlockSpec((D, tn), lambda j: (0, j)),
            pl.BlockSpec((1, tn), lambda j: (0, j)),
        ],
        out_specs=pl.BlockSpec((B, tn), lambda j: (0, j)),
        out_shape=jax.ShapeDtypeStruct((B, N), F32),
        compiler_params=_cparams(("parallel",), 2 * D * tn * 4 + D * tn * 2),
        name="ada_mod",
    )(c, w_ada, b_ada.reshape(1, N))


def _modulated_norm(x, g, sc, sh):
    ms = jnp.mean(x * x, axis=-1, keepdims=True)
    return (x * lax.rsqrt(ms + NORM_EPS) * g) * (1.0 + sc) + sh


def _inproj_kernel(x_ref, g_ref, sh_ref, sc_ref, w_ref, o_ref, h_scr):
    @pl.when(pl.program_id(1) == 0)
    def _():
        h_scr[...] = _modulated_norm(x_ref[...], g_ref[...], sc_ref[0], sh_ref[0]).astype(BF16)

    o_ref[...] = jnp.dot(h_scr[...], w_ref[...], preferred_element_type=F32)


def _in_proj(x2, g1, mod6, w_in_p, T):
    BT, D = x2.shape
    NP = w_in_p.shape[1]
    tm = min(512, T)
    tn = 1024
    tpb = T // tm
    vm = 2 * tm * D * 4 + tm * D * 2 + 2 * D * tn * 2 + 2 * tm * tn * 4 + 2 * tm * D * 4
    return pl.pallas_call(
        _inproj_kernel,
        grid=(BT // tm, NP // tn),
        in_specs=[
            pl.BlockSpec((tm, D), lambda i, j: (i, 0)),
            pl.BlockSpec((1, D), lambda i, j: (0, 0)),
            pl.BlockSpec((1, 1, D), lambda i, j: ((i // tpb) * 6 + 0, 0, 0)),
            pl.BlockSpec((1, 1, D), lambda i, j: ((i // tpb) * 6 + 1, 0, 0)),
            pl.BlockSpec((D, tn), lambda i, j: (0, j)),
        ],
        out_specs=pl.BlockSpec((tm, tn), lambda i, j: (i, j)),
        out_shape=jax.ShapeDtypeStruct((BT, NP), F32),
        scratch_shapes=[pltpu.VMEM((tm, D), BF16)],
        compiler_params=_cparams(("parallel", "arbitrary"), vm),
        name="in_proj",
    )(x2, g1, mod6, mod6, w_in_p)


def _merge_kernel(oa_ref, ob_ref, wa_ref, wb_ref, ga_ref, gb_ref, o_ref):
    ya = jnp.dot(oa_ref[...], wa_ref[...], preferred_element_type=F32)
    yb = jnp.dot(ob_ref[...], wb_ref[...], preferred_element_type=F32)
    o_ref[...] = (jax.nn.sigmoid(ga_ref[...]) * ya + jax.nn.sigmoid(gb_ref[...]) * yb).astype(BF16)


def _merge(o_a, o_b, w_oa, w_ob, proj):
    BT, W = o_a.shape
    D = w_oa.shape[1]
    tm, tn = 512, 1024
    ga0 = COL_MERGE // tn
    gb0 = (COL_MERGE + D) // tn
    vm = 2 * (2 * tm * W * 2 + 2 * W * tn * 2 + 2 * tm * tn * 4 + tm * tn * 2) + 3 * tm * tn * 4
    return pl.pallas_call(
        _merge_kernel,
        grid=(BT // tm, D // tn),
        in_specs=[
            pl.BlockSpec((tm, W), lambda i, j: (i, 0)),
            pl.BlockSpec((tm, W), lambda i, j: (i, 0)),
            pl.BlockSpec((W, tn), lambda i, j: (0, j)),
            pl.BlockSpec((W, tn), lambda i, j: (0, j)),
            pl.BlockSpec((tm, tn), lambda i, j: (i, ga0 + j)),
            pl.BlockSpec((tm, tn), lambda i, j: (i, gb0 + j)),
        ],
        out_specs=pl.BlockSpec((tm, tn), lambda i, j: (i, j)),
        out_shape=jax.ShapeDtypeStruct((BT, D), BF16),
        compiler_params=_cparams(("parallel", "parallel"), vm),
        name="merge",
    )(o_a, o_b, w_oa, w_ob, proj, proj)


def _outproj_kernel(m_ref, w_ref, x_ref, gt_ref, g_ref, sh_ref, sc_ref, x1_ref, h2_ref):
    y = jnp.dot(m_ref[...], w_ref[...], preferred_element_type=F32)
    x1 = x_ref[...] + gt_ref[0] * y
    x1_ref[...] = x1
    h2_ref[...] = _modulated_norm(x1, g_ref[...], sc_ref[0], sh_ref[0]).astype(BF16)


def _out_proj(mixed, w_out, x2, mod6, g2, T):
    BT, D = x2.shape
    tm = min(512, T)
    tpb = T // tm
    vm = 2 * (tm * D * 2 + D * D * 2 + tm * D * 4 + tm * D * 4 + tm * D * 2) + 3 * tm * D * 4
    mod_spec = lambda k: pl.BlockSpec((1, 1, D), lambda i: ((i // tpb) * 6 + k, 0, 0))
    return pl.pallas_call(
        _outproj_kernel,
        grid=(BT // tm,),
        in_specs=[
            pl.BlockSpec((tm, D), lambda i: (i, 0)),
            pl.BlockSpec((D, D), lambda i: (0, 0)),
            pl.BlockSpec((tm, D), lambda i: (i, 0)),
            mod_spec(2),
            pl.BlockSpec((1, D), lambda i: (0, 0)),
            mod_spec(3),
            mod_spec(4),
        ],
        out_specs=[pl.BlockSpec((tm, D), lambda i: (i, 0)), pl.BlockSpec((tm, D), lambda i: (i, 0))],
        out_shape=[jax.ShapeDtypeStruct((BT, D), F32), jax.ShapeDtypeStruct((BT, D), BF16)],
        compiler_params=_cparams(("parallel",), vm),
        name="out_proj",
    )(mixed, w_out, x2, mod6, g2, mod6, mod6)


def _mlp_kernel(h_ref, wu_ref, wd_ref, x_ref, gt_ref, o_ref, acc_ref):
    f = pl.program_id(1)
    u = jnp.dot(h_ref[...], wu_ref[...], preferred_element_type=F32)
    u = jnp.square(jnp.maximum(u, 0.0)).astype(BF16)
    part = jnp.dot(u, wd_ref[...], preferred_element_type=F32)

    @pl.when(f == 0)
    def _():
        acc_ref[...] = part

    @pl.when(f > 0)
    def _():
        acc_ref[...] += part

    @pl.when(f == pl.num_programs(1) - 1)
    def _():
        o_ref[...] = x_ref[...] + gt_ref[0] * acc_ref[...]


def _mlp(h2, w_up, w_down, x1, mod6, T):
    BT, D = x1.shape
    F = w_up.shape[1]
    tm = min(512, T)
    tf = 1024
    tpb = T // tm
    vm = 2 * (tm * D * 2 + 2 * D * tf * 2 + 2 * tm * D * 4) + tm * D * 4 + 2 * tm * tf * 4
    return pl.pallas_call(
        _mlp_kernel,
        grid=(BT // tm, F // tf),
        in_specs=[
            pl.BlockSpec((tm, D), lambda i, f: (i, 0)),
            pl.BlockSpec((D, tf), lambda i, f: (0, f)),
            pl.BlockSpec((tf, D), lambda i, f: (f, 0)),
            pl.BlockSpec((tm, D), lambda i, f: (i, 0)),
            pl.BlockSpec((1, 1, D), lambda i, f: ((i // tpb) * 6 + 5, 0, 0)),
        ],
        out_specs=pl.BlockSpec((tm, D), lambda i, f: (i, 0)),
        out_shape=jax.ShapeDtypeStruct((BT, D), F32),
        scratch_shapes=[pltpu.VMEM((tm, D), F32)],
        compiler_params=_cparams(("parallel", "arbitrary"), vm),
        name="mlp",
    )(h2, w_up, w_down, x1, mod6)


def _dotb(a, b):
    return jnp.dot(a.astype(BF16), b.astype(BF16), preferred_element_type=F32)


def _dotb_nt(a, b):
    return lax.dot_general(a.astype(BF16), b.astype(BF16), (((1,), (1,)), ((), ())),
                           preferred_element_type=F32)


def _doth(a, b):
    return jnp.dot(a, b, precision=HIGHEST, preferred_element_type=F32)


def _iota(shape, axis):
    return lax.broadcasted_iota(jnp.int32, shape, axis)


def _same_head_mask(n):
    return (_iota((n, n), 0) >> 6) == (_iota((n, n), 1) >> 6)


RW_TC = 256
RW_C = 64
RW_LANES = 4 * HEAD_DIM


def _block_diag(x, bdmask):
    return jnp.where(bdmask, jnp.concatenate([x, x, x, x], axis=0), 0.0)


def _rwkv_kernel(r_ref, k_ref, v_ref, s_ref, mur_ref, muk_ref, muv_ref, mus_ref,
                 w0_ref, a0_ref, kk_ref, ka_ref, rk_ref, lnw_ref, lnb_ref,
                 w2_ref, a2_ref, g2_ref, o_ref,
                 pr_scr, pk_scr, pv_scr, ps_scr, state_scr):
    TC = r_ref.shape[0]
    C = RW_C
    W = RW_LANES

    @pl.when(pl.program_id(2) == 0)
    def _():
        pr_scr[...] = jnp.zeros_like(pr_scr)
        pk_scr[...] = jnp.zeros_like(pk_scr)
        pv_scr[...] = jnp.zeros_like(pv_scr)
        ps_scr[...] = jnp.zeros_like(ps_scr)
        state_scr[...] = jnp.zeros_like(state_scr)

    def shift_mix(p_ref, prev_scr, mu_ref):
        p = p_ref[...]
        row = _iota(p.shape, 0)
        shifted = jnp.where(row == 0, prev_scr[0:1, :], pltpu.roll(p, 1, 0))
        prev_scr[0:1, :] = p[TC - 1:TC, :]
        return p + (shifted - p) * mu_ref[...]

    r = shift_mix(r_ref, pr_scr, mur_ref)
    k = shift_mix(k_ref, pk_scr, muk_ref)
    v = shift_mix(v_ref, pv_scr, muv_ref)
    sm = shift_mix(s_ref, ps_scr, mus_ref)
    xwa = sm[:, 0:SM_XG]
    xg = sm[:, SM_XG:SM_XG + 256]

    bdmask = _same_head_mask(W)
    bones = jnp.where(bdmask, 1.0, 0.0)
    tri = jnp.where(bdmask & (_iota((W, W), 1) <= _iota((W, W), 0)), 1.0, 0.0)

    wlin = w0_ref[...] + _doth(jnp.tanh(xwa), w2_ref[...])
    a = jax.nn.sigmoid(a0_ref[...] + _doth(xwa, a2_ref[...]))
    g = _doth(jax.nn.sigmoid(xg), g2_ref[...])
    z = -wlin
    softplus = jnp.maximum(z, 0.0) + jnp.log(1.0 + jnp.exp(-jnp.abs(z)))
    w = -softplus - 0.5
    ld = -jnp.exp(w)
    cum = _doth(tri, ld)

    kk = k * kk_ref[...]
    kk = kk / jnp.maximum(jnp.sqrt(_doth(kk * kk, bones)), 1e-12)
    k2 = k * (1.0 + (a - 1.0) * ka_ref[...])
    bvec = kk * a
    avec = -kk

    lane = _iota((C, W), 1)
    row = _iota((C, W), 0)
    lanehead = lane >> 6
    strict = (lane & 63) < row
    incl = (lane & 63) <= row
    eye_cat = jnp.where((lane & 63) == row, 1.0, 0.0)

    S = state_scr[...]
    ys = []
    for q in range(TC // C):
        sl = slice(q * C, (q + 1) * C)
        cum_q = cum[sl]
        cum_last = cum_q[C - 1:C, :]
        e_inv = jnp.exp(-cum_q)
        e_end = jnp.exp(cum_last - cum_q)
        At = avec[sl] * jnp.exp(cum_q - ld[sl])
        Rt = r[sl] * jnp.exp(cum_q)
        Bt = bvec[sl] * e_inv
        Kt = k2[sl] * e_inv
        Vq = v[sl]
        bk = jnp.concatenate([jnp.where(lanehead == h, X, 0.0) for X in (Bt, Kt) for h in range(4)], axis=0)
        AA = _dotb_nt(jnp.concatenate([At, Rt], axis=0), bk)
        A_ab = jnp.where(strict, AA[0:C, 0:W], 0.0)
        A_ak = jnp.where(strict, AA[0:C, W:2 * W], 0.0)
        A_rb = jnp.where(incl, AA[C:2 * C, 0:W], 0.0)
        A_rk = jnp.where(incl, AA[C:2 * C, W:2 * W], 0.0)
        M = A_ab
        Tm = eye_cat + A_ab
        for _ in range(5):
            M = _dotb(M, _block_diag(M, bdmask))
            Tm = Tm + _dotb(M, _block_diag(Tm, bdmask))
        Vbd = _block_diag(Vq, bdmask)
        rhs = _dotb_nt(At, S) + _dotb(A_ak, Vbd)
        U = _dotb(Tm, _block_diag(rhs, bdmask))
        ys.append(_dotb_nt(Rt, S) + _dotb(A_rb, _block_diag(U, bdmask)) + _dotb(A_rk, Vbd))
        uv = jnp.concatenate([U, Vq], axis=0)
        bkg = jnp.concatenate([bvec[sl] * e_end, k2[sl] * e_end], axis=0)
        S = S * jnp.exp(cum_last) + jnp.where(bdmask, _dotb(uv.T, bkg), 0.0)
    state_scr[...] = S
    y = jnp.concatenate(ys, axis=0)

    inv_n = 1.0 / HEAD_DIM
    mean = _doth(y, bones) * inv_n
    d = y - mean
    var = _doth(d * d, bones) * inv_n
    yn = d * lax.rsqrt(var + RWKV_GN_EPS) * lnw_ref[...] + lnb_ref[...]
    bonus = _doth(r * k2 * rk_ref[...], bones) * v
    o_ref[...] = ((yn + bonus) * g).astype(BF16)


def _rwkv_mix(proj, pr, B, T):
    BT = proj.shape[0]
    TC = min(RW_TC, T)
    W = RW_LANES
    nct = T // TC
    HG = RWKV_WIDTH // W
    sm_blk = COL_SMALL // SMALL_W
    row = lambda b, h, c: b * nct + c
    vec = pl.BlockSpec((1, W), lambda b, h, c: (0, h))
    in_specs = [
        pl.BlockSpec((TC, W), lambda b, h, c: (row(b, h, c), h)),
        pl.BlockSpec((TC, W), lambda b, h, c: (row(b, h, c), HG + h)),
        pl.BlockSpec((TC, W), lambda b, h, c: (row(b, h, c), 2 * HG + h)),
        pl.BlockSpec((TC, SMALL_W), lambda b, h, c: (row(b, h, c), sm_blk)),
        vec, vec, vec,
        pl.BlockSpec((1, SMALL_W), lambda b, h, c: (0, 0)),
        vec, vec, vec, vec, vec, vec, vec,
        pl.BlockSpec((SM_XG, W), lambda b, h, c: (0, h)),
        pl.BlockSpec((SM_XG, W), lambda b, h, c: (0, h)),
        pl.BlockSpec((256, W), lambda b, h, c: (0, h)),
    ]
    vm = 2 * (3 * TC * W * 4 + TC * SMALL_W * 4 + TC * W * 2) + 40 * TC * W * 4
    return pl.pallas_call(
        _rwkv_kernel,
        grid=(B, HG, nct),
        in_specs=in_specs,
        out_specs=pl.BlockSpec((TC, W), lambda b, h, c: (row(b, h, c), h)),
        out_shape=jax.ShapeDtypeStruct((BT, RWKV_WIDTH), BF16),
        scratch_shapes=[pltpu.VMEM((8, W), F32), pltpu.VMEM((8, W), F32), pltpu.VMEM((8, W), F32),
                        pltpu.VMEM((8, SMALL_W), F32), pltpu.VMEM((W, W), F32)],
        compiler_params=_cparams(("parallel", "parallel", "arbitrary"), vm),
        name="rwkv_mix",
    )(proj, proj, proj, proj, pr["mu_r"], pr["mu_k"], pr["mu_v"], pr["mu_s"],
      pr["w0"], pr["a0"], pr["k_k"], pr["k_a"], pr["r_k"], pr["ln_w"], pr["ln_b"],
      pr["w2p"], pr["a2p"], pr["g2p"])


NSA_TT = 256


def _nsa_prep_kernel(q_ref, ks_ref, kw_ref, s_ref, qg_ref, kgs_ref, kgw_ref, e_ref, gsel_ref,
                     qp_ref, kvs_ref, kvw_ref, gt_ref):
    q = q_ref[...]
    bones = jnp.where(_same_head_mask(q.shape[1]), 1.0, 0.0)
    ms = _doth(q * q, bones) * (1.0 / HEAD_DIM)
    qn = (q * lax.rsqrt(ms + NORM_EPS) * qg_ref[...]) * (HEAD_DIM ** -0.5)
    qs = jnp.dot(qn.astype(BF16), e_ref[...], preferred_element_type=F32).astype(BF16)
    for h in range(NSA_GROUP):
        qp_ref[0, h] = qs[:, h * LANES:(h + 1) * LANES]

    bones2 = jnp.where(_same_head_mask(LANES), 1.0, 0.0)
    is_k = _iota((1, LANES), 1) < HEAD_DIM

    def norm_k(x, gain):
        ms = _doth(x * x, bones2) * (1.0 / HEAD_DIM)
        return (x * jnp.where(is_k, lax.rsqrt(ms + NORM_EPS) * gain, 1.0)).astype(BF16)

    kvs_ref[0, 0] = norm_k(ks_ref[...], kgs_ref[...])
    kvw_ref[0, 0] = norm_k(kw_ref[...], kgw_ref[...])
    gt_ref[0, 0] = jax.nn.sigmoid(_doth(s_ref[...], gsel_ref[0]))


def _nsa_prep(proj, pn, B, T):
    tt = min(NSA_TT, T)
    ntt = T // tt
    G = NSA_KV_HEADS
    QW = NSA_GROUP * HEAD_DIM
    kv_blk = COL_KV // LANES
    row = lambda b, t, g: b * ntt + t
    vm = 2 * (tt * QW * 4 + 2 * tt * LANES * 4 + tt * SMALL_W * 4 + QW * 4 * LANES * 2 + SMALL_W * LANES * 4
              + 4 * tt * LANES * 2 + 2 * tt * LANES * 2 + tt * LANES * 4) + 12 * tt * QW * 4
    return pl.pallas_call(
        _nsa_prep_kernel,
        grid=(B, ntt, G),
        in_specs=[
            pl.BlockSpec((tt, QW), lambda b, t, g: (row(b, t, g), COL_Q // QW + g)),
            pl.BlockSpec((tt, LANES), lambda b, t, g: (row(b, t, g), kv_blk + 3 * g + 1)),
            pl.BlockSpec((tt, LANES), lambda b, t, g: (row(b, t, g), kv_blk + 3 * g + 2)),
            pl.BlockSpec((tt, SMALL_W), lambda b, t, g: (row(b, t, g), COL_SMALL // SMALL_W)),
            pl.BlockSpec((1, QW), lambda b, t, g: (0, 0)),
            pl.BlockSpec((1, LANES), lambda b, t, g: (0, 0)),
            pl.BlockSpec((1, LANES), lambda b, t, g: (0, 0)),
            pl.BlockSpec((QW, NSA_GROUP * LANES), lambda b, t, g: (0, 0)),
            pl.BlockSpec((1, SMALL_W, LANES), lambda b, t, g: (g, 0, 0)),
        ],
        out_specs=[
            pl.BlockSpec((1, NSA_GROUP, tt, LANES), lambda b, t, g: (b, g, t, 0)),
            pl.BlockSpec((1, 1, tt, LANES), lambda b, t, g: (b, g, t, 0)),
            pl.BlockSpec((1, 1, tt, LANES), lambda b, t, g: (b, g, t, 0)),
            pl.BlockSpec((1, 1, tt, LANES), lambda b, t, g: (b, g, t, 0)),
        ],
        out_shape=[
            jax.ShapeDtypeStruct((B, NSA_HEADS, T, LANES), BF16),
            jax.ShapeDtypeStruct((B, G, T, LANES), BF16),
            jax.ShapeDtypeStruct((B, G, T, LANES), BF16),
            jax.ShapeDtypeStruct((B, G, T, LANES), F32),
        ],
        compiler_params=_cparams(("parallel", "parallel", "parallel"), vm),
        name="nsa_prep",
    )(proj, proj, proj, proj, pn["q_g"], pn["kg_sel"], pn["kg_win"], pn["q_spread"], pn["gate_sel"])


def _gelu_tanh(x):
    return 0.5 * x * (1.0 + jnp.tanh(math.sqrt(2.0 / math.pi) * (x + 0.044715 * (x * x * x))))


def _nsa_compress_kernel(x_ref, pea_ref, peb_ref, wa_ref, wb_ref, w2_ref, kg_ref, o_ref):
    nsub = x_ref.shape[0] // CMP_STRIDE
    xs = jnp.concatenate([x_ref[pl.ds(s, nsub, stride=CMP_STRIDE), :] for s in range(CMP_STRIDE)], axis=1)
    p0 = jnp.dot((xs + pea_ref[...]).astype(BF16), wa_ref[...], preferred_element_type=F32)
    p1 = jnp.dot((xs + peb_ref[...]).astype(BF16), wb_ref[...], preferred_element_type=F32)
    hid = _gelu_tanh(p0 + pltpu.roll(p1, nsub - 1, 0))
    out = jnp.dot(hid.astype(BF16), w2_ref[...], preferred_element_type=F32)
    bones2 = jnp.where(_same_head_mask(LANES), 1.0, 0.0)
    ms = _doth(out * out, bones2) * (1.0 / HEAD_DIM)
    is_k = _iota((1, LANES), 1) < HEAD_DIM
    o_ref[0, 0] = (out * jnp.where(is_k, lax.rsqrt(ms + NORM_EPS) * kg_ref[...], 1.0)).astype(BF16)


def _nsa_compress(proj, pn, B, T):
    G = NSA_KV_HEADS
    nsub = T // CMP_STRIDE
    kv_blk = COL_KV // LANES
    KW = CMP_STRIDE * LANES
    full = lambda shape: pl.BlockSpec(shape, lambda b, g: tuple(0 for _ in shape))
    vm = 2 * (T * LANES * 4 + 2 * KW * LANES * 2 + nsub * LANES * 2) + 6 * nsub * KW * 4
    return pl.pallas_call(
        _nsa_compress_kernel,
        grid=(B, G),
        in_specs=[
            pl.BlockSpec((T, LANES), lambda b, g: (b, kv_blk + 3 * g)),
            full((1, KW)), full((1, KW)), full((KW, LANES)), full((KW, LANES)), full((LANES, LANES)),
            full((1, LANES)),
        ],
        out_specs=pl.BlockSpec((1, 1, nsub, LANES), lambda b, g: (b, g, 0, 0)),
        out_shape=jax.ShapeDtypeStruct((B, G, nsub, LANES), BF16),
        compiler_params=_cparams(("parallel", "parallel"), vm),
        name="nsa_compress",
    )(proj, pn["pe_a"], pn["pe_b"], pn["cw_a"], pn["cw_b"], pn["cw2"], pn["kg_cmp"])


def _nsa_attn_kernel(q_ref, kvc_ref, kvs_ref, kvw_ref, gt_ref, dtab_ref, ptab_ref, ftab_ref, ctab_ref,
                     selm_ref, expand_ref, gather_ref, o_ref, m_scr, l_scr, acc_scr, selk_scr, *, n_sel):
    i = pl.program_id(2)
    QB = QUERY_BLOCK
    R = NSA_GROUP * QB
    t0 = i * QB
    q = q_ref[0].reshape(R, LANES)
    rowq = _iota((R, LANES), 0) & (QB - 1)
    col = _iota((R, LANES), 1)
    causal = col <= rowq

    kvc = kvc_ref[0, 0]
    ncp = kvc.shape[0]
    ccol = _iota((R, ncp), 1)
    crow = _iota((R, ncp), 0) & (QB - 1)
    cmask = (t0 + crow) >= (ccol * CMP_STRIDE + (CMP_BLOCK - 1))
    s = _dotb_nt(q, kvc) + ctab_ref[0].reshape(R, ncp)
    s = jnp.where(cmask, s, NEG_INF)
    p = jnp.where(cmask, jnp.exp(s - jnp.max(s, axis=-1, keepdims=True)), 0.0)
    lsum = jnp.sum(p, axis=-1, keepdims=True)
    p = p * jnp.where(lsum > 0.0, 1.0 / lsum, 0.0)
    o_c = jnp.dot(p.astype(BF16), kvc, preferred_element_type=F32)

    psum = p[0:QB] + p[QB:2 * QB] + p[2 * QB:3 * QB] + p[3 * QB:4 * QB]
    imp = _doth(psum, selm_ref[...])
    blk = _iota((QB, LANES), 1)
    cur = (t0 + _iota((QB, LANES), 0)) >> 6
    ns = kvs_ref.shape[2] // SEL_BLOCK
    forced = (blk == 0) | (blk == cur) | (blk == cur - 1)
    score = jnp.where(forced, FORCE_SCORE, jnp.where(blk <= cur, imp, -1.0))
    score = jnp.where(blk < ns, score, -2.0)
    chosen = jnp.zeros((QB, LANES), F32)
    for _ in range(n_sel):
        mx = jnp.max(score, axis=-1, keepdims=True)
        first = jnp.min(jnp.where(score == mx, blk, LANES), axis=-1, keepdims=True)
        hit = blk == first
        chosen = jnp.where(hit, 1.0, chosen)
        score = jnp.where(hit, -3.0e38, score)
    chosen = chosen.astype(BF16)
    nkt = selk_scr.shape[0]
    for j in range(nkt):
        selk_scr[j] = jnp.dot(chosen, expand_ref[:, j * LANES:(j + 1) * LANES], preferred_element_type=F32)

    def sel_valid(j):
        m = selk_scr[j] > 0.5
        return jnp.concatenate([m, m, m, m], axis=0)

    def reset():
        m_scr[...] = jnp.full_like(m_scr, NEG_INF)
        l_scr[...] = jnp.zeros_like(l_scr)
        acc_scr[...] = jnp.zeros_like(acc_scr)

    def tile(kv, add, valid):
        s = _dotb_nt(q, kv) + add
        if valid is not None:
            s = jnp.where(valid, s, NEG_INF)
        m_old = m_scr[...]
        m_new = jnp.maximum(m_old, jnp.max(s, axis=-1, keepdims=True))
        p = jnp.exp(s - m_new)
        if valid is not None:
            p = jnp.where(valid, p, 0.0)
        alpha = jnp.exp(m_old - m_new)
        l_scr[...] = alpha * l_scr[...] + jnp.sum(p, axis=-1, keepdims=True)
        acc_scr[...] = alpha * acc_scr[...] + jnp.dot(p.astype(BF16), kv, preferred_element_type=F32)
        m_scr[...] = m_new

    def finish():
        return acc_scr[...] * (1.0 / l_scr[...])

    def kv_tile(ref, j):
        return ref[0, 0, pl.ds(pl.multiple_of(j * QB, QB), QB), :]

    dtab = dtab_ref[...].reshape(R, LANES)
    ptab = ptab_ref[...].reshape(R, LANES)
    ftab = ftab_ref[...].reshape(R, LANES)

    reset()
    tile(kv_tile(kvs_ref, i), dtab, causal & sel_valid(i))

    @pl.when(i >= 1)
    def _():
        tile(kv_tile(kvs_ref, i - 1), ptab, sel_valid(i - 1))

    def far_body(j, carry):
        tile(kv_tile(kvs_ref, j), ftab, sel_valid(j))
        return carry

    lax.fori_loop(0, jnp.maximum(i - 1, 0), far_body, 0)
    o_s = finish()

    reset()
    tile(kv_tile(kvw_ref, i), dtab, causal)
    for d, add, valid in ((1, ptab, None), (2, ftab, None), (3, ftab, None), (4, ftab, col > rowq)):
        @pl.when(i >= d)
        def _(d=d, add=add, valid=valid):
            tile(kv_tile(kvw_ref, i - d), add, valid)
    o_w = finish()

    gt = gt_ref[0, 0]
    outs = []
    for h in range(NSA_GROUP):
        rs = slice(h * QB, (h + 1) * QB)
        outs.append(gt[:, 3 * h:3 * h + 1] * o_c[rs] + gt[:, 3 * h + 1:3 * h + 2] * o_s[rs]
                    + gt[:, 3 * h + 2:3 * h + 3] * o_w[rs])
    cat = jnp.concatenate(outs, axis=1).astype(BF16)
    o_ref[...] = jnp.dot(cat, gather_ref[...], preferred_element_type=F32).astype(BF16)


def _nsa_attn(qp, kvc, kvs, kvw, gates, pn, B, T):
    G = NSA_KV_HEADS
    QB = QUERY_BLOCK
    NQ = T // QB
    ncp = kvc.shape[2]
    R = NSA_GROUP * QB
    n_sel = min(N_SEL, T // SEL_BLOCK)
    tab = pl.BlockSpec((NSA_GROUP, QB, LANES), lambda b, g, i: (g, 0, 0))
    vm = (2 * (R * LANES * 2 + ncp * LANES * 2 + 2 * T * LANES * 2 + QB * LANES * 4 + 3 * R * LANES * 4
               + R * ncp * 4 + ncp * LANES * 4 + LANES * T * 2 + NSA_GROUP * LANES * NSA_GROUP * HEAD_DIM * 2
               + QB * NSA_GROUP * HEAD_DIM * 2)
          + 3 * R * LANES * 4 + NQ * QB * LANES * 4 + 16 * R * LANES * 4)
    return pl.pallas_call(
        functools.partial(_nsa_attn_kernel, n_sel=n_sel),
        grid=(B, G, NQ),
        in_specs=[
            pl.BlockSpec((1, NSA_GROUP, QB, LANES), lambda b, g, i: (b, g, i, 0)),
            pl.BlockSpec((1, 1, ncp, LANES), lambda b, g, i: (b, g, 0, 0)),
            pl.BlockSpec((1, 1, T, LANES), lambda b, g, i: (b, g, 0, 0)),
            pl.BlockSpec((1, 1, T, LANES), lambda b, g, i: (b, g, 0, 0)),
            pl.BlockSpec((1, 1, QB, LANES), lambda b, g, i: (b, g, i, 0)),
            tab, tab, tab,
            pl.BlockSpec((1, NSA_GROUP, QB, ncp), lambda b, g, i: (i, g, 0, 0)),
            pl.BlockSpec((ncp, LANES), lambda b, g, i: (0, 0)),
            pl.BlockSpec((LANES, T), lambda b, g, i: (0, 0)),
            pl.BlockSpec((NSA_GROUP * LANES, NSA_GROUP * HEAD_DIM), lambda b, g, i: (0, 0)),
        ],
        out_specs=pl.BlockSpec((QB, NSA_GROUP * HEAD_DIM), lambda b, g, i: (b * NQ + i, g)),
        out_shape=jax.ShapeDtypeStruct((B * T, NSA_WIDTH), BF16),
        scratch_shapes=[pltpu.VMEM((R, 1), F32), pltpu.VMEM((R, 1), F32), pltpu.VMEM((R, LANES), F32),
                        pltpu.VMEM((NQ, QB, LANES), F32)],
        compiler_params=_cparams(("parallel", "parallel", "arbitrary"), vm),
        name="nsa_attn",
    )(qp, kvc, kvs, kvw, gates, pn["dtab"], pn["ptab"], pn["ftab"], pn["ctab"],
      pn["sel_m"], pn["expand"], pn["gather"])


def _rel_bucket_table():
    n = np.arange(REL_MAX_DIST + 1)
    max_exact = REL_BUCKETS // 2
    nf = np.maximum(n, max_exact).astype(np.float32)
    large = max_exact + (np.log(nf / np.float32(max_exact)) / np.float32(math.log(REL_MAX_DIST / max_exact))
                         * np.float32(REL_BUCKETS - max_exact)).astype(np.int32)
    large = np.minimum(large, REL_BUCKETS - 1)
    return np.where(n < max_exact, n, large).astype(np.int32)


def _cmp_to_sel_matrix(T, ncp):
    nc = T // CMP_STRIDE - CMP_BLOCK // CMP_STRIDE + 1
    ns = T // SEL_BLOCK
    cs = np.arange(nc) * CMP_STRIDE
    ss = np.arange(ns) * SEL_BLOCK
    lo = np.maximum(cs[:, None], ss[None, :])
    hi = np.minimum(cs[:, None] + CMP_BLOCK, ss[None, :] + SEL_BLOCK)
    out = np.zeros((ncp, LANES), np.float32)
    out[:nc, :ns] = np.maximum(hi - lo, 0) / CMP_BLOCK
    return out


def _prep_in_proj_weight(w_in):
    D = w_in.shape[0]
    nsa0 = RWKV_COLS
    kv0 = nsa0 + NSA_WIDTH
    gates0 = kv0 + 6 * NSA_KV_WIDTH
    merge0 = RWKV_COLS + NSA_COLS
    kv = w_in[:, kv0:gates0].reshape(D, 3, 2, NSA_KV_HEADS, HEAD_DIM)
    kv = jnp.transpose(kv, (0, 3, 1, 2, 4)).reshape(D, 6 * NSA_KV_WIDTH)
    pad = jnp.zeros((D, SMALL_W - (SM_GATES + 3 * NSA_HEADS)), w_in.dtype)
    return jnp.concatenate([
        w_in[:, 0:3 * RWKV_WIDTH],
        w_in[:, nsa0:kv0],
        w_in[:, merge0:merge0 + 2 * D_MODEL],
        kv,
        w_in[:, 3 * RWKV_WIDTH:RWKV_COLS],
        w_in[:, gates0:merge0],
        pad,
    ], axis=1).astype(BF16)


def _prep_rwkv_params(mu, w0, w2, a0, a2, g2, k_k, k_a, r_k, ln_w, ln_b):
    C = RWKV_WIDTH
    row = lambda z: z.reshape(1, -1).astype(F32)
    mu_s = jnp.concatenate([mu[3 * C:], jnp.zeros((SMALL_W - (RWKV_COLS - 3 * C),), F32)]).reshape(1, SMALL_W)
    zl = jnp.zeros((DECAY_LORA, C), F32)
    return dict(
        mu_r=row(mu[0:C]), mu_k=row(mu[C:2 * C]), mu_v=row(mu[2 * C:3 * C]), mu_s=mu_s,
        w0=row(w0), a0=row(a0), k_k=row(k_k), k_a=row(k_a), r_k=row(r_k), ln_w=row(ln_w), ln_b=row(ln_b),
        w2p=jnp.concatenate([w2, zl], axis=0),
        a2p=jnp.concatenate([zl, a2], axis=0),
        g2p=jnp.concatenate([g2, jnp.zeros((256 - GATE_LORA, C), F32)], axis=0),
    )


def _prep_nsa_params(pe_k, w1_k, w2_k, pe_v, w1_v, w2_v, q_g, k_g, rel_bias, T):
    hd = HEAD_DIM
    ones = jnp.ones((hd,), F32)
    zeros = jnp.zeros((hd, hd), F32)
    ncp = T // CMP_STRIDE
    NQ = T // QUERY_BLOCK

    def blockdiag(a, b):
        return jnp.concatenate([jnp.concatenate([a, zeros], axis=1), jnp.concatenate([zeros, b], axis=1)], axis=0)

    w1k = w1_k.reshape(CMP_BLOCK, hd, hd)
    w1v = w1_v.reshape(CMP_BLOCK, hd, hd)
    w1 = jnp.stack([blockdiag(w1k[s], w1v[s]) for s in range(CMP_BLOCK)])
    pe = jnp.concatenate([pe_k, pe_v], axis=1)
    half = CMP_STRIDE

    spread = np.zeros((NSA_GROUP * hd, NSA_GROUP * LANES), np.float32)
    gather = np.zeros((NSA_GROUP * LANES, NSA_GROUP * hd), np.float32)
    for h in range(NSA_GROUP):
        for d in range(hd):
            spread[h * hd + d, h * LANES + d] = 1.0
            gather[h * LANES + hd + d, h * hd + d] = 1.0
    gate_sel = np.zeros((NSA_KV_HEADS, SMALL_W, LANES), np.float32)
    for g in range(NSA_KV_HEADS):
        for j in range(3 * NSA_GROUP):
            gate_sel[g, SM_GATES + 3 * NSA_GROUP * g + j, j] = 1.0
    expand = (np.arange(LANES)[:, None] == (np.arange(T)[None, :] // SEL_BLOCK)).astype(np.float32)

    bvec = jnp.transpose(rel_bias[_rel_bucket_table()], (1, 0))
    qi = np.arange(QUERY_BLOCK)[:, None]
    ci = np.arange(LANES)[None, :]
    clip = lambda r: np.clip(r, 0, REL_MAX_DIST)
    ni = np.arange(ncp)[None, None, :]
    rel_c = (np.arange(NQ)[:, None, None] * QUERY_BLOCK + qi[None]) - (ni * CMP_STRIDE + CMP_BLOCK - 1)
    ctab = jnp.transpose(bvec[:, clip(rel_c)], (1, 0, 2, 3))
    return dict(
        q_g=jnp.tile(q_g, NSA_GROUP).reshape(1, -1),
        kg_cmp=jnp.concatenate([k_g[0], ones]).reshape(1, LANES),
        kg_sel=jnp.concatenate([k_g[1], ones]).reshape(1, LANES),
        kg_win=jnp.concatenate([k_g[2], ones]).reshape(1, LANES),
        pe_a=pe[:half].reshape(1, half * LANES), pe_b=pe[half:].reshape(1, half * LANES),
        cw_a=w1[:half].reshape(half * LANES, LANES).astype(BF16),
        cw_b=w1[half:].reshape(half * LANES, LANES).astype(BF16),
        cw2=blockdiag(w2_k, w2_v).astype(BF16),
        q_spread=jnp.asarray(spread, BF16), gather=jnp.asarray(gather, BF16),
        gate_sel=jnp.asarray(gate_sel), expand=jnp.asarray(expand, BF16),
        sel_m=jnp.asarray(_cmp_to_sel_matrix(T, ncp)),
        dtab=bvec[:, clip(qi - ci)], ptab=bvec[:, clip(qi - ci + QUERY_BLOCK)],
        ftab=jnp.broadcast_to(bvec[:, REL_MAX_DIST][:, None, None], (NSA_HEADS, QUERY_BLOCK, LANES)),
        ctab=ctab,
    )


def kernel(x, c, w_ada, b_ada, norm1_g, norm2_g, w_in, rwkv_mu, rwkv_w0, rwkv_w2, rwkv_a0, rwkv_a2, rwkv_g2, rwkv_k_k, rwkv_k_a, rwkv_r_k, rwkv_ln_w, rwkv_ln_b, cmp_pe_k, cmp_w1_k, cmp_w2_k, cmp_pe_v, cmp_w1_v, cmp_w2_v, q_norm_g, k_norm_g, rel_bias, w_o_rwkv, w_o_nsa, w_out, w_up, w_down):
    B, T, D = x.shape
    depth = w_in.shape[0]
    x2 = x.reshape(B * T, D)
    for l in range(depth):
        mod6 = _ada_mod(c, w_ada[l], b_ada[l]).reshape(B * 6, 1, D)
        proj = _in_proj(x2, norm1_g[l].reshape(1, D), mod6, _prep_in_proj_weight(w_in[l]), T)
        pr = _prep_rwkv_params(rwkv_mu[l], rwkv_w0[l], rwkv_w2[l], rwkv_a0[l], rwkv_a2[l], rwkv_g2[l],
                               rwkv_k_k[l], rwkv_k_a[l], rwkv_r_k[l], rwkv_ln_w[l], rwkv_ln_b[l])
        o_a = _rwkv_mix(proj, pr, B, T)
        pn = _prep_nsa_params(cmp_pe_k[l], cmp_w1_k[l], cmp_w2_k[l], cmp_pe_v[l], cmp_w1_v[l], cmp_w2_v[l],
                              q_norm_g[l], k_norm_g[l], rel_bias, T)
        qp, kvs, kvw, gates = _nsa_prep(proj, pn, B, T)
        kvc = _nsa_compress(proj, pn, B, T)
        o_b = _nsa_attn(qp, kvc, kvs, kvw, gates, pn, B, T)
        mixed = _merge(o_a, o_b, w_o_rwkv[l].astype(BF16), w_o_nsa[l].astype(BF16), proj)
        x1, h2 = _out_proj(mixed, w_out[l].astype(BF16), x2, mod6, norm2_g[l].reshape(1, D), T)
        x2 = _mlp(h2, w_up[l].astype(BF16), w_down[l].astype(BF16), x1, mod6, T)
    return x2.reshape(B, T, D)
```

```python
import functools
import math

import numpy as np
import jax
import jax.numpy as jnp
from jax import lax
from jax.experimental import pallas as pl
from jax.experimental.pallas import tpu as pltpu

F32 = jnp.float32
BF16 = jnp.bfloat16
HIGHEST = lax.Precision.HIGHEST

D_MODEL = 2048
HEAD_DIM = 64
RWKV_WIDTH = D_MODEL // 2
DECAY_LORA = 64
ICLR_LORA = 64
GATE_LORA = 160
RWKV_GN_EPS = 64e-5
NSA_WIDTH = D_MODEL // 2
NSA_HEADS = NSA_WIDTH // HEAD_DIM
NSA_KV_HEADS = 4
NSA_GROUP = NSA_HEADS // NSA_KV_HEADS
NSA_KV_WIDTH = NSA_KV_HEADS * HEAD_DIM
CMP_BLOCK = 32
CMP_STRIDE = 16
SEL_BLOCK = 64
N_SEL = 8
WINDOW = 512
QUERY_BLOCK = 128
REL_BUCKETS = 32
REL_MAX_DIST = 128
D_FF = 4 * D_MODEL
NORM_EPS = 1e-6
NEG_INF = -1e30
FORCE_SCORE = 1e4

RWKV_COLS = 3 * RWKV_WIDTH + DECAY_LORA + ICLR_LORA + GATE_LORA
NSA_COLS = NSA_WIDTH + 6 * NSA_KV_WIDTH + 3 * NSA_HEADS

V7X_VMEM_BYTES = 64 * 1024 * 1024
LANES = 128

COL_RKV = 0
COL_Q = 3 * RWKV_WIDTH
COL_MERGE = COL_Q + NSA_WIDTH
COL_KV = COL_MERGE + 2 * D_MODEL
COL_SMALL = COL_KV + 6 * NSA_KV_WIDTH
SMALL_W = 512
PROJ_COLS = COL_SMALL + SMALL_W
SM_XG = DECAY_LORA + ICLR_LORA
SM_GATES = SM_XG + GATE_LORA


def _vmem_limit(nbytes):
    return int(min(nbytes * 5 // 4 + (4 << 20), V7X_VMEM_BYTES - (8 << 20)))


def _cparams(sem, vmem_bytes):
    return pltpu.CompilerParams(dimension_semantics=sem, vmem_limit_bytes=_vmem_limit(vmem_bytes))


def _ada_kernel(c_ref, w_ref, b_ref, o_ref):
    c = c_ref[...]
    s = c * jax.nn.sigmoid(c)
    o_ref[...] = jnp.dot(s.astype(BF16), w_ref[...].astype(BF16), preferred_element_type=F32) + b_ref[...]


def _ada_mod(c, w_ada, b_ada):
    B, D = c.shape
    N = w_ada.shape[1]
    tn = 1024
    return pl.pallas_call(
        _ada_kernel,
        grid=(N // tn,),
        in_specs=[
            pl.BlockSpec((B, D), lambda j: (0, 0)),
            pl.BlockSpec((D, tn), lambda j: (0, j)),
            pl.BlockSpec((1, tn), lambda j: (0, j)),
        ],
        out_specs=pl.BlockSpec((B, tn), lambda j: (0, j)),
        out_shape=jax.ShapeDtypeStruct((B, N), F32),
        compiler_params=_cparams(("parallel",), 2 * D * tn * 4 + D * tn * 2),
        name="ada_mod",
    )(c, w_ada, b_ada.reshape(1, N))


def _modulated_norm(x, g, sc, sh):
    ms = jnp.mean(x * x, axis=-1, keepdims=True)
    return (x * lax.rsqrt(ms + NORM_EPS) * g) * (1.0 + sc) + sh


def _inproj_kernel(x_ref, g_ref, sh_ref, sc_ref, w_ref, o_ref, h_scr):
    @pl.when(pl.program_id(1) == 0)
    def _():
        h_scr[...] = _modulated_norm(x_ref[...], g_ref[...], sc_ref[0], sh_ref[0]).astype(BF16)

    o_ref[...] = jnp.dot(h_scr[...], w_ref[...], preferred_element_type=F32)


def _in_proj(x2, g1, mod6, w_in_p, T):
    BT, D = x2.shape
    NP = w_in_p.shape[1]
    tm = min(512, T)
    tn = 1024
    tpb = T // tm
    vm = 2 * tm * D * 4 + tm * D * 2 + 2 * D * tn * 2 + 2 * tm * tn * 4 + 2 * tm * D * 4
    return pl.pallas_call(
        _inproj_kernel,
        grid=(BT // tm, NP // tn),
        in_specs=[
            pl.BlockSpec((tm, D), lambda i, j: (i, 0)),
            pl.BlockSpec((1, D), lambda i, j: (0, 0)),
            pl.BlockSpec((1, 1, D), lambda i, j: ((i // tpb) * 6 + 0, 0, 0)),
            pl.BlockSpec((1, 1, D), lambda i, j: ((i // tpb) * 6 + 1, 0, 0)),
            pl.BlockSpec((D, tn), lambda i, j: (0, j)),
        ],
        out_specs=pl.BlockSpec((tm, tn), lambda i, j: (i, j)),
        out_shape=jax.ShapeDtypeStruct((BT, NP), F32),
        scratch_shapes=[pltpu.VMEM((tm, D), BF16)],
        compiler_params=_cparams(("parallel", "arbitrary"), vm),
        name="in_proj",
    )(x2, g1, mod6, mod6, w_in_p)


def _merge_kernel(oa_ref, ob_ref, wa_ref, wb_ref, ga_ref, gb_ref, o_ref):
    ya = jnp.dot(oa_ref[...], wa_ref[...], preferred_element_type=F32)
    yb = jnp.dot(ob_ref[...], wb_ref[...], preferred_element_type=F32)
    o_ref[...] = (jax.nn.sigmoid(ga_ref[...]) * ya + jax.nn.sigmoid(gb_ref[...]) * yb).astype(BF16)


def _merge(o_a, o_b, w_oa, w_ob, proj):
    BT, W = o_a.shape
    D = w_oa.shape[1]
    tm, tn = 512, 1024
    ga0 = COL_MERGE // tn
    gb0 = (COL_MERGE + D) // tn
    vm = 2 * (2 * tm * W * 2 + 2 * W * tn * 2 + 2 * tm * tn * 4 + tm * tn * 2) + 3 * tm * tn * 4
    return pl.pallas_call(
        _merge_kernel,
        grid=(BT // tm, D // tn),
        in_specs=[
            pl.BlockSpec((tm, W), lambda i, j: (i, 0)),
            pl.BlockSpec((tm, W), lambda i, j: (i, 0)),
            pl.BlockSpec((W, tn), lambda i, j: (0, j)),
            pl.BlockSpec((W, tn), lambda i, j: (0, j)),
            pl.BlockSpec((tm, tn), lambda i, j: (i, ga0 + j)),
            pl.BlockSpec((tm, tn), lambda i, j: (i, gb0 + j)),
        ],
        out_specs=pl.BlockSpec((tm, tn), lambda i, j: (i, j)),
        out_shape=jax.ShapeDtypeStruct((BT, D), BF16),
        compiler_params=_cparams(("parallel", "parallel"), vm),
        name="merge",
    )(o_a, o_b, w_oa, w_ob, proj, proj)


def _outproj_kernel(m_ref, w_ref, x_ref, gt_ref, g_ref, sh_ref, sc_ref, x1_ref, h2_ref):
    y = jnp.dot(m_ref[...], w_ref[...], preferred_element_type=F32)
    x1 = x_ref[...] + gt_ref[0] * y
    x1_ref[...] = x1
    h2_ref[...] = _modulated_norm(x1, g_ref[...], sc_ref[0], sh_ref[0]).astype(BF16)


def _out_proj(mixed, w_out, x2, mod6, g2, T):
    BT, D = x2.shape
    tm = min(512, T)
    tpb = T // tm
    vm = 2 * (tm * D * 2 + D * D * 2 + tm * D * 4 + tm * D * 4 + tm * D * 2) + 3 * tm * D * 4
    mod_spec = lambda k: pl.BlockSpec((1, 1, D), lambda i: ((i // tpb) * 6 + k, 0, 0))
    return pl.pallas_call(
        _outproj_kernel,
        grid=(BT // tm,),
        in_specs=[
            pl.BlockSpec((tm, D), lambda i: (i, 0)),
            pl.BlockSpec((D, D), lambda i: (0, 0)),
            pl.BlockSpec((tm, D), lambda i: (i, 0)),
            mod_spec(2),
            pl.BlockSpec((1, D), lambda i: (0, 0)),
            mod_spec(3),
            mod_spec(4),
        ],
        out_specs=[pl.BlockSpec((tm, D), lambda i: (i, 0)), pl.BlockSpec((tm, D), lambda i: (i, 0))],
        out_shape=[jax.ShapeDtypeStruct((BT, D), F32), jax.ShapeDtypeStruct((BT, D), BF16)],
        compiler_params=_cparams(("parallel",), vm),
        name="out_proj",
    )(mixed, w_out, x2, mod6, g2, mod6, mod6)


def _mlp_kernel(h_ref, wu_ref, wd_ref, x_ref, gt_ref, o_ref, acc_ref):
    f = pl.program_id(1)
    u = jnp.dot(h_ref[...], wu_ref[...], preferred_element_type=F32)
    u = jnp.square(jnp.maximum(u, 0.0)).astype(BF16)
    part = jnp.dot(u, wd_ref[...], preferred_element_type=F32)

    @pl.when(f == 0)
    def _():
        acc_ref[...] = part

    @pl.when(f > 0)
    def _():
        acc_ref[...] += part

    @pl.when(f == pl.num_programs(1) - 1)
    def _():
        o_ref[...] = x_ref[...] + gt_ref[0] * acc_ref[...]


def _mlp(h2, w_up, w_down, x1, mod6, T):
    BT, D = x1.shape
    F = w_up.shape[1]
    tm = min(512, T)
    tf = 1024
    tpb = T // tm
    vm = 2 * (tm * D * 2 + 2 * D * tf * 2 + 2 * tm * D * 4) + tm * D * 4 + 2 * tm * tf * 4
    return pl.pallas_call(
        _mlp_kernel,
        grid=(BT // tm, F // tf),
        in_specs=[
            pl.BlockSpec((tm, D), lambda i, f: (i, 0)),
            pl.BlockSpec((D, tf), lambda i, f: (0, f)),
            pl.BlockSpec((tf, D), lambda i, f: (f, 0)),
            pl.BlockSpec((tm, D), lambda i, f: (i, 0)),
            pl.BlockSpec((1, 1, D), lambda i, f: ((i // tpb) * 6 + 5, 0, 0)),
        ],
        out_specs=pl.BlockSpec((tm, D), lambda i, f: (i, 0)),
        out_shape=jax.ShapeDtypeStruct((BT, D), F32),
        scratch_shapes=[pltpu.VMEM((tm, D), F32)],
        compiler_params=_cparams(("parallel", "arbitrary"), vm),
        name="mlp",
    )(h2, w_up, w_down, x1, mod6)


def _dotb(a, b):
    return jnp.dot(a.astype(BF16), b.astype(BF16), preferred_element_type=F32)


def _dotb_nt(a, b):
    return lax.dot_general(a.astype(BF16), b.astype(BF16), (((1,), (1,)), ((), ())),
                           preferred_element_type=F32)


def _split_bf16(x, terms):
    parts, rem = [], x
    for t in range(terms):
        p = rem.astype(BF16)
        parts.append(p)
        if t + 1 < terms:
            rem = rem - p.astype(F32)
    return parts


def _dot_sel(x, sel, terms):
    sel = sel.astype(BF16)
    return sum(jnp.dot(p, sel, preferred_element_type=F32) for p in _split_bf16(x, terms))


def _sel_dot(sel, x, terms):
    sel = sel.astype(BF16)
    return sum(jnp.dot(sel, p, preferred_element_type=F32) for p in _split_bf16(x, terms))


def _dot3(a, b_hi, b_lo):
    a_hi, a_lo = _split_bf16(a, 2)
    return (jnp.dot(a_hi, b_hi, preferred_element_type=F32) + jnp.dot(a_lo, b_hi, preferred_element_type=F32)
            + jnp.dot(a_hi, b_lo, preferred_element_type=F32))


def _iota(shape, axis):
    return lax.broadcasted_iota(jnp.int32, shape, axis)


def _same_head_mask(n):
    return (_iota((n, n), 0) >> 6) == (_iota((n, n), 1) >> 6)


RW_TC = 256
RW_C = 64
RW_LANES = 4 * HEAD_DIM


def _block_diag(x, bdmask):
    return jnp.where(bdmask, jnp.concatenate([x, x, x, x], axis=0), 0.0)


def _rwkv_kernel(r_ref, k_ref, v_ref, s_ref, mur_ref, muk_ref, muv_ref, mus_ref,
                 w0_ref, a0_ref, kk_ref, ka_ref, rk_ref, lnw_ref, lnb_ref,
                 w2h_ref, w2l_ref, a2h_ref, a2l_ref, g2h_ref, g2l_ref, o_ref,
                 pr_scr, pk_scr, pv_scr, ps_scr, state_scr):
    TC = r_ref.shape[0]
    C = RW_C
    W = RW_LANES

    @pl.when(pl.program_id(1) == 0)
    def _():
        pr_scr[...] = jnp.zeros_like(pr_scr)
        pk_scr[...] = jnp.zeros_like(pk_scr)
        pv_scr[...] = jnp.zeros_like(pv_scr)
        ps_scr[...] = jnp.zeros_like(ps_scr)
        state_scr[...] = jnp.zeros_like(state_scr)

    def shift_mix(p_ref, prev_scr, mu_ref):
        p = p_ref[...]
        row = _iota(p.shape, 0)
        shifted = jnp.where(row == 0, prev_scr[0:1, :], pltpu.roll(p, 1, 0))
        prev_scr[0:1, :] = p[TC - 1:TC, :]
        return p + (shifted - p) * mu_ref[...]

    CW = r_ref.shape[1]
    G = CW // W
    NQ = TC // C
    groups = lambda x: [x[:, g * W:(g + 1) * W] for g in range(G)]
    per_group = lambda f, x: jnp.concatenate([f(xg) for xg in groups(x)], axis=1)

    r = shift_mix(r_ref, pr_scr, mur_ref)
    k = shift_mix(k_ref, pk_scr, muk_ref)
    v = shift_mix(v_ref, pv_scr, muv_ref)
    sm = shift_mix(s_ref, ps_scr, mus_ref)
    xwa = sm[:, 0:SM_XG]

    bdmask = _same_head_mask(W)
    bones = jnp.where(bdmask, 1.0, 0.0)
    head_sum = lambda x: per_group(lambda xg: _dot_sel(xg, bones, 2), x)
    tri = jnp.where(_same_head_mask(TC) & (_iota((TC, TC), 1) <= _iota((TC, TC), 0)), 1.0, 0.0)
    lane = _iota((C, W), 1)
    row = _iota((C, W), 0)
    lanehead = lane >> 6
    strict = (lane & 63) < row
    incl = (lane & 63) <= row
    eye_cat = jnp.where((lane & 63) == row, 1.0, 0.0)

    wlin = w0_ref[...] + _dot3(jnp.tanh(xwa), w2h_ref[...], w2l_ref[...])
    a = jax.nn.sigmoid(a0_ref[...] + _dot3(xwa, a2h_ref[...], a2l_ref[...]))
    gate = _dot3(jax.nn.sigmoid(sm[:, SM_XG:SM_XG + 256]), g2h_ref[...], g2l_ref[...])
    z = -wlin
    softplus = jnp.maximum(z, 0.0) + jnp.log(1.0 + jnp.exp(-jnp.abs(z)))
    ld = -jnp.exp(-softplus - 0.5)
    cum = _sel_dot(tri, ld, 3)
    cum_last = jnp.concatenate(
        [jnp.broadcast_to(cum[(q + 1) * C - 1:(q + 1) * C, :], (C, CW)) for q in range(NQ)], axis=0)
    kk = k * kk_ref[...]
    kk = kk / jnp.maximum(jnp.sqrt(head_sum(kk * kk)), 1e-12)
    k2 = k * (1.0 + (a - 1.0) * ka_ref[...])
    bvec = kk * a
    e_inv = jnp.exp(-cum)
    e_end = jnp.exp(cum_last - cum)
    At_f = -kk * jnp.exp(cum - ld)
    Rt_f = r * jnp.exp(cum)
    Bt_f = bvec * e_inv
    Kt_f = k2 * e_inv
    Bg_f = bvec * e_end
    Kg_f = k2 * e_end
    g_end = jnp.exp(cum_last)

    items = [(g, q) for q in range(NQ) for g in range(G)]
    blk = lambda x, g, q: x[q * C:(q + 1) * C, g * W:(g + 1) * W]
    bd = lambda x: _block_diag(x, bdmask)
    At = {it: blk(At_f, *it) for it in items}
    Rt = {it: blk(Rt_f, *it) for it in items}
    Vq = {it: blk(v, *it) for it in items}
    AA = {}
    for it in items:
        bk = jnp.concatenate([jnp.where(lanehead == h, X, 0.0)
                              for X in (blk(Bt_f, *it), blk(Kt_f, *it)) for h in range(4)], axis=0)
        AA[it] = _dotb_nt(jnp.concatenate([At[it], Rt[it]], axis=0), bk)
    A_ab = {it: jnp.where(strict, AA[it][0:C, 0:W], 0.0) for it in items}
    A_ak = {it: jnp.where(strict, AA[it][0:C, W:2 * W], 0.0) for it in items}
    A_rb = {it: jnp.where(incl, AA[it][C:2 * C, 0:W], 0.0) for it in items}
    A_rk = {it: jnp.where(incl, AA[it][C:2 * C, W:2 * W], 0.0) for it in items}
    M = dict(A_ab)
    Tm = {it: eye_cat + A_ab[it] for it in items}
    for _ in range(5):
        M = {it: _dotb(M[it], bd(M[it])) for it in items}
        Tm = {it: Tm[it] + _dotb(M[it], bd(Tm[it])) for it in items}
    Vbd = {it: bd(Vq[it]) for it in items}
    akv = {it: _dotb(A_ak[it], Vbd[it]) for it in items}
    rkv = {it: _dotb(A_rk[it], Vbd[it]) for it in items}

    S = [state_scr[g] for g in range(G)]
    ys = []
    for q in range(NQ):
        its = [(g, q) for g in range(G)]
        rhs = [_dotb_nt(At[it], S[it[0]]) + akv[it] for it in its]
        U = [_dotb(Tm[it], bd(rhs[g])) for g, it in enumerate(its)]
        ys.append(jnp.concatenate(
            [_dotb_nt(Rt[it], S[g]) + _dotb(A_rb[it], bd(U[g])) + rkv[it] for g, it in enumerate(its)], axis=1))
        upd = [_dotb(jnp.concatenate([U[g], Vq[it]], axis=0).T,
                     jnp.concatenate([blk(Bg_f, *it), blk(Kg_f, *it)], axis=0)) for g, it in enumerate(its)]
        S = [S[g] * blk(g_end, g, q)[0:1, :] + jnp.where(bdmask, upd[g], 0.0) for g in range(G)]
    for g in range(G):
        state_scr[g] = S[g]
    y = jnp.concatenate(ys, axis=0)

    inv_n = 1.0 / HEAD_DIM
    d = y - head_sum(y) * inv_n
    var = head_sum(d * d) * inv_n
    yn = d * lax.rsqrt(var + RWKV_GN_EPS) * lnw_ref[...] + lnb_ref[...]
    bonus = head_sum(r * k2 * rk_ref[...]) * v
    o_ref[...] = ((yn + bonus) * gate).astype(BF16)


def _rwkv_mix(proj, pr, B, T):
    BT = proj.shape[0]
    TC = min(RW_TC, T)
    W = RW_LANES
    CW = RWKV_WIDTH
    nct = T // TC
    row = lambda b, c: b * nct + c
    full = lambda shape: pl.BlockSpec(shape, lambda b, c: (0, 0))
    vec = full((1, CW))
    in_specs = [
        pl.BlockSpec((TC, CW), lambda b, c: (row(b, c), 0)),
        pl.BlockSpec((TC, CW), lambda b, c: (row(b, c), 1)),
        pl.BlockSpec((TC, CW), lambda b, c: (row(b, c), 2)),
        pl.BlockSpec((TC, SMALL_W), lambda b, c: (row(b, c), COL_SMALL // SMALL_W)),
        vec, vec, vec,
        full((1, SMALL_W)),
        vec, vec, vec, vec, vec, vec, vec,
        full((SM_XG, CW)), full((SM_XG, CW)), full((SM_XG, CW)), full((SM_XG, CW)),
        full((256, CW)), full((256, CW)),
    ]
    vm = (2 * (3 * TC * CW * 4 + TC * SMALL_W * 4 + TC * CW * 2 + (4 * SM_XG + 2 * 256) * CW * 2)
          + 24 * TC * CW * 4)
    return pl.pallas_call(
        _rwkv_kernel,
        grid=(B, nct),
        in_specs=in_specs,
        out_specs=pl.BlockSpec((TC, CW), lambda b, c: (row(b, c), 0)),
        out_shape=jax.ShapeDtypeStruct((BT, CW), BF16),
        scratch_shapes=[pltpu.VMEM((8, CW), F32), pltpu.VMEM((8, CW), F32), pltpu.VMEM((8, CW), F32),
                        pltpu.VMEM((8, SMALL_W), F32), pltpu.VMEM((CW // W, W, W), F32)],
        compiler_params=_cparams(("parallel", "arbitrary"), vm),
        name="rwkv_mix",
    )(proj, proj, proj, proj, pr["mu_r"], pr["mu_k"], pr["mu_v"], pr["mu_s"],
      pr["w0"], pr["a0"], pr["k_k"], pr["k_a"], pr["r_k"], pr["ln_w"], pr["ln_b"],
      pr["w2h"], pr["w2l"], pr["a2h"], pr["a2l"], pr["g2h"], pr["g2l"])


NSA_TT = 256


def _nsa_prep_kernel(q_ref, ks_ref, kw_ref, s_ref, qg_ref, kgs_ref, kgw_ref, e_ref, gsel_ref,
                     qp_ref, kvs_ref, kvw_ref, gt_ref):
    q = q_ref[...]
    bones = jnp.where(_same_head_mask(q.shape[1]), 1.0, 0.0)
    ms = _dot_sel(q * q, bones, 2) * (1.0 / HEAD_DIM)
    qn = (q * lax.rsqrt(ms + NORM_EPS) * qg_ref[...]) * (HEAD_DIM ** -0.5)
    qs = jnp.dot(qn.astype(BF16), e_ref[...], preferred_element_type=F32).astype(BF16)
    for h in range(NSA_GROUP):
        qp_ref[0, h] = qs[:, h * LANES:(h + 1) * LANES]

    bones2 = jnp.where(_same_head_mask(LANES), 1.0, 0.0)
    is_k = _iota((1, LANES), 1) < HEAD_DIM

    def norm_k(x, gain):
        ms = _dot_sel(x * x, bones2, 2) * (1.0 / HEAD_DIM)
        return (x * jnp.where(is_k, lax.rsqrt(ms + NORM_EPS) * gain, 1.0)).astype(BF16)

    kvs_ref[0, 0] = norm_k(ks_ref[...], kgs_ref[...])
    kvw_ref[0, 0] = norm_k(kw_ref[...], kgw_ref[...])
    gt_ref[0, 0] = jax.nn.sigmoid(_dot_sel(s_ref[...], gsel_ref[0], 3))


def _nsa_prep(proj, pn, B, T):
    tt = min(NSA_TT, T)
    ntt = T // tt
    G = NSA_KV_HEADS
    QW = NSA_GROUP * HEAD_DIM
    kv_blk = COL_KV // LANES
    row = lambda b, t, g: b * ntt + t
    vm = 2 * (tt * QW * 4 + 2 * tt * LANES * 4 + tt * SMALL_W * 4 + QW * 4 * LANES * 2 + SMALL_W * LANES * 4
              + 4 * tt * LANES * 2 + 2 * tt * LANES * 2 + tt * LANES * 4) + 12 * tt * QW * 4
    return pl.pallas_call(
        _nsa_prep_kernel,
        grid=(B, ntt, G),
        in_specs=[
            pl.BlockSpec((tt, QW), lambda b, t, g: (row(b, t, g), COL_Q // QW + g)),
            pl.BlockSpec((tt, LANES), lambda b, t, g: (row(b, t, g), kv_blk + 3 * g + 1)),
            pl.BlockSpec((tt, LANES), lambda b, t, g: (row(b, t, g), kv_blk + 3 * g + 2)),
            pl.BlockSpec((tt, SMALL_W), lambda b, t, g: (row(b, t, g), COL_SMALL // SMALL_W)),
            pl.BlockSpec((1, QW), lambda b, t, g: (0, 0)),
            pl.BlockSpec((1, LANES), lambda b, t, g: (0, 0)),
            pl.BlockSpec((1, LANES), lambda b, t, g: (0, 0)),
            pl.BlockSpec((QW, NSA_GROUP * LANES), lambda b, t, g: (0, 0)),
            pl.BlockSpec((1, SMALL_W, LANES), lambda b, t, g: (g, 0, 0)),
        ],
        out_specs=[
            pl.BlockSpec((1, NSA_GROUP, tt, LANES), lambda b, t, g: (b, g, t, 0)),
            pl.BlockSpec((1, 1, tt, LANES), lambda b, t, g: (b, g, t, 0)),
            pl.BlockSpec((1, 1, tt, LANES), lambda b, t, g: (b, g, t, 0)),
            pl.BlockSpec((1, 1, tt, LANES), lambda b, t, g: (b, g, t, 0)),
        ],
        out_shape=[
            jax.ShapeDtypeStruct((B, NSA_HEADS, T, LANES), BF16),
            jax.ShapeDtypeStruct((B, G, T, LANES), BF16),
            jax.ShapeDtypeStruct((B, G, T, LANES), BF16),
            jax.ShapeDtypeStruct((B, G, T, LANES), F32),
        ],
        compiler_params=_cparams(("parallel", "parallel", "parallel"), vm),
        name="nsa_prep",
    )(proj, proj, proj, proj, pn["q_g"], pn["kg_sel"], pn["kg_win"], pn["q_spread"], pn["gate_sel"])


def _gelu_tanh(x):
    return 0.5 * x * (1.0 + jnp.tanh(math.sqrt(2.0 / math.pi) * (x + 0.044715 * (x * x * x))))


def _nsa_compress_kernel(x_ref, pea_ref, peb_ref, wa_ref, wb_ref, w2_ref, kg_ref, o_ref):
    nsub = x_ref.shape[0] // CMP_STRIDE
    xs = jnp.concatenate([x_ref[pl.ds(s, nsub, stride=CMP_STRIDE), :] for s in range(CMP_STRIDE)], axis=1)
    p0 = jnp.dot((xs + pea_ref[...]).astype(BF16), wa_ref[...], preferred_element_type=F32)
    p1 = jnp.dot((xs + peb_ref[...]).astype(BF16), wb_ref[...], preferred_element_type=F32)
    hid = _gelu_tanh(p0 + pltpu.roll(p1, nsub - 1, 0))
    out = jnp.dot(hid.astype(BF16), w2_ref[...], preferred_element_type=F32)
    bones2 = jnp.where(_same_head_mask(LANES), 1.0, 0.0)
    ms = _dot_sel(out * out, bones2, 2) * (1.0 / HEAD_DIM)
    is_k = _iota((1, LANES), 1) < HEAD_DIM
    o_ref[0, 0] = (out * jnp.where(is_k, lax.rsqrt(ms + NORM_EPS) * kg_ref[...], 1.0)).astype(BF16)


def _nsa_compress(proj, pn, B, T):
    G = NSA_KV_HEADS
    nsub = T // CMP_STRIDE
    kv_blk = COL_KV // LANES
    KW = CMP_STRIDE * LANES
    full = lambda shape: pl.BlockSpec(shape, lambda b, g: tuple(0 for _ in shape))
    vm = 2 * (T * LANES * 4 + 2 * KW * LANES * 2 + nsub * LANES * 2) + 6 * nsub * KW * 4
    return pl.pallas_call(
        _nsa_compress_kernel,
        grid=(B, G),
        in_specs=[
            pl.BlockSpec((T, LANES), lambda b, g: (b, kv_blk + 3 * g)),
            full((1, KW)), full((1, KW)), full((KW, LANES)), full((KW, LANES)), full((LANES, LANES)),
            full((1, LANES)),
        ],
        out_specs=pl.BlockSpec((1, 1, nsub, LANES), lambda b, g: (b, g, 0, 0)),
        out_shape=jax.ShapeDtypeStruct((B, G, nsub, LANES), BF16),
        compiler_params=_cparams(("parallel", "parallel"), vm),
        name="nsa_compress",
    )(proj, pn["pe_a"], pn["pe_b"], pn["cw_a"], pn["cw_b"], pn["cw2"], pn["kg_cmp"])


NSA_KPAD = WINDOW
SEL_CHUNK = 4 * QUERY_BLOCK
WIN_KEYS = WINDOW + QUERY_BLOCK


def _masked_softmax_pv(s, valid, kv):
    m = jnp.max(jnp.where(valid, s, NEG_INF), axis=-1, keepdims=True)
    p = jnp.where(valid, jnp.exp(s - m), 0.0)
    return jnp.dot(p.astype(BF16), kv, preferred_element_type=F32), m, jnp.sum(p, axis=-1, keepdims=True)


def _nsa_attn_kernel(q_ref, kvc_ref, kvs_ref, kvw_ref, gt_ref, stab_ref, wtab_ref, ftab_ref, ctab_ref,
                     selmt_ref, expand_ref, gather_ref, o_ref, m_scr, l_scr, acc_scr, sc_scr, *, n_sel):
    i = pl.program_id(2)
    QB = QUERY_BLOCK
    R = NSA_GROUP * QB
    t0 = i * QB
    q = q_ref[0].reshape(R, LANES)
    tile4 = lambda z: jnp.concatenate([z, z, z, z], axis=0)

    kvc = kvc_ref[0, 0]
    ncp = kvc.shape[0]
    ccol = _iota((R, ncp), 1)
    crow = _iota((R, ncp), 0) & (QB - 1)
    cmask = (t0 + crow) >= (ccol * CMP_STRIDE + (CMP_BLOCK - 1))
    s = _dotb_nt(q, kvc) + ctab_ref[0].reshape(R, ncp)
    s = jnp.where(cmask, s, NEG_INF)
    p = jnp.where(cmask, jnp.exp(s - jnp.max(s, axis=-1, keepdims=True)), 0.0)
    lsum = jnp.sum(p, axis=-1, keepdims=True)
    p = p * jnp.where(lsum > 0.0, 1.0 / lsum, 0.0)
    o_c = jnp.dot(p.astype(BF16), kvc, preferred_element_type=F32)

    kw = kvw_ref[0, 0, pl.ds(pl.multiple_of(t0, QB), WIN_KEYS), :]
    wcol = _iota((R, WIN_KEYS), 1)
    wrow = _iota((R, WIN_KEYS), 0) & (QB - 1)
    wvalid = (wcol > wrow) & (wcol <= wrow + WINDOW) & (wcol >= NSA_KPAD - t0)
    acc_w, _, l_w = _masked_softmax_pv(_dotb_nt(q, kw) + wtab_ref[...].reshape(R, WIN_KEYS), wvalid, kw)
    o_w = acc_w * (1.0 / l_w)

    psum = p[0:QB] + p[QB:2 * QB] + p[2 * QB:3 * QB] + p[3 * QB:4 * QB]
    imp = lax.dot_general(selmt_ref[...], psum, (((1,), (1,)), ((), ())), precision=HIGHEST,
                          preferred_element_type=F32)
    ns = imp.shape[0]
    blk = _iota((ns, QB), 0)
    cur = (t0 + _iota((ns, QB), 1)) >> 6
    forced = (blk == 0) | (blk == cur) | (blk == cur - 1)
    score = jnp.where(forced, FORCE_SCORE, jnp.where(blk <= cur, imp, -1.0))
    sc_scr[...] = score
    rank = jnp.zeros((ns, QB), F32)
    for j in range(ns):
        other = sc_scr[j:j + 1, :]
        rank = rank + jnp.where(other > score, 1.0, 0.0) + jnp.where(other == score, jnp.where(blk > j, 1.0, 0.0), 0.0)
    chosen_t = jnp.where(rank < n_sel, 1.0, 0.0)
    chosen = jnp.concatenate([chosen_t, jnp.zeros((LANES - ns, QB), F32)], axis=0).T.astype(BF16)

    def sel_valid(first_tile):
        ex = jnp.concatenate([expand_ref[first_tile + u] for u in range(4)], axis=1)
        return tile4(jnp.dot(chosen, ex, preferred_element_type=F32)) > 0.5

    ks = kvs_ref[0, 0, pl.ds(pl.multiple_of(t0 + QB, QB), SEL_CHUNK), :]
    scol = _iota((R, SEL_CHUNK), 1)
    srow = _iota((R, SEL_CHUNK), 0) & (QB - 1)
    svalid = sel_valid(i) & (scol <= srow + (SEL_CHUNK - QB))
    acc_s, m_s, l_s = _masked_softmax_pv(_dotb_nt(q, ks) + stab_ref[...].reshape(R, SEL_CHUNK), svalid, ks)
    m_scr[...] = m_s
    l_scr[...] = l_s
    acc_scr[...] = acc_s
    far_bias = jnp.concatenate([ftab_ref[...].reshape(R, LANES)] * 4, axis=1)

    def far_body(it, carry):
        first = i - 4 * (it + 1)
        kf = kvs_ref[0, 0, pl.ds(pl.multiple_of((first + 1) * QB, QB), SEL_CHUNK), :]
        valid = sel_valid(first)
        s = jnp.where(valid, _dotb_nt(q, kf) + far_bias, NEG_INF)
        m_old = m_scr[...]
        m_new = jnp.maximum(m_old, jnp.max(s, axis=-1, keepdims=True))
        p = jnp.where(valid, jnp.exp(s - m_new), 0.0)
        alpha = jnp.exp(m_old - m_new)
        l_scr[...] = alpha * l_scr[...] + jnp.sum(p, axis=-1, keepdims=True)
        acc_scr[...] = alpha * acc_scr[...] + jnp.dot(p.astype(BF16), kf, preferred_element_type=F32)
        m_scr[...] = m_new
        return carry

    lax.fori_loop(0, i // 4, far_body, 0)
    o_s = acc_scr[...] * (1.0 / l_scr[...])

    gt = gt_ref[0, 0]
    outs = []
    for h in range(NSA_GROUP):
        rs = slice(h * QB, (h + 1) * QB)
        outs.append(gt[:, 3 * h:3 * h + 1] * o_c[rs] + gt[:, 3 * h + 1:3 * h + 2] * o_s[rs]
                    + gt[:, 3 * h + 2:3 * h + 3] * o_w[rs])
    cat = jnp.concatenate(outs, axis=1).astype(BF16)
    o_ref[...] = jnp.dot(cat, gather_ref[...], preferred_element_type=F32).astype(BF16)


def _nsa_attn(qp, kvc, kvs, kvw, gates, pn, B, T):
    G = NSA_KV_HEADS
    QB = QUERY_BLOCK
    NQ = T // QB
    ncp = kvc.shape[2]
    ns = T // SEL_BLOCK
    TP = T + NSA_KPAD
    R = NSA_GROUP * QB
    n_sel = min(N_SEL, ns)
    gtab = lambda w: pl.BlockSpec((NSA_GROUP, QB, w), lambda b, g, i: (g, 0, 0))
    vm = (2 * (R * LANES * 2 + ncp * LANES * 2 + 2 * TP * LANES * 2 + QB * LANES * 4
               + R * (SEL_CHUNK + WIN_KEYS + LANES + ncp) * 4 + ns * ncp * 4 + (NQ + 3) * LANES * LANES * 2
               + NSA_GROUP * LANES * NSA_GROUP * HEAD_DIM * 2 + QB * NSA_GROUP * HEAD_DIM * 2)
          + 2 * R * LANES * 4 + 8 * R * WIN_KEYS * 4)
    return pl.pallas_call(
        functools.partial(_nsa_attn_kernel, n_sel=n_sel),
        grid=(B, G, NQ),
        in_specs=[
            pl.BlockSpec((1, NSA_GROUP, QB, LANES), lambda b, g, i: (b, g, i, 0)),
            pl.BlockSpec((1, 1, ncp, LANES), lambda b, g, i: (b, g, 0, 0)),
            pl.BlockSpec((1, 1, TP, LANES), lambda b, g, i: (b, g, 0, 0)),
            pl.BlockSpec((1, 1, TP, LANES), lambda b, g, i: (b, g, 0, 0)),
            pl.BlockSpec((1, 1, QB, LANES), lambda b, g, i: (b, g, i, 0)),
            gtab(SEL_CHUNK), gtab(WIN_KEYS), gtab(LANES),
            pl.BlockSpec((1, NSA_GROUP, QB, ncp), lambda b, g, i: (i, g, 0, 0)),
            pl.BlockSpec((ns, ncp), lambda b, g, i: (0, 0)),
            pl.BlockSpec((NQ + 3, LANES, LANES), lambda b, g, i: (0, 0, 0)),
            pl.BlockSpec((NSA_GROUP * LANES, NSA_GROUP * HEAD_DIM), lambda b, g, i: (0, 0)),
        ],
        out_specs=pl.BlockSpec((QB, NSA_GROUP * HEAD_DIM), lambda b, g, i: (b * NQ + i, g)),
        out_shape=jax.ShapeDtypeStruct((B * T, NSA_WIDTH), BF16),
        scratch_shapes=[pltpu.VMEM((R, 1), F32), pltpu.VMEM((R, 1), F32), pltpu.VMEM((R, LANES), F32),
                        pltpu.VMEM((ns, QB), F32)],
        compiler_params=_cparams(("parallel", "parallel", "arbitrary"), vm),
        name="nsa_attn",
    )(qp, kvc, kvs, kvw, gates, pn["stab"], pn["wtab"], pn["ftab"], pn["ctab"],
      pn["sel_mt"], pn["expand"], pn["gather"])


def _rel_bucket_table():
    n = np.arange(REL_MAX_DIST + 1)
    max_exact = REL_BUCKETS // 2
    nf = np.maximum(n, max_exact).astype(np.float32)
    large = max_exact + (np.log(nf / np.float32(max_exact)) / np.float32(math.log(REL_MAX_DIST / max_exact))
                         * np.float32(REL_BUCKETS - max_exact)).astype(np.int32)
    large = np.minimum(large, REL_BUCKETS - 1)
    return np.where(n < max_exact, n, large).astype(np.int32)


def _sel_to_cmp_matrix(T, ncp):
    nc = T // CMP_STRIDE - CMP_BLOCK // CMP_STRIDE + 1
    ns = T // SEL_BLOCK
    cs = np.arange(nc) * CMP_STRIDE
    ss = np.arange(ns) * SEL_BLOCK
    lo = np.maximum(cs[None, :], ss[:, None])
    hi = np.minimum(cs[None, :] + CMP_BLOCK, ss[:, None] + SEL_BLOCK)
    out = np.zeros((ns, ncp), np.float32)
    out[:, :nc] = np.maximum(hi - lo, 0) / CMP_BLOCK
    return out


def _bias_tables(rel_bias, rel):
    bucket = _rel_bucket_table()[np.clip(rel, 0, REL_MAX_DIST)]
    onehot = (jnp.asarray(bucket.reshape(1, -1)) == jnp.arange(REL_BUCKETS, dtype=jnp.int32)[:, None]).astype(F32)
    tab = jnp.einsum('bh,bn->hn', rel_bias, onehot, precision=HIGHEST)
    return tab.reshape((rel_bias.shape[1],) + rel.shape)


def _prep_in_proj_weight(w_in):
    D = w_in.shape[0]
    nsa0 = RWKV_COLS
    kv0 = nsa0 + NSA_WIDTH
    gates0 = kv0 + 6 * NSA_KV_WIDTH
    merge0 = RWKV_COLS + NSA_COLS
    kv = w_in[:, kv0:gates0].reshape(D, 3, 2, NSA_KV_HEADS, HEAD_DIM)
    kv = jnp.transpose(kv, (0, 3, 1, 2, 4)).reshape(D, 6 * NSA_KV_WIDTH)
    pad = jnp.zeros((D, SMALL_W - (SM_GATES + 3 * NSA_HEADS)), w_in.dtype)
    return jnp.concatenate([
        w_in[:, 0:3 * RWKV_WIDTH],
        w_in[:, nsa0:kv0],
        w_in[:, merge0:merge0 + 2 * D_MODEL],
        kv,
        w_in[:, 3 * RWKV_WIDTH:RWKV_COLS],
        w_in[:, gates0:merge0],
        pad,
    ], axis=1).astype(BF16)


def _prep_rwkv_params(mu, w0, w2, a0, a2, g2, k_k, k_a, r_k, ln_w, ln_b):
    C = RWKV_WIDTH
    row = lambda z: z.reshape(1, -1).astype(F32)
    mu_s = jnp.concatenate([mu[3 * C:], jnp.zeros((SMALL_W - (RWKV_COLS - 3 * C),), F32)]).reshape(1, SMALL_W)
    zl = jnp.zeros((DECAY_LORA, C), F32)

    def hi_lo(w):
        hi = w.astype(BF16)
        return hi, (w - hi.astype(F32)).astype(BF16)

    w2h, w2l = hi_lo(jnp.concatenate([w2, zl], axis=0))
    a2h, a2l = hi_lo(jnp.concatenate([zl, a2], axis=0))
    g2h, g2l = hi_lo(jnp.concatenate([g2, jnp.zeros((256 - GATE_LORA, C), F32)], axis=0))
    return dict(
        mu_r=row(mu[0:C]), mu_k=row(mu[C:2 * C]), mu_v=row(mu[2 * C:3 * C]), mu_s=mu_s,
        w0=row(w0), a0=row(a0), k_k=row(k_k), k_a=row(k_a), r_k=row(r_k), ln_w=row(ln_w), ln_b=row(ln_b),
        w2h=w2h, w2l=w2l, a2h=a2h, a2l=a2l, g2h=g2h, g2l=g2l,
    )


def _prep_nsa_params(pe_k, w1_k, w2_k, pe_v, w1_v, w2_v, q_g, k_g, rel_bias, T):
    hd = HEAD_DIM
    ones = jnp.ones((hd,), F32)
    zeros = jnp.zeros((hd, hd), F32)
    ncp = T // CMP_STRIDE
    NQ = T // QUERY_BLOCK

    def blockdiag(a, b):
        return jnp.concatenate([jnp.concatenate([a, zeros], axis=1), jnp.concatenate([zeros, b], axis=1)], axis=0)

    w1k = w1_k.reshape(CMP_BLOCK, hd, hd)
    w1v = w1_v.reshape(CMP_BLOCK, hd, hd)
    w1 = jnp.stack([blockdiag(w1k[s], w1v[s]) for s in range(CMP_BLOCK)])
    pe = jnp.concatenate([pe_k, pe_v], axis=1)
    half = CMP_STRIDE

    spread = np.zeros((NSA_GROUP * hd, NSA_GROUP * LANES), np.float32)
    gather = np.zeros((NSA_GROUP * LANES, NSA_GROUP * hd), np.float32)
    for h in range(NSA_GROUP):
        for d in range(hd):
            spread[h * hd + d, h * LANES + d] = 1.0
            gather[h * LANES + hd + d, h * hd + d] = 1.0
    gate_sel = np.zeros((NSA_KV_HEADS, SMALL_W, LANES), np.float32)
    for g in range(NSA_KV_HEADS):
        for j in range(3 * NSA_GROUP):
            gate_sel[g, SM_GATES + 3 * NSA_GROUP * g + j, j] = 1.0
    expand = np.zeros((NQ + 3, LANES, LANES), np.float32)
    for j in range(NQ):
        for c in range(LANES):
            expand[j + 3, (j * QUERY_BLOCK + c) // SEL_BLOCK, c] = 1.0

    qi = np.arange(QUERY_BLOCK)[:, None]
    noff = (NQ - 1) * (QUERY_BLOCK // CMP_STRIDE)
    rel_all = np.concatenate([
        qi + (SEL_CHUNK - QUERY_BLOCK) - np.arange(SEL_CHUNK)[None, :],
        qi + WINDOW - np.arange(WIN_KEYS)[None, :],
        qi - (np.arange(ncp + noff)[None, :] - noff) * CMP_STRIDE - (CMP_BLOCK - 1),
        np.full((QUERY_BLOCK, LANES), REL_MAX_DIST),
    ], axis=1)
    tabs = _bias_tables(rel_bias, rel_all)
    o1, o2, o3 = SEL_CHUNK, SEL_CHUNK + WIN_KEYS, SEL_CHUNK + WIN_KEYS + ncp + noff
    cwide = tabs[:, :, o2:o3]
    step = QUERY_BLOCK // CMP_STRIDE
    ctab = jnp.stack([cwide[:, :, noff - step * i:noff - step * i + ncp] for i in range(NQ)])
    return dict(
        stab=tabs[:, :, 0:o1], wtab=tabs[:, :, o1:o2], ftab=tabs[:, :, o3:],
        q_g=jnp.tile(q_g, NSA_GROUP).reshape(1, -1),
        kg_cmp=jnp.concatenate([k_g[0], ones]).reshape(1, LANES),
        kg_sel=jnp.concatenate([k_g[1], ones]).reshape(1, LANES),
        kg_win=jnp.concatenate([k_g[2], ones]).reshape(1, LANES),
        pe_a=pe[:half].reshape(1, half * LANES), pe_b=pe[half:].reshape(1, half * LANES),
        cw_a=w1[:half].reshape(half * LANES, LANES).astype(BF16),
        cw_b=w1[half:].reshape(half * LANES, LANES).astype(BF16),
        cw2=blockdiag(w2_k, w2_v).astype(BF16),
        q_spread=jnp.asarray(spread, BF16), gather=jnp.asarray(gather, BF16),
        gate_sel=jnp.asarray(gate_sel), expand=jnp.asarray(expand, BF16),
        sel_mt=jnp.asarray(_sel_to_cmp_matrix(T, ncp)),
        ctab=ctab,
    )


def kernel(x, c, w_ada, b_ada, norm1_g, norm2_g, w_in, rwkv_mu, rwkv_w0, rwkv_w2, rwkv_a0, rwkv_a2, rwkv_g2, rwkv_k_k, rwkv_k_a, rwkv_r_k, rwkv_ln_w, rwkv_ln_b, cmp_pe_k, cmp_w1_k, cmp_w2_k, cmp_pe_v, cmp_w1_v, cmp_w2_v, q_norm_g, k_norm_g, rel_bias, w_o_rwkv, w_o_nsa, w_out, w_up, w_down):
    B, T, D = x.shape
    depth = w_in.shape[0]
    x2 = x.reshape(B * T, D)
    for l in range(depth):
        mod6 = _ada_mod(c, w_ada[l], b_ada[l]).reshape(B * 6, 1, D)
        proj = _in_proj(x2, norm1_g[l].reshape(1, D), mod6, _prep_in_proj_weight(w_in[l]), T)
        pr = _prep_rwkv_params(rwkv_mu[l], rwkv_w0[l], rwkv_w2[l], rwkv_a0[l], rwkv_a2[l], rwkv_g2[l],
                               rwkv_k_k[l], rwkv_k_a[l], rwkv_r_k[l], rwkv_ln_w[l], rwkv_ln_b[l])
        o_a = _rwkv_mix(proj, pr, B, T)
        pn = _prep_nsa_params(cmp_pe_k[l], cmp_w1_k[l], cmp_w2_k[l], cmp_pe_v[l], cmp_w1_v[l], cmp_w2_v[l],
                              q_norm_g[l], k_norm_g[l], rel_bias, T)
        qp, kvs, kvw, gates = _nsa_prep(proj, pn, B, T)
        front_pad = ((0, 0), (0, 0), (NSA_KPAD, 0), (0, 0))
        kvs, kvw = jnp.pad(kvs, front_pad), jnp.pad(kvw, front_pad)
        kvc = _nsa_compress(proj, pn, B, T)
        o_b = _nsa_attn(qp, kvc, kvs, kvw, gates, pn, B, T)
        mixed = _merge(o_a, o_b, w_o_rwkv[l].astype(BF16), w_o_nsa[l].astype(BF16), proj)
        x1, h2 = _out_proj(mixed, w_out[l].astype(BF16), x2, mod6, norm2_g[l].reshape(1, D), T)
        x2 = _mlp(h2, w_up[l].astype(BF16), w_down[l].astype(BF16), x1, mod6, T)
    return x2.reshape(B, T, D)
```

```python
import functools
import math

import numpy as np
import jax
import jax.numpy as jnp
from jax import lax
from jax.experimental import pallas as pl
from jax.experimental.pallas import tpu as pltpu

F32 = jnp.float32
BF16 = jnp.bfloat16
HIGHEST = lax.Precision.HIGHEST

D_MODEL = 2048
HEAD_DIM = 64
RWKV_WIDTH = D_MODEL // 2
DECAY_LORA = 64
ICLR_LORA = 64
GATE_LORA = 160
RWKV_GN_EPS = 64e-5
NSA_WIDTH = D_MODEL // 2
NSA_HEADS = NSA_WIDTH // HEAD_DIM
NSA_KV_HEADS = 4
NSA_GROUP = NSA_HEADS // NSA_KV_HEADS
NSA_KV_WIDTH = NSA_KV_HEADS * HEAD_DIM
CMP_BLOCK = 32
CMP_STRIDE = 16
SEL_BLOCK = 64
N_SEL = 8
WINDOW = 512
QUERY_BLOCK = 128
REL_BUCKETS = 32
REL_MAX_DIST = 128
D_FF = 4 * D_MODEL
NORM_EPS = 1e-6
NEG_INF = -1e30
FORCE_SCORE = 1e4

RWKV_COLS = 3 * RWKV_WIDTH + DECAY_LORA + ICLR_LORA + GATE_LORA
NSA_COLS = NSA_WIDTH + 6 * NSA_KV_WIDTH + 3 * NSA_HEADS

V7X_VMEM_BYTES = 64 * 1024 * 1024
LANES = 128

COL_RKV = 0
COL_Q = 3 * RWKV_WIDTH
COL_MERGE = COL_Q + NSA_WIDTH
COL_KV = COL_MERGE + 2 * D_MODEL
COL_SMALL = COL_KV + 6 * NSA_KV_WIDTH
SMALL_W = 512
PROJ_COLS = COL_SMALL + SMALL_W
SM_XG = DECAY_LORA + ICLR_LORA
SM_GATES = SM_XG + GATE_LORA


def _vmem_limit(nbytes):
    return int(min(nbytes * 5 // 4 + (4 << 20), V7X_VMEM_BYTES - (8 << 20)))


def _cparams(sem, vmem_bytes):
    return pltpu.CompilerParams(dimension_semantics=sem, vmem_limit_bytes=_vmem_limit(vmem_bytes))


def _ada_kernel(c_ref, w_ref, b_ref, o_ref):
    c = c_ref[...]
    s = c * jax.nn.sigmoid(c)
    o_ref[...] = jnp.dot(s.astype(BF16), w_ref[...].astype(BF16), preferred_element_type=F32) + b_ref[...]


def _ada_mod(c, w_ada, b_ada):
    B, D = c.shape
    N = w_ada.shape[1]
    tn = 1024
    return pl.pallas_call(
        _ada_kernel,
        grid=(N // tn,),
        in_specs=[
            pl.BlockSpec((B, D), lambda j: (0, 0)),
            pl.BlockSpec((D, tn), lambda j: (0, j)),
            pl.BlockSpec((1, tn), lambda j: (0, j)),
        ],
        out_specs=pl.BlockSpec((B, tn), lambda j: (0, j)),
        out_shape=jax.ShapeDtypeStruct((B, N), F32),
        compiler_params=_cparams(("parallel",), 2 * D * tn * 4 + D * tn * 2),
        name="ada_mod",
    )(c, w_ada, b_ada.reshape(1, N))


def _modulated_norm(x, g, sc, sh):
    ms = jnp.mean(x * x, axis=-1, keepdims=True)
    return (x * lax.rsqrt(ms + NORM_EPS) * g) * (1.0 + sc) + sh


def _inproj_kernel(x_ref, g_ref, sh_ref, sc_ref, w_ref, o_ref, h_scr):
    @pl.when(pl.program_id(1) == 0)
    def _():
        h_scr[...] = _modulated_norm(x_ref[...], g_ref[...], sc_ref[0], sh_ref[0]).astype(BF16)

    o_ref[...] = jnp.dot(h_scr[...], w_ref[...], preferred_element_type=F32)


def _in_proj(x2, g1, mod6, w_in_p, T):
    BT, D = x2.shape
    NP = w_in_p.shape[1]
    tm = min(512, T)
    tn = 1024
    tpb = T // tm
    vm = 2 * tm * D * 4 + tm * D * 2 + 2 * D * tn * 2 + 2 * tm * tn * 4 + 2 * tm * D * 4
    return pl.pallas_call(
        _inproj_kernel,
        grid=(BT // tm, NP // tn),
        in_specs=[
            pl.BlockSpec((tm, D), lambda i, j: (i, 0)),
            pl.BlockSpec((1, D), lambda i, j: (0, 0)),
            pl.BlockSpec((1, 1, D), lambda i, j: ((i // tpb) * 6 + 0, 0, 0)),
            pl.BlockSpec((1, 1, D), lambda i, j: ((i // tpb) * 6 + 1, 0, 0)),
            pl.BlockSpec((D, tn), lambda i, j: (0, j)),
        ],
        out_specs=pl.BlockSpec((tm, tn), lambda i, j: (i, j)),
        out_shape=jax.ShapeDtypeStruct((BT, NP), F32),
        scratch_shapes=[pltpu.VMEM((tm, D), BF16)],
        compiler_params=_cparams(("parallel", "arbitrary"), vm),
        name="in_proj",
    )(x2, g1, mod6, mod6, w_in_p)


def _merge_kernel(oa_ref, ob_ref, wa_ref, wb_ref, ga_ref, gb_ref, o_ref):
    ya = jnp.dot(oa_ref[...], wa_ref[...], preferred_element_type=F32)
    yb = jnp.dot(ob_ref[...], wb_ref[...], preferred_element_type=F32)
    o_ref[...] = (jax.nn.sigmoid(ga_ref[...]) * ya + jax.nn.sigmoid(gb_ref[...]) * yb).astype(BF16)


def _merge(o_a, o_b, w_oa, w_ob, proj):
    BT, W = o_a.shape
    D = w_oa.shape[1]
    tm, tn = 512, 1024
    ga0 = COL_MERGE // tn
    gb0 = (COL_MERGE + D) // tn
    vm = 2 * (2 * tm * W * 2 + 2 * W * tn * 2 + 2 * tm * tn * 4 + tm * tn * 2) + 3 * tm * tn * 4
    return pl.pallas_call(
        _merge_kernel,
        grid=(BT // tm, D // tn),
        in_specs=[
            pl.BlockSpec((tm, W), lambda i, j: (i, 0)),
            pl.BlockSpec((tm, W), lambda i, j: (i, 0)),
            pl.BlockSpec((W, tn), lambda i, j: (0, j)),
            pl.BlockSpec((W, tn), lambda i, j: (0, j)),
            pl.BlockSpec((tm, tn), lambda i, j: (i, ga0 + j)),
            pl.BlockSpec((tm, tn), lambda i, j: (i, gb0 + j)),
        ],
        out_specs=pl.BlockSpec((tm, tn), lambda i, j: (i, j)),
        out_shape=jax.ShapeDtypeStruct((BT, D), BF16),
        compiler_params=_cparams(("parallel", "parallel"), vm),
        name="merge",
    )(o_a, o_b, w_oa, w_ob, proj, proj)


def _outproj_kernel(m_ref, w_ref, x_ref, gt_ref, g_ref, sh_ref, sc_ref, x1_ref, h2_ref):
    y = jnp.dot(m_ref[...], w_ref[...], preferred_element_type=F32)
    x1 = x_ref[...] + gt_ref[0] * y
    x1_ref[...] = x1
    h2_ref[...] = _modulated_norm(x1, g_ref[...], sc_ref[0], sh_ref[0]).astype(BF16)


def _out_proj(mixed, w_out, x2, mod6, g2, T):
    BT, D = x2.shape
    tm = min(512, T)
    tpb = T // tm
    vm = 2 * (tm * D * 2 + D * D * 2 + tm * D * 4 + tm * D * 4 + tm * D * 2) + 3 * tm * D * 4
    mod_spec = lambda k: pl.BlockSpec((1, 1, D), lambda i: ((i // tpb) * 6 + k, 0, 0))
    return pl.pallas_call(
        _outproj_kernel,
        grid=(BT // tm,),
        in_specs=[
            pl.BlockSpec((tm, D), lambda i: (i, 0)),
            pl.BlockSpec((D, D), lambda i: (0, 0)),
            pl.BlockSpec((tm, D), lambda i: (i, 0)),
            mod_spec(2),
            pl.BlockSpec((1, D), lambda i: (0, 0)),
            mod_spec(3),
            mod_spec(4),
        ],
        out_specs=[pl.BlockSpec((tm, D), lambda i: (i, 0)), pl.BlockSpec((tm, D), lambda i: (i, 0))],
        out_shape=[jax.ShapeDtypeStruct((BT, D), F32), jax.ShapeDtypeStruct((BT, D), BF16)],
        compiler_params=_cparams(("parallel",), vm),
        name="out_proj",
    )(mixed, w_out, x2, mod6, g2, mod6, mod6)


def _mlp_kernel(h_ref, wu_ref, wd_ref, x_ref, gt_ref, o_ref, acc_ref):
    f = pl.program_id(1)
    u = jnp.dot(h_ref[...], wu_ref[...], preferred_element_type=F32)
    u = jnp.square(jnp.maximum(u, 0.0)).astype(BF16)
    part = jnp.dot(u, wd_ref[...], preferred_element_type=F32)

    @pl.when(f == 0)
    def _():
        acc_ref[...] = part

    @pl.when(f > 0)
    def _():
        acc_ref[...] += part

    @pl.when(f == pl.num_programs(1) - 1)
    def _():
        o_ref[...] = x_ref[...] + gt_ref[0] * acc_ref[...]


def _mlp(h2, w_up, w_down, x1, mod6, T):
    BT, D = x1.shape
    F = w_up.shape[1]
    tm = min(512, T)
    tf = 1024
    tpb = T // tm
    vm = 2 * (tm * D * 2 + 2 * D * tf * 2 + 2 * tm * D * 4) + tm * D * 4 + 2 * tm * tf * 4
    return pl.pallas_call(
        _mlp_kernel,
        grid=(BT // tm, F // tf),
        in_specs=[
            pl.BlockSpec((tm, D), lambda i, f: (i, 0)),
            pl.BlockSpec((D, tf), lambda i, f: (0, f)),
            pl.BlockSpec((tf, D), lambda i, f: (f, 0)),
            pl.BlockSpec((tm, D), lambda i, f: (i, 0)),
            pl.BlockSpec((1, 1, D), lambda i, f: ((i // tpb) * 6 + 5, 0, 0)),
        ],
        out_specs=pl.BlockSpec((tm, D), lambda i, f: (i, 0)),
        out_shape=jax.ShapeDtypeStruct((BT, D), F32),
        scratch_shapes=[pltpu.VMEM((tm, D), F32)],
        compiler_params=_cparams(("parallel", "arbitrary"), vm),
        name="mlp",
    )(h2, w_up, w_down, x1, mod6)


def _dotb(a, b):
    return jnp.dot(a.astype(BF16), b.astype(BF16), preferred_element_type=F32)


def _dotb_nt(a, b):
    return lax.dot_general(a.astype(BF16), b.astype(BF16), (((1,), (1,)), ((), ())),
                           preferred_element_type=F32)


def _split_bf16(x, terms):
    parts, rem = [], x
    for t in range(terms):
        p = rem.astype(BF16)
        parts.append(p)
        if t + 1 < terms:
            rem = rem - p.astype(F32)
    return parts


def _dot_sel(x, sel, terms):
    sel = sel.astype(BF16)
    return sum(jnp.dot(p, sel, preferred_element_type=F32) for p in _split_bf16(x, terms))


def _sel_dot(sel, x, terms):
    sel = sel.astype(BF16)
    return sum(jnp.dot(sel, p, preferred_element_type=F32) for p in _split_bf16(x, terms))


def _dot3(a, b_hi, b_lo):
    a_hi, a_lo = _split_bf16(a, 2)
    return (jnp.dot(a_hi, b_hi, preferred_element_type=F32) + jnp.dot(a_lo, b_hi, preferred_element_type=F32)
            + jnp.dot(a_hi, b_lo, preferred_element_type=F32))


def _iota(shape, axis):
    return lax.broadcasted_iota(jnp.int32, shape, axis)


def _same_head_mask(n):
    return (_iota((n, n), 0) >> 6) == (_iota((n, n), 1) >> 6)


RW_TC = 256
RW_C = 64
RW_LANES = 4 * HEAD_DIM


def _block_diag(x, bdmask):
    return jnp.where(bdmask, jnp.concatenate([x, x, x, x], axis=0), 0.0)


def _rwkv_kernel(r_ref, k_ref, v_ref, s_ref, mur_ref, muk_ref, muv_ref, mus_ref,
                 w0_ref, a0_ref, kk_ref, ka_ref, rk_ref, lnw_ref, lnb_ref,
                 w2h_ref, w2l_ref, a2h_ref, a2l_ref, g2h_ref, g2l_ref, o_ref,
                 pr_scr, pk_scr, pv_scr, ps_scr, state_scr):
    TC = r_ref.shape[0]
    C = RW_C
    W = RW_LANES

    @pl.when(pl.program_id(1) == 0)
    def _():
        pr_scr[...] = jnp.zeros_like(pr_scr)
        pk_scr[...] = jnp.zeros_like(pk_scr)
        pv_scr[...] = jnp.zeros_like(pv_scr)
        ps_scr[...] = jnp.zeros_like(ps_scr)
        state_scr[...] = jnp.zeros_like(state_scr)

    def shift_mix(p_ref, prev_scr, mu_ref):
        p = p_ref[...]
        row = _iota(p.shape, 0)
        shifted = jnp.where(row == 0, prev_scr[0:1, :], pltpu.roll(p, 1, 0))
        prev_scr[0:1, :] = p[TC - 1:TC, :]
        return p + (shifted - p) * mu_ref[...]

    CW = r_ref.shape[1]
    G = CW // W
    NQ = TC // C
    groups = lambda x: [x[:, g * W:(g + 1) * W] for g in range(G)]
    per_group = lambda f, x: jnp.concatenate([f(xg) for xg in groups(x)], axis=1)

    r = shift_mix(r_ref, pr_scr, mur_ref)
    k = shift_mix(k_ref, pk_scr, muk_ref)
    v = shift_mix(v_ref, pv_scr, muv_ref)
    sm = shift_mix(s_ref, ps_scr, mus_ref)
    xwa = sm[:, 0:SM_XG]

    bdmask = _same_head_mask(W)
    bones = jnp.where(bdmask, 1.0, 0.0)
    head_sum = lambda x: per_group(lambda xg: _dot_sel(xg, bones, 2), x)
    tri = jnp.where(_same_head_mask(TC) & (_iota((TC, TC), 1) <= _iota((TC, TC), 0)), 1.0, 0.0)
    lane = _iota((C, W), 1)
    row = _iota((C, W), 0)
    lanehead = lane >> 6
    strict = (lane & 63) < row
    incl = (lane & 63) <= row
    eye_cat = jnp.where((lane & 63) == row, 1.0, 0.0)

    wlin = w0_ref[...] + _dot3(jnp.tanh(xwa), w2h_ref[...], w2l_ref[...])
    a = jax.nn.sigmoid(a0_ref[...] + _dot3(xwa, a2h_ref[...], a2l_ref[...]))
    gate = _dot3(jax.nn.sigmoid(sm[:, SM_XG:SM_XG + 256]), g2h_ref[...], g2l_ref[...])
    z = -wlin
    softplus = jnp.maximum(z, 0.0) + jnp.log(1.0 + jnp.exp(-jnp.abs(z)))
    ld = -jnp.exp(-softplus - 0.5)
    cum = _sel_dot(tri, ld, 3)
    cum_last = jnp.concatenate(
        [jnp.broadcast_to(cum[(q + 1) * C - 1:(q + 1) * C, :], (C, CW)) for q in range(NQ)], axis=0)
    kk = k * kk_ref[...]
    kk = kk / jnp.maximum(jnp.sqrt(head_sum(kk * kk)), 1e-12)
    k2 = k * (1.0 + (a - 1.0) * ka_ref[...])
    bvec = kk * a
    e_inv = jnp.exp(-cum)
    e_end = jnp.exp(cum_last - cum)
    At_f = -kk * jnp.exp(cum - ld)
    Rt_f = r * jnp.exp(cum)
    Bt_f = bvec * e_inv
    Kt_f = k2 * e_inv
    Bg_f = bvec * e_end
    Kg_f = k2 * e_end
    g_end = jnp.exp(cum_last)

    items = [(g, q) for q in range(NQ) for g in range(G)]
    blk = lambda x, g, q: x[q * C:(q + 1) * C, g * W:(g + 1) * W]
    bd = lambda x: _block_diag(x, bdmask)
    At = {it: blk(At_f, *it) for it in items}
    Rt = {it: blk(Rt_f, *it) for it in items}
    Vq = {it: blk(v, *it) for it in items}
    AA = {}
    for it in items:
        bk = jnp.concatenate([jnp.where(lanehead == h, X, 0.0)
                              for X in (blk(Bt_f, *it), blk(Kt_f, *it)) for h in range(4)], axis=0)
        AA[it] = _dotb_nt(jnp.concatenate([At[it], Rt[it]], axis=0), bk)
    A_ab = {it: jnp.where(strict, AA[it][0:C, 0:W], 0.0) for it in items}
    A_ak = {it: jnp.where(strict, AA[it][0:C, W:2 * W], 0.0) for it in items}
    A_rb = {it: jnp.where(incl, AA[it][C:2 * C, 0:W], 0.0) for it in items}
    A_rk = {it: jnp.where(incl, AA[it][C:2 * C, W:2 * W], 0.0) for it in items}
    M = dict(A_ab)
    Tm = {it: eye_cat + A_ab[it] for it in items}
    for _ in range(5):
        M = {it: _dotb(M[it], bd(M[it])) for it in items}
        Tm = {it: Tm[it] + _dotb(M[it], bd(Tm[it])) for it in items}
    Vbd = {it: bd(Vq[it]) for it in items}
    akv = {it: _dotb(A_ak[it], Vbd[it]) for it in items}
    rkv = {it: _dotb(A_rk[it], Vbd[it]) for it in items}

    S = [state_scr[g] for g in range(G)]
    ys = []
    for q in range(NQ):
        its = [(g, q) for g in range(G)]
        rhs = [_dotb_nt(At[it], S[it[0]]) + akv[it] for it in its]
        U = [_dotb(Tm[it], bd(rhs[g])) for g, it in enumerate(its)]
        ys.append(jnp.concatenate(
            [_dotb_nt(Rt[it], S[g]) + _dotb(A_rb[it], bd(U[g])) + rkv[it] for g, it in enumerate(its)], axis=1))
        upd = [_dotb(jnp.concatenate([U[g], Vq[it]], axis=0).T,
                     jnp.concatenate([blk(Bg_f, *it), blk(Kg_f, *it)], axis=0)) for g, it in enumerate(its)]
        S = [S[g] * blk(g_end, g, q)[0:1, :] + jnp.where(bdmask, upd[g], 0.0) for g in range(G)]
    for g in range(G):
        state_scr[g] = S[g]
    y = jnp.concatenate(ys, axis=0)

    inv_n = 1.0 / HEAD_DIM
    d = y - head_sum(y) * inv_n
    var = head_sum(d * d) * inv_n
    yn = d * lax.rsqrt(var + RWKV_GN_EPS) * lnw_ref[...] + lnb_ref[...]
    bonus = head_sum(r * k2 * rk_ref[...]) * v
    o_ref[...] = ((yn + bonus) * gate).astype(BF16)


def _rwkv_mix(proj, pr, B, T):
    BT = proj.shape[0]
    TC = min(RW_TC, T)
    W = RW_LANES
    CW = RWKV_WIDTH
    nct = T // TC
    row = lambda b, c: b * nct + c
    full = lambda shape: pl.BlockSpec(shape, lambda b, c: (0, 0))
    vec = full((1, CW))
    in_specs = [
        pl.BlockSpec((TC, CW), lambda b, c: (row(b, c), 0)),
        pl.BlockSpec((TC, CW), lambda b, c: (row(b, c), 1)),
        pl.BlockSpec((TC, CW), lambda b, c: (row(b, c), 2)),
        pl.BlockSpec((TC, SMALL_W), lambda b, c: (row(b, c), COL_SMALL // SMALL_W)),
        vec, vec, vec,
        full((1, SMALL_W)),
        vec, vec, vec, vec, vec, vec, vec,
        full((SM_XG, CW)), full((SM_XG, CW)), full((SM_XG, CW)), full((SM_XG, CW)),
        full((256, CW)), full((256, CW)),
    ]
    vm = (2 * (3 * TC * CW * 4 + TC * SMALL_W * 4 + TC * CW * 2 + (4 * SM_XG + 2 * 256) * CW * 2)
          + 24 * TC * CW * 4)
    return pl.pallas_call(
        _rwkv_kernel,
        grid=(B, nct),
        in_specs=in_specs,
        out_specs=pl.BlockSpec((TC, CW), lambda b, c: (row(b, c), 0)),
        out_shape=jax.ShapeDtypeStruct((BT, CW), BF16),
        scratch_shapes=[pltpu.VMEM((8, CW), F32), pltpu.VMEM((8, CW), F32), pltpu.VMEM((8, CW), F32),
                        pltpu.VMEM((8, SMALL_W), F32), pltpu.VMEM((CW // W, W, W), F32)],
        compiler_params=_cparams(("parallel", "arbitrary"), vm),
        name="rwkv_mix",
    )(proj, proj, proj, proj, pr["mu_r"], pr["mu_k"], pr["mu_v"], pr["mu_s"],
      pr["w0"], pr["a0"], pr["k_k"], pr["k_a"], pr["r_k"], pr["ln_w"], pr["ln_b"],
      pr["w2h"], pr["w2l"], pr["a2h"], pr["a2l"], pr["g2h"], pr["g2l"])


NSA_TT = 256
PADFLAG_LANE = 32


def _nsa_prep_kernel(q_ref, ks_ref, kw_ref, s_ref, qg_ref, kgs_ref, kgw_ref, e_ref, gsel_ref,
                     qp_ref, ksl_ref, vsl_ref, kwl_ref, vwl_ref, gt_ref):
    q = q_ref[...]
    bones = jnp.where(_same_head_mask(q.shape[1]), 1.0, 0.0)
    ms = _dot_sel(q * q, bones, 2) * (1.0 / HEAD_DIM)
    qn = (q * lax.rsqrt(ms + NORM_EPS) * qg_ref[...]) * (HEAD_DIM ** -0.5)
    qs = jnp.dot(qn.astype(BF16), e_ref[...], preferred_element_type=F32).astype(BF16)
    for h in range(NSA_GROUP):
        qp_ref[0, h] = qs[:, h * LANES:(h + 1) * LANES]

    tt = q.shape[0]
    bones2 = jnp.where(_same_head_mask(LANES), 1.0, 0.0)
    lane = _iota((tt, LANES), 1)
    block_id = (pl.program_id(1) * tt + _iota((tt, LANES), 0)) >> 6
    onehot = jnp.where(lane == block_id, 1.0, 0.0)

    def slabs(x, gain, k_ref, v_ref):
        ms = _dot_sel(x * x, bones2, 2) * (1.0 / HEAD_DIM)
        k_ref[0, 0] = jnp.where(lane >= HEAD_DIM, x * lax.rsqrt(ms + NORM_EPS) * gain, onehot).astype(BF16)
        v_ref[0, 0] = jnp.where(lane < HEAD_DIM, x, 1.0).astype(BF16)

    slabs(ks_ref[...], kgs_ref[...], ksl_ref, vsl_ref)
    slabs(kw_ref[...], kgw_ref[...], kwl_ref, vwl_ref)
    gt_ref[0, 0] = jax.nn.sigmoid(_dot_sel(s_ref[...], gsel_ref[0], 3))


def _nsa_prep(proj, pn, B, T):
    tt = min(NSA_TT, T)
    ntt = T // tt
    G = NSA_KV_HEADS
    QW = NSA_GROUP * HEAD_DIM
    kv_blk = COL_KV // LANES
    row = lambda b, t, g: b * ntt + t
    vm = 2 * (tt * QW * 4 + 2 * tt * LANES * 4 + tt * SMALL_W * 4 + QW * 4 * LANES * 2 + SMALL_W * LANES * 4
              + 4 * tt * LANES * 2 + 2 * tt * LANES * 2 + tt * LANES * 4) + 12 * tt * QW * 4
    return pl.pallas_call(
        _nsa_prep_kernel,
        grid=(B, ntt, G),
        in_specs=[
            pl.BlockSpec((tt, QW), lambda b, t, g: (row(b, t, g), COL_Q // QW + g)),
            pl.BlockSpec((tt, LANES), lambda b, t, g: (row(b, t, g), kv_blk + 3 * g + 1)),
            pl.BlockSpec((tt, LANES), lambda b, t, g: (row(b, t, g), kv_blk + 3 * g + 2)),
            pl.BlockSpec((tt, SMALL_W), lambda b, t, g: (row(b, t, g), COL_SMALL // SMALL_W)),
            pl.BlockSpec((1, QW), lambda b, t, g: (0, 0)),
            pl.BlockSpec((1, LANES), lambda b, t, g: (0, 0)),
            pl.BlockSpec((1, LANES), lambda b, t, g: (0, 0)),
            pl.BlockSpec((QW, NSA_GROUP * LANES), lambda b, t, g: (0, 0)),
            pl.BlockSpec((1, SMALL_W, LANES), lambda b, t, g: (g, 0, 0)),
        ],
        out_specs=[pl.BlockSpec((1, NSA_GROUP, tt, LANES), lambda b, t, g: (b, g, t, 0))]
        + [pl.BlockSpec((1, 1, tt, LANES), lambda b, t, g: (b, g, t, 0))] * 5,
        out_shape=[jax.ShapeDtypeStruct((B, NSA_HEADS, T, LANES), BF16)]
        + [jax.ShapeDtypeStruct((B, G, T, LANES), BF16)] * 4
        + [jax.ShapeDtypeStruct((B, G, T, LANES), F32)],
        compiler_params=_cparams(("parallel", "parallel", "parallel"), vm),
        name="nsa_prep",
    )(proj, proj, proj, proj, pn["q_g"], pn["kg_sel"], pn["kg_win"], pn["q_spread"], pn["gate_sel"])


def _gelu_tanh(x):
    return 0.5 * x * (1.0 + jnp.tanh(math.sqrt(2.0 / math.pi) * (x + 0.044715 * (x * x * x))))


def _nsa_compress_kernel(x_ref, pea_ref, peb_ref, wa_ref, wb_ref, w2_ref, kg_ref, kc_ref, vc_ref):
    nsub = x_ref.shape[0] // CMP_STRIDE
    xs = jnp.concatenate([x_ref[pl.ds(s, nsub, stride=CMP_STRIDE), :] for s in range(CMP_STRIDE)], axis=1)
    p0 = jnp.dot((xs + pea_ref[...]).astype(BF16), wa_ref[...], preferred_element_type=F32)
    p1 = jnp.dot((xs + peb_ref[...]).astype(BF16), wb_ref[...], preferred_element_type=F32)
    hid = _gelu_tanh(p0 + pltpu.roll(p1, nsub - 1, 0))
    out = jnp.dot(hid.astype(BF16), w2_ref[...], preferred_element_type=F32)
    bones2 = jnp.where(_same_head_mask(LANES), 1.0, 0.0)
    ms = _dot_sel(out * out, bones2, 2) * (1.0 / HEAD_DIM)
    is_k = _iota(out.shape, 1) >= HEAD_DIM
    kc_ref[0, 0] = jnp.where(is_k, out * lax.rsqrt(ms + NORM_EPS) * kg_ref[...], 0.0).astype(BF16)
    vc_ref[0, 0] = jnp.where(is_k, 0.0, out).astype(BF16)


def _nsa_compress(proj, pn, B, T):
    G = NSA_KV_HEADS
    nsub = T // CMP_STRIDE
    kv_blk = COL_KV // LANES
    KW = CMP_STRIDE * LANES
    full = lambda shape: pl.BlockSpec(shape, lambda b, g: tuple(0 for _ in shape))
    vm = 2 * (T * LANES * 4 + 2 * KW * LANES * 2 + nsub * LANES * 2) + 6 * nsub * KW * 4
    return pl.pallas_call(
        _nsa_compress_kernel,
        grid=(B, G),
        in_specs=[
            pl.BlockSpec((T, LANES), lambda b, g: (b, kv_blk + 3 * g)),
            full((1, KW)), full((1, KW)), full((KW, LANES)), full((KW, LANES)), full((LANES, LANES)),
            full((1, LANES)),
        ],
        out_specs=[pl.BlockSpec((1, 1, nsub, LANES), lambda b, g: (b, g, 0, 0))] * 2,
        out_shape=[jax.ShapeDtypeStruct((B, G, nsub, LANES), BF16)] * 2,
        compiler_params=_cparams(("parallel", "parallel"), vm),
        name="nsa_compress",
    )(proj, pn["pe_a"], pn["pe_b"], pn["cw_a"], pn["cw_b"], pn["cw2"], pn["kg_cmp"])


NSA_KPAD = WINDOW
SEL_CHUNK = 4 * QUERY_BLOCK
WIN_KEYS = WINDOW + QUERY_BLOCK


def _lane_tile_max(s):
    tiles = [s[:, j * LANES:(j + 1) * LANES] for j in range(s.shape[1] // LANES)]
    while len(tiles) > 1:
        tiles = [jnp.maximum(a, b) for a, b in zip(tiles[0::2], tiles[1::2])] + ([tiles[-1]] if len(tiles) % 2 else [])
    return tiles[0]


def _nsa_attn_kernel(q_ref, kc_ref, vc_ref, ks_ref, vs_ref, kw_ref, vw_ref, gt_ref, stab_ref, wtab_ref, ctab_ref,
                     selmt_ref, gather_ref, grep_ref, o_ref, sbuf, mx_scr, acc_scr, sc_scr, *, n_sel):
    i = pl.program_id(2)
    QB = QUERY_BLOCK
    R = NSA_GROUP * QB
    t0 = i * QB
    q = q_ref[0].reshape(R, LANES)
    q32 = q.astype(F32)
    tile4 = lambda z: jnp.concatenate([z, z, z, z], axis=0)
    lane = _iota((QB, LANES), 1)
    pad_mask = jnp.where(lane == PADFLAG_LANE, NEG_INF, 0.0)

    kc = kc_ref[0, 0]
    ncp = kc.shape[0]
    s = _dotb_nt(q, kc) + ctab_ref[0].reshape(R, ncp)
    m = jnp.max(s, axis=-1, keepdims=True)
    p_c = jnp.exp(s - m)
    lsum = jnp.sum(p_c, axis=-1, keepdims=True)
    p_c = p_c * jnp.where(m > 0.5 * NEG_INF, 1.0 / lsum, 0.0)
    o_c = jnp.dot(p_c.astype(BF16), vc_ref[0, 0], preferred_element_type=F32)

    wrows = pl.ds(pl.multiple_of(t0, QB), WIN_KEYS)
    q_win = (q32 + tile4(pad_mask)).astype(BF16)
    s = _dotb_nt(q_win, kw_ref[0, 0, wrows, :]) + wtab_ref[...].reshape(R, WIN_KEYS)
    p = jnp.exp(s - jnp.max(_lane_tile_max(s), axis=-1, keepdims=True))
    acc_w = jnp.dot(p.astype(BF16), vw_ref[0, 0, wrows, :], preferred_element_type=F32)

    psum = p_c[0:QB] + p_c[QB:2 * QB] + p_c[2 * QB:3 * QB] + p_c[3 * QB:4 * QB]
    selmt = selmt_ref[...]
    imp = sum(lax.dot_general(selmt, part, (((1,), (1,)), ((), ())), preferred_element_type=F32)
              for part in _split_bf16(psum, 3))
    ns = imp.shape[0]
    blk = _iota((ns, QB), 0)
    cur = (t0 + _iota((ns, QB), 1)) >> 6
    forced = (blk == 0) | (blk == cur) | (blk == cur - 1)
    score = jnp.where(forced, FORCE_SCORE, jnp.where(blk <= cur, imp, -1.0))
    sc_scr[...] = score
    ranks = []
    for j in range(ns):
        other = sc_scr[j:j + 1, :]
        ranks.append(jnp.where(other > score, 1.0, 0.0) + jnp.where(other == score, jnp.where(blk > j, 1.0, 0.0), 0.0))
    while len(ranks) > 1:
        ranks = [a + b for a, b in zip(ranks[0::2], ranks[1::2])] + ([ranks[-1]] if len(ranks) % 2 else [])
    rank = ranks[0]
    chosen_t = jnp.where(rank < n_sel, 1.0, 0.0)
    chosen = jnp.concatenate([chosen_t, jnp.zeros((LANES - ns, QB), F32)], axis=0).T
    sel_mask = jnp.where(lane < ns, (chosen - 1.0) * (-NEG_INF), pad_mask)
    q_sel = (q32 + tile4(sel_mask)).astype(BF16)

    def chunk_rows(c):
        return pl.ds(pl.multiple_of((i - 4 * c + 1) * QB, QB), SEL_CHUNK)

    s0 = _dotb_nt(q_sel, ks_ref[0, 0, chunk_rows(0), :]) + stab_ref[...].reshape(R, SEL_CHUNK)
    sbuf[0] = s0
    mx_scr[...] = _lane_tile_max(s0)

    def scores_body(c, carry):
        s = _dotb_nt(q_sel, ks_ref[0, 0, chunk_rows(c), :])
        sbuf[c] = s
        mx_scr[...] = jnp.maximum(mx_scr[...], _lane_tile_max(s))
        return carry

    n_far = i // 4
    lax.fori_loop(1, n_far + 1, scores_body, 0)
    m_s = jnp.max(mx_scr[...], axis=-1, keepdims=True)

    acc_scr[...] = jnp.zeros_like(acc_scr)

    def values_body(c, carry):
        p = jnp.exp(sbuf[c] - m_s).astype(BF16)
        acc_scr[...] += jnp.dot(p, vs_ref[0, 0, chunk_rows(c), :], preferred_element_type=F32)
        return carry

    lax.fori_loop(0, n_far + 1, values_body, 0)
    acc_s = acc_scr[...]

    low = _iota((R, LANES), 1) < HEAD_DIM

    def natural(o):
        cat = jnp.concatenate([o[h * QB:(h + 1) * QB] for h in range(NSA_GROUP)], axis=1).astype(BF16)
        return jnp.dot(cat, gather_ref[...], preferred_element_type=F32)

    normalised = lambda acc: jnp.where(low, acc * (1.0 / pltpu.roll(acc, HEAD_DIM, 1)), 0.0)
    gt = gt_ref[0, 0]
    gate = lambda c: _dot_sel(gt, grep_ref[c], 2)
    out = gate(0) * natural(o_c) + gate(1) * natural(normalised(acc_s)) + gate(2) * natural(normalised(acc_w))
    o_ref[...] = out.astype(BF16)


def _nsa_attn(qp, kc, vc, ks, vs, kw, vw, gates, pn, B, T):
    G = NSA_KV_HEADS
    QB = QUERY_BLOCK
    NQ = T // QB
    ncp = kc.shape[2]
    ns = T // SEL_BLOCK
    TP = T + NSA_KPAD
    R = NSA_GROUP * QB
    n_sel = min(N_SEL, ns)
    n_chunks = (NQ - 1) // 4 + 1
    gtab = lambda w: pl.BlockSpec((NSA_GROUP, QB, w), lambda b, g, i: (g, 0, 0))
    slab = lambda rows: pl.BlockSpec((1, 1, rows, LANES), lambda b, g, i: (b, g, 0, 0))
    vm = (2 * (R * LANES * 2 + 2 * ncp * LANES * 2 + 4 * TP * LANES * 2 + QB * LANES * 4
               + R * (SEL_CHUNK + WIN_KEYS + ncp) * 4 + ns * ncp * 4
               + NSA_GROUP * LANES * NSA_GROUP * HEAD_DIM * 2 + QB * NSA_GROUP * HEAD_DIM * 2)
          + n_chunks * R * SEL_CHUNK * 4 + 2 * R * LANES * 4 + 6 * R * WIN_KEYS * 4)
    return pl.pallas_call(
        functools.partial(_nsa_attn_kernel, n_sel=n_sel),
        grid=(B, G, NQ),
        in_specs=[
            pl.BlockSpec((1, NSA_GROUP, QB, LANES), lambda b, g, i: (b, g, i, 0)),
            slab(ncp), slab(ncp), slab(TP), slab(TP), slab(TP), slab(TP),
            pl.BlockSpec((1, 1, QB, LANES), lambda b, g, i: (b, g, i, 0)),
            gtab(SEL_CHUNK), gtab(WIN_KEYS),
            pl.BlockSpec((1, NSA_GROUP, QB, ncp), lambda b, g, i: (i, g, 0, 0)),
            pl.BlockSpec((ns, ncp), lambda b, g, i: (0, 0)),
            pl.BlockSpec((NSA_GROUP * LANES, NSA_GROUP * HEAD_DIM), lambda b, g, i: (0, 0)),
            pl.BlockSpec((3, LANES, NSA_GROUP * HEAD_DIM), lambda b, g, i: (0, 0, 0)),
        ],
        out_specs=pl.BlockSpec((QB, NSA_GROUP * HEAD_DIM), lambda b, g, i: (b * NQ + i, g)),
        out_shape=jax.ShapeDtypeStruct((B * T, NSA_WIDTH), BF16),
        scratch_shapes=[pltpu.VMEM((n_chunks, R, SEL_CHUNK), F32), pltpu.VMEM((R, LANES), F32),
                        pltpu.VMEM((R, LANES), F32), pltpu.VMEM((ns, QB), F32)],
        compiler_params=_cparams(("parallel", "parallel", "arbitrary"), vm),
        name="nsa_attn",
    )(qp, kc, vc, ks, vs, kw, vw, gates, pn["stab"], pn["wtab"], pn["ctab"], pn["sel_mt"], pn["gather"],
      pn["gate_rep"])


def _rel_bucket_table():
    n = np.arange(REL_MAX_DIST + 1)
    max_exact = REL_BUCKETS // 2
    nf = np.maximum(n, max_exact).astype(np.float32)
    large = max_exact + (np.log(nf / np.float32(max_exact)) / np.float32(math.log(REL_MAX_DIST / max_exact))
                         * np.float32(REL_BUCKETS - max_exact)).astype(np.int32)
    large = np.minimum(large, REL_BUCKETS - 1)
    return np.where(n < max_exact, n, large).astype(np.int32)


def _sel_to_cmp_matrix(T, ncp):
    nc = T // CMP_STRIDE - CMP_BLOCK // CMP_STRIDE + 1
    ns = T // SEL_BLOCK
    cs = np.arange(nc) * CMP_STRIDE
    ss = np.arange(ns) * SEL_BLOCK
    lo = np.maximum(cs[None, :], ss[:, None])
    hi = np.minimum(cs[None, :] + CMP_BLOCK, ss[:, None] + SEL_BLOCK)
    out = np.zeros((ns, ncp), np.float32)
    out[:, :nc] = np.maximum(hi - lo, 0) / CMP_BLOCK
    return out


def _bias_tables(rel_bias, rel):
    bucket = _rel_bucket_table()[np.clip(rel, 0, REL_MAX_DIST)]
    onehot = (jnp.asarray(bucket.reshape(1, -1)) == jnp.arange(REL_BUCKETS, dtype=jnp.int32)[:, None]).astype(F32)
    tab = jnp.einsum('bh,bn->hn', rel_bias, onehot, precision=HIGHEST)
    return tab.reshape((rel_bias.shape[1],) + rel.shape)


def _prep_in_proj_weight(w_in):
    D = w_in.shape[0]
    nsa0 = RWKV_COLS
    kv0 = nsa0 + NSA_WIDTH
    gates0 = kv0 + 6 * NSA_KV_WIDTH
    merge0 = RWKV_COLS + NSA_COLS
    kv = w_in[:, kv0:gates0].reshape(D, 3, 2, NSA_KV_HEADS, HEAD_DIM)[:, :, ::-1]
    kv = jnp.transpose(kv, (0, 3, 1, 2, 4)).reshape(D, 6 * NSA_KV_WIDTH)
    pad = jnp.zeros((D, SMALL_W - (SM_GATES + 3 * NSA_HEADS)), w_in.dtype)
    return jnp.concatenate([
        w_in[:, 0:3 * RWKV_WIDTH],
        w_in[:, nsa0:kv0],
        w_in[:, merge0:merge0 + 2 * D_MODEL],
        kv,
        w_in[:, 3 * RWKV_WIDTH:RWKV_COLS],
        w_in[:, gates0:merge0],
        pad,
    ], axis=1).astype(BF16)


def _prep_rwkv_params(mu, w0, w2, a0, a2, g2, k_k, k_a, r_k, ln_w, ln_b):
    C = RWKV_WIDTH
    row = lambda z: z.reshape(1, -1).astype(F32)
    mu_s = jnp.concatenate([mu[3 * C:], jnp.zeros((SMALL_W - (RWKV_COLS - 3 * C),), F32)]).reshape(1, SMALL_W)
    zl = jnp.zeros((DECAY_LORA, C), F32)

    def hi_lo(w):
        hi = w.astype(BF16)
        return hi, (w - hi.astype(F32)).astype(BF16)

    w2h, w2l = hi_lo(jnp.concatenate([w2, zl], axis=0))
    a2h, a2l = hi_lo(jnp.concatenate([zl, a2], axis=0))
    g2h, g2l = hi_lo(jnp.concatenate([g2, jnp.zeros((256 - GATE_LORA, C), F32)], axis=0))
    return dict(
        mu_r=row(mu[0:C]), mu_k=row(mu[C:2 * C]), mu_v=row(mu[2 * C:3 * C]), mu_s=mu_s,
        w0=row(w0), a0=row(a0), k_k=row(k_k), k_a=row(k_a), r_k=row(r_k), ln_w=row(ln_w), ln_b=row(ln_b),
        w2h=w2h, w2l=w2l, a2h=a2h, a2l=a2l, g2h=g2h, g2l=g2l,
    )


def _prep_nsa_params(pe_k, w1_k, w2_k, pe_v, w1_v, w2_v, q_g, k_g, rel_bias, T):
    hd = HEAD_DIM
    ones = jnp.ones((hd,), F32)
    zeros = jnp.zeros((hd, hd), F32)
    ncp = T // CMP_STRIDE
    NQ = T // QUERY_BLOCK

    def blockdiag(a, b):
        return jnp.concatenate([jnp.concatenate([a, zeros], axis=1), jnp.concatenate([zeros, b], axis=1)], axis=0)

    w1k = w1_k.reshape(CMP_BLOCK, hd, hd)
    w1v = w1_v.reshape(CMP_BLOCK, hd, hd)
    w1 = jnp.stack([blockdiag(w1v[s], w1k[s]) for s in range(CMP_BLOCK)])
    pe = jnp.concatenate([pe_v, pe_k], axis=1)
    half = CMP_STRIDE

    spread = np.zeros((NSA_GROUP * hd, NSA_GROUP * LANES), np.float32)
    gather = np.zeros((NSA_GROUP * LANES, NSA_GROUP * hd), np.float32)
    for h in range(NSA_GROUP):
        for d in range(hd):
            spread[h * hd + d, h * LANES + hd + d] = 1.0
            gather[h * LANES + d, h * hd + d] = 1.0
    gate_sel = np.zeros((NSA_KV_HEADS, SMALL_W, LANES), np.float32)
    for g in range(NSA_KV_HEADS):
        for j in range(3 * NSA_GROUP):
            gate_sel[g, SM_GATES + 3 * NSA_GROUP * g + j, j] = 1.0
    gate_rep = np.zeros((3, LANES, NSA_GROUP * hd), np.float32)
    for c in range(3):
        for h in range(NSA_GROUP):
            gate_rep[c, 3 * h + c, h * hd:(h + 1) * hd] = 1.0

    qi = np.arange(QUERY_BLOCK)[:, None]
    noff = (NQ - 1) * (QUERY_BLOCK // CMP_STRIDE)
    rel_s = qi + (SEL_CHUNK - QUERY_BLOCK) - np.arange(SEL_CHUNK)[None, :]
    rel_w = qi + WINDOW - np.arange(WIN_KEYS)[None, :]
    rel_c = qi - (np.arange(ncp + noff)[None, :] - noff) * CMP_STRIDE - (CMP_BLOCK - 1)
    far = np.full((QUERY_BLOCK, 1), REL_MAX_DIST)
    tabs = _bias_tables(rel_bias, np.concatenate([rel_s, rel_w, rel_c, far], axis=1))
    o1, o2, o3 = SEL_CHUNK, SEL_CHUNK + WIN_KEYS, SEL_CHUNK + WIN_KEYS + ncp + noff
    masked = lambda tab, ok: jnp.where(jnp.asarray(ok)[None], tab, NEG_INF)
    stab = masked(tabs[:, :, 0:o1] - tabs[:, :, o3:], rel_s >= 0)
    wtab = masked(tabs[:, :, o1:o2], (rel_w >= 0) & (rel_w < WINDOW))
    cwide = masked(tabs[:, :, o2:o3], rel_c >= 0)
    step = QUERY_BLOCK // CMP_STRIDE
    ctab = jnp.stack([cwide[:, :, noff - step * i:noff - step * i + ncp] for i in range(NQ)])
    return dict(
        stab=stab, wtab=wtab, ctab=ctab,
        q_g=jnp.tile(q_g, NSA_GROUP).reshape(1, -1),
        kg_cmp=jnp.concatenate([ones, k_g[0]]).reshape(1, LANES),
        kg_sel=jnp.concatenate([ones, k_g[1]]).reshape(1, LANES),
        kg_win=jnp.concatenate([ones, k_g[2]]).reshape(1, LANES),
        pe_a=pe[:half].reshape(1, half * LANES), pe_b=pe[half:].reshape(1, half * LANES),
        cw_a=w1[:half].reshape(half * LANES, LANES).astype(BF16),
        cw_b=w1[half:].reshape(half * LANES, LANES).astype(BF16),
        cw2=blockdiag(w2_v, w2_k).astype(BF16),
        q_spread=jnp.asarray(spread, BF16), gather=jnp.asarray(gather, BF16),
        gate_sel=jnp.asarray(gate_sel), gate_rep=jnp.asarray(gate_rep, BF16),
        sel_mt=jnp.asarray(_sel_to_cmp_matrix(T, ncp)),
    )


def _front_pad(slab, flagged):
    B, G, _, L = slab.shape
    row = jnp.zeros((L,), slab.dtype)
    if flagged:
        row = row.at[PADFLAG_LANE].set(1.0)
    return jnp.concatenate([jnp.broadcast_to(row, (B, G, NSA_KPAD, L)), slab], axis=2)


def kernel(x, c, w_ada, b_ada, norm1_g, norm2_g, w_in, rwkv_mu, rwkv_w0, rwkv_w2, rwkv_a0, rwkv_a2, rwkv_g2, rwkv_k_k, rwkv_k_a, rwkv_r_k, rwkv_ln_w, rwkv_ln_b, cmp_pe_k, cmp_w1_k, cmp_w2_k, cmp_pe_v, cmp_w1_v, cmp_w2_v, q_norm_g, k_norm_g, rel_bias, w_o_rwkv, w_o_nsa, w_out, w_up, w_down):
    B, T, D = x.shape
    depth = w_in.shape[0]
    x2 = x.reshape(B * T, D)
    for l in range(depth):
        mod6 = _ada_mod(c, w_ada[l], b_ada[l]).reshape(B * 6, 1, D)
        proj = _in_proj(x2, norm1_g[l].reshape(1, D), mod6, _prep_in_proj_weight(w_in[l]), T)
        pr = _prep_rwkv_params(rwkv_mu[l], rwkv_w0[l], rwkv_w2[l], rwkv_a0[l], rwkv_a2[l], rwkv_g2[l],
                               rwkv_k_k[l], rwkv_k_a[l], rwkv_r_k[l], rwkv_ln_w[l], rwkv_ln_b[l])
        o_a = _rwkv_mix(proj, pr, B, T)
        pn = _prep_nsa_params(cmp_pe_k[l], cmp_w1_k[l], cmp_w2_k[l], cmp_pe_v[l], cmp_w1_v[l], cmp_w2_v[l],
                              q_norm_g[l], k_norm_g[l], rel_bias, T)
        qp, ks, vs, kw, vw, gates = _nsa_prep(proj, pn, B, T)
        ks, vs, kw, vw = _front_pad(ks, True), _front_pad(vs, False), _front_pad(kw, True), _front_pad(vw, False)
        kc, vc = _nsa_compress(proj, pn, B, T)
        o_b = _nsa_attn(qp, kc, vc, ks, vs, kw, vw, gates, pn, B, T)
        mixed = _merge(o_a, o_b, w_o_rwkv[l].astype(BF16), w_o_nsa[l].astype(BF16), proj)
        x1, h2 = _out_proj(mixed, w_out[l].astype(BF16), x2, mod6, norm2_g[l].reshape(1, D), T)
        x2 = _mlp(h2, w_up[l].astype(BF16), w_down[l].astype(BF16), x1, mod6, T)
    return x2.reshape(B, T, D)
```

```python
import functools
import math

import numpy as np
import jax
import jax.numpy as jnp
from jax import lax
from jax.experimental import pallas as pl
from jax.experimental.pallas import tpu as pltpu

F32 = jnp.float32
BF16 = jnp.bfloat16
HIGHEST = lax.Precision.HIGHEST

D_MODEL = 2048
HEAD_DIM = 64
RWKV_WIDTH = D_MODEL // 2
DECAY_LORA = 64
ICLR_LORA = 64
GATE_LORA = 160
RWKV_GN_EPS = 64e-5
NSA_WIDTH = D_MODEL // 2
NSA_HEADS = NSA_WIDTH // HEAD_DIM
NSA_KV_HEADS = 4
NSA_GROUP = NSA_HEADS // NSA_KV_HEADS
NSA_KV_WIDTH = NSA_KV_HEADS * HEAD_DIM
CMP_BLOCK = 32
CMP_STRIDE = 16
SEL_BLOCK = 64
N_SEL = 8
WINDOW = 512
QUERY_BLOCK = 128
REL_BUCKETS = 32
REL_MAX_DIST = 128
D_FF = 4 * D_MODEL
NORM_EPS = 1e-6
NEG_INF = -1e30
FORCE_SCORE = 1e4

RWKV_COLS = 3 * RWKV_WIDTH + DECAY_LORA + ICLR_LORA + GATE_LORA
NSA_COLS = NSA_WIDTH + 6 * NSA_KV_WIDTH + 3 * NSA_HEADS

V7X_VMEM_BYTES = 64 * 1024 * 1024
LANES = 128

COL_RKV = 0
COL_Q = 3 * RWKV_WIDTH
COL_MERGE = COL_Q + NSA_WIDTH
COL_KV = COL_MERGE + 2 * D_MODEL
COL_SMALL = COL_KV + 6 * NSA_KV_WIDTH
SMALL_W = 512
PROJ_COLS = COL_SMALL + SMALL_W
SM_XG = DECAY_LORA + ICLR_LORA
SM_GATES = SM_XG + GATE_LORA


def _vmem_limit(nbytes):
    return int(min(nbytes * 5 // 4 + (4 << 20), V7X_VMEM_BYTES - (8 << 20)))


def _cparams(sem, vmem_bytes):
    return pltpu.CompilerParams(dimension_semantics=sem, vmem_limit_bytes=_vmem_limit(vmem_bytes))


def _ada_kernel(c_ref, w_ref, b_ref, o_ref):
    c = c_ref[...]
    s = c * jax.nn.sigmoid(c)
    o_ref[...] = jnp.dot(s.astype(BF16), w_ref[...].astype(BF16), preferred_element_type=F32) + b_ref[...]


def _ada_mod(c, w_ada, b_ada):
    B, D = c.shape
    N = w_ada.shape[1]
    tn = 1024
    return pl.pallas_call(
        _ada_kernel,
        grid=(N // tn,),
        in_specs=[
            pl.BlockSpec((B, D), lambda j: (0, 0)),
            pl.BlockSpec((D, tn), lambda j: (0, j)),
            pl.BlockSpec((1, tn), lambda j: (0, j)),
        ],
        out_specs=pl.BlockSpec((B, tn), lambda j: (0, j)),
        out_shape=jax.ShapeDtypeStruct((B, N), F32),
        compiler_params=_cparams(("parallel",), 2 * D * tn * 4 + D * tn * 2),
        name="ada_mod",
    )(c, w_ada, b_ada.reshape(1, N))


def _modulated_norm(x, g, sc, sh):
    ms = jnp.mean(x * x, axis=-1, keepdims=True)
    return (x * lax.rsqrt(ms + NORM_EPS) * g) * (1.0 + sc) + sh


def _inproj_kernel(x_ref, g_ref, sh_ref, sc_ref, w_ref, o_ref, h_scr):
    @pl.when(pl.program_id(1) == 0)
    def _():
        h_scr[...] = _modulated_norm(x_ref[...], g_ref[...], sc_ref[0], sh_ref[0]).astype(BF16)

    o_ref[...] = jnp.dot(h_scr[...], w_ref[...], preferred_element_type=F32)


def _in_proj(x2, g1, mod6, w_in_p, T):
    BT, D = x2.shape
    NP = w_in_p.shape[1]
    tm = min(1024, T)
    tn = 512
    tpb = T // tm
    vm = 2 * tm * D * 4 + tm * D * 2 + 2 * D * tn * 2 + 2 * tm * tn * 4 + 2 * tm * D * 4
    return pl.pallas_call(
        _inproj_kernel,
        grid=(BT // tm, NP // tn),
        in_specs=[
            pl.BlockSpec((tm, D), lambda i, j: (i, 0)),
            pl.BlockSpec((1, D), lambda i, j: (0, 0)),
            pl.BlockSpec((1, 1, D), lambda i, j: ((i // tpb) * 6 + 0, 0, 0)),
            pl.BlockSpec((1, 1, D), lambda i, j: ((i // tpb) * 6 + 1, 0, 0)),
            pl.BlockSpec((D, tn), lambda i, j: (0, j)),
        ],
        out_specs=pl.BlockSpec((tm, tn), lambda i, j: (i, j)),
        out_shape=jax.ShapeDtypeStruct((BT, NP), F32),
        scratch_shapes=[pltpu.VMEM((tm, D), BF16)],
        compiler_params=_cparams(("parallel", "arbitrary"), vm),
        name="in_proj",
    )(x2, g1, mod6, mod6, w_in_p)


def _merge_kernel(oa_ref, ob_ref, wa_ref, wb_ref, ga_ref, gb_ref, o_ref):
    ya = jnp.dot(oa_ref[...], wa_ref[...], preferred_element_type=F32)
    yb = jnp.dot(ob_ref[...], wb_ref[...], preferred_element_type=F32)
    o_ref[...] = (jax.nn.sigmoid(ga_ref[...]) * ya + jax.nn.sigmoid(gb_ref[...]) * yb).astype(BF16)


def _merge(o_a, o_b, w_oa, w_ob, proj):
    BT, W = o_a.shape
    D = w_oa.shape[1]
    tm, tn = 512, 1024
    ga0 = COL_MERGE // tn
    gb0 = (COL_MERGE + D) // tn
    vm = 2 * (2 * tm * W * 2 + 2 * W * tn * 2 + 2 * tm * tn * 4 + tm * tn * 2) + 3 * tm * tn * 4
    return pl.pallas_call(
        _merge_kernel,
        grid=(BT // tm, D // tn),
        in_specs=[
            pl.BlockSpec((tm, W), lambda i, j: (i, 0)),
            pl.BlockSpec((tm, W), lambda i, j: (i, 0)),
            pl.BlockSpec((W, tn), lambda i, j: (0, j)),
            pl.BlockSpec((W, tn), lambda i, j: (0, j)),
            pl.BlockSpec((tm, tn), lambda i, j: (i, ga0 + j)),
            pl.BlockSpec((tm, tn), lambda i, j: (i, gb0 + j)),
        ],
        out_specs=pl.BlockSpec((tm, tn), lambda i, j: (i, j)),
        out_shape=jax.ShapeDtypeStruct((BT, D), BF16),
        compiler_params=_cparams(("parallel", "parallel"), vm),
        name="merge",
    )(o_a, o_b, w_oa, w_ob, proj, proj)


def _outproj_kernel(m_ref, w_ref, x_ref, gt_ref, g_ref, sh_ref, sc_ref, x1_ref, h2_ref):
    y = jnp.dot(m_ref[...], w_ref[...], preferred_element_type=F32)
    x1 = x_ref[...] + gt_ref[0] * y
    x1_ref[...] = x1
    h2_ref[...] = _modulated_norm(x1, g_ref[...], sc_ref[0], sh_ref[0]).astype(BF16)


def _out_proj(mixed, w_out, x2, mod6, g2, T):
    BT, D = x2.shape
    tm = min(512, T)
    tpb = T // tm
    vm = 2 * (tm * D * 2 + D * D * 2 + tm * D * 4 + tm * D * 4 + tm * D * 2) + 3 * tm * D * 4
    mod_spec = lambda k: pl.BlockSpec((1, 1, D), lambda i: ((i // tpb) * 6 + k, 0, 0))
    return pl.pallas_call(
        _outproj_kernel,
        grid=(BT // tm,),
        in_specs=[
            pl.BlockSpec((tm, D), lambda i: (i, 0)),
            pl.BlockSpec((D, D), lambda i: (0, 0)),
            pl.BlockSpec((tm, D), lambda i: (i, 0)),
            mod_spec(2),
            pl.BlockSpec((1, D), lambda i: (0, 0)),
            mod_spec(3),
            mod_spec(4),
        ],
        out_specs=[pl.BlockSpec((tm, D), lambda i: (i, 0)), pl.BlockSpec((tm, D), lambda i: (i, 0))],
        out_shape=[jax.ShapeDtypeStruct((BT, D), F32), jax.ShapeDtypeStruct((BT, D), BF16)],
        compiler_params=_cparams(("parallel",), vm),
        name="out_proj",
    )(mixed, w_out, x2, mod6, g2, mod6, mod6)


def _mlp_kernel(h_ref, wu_ref, wd_ref, x_ref, gt_ref, o_ref, acc_ref):
    f = pl.program_id(1)
    u = jnp.dot(h_ref[...], wu_ref[...], preferred_element_type=F32)
    u = jnp.square(jnp.maximum(u, 0.0)).astype(BF16)
    part = jnp.dot(u, wd_ref[...], preferred_element_type=F32)

    @pl.when(f == 0)
    def _():
        acc_ref[...] = part

    @pl.when(f > 0)
    def _():
        acc_ref[...] += part

    @pl.when(f == pl.num_programs(1) - 1)
    def _():
        o_ref[...] = x_ref[...] + gt_ref[0] * acc_ref[...]


def _mlp(h2, w_up, w_down, x1, mod6, T):
    BT, D = x1.shape
    F = w_up.shape[1]
    tm = min(512, T)
    tf = 1024
    tpb = T // tm
    vm = 2 * (tm * D * 2 + 2 * D * tf * 2 + 2 * tm * D * 4) + tm * D * 4 + 2 * tm * tf * 4
    return pl.pallas_call(
        _mlp_kernel,
        grid=(BT // tm, F // tf),
        in_specs=[
            pl.BlockSpec((tm, D), lambda i, f: (i, 0)),
            pl.BlockSpec((D, tf), lambda i, f: (0, f)),
            pl.BlockSpec((tf, D), lambda i, f: (f, 0)),
            pl.BlockSpec((tm, D), lambda i, f: (i, 0)),
            pl.BlockSpec((1, 1, D), lambda i, f: ((i // tpb) * 6 + 5, 0, 0)),
        ],
        out_specs=pl.BlockSpec((tm, D), lambda i, f: (i, 0)),
        out_shape=jax.ShapeDtypeStruct((BT, D), F32),
        scratch_shapes=[pltpu.VMEM((tm, D), F32)],
        compiler_params=_cparams(("parallel", "arbitrary"), vm),
        name="mlp",
    )(h2, w_up, w_down, x1, mod6)


def _dotb(a, b):
    return jnp.dot(a.astype(BF16), b.astype(BF16), preferred_element_type=F32)


def _dotb_nt(a, b):
    return lax.dot_general(a.astype(BF16), b.astype(BF16), (((1,), (1,)), ((), ())),
                           preferred_element_type=F32)


def _split_bf16(x, terms):
    parts, rem = [], x
    for t in range(terms):
        p = rem.astype(BF16)
        parts.append(p)
        if t + 1 < terms:
            rem = rem - p.astype(F32)
    return parts


def _dot_sel(x, sel, terms):
    sel = sel.astype(BF16)
    return sum(jnp.dot(p, sel, preferred_element_type=F32) for p in _split_bf16(x, terms))


def _sel_dot(sel, x, terms):
    sel = sel.astype(BF16)
    return sum(jnp.dot(sel, p, preferred_element_type=F32) for p in _split_bf16(x, terms))


def _dot3(a, b_hi, b_lo):
    a_hi, a_lo = _split_bf16(a, 2)
    return (jnp.dot(a_hi, b_hi, preferred_element_type=F32) + jnp.dot(a_lo, b_hi, preferred_element_type=F32)
            + jnp.dot(a_hi, b_lo, preferred_element_type=F32))


def _iota(shape, axis):
    return lax.broadcasted_iota(jnp.int32, shape, axis)


def _same_head_mask(n):
    return (_iota((n, n), 0) >> 6) == (_iota((n, n), 1) >> 6)


RW_TC = 256
RW_C = 64
RW_LANES = 4 * HEAD_DIM


def _block_diag(x, bdmask):
    return jnp.where(bdmask, jnp.concatenate([x, x, x, x], axis=0), 0.0)


def _rwkv_kernel(r_ref, k_ref, v_ref, s_ref, mur_ref, muk_ref, muv_ref, mus_ref,
                 w0_ref, a0_ref, kk_ref, ka_ref, rk_ref, lnw_ref, lnb_ref,
                 w2h_ref, w2l_ref, a2h_ref, a2l_ref, g2h_ref, g2l_ref, o_ref,
                 pr_scr, pk_scr, pv_scr, ps_scr, state_scr):
    TC = r_ref.shape[0]
    C = RW_C
    W = RW_LANES

    @pl.when(pl.program_id(1) == 0)
    def _():
        pr_scr[...] = jnp.zeros_like(pr_scr)
        pk_scr[...] = jnp.zeros_like(pk_scr)
        pv_scr[...] = jnp.zeros_like(pv_scr)
        ps_scr[...] = jnp.zeros_like(ps_scr)
        state_scr[...] = jnp.zeros_like(state_scr)

    def shift_mix(p_ref, prev_scr, mu_ref):
        p = p_ref[...]
        row = _iota(p.shape, 0)
        shifted = jnp.where(row == 0, prev_scr[0:1, :], pltpu.roll(p, 1, 0))
        prev_scr[0:1, :] = p[TC - 1:TC, :]
        return p + (shifted - p) * mu_ref[...]

    CW = r_ref.shape[1]
    G = CW // W
    NQ = TC // C
    groups = lambda x: [x[:, g * W:(g + 1) * W] for g in range(G)]
    per_group = lambda f, x: jnp.concatenate([f(xg) for xg in groups(x)], axis=1)

    r = shift_mix(r_ref, pr_scr, mur_ref)
    k = shift_mix(k_ref, pk_scr, muk_ref)
    v = shift_mix(v_ref, pv_scr, muv_ref)
    sm = shift_mix(s_ref, ps_scr, mus_ref)
    xwa = sm[:, 0:SM_XG]

    bdmask = _same_head_mask(W)
    bones = jnp.where(bdmask, 1.0, 0.0)
    head_sum = lambda x: per_group(lambda xg: _dot_sel(xg, bones, 2), x)
    tri = jnp.where(_same_head_mask(TC) & (_iota((TC, TC), 1) <= _iota((TC, TC), 0)), 1.0, 0.0)
    lane = _iota((C, W), 1)
    row = _iota((C, W), 0)
    lanehead = lane >> 6
    strict = (lane & 63) < row
    incl = (lane & 63) <= row
    eye_cat = jnp.where((lane & 63) == row, 1.0, 0.0)

    wlin = w0_ref[...] + _dot3(jnp.tanh(xwa), w2h_ref[...], w2l_ref[...])
    a = jax.nn.sigmoid(a0_ref[...] + _dot3(xwa, a2h_ref[...], a2l_ref[...]))
    gate = _dot3(jax.nn.sigmoid(sm[:, SM_XG:SM_XG + 256]), g2h_ref[...], g2l_ref[...])
    z = -wlin
    softplus = jnp.maximum(z, 0.0) + jnp.log(1.0 + jnp.exp(-jnp.abs(z)))
    ld = -jnp.exp(-softplus - 0.5)
    cum = _sel_dot(tri, ld, 3)
    cum_last = jnp.concatenate(
        [jnp.broadcast_to(cum[(q + 1) * C - 1:(q + 1) * C, :], (C, CW)) for q in range(NQ)], axis=0)
    kk = k * kk_ref[...]
    kk = kk / jnp.maximum(jnp.sqrt(head_sum(kk * kk)), 1e-12)
    k2 = k * (1.0 + (a - 1.0) * ka_ref[...])
    bvec = kk * a
    e_inv = jnp.exp(-cum)
    e_end = jnp.exp(cum_last - cum)
    At_f = -kk * jnp.exp(cum - ld)
    Rt_f = r * jnp.exp(cum)
    Bt_f = bvec * e_inv
    Kt_f = k2 * e_inv
    Bg_f = bvec * e_end
    Kg_f = k2 * e_end
    g_end = jnp.exp(cum_last)

    items = [(g, q) for q in range(NQ) for g in range(G)]
    blk = lambda x, g, q: x[q * C:(q + 1) * C, g * W:(g + 1) * W]
    bd = lambda x: _block_diag(x, bdmask)
    At = {it: blk(At_f, *it) for it in items}
    Rt = {it: blk(Rt_f, *it) for it in items}
    Vq = {it: blk(v, *it) for it in items}
    AA = {}
    for it in items:
        bk = jnp.concatenate([jnp.where(lanehead == h, X, 0.0)
                              for X in (blk(Bt_f, *it), blk(Kt_f, *it)) for h in range(4)], axis=0)
        AA[it] = _dotb_nt(jnp.concatenate([At[it], Rt[it]], axis=0), bk)
    A_ab = {it: jnp.where(strict, AA[it][0:C, 0:W], 0.0) for it in items}
    A_ak = {it: jnp.where(strict, AA[it][0:C, W:2 * W], 0.0) for it in items}
    A_rb = {it: jnp.where(incl, AA[it][C:2 * C, 0:W], 0.0) for it in items}
    A_rk = {it: jnp.where(incl, AA[it][C:2 * C, W:2 * W], 0.0) for it in items}
    M = dict(A_ab)
    Tm = {it: eye_cat + A_ab[it] for it in items}
    for _ in range(5):
        M = {it: _dotb(M[it], bd(M[it])) for it in items}
        Tm = {it: Tm[it] + _dotb(M[it], bd(Tm[it])) for it in items}
    Vbd = {it: bd(Vq[it]) for it in items}
    akv = {it: _dotb(A_ak[it], Vbd[it]) for it in items}
    rkv = {it: _dotb(A_rk[it], Vbd[it]) for it in items}

    S = [state_scr[g] for g in range(G)]
    ys = []
    for q in range(NQ):
        its = [(g, q) for g in range(G)]
        rhs = [_dotb_nt(At[it], S[it[0]]) + akv[it] for it in its]
        U = [_dotb(Tm[it], bd(rhs[g])) for g, it in enumerate(its)]
        ys.append(jnp.concatenate(
            [_dotb_nt(Rt[it], S[g]) + _dotb(A_rb[it], bd(U[g])) + rkv[it] for g, it in enumerate(its)], axis=1))
        upd = [_dotb(jnp.concatenate([U[g], Vq[it]], axis=0).T,
                     jnp.concatenate([blk(Bg_f, *it), blk(Kg_f, *it)], axis=0)) for g, it in enumerate(its)]
        S = [S[g] * blk(g_end, g, q)[0:1, :] + jnp.where(bdmask, upd[g], 0.0) for g in range(G)]
    for g in range(G):
        state_scr[g] = S[g]
    y = jnp.concatenate(ys, axis=0)

    inv_n = 1.0 / HEAD_DIM
    d = y - head_sum(y) * inv_n
    var = head_sum(d * d) * inv_n
    yn = d * lax.rsqrt(var + RWKV_GN_EPS) * lnw_ref[...] + lnb_ref[...]
    bonus = head_sum(r * k2 * rk_ref[...]) * v
    o_ref[...] = ((yn + bonus) * gate).astype(BF16)


def _rwkv_mix(proj, pr, B, T):
    BT = proj.shape[0]
    TC = min(RW_TC, T)
    W = RW_LANES
    CW = RWKV_WIDTH
    nct = T // TC
    row = lambda b, c: b * nct + c
    full = lambda shape: pl.BlockSpec(shape, lambda b, c: (0, 0))
    vec = full((1, CW))
    in_specs = [
        pl.BlockSpec((TC, CW), lambda b, c: (row(b, c), 0)),
        pl.BlockSpec((TC, CW), lambda b, c: (row(b, c), 1)),
        pl.BlockSpec((TC, CW), lambda b, c: (row(b, c), 2)),
        pl.BlockSpec((TC, SMALL_W), lambda b, c: (row(b, c), COL_SMALL // SMALL_W)),
        vec, vec, vec,
        full((1, SMALL_W)),
        vec, vec, vec, vec, vec, vec, vec,
        full((SM_XG, CW)), full((SM_XG, CW)), full((SM_XG, CW)), full((SM_XG, CW)),
        full((256, CW)), full((256, CW)),
    ]
    vm = (2 * (3 * TC * CW * 4 + TC * SMALL_W * 4 + TC * CW * 2 + (4 * SM_XG + 2 * 256) * CW * 2)
          + 24 * TC * CW * 4)
    return pl.pallas_call(
        _rwkv_kernel,
        grid=(B, nct),
        in_specs=in_specs,
        out_specs=pl.BlockSpec((TC, CW), lambda b, c: (row(b, c), 0)),
        out_shape=jax.ShapeDtypeStruct((BT, CW), BF16),
        scratch_shapes=[pltpu.VMEM((8, CW), F32), pltpu.VMEM((8, CW), F32), pltpu.VMEM((8, CW), F32),
                        pltpu.VMEM((8, SMALL_W), F32), pltpu.VMEM((CW // W, W, W), F32)],
        compiler_params=_cparams(("parallel", "arbitrary"), vm),
        name="rwkv_mix",
    )(proj, proj, proj, proj, pr["mu_r"], pr["mu_k"], pr["mu_v"], pr["mu_s"],
      pr["w0"], pr["a0"], pr["k_k"], pr["k_a"], pr["r_k"], pr["ln_w"], pr["ln_b"],
      pr["w2h"], pr["w2l"], pr["a2h"], pr["a2l"], pr["g2h"], pr["g2l"])


NSA_TT = 256
PADFLAG_LANE = 32


def _nsa_prep_kernel(q_ref, ks_ref, kw_ref, s_ref, qg_ref, kgs_ref, kgw_ref, e_ref, gsel_ref,
                     qp_ref, ksl_ref, vsl_ref, kwl_ref, vwl_ref, gt_ref):
    q = q_ref[...]
    bones = jnp.where(_same_head_mask(q.shape[1]), 1.0, 0.0)
    ms = _dot_sel(q * q, bones, 2) * (1.0 / HEAD_DIM)
    qn = (q * lax.rsqrt(ms + NORM_EPS) * qg_ref[...]) * (HEAD_DIM ** -0.5)
    qs = jnp.dot(qn.astype(BF16), e_ref[...], preferred_element_type=F32).astype(BF16)
    for h in range(NSA_GROUP):
        qp_ref[0, h] = qs[:, h * LANES:(h + 1) * LANES]

    tt = q.shape[0]
    bones2 = jnp.where(_same_head_mask(LANES), 1.0, 0.0)
    lane = _iota((tt, LANES), 1)
    block_id = (pl.program_id(1) * tt + _iota((tt, LANES), 0)) >> 6
    onehot = jnp.where(lane == block_id, 1.0, 0.0)

    def slabs(x, gain, k_ref, v_ref):
        ms = _dot_sel(x * x, bones2, 2) * (1.0 / HEAD_DIM)
        k_ref[0, 0] = jnp.where(lane >= HEAD_DIM, x * lax.rsqrt(ms + NORM_EPS) * gain, onehot).astype(BF16)
        v_ref[0, 0] = jnp.where(lane < HEAD_DIM, x, 1.0).astype(BF16)

    slabs(ks_ref[...], kgs_ref[...], ksl_ref, vsl_ref)
    slabs(kw_ref[...], kgw_ref[...], kwl_ref, vwl_ref)
    gt_ref[0, 0] = jax.nn.sigmoid(_dot_sel(s_ref[...], gsel_ref[0], 3))


def _nsa_prep(proj, pn, B, T):
    tt = min(NSA_TT, T)
    ntt = T // tt
    G = NSA_KV_HEADS
    QW = NSA_GROUP * HEAD_DIM
    kv_blk = COL_KV // LANES
    row = lambda b, t, g: b * ntt + t
    vm = 2 * (tt * QW * 4 + 2 * tt * LANES * 4 + tt * SMALL_W * 4 + QW * 4 * LANES * 2 + SMALL_W * LANES * 4
              + 4 * tt * LANES * 2 + 2 * tt * LANES * 2 + tt * LANES * 4) + 12 * tt * QW * 4
    return pl.pallas_call(
        _nsa_prep_kernel,
        grid=(B, ntt, G),
        in_specs=[
            pl.BlockSpec((tt, QW), lambda b, t, g: (row(b, t, g), COL_Q // QW + g)),
            pl.BlockSpec((tt, LANES), lambda b, t, g: (row(b, t, g), kv_blk + 3 * g + 1)),
            pl.BlockSpec((tt, LANES), lambda b, t, g: (row(b, t, g), kv_blk + 3 * g + 2)),
            pl.BlockSpec((tt, SMALL_W), lambda b, t, g: (row(b, t, g), COL_SMALL // SMALL_W)),
            pl.BlockSpec((1, QW), lambda b, t, g: (0, 0)),
            pl.BlockSpec((1, LANES), lambda b, t, g: (0, 0)),
            pl.BlockSpec((1, LANES), lambda b, t, g: (0, 0)),
            pl.BlockSpec((QW, NSA_GROUP * LANES), lambda b, t, g: (0, 0)),
            pl.BlockSpec((1, SMALL_W, LANES), lambda b, t, g: (g, 0, 0)),
        ],
        out_specs=[pl.BlockSpec((1, NSA_GROUP, tt, LANES), lambda b, t, g: (b, g, t, 0))]
        + [pl.BlockSpec((1, 1, tt, LANES), lambda b, t, g: (b, g, t, 0))] * 5,
        out_shape=[jax.ShapeDtypeStruct((B, NSA_HEADS, T, LANES), BF16)]
        + [jax.ShapeDtypeStruct((B, G, T, LANES), BF16)] * 4
        + [jax.ShapeDtypeStruct((B, G, T, LANES), F32)],
        compiler_params=_cparams(("parallel", "parallel", "parallel"), vm),
        name="nsa_prep",
    )(proj, proj, proj, proj, pn["q_g"], pn["kg_sel"], pn["kg_win"], pn["q_spread"], pn["gate_sel"])


def _gelu_tanh(x):
    return 0.5 * x * (1.0 + jnp.tanh(math.sqrt(2.0 / math.pi) * (x + 0.044715 * (x * x * x))))


def _nsa_compress_kernel(x_ref, pea_ref, peb_ref, wa_ref, wb_ref, w2_ref, kg_ref, kc_ref, vc_ref):
    nsub = x_ref.shape[0] // CMP_STRIDE
    xs = jnp.concatenate([x_ref[pl.ds(s, nsub, stride=CMP_STRIDE), :] for s in range(CMP_STRIDE)], axis=1)
    p0 = jnp.dot((xs + pea_ref[...]).astype(BF16), wa_ref[...], preferred_element_type=F32)
    p1 = jnp.dot((xs + peb_ref[...]).astype(BF16), wb_ref[...], preferred_element_type=F32)
    hid = _gelu_tanh(p0 + pltpu.roll(p1, nsub - 1, 0))
    out = jnp.dot(hid.astype(BF16), w2_ref[...], preferred_element_type=F32)
    bones2 = jnp.where(_same_head_mask(LANES), 1.0, 0.0)
    ms = _dot_sel(out * out, bones2, 2) * (1.0 / HEAD_DIM)
    is_k = _iota(out.shape, 1) >= HEAD_DIM
    kc_ref[0, 0] = jnp.where(is_k, out * lax.rsqrt(ms + NORM_EPS) * kg_ref[...], 0.0).astype(BF16)
    vc_ref[0, 0] = jnp.where(is_k, 0.0, out).astype(BF16)


def _nsa_compress(proj, pn, B, T):
    G = NSA_KV_HEADS
    nsub = T // CMP_STRIDE
    kv_blk = COL_KV // LANES
    KW = CMP_STRIDE * LANES
    full = lambda shape: pl.BlockSpec(shape, lambda b, g: tuple(0 for _ in shape))
    vm = 2 * (T * LANES * 4 + 2 * KW * LANES * 2 + nsub * LANES * 2) + 6 * nsub * KW * 4
    return pl.pallas_call(
        _nsa_compress_kernel,
        grid=(B, G),
        in_specs=[
            pl.BlockSpec((T, LANES), lambda b, g: (b, kv_blk + 3 * g)),
            full((1, KW)), full((1, KW)), full((KW, LANES)), full((KW, LANES)), full((LANES, LANES)),
            full((1, LANES)),
        ],
        out_specs=[pl.BlockSpec((1, 1, nsub, LANES), lambda b, g: (b, g, 0, 0))] * 2,
        out_shape=[jax.ShapeDtypeStruct((B, G, nsub, LANES), BF16)] * 2,
        compiler_params=_cparams(("parallel", "parallel"), vm),
        name="nsa_compress",
    )(proj, pn["pe_a"], pn["pe_b"], pn["cw_a"], pn["cw_b"], pn["cw2"], pn["kg_cmp"])


NSA_GROUPS_PER_STEP = 2
NSA_KPAD = WINDOW
SEL_CHUNK = 4 * QUERY_BLOCK
WIN_KEYS = WINDOW + QUERY_BLOCK


def _lane_tile_max(s):
    tiles = [s[:, j * LANES:(j + 1) * LANES] for j in range(s.shape[1] // LANES)]
    while len(tiles) > 1:
        tiles = [jnp.maximum(a, b) for a, b in zip(tiles[0::2], tiles[1::2])] + ([tiles[-1]] if len(tiles) % 2 else [])
    return tiles[0]


def _nsa_attn_kernel(q_ref, kc_ref, vc_ref, ks_ref, vs_ref, kw_ref, vw_ref, gt_ref, stab_ref, wtab_ref, ctab_ref,
                     selmt_ref, gather_ref, grep_ref, o_ref, sbuf, mx_scr, acc_scr, sc_scr, *, n_sel):
    i = pl.program_id(2)
    QB = QUERY_BLOCK
    HG = NSA_GROUP
    R = HG * QB
    t0 = i * QB
    groups = range(kc_ref.shape[1])
    each = lambda f: [f(g) for g in groups]
    tile4 = lambda z: jnp.concatenate([z, z, z, z], axis=0)
    heads = lambda ref, g: ref[HG * g:HG * (g + 1)]
    lane = _iota((QB, LANES), 1)
    pad_mask = jnp.where(lane == PADFLAG_LANE, NEG_INF, 0.0)
    q = each(lambda g: q_ref[0, HG * g:HG * (g + 1)].reshape(R, LANES))
    q32 = each(lambda g: q[g].astype(F32))

    ncp = kc_ref.shape[2]
    s = each(lambda g: _dotb_nt(q[g], kc_ref[0, g]) + heads(ctab_ref, g)[:, 0].reshape(R, ncp))
    m = each(lambda g: jnp.max(s[g], axis=-1, keepdims=True))
    p_c = each(lambda g: jnp.exp(s[g] - m[g]))
    lsum = each(lambda g: jnp.sum(p_c[g], axis=-1, keepdims=True))
    p_c = each(lambda g: p_c[g] * jnp.where(m[g] > 0.5 * NEG_INF, 1.0 / lsum[g], 0.0))
    o_c = each(lambda g: jnp.dot(p_c[g].astype(BF16), vc_ref[0, g], preferred_element_type=F32))

    wrows = pl.ds(pl.multiple_of(t0, QB), WIN_KEYS)
    q_win = each(lambda g: (q32[g] + tile4(pad_mask)).astype(BF16))
    s = each(lambda g: _dotb_nt(q_win[g], kw_ref[0, g, wrows, :]) + heads(wtab_ref, g).reshape(R, WIN_KEYS))
    p = each(lambda g: jnp.exp(s[g] - jnp.max(_lane_tile_max(s[g]), axis=-1, keepdims=True)))
    acc_w = each(lambda g: jnp.dot(p[g].astype(BF16), vw_ref[0, g, wrows, :], preferred_element_type=F32))

    psum = each(lambda g: p_c[g][0:QB] + p_c[g][QB:2 * QB] + p_c[g][2 * QB:3 * QB] + p_c[g][3 * QB:4 * QB])
    selmt = selmt_ref[...]
    parts = each(lambda g: _split_bf16(psum[g], 3))
    imp = each(lambda g: sum(lax.dot_general(selmt, part, (((1,), (1,)), ((), ())), preferred_element_type=F32)
                             for part in parts[g]))
    ns = selmt.shape[0]
    blk = _iota((ns, QB), 0)
    cur = (t0 + _iota((ns, QB), 1)) >> 6
    forced = (blk == 0) | (blk == cur) | (blk == cur - 1)
    score = each(lambda g: jnp.where(forced, FORCE_SCORE, jnp.where(blk <= cur, imp[g], -1.0)))
    for g in groups:
        sc_scr[g] = score[g]
    ranks = [[] for _ in groups]
    for j in range(ns):
        lower = jnp.where(blk > j, 1.0, 0.0)
        for g in groups:
            other = sc_scr[g, j:j + 1, :]
            ranks[g].append(jnp.where(other > score[g], 1.0, 0.0) + jnp.where(other == score[g], lower, 0.0))
    while len(ranks[0]) > 1:
        ranks = [[a + b for a, b in zip(r[0::2], r[1::2])] + ([r[-1]] if len(r) % 2 else []) for r in ranks]
    chosen_t = each(lambda g: jnp.where(ranks[g][0] < n_sel, 1.0, 0.0))
    chosen = each(lambda g: jnp.concatenate([chosen_t[g], jnp.zeros((LANES - ns, QB), F32)], axis=0).T)
    q_sel = each(lambda g: (q32[g] + tile4(jnp.where(lane < ns, (chosen[g] - 1.0) * (-NEG_INF), pad_mask))
                            ).astype(BF16))

    def chunk_rows(c):
        return pl.ds(pl.multiple_of((i - 4 * c + 1) * QB, QB), SEL_CHUNK)

    s0 = each(lambda g: _dotb_nt(q_sel[g], ks_ref[0, g, chunk_rows(0), :]) + heads(stab_ref, g).reshape(R, SEL_CHUNK))
    for g in groups:
        sbuf[g, 0] = s0[g]
        mx_scr[g] = _lane_tile_max(s0[g])

    def scores_body(c, carry):
        s = each(lambda g: _dotb_nt(q_sel[g], ks_ref[0, g, chunk_rows(c), :]))
        for g in groups:
            sbuf[g, c] = s[g]
            mx_scr[g] = jnp.maximum(mx_scr[g], _lane_tile_max(s[g]))
        return carry

    n_far = i // 4
    lax.fori_loop(1, n_far + 1, scores_body, 0)
    m_s = each(lambda g: jnp.max(mx_scr[g], axis=-1, keepdims=True))

    acc_scr[...] = jnp.zeros_like(acc_scr)

    def values_body(c, carry):
        p = each(lambda g: jnp.exp(sbuf[g, c] - m_s[g]).astype(BF16))
        for g in groups:
            acc_scr[g] += jnp.dot(p[g], vs_ref[0, g, chunk_rows(c), :], preferred_element_type=F32)
        return carry

    lax.fori_loop(0, n_far + 1, values_body, 0)

    low = _iota((R, LANES), 1) < HEAD_DIM
    normalised = lambda acc: jnp.where(low, acc * (1.0 / pltpu.roll(acc, HEAD_DIM, 1)), 0.0)

    def natural(o):
        cat = jnp.concatenate([o[h * QB:(h + 1) * QB] for h in range(HG)], axis=1).astype(BF16)
        return jnp.dot(cat, gather_ref[...], preferred_element_type=F32)

    branches = each(lambda g: [o_c[g], normalised(acc_scr[g]), normalised(acc_w[g])])
    gate = each(lambda g: [_dot_sel(gt_ref[0, g], grep_ref[c], 2) for c in range(3)])
    W = HG * HEAD_DIM
    for g in groups:
        out = sum(gate[g][c] * natural(branches[g][c]) for c in range(3))
        o_ref[:, g * W:(g + 1) * W] = out.astype(BF16)


def _nsa_attn(qp, kc, vc, ks, vs, kw, vw, gates, pn, B, T):
    G = NSA_KV_HEADS
    QB = QUERY_BLOCK
    NQ = T // QB
    ncp = kc.shape[2]
    ns = T // SEL_BLOCK
    TP = T + NSA_KPAD
    R = NSA_GROUP * QB
    n_sel = min(N_SEL, ns)
    n_chunks = (NQ - 1) // 4 + 1
    NG = NSA_GROUPS_PER_STEP
    HB = NSA_GROUP * NG
    gtab = lambda w: pl.BlockSpec((HB, QB, w), lambda b, g, i: (g, 0, 0))
    slab = lambda rows: pl.BlockSpec((1, NG, rows, LANES), lambda b, g, i: (b, g, 0, 0))
    vm = (2 * NG * (R * LANES * 2 + 2 * ncp * LANES * 2 + 4 * TP * LANES * 2 + QB * LANES * 4
                    + R * (SEL_CHUNK + WIN_KEYS + ncp) * 4 + QB * NSA_GROUP * HEAD_DIM * 2)
          + NG * (n_chunks * R * SEL_CHUNK * 4 + 2 * R * LANES * 4 + 6 * R * WIN_KEYS * 4))
    return pl.pallas_call(
        functools.partial(_nsa_attn_kernel, n_sel=n_sel),
        grid=(B, G // NG, NQ),
        in_specs=[
            pl.BlockSpec((1, HB, QB, LANES), lambda b, g, i: (b, g, i, 0)),
            slab(ncp), slab(ncp), slab(TP), slab(TP), slab(TP), slab(TP),
            pl.BlockSpec((1, NG, QB, LANES), lambda b, g, i: (b, g, i, 0)),
            gtab(SEL_CHUNK), gtab(WIN_KEYS),
            pl.BlockSpec((HB, 1, QB, ncp), lambda b, g, i: (g, i, 0, 0)),
            pl.BlockSpec((ns, ncp), lambda b, g, i: (0, 0)),
            pl.BlockSpec((NSA_GROUP * LANES, NSA_GROUP * HEAD_DIM), lambda b, g, i: (0, 0)),
            pl.BlockSpec((3, LANES, NSA_GROUP * HEAD_DIM), lambda b, g, i: (0, 0, 0)),
        ],
        out_specs=pl.BlockSpec((QB, HB * HEAD_DIM), lambda b, g, i: (b * NQ + i, g)),
        out_shape=jax.ShapeDtypeStruct((B * T, NSA_WIDTH), BF16),
        scratch_shapes=[pltpu.VMEM((NG, n_chunks, R, SEL_CHUNK), F32), pltpu.VMEM((NG, R, LANES), F32),
                        pltpu.VMEM((NG, R, LANES), F32), pltpu.VMEM((NG, ns, QB), F32)],
        compiler_params=_cparams(("parallel", "parallel", "arbitrary"), vm),
        name="nsa_attn",
    )(qp, kc, vc, ks, vs, kw, vw, gates, pn["stab"], pn["wtab"], pn["ctab"], pn["sel_mt"], pn["gather"],
      pn["gate_rep"])


def _rel_bucket_table():
    n = np.arange(REL_MAX_DIST + 1)
    max_exact = REL_BUCKETS // 2
    nf = np.maximum(n, max_exact).astype(np.float32)
    large = max_exact + (np.log(nf / np.float32(max_exact)) / np.float32(math.log(REL_MAX_DIST / max_exact))
                         * np.float32(REL_BUCKETS - max_exact)).astype(np.int32)
    large = np.minimum(large, REL_BUCKETS - 1)
    return np.where(n < max_exact, n, large).astype(np.int32)


def _sel_to_cmp_matrix(T, ncp):
    nc = T // CMP_STRIDE - CMP_BLOCK // CMP_STRIDE + 1
    ns = T // SEL_BLOCK
    cs = np.arange(nc) * CMP_STRIDE
    ss = np.arange(ns) * SEL_BLOCK
    lo = np.maximum(cs[None, :], ss[:, None])
    hi = np.minimum(cs[None, :] + CMP_BLOCK, ss[:, None] + SEL_BLOCK)
    out = np.zeros((ns, ncp), np.float32)
    out[:, :nc] = np.maximum(hi - lo, 0) / CMP_BLOCK
    return out


def _bias_tables(rel_bias, rel):
    bucket = _rel_bucket_table()[np.clip(rel, 0, REL_MAX_DIST)]
    onehot = (jnp.asarray(bucket.reshape(1, -1)) == jnp.arange(REL_BUCKETS, dtype=jnp.int32)[:, None]).astype(F32)
    tab = jnp.einsum('bh,bn->hn', rel_bias, onehot, precision=HIGHEST)
    return tab.reshape((rel_bias.shape[1],) + rel.shape)


def _prep_in_proj_weight(w_in):
    D = w_in.shape[0]
    nsa0 = RWKV_COLS
    kv0 = nsa0 + NSA_WIDTH
    gates0 = kv0 + 6 * NSA_KV_WIDTH
    merge0 = RWKV_COLS + NSA_COLS
    cols = [(0, 3 * RWKV_WIDTH), (nsa0, NSA_WIDTH), (merge0, 2 * D_MODEL)]
    for g in range(NSA_KV_HEADS):
        for branch in range(3):
            for kv in (1, 0):
                cols.append((kv0 + (2 * branch + kv) * NSA_KV_WIDTH + g * HEAD_DIM, HEAD_DIM))
    cols += [(3 * RWKV_WIDTH, RWKV_COLS - 3 * RWKV_WIDTH), (gates0, 3 * NSA_HEADS)]
    pad = jnp.zeros((D, SMALL_W - (SM_GATES + 3 * NSA_HEADS)), w_in.dtype)
    return jnp.concatenate([w_in[:, s:s + n] for s, n in cols] + [pad], axis=1).astype(BF16)


def _prep_rwkv_params(mu, w0, w2, a0, a2, g2, k_k, k_a, r_k, ln_w, ln_b):
    C = RWKV_WIDTH
    row = lambda z: z.reshape(1, -1).astype(F32)
    mu_s = jnp.concatenate([mu[3 * C:], jnp.zeros((SMALL_W - (RWKV_COLS - 3 * C),), F32)]).reshape(1, SMALL_W)
    zl = jnp.zeros((DECAY_LORA, C), F32)

    def hi_lo(w):
        hi = w.astype(BF16)
        return hi, (w - hi.astype(F32)).astype(BF16)

    w2h, w2l = hi_lo(jnp.concatenate([w2, zl], axis=0))
    a2h, a2l = hi_lo(jnp.concatenate([zl, a2], axis=0))
    g2h, g2l = hi_lo(jnp.concatenate([g2, jnp.zeros((256 - GATE_LORA, C), F32)], axis=0))
    return dict(
        mu_r=row(mu[0:C]), mu_k=row(mu[C:2 * C]), mu_v=row(mu[2 * C:3 * C]), mu_s=mu_s,
        w0=row(w0), a0=row(a0), k_k=row(k_k), k_a=row(k_a), r_k=row(r_k), ln_w=row(ln_w), ln_b=row(ln_b),
        w2h=w2h, w2l=w2l, a2h=a2h, a2l=a2l, g2h=g2h, g2l=g2l,
    )


def _prep_nsa_params(pe_k, w1_k, w2_k, pe_v, w1_v, w2_v, q_g, k_g, rel_bias, T):
    hd = HEAD_DIM
    ones = jnp.ones((hd,), F32)
    ncp = T // CMP_STRIDE
    NQ = T // QUERY_BLOCK

    def blockdiag(a, b):
        lead = ((0, 0),) * (a.ndim - 2)
        return jnp.pad(a, lead + ((0, hd), (0, hd))) + jnp.pad(b, lead + ((hd, 0), (hd, 0)))

    w1 = blockdiag(w1_v.reshape(CMP_BLOCK, hd, hd), w1_k.reshape(CMP_BLOCK, hd, hd))
    pe = jnp.concatenate([pe_v, pe_k], axis=1)
    half = CMP_STRIDE

    spread = np.zeros((NSA_GROUP * hd, NSA_GROUP * LANES), np.float32)
    gather = np.zeros((NSA_GROUP * LANES, NSA_GROUP * hd), np.float32)
    for h in range(NSA_GROUP):
        for d in range(hd):
            spread[h * hd + d, h * LANES + hd + d] = 1.0
            gather[h * LANES + d, h * hd + d] = 1.0
    gate_sel = np.zeros((NSA_KV_HEADS, SMALL_W, LANES), np.float32)
    for g in range(NSA_KV_HEADS):
        for j in range(3 * NSA_GROUP):
            gate_sel[g, SM_GATES + 3 * NSA_GROUP * g + j, j] = 1.0
    gate_rep = np.zeros((3, LANES, NSA_GROUP * hd), np.float32)
    for c in range(3):
        for h in range(NSA_GROUP):
            gate_rep[c, 3 * h + c, h * hd:(h + 1) * hd] = 1.0

    qi = np.arange(QUERY_BLOCK)[:, None]
    rel_s = qi + (SEL_CHUNK - QUERY_BLOCK) - np.arange(SEL_CHUNK)[None, :]
    rel_w = qi + WINDOW - np.arange(WIN_KEYS)[None, :]
    rel_c = ((np.arange(NQ)[:, None, None] * QUERY_BLOCK + qi[None])
             - (np.arange(ncp)[None, None, :] * CMP_STRIDE + CMP_BLOCK - 1))
    far = np.full((QUERY_BLOCK, 1), REL_MAX_DIST)
    tabs = _bias_tables(rel_bias, np.concatenate([rel_s, rel_w, far], axis=1))
    o1, o2 = SEL_CHUNK, SEL_CHUNK + WIN_KEYS
    masked = lambda tab, ok: jnp.where(jnp.asarray(ok)[None], tab, NEG_INF)
    stab = masked(tabs[:, :, 0:o1] - tabs[:, :, o2:], rel_s >= 0)
    wtab = masked(tabs[:, :, o1:o2], (rel_w >= 0) & (rel_w < WINDOW))
    ctab = masked(_bias_tables(rel_bias, rel_c), rel_c >= 0)
    return dict(
        stab=stab, wtab=wtab, ctab=ctab,
        q_g=jnp.tile(q_g, NSA_GROUP).reshape(1, -1),
        kg_cmp=jnp.concatenate([ones, k_g[0]]).reshape(1, LANES),
        kg_sel=jnp.concatenate([ones, k_g[1]]).reshape(1, LANES),
        kg_win=jnp.concatenate([ones, k_g[2]]).reshape(1, LANES),
        pe_a=pe[:half].reshape(1, half * LANES), pe_b=pe[half:].reshape(1, half * LANES),
        cw_a=w1[:half].reshape(half * LANES, LANES).astype(BF16),
        cw_b=w1[half:].reshape(half * LANES, LANES).astype(BF16),
        cw2=blockdiag(w2_v, w2_k).astype(BF16),
        q_spread=jnp.asarray(spread, BF16), gather=jnp.asarray(gather, BF16),
        gate_sel=jnp.asarray(gate_sel), gate_rep=jnp.asarray(gate_rep, BF16),
        sel_mt=jnp.asarray(_sel_to_cmp_matrix(T, ncp)),
    )


def _front_pad(slab, flagged):
    B, G, _, L = slab.shape
    row = jnp.zeros((L,), slab.dtype)
    if flagged:
        row = row.at[PADFLAG_LANE].set(1.0)
    return jnp.concatenate([jnp.broadcast_to(row, (B, G, NSA_KPAD, L)), slab], axis=2)


def kernel(x, c, w_ada, b_ada, norm1_g, norm2_g, w_in, rwkv_mu, rwkv_w0, rwkv_w2, rwkv_a0, rwkv_a2, rwkv_g2, rwkv_k_k, rwkv_k_a, rwkv_r_k, rwkv_ln_w, rwkv_ln_b, cmp_pe_k, cmp_w1_k, cmp_w2_k, cmp_pe_v, cmp_w1_v, cmp_w2_v, q_norm_g, k_norm_g, rel_bias, w_o_rwkv, w_o_nsa, w_out, w_up, w_down):
    B, T, D = x.shape
    depth = w_in.shape[0]
    x2 = x.reshape(B * T, D)
    for l in range(depth):
        mod6 = _ada_mod(c, w_ada[l], b_ada[l]).reshape(B * 6, 1, D)
        proj = _in_proj(x2, norm1_g[l].reshape(1, D), mod6, _prep_in_proj_weight(w_in[l]), T)
        pr = _prep_rwkv_params(rwkv_mu[l], rwkv_w0[l], rwkv_w2[l], rwkv_a0[l], rwkv_a2[l], rwkv_g2[l],
                               rwkv_k_k[l], rwkv_k_a[l], rwkv_r_k[l], rwkv_ln_w[l], rwkv_ln_b[l])
        o_a = _rwkv_mix(proj, pr, B, T)
        pn = _prep_nsa_params(cmp_pe_k[l], cmp_w1_k[l], cmp_w2_k[l], cmp_pe_v[l], cmp_w1_v[l], cmp_w2_v[l],
                              q_norm_g[l], k_norm_g[l], rel_bias, T)
        qp, ks, vs, kw, vw, gates = _nsa_prep(proj, pn, B, T)
        ks, vs, kw, vw = _front_pad(ks, True), _front_pad(vs, False), _front_pad(kw, True), _front_pad(vw, False)
        kc, vc = _nsa_compress(proj, pn, B, T)
        o_b = _nsa_attn(qp, kc, vc, ks, vs, kw, vw, gates, pn, B, T)
        mixed = _merge(o_a, o_b, w_o_rwkv[l].astype(BF16), w_o_nsa[l].astype(BF16), proj)
        x1, h2 = _out_proj(mixed, w_out[l].astype(BF16), x2, mod6, norm2_g[l].reshape(1, D), T)
        x2 = _mlp(h2, w_up[l].astype(BF16), w_down[l].astype(BF16), x1, mod6, T)
    return x2.reshape(B, T, D)
```

```python
import functools
import math

import numpy as np
import jax
import jax.numpy as jnp
from jax import lax
from jax.experimental import pallas as pl
from jax.experimental.pallas import tpu as pltpu

F32 = jnp.float32
BF16 = jnp.bfloat16
HIGHEST = lax.Precision.HIGHEST

D_MODEL = 2048
HEAD_DIM = 64
RWKV_WIDTH = D_MODEL // 2
DECAY_LORA = 64
ICLR_LORA = 64
GATE_LORA = 160
RWKV_GN_EPS = 64e-5
NSA_WIDTH = D_MODEL // 2
NSA_HEADS = NSA_WIDTH // HEAD_DIM
NSA_KV_HEADS = 4
NSA_GROUP = NSA_HEADS // NSA_KV_HEADS
NSA_KV_WIDTH = NSA_KV_HEADS * HEAD_DIM
CMP_BLOCK = 32
CMP_STRIDE = 16
SEL_BLOCK = 64
N_SEL = 8
WINDOW = 512
QUERY_BLOCK = 128
REL_BUCKETS = 32
REL_MAX_DIST = 128
D_FF = 4 * D_MODEL
NORM_EPS = 1e-6
NEG_INF = -1e30
FORCE_SCORE = 1e4

RWKV_COLS = 3 * RWKV_WIDTH + DECAY_LORA + ICLR_LORA + GATE_LORA
NSA_COLS = NSA_WIDTH + 6 * NSA_KV_WIDTH + 3 * NSA_HEADS

V7X_VMEM_BYTES = 64 * 1024 * 1024
LANES = 128

COL_RKV = 0
COL_Q = 3 * RWKV_WIDTH
COL_MERGE = COL_Q + NSA_WIDTH
COL_KV = COL_MERGE + 2 * D_MODEL
COL_SMALL = COL_KV + 6 * NSA_KV_WIDTH
SMALL_W = 512
PROJ_COLS = COL_SMALL + SMALL_W
SM_XG = DECAY_LORA + ICLR_LORA
SM_GATES = SM_XG + GATE_LORA


def _vmem_limit(nbytes):
    return int(min(nbytes * 5 // 4 + (4 << 20), V7X_VMEM_BYTES - (8 << 20)))


def _cparams(sem, vmem_bytes):
    return pltpu.CompilerParams(dimension_semantics=sem, vmem_limit_bytes=_vmem_limit(vmem_bytes))


def _ada_kernel(c_ref, w_ref, b_ref, o_ref):
    c = c_ref[...]
    s = c * jax.nn.sigmoid(c)
    o_ref[...] = jnp.dot(s.astype(BF16), w_ref[...].astype(BF16), preferred_element_type=F32) + b_ref[...]


def _ada_mod(c, w_ada, b_ada):
    B, D = c.shape
    N = w_ada.shape[1]
    tn = 1024
    return pl.pallas_call(
        _ada_kernel,
        grid=(N // tn,),
        in_specs=[
            pl.BlockSpec((B, D), lambda j: (0, 0)),
            pl.BlockSpec((D, tn), lambda j: (0, j)),
            pl.BlockSpec((1, tn), lambda j: (0, j)),
        ],
        out_specs=pl.BlockSpec((B, tn), lambda j: (0, j)),
        out_shape=jax.ShapeDtypeStruct((B, N), F32),
        compiler_params=_cparams(("parallel",), 2 * D * tn * 4 + D * tn * 2),
        name="ada_mod",
    )(c, w_ada, b_ada.reshape(1, N))


def _modulated_norm(x, g, sc, sh):
    ms = jnp.mean(x * x, axis=-1, keepdims=True)
    return (x * lax.rsqrt(ms + NORM_EPS) * g) * (1.0 + sc) + sh


def _inproj_kernel(x_ref, g_ref, sh_ref, sc_ref, w_ref, o_ref, h_scr):
    @pl.when(pl.program_id(1) == 0)
    def _():
        h_scr[...] = _modulated_norm(x_ref[...], g_ref[...], sc_ref[0], sh_ref[0]).astype(BF16)

    o_ref[...] = jnp.dot(h_scr[...], w_ref[...], preferred_element_type=F32)


def _in_proj(x2, g1, mod6, w_in_p, T):
    BT, D = x2.shape
    NP = w_in_p.shape[1]
    tm = min(1024, T)
    tn = 512
    tpb = T // tm
    vm = 2 * tm * D * 4 + tm * D * 2 + 2 * D * tn * 2 + 2 * tm * tn * 4 + 2 * tm * D * 4
    return pl.pallas_call(
        _inproj_kernel,
        grid=(BT // tm, NP // tn),
        in_specs=[
            pl.BlockSpec((tm, D), lambda i, j: (i, 0)),
            pl.BlockSpec((1, D), lambda i, j: (0, 0)),
            pl.BlockSpec((1, 1, D), lambda i, j: ((i // tpb) * 6 + 0, 0, 0)),
            pl.BlockSpec((1, 1, D), lambda i, j: ((i // tpb) * 6 + 1, 0, 0)),
            pl.BlockSpec((D, tn), lambda i, j: (0, j)),
        ],
        out_specs=pl.BlockSpec((tm, tn), lambda i, j: (i, j)),
        out_shape=jax.ShapeDtypeStruct((BT, NP), F32),
        scratch_shapes=[pltpu.VMEM((tm, D), BF16)],
        compiler_params=_cparams(("parallel", "arbitrary"), vm),
        name="in_proj",
    )(x2, g1, mod6, mod6, w_in_p)


def _merge_kernel(oa_ref, ob_ref, wa_ref, wb_ref, ga_ref, gb_ref, o_ref):
    ya = jnp.dot(oa_ref[...], wa_ref[...], preferred_element_type=F32)
    yb = jnp.dot(ob_ref[...], wb_ref[...], preferred_element_type=F32)
    o_ref[...] = (jax.nn.sigmoid(ga_ref[...]) * ya + jax.nn.sigmoid(gb_ref[...]) * yb).astype(BF16)


def _merge(o_a, o_b, w_oa, w_ob, proj):
    BT, W = o_a.shape
    D = w_oa.shape[1]
    tm, tn = 512, 1024
    ga0 = COL_MERGE // tn
    gb0 = (COL_MERGE + D) // tn
    vm = 2 * (2 * tm * W * 2 + 2 * W * tn * 2 + 2 * tm * tn * 4 + tm * tn * 2) + 3 * tm * tn * 4
    return pl.pallas_call(
        _merge_kernel,
        grid=(BT // tm, D // tn),
        in_specs=[
            pl.BlockSpec((tm, W), lambda i, j: (i, 0)),
            pl.BlockSpec((tm, W), lambda i, j: (i, 0)),
            pl.BlockSpec((W, tn), lambda i, j: (0, j)),
            pl.BlockSpec((W, tn), lambda i, j: (0, j)),
            pl.BlockSpec((tm, tn), lambda i, j: (i, ga0 + j)),
            pl.BlockSpec((tm, tn), lambda i, j: (i, gb0 + j)),
        ],
        out_specs=pl.BlockSpec((tm, tn), lambda i, j: (i, j)),
        out_shape=jax.ShapeDtypeStruct((BT, D), BF16),
        compiler_params=_cparams(("parallel", "parallel"), vm),
        name="merge",
    )(o_a, o_b, w_oa, w_ob, proj, proj)


def _outproj_kernel(m_ref, w_ref, x_ref, gt_ref, g_ref, sh_ref, sc_ref, x1_ref, h2_ref):
    y = jnp.dot(m_ref[...], w_ref[...], preferred_element_type=F32)
    x1 = x_ref[...] + gt_ref[0] * y
    x1_ref[...] = x1
    h2_ref[...] = _modulated_norm(x1, g_ref[...], sc_ref[0], sh_ref[0]).astype(BF16)


def _out_proj(mixed, w_out, x2, mod6, g2, T):
    BT, D = x2.shape
    tm = min(512, T)
    tpb = T // tm
    vm = 2 * (tm * D * 2 + D * D * 2 + tm * D * 4 + tm * D * 4 + tm * D * 2) + 3 * tm * D * 4
    mod_spec = lambda k: pl.BlockSpec((1, 1, D), lambda i: ((i // tpb) * 6 + k, 0, 0))
    return pl.pallas_call(
        _outproj_kernel,
        grid=(BT // tm,),
        in_specs=[
            pl.BlockSpec((tm, D), lambda i: (i, 0)),
            pl.BlockSpec((D, D), lambda i: (0, 0)),
            pl.BlockSpec((tm, D), lambda i: (i, 0)),
            mod_spec(2),
            pl.BlockSpec((1, D), lambda i: (0, 0)),
            mod_spec(3),
            mod_spec(4),
        ],
        out_specs=[pl.BlockSpec((tm, D), lambda i: (i, 0)), pl.BlockSpec((tm, D), lambda i: (i, 0))],
        out_shape=[jax.ShapeDtypeStruct((BT, D), F32), jax.ShapeDtypeStruct((BT, D), BF16)],
        compiler_params=_cparams(("parallel",), vm),
        name="out_proj",
    )(mixed, w_out, x2, mod6, g2, mod6, mod6)


def _mlp_kernel(h_ref, wu_ref, wd_ref, x_ref, gt_ref, o_ref, acc_ref):
    f = pl.program_id(1)
    u = jnp.dot(h_ref[...], wu_ref[...], preferred_element_type=F32)
    u = jnp.square(jnp.maximum(u, 0.0)).astype(BF16)
    part = jnp.dot(u, wd_ref[...], preferred_element_type=F32)

    @pl.when(f == 0)
    def _():
        acc_ref[...] = part

    @pl.when(f > 0)
    def _():
        acc_ref[...] += part

    @pl.when(f == pl.num_programs(1) - 1)
    def _():
        o_ref[...] = x_ref[...] + gt_ref[0] * acc_ref[...]


def _mlp(h2, w_up, w_down, x1, mod6, T):
    BT, D = x1.shape
    F = w_up.shape[1]
    tm = min(512, T)
    tf = 1024
    tpb = T // tm
    vm = 2 * (tm * D * 2 + 2 * D * tf * 2 + 2 * tm * D * 4) + tm * D * 4 + 2 * tm * tf * 4
    return pl.pallas_call(
        _mlp_kernel,
        grid=(BT // tm, F // tf),
        in_specs=[
            pl.BlockSpec((tm, D), lambda i, f: (i, 0)),
            pl.BlockSpec((D, tf), lambda i, f: (0, f)),
            pl.BlockSpec((tf, D), lambda i, f: (f, 0)),
            pl.BlockSpec((tm, D), lambda i, f: (i, 0)),
            pl.BlockSpec((1, 1, D), lambda i, f: ((i // tpb) * 6 + 5, 0, 0)),
        ],
        out_specs=pl.BlockSpec((tm, D), lambda i, f: (i, 0)),
        out_shape=jax.ShapeDtypeStruct((BT, D), F32),
        scratch_shapes=[pltpu.VMEM((tm, D), F32)],
        compiler_params=_cparams(("parallel", "arbitrary"), vm),
        name="mlp",
    )(h2, w_up, w_down, x1, mod6)


def _dotb(a, b):
    return jnp.dot(a.astype(BF16), b.astype(BF16), preferred_element_type=F32)


def _dotb_nt(a, b):
    return lax.dot_general(a.astype(BF16), b.astype(BF16), (((1,), (1,)), ((), ())),
                           preferred_element_type=F32)


def _split_bf16(x, terms):
    parts, rem = [], x
    for t in range(terms):
        p = rem.astype(BF16)
        parts.append(p)
        if t + 1 < terms:
            rem = rem - p.astype(F32)
    return parts


def _dot_sel(x, sel, terms):
    sel = sel.astype(BF16)
    return sum(jnp.dot(p, sel, preferred_element_type=F32) for p in _split_bf16(x, terms))


def _sel_dot(sel, x, terms):
    sel = sel.astype(BF16)
    return sum(jnp.dot(sel, p, preferred_element_type=F32) for p in _split_bf16(x, terms))


def _dot3(a, b_hi, b_lo):
    a_hi, a_lo = _split_bf16(a, 2)
    return (jnp.dot(a_hi, b_hi, preferred_element_type=F32) + jnp.dot(a_lo, b_hi, preferred_element_type=F32)
            + jnp.dot(a_hi, b_lo, preferred_element_type=F32))


def _iota(shape, axis):
    return lax.broadcasted_iota(jnp.int32, shape, axis)


def _same_head_mask(n):
    return (_iota((n, n), 0) >> 6) == (_iota((n, n), 1) >> 6)


RW_TC = 256
RW_C = 64
RW_LANES = 4 * HEAD_DIM


def _block_diag(x, bdmask):
    return jnp.where(bdmask, jnp.concatenate([x, x, x, x], axis=0), 0.0)


def _rwkv_kernel(r_ref, k_ref, v_ref, s_ref, mur_ref, muk_ref, muv_ref, mus_ref,
                 w0_ref, a0_ref, kk_ref, ka_ref, rk_ref, lnw_ref, lnb_ref,
                 w2h_ref, w2l_ref, a2_ref, g2_ref, o_ref,
                 pr_scr, pk_scr, pv_scr, ps_scr, state_scr):
    TC = r_ref.shape[0]
    C = RW_C
    W = RW_LANES

    @pl.when(pl.program_id(1) == 0)
    def _():
        pr_scr[...] = jnp.zeros_like(pr_scr)
        pk_scr[...] = jnp.zeros_like(pk_scr)
        pv_scr[...] = jnp.zeros_like(pv_scr)
        ps_scr[...] = jnp.zeros_like(ps_scr)
        state_scr[...] = jnp.zeros_like(state_scr)

    def shift_mix(p_ref, prev_scr, mu_ref):
        p = p_ref[...]
        rolled = pltpu.roll(p, 1, 0)
        first = jnp.where(_iota((8, p.shape[1]), 0) == 0, prev_scr[0:1, :], rolled[0:8])
        shifted = jnp.concatenate([first, rolled[8:]], axis=0)
        prev_scr[0:1, :] = p[TC - 1:TC, :]
        return p + (shifted - p) * mu_ref[...]

    CW = r_ref.shape[1]
    G = CW // W
    NQ = TC // C
    groups = lambda x: [x[:, g * W:(g + 1) * W] for g in range(G)]
    per_group = lambda f, x: jnp.concatenate([f(xg) for xg in groups(x)], axis=1)

    r = shift_mix(r_ref, pr_scr, mur_ref)
    k = shift_mix(k_ref, pk_scr, muk_ref)
    v = shift_mix(v_ref, pv_scr, muv_ref)
    sm = shift_mix(s_ref, ps_scr, mus_ref)
    xwa = sm[:, 0:SM_XG]

    bdmask = _same_head_mask(W)
    bones = jnp.where(bdmask, 1.0, 0.0)
    head_sum = lambda x: per_group(lambda xg: _dot_sel(xg, bones, 1), x)
    tri = jnp.where(_same_head_mask(TC) & (_iota((TC, TC), 1) <= _iota((TC, TC), 0)), 1.0, 0.0)
    lane = _iota((C, W), 1)
    row = _iota((C, W), 0)
    lanehead = lane >> 6
    strict = (lane & 63) < row
    incl = (lane & 63) <= row
    eye_cat = jnp.where((lane & 63) == row, 1.0, 0.0)

    wlin = w0_ref[...] + _dot3(jnp.tanh(xwa), w2h_ref[...], w2l_ref[...])
    a = jax.nn.sigmoid(a0_ref[...] + _dotb(xwa, a2_ref[...]))
    gate = _dotb(jax.nn.sigmoid(sm[:, SM_XG:SM_XG + 256]), g2_ref[...])
    z = -wlin
    softplus = jnp.maximum(z, 0.0) + jnp.log(1.0 + jnp.exp(-jnp.abs(z)))
    ld = -jnp.exp(-softplus - 0.5)
    cum = _sel_dot(tri, ld, 3)
    cum_last = jnp.concatenate(
        [jnp.broadcast_to(cum[(q + 1) * C - 1:(q + 1) * C, :], (C, CW)) for q in range(NQ)], axis=0)
    kk = k * kk_ref[...]
    kk = kk * lax.rsqrt(jnp.maximum(head_sum(kk * kk), 1e-24))
    k2 = k * (1.0 + (a - 1.0) * ka_ref[...])
    bvec = kk * a
    e_inv = jnp.exp(-cum)
    e_end = jnp.exp(cum_last - cum)
    At_f = -kk * jnp.exp(cum - ld)
    Rt_f = r * jnp.exp(cum)
    Bt_f = bvec * e_inv
    Kt_f = k2 * e_inv
    Bg_f = bvec * e_end
    Kg_f = k2 * e_end
    g_end = jnp.exp(cum_last)

    items = [(g, q) for q in range(NQ) for g in range(G)]
    blk = lambda x, g, q: x[q * C:(q + 1) * C, g * W:(g + 1) * W]
    bd01 = jnp.where(bdmask, 1.0, 0.0).astype(BF16)

    def bd(x):
        xb = x.astype(BF16)
        return jnp.concatenate([xb, xb, xb, xb], axis=0) * bd01

    At = {it: blk(At_f, *it) for it in items}
    Rt = {it: blk(Rt_f, *it) for it in items}
    Vq = {it: blk(v, *it) for it in items}
    AA = {}
    for it in items:
        bk = jnp.concatenate([jnp.where(lanehead == h, X, 0.0)
                              for X in (blk(Bt_f, *it), blk(Kt_f, *it)) for h in range(4)], axis=0)
        AA[it] = _dotb_nt(jnp.concatenate([At[it], Rt[it]], axis=0), bk)
    A_ab = {it: jnp.where(strict, AA[it][0:C, 0:W], 0.0) for it in items}
    A_ak = {it: jnp.where(strict, AA[it][0:C, W:2 * W], 0.0) for it in items}
    A_rb = {it: jnp.where(incl, AA[it][C:2 * C, 0:W], 0.0) for it in items}
    A_rk = {it: jnp.where(incl, AA[it][C:2 * C, W:2 * W], 0.0) for it in items}
    M = dict(A_ab)
    Tm = {it: eye_cat + A_ab[it] for it in items}
    for _ in range(5):
        M = {it: _dotb(M[it], bd(M[it])) for it in items}
        Tm = {it: Tm[it] + _dotb(M[it], bd(Tm[it])) for it in items}
    Vbd = {it: bd(Vq[it]) for it in items}
    akv = {it: _dotb(A_ak[it], Vbd[it]) for it in items}
    rkv = {it: _dotb(A_rk[it], Vbd[it]) for it in items}

    S = [state_scr[g] for g in range(G)]
    ys = []
    for q in range(NQ):
        its = [(g, q) for g in range(G)]
        rhs = [_dotb_nt(At[it], S[it[0]]) + akv[it] for it in its]
        U = [_dotb(Tm[it], bd(rhs[g])) for g, it in enumerate(its)]
        ys.append(jnp.concatenate(
            [_dotb_nt(Rt[it], S[g]) + _dotb(A_rb[it], bd(U[g])) + rkv[it] for g, it in enumerate(its)], axis=1))
        upd = [_dotb(jnp.concatenate([U[g], Vq[it]], axis=0).T,
                     jnp.concatenate([blk(Bg_f, *it), blk(Kg_f, *it)], axis=0)) for g, it in enumerate(its)]
        S = [S[g] * blk(g_end, g, q)[0:1, :] + jnp.where(bdmask, upd[g], 0.0) for g in range(G)]
    for g in range(G):
        state_scr[g] = S[g]
    y = jnp.concatenate(ys, axis=0)

    inv_n = 1.0 / HEAD_DIM
    d = y - head_sum(y) * inv_n
    var = head_sum(d * d) * inv_n
    yn = d * lax.rsqrt(var + RWKV_GN_EPS) * lnw_ref[...] + lnb_ref[...]
    bonus = head_sum(r * k2 * rk_ref[...]) * v
    o_ref[...] = ((yn + bonus) * gate).astype(BF16)


def _rwkv_mix(proj, pr, B, T):
    BT = proj.shape[0]
    TC = min(RW_TC, T)
    W = RW_LANES
    CW = RWKV_WIDTH
    nct = T // TC
    row = lambda b, c: b * nct + c
    full = lambda shape: pl.BlockSpec(shape, lambda b, c: (0, 0))
    vec = full((1, CW))
    in_specs = [
        pl.BlockSpec((TC, CW), lambda b, c: (row(b, c), 0)),
        pl.BlockSpec((TC, CW), lambda b, c: (row(b, c), 1)),
        pl.BlockSpec((TC, CW), lambda b, c: (row(b, c), 2)),
        pl.BlockSpec((TC, SMALL_W), lambda b, c: (row(b, c), COL_SMALL // SMALL_W)),
        vec, vec, vec,
        full((1, SMALL_W)),
        vec, vec, vec, vec, vec, vec, vec,
        full((SM_XG, CW)), full((SM_XG, CW)), full((SM_XG, CW)), full((256, CW)),
    ]
    vm = (2 * (3 * TC * CW * 4 + TC * SMALL_W * 4 + TC * CW * 2 + (3 * SM_XG + 256) * CW * 2)
          + 24 * TC * CW * 4)
    return pl.pallas_call(
        _rwkv_kernel,
        grid=(B, nct),
        in_specs=in_specs,
        out_specs=pl.BlockSpec((TC, CW), lambda b, c: (row(b, c), 0)),
        out_shape=jax.ShapeDtypeStruct((BT, CW), BF16),
        scratch_shapes=[pltpu.VMEM((8, CW), F32), pltpu.VMEM((8, CW), F32), pltpu.VMEM((8, CW), F32),
                        pltpu.VMEM((8, SMALL_W), F32), pltpu.VMEM((CW // W, W, W), F32)],
        compiler_params=_cparams(("parallel", "arbitrary"), vm),
        name="rwkv_mix",
    )(proj, proj, proj, proj, pr["mu_r"], pr["mu_k"], pr["mu_v"], pr["mu_s"],
      pr["w0"], pr["a0"], pr["k_k"], pr["k_a"], pr["r_k"], pr["ln_w"], pr["ln_b"],
      pr["w2h"], pr["w2l"], pr["a2"], pr["g2"])


NSA_TT = 256
PADFLAG_LANE = 32


def _nsa_prep_kernel(q_ref, kv0_ref, kv1_ref, kv2_ref, s_ref, qg_ref, kgs_ref, kgw_ref, e_ref, gsel_ref,
                     qp_ref, ksl_ref, vsl_ref, kwl_ref, vwl_ref, gt_ref):
    tt = q_ref.shape[0]
    QW = NSA_GROUP * HEAD_DIM
    kv_refs = (kv0_ref, kv1_ref, kv2_ref)
    bones = jnp.where(_same_head_mask(QW), 1.0, 0.0)
    bones2 = jnp.where(_same_head_mask(LANES), 1.0, 0.0)
    lane = _iota((tt, LANES), 1)
    block_id = (pl.program_id(1) * tt + _iota((tt, LANES), 0)) >> 6
    onehot = jnp.where(lane == block_id, 1.0, 0.0)
    small = _split_bf16(s_ref[...], 3)

    def pair(g, branch):
        off = (3 * g + branch) * LANES
        return kv_refs[off // SMALL_W][:, off % SMALL_W:off % SMALL_W + LANES]

    def slabs(g, x, gain, k_ref, v_ref):
        ms = _dot_sel(x * x, bones2, 1) * (1.0 / HEAD_DIM)
        k_ref[0, g] = jnp.where(lane >= HEAD_DIM, x * lax.rsqrt(ms + NORM_EPS) * gain, onehot).astype(BF16)
        v_ref[0, g] = jnp.where(lane < HEAD_DIM, x, 1.0).astype(BF16)

    for g in range(NSA_KV_HEADS):
        q = q_ref[:, g * QW:(g + 1) * QW]
        ms = _dot_sel(q * q, bones, 1) * (1.0 / HEAD_DIM)
        qn = (q * lax.rsqrt(ms + NORM_EPS) * qg_ref[...]) * (HEAD_DIM ** -0.5)
        qs = jnp.dot(qn.astype(BF16), e_ref[...], preferred_element_type=F32).astype(BF16)
        for h in range(NSA_GROUP):
            qp_ref[0, NSA_GROUP * g + h] = qs[:, h * LANES:(h + 1) * LANES]
        slabs(g, pair(g, 1), kgs_ref[...], ksl_ref, vsl_ref)
        slabs(g, pair(g, 2), kgw_ref[...], kwl_ref, vwl_ref)
        sel = gsel_ref[g].astype(BF16)
        gt_ref[0, g] = jax.nn.sigmoid(sum(jnp.dot(part, sel, preferred_element_type=F32) for part in small))


def _nsa_prep(proj, pn, B, T):
    tt = min(NSA_TT, T)
    ntt = T // tt
    G = NSA_KV_HEADS
    QW = NSA_GROUP * HEAD_DIM
    row = lambda b, t: b * ntt + t
    full = lambda shape: pl.BlockSpec(shape, lambda b, t: tuple(0 for _ in shape))
    kv_spec = lambda j: pl.BlockSpec((tt, SMALL_W), lambda b, t: (row(b, t), COL_KV // SMALL_W + j))
    vm = 2 * (tt * NSA_WIDTH * 4 + 4 * tt * SMALL_W * 4 + QW * 4 * LANES * 2 + G * SMALL_W * LANES * 4
              + 16 * tt * LANES * 2 + 16 * tt * LANES * 2 + 4 * tt * LANES * 4) + 24 * tt * QW * 4
    return pl.pallas_call(
        _nsa_prep_kernel,
        grid=(B, ntt),
        in_specs=[
            pl.BlockSpec((tt, NSA_WIDTH), lambda b, t: (row(b, t), COL_Q // NSA_WIDTH)),
            kv_spec(0), kv_spec(1), kv_spec(2),
            pl.BlockSpec((tt, SMALL_W), lambda b, t: (row(b, t), COL_SMALL // SMALL_W)),
            full((1, QW)), full((1, LANES)), full((1, LANES)), full((QW, NSA_GROUP * LANES)),
            full((G, SMALL_W, LANES)),
        ],
        out_specs=[pl.BlockSpec((1, NSA_HEADS, tt, LANES), lambda b, t: (b, 0, t, 0))]
        + [pl.BlockSpec((1, G, tt, LANES), lambda b, t: (b, 0, t, 0))] * 5,
        out_shape=[jax.ShapeDtypeStruct((B, NSA_HEADS, T, LANES), BF16)]
        + [jax.ShapeDtypeStruct((B, G, T, LANES), BF16)] * 4
        + [jax.ShapeDtypeStruct((B, G, T, LANES), F32)],
        compiler_params=_cparams(("parallel", "parallel"), vm),
        name="nsa_prep",
    )(proj, proj, proj, proj, proj, pn["q_g"], pn["kg_sel"], pn["kg_win"], pn["q_spread"], pn["gate_sel"])


def _gelu_tanh(x):
    return 0.5 * x * (1.0 + jnp.tanh(math.sqrt(2.0 / math.pi) * (x + 0.044715 * (x * x * x))))


def _nsa_compress_kernel(x_ref, pea_ref, peb_ref, wa_ref, wb_ref, w2_ref, kg_ref, kc_ref, vc_ref):
    nsub = x_ref.shape[0] // CMP_STRIDE
    xs = jnp.concatenate([x_ref[pl.ds(s, nsub, stride=CMP_STRIDE), :] for s in range(CMP_STRIDE)], axis=1)
    p0 = jnp.dot((xs + pea_ref[...]).astype(BF16), wa_ref[...], preferred_element_type=F32)
    p1 = jnp.dot((xs + peb_ref[...]).astype(BF16), wb_ref[...], preferred_element_type=F32)
    hid = _gelu_tanh(p0 + pltpu.roll(p1, nsub - 1, 0))
    out = jnp.dot(hid.astype(BF16), w2_ref[...], preferred_element_type=F32)
    bones2 = jnp.where(_same_head_mask(LANES), 1.0, 0.0)
    ms = _dot_sel(out * out, bones2, 2) * (1.0 / HEAD_DIM)
    is_k = _iota(out.shape, 1) >= HEAD_DIM
    kc_ref[0, 0] = jnp.where(is_k, out * lax.rsqrt(ms + NORM_EPS) * kg_ref[...], 0.0).astype(BF16)
    vc_ref[0, 0] = jnp.where(is_k, 0.0, out).astype(BF16)


def _nsa_compress(proj, pn, B, T):
    G = NSA_KV_HEADS
    nsub = T // CMP_STRIDE
    kv_blk = COL_KV // LANES
    KW = CMP_STRIDE * LANES
    full = lambda shape: pl.BlockSpec(shape, lambda b, g: tuple(0 for _ in shape))
    vm = 2 * (T * LANES * 4 + 2 * KW * LANES * 2 + nsub * LANES * 2) + 6 * nsub * KW * 4
    return pl.pallas_call(
        _nsa_compress_kernel,
        grid=(B, G),
        in_specs=[
            pl.BlockSpec((T, LANES), lambda b, g: (b, kv_blk + 3 * g)),
            full((1, KW)), full((1, KW)), full((KW, LANES)), full((KW, LANES)), full((LANES, LANES)),
            full((1, LANES)),
        ],
        out_specs=[pl.BlockSpec((1, 1, nsub, LANES), lambda b, g: (b, g, 0, 0))] * 2,
        out_shape=[jax.ShapeDtypeStruct((B, G, nsub, LANES), BF16)] * 2,
        compiler_params=_cparams(("parallel", "parallel"), vm),
        name="nsa_compress",
    )(proj, pn["pe_a"], pn["pe_b"], pn["cw_a"], pn["cw_b"], pn["cw2"], pn["kg_cmp"])


NSA_GROUPS_PER_STEP = 2
NSA_KPAD = WINDOW
SEL_CHUNK = 4 * QUERY_BLOCK
WIN_KEYS = WINDOW + QUERY_BLOCK


def _lane_tile_max(s):
    tiles = [s[:, j * LANES:(j + 1) * LANES] for j in range(s.shape[1] // LANES)]
    while len(tiles) > 1:
        tiles = [jnp.maximum(a, b) for a, b in zip(tiles[0::2], tiles[1::2])] + ([tiles[-1]] if len(tiles) % 2 else [])
    return tiles[0]


def _nsa_attn_kernel(q_ref, kc_ref, vc_ref, ks_ref, vs_ref, kw_ref, vw_ref, gt_ref, stab_ref, wtab_ref, ctab_ref,
                     selmt_ref, gather_ref, grep_ref, o_ref, sbuf, mx_scr, acc_scr, sc_scr, *, n_sel):
    i = pl.program_id(2)
    QB = QUERY_BLOCK
    HG = NSA_GROUP
    R = HG * QB
    t0 = i * QB
    groups = range(kc_ref.shape[1])
    each = lambda f: [f(g) for g in groups]
    tile4 = lambda z: jnp.concatenate([z, z, z, z], axis=0)
    heads = lambda ref, g: ref[HG * g:HG * (g + 1)]
    lane = _iota((QB, LANES), 1)
    pad_mask = jnp.where(lane == PADFLAG_LANE, NEG_INF, 0.0)
    q = each(lambda g: q_ref[0, HG * g:HG * (g + 1)].reshape(R, LANES))
    q32 = each(lambda g: q[g].astype(F32))

    ncp = kc_ref.shape[2]
    s = each(lambda g: _dotb_nt(q[g], kc_ref[0, g]) + heads(ctab_ref, g)[:, 0].reshape(R, ncp))
    m = each(lambda g: jnp.max(s[g], axis=-1, keepdims=True))
    p_c = each(lambda g: jnp.exp(s[g] - m[g]))
    lsum = each(lambda g: jnp.sum(p_c[g], axis=-1, keepdims=True))
    p_c = each(lambda g: p_c[g] * jnp.where(m[g] > 0.5 * NEG_INF, 1.0 / lsum[g], 0.0))
    o_c = each(lambda g: jnp.dot(p_c[g].astype(BF16), vc_ref[0, g], preferred_element_type=F32))

    wrows = pl.ds(pl.multiple_of(t0, QB), WIN_KEYS)
    q_win = each(lambda g: (q32[g] + tile4(pad_mask)).astype(BF16))
    s = each(lambda g: _dotb_nt(q_win[g], kw_ref[0, g, wrows, :]) + heads(wtab_ref, g).reshape(R, WIN_KEYS))
    p = each(lambda g: jnp.exp(s[g] - jnp.max(_lane_tile_max(s[g]), axis=-1, keepdims=True)))
    acc_w = each(lambda g: jnp.dot(p[g].astype(BF16), vw_ref[0, g, wrows, :], preferred_element_type=F32))

    psum = each(lambda g: p_c[g][0:QB] + p_c[g][QB:2 * QB] + p_c[g][2 * QB:3 * QB] + p_c[g][3 * QB:4 * QB])
    selmt = selmt_ref[...]
    parts = each(lambda g: _split_bf16(psum[g], 3))
    imp = each(lambda g: sum(lax.dot_general(selmt, part, (((1,), (1,)), ((), ())), preferred_element_type=F32)
                             for part in parts[g]))
    ns = selmt.shape[0]
    blk = _iota((ns, QB), 0)
    cur = (t0 + _iota((ns, QB), 1)) >> 6
    forced = (blk == 0) | (blk == cur) | (blk == cur - 1)
    score = each(lambda g: jnp.where(forced, FORCE_SCORE, jnp.where(blk <= cur, imp[g], -1.0)))
    for g in groups:
        sc_scr[g] = score[g]
    ranks = [[] for _ in groups]
    for j in range(ns):
        lower = jnp.where(blk > j, 1.0, 0.0)
        for g in groups:
            other = sc_scr[g, j:j + 1, :]
            ranks[g].append(jnp.where(other > score[g], 1.0, 0.0) + jnp.where(other == score[g], lower, 0.0))
    while len(ranks[0]) > 1:
        ranks = [[a + b for a, b in zip(r[0::2], r[1::2])] + ([r[-1]] if len(r) % 2 else []) for r in ranks]
    chosen_t = each(lambda g: jnp.where(ranks[g][0] < n_sel, 1.0, 0.0))
    chosen = each(lambda g: jnp.concatenate([chosen_t[g], jnp.zeros((LANES - ns, QB), F32)], axis=0).T)
    q_sel = each(lambda g: (q32[g] + tile4(jnp.where(lane < ns, (chosen[g] - 1.0) * (-NEG_INF), pad_mask))
                            ).astype(BF16))

    def chunk_rows(c):
        return pl.ds(pl.multiple_of((i - 4 * c + 1) * QB, QB), SEL_CHUNK)

    s0 = each(lambda g: _dotb_nt(q_sel[g], ks_ref[0, g, chunk_rows(0), :]) + heads(stab_ref, g).reshape(R, SEL_CHUNK))
    for g in groups:
        sbuf[g, 0] = s0[g]
        mx_scr[g] = _lane_tile_max(s0[g])

    def scores_body(c, carry):
        s = each(lambda g: _dotb_nt(q_sel[g], ks_ref[0, g, chunk_rows(c), :]))
        for g in groups:
            sbuf[g, c] = s[g]
            mx_scr[g] = jnp.maximum(mx_scr[g], _lane_tile_max(s[g]))
        return carry

    n_far = i // 4
    lax.fori_loop(1, n_far + 1, scores_body, 0)
    m_s = each(lambda g: jnp.max(mx_scr[g], axis=-1, keepdims=True))

    acc_scr[...] = jnp.zeros_like(acc_scr)

    def values_body(c, carry):
        p = each(lambda g: jnp.exp(sbuf[g, c] - m_s[g]).astype(BF16))
        for g in groups:
            acc_scr[g] += jnp.dot(p[g], vs_ref[0, g, chunk_rows(c), :], preferred_element_type=F32)
        return carry

    lax.fori_loop(0, n_far + 1, values_body, 0)

    low = _iota((R, LANES), 1) < HEAD_DIM
    normalised = lambda acc: jnp.where(low, acc * (1.0 / pltpu.roll(acc, HEAD_DIM, 1)), 0.0)

    def natural(o):
        cat = jnp.concatenate([o[h * QB:(h + 1) * QB] for h in range(HG)], axis=1).astype(BF16)
        return jnp.dot(cat, gather_ref[...], preferred_element_type=F32)

    branches = each(lambda g: [o_c[g], normalised(acc_scr[g]), normalised(acc_w[g])])
    gate = each(lambda g: [_dot_sel(gt_ref[0, g], grep_ref[c], 2) for c in range(3)])
    W = HG * HEAD_DIM
    for g in groups:
        out = sum(gate[g][c] * natural(branches[g][c]) for c in range(3))
        o_ref[:, g * W:(g + 1) * W] = out.astype(BF16)


def _nsa_attn(qp, kc, vc, ks, vs, kw, vw, gates, pn, B, T):
    G = NSA_KV_HEADS
    QB = QUERY_BLOCK
    NQ = T // QB
    ncp = kc.shape[2]
    ns = T // SEL_BLOCK
    TP = T + NSA_KPAD
    R = NSA_GROUP * QB
    n_sel = min(N_SEL, ns)
    n_chunks = (NQ - 1) // 4 + 1
    NG = NSA_GROUPS_PER_STEP
    HB = NSA_GROUP * NG
    gtab = lambda w: pl.BlockSpec((HB, QB, w), lambda b, g, i: (g, 0, 0))
    slab = lambda rows: pl.BlockSpec((1, NG, rows, LANES), lambda b, g, i: (b, g, 0, 0))
    vm = (2 * NG * (R * LANES * 2 + 2 * ncp * LANES * 2 + 4 * TP * LANES * 2 + QB * LANES * 4
                    + R * (SEL_CHUNK + WIN_KEYS + ncp) * 4 + QB * NSA_GROUP * HEAD_DIM * 2)
          + NG * (n_chunks * R * SEL_CHUNK * 4 + 2 * R * LANES * 4 + 6 * R * WIN_KEYS * 4))
    return pl.pallas_call(
        functools.partial(_nsa_attn_kernel, n_sel=n_sel),
        grid=(B, G // NG, NQ),
        in_specs=[
            pl.BlockSpec((1, HB, QB, LANES), lambda b, g, i: (b, g, i, 0)),
            slab(ncp), slab(ncp), slab(TP), slab(TP), slab(TP), slab(TP),
            pl.BlockSpec((1, NG, QB, LANES), lambda b, g, i: (b, g, i, 0)),
            gtab(SEL_CHUNK), gtab(WIN_KEYS),
            pl.BlockSpec((HB, 1, QB, ncp), lambda b, g, i: (g, i, 0, 0)),
            pl.BlockSpec((ns, ncp), lambda b, g, i: (0, 0)),
            pl.BlockSpec((NSA_GROUP * LANES, NSA_GROUP * HEAD_DIM), lambda b, g, i: (0, 0)),
            pl.BlockSpec((3, LANES, NSA_GROUP * HEAD_DIM), lambda b, g, i: (0, 0, 0)),
        ],
        out_specs=pl.BlockSpec((QB, HB * HEAD_DIM), lambda b, g, i: (b * NQ + i, g)),
        out_shape=jax.ShapeDtypeStruct((B * T, NSA_WIDTH), BF16),
        scratch_shapes=[pltpu.VMEM((NG, n_chunks, R, SEL_CHUNK), F32), pltpu.VMEM((NG, R, LANES), F32),
                        pltpu.VMEM((NG, R, LANES), F32), pltpu.VMEM((NG, ns, QB), F32)],
        compiler_params=_cparams(("parallel", "parallel", "arbitrary"), vm),
        name="nsa_attn",
    )(qp, kc, vc, ks, vs, kw, vw, gates, pn["stab"], pn["wtab"], pn["ctab"], pn["sel_mt"], pn["gather"],
      pn["gate_rep"])


def _rel_bucket_table():
    n = np.arange(REL_MAX_DIST + 1)
    max_exact = REL_BUCKETS // 2
    nf = np.maximum(n, max_exact).astype(np.float32)
    large = max_exact + (np.log(nf / np.float32(max_exact)) / np.float32(math.log(REL_MAX_DIST / max_exact))
                         * np.float32(REL_BUCKETS - max_exact)).astype(np.int32)
    large = np.minimum(large, REL_BUCKETS - 1)
    return np.where(n < max_exact, n, large).astype(np.int32)


def _sel_to_cmp_matrix(T, ncp):
    nc = T // CMP_STRIDE - CMP_BLOCK // CMP_STRIDE + 1
    ns = T // SEL_BLOCK
    cs = np.arange(nc) * CMP_STRIDE
    ss = np.arange(ns) * SEL_BLOCK
    lo = np.maximum(cs[None, :], ss[:, None])
    hi = np.minimum(cs[None, :] + CMP_BLOCK, ss[:, None] + SEL_BLOCK)
    out = np.zeros((ns, ncp), np.float32)
    out[:, :nc] = np.maximum(hi - lo, 0) / CMP_BLOCK
    return out


def _bias_tables(rel_bias, rel):
    bucket = _rel_bucket_table()[np.clip(rel, 0, REL_MAX_DIST)]
    onehot = (jnp.asarray(bucket.reshape(1, -1)) == jnp.arange(REL_BUCKETS, dtype=jnp.int32)[:, None]).astype(F32)
    tab = jnp.einsum('bh,bn->hn', rel_bias, onehot, precision=HIGHEST)
    return tab.reshape((rel_bias.shape[1],) + rel.shape)


def _prep_in_proj_weight(w_in_all, l):
    D = w_in_all.shape[1]
    nsa0 = RWKV_COLS
    kv0 = nsa0 + NSA_WIDTH
    gates0 = kv0 + 6 * NSA_KV_WIDTH
    merge0 = RWKV_COLS + NSA_COLS
    cols = [(0, 3 * RWKV_WIDTH), (nsa0, NSA_WIDTH), (merge0, 2 * D_MODEL)]
    for g in range(NSA_KV_HEADS):
        for branch in range(3):
            for kv in (1, 0):
                cols.append((kv0 + (2 * branch + kv) * NSA_KV_WIDTH + g * HEAD_DIM, HEAD_DIM))
    cols += [(3 * RWKV_WIDTH, RWKV_COLS - 3 * RWKV_WIDTH), (gates0, 3 * NSA_HEADS)]
    pad = jnp.zeros((D, SMALL_W - (SM_GATES + 3 * NSA_HEADS)), BF16)
    return jnp.concatenate([w_in_all[l, :, s:s + n].astype(BF16) for s, n in cols] + [pad], axis=1)


def _prep_rwkv_params(mu, w0, w2, a0, a2, g2, k_k, k_a, r_k, ln_w, ln_b):
    C = RWKV_WIDTH
    row = lambda z: z.reshape(1, -1).astype(F32)
    mu_s = jnp.concatenate([mu[3 * C:], jnp.zeros((SMALL_W - (RWKV_COLS - 3 * C),), F32)]).reshape(1, SMALL_W)
    zl = jnp.zeros((DECAY_LORA, C), F32)

    w2p = jnp.concatenate([w2, zl], axis=0)
    w2h = w2p.astype(BF16)
    return dict(
        mu_r=row(mu[0:C]), mu_k=row(mu[C:2 * C]), mu_v=row(mu[2 * C:3 * C]), mu_s=mu_s,
        w0=row(w0), a0=row(a0), k_k=row(k_k), k_a=row(k_a), r_k=row(r_k), ln_w=row(ln_w), ln_b=row(ln_b),
        w2h=w2h, w2l=(w2p - w2h.astype(F32)).astype(BF16),
        a2=jnp.concatenate([zl, a2], axis=0).astype(BF16),
        g2=jnp.concatenate([g2, jnp.zeros((256 - GATE_LORA, C), F32)], axis=0).astype(BF16),
    )


def _prep_nsa_params(pe_k, w1_k, w2_k, pe_v, w1_v, w2_v, q_g, k_g, rel_bias, T):
    hd = HEAD_DIM
    ones = jnp.ones((hd,), F32)
    ncp = T // CMP_STRIDE
    NQ = T // QUERY_BLOCK

    def blockdiag(a, b):
        lead = ((0, 0),) * (a.ndim - 2)
        return jnp.pad(a, lead + ((0, hd), (0, hd))) + jnp.pad(b, lead + ((hd, 0), (hd, 0)))

    w1 = blockdiag(w1_v.reshape(CMP_BLOCK, hd, hd), w1_k.reshape(CMP_BLOCK, hd, hd))
    pe = jnp.concatenate([pe_v, pe_k], axis=1)
    half = CMP_STRIDE

    spread = np.zeros((NSA_GROUP * hd, NSA_GROUP * LANES), np.float32)
    gather = np.zeros((NSA_GROUP * LANES, NSA_GROUP * hd), np.float32)
    for h in range(NSA_GROUP):
        for d in range(hd):
            spread[h * hd + d, h * LANES + hd + d] = 1.0
            gather[h * LANES + d, h * hd + d] = 1.0
    gate_sel = np.zeros((NSA_KV_HEADS, SMALL_W, LANES), np.float32)
    for g in range(NSA_KV_HEADS):
        for j in range(3 * NSA_GROUP):
            gate_sel[g, SM_GATES + 3 * NSA_GROUP * g + j, j] = 1.0
    gate_rep = np.zeros((3, LANES, NSA_GROUP * hd), np.float32)
    for c in range(3):
        for h in range(NSA_GROUP):
            gate_rep[c, 3 * h + c, h * hd:(h + 1) * hd] = 1.0

    qi = np.arange(QUERY_BLOCK)[:, None]
    rel_s = qi + (SEL_CHUNK - QUERY_BLOCK) - np.arange(SEL_CHUNK)[None, :]
    rel_w = qi + WINDOW - np.arange(WIN_KEYS)[None, :]
    rel_c = ((np.arange(NQ)[:, None, None] * QUERY_BLOCK + qi[None])
             - (np.arange(ncp)[None, None, :] * CMP_STRIDE + CMP_BLOCK - 1))
    far = np.full((QUERY_BLOCK, 1), REL_MAX_DIST)
    tabs = _bias_tables(rel_bias, np.concatenate([rel_s, rel_w, far], axis=1))
    o1, o2 = SEL_CHUNK, SEL_CHUNK + WIN_KEYS
    masked = lambda tab, ok: jnp.where(jnp.asarray(ok)[None], tab, NEG_INF)
    stab = masked(tabs[:, :, 0:o1] - tabs[:, :, o2:], rel_s >= 0)
    wtab = masked(tabs[:, :, o1:o2], (rel_w >= 0) & (rel_w < WINDOW))
    ctab = masked(_bias_tables(rel_bias, rel_c), rel_c >= 0)
    return dict(
        stab=stab, wtab=wtab, ctab=ctab,
        q_g=jnp.tile(q_g, NSA_GROUP).reshape(1, -1),
        kg_cmp=jnp.concatenate([ones, k_g[0]]).reshape(1, LANES),
        kg_sel=jnp.concatenate([ones, k_g[1]]).reshape(1, LANES),
        kg_win=jnp.concatenate([ones, k_g[2]]).reshape(1, LANES),
        pe_a=pe[:half].reshape(1, half * LANES), pe_b=pe[half:].reshape(1, half * LANES),
        cw_a=w1[:half].reshape(half * LANES, LANES).astype(BF16),
        cw_b=w1[half:].reshape(half * LANES, LANES).astype(BF16),
        cw2=blockdiag(w2_v, w2_k).astype(BF16),
        q_spread=jnp.asarray(spread, BF16), gather=jnp.asarray(gather, BF16),
        gate_sel=jnp.asarray(gate_sel), gate_rep=jnp.asarray(gate_rep, BF16),
        sel_mt=jnp.asarray(_sel_to_cmp_matrix(T, ncp)),
    )


def _front_pad(slab, flagged):
    B, G, _, L = slab.shape
    row = jnp.zeros((L,), slab.dtype)
    if flagged:
        row = row.at[PADFLAG_LANE].set(1.0)
    return jnp.concatenate([jnp.broadcast_to(row, (B, G, NSA_KPAD, L)), slab], axis=2)


def kernel(x, c, w_ada, b_ada, norm1_g, norm2_g, w_in, rwkv_mu, rwkv_w0, rwkv_w2, rwkv_a0, rwkv_a2, rwkv_g2, rwkv_k_k, rwkv_k_a, rwkv_r_k, rwkv_ln_w, rwkv_ln_b, cmp_pe_k, cmp_w1_k, cmp_w2_k, cmp_pe_v, cmp_w1_v, cmp_w2_v, q_norm_g, k_norm_g, rel_bias, w_o_rwkv, w_o_nsa, w_out, w_up, w_down):
    B, T, D = x.shape
    depth = w_in.shape[0]
    x2 = x.reshape(B * T, D)
    for l in range(depth):
        mod6 = _ada_mod(c, w_ada[l], b_ada[l]).reshape(B * 6, 1, D)
        proj = _in_proj(x2, norm1_g[l].reshape(1, D), mod6, _prep_in_proj_weight(w_in, l), T)
        pr = _prep_rwkv_params(rwkv_mu[l], rwkv_w0[l], rwkv_w2[l], rwkv_a0[l], rwkv_a2[l], rwkv_g2[l],
                               rwkv_k_k[l], rwkv_k_a[l], rwkv_r_k[l], rwkv_ln_w[l], rwkv_ln_b[l])
        o_a = _rwkv_mix(proj, pr, B, T)
        pn = _prep_nsa_params(cmp_pe_k[l], cmp_w1_k[l], cmp_w2_k[l], cmp_pe_v[l], cmp_w1_v[l], cmp_w2_v[l],
                              q_norm_g[l], k_norm_g[l], rel_bias, T)
        qp, ks, vs, kw, vw, gates = _nsa_prep(proj, pn, B, T)
        ks, vs, kw, vw = _front_pad(ks, True), _front_pad(vs, False), _front_pad(kw, True), _front_pad(vw, False)
        kc, vc = _nsa_compress(proj, pn, B, T)
        o_b = _nsa_attn(qp, kc, vc, ks, vs, kw, vw, gates, pn, B, T)
        mixed = _merge(o_a, o_b, w_o_rwkv[l].astype(BF16), w_o_nsa[l].astype(BF16), proj)
        x1, h2 = _out_proj(mixed, w_out[l].astype(BF16), x2, mod6, norm2_g[l].reshape(1, D), T)
        x2 = _mlp(h2, w_up[l].astype(BF16), w_down[l].astype(BF16), x1, mod6, T)
    return x2.reshape(B, T, D)
```

```python
import functools
import math

import numpy as np
import jax
import jax.numpy as jnp
from jax import lax
from jax.experimental import pallas as pl
from jax.experimental.pallas import tpu as pltpu

F32 = jnp.float32
BF16 = jnp.bfloat16
HIGHEST = lax.Precision.HIGHEST

D_MODEL = 2048
HEAD_DIM = 64
RWKV_WIDTH = D_MODEL // 2
DECAY_LORA = 64
ICLR_LORA = 64
GATE_LORA = 160
RWKV_GN_EPS = 64e-5
NSA_WIDTH = D_MODEL // 2
NSA_HEADS = NSA_WIDTH // HEAD_DIM
NSA_KV_HEADS = 4
NSA_GROUP = NSA_HEADS // NSA_KV_HEADS
NSA_KV_WIDTH = NSA_KV_HEADS * HEAD_DIM
CMP_BLOCK = 32
CMP_STRIDE = 16
SEL_BLOCK = 64
N_SEL = 8
WINDOW = 512
QUERY_BLOCK = 128
REL_BUCKETS = 32
REL_MAX_DIST = 128
D_FF = 4 * D_MODEL
NORM_EPS = 1e-6
NEG_INF = -1e30
FORCE_SCORE = 1e4

RWKV_COLS = 3 * RWKV_WIDTH + DECAY_LORA + ICLR_LORA + GATE_LORA
NSA_COLS = NSA_WIDTH + 6 * NSA_KV_WIDTH + 3 * NSA_HEADS

V7X_VMEM_BYTES = 64 * 1024 * 1024
LANES = 128

COL_RKV = 0
COL_Q = 3 * RWKV_WIDTH
COL_MERGE = COL_Q + NSA_WIDTH
COL_KV = COL_MERGE + 2 * D_MODEL
COL_SMALL = COL_KV + 6 * NSA_KV_WIDTH
SMALL_W = 512
PROJ_COLS = COL_SMALL + SMALL_W
SM_XG = DECAY_LORA + ICLR_LORA
SM_GATES = SM_XG + GATE_LORA


def _vmem_limit(nbytes):
    return int(min(nbytes * 5 // 4 + (4 << 20), V7X_VMEM_BYTES - (8 << 20)))


def _cparams(sem, vmem_bytes):
    return pltpu.CompilerParams(dimension_semantics=sem, vmem_limit_bytes=_vmem_limit(vmem_bytes))


def _ada_kernel(c_ref, w_ref, b_ref, o_ref):
    c = c_ref[...]
    s = c * jax.nn.sigmoid(c)
    o_ref[...] = jnp.dot(s.astype(BF16), w_ref[...].astype(BF16), preferred_element_type=F32) + b_ref[...]


def _ada_mod(c, w_ada, b_ada):
    B, D = c.shape
    N = w_ada.shape[1]
    tn = 1024
    return pl.pallas_call(
        _ada_kernel,
        grid=(N // tn,),
        in_specs=[
            pl.BlockSpec((B, D), lambda j: (0, 0)),
            pl.BlockSpec((D, tn), lambda j: (0, j)),
            pl.BlockSpec((1, tn), lambda j: (0, j)),
        ],
        out_specs=pl.BlockSpec((B, tn), lambda j: (0, j)),
        out_shape=jax.ShapeDtypeStruct((B, N), F32),
        compiler_params=_cparams(("parallel",), 2 * D * tn * 4 + D * tn * 2),
        name="ada_mod",
    )(c, w_ada, b_ada.reshape(1, N))


def _modulated_norm(x, g, sc, sh):
    ms = jnp.mean(x * x, axis=-1, keepdims=True)
    return (x * lax.rsqrt(ms + NORM_EPS) * g) * (1.0 + sc) + sh


def _inproj_kernel(x_ref, g_ref, sh_ref, sc_ref, w_ref, o_ref, h_scr):
    @pl.when(pl.program_id(1) == 0)
    def _():
        h_scr[...] = _modulated_norm(x_ref[...], g_ref[...], sc_ref[0], sh_ref[0]).astype(BF16)

    o_ref[...] = jnp.dot(h_scr[...], w_ref[...], preferred_element_type=F32)


def _in_proj(x2, g1, mod6, w_in_p, T):
    BT, D = x2.shape
    NP = w_in_p.shape[1]
    tm = min(1024, T)
    tn = 512
    tpb = T // tm
    vm = 2 * tm * D * 4 + tm * D * 2 + 2 * D * tn * 2 + 2 * tm * tn * 4 + 2 * tm * D * 4
    return pl.pallas_call(
        _inproj_kernel,
        grid=(BT // tm, NP // tn),
        in_specs=[
            pl.BlockSpec((tm, D), lambda i, j: (i, 0)),
            pl.BlockSpec((1, D), lambda i, j: (0, 0)),
            pl.BlockSpec((1, 1, D), lambda i, j: ((i // tpb) * 6 + 0, 0, 0)),
            pl.BlockSpec((1, 1, D), lambda i, j: ((i // tpb) * 6 + 1, 0, 0)),
            pl.BlockSpec((D, tn), lambda i, j: (0, j)),
        ],
        out_specs=pl.BlockSpec((tm, tn), lambda i, j: (i, j)),
        out_shape=jax.ShapeDtypeStruct((BT, NP), F32),
        scratch_shapes=[pltpu.VMEM((tm, D), BF16)],
        compiler_params=_cparams(("parallel", "arbitrary"), vm),
        name="in_proj",
    )(x2, g1, mod6, mod6, w_in_p)


def _merge_kernel(oa_ref, ob_ref, wa_ref, wb_ref, ga_ref, gb_ref, o_ref):
    ya = jnp.dot(oa_ref[...], wa_ref[...], preferred_element_type=F32)
    yb = jnp.dot(ob_ref[...], wb_ref[...], preferred_element_type=F32)
    o_ref[...] = (jax.nn.sigmoid(ga_ref[...]) * ya + jax.nn.sigmoid(gb_ref[...]) * yb).astype(BF16)


def _merge(o_a, o_b, w_oa, w_ob, proj):
    BT, W = o_a.shape
    D = w_oa.shape[1]
    tm, tn = 512, 1024
    ga0 = COL_MERGE // tn
    gb0 = (COL_MERGE + D) // tn
    vm = 2 * (2 * tm * W * 2 + 2 * W * tn * 2 + 2 * tm * tn * 4 + tm * tn * 2) + 3 * tm * tn * 4
    return pl.pallas_call(
        _merge_kernel,
        grid=(BT // tm, D // tn),
        in_specs=[
            pl.BlockSpec((tm, W), lambda i, j: (i, 0)),
            pl.BlockSpec((tm, W), lambda i, j: (i, 0)),
            pl.BlockSpec((W, tn), lambda i, j: (0, j)),
            pl.BlockSpec((W, tn), lambda i, j: (0, j)),
            pl.BlockSpec((tm, tn), lambda i, j: (i, ga0 + j)),
            pl.BlockSpec((tm, tn), lambda i, j: (i, gb0 + j)),
        ],
        out_specs=pl.BlockSpec((tm, tn), lambda i, j: (i, j)),
        out_shape=jax.ShapeDtypeStruct((BT, D), BF16),
        compiler_params=_cparams(("parallel", "parallel"), vm),
        name="merge",
    )(o_a, o_b, w_oa, w_ob, proj, proj)


def _outproj_kernel(m_ref, w_ref, x_ref, gt_ref, g_ref, sh_ref, sc_ref, x1_ref, h2_ref):
    y = jnp.dot(m_ref[...], w_ref[...], preferred_element_type=F32)
    x1 = x_ref[...] + gt_ref[0] * y
    x1_ref[...] = x1
    h2_ref[...] = _modulated_norm(x1, g_ref[...], sc_ref[0], sh_ref[0]).astype(BF16)


def _out_proj(mixed, w_out, x2, mod6, g2, T):
    BT, D = x2.shape
    tm = min(512, T)
    tpb = T // tm
    vm = 2 * (tm * D * 2 + D * D * 2 + tm * D * 4 + tm * D * 4 + tm * D * 2) + 3 * tm * D * 4
    mod_spec = lambda k: pl.BlockSpec((1, 1, D), lambda i: ((i // tpb) * 6 + k, 0, 0))
    return pl.pallas_call(
        _outproj_kernel,
        grid=(BT // tm,),
        in_specs=[
            pl.BlockSpec((tm, D), lambda i: (i, 0)),
            pl.BlockSpec((D, D), lambda i: (0, 0)),
            pl.BlockSpec((tm, D), lambda i: (i, 0)),
            mod_spec(2),
            pl.BlockSpec((1, D), lambda i: (0, 0)),
            mod_spec(3),
            mod_spec(4),
        ],
        out_specs=[pl.BlockSpec((tm, D), lambda i: (i, 0)), pl.BlockSpec((tm, D), lambda i: (i, 0))],
        out_shape=[jax.ShapeDtypeStruct((BT, D), F32), jax.ShapeDtypeStruct((BT, D), BF16)],
        compiler_params=_cparams(("parallel",), vm),
        name="out_proj",
    )(mixed, w_out, x2, mod6, g2, mod6, mod6)


def _mlp_kernel(h_ref, wu_ref, wd_ref, x_ref, gt_ref, o_ref, acc_ref):
    f = pl.program_id(1)
    u = jnp.dot(h_ref[...], wu_ref[...], preferred_element_type=F32)
    u = jnp.square(jnp.maximum(u, 0.0)).astype(BF16)
    part = jnp.dot(u, wd_ref[...], preferred_element_type=F32)

    @pl.when(f == 0)
    def _():
        acc_ref[...] = part

    @pl.when(f > 0)
    def _():
        acc_ref[...] += part

    @pl.when(f == pl.num_programs(1) - 1)
    def _():
        o_ref[...] = x_ref[...] + gt_ref[0] * acc_ref[...]


def _mlp(h2, w_up, w_down, x1, mod6, T):
    BT, D = x1.shape
    F = w_up.shape[1]
    tm = min(512, T)
    tf = 1024
    tpb = T // tm
    vm = 2 * (tm * D * 2 + 2 * D * tf * 2 + 2 * tm * D * 4) + tm * D * 4 + 2 * tm * tf * 4
    return pl.pallas_call(
        _mlp_kernel,
        grid=(BT // tm, F // tf),
        in_specs=[
            pl.BlockSpec((tm, D), lambda i, f: (i, 0)),
            pl.BlockSpec((D, tf), lambda i, f: (0, f)),
            pl.BlockSpec((tf, D), lambda i, f: (f, 0)),
            pl.BlockSpec((tm, D), lambda i, f: (i, 0)),
            pl.BlockSpec((1, 1, D), lambda i, f: ((i // tpb) * 6 + 5, 0, 0)),
        ],
        out_specs=pl.BlockSpec((tm, D), lambda i, f: (i, 0)),
        out_shape=jax.ShapeDtypeStruct((BT, D), F32),
        scratch_shapes=[pltpu.VMEM((tm, D), F32)],
        compiler_params=_cparams(("parallel", "arbitrary"), vm),
        name="mlp",
    )(h2, w_up, w_down, x1, mod6)


def _dotb(a, b):
    return jnp.dot(a.astype(BF16), b.astype(BF16), preferred_element_type=F32)


def _dotb_nt(a, b):
    return lax.dot_general(a.astype(BF16), b.astype(BF16), (((1,), (1,)), ((), ())),
                           preferred_element_type=F32)


def _split_bf16(x, terms):
    parts, rem = [], x
    for t in range(terms):
        p = rem.astype(BF16)
        parts.append(p)
        if t + 1 < terms:
            rem = rem - p.astype(F32)
    return parts


def _dot_sel(x, sel, terms):
    sel = sel.astype(BF16)
    return sum(jnp.dot(p, sel, preferred_element_type=F32) for p in _split_bf16(x, terms))


def _sel_dot(sel, x, terms):
    sel = sel.astype(BF16)
    return sum(jnp.dot(sel, p, preferred_element_type=F32) for p in _split_bf16(x, terms))


def _dot3(a, b_hi, b_lo):
    a_hi, a_lo = _split_bf16(a, 2)
    return (jnp.dot(a_hi, b_hi, preferred_element_type=F32) + jnp.dot(a_lo, b_hi, preferred_element_type=F32)
            + jnp.dot(a_hi, b_lo, preferred_element_type=F32))


def _iota(shape, axis):
    return lax.broadcasted_iota(jnp.int32, shape, axis)


def _same_head_mask(n):
    return (_iota((n, n), 0) >> 6) == (_iota((n, n), 1) >> 6)


RW_TC = 256
RW_C = 64
RW_LANES = 4 * HEAD_DIM


def _block_diag(x, bdmask):
    return jnp.where(bdmask, jnp.concatenate([x, x, x, x], axis=0), 0.0)


def _rwkv_kernel(r_ref, k_ref, v_ref, s_ref, mur_ref, muk_ref, muv_ref, mus_ref,
                 w0_ref, a0_ref, kk_ref, ka_ref, rk_ref, lnw_ref, lnb_ref,
                 w2h_ref, w2l_ref, a2_ref, g2_ref, o_ref,
                 pr_scr, pk_scr, pv_scr, ps_scr, state_scr):
    TC = r_ref.shape[0]
    C = RW_C
    W = RW_LANES

    @pl.when(pl.program_id(1) == 0)
    def _():
        pr_scr[...] = jnp.zeros_like(pr_scr)
        pk_scr[...] = jnp.zeros_like(pk_scr)
        pv_scr[...] = jnp.zeros_like(pv_scr)
        ps_scr[...] = jnp.zeros_like(ps_scr)
        state_scr[...] = jnp.zeros_like(state_scr)

    def shift_mix(p_ref, prev_scr, mu_ref):
        p = p_ref[...]
        rolled = pltpu.roll(p, 1, 0)
        first = jnp.where(_iota((8, p.shape[1]), 0) == 0, prev_scr[0:1, :], rolled[0:8])
        shifted = jnp.concatenate([first, rolled[8:]], axis=0)
        prev_scr[0:1, :] = p[TC - 1:TC, :]
        return p + (shifted - p) * mu_ref[...]

    CW = r_ref.shape[1]
    G = CW // W
    NQ = TC // C
    groups = lambda x: [x[:, g * W:(g + 1) * W] for g in range(G)]
    per_group = lambda f, x: jnp.concatenate([f(xg) for xg in groups(x)], axis=1)

    r = shift_mix(r_ref, pr_scr, mur_ref)
    k = shift_mix(k_ref, pk_scr, muk_ref)
    v = shift_mix(v_ref, pv_scr, muv_ref)
    sm = shift_mix(s_ref, ps_scr, mus_ref)
    xwa = sm[:, 0:SM_XG]

    bdmask = _same_head_mask(W)
    bones = jnp.where(bdmask, 1.0, 0.0)
    head_sum = lambda x: per_group(lambda xg: _dot_sel(xg, bones, 1), x)
    tri = jnp.where(_same_head_mask(TC) & (_iota((TC, TC), 1) <= _iota((TC, TC), 0)), 1.0, 0.0)
    lane = _iota((C, W), 1)
    row = _iota((C, W), 0)
    lanehead = lane >> 6
    strict = (lane & 63) < row
    incl = (lane & 63) <= row
    eye_cat = jnp.where((lane & 63) == row, 1.0, 0.0)

    wlin = w0_ref[...] + _dot3(jnp.tanh(xwa), w2h_ref[...], w2l_ref[...])
    a = jax.nn.sigmoid(a0_ref[...] + _dotb(xwa, a2_ref[...]))
    gate = _dotb(jax.nn.sigmoid(sm[:, SM_XG:SM_XG + 256]), g2_ref[...])
    z = -wlin
    softplus = jnp.maximum(z, 0.0) + jnp.log(1.0 + jnp.exp(-jnp.abs(z)))
    ld = -jnp.exp(-softplus - 0.5)
    cum = _sel_dot(tri, ld, 3)
    cum_last = jnp.concatenate(
        [jnp.broadcast_to(cum[(q + 1) * C - 1:(q + 1) * C, :], (C, CW)) for q in range(NQ)], axis=0)
    kk = k * kk_ref[...]
    kk = kk * lax.rsqrt(jnp.maximum(head_sum(kk * kk), 1e-24))
    k2 = k * (1.0 + (a - 1.0) * ka_ref[...])
    bvec = kk * a
    e_inv = jnp.exp(-cum)
    e_end = jnp.exp(cum_last - cum)
    At_f = -kk * jnp.exp(cum - ld)
    Rt_f = r * jnp.exp(cum)
    Bt_f = bvec * e_inv
    Kt_f = k2 * e_inv
    Bg_f = bvec * e_end
    Kg_f = k2 * e_end
    g_end = jnp.exp(cum_last)

    items = [(g, q) for q in range(NQ) for g in range(G)]
    blk = lambda x, g, q: x[q * C:(q + 1) * C, g * W:(g + 1) * W]
    bd01 = jnp.where(bdmask, 1.0, 0.0).astype(BF16)

    def bd(x):
        xb = x.astype(BF16)
        return jnp.concatenate([xb, xb, xb, xb], axis=0) * bd01

    At = {it: blk(At_f, *it) for it in items}
    Rt = {it: blk(Rt_f, *it) for it in items}
    Vq = {it: blk(v, *it) for it in items}
    AA = {}
    for it in items:
        bk = jnp.concatenate([jnp.where(lanehead == h, X, 0.0)
                              for X in (blk(Bt_f, *it), blk(Kt_f, *it)) for h in range(4)], axis=0)
        AA[it] = _dotb_nt(jnp.concatenate([At[it], Rt[it]], axis=0), bk)
    A_ab = {it: jnp.where(strict, AA[it][0:C, 0:W], 0.0) for it in items}
    A_ak = {it: jnp.where(strict, AA[it][0:C, W:2 * W], 0.0) for it in items}
    A_rb = {it: jnp.where(incl, AA[it][C:2 * C, 0:W], 0.0) for it in items}
    A_rk = {it: jnp.where(incl, AA[it][C:2 * C, W:2 * W], 0.0) for it in items}
    M = dict(A_ab)
    Tm = {it: eye_cat + A_ab[it] for it in items}
    for _ in range(5):
        M = {it: _dotb(M[it], bd(M[it])) for it in items}
        Tm = {it: Tm[it] + _dotb(M[it], bd(Tm[it])) for it in items}
    Vbd = {it: bd(Vq[it]) for it in items}
    akv = {it: _dotb(A_ak[it], Vbd[it]) for it in items}
    rkv = {it: _dotb(A_rk[it], Vbd[it]) for it in items}

    S = [state_scr[g] for g in range(G)]
    ys = []
    for q in range(NQ):
        its = [(g, q) for g in range(G)]
        rhs = [_dotb_nt(At[it], S[it[0]]) + akv[it] for it in its]
        U = [_dotb(Tm[it], bd(rhs[g])) for g, it in enumerate(its)]
        ys.append(jnp.concatenate(
            [_dotb_nt(Rt[it], S[g]) + _dotb(A_rb[it], bd(U[g])) + rkv[it] for g, it in enumerate(its)], axis=1))
        upd = [_dotb(jnp.concatenate([U[g], Vq[it]], axis=0).T,
                     jnp.concatenate([blk(Bg_f, *it), blk(Kg_f, *it)], axis=0)) for g, it in enumerate(its)]
        S = [S[g] * blk(g_end, g, q)[0:1, :] + jnp.where(bdmask, upd[g], 0.0) for g in range(G)]
    for g in range(G):
        state_scr[g] = S[g]
    y = jnp.concatenate(ys, axis=0)

    inv_n = 1.0 / HEAD_DIM
    d = y - head_sum(y) * inv_n
    var = head_sum(d * d) * inv_n
    yn = d * lax.rsqrt(var + RWKV_GN_EPS) * lnw_ref[...] + lnb_ref[...]
    bonus = head_sum(r * k2 * rk_ref[...]) * v
    o_ref[...] = ((yn + bonus) * gate).astype(BF16)


def _rwkv_mix(proj, pr, B, T):
    BT = proj.shape[0]
    TC = min(RW_TC, T)
    W = RW_LANES
    CW = RWKV_WIDTH
    nct = T // TC
    row = lambda b, c: b * nct + c
    full = lambda shape: pl.BlockSpec(shape, lambda b, c: (0, 0))
    vec = full((1, CW))
    in_specs = [
        pl.BlockSpec((TC, CW), lambda b, c: (row(b, c), 0)),
        pl.BlockSpec((TC, CW), lambda b, c: (row(b, c), 1)),
        pl.BlockSpec((TC, CW), lambda b, c: (row(b, c), 2)),
        pl.BlockSpec((TC, SMALL_W), lambda b, c: (row(b, c), COL_SMALL // SMALL_W)),
        vec, vec, vec,
        full((1, SMALL_W)),
        vec, vec, vec, vec, vec, vec, vec,
        full((SM_XG, CW)), full((SM_XG, CW)), full((SM_XG, CW)), full((256, CW)),
    ]
    vm = (2 * (3 * TC * CW * 4 + TC * SMALL_W * 4 + TC * CW * 2 + (3 * SM_XG + 256) * CW * 2)
          + 24 * TC * CW * 4)
    return pl.pallas_call(
        _rwkv_kernel,
        grid=(B, nct),
        in_specs=in_specs,
        out_specs=pl.BlockSpec((TC, CW), lambda b, c: (row(b, c), 0)),
        out_shape=jax.ShapeDtypeStruct((BT, CW), BF16),
        scratch_shapes=[pltpu.VMEM((8, CW), F32), pltpu.VMEM((8, CW), F32), pltpu.VMEM((8, CW), F32),
                        pltpu.VMEM((8, SMALL_W), F32), pltpu.VMEM((CW // W, W, W), F32)],
        compiler_params=_cparams(("parallel", "arbitrary"), vm),
        name="rwkv_mix",
    )(proj, proj, proj, proj, pr["mu_r"], pr["mu_k"], pr["mu_v"], pr["mu_s"],
      pr["w0"], pr["a0"], pr["k_k"], pr["k_a"], pr["r_k"], pr["ln_w"], pr["ln_b"],
      pr["w2h"], pr["w2l"], pr["a2"], pr["g2"])


NSA_TT = 256
BLOCK_LANE0 = HEAD_DIM
PADFLAG_LANE = BLOCK_LANE0 + 32


def _nsa_prep_kernel(q_ref, kv0_ref, kv1_ref, kv2_ref, s_ref, qg_ref, kgs_ref, kgw_ref, e_ref, gsel_ref,
                     qp_ref, ksl_ref, vsl_ref, kwl_ref, vwl_ref, gt_ref):
    tt = q_ref.shape[0]
    QW = NSA_GROUP * HEAD_DIM
    kv_refs = (kv0_ref, kv1_ref, kv2_ref)
    bones = jnp.where(_same_head_mask(QW), 1.0, 0.0)
    bones2 = jnp.where(_same_head_mask(LANES), 1.0, 0.0)
    lane = _iota((tt, LANES), 1)
    block_id = (pl.program_id(1) * tt + _iota((tt, LANES), 0)) >> 6
    onehot = jnp.where(lane == block_id + BLOCK_LANE0, 1.0, 0.0)
    small = _split_bf16(s_ref[...], 3)

    def pair(g, branch):
        off = (3 * g + branch) * LANES
        return kv_refs[off // SMALL_W][:, off % SMALL_W:off % SMALL_W + LANES]

    def slabs(g, x, gain, k_ref, v_ref):
        ms = _dot_sel(x * x, bones2, 1) * (1.0 / HEAD_DIM)
        k_ref[0, g] = jnp.where(lane < HEAD_DIM, x * lax.rsqrt(ms + NORM_EPS) * gain, onehot).astype(BF16)
        v_ref[0, g] = jnp.where(lane >= HEAD_DIM, x, 1.0).astype(BF16)

    for g in range(NSA_KV_HEADS):
        q = q_ref[:, g * QW:(g + 1) * QW]
        ms = _dot_sel(q * q, bones, 1) * (1.0 / HEAD_DIM)
        qn = (q * lax.rsqrt(ms + NORM_EPS) * qg_ref[...]) * (HEAD_DIM ** -0.5)
        qs = jnp.dot(qn.astype(BF16), e_ref[...], preferred_element_type=F32).astype(BF16)
        for h in range(NSA_GROUP):
            qp_ref[0, NSA_GROUP * g + h] = qs[:, h * LANES:(h + 1) * LANES]
        slabs(g, pair(g, 1), kgs_ref[...], ksl_ref, vsl_ref)
        slabs(g, pair(g, 2), kgw_ref[...], kwl_ref, vwl_ref)
        sel = gsel_ref[g].astype(BF16)
        gt_ref[0, g] = jax.nn.sigmoid(sum(jnp.dot(part, sel, preferred_element_type=F32) for part in small))


def _nsa_prep(proj, pn, B, T):
    tt = min(NSA_TT, T)
    ntt = T // tt
    G = NSA_KV_HEADS
    QW = NSA_GROUP * HEAD_DIM
    row = lambda b, t: b * ntt + t
    full = lambda shape: pl.BlockSpec(shape, lambda b, t: tuple(0 for _ in shape))
    kv_spec = lambda j: pl.BlockSpec((tt, SMALL_W), lambda b, t: (row(b, t), COL_KV // SMALL_W + j))
    vm = 2 * (tt * NSA_WIDTH * 4 + 4 * tt * SMALL_W * 4 + QW * 4 * LANES * 2 + G * SMALL_W * LANES * 4
              + 16 * tt * LANES * 2 + 16 * tt * LANES * 2 + 4 * tt * LANES * 4) + 24 * tt * QW * 4
    return pl.pallas_call(
        _nsa_prep_kernel,
        grid=(B, ntt),
        in_specs=[
            pl.BlockSpec((tt, NSA_WIDTH), lambda b, t: (row(b, t), COL_Q // NSA_WIDTH)),
            kv_spec(0), kv_spec(1), kv_spec(2),
            pl.BlockSpec((tt, SMALL_W), lambda b, t: (row(b, t), COL_SMALL // SMALL_W)),
            full((1, QW)), full((1, LANES)), full((1, LANES)), full((QW, NSA_GROUP * LANES)),
            full((G, SMALL_W, LANES)),
        ],
        out_specs=[pl.BlockSpec((1, NSA_HEADS, tt, LANES), lambda b, t: (b, 0, t, 0))]
        + [pl.BlockSpec((1, G, tt, LANES), lambda b, t: (b, 0, t, 0))] * 5,
        out_shape=[jax.ShapeDtypeStruct((B, NSA_HEADS, T, LANES), BF16)]
        + [jax.ShapeDtypeStruct((B, G, T, LANES), BF16)] * 4
        + [jax.ShapeDtypeStruct((B, G, T, LANES), F32)],
        compiler_params=_cparams(("parallel", "parallel"), vm),
        name="nsa_prep",
    )(proj, proj, proj, proj, proj, pn["q_g"], pn["kg_sel"], pn["kg_win"], pn["q_spread"], pn["gate_sel"])


def _gelu_tanh(x):
    return 0.5 * x * (1.0 + jnp.tanh(math.sqrt(2.0 / math.pi) * (x + 0.044715 * (x * x * x))))


def _nsa_compress_kernel(x_ref, pea_ref, peb_ref, wa_ref, wb_ref, w2_ref, kg_ref, kc_ref, vc_ref):
    nsub = x_ref.shape[0] // CMP_STRIDE
    xs = jnp.concatenate([x_ref[pl.ds(s, nsub, stride=CMP_STRIDE), :] for s in range(CMP_STRIDE)], axis=1)
    p0 = jnp.dot((xs + pea_ref[...]).astype(BF16), wa_ref[...], preferred_element_type=F32)
    p1 = jnp.dot((xs + peb_ref[...]).astype(BF16), wb_ref[...], preferred_element_type=F32)
    hid = _gelu_tanh(p0 + pltpu.roll(p1, nsub - 1, 0))
    out = jnp.dot(hid.astype(BF16), w2_ref[...], preferred_element_type=F32)
    bones2 = jnp.where(_same_head_mask(LANES), 1.0, 0.0)
    ms = _dot_sel(out * out, bones2, 2) * (1.0 / HEAD_DIM)
    is_k = _iota(out.shape, 1) < HEAD_DIM
    kc_ref[0, 0] = jnp.where(is_k, out * lax.rsqrt(ms + NORM_EPS) * kg_ref[...], 0.0).astype(BF16)
    vc_ref[0, 0] = jnp.where(is_k, 0.0, out).astype(BF16)


def _nsa_compress(proj, pn, B, T):
    G = NSA_KV_HEADS
    nsub = T // CMP_STRIDE
    kv_blk = COL_KV // LANES
    KW = CMP_STRIDE * LANES
    full = lambda shape: pl.BlockSpec(shape, lambda b, g: tuple(0 for _ in shape))
    vm = 2 * (T * LANES * 4 + 2 * KW * LANES * 2 + nsub * LANES * 2) + 6 * nsub * KW * 4
    return pl.pallas_call(
        _nsa_compress_kernel,
        grid=(B, G),
        in_specs=[
            pl.BlockSpec((T, LANES), lambda b, g: (b, kv_blk + 3 * g)),
            full((1, KW)), full((1, KW)), full((KW, LANES)), full((KW, LANES)), full((LANES, LANES)),
            full((1, LANES)),
        ],
        out_specs=[pl.BlockSpec((1, 1, nsub, LANES), lambda b, g: (b, g, 0, 0))] * 2,
        out_shape=[jax.ShapeDtypeStruct((B, G, nsub, LANES), BF16)] * 2,
        compiler_params=_cparams(("parallel", "parallel"), vm),
        name="nsa_compress",
    )(proj, pn["pe_a"], pn["pe_b"], pn["cw_a"], pn["cw_b"], pn["cw2"], pn["kg_cmp"])


NSA_GROUPS_PER_STEP = 4
NSA_KPAD = WINDOW
SEL_CHUNK = 4 * QUERY_BLOCK
WIN_KEYS = WINDOW + QUERY_BLOCK


def _lane_tile_max(s):
    tiles = [s[:, j * LANES:(j + 1) * LANES] for j in range(s.shape[1] // LANES)]
    while len(tiles) > 1:
        tiles = [jnp.maximum(a, b) for a, b in zip(tiles[0::2], tiles[1::2])] + ([tiles[-1]] if len(tiles) % 2 else [])
    return tiles[0]


def _nsa_attn_kernel(q_ref, kc_ref, vc_ref, ks_ref, vs_ref, kw_ref, vw_ref, gt_ref, stab_ref, wtab_ref, ctab_ref,
                     selmt_ref, gather_ref, grep_ref, o_ref, sbuf, mx_scr, acc_scr, sc_scr, *, n_sel):
    i = pl.program_id(2)
    QB = QUERY_BLOCK
    HG = NSA_GROUP
    R = HG * QB
    t0 = i * QB
    groups = range(kc_ref.shape[1])
    each = lambda f: [f(g) for g in groups]
    tile4 = lambda z: jnp.concatenate([z, z, z, z], axis=0)
    heads = lambda ref, g: ref[HG * g:HG * (g + 1)]
    lane = _iota((QB, LANES), 1)
    pad_mask = jnp.where(lane == PADFLAG_LANE, NEG_INF, 0.0)
    q = each(lambda g: q_ref[0, HG * g:HG * (g + 1)].reshape(R, LANES))
    q32 = each(lambda g: q[g].astype(F32))

    ncp = kc_ref.shape[2]
    s = each(lambda g: _dotb_nt(q[g], kc_ref[0, g]) + heads(ctab_ref, g)[:, 0].reshape(R, ncp))
    m = each(lambda g: jnp.max(s[g], axis=-1, keepdims=True))
    p_c = each(lambda g: jnp.exp(s[g] - m[g]))
    lsum = each(lambda g: jnp.sum(p_c[g], axis=-1, keepdims=True))
    p_c = each(lambda g: p_c[g] * jnp.where(m[g] > 0.5 * NEG_INF, 1.0 / lsum[g], 0.0))
    o_c = each(lambda g: jnp.dot(p_c[g].astype(BF16), vc_ref[0, g], preferred_element_type=F32))

    wrows = pl.ds(pl.multiple_of(t0, QB), WIN_KEYS)
    q_win = each(lambda g: (q32[g] + tile4(pad_mask)).astype(BF16))
    s = each(lambda g: _dotb_nt(q_win[g], kw_ref[0, g, wrows, :]) + heads(wtab_ref, g).reshape(R, WIN_KEYS))
    p = each(lambda g: jnp.exp(s[g] - jnp.max(_lane_tile_max(s[g]), axis=-1, keepdims=True)))
    acc_w = each(lambda g: jnp.dot(p[g].astype(BF16), vw_ref[0, g, wrows, :], preferred_element_type=F32))

    psum = each(lambda g: p_c[g][0:QB] + p_c[g][QB:2 * QB] + p_c[g][2 * QB:3 * QB] + p_c[g][3 * QB:4 * QB])
    selmt = selmt_ref[...]
    parts = each(lambda g: _split_bf16(psum[g], 3))
    imp = each(lambda g: sum(lax.dot_general(selmt, part, (((1,), (1,)), ((), ())), preferred_element_type=F32)
                             for part in parts[g]))
    ns = selmt.shape[0]
    blk = _iota((ns, QB), 0)
    cur = (t0 + _iota((ns, QB), 1)) >> 6
    forced = (blk == 0) | (blk == cur) | (blk == cur - 1)
    score = each(lambda g: jnp.where(forced, FORCE_SCORE, jnp.where(blk <= cur, imp[g], -1.0)))
    for g in groups:
        sc_scr[g] = score[g]
    ranks = [[] for _ in groups]
    for j in range(ns):
        lower = jnp.where(blk > j, 1.0, 0.0)
        for g in groups:
            other = sc_scr[g, j:j + 1, :]
            ranks[g].append(jnp.where(other > score[g], 1.0, 0.0) + jnp.where(other == score[g], lower, 0.0))
    while len(ranks[0]) > 1:
        ranks = [[a + b for a, b in zip(r[0::2], r[1::2])] + ([r[-1]] if len(r) % 2 else []) for r in ranks]
    chosen_t = each(lambda g: jnp.where(ranks[g][0] < n_sel, 1.0, 0.0))
    zrows = lambda n: jnp.zeros((n, QB), F32)
    chosen = each(lambda g: jnp.concatenate([zrows(BLOCK_LANE0), chosen_t[g], zrows(LANES - BLOCK_LANE0 - ns)],
                                            axis=0).T)
    is_block_lane = (lane >= BLOCK_LANE0) & (lane < BLOCK_LANE0 + ns)
    q_sel = each(lambda g: (q32[g] + tile4(jnp.where(is_block_lane, (chosen[g] - 1.0) * (-NEG_INF), pad_mask))
                            ).astype(BF16))

    def chunk_rows(c):
        return pl.ds(pl.multiple_of((i - 4 * c + 1) * QB, QB), SEL_CHUNK)

    s0 = each(lambda g: _dotb_nt(q_sel[g], ks_ref[0, g, chunk_rows(0), :]) + heads(stab_ref, g).reshape(R, SEL_CHUNK))
    for g in groups:
        sbuf[g, 0] = s0[g]
        mx_scr[g] = _lane_tile_max(s0[g])

    def scores_body(c, carry):
        s = each(lambda g: _dotb_nt(q_sel[g], ks_ref[0, g, chunk_rows(c), :]))
        for g in groups:
            sbuf[g, c] = s[g]
            mx_scr[g] = jnp.maximum(mx_scr[g], _lane_tile_max(s[g]))
        return carry

    n_far = i // 4
    lax.fori_loop(1, n_far + 1, scores_body, 0)
    m_s = each(lambda g: jnp.max(mx_scr[g], axis=-1, keepdims=True))

    acc_scr[...] = jnp.zeros_like(acc_scr)

    def values_body(c, carry):
        p = each(lambda g: jnp.exp(sbuf[g, c] - m_s[g]).astype(BF16))
        for g in groups:
            acc_scr[g] += jnp.dot(p[g], vs_ref[0, g, chunk_rows(c), :], preferred_element_type=F32)
        return carry

    lax.fori_loop(0, n_far + 1, values_body, 0)

    is_value = _iota((R, LANES), 1) >= HEAD_DIM
    normalised = lambda acc: jnp.where(is_value, acc * (1.0 / pltpu.roll(acc, HEAD_DIM, 1)), 0.0)

    def natural(o):
        cat = jnp.concatenate([o[h * QB:(h + 1) * QB] for h in range(HG)], axis=1).astype(BF16)
        return jnp.dot(cat, gather_ref[...], preferred_element_type=F32)

    branches = each(lambda g: [o_c[g], normalised(acc_scr[g]), normalised(acc_w[g])])
    gate = each(lambda g: [_dot_sel(gt_ref[0, g], grep_ref[c], 2) for c in range(3)])
    W = HG * HEAD_DIM
    for g in groups:
        out = sum(gate[g][c] * natural(branches[g][c]) for c in range(3))
        o_ref[:, g * W:(g + 1) * W] = out.astype(BF16)


def _nsa_attn(qp, kc, vc, ks, vs, kw, vw, gates, pn, B, T):
    G = NSA_KV_HEADS
    QB = QUERY_BLOCK
    NQ = T // QB
    ncp = kc.shape[2]
    ns = T // SEL_BLOCK
    TP = T + NSA_KPAD
    R = NSA_GROUP * QB
    n_sel = min(N_SEL, ns)
    n_chunks = (NQ - 1) // 4 + 1
    NG = NSA_GROUPS_PER_STEP
    HB = NSA_GROUP * NG
    once = pl.Buffered(1)
    gtab = lambda w: pl.BlockSpec((HB, QB, w), lambda b, g, i: (g, 0, 0), pipeline_mode=once)
    slab = lambda rows: pl.BlockSpec((1, NG, rows, LANES), lambda b, g, i: (b, g, 0, 0), pipeline_mode=once)
    vm = (NG * (2 * ncp * LANES * 2 + 4 * TP * LANES * 2 + R * (SEL_CHUNK + WIN_KEYS) * 4)
          + 2 * NG * (R * LANES * 2 + QB * LANES * 4 + R * ncp * 4 + QB * NSA_GROUP * HEAD_DIM * 2)
          + NG * (n_chunks * R * SEL_CHUNK * 4 + 2 * R * LANES * 4 + 4 * R * WIN_KEYS * 4))
    return pl.pallas_call(
        functools.partial(_nsa_attn_kernel, n_sel=n_sel),
        grid=(B, G // NG, NQ),
        in_specs=[
            pl.BlockSpec((1, HB, QB, LANES), lambda b, g, i: (b, g, i, 0)),
            slab(ncp), slab(ncp), slab(TP), slab(TP), slab(TP), slab(TP),
            pl.BlockSpec((1, NG, QB, LANES), lambda b, g, i: (b, g, i, 0)),
            gtab(SEL_CHUNK), gtab(WIN_KEYS),
            pl.BlockSpec((HB, 1, QB, ncp), lambda b, g, i: (g, i, 0, 0)),
            pl.BlockSpec((ns, ncp), lambda b, g, i: (0, 0)),
            pl.BlockSpec((NSA_GROUP * LANES, NSA_GROUP * HEAD_DIM), lambda b, g, i: (0, 0)),
            pl.BlockSpec((3, LANES, NSA_GROUP * HEAD_DIM), lambda b, g, i: (0, 0, 0)),
        ],
        out_specs=pl.BlockSpec((QB, HB * HEAD_DIM), lambda b, g, i: (b * NQ + i, g)),
        out_shape=jax.ShapeDtypeStruct((B * T, NSA_WIDTH), BF16),
        scratch_shapes=[pltpu.VMEM((NG, n_chunks, R, SEL_CHUNK), F32), pltpu.VMEM((NG, R, LANES), F32),
                        pltpu.VMEM((NG, R, LANES), F32), pltpu.VMEM((NG, ns, QB), F32)],
        compiler_params=_cparams(("parallel", "parallel", "arbitrary"), vm),
        name="nsa_attn",
    )(qp, kc, vc, ks, vs, kw, vw, gates, pn["stab"], pn["wtab"], pn["ctab"], pn["sel_mt"], pn["gather"],
      pn["gate_rep"])


def _rel_bucket_table():
    n = np.arange(REL_MAX_DIST + 1)
    max_exact = REL_BUCKETS // 2
    nf = np.maximum(n, max_exact).astype(np.float32)
    large = max_exact + (np.log(nf / np.float32(max_exact)) / np.float32(math.log(REL_MAX_DIST / max_exact))
                         * np.float32(REL_BUCKETS - max_exact)).astype(np.int32)
    large = np.minimum(large, REL_BUCKETS - 1)
    return np.where(n < max_exact, n, large).astype(np.int32)


def _sel_to_cmp_matrix(T, ncp):
    nc = T // CMP_STRIDE - CMP_BLOCK // CMP_STRIDE + 1
    ns = T // SEL_BLOCK
    cs = np.arange(nc) * CMP_STRIDE
    ss = np.arange(ns) * SEL_BLOCK
    lo = np.maximum(cs[None, :], ss[:, None])
    hi = np.minimum(cs[None, :] + CMP_BLOCK, ss[:, None] + SEL_BLOCK)
    out = np.zeros((ns, ncp), np.float32)
    out[:, :nc] = np.maximum(hi - lo, 0) / CMP_BLOCK
    return out


def _bias_tables(rel_bias, rel):
    bucket = _rel_bucket_table()[np.clip(rel, 0, REL_MAX_DIST)]
    onehot = (jnp.asarray(bucket.reshape(1, -1)) == jnp.arange(REL_BUCKETS, dtype=jnp.int32)[:, None]).astype(F32)
    tab = jnp.einsum('bh,bn->hn', rel_bias, onehot, precision=HIGHEST)
    return tab.reshape((rel_bias.shape[1],) + rel.shape)


def _prep_in_proj_weight(w_in_all, l):
    D = w_in_all.shape[1]
    nsa0 = RWKV_COLS
    kv0 = nsa0 + NSA_WIDTH
    gates0 = kv0 + 6 * NSA_KV_WIDTH
    merge0 = RWKV_COLS + NSA_COLS
    w_in = w_in_all[l]
    kv = w_in[:, kv0:gates0].reshape(D, 3, 2, NSA_KV_HEADS, HEAD_DIM)
    kv = jnp.transpose(kv, (0, 3, 1, 2, 4)).reshape(D, 6 * NSA_KV_WIDTH)
    pad = jnp.zeros((D, SMALL_W - (SM_GATES + 3 * NSA_HEADS)), w_in.dtype)
    return jnp.concatenate([
        w_in[:, 0:3 * RWKV_WIDTH],
        w_in[:, nsa0:kv0],
        w_in[:, merge0:merge0 + 2 * D_MODEL],
        kv,
        w_in[:, 3 * RWKV_WIDTH:RWKV_COLS],
        w_in[:, gates0:merge0],
        pad,
    ], axis=1).astype(BF16)


def _prep_rwkv_params(mu, w0, w2, a0, a2, g2, k_k, k_a, r_k, ln_w, ln_b):
    C = RWKV_WIDTH
    row = lambda z: z.reshape(1, -1).astype(F32)
    mu_s = jnp.concatenate([mu[3 * C:], jnp.zeros((SMALL_W - (RWKV_COLS - 3 * C),), F32)]).reshape(1, SMALL_W)
    zl = jnp.zeros((DECAY_LORA, C), F32)

    w2p = jnp.concatenate([w2, zl], axis=0)
    w2h = w2p.astype(BF16)
    return dict(
        mu_r=row(mu[0:C]), mu_k=row(mu[C:2 * C]), mu_v=row(mu[2 * C:3 * C]), mu_s=mu_s,
        w0=row(w0), a0=row(a0), k_k=row(k_k), k_a=row(k_a), r_k=row(r_k), ln_w=row(ln_w), ln_b=row(ln_b),
        w2h=w2h, w2l=(w2p - w2h.astype(F32)).astype(BF16),
        a2=jnp.concatenate([zl, a2], axis=0).astype(BF16),
        g2=jnp.concatenate([g2, jnp.zeros((256 - GATE_LORA, C), F32)], axis=0).astype(BF16),
    )


def _prep_nsa_params(pe_k, w1_k, w2_k, pe_v, w1_v, w2_v, q_g, k_g, rel_bias, T):
    hd = HEAD_DIM
    ones = jnp.ones((hd,), F32)
    ncp = T // CMP_STRIDE
    NQ = T // QUERY_BLOCK

    def blockdiag(a, b):
        lead = ((0, 0),) * (a.ndim - 2)
        return jnp.pad(a, lead + ((0, hd), (0, hd))) + jnp.pad(b, lead + ((hd, 0), (hd, 0)))

    w1 = blockdiag(w1_k.reshape(CMP_BLOCK, hd, hd), w1_v.reshape(CMP_BLOCK, hd, hd))
    pe = jnp.concatenate([pe_k, pe_v], axis=1)
    half = CMP_STRIDE

    spread = np.zeros((NSA_GROUP * hd, NSA_GROUP * LANES), np.float32)
    gather = np.zeros((NSA_GROUP * LANES, NSA_GROUP * hd), np.float32)
    for h in range(NSA_GROUP):
        for d in range(hd):
            spread[h * hd + d, h * LANES + d] = 1.0
            gather[h * LANES + hd + d, h * hd + d] = 1.0
    gate_sel = np.zeros((NSA_KV_HEADS, SMALL_W, LANES), np.float32)
    for g in range(NSA_KV_HEADS):
        for j in range(3 * NSA_GROUP):
            gate_sel[g, SM_GATES + 3 * NSA_GROUP * g + j, j] = 1.0
    gate_rep = np.zeros((3, LANES, NSA_GROUP * hd), np.float32)
    for c in range(3):
        for h in range(NSA_GROUP):
            gate_rep[c, 3 * h + c, h * hd:(h + 1) * hd] = 1.0

    qi = np.arange(QUERY_BLOCK)[:, None]
    rel_s = qi + (SEL_CHUNK - QUERY_BLOCK) - np.arange(SEL_CHUNK)[None, :]
    rel_w = qi + WINDOW - np.arange(WIN_KEYS)[None, :]
    rel_c = ((np.arange(NQ)[:, None, None] * QUERY_BLOCK + qi[None])
             - (np.arange(ncp)[None, None, :] * CMP_STRIDE + CMP_BLOCK - 1))
    far = np.full((QUERY_BLOCK, 1), REL_MAX_DIST)
    tabs = _bias_tables(rel_bias, np.concatenate([rel_s, rel_w, far], axis=1))
    o1, o2 = SEL_CHUNK, SEL_CHUNK + WIN_KEYS
    masked = lambda tab, ok: jnp.where(jnp.asarray(ok)[None], tab, NEG_INF)
    stab = masked(tabs[:, :, 0:o1] - tabs[:, :, o2:], rel_s >= 0)
    wtab = masked(tabs[:, :, o1:o2], (rel_w >= 0) & (rel_w < WINDOW))
    ctab = masked(_bias_tables(rel_bias, rel_c), rel_c >= 0)
    return dict(
        stab=stab, wtab=wtab, ctab=ctab,
        q_g=jnp.tile(q_g, NSA_GROUP).reshape(1, -1),
        kg_cmp=jnp.concatenate([k_g[0], ones]).reshape(1, LANES),
        kg_sel=jnp.concatenate([k_g[1], ones]).reshape(1, LANES),
        kg_win=jnp.concatenate([k_g[2], ones]).reshape(1, LANES),
        pe_a=pe[:half].reshape(1, half * LANES), pe_b=pe[half:].reshape(1, half * LANES),
        cw_a=w1[:half].reshape(half * LANES, LANES).astype(BF16),
        cw_b=w1[half:].reshape(half * LANES, LANES).astype(BF16),
        cw2=blockdiag(w2_k, w2_v).astype(BF16),
        q_spread=jnp.asarray(spread, BF16), gather=jnp.asarray(gather, BF16),
        gate_sel=jnp.asarray(gate_sel), gate_rep=jnp.asarray(gate_rep, BF16),
        sel_mt=jnp.asarray(_sel_to_cmp_matrix(T, ncp)),
    )


def _front_pad(slab, flagged):
    B, G, _, L = slab.shape
    row = jnp.zeros((L,), slab.dtype)
    if flagged:
        row = row.at[PADFLAG_LANE].set(1.0)
    return jnp.concatenate([jnp.broadcast_to(row, (B, G, NSA_KPAD, L)), slab], axis=2)


def kernel(x, c, w_ada, b_ada, norm1_g, norm2_g, w_in, rwkv_mu, rwkv_w0, rwkv_w2, rwkv_a0, rwkv_a2, rwkv_g2, rwkv_k_k, rwkv_k_a, rwkv_r_k, rwkv_ln_w, rwkv_ln_b, cmp_pe_k, cmp_w1_k, cmp_w2_k, cmp_pe_v, cmp_w1_v, cmp_w2_v, q_norm_g, k_norm_g, rel_bias, w_o_rwkv, w_o_nsa, w_out, w_up, w_down):
    B, T, D = x.shape
    depth = w_in.shape[0]
    x2 = x.reshape(B * T, D)
    for l in range(depth):
        mod6 = _ada_mod(c, w_ada[l], b_ada[l]).reshape(B * 6, 1, D)
        proj = _in_proj(x2, norm1_g[l].reshape(1, D), mod6, _prep_in_proj_weight(w_in, l), T)
        pr = _prep_rwkv_params(rwkv_mu[l], rwkv_w0[l], rwkv_w2[l], rwkv_a0[l], rwkv_a2[l], rwkv_g2[l],
                               rwkv_k_k[l], rwkv_k_a[l], rwkv_r_k[l], rwkv_ln_w[l], rwkv_ln_b[l])
        o_a = _rwkv_mix(proj, pr, B, T)
        pn = _prep_nsa_params(cmp_pe_k[l], cmp_w1_k[l], cmp_w2_k[l], cmp_pe_v[l], cmp_w1_v[l], cmp_w2_v[l],
                              q_norm_g[l], k_norm_g[l], rel_bias, T)
        qp, ks, vs, kw, vw, gates = _nsa_prep(proj, pn, B, T)
        ks, vs, kw, vw = _front_pad(ks, True), _front_pad(vs, False), _front_pad(kw, True), _front_pad(vw, False)
        kc, vc = _nsa_compress(proj, pn, B, T)
        o_b = _nsa_attn(qp, kc, vc, ks, vs, kw, vw, gates, pn, B, T)
        mixed = _merge(o_a, o_b, w_o_rwkv[l].astype(BF16), w_o_nsa[l].astype(BF16), proj)
        x1, h2 = _out_proj(mixed, w_out[l].astype(BF16), x2, mod6, norm2_g[l].reshape(1, D), T)
        x2 = _mlp(h2, w_up[l].astype(BF16), w_down[l].astype(BF16), x1, mod6, T)
    return x2.reshape(B, T, D)
```

```python
import functools
import math

import numpy as np
import jax
import jax.numpy as jnp
from jax import lax
from jax.experimental import pallas as pl
from jax.experimental.pallas import tpu as pltpu

F32 = jnp.float32
BF16 = jnp.bfloat16
HIGHEST = lax.Precision.HIGHEST

D_MODEL = 2048
HEAD_DIM = 64
RWKV_WIDTH = D_MODEL // 2
DECAY_LORA = 64
ICLR_LORA = 64
GATE_LORA = 160
RWKV_GN_EPS = 64e-5
NSA_WIDTH = D_MODEL // 2
NSA_HEADS = NSA_WIDTH // HEAD_DIM
NSA_KV_HEADS = 4
NSA_GROUP = NSA_HEADS // NSA_KV_HEADS
NSA_KV_WIDTH = NSA_KV_HEADS * HEAD_DIM
CMP_BLOCK = 32
CMP_STRIDE = 16
SEL_BLOCK = 64
N_SEL = 8
WINDOW = 512
QUERY_BLOCK = 128
REL_BUCKETS = 32
REL_MAX_DIST = 128
D_FF = 4 * D_MODEL
NORM_EPS = 1e-6
NEG_INF = -1e30
FORCE_SCORE = 1e4

RWKV_COLS = 3 * RWKV_WIDTH + DECAY_LORA + ICLR_LORA + GATE_LORA
NSA_COLS = NSA_WIDTH + 6 * NSA_KV_WIDTH + 3 * NSA_HEADS

V7X_VMEM_BYTES = 64 * 1024 * 1024
LANES = 128

COL_RKV = 0
COL_Q = 3 * RWKV_WIDTH
COL_MERGE = COL_Q + NSA_WIDTH
COL_KV = COL_MERGE + 2 * D_MODEL
COL_SMALL = COL_KV + 6 * NSA_KV_WIDTH
SMALL_W = 512
PROJ_COLS = COL_SMALL + SMALL_W
SM_XG = DECAY_LORA + ICLR_LORA
SM_GATES = SM_XG + GATE_LORA


def _vmem_limit(nbytes):
    return int(min(nbytes * 5 // 4 + (4 << 20), V7X_VMEM_BYTES - (8 << 20)))


def _cparams(sem, vmem_bytes):
    return pltpu.CompilerParams(dimension_semantics=sem, vmem_limit_bytes=_vmem_limit(vmem_bytes))


def _ada_kernel(c_ref, w_ref, b_ref, o_ref):
    c = c_ref[...]
    s = c * jax.nn.sigmoid(c)
    o_ref[...] = jnp.dot(s.astype(BF16), w_ref[...].astype(BF16), preferred_element_type=F32) + b_ref[...]


def _ada_mod(c, w_ada, b_ada):
    B, D = c.shape
    N = w_ada.shape[1]
    tn = 1024
    return pl.pallas_call(
        _ada_kernel,
        grid=(N // tn,),
        in_specs=[
            pl.BlockSpec((B, D), lambda j: (0, 0)),
            pl.BlockSpec((D, tn), lambda j: (0, j)),
            pl.BlockSpec((1, tn), lambda j: (0, j)),
        ],
        out_specs=pl.BlockSpec((B, tn), lambda j: (0, j)),
        out_shape=jax.ShapeDtypeStruct((B, N), F32),
        compiler_params=_cparams(("parallel",), 2 * D * tn * 4 + D * tn * 2),
        name="ada_mod",
    )(c, w_ada, b_ada.reshape(1, N))


def _modulated_norm(x, g, sc, sh):
    ms = jnp.mean(x * x, axis=-1, keepdims=True)
    return (x * lax.rsqrt(ms + NORM_EPS) * g) * (1.0 + sc) + sh


def _inproj_kernel(x_ref, g_ref, sh_ref, sc_ref, w_ref, o_ref, h_scr):
    @pl.when(pl.program_id(1) == 0)
    def _():
        h_scr[...] = _modulated_norm(x_ref[...], g_ref[...], sc_ref[0], sh_ref[0]).astype(BF16)

    o_ref[...] = jnp.dot(h_scr[...], w_ref[...], preferred_element_type=F32)


def _in_proj(x2, g1, mod6, w_in_p, T):
    BT, D = x2.shape
    NP = w_in_p.shape[1]
    tm = min(1024, T)
    tn = 1024
    tpb = T // tm
    vm = 2 * tm * D * 4 + tm * D * 2 + 2 * D * tn * 2 + 2 * tm * tn * 4 + 2 * tm * D * 4
    return pl.pallas_call(
        _inproj_kernel,
        grid=(BT // tm, NP // tn),
        in_specs=[
            pl.BlockSpec((tm, D), lambda i, j: (i, 0)),
            pl.BlockSpec((1, D), lambda i, j: (0, 0)),
            pl.BlockSpec((1, 1, D), lambda i, j: ((i // tpb) * 6 + 0, 0, 0)),
            pl.BlockSpec((1, 1, D), lambda i, j: ((i // tpb) * 6 + 1, 0, 0)),
            pl.BlockSpec((D, tn), lambda i, j: (0, j)),
        ],
        out_specs=pl.BlockSpec((tm, tn), lambda i, j: (i, j)),
        out_shape=jax.ShapeDtypeStruct((BT, NP), F32),
        scratch_shapes=[pltpu.VMEM((tm, D), BF16)],
        compiler_params=_cparams(("parallel", "arbitrary"), vm),
        name="in_proj",
    )(x2, g1, mod6, mod6, w_in_p)


def _merge_kernel(oa_ref, ob_ref, wa_ref, wb_ref, ga_ref, gb_ref, o_ref):
    ya = jnp.dot(oa_ref[...], wa_ref[...], preferred_element_type=F32)
    yb = jnp.dot(ob_ref[...], wb_ref[...], preferred_element_type=F32)
    o_ref[...] = (jax.nn.sigmoid(ga_ref[...]) * ya + jax.nn.sigmoid(gb_ref[...]) * yb).astype(BF16)


def _merge(o_a, o_b, w_oa, w_ob, proj):
    BT, W = o_a.shape
    D = w_oa.shape[1]
    tm, tn = 512, 1024
    ga0 = COL_MERGE // tn
    gb0 = (COL_MERGE + D) // tn
    vm = 2 * (2 * tm * W * 2 + 2 * W * tn * 2 + 2 * tm * tn * 4 + tm * tn * 2) + 3 * tm * tn * 4
    return pl.pallas_call(
        _merge_kernel,
        grid=(BT // tm, D // tn),
        in_specs=[
            pl.BlockSpec((tm, W), lambda i, j: (i, 0)),
            pl.BlockSpec((tm, W), lambda i, j: (i, 0)),
            pl.BlockSpec((W, tn), lambda i, j: (0, j)),
            pl.BlockSpec((W, tn), lambda i, j: (0, j)),
            pl.BlockSpec((tm, tn), lambda i, j: (i, ga0 + j)),
            pl.BlockSpec((tm, tn), lambda i, j: (i, gb0 + j)),
        ],
        out_specs=pl.BlockSpec((tm, tn), lambda i, j: (i, j)),
        out_shape=jax.ShapeDtypeStruct((BT, D), BF16),
        compiler_params=_cparams(("parallel", "parallel"), vm),
        name="merge",
    )(o_a, o_b, w_oa, w_ob, proj, proj)


def _outproj_kernel(m_ref, w_ref, x_ref, gt_ref, g_ref, sh_ref, sc_ref, x1_ref, h2_ref):
    y = jnp.dot(m_ref[...], w_ref[...], preferred_element_type=F32)
    x1 = x_ref[...] + gt_ref[0] * y
    x1_ref[...] = x1
    h2_ref[...] = _modulated_norm(x1, g_ref[...], sc_ref[0], sh_ref[0]).astype(BF16)


def _out_proj(mixed, w_out, x2, mod6, g2, T):
    BT, D = x2.shape
    tm = min(512, T)
    tpb = T // tm
    vm = 2 * (tm * D * 2 + D * D * 2 + tm * D * 4 + tm * D * 4 + tm * D * 2) + 3 * tm * D * 4
    mod_spec = lambda k: pl.BlockSpec((1, 1, D), lambda i: ((i // tpb) * 6 + k, 0, 0))
    return pl.pallas_call(
        _outproj_kernel,
        grid=(BT // tm,),
        in_specs=[
            pl.BlockSpec((tm, D), lambda i: (i, 0)),
            pl.BlockSpec((D, D), lambda i: (0, 0)),
            pl.BlockSpec((tm, D), lambda i: (i, 0)),
            mod_spec(2),
            pl.BlockSpec((1, D), lambda i: (0, 0)),
            mod_spec(3),
            mod_spec(4),
        ],
        out_specs=[pl.BlockSpec((tm, D), lambda i: (i, 0)), pl.BlockSpec((tm, D), lambda i: (i, 0))],
        out_shape=[jax.ShapeDtypeStruct((BT, D), F32), jax.ShapeDtypeStruct((BT, D), BF16)],
        compiler_params=_cparams(("parallel",), vm),
        name="out_proj",
    )(mixed, w_out, x2, mod6, g2, mod6, mod6)


def _mlp_kernel(h_ref, wu_ref, wd_ref, x_ref, gt_ref, o_ref, acc_ref):
    f = pl.program_id(1)
    u = jnp.dot(h_ref[...], wu_ref[...], preferred_element_type=F32)
    u = jnp.square(jnp.maximum(u, 0.0)).astype(BF16)
    part = jnp.dot(u, wd_ref[...], preferred_element_type=F32)

    @pl.when(f == 0)
    def _():
        acc_ref[...] = part

    @pl.when(f > 0)
    def _():
        acc_ref[...] += part

    @pl.when(f == pl.num_programs(1) - 1)
    def _():
        o_ref[...] = x_ref[...] + gt_ref[0] * acc_ref[...]


def _mlp(h2, w_up, w_down, x1, mod6, T):
    BT, D = x1.shape
    F = w_up.shape[1]
    tm = min(512, T)
    tf = 1024
    tpb = T // tm
    vm = 2 * (tm * D * 2 + 2 * D * tf * 2 + 2 * tm * D * 4) + tm * D * 4 + 2 * tm * tf * 4
    return pl.pallas_call(
        _mlp_kernel,
        grid=(BT // tm, F // tf),
        in_specs=[
            pl.BlockSpec((tm, D), lambda i, f: (i, 0)),
            pl.BlockSpec((D, tf), lambda i, f: (0, f)),
            pl.BlockSpec((tf, D), lambda i, f: (f, 0)),
            pl.BlockSpec((tm, D), lambda i, f: (i, 0)),
            pl.BlockSpec((1, 1, D), lambda i, f: ((i // tpb) * 6 + 5, 0, 0)),
        ],
        out_specs=pl.BlockSpec((tm, D), lambda i, f: (i, 0)),
        out_shape=jax.ShapeDtypeStruct((BT, D), F32),
        scratch_shapes=[pltpu.VMEM((tm, D), F32)],
        compiler_params=_cparams(("parallel", "arbitrary"), vm),
        name="mlp",
    )(h2, w_up, w_down, x1, mod6)


def _dotb(a, b):
    return jnp.dot(a.astype(BF16), b.astype(BF16), preferred_element_type=F32)


def _dotb_nt(a, b):
    return lax.dot_general(a.astype(BF16), b.astype(BF16), (((1,), (1,)), ((), ())),
                           preferred_element_type=F32)


def _split_bf16(x, terms):
    parts, rem = [], x
    for t in range(terms):
        p = rem.astype(BF16)
        parts.append(p)
        if t + 1 < terms:
            rem = rem - p.astype(F32)
    return parts


def _dot_sel(x, sel, terms):
    sel = sel.astype(BF16)
    return sum(jnp.dot(p, sel, preferred_element_type=F32) for p in _split_bf16(x, terms))


def _sel_dot(sel, x, terms):
    sel = sel.astype(BF16)
    return sum(jnp.dot(sel, p, preferred_element_type=F32) for p in _split_bf16(x, terms))


def _dot3(a, b_hi, b_lo):
    a_hi, a_lo = _split_bf16(a, 2)
    return (jnp.dot(a_hi, b_hi, preferred_element_type=F32) + jnp.dot(a_lo, b_hi, preferred_element_type=F32)
            + jnp.dot(a_hi, b_lo, preferred_element_type=F32))


def _iota(shape, axis):
    return lax.broadcasted_iota(jnp.int32, shape, axis)


def _same_head_mask(n):
    return (_iota((n, n), 0) >> 6) == (_iota((n, n), 1) >> 6)


RW_TC = 256
RW_C = 64
RW_LANES = 4 * HEAD_DIM


def _block_diag(x, bdmask):
    return jnp.where(bdmask, jnp.concatenate([x, x, x, x], axis=0), 0.0)


def _rwkv_kernel(r_ref, k_ref, v_ref, s_ref, mur_ref, muk_ref, muv_ref, mus_ref,
                 w0_ref, a0_ref, kk_ref, ka_ref, rk_ref, lnw_ref, lnb_ref,
                 w2h_ref, w2l_ref, a2_ref, g2_ref, o_ref,
                 pr_scr, pk_scr, pv_scr, ps_scr, state_scr):
    TC = r_ref.shape[0]
    C = RW_C
    W = RW_LANES

    @pl.when(pl.program_id(1) == 0)
    def _():
        pr_scr[...] = jnp.zeros_like(pr_scr)
        pk_scr[...] = jnp.zeros_like(pk_scr)
        pv_scr[...] = jnp.zeros_like(pv_scr)
        ps_scr[...] = jnp.zeros_like(ps_scr)
        state_scr[...] = jnp.zeros_like(state_scr)

    def shift_mix(p_ref, prev_scr, mu_ref):
        p = p_ref[...]
        rolled = pltpu.roll(p, 1, 0)
        first = jnp.where(_iota((8, p.shape[1]), 0) == 0, prev_scr[0:1, :], rolled[0:8])
        shifted = jnp.concatenate([first, rolled[8:]], axis=0)
        prev_scr[0:1, :] = p[TC - 1:TC, :]
        return p + (shifted - p) * mu_ref[...]

    CW = r_ref.shape[1]
    G = CW // W
    NQ = TC // C
    groups = lambda x: [x[:, g * W:(g + 1) * W] for g in range(G)]
    per_group = lambda f, x: jnp.concatenate([f(xg) for xg in groups(x)], axis=1)

    r = shift_mix(r_ref, pr_scr, mur_ref)
    k = shift_mix(k_ref, pk_scr, muk_ref)
    v = shift_mix(v_ref, pv_scr, muv_ref)
    sm = shift_mix(s_ref, ps_scr, mus_ref)
    xwa = sm[:, 0:SM_XG]

    bdmask = _same_head_mask(W)
    bones = jnp.where(bdmask, 1.0, 0.0)
    head_sum = lambda x: per_group(lambda xg: _dot_sel(xg, bones, 1), x)
    tri = jnp.where(_same_head_mask(TC) & (_iota((TC, TC), 1) <= _iota((TC, TC), 0)), 1.0, 0.0)
    lane = _iota((C, W), 1)
    row = _iota((C, W), 0)
    lanehead = lane >> 6
    strict = (lane & 63) < row
    incl = (lane & 63) <= row
    eye_cat = jnp.where((lane & 63) == row, 1.0, 0.0)

    wlin = w0_ref[...] + _dot3(jnp.tanh(xwa), w2h_ref[...], w2l_ref[...])
    a = jax.nn.sigmoid(a0_ref[...] + _dotb(xwa, a2_ref[...]))
    gate = _dotb(jax.nn.sigmoid(sm[:, SM_XG:SM_XG + 256]), g2_ref[...])
    z = -wlin
    softplus = jnp.maximum(z, 0.0) + jnp.log(1.0 + jnp.exp(-jnp.abs(z)))
    ld = -jnp.exp(-softplus - 0.5)
    cum = _sel_dot(tri, ld, 3)
    cum_last = jnp.concatenate(
        [jnp.broadcast_to(cum[(q + 1) * C - 1:(q + 1) * C, :], (C, CW)) for q in range(NQ)], axis=0)
    kk = k * kk_ref[...]
    kk = kk * lax.rsqrt(jnp.maximum(head_sum(kk * kk), 1e-24))
    k2 = k * (1.0 + (a - 1.0) * ka_ref[...])
    bvec = kk * a
    e_inv = jnp.exp(-cum)
    e_end = jnp.exp(cum_last - cum)
    At_f = -kk * jnp.exp(cum - ld)
    Rt_f = r * jnp.exp(cum)
    Bt_f = bvec * e_inv
    Kt_f = k2 * e_inv
    Bg_f = bvec * e_end
    Kg_f = k2 * e_end
    g_end = jnp.exp(cum_last)

    items = [(g, q) for q in range(NQ) for g in range(G)]
    blk = lambda x, g, q: x[q * C:(q + 1) * C, g * W:(g + 1) * W]
    bd01 = jnp.where(bdmask, 1.0, 0.0).astype(BF16)

    def bd(x):
        xb = x.astype(BF16)
        return jnp.concatenate([xb, xb, xb, xb], axis=0) * bd01

    At = {it: blk(At_f, *it) for it in items}
    Rt = {it: blk(Rt_f, *it) for it in items}
    Vq = {it: blk(v, *it) for it in items}
    AA = {}
    for it in items:
        bk = jnp.concatenate([jnp.where(lanehead == h, X, 0.0)
                              for X in (blk(Bt_f, *it), blk(Kt_f, *it)) for h in range(4)], axis=0)
        AA[it] = _dotb_nt(jnp.concatenate([At[it], Rt[it]], axis=0), bk)
    A_ab = {it: jnp.where(strict, AA[it][0:C, 0:W], 0.0) for it in items}
    A_ak = {it: jnp.where(strict, AA[it][0:C, W:2 * W], 0.0) for it in items}
    A_rb = {it: jnp.where(incl, AA[it][C:2 * C, 0:W], 0.0) for it in items}
    A_rk = {it: jnp.where(incl, AA[it][C:2 * C, W:2 * W], 0.0) for it in items}
    M = dict(A_ab)
    Tm = {it: eye_cat + A_ab[it] for it in items}
    for _ in range(5):
        M = {it: _dotb(M[it], bd(M[it])) for it in items}
        Tm = {it: Tm[it] + _dotb(M[it], bd(Tm[it])) for it in items}
    Vbd = {it: bd(Vq[it]) for it in items}
    akv = {it: _dotb(A_ak[it], Vbd[it]) for it in items}
    rkv = {it: _dotb(A_rk[it], Vbd[it]) for it in items}

    S = [state_scr[g] for g in range(G)]
    ys = []
    for q in range(NQ):
        its = [(g, q) for g in range(G)]
        rhs = [_dotb_nt(At[it], S[it[0]]) + akv[it] for it in its]
        U = [_dotb(Tm[it], bd(rhs[g])) for g, it in enumerate(its)]
        ys.append(jnp.concatenate(
            [_dotb_nt(Rt[it], S[g]) + _dotb(A_rb[it], bd(U[g])) + rkv[it] for g, it in enumerate(its)], axis=1))
        upd = [_dotb(jnp.concatenate([U[g], Vq[it]], axis=0).T,
                     jnp.concatenate([blk(Bg_f, *it), blk(Kg_f, *it)], axis=0)) for g, it in enumerate(its)]
        S = [S[g] * blk(g_end, g, q)[0:1, :] + jnp.where(bdmask, upd[g], 0.0) for g in range(G)]
    for g in range(G):
        state_scr[g] = S[g]
    y = jnp.concatenate(ys, axis=0)

    inv_n = 1.0 / HEAD_DIM
    d = y - head_sum(y) * inv_n
    var = head_sum(d * d) * inv_n
    yn = d * lax.rsqrt(var + RWKV_GN_EPS) * lnw_ref[...] + lnb_ref[...]
    bonus = head_sum(r * k2 * rk_ref[...]) * v
    o_ref[...] = ((yn + bonus) * gate).astype(BF16)


def _rwkv_mix(proj, pr, B, T):
    BT = proj.shape[0]
    TC = min(RW_TC, T)
    W = RW_LANES
    CW = RWKV_WIDTH
    nct = T // TC
    row = lambda b, c: b * nct + c
    full = lambda shape: pl.BlockSpec(shape, lambda b, c: (0, 0))
    vec = full((1, CW))
    in_specs = [
        pl.BlockSpec((TC, CW), lambda b, c: (row(b, c), 0)),
        pl.BlockSpec((TC, CW), lambda b, c: (row(b, c), 1)),
        pl.BlockSpec((TC, CW), lambda b, c: (row(b, c), 2)),
        pl.BlockSpec((TC, SMALL_W), lambda b, c: (row(b, c), COL_SMALL // SMALL_W)),
        vec, vec, vec,
        full((1, SMALL_W)),
        vec, vec, vec, vec, vec, vec, vec,
        full((SM_XG, CW)), full((SM_XG, CW)), full((SM_XG, CW)), full((256, CW)),
    ]
    vm = (2 * (3 * TC * CW * 4 + TC * SMALL_W * 4 + TC * CW * 2 + (3 * SM_XG + 256) * CW * 2)
          + 24 * TC * CW * 4)
    return pl.pallas_call(
        _rwkv_kernel,
        grid=(B, nct),
        in_specs=in_specs,
        out_specs=pl.BlockSpec((TC, CW), lambda b, c: (row(b, c), 0)),
        out_shape=jax.ShapeDtypeStruct((BT, CW), BF16),
        scratch_shapes=[pltpu.VMEM((8, CW), F32), pltpu.VMEM((8, CW), F32), pltpu.VMEM((8, CW), F32),
                        pltpu.VMEM((8, SMALL_W), F32), pltpu.VMEM((CW // W, W, W), F32)],
        compiler_params=_cparams(("parallel", "arbitrary"), vm),
        name="rwkv_mix",
    )(proj, proj, proj, proj, pr["mu_r"], pr["mu_k"], pr["mu_v"], pr["mu_s"],
      pr["w0"], pr["a0"], pr["k_k"], pr["k_a"], pr["r_k"], pr["ln_w"], pr["ln_b"],
      pr["w2h"], pr["w2l"], pr["a2"], pr["g2"])


NSA_TT = 256
BLOCK_LANE0 = HEAD_DIM
PADFLAG_LANE = BLOCK_LANE0 + 32


def _nsa_prep_kernel(q_ref, kv0_ref, kv1_ref, kv2_ref, s_ref, qg_ref, kgs_ref, kgw_ref, e_ref, gsel_ref,
                     qp_ref, ksl_ref, vsl_ref, kwl_ref, vwl_ref, gt_ref):
    tt = q_ref.shape[0]
    QW = NSA_GROUP * HEAD_DIM
    kv_refs = (kv0_ref, kv1_ref, kv2_ref)
    bones = jnp.where(_same_head_mask(QW), 1.0, 0.0)
    bones2 = jnp.where(_same_head_mask(LANES), 1.0, 0.0)
    lane = _iota((tt, LANES), 1)
    block_id = (pl.program_id(1) * tt + _iota((tt, LANES), 0)) >> 6
    onehot = jnp.where(lane == block_id + BLOCK_LANE0, 1.0, 0.0)
    small = _split_bf16(s_ref[...], 3)

    def pair(g, branch):
        off = (3 * g + branch) * LANES
        return kv_refs[off // SMALL_W][:, off % SMALL_W:off % SMALL_W + LANES]

    def slabs(g, x, gain, k_ref, v_ref):
        ms = _dot_sel(x * x, bones2, 1) * (1.0 / HEAD_DIM)
        k_ref[0, g] = jnp.where(lane < HEAD_DIM, x * lax.rsqrt(ms + NORM_EPS) * gain, onehot).astype(BF16)
        v_ref[0, g] = jnp.where(lane >= HEAD_DIM, x, 1.0).astype(BF16)

    for g in range(NSA_KV_HEADS):
        q = q_ref[:, g * QW:(g + 1) * QW]
        ms = _dot_sel(q * q, bones, 1) * (1.0 / HEAD_DIM)
        qn = (q * lax.rsqrt(ms + NORM_EPS) * qg_ref[...]) * (HEAD_DIM ** -0.5)
        qs = jnp.dot(qn.astype(BF16), e_ref[...], preferred_element_type=F32).astype(BF16)
        for h in range(NSA_GROUP):
            qp_ref[0, NSA_GROUP * g + h] = qs[:, h * LANES:(h + 1) * LANES]
        slabs(g, pair(g, 1), kgs_ref[...], ksl_ref, vsl_ref)
        slabs(g, pair(g, 2), kgw_ref[...], kwl_ref, vwl_ref)
        sel = gsel_ref[g].astype(BF16)
        gt_ref[0, g] = jax.nn.sigmoid(sum(jnp.dot(part, sel, preferred_element_type=F32) for part in small))


def _nsa_prep(proj, pn, B, T):
    tt = min(NSA_TT, T)
    ntt = T // tt
    G = NSA_KV_HEADS
    QW = NSA_GROUP * HEAD_DIM
    row = lambda b, t: b * ntt + t
    full = lambda shape: pl.BlockSpec(shape, lambda b, t: tuple(0 for _ in shape))
    kv_spec = lambda j: pl.BlockSpec((tt, SMALL_W), lambda b, t: (row(b, t), COL_KV // SMALL_W + j))
    vm = 2 * (tt * NSA_WIDTH * 4 + 4 * tt * SMALL_W * 4 + QW * 4 * LANES * 2 + G * SMALL_W * LANES * 4
              + 16 * tt * LANES * 2 + 16 * tt * LANES * 2 + 4 * tt * LANES * 4) + 24 * tt * QW * 4
    return pl.pallas_call(
        _nsa_prep_kernel,
        grid=(B, ntt),
        in_specs=[
            pl.BlockSpec((tt, NSA_WIDTH), lambda b, t: (row(b, t), COL_Q // NSA_WIDTH)),
            kv_spec(0), kv_spec(1), kv_spec(2),
            pl.BlockSpec((tt, SMALL_W), lambda b, t: (row(b, t), COL_SMALL // SMALL_W)),
            full((1, QW)), full((1, LANES)), full((1, LANES)), full((QW, NSA_GROUP * LANES)),
            full((G, SMALL_W, LANES)),
        ],
        out_specs=[pl.BlockSpec((1, NSA_HEADS, tt, LANES), lambda b, t: (b, 0, t, 0))]
        + [pl.BlockSpec((1, G, tt, LANES), lambda b, t: (b, 0, t, 0))] * 5,
        out_shape=[jax.ShapeDtypeStruct((B, NSA_HEADS, T, LANES), BF16)]
        + [jax.ShapeDtypeStruct((B, G, T, LANES), BF16)] * 4
        + [jax.ShapeDtypeStruct((B, G, T, LANES), F32)],
        compiler_params=_cparams(("parallel", "parallel"), vm),
        name="nsa_prep",
    )(proj, proj, proj, proj, proj, pn["q_g"], pn["kg_sel"], pn["kg_win"], pn["q_spread"], pn["gate_sel"])


def _gelu_tanh(x):
    return 0.5 * x * (1.0 + jnp.tanh(math.sqrt(2.0 / math.pi) * (x + 0.044715 * (x * x * x))))


def _nsa_compress_kernel(x_ref, pea_ref, peb_ref, wa_ref, wb_ref, w2_ref, kg_ref, kc_ref, vc_ref):
    nsub = x_ref.shape[0] // CMP_STRIDE
    xs = jnp.concatenate([x_ref[pl.ds(s, nsub, stride=CMP_STRIDE), :] for s in range(CMP_STRIDE)], axis=1)
    p0 = jnp.dot((xs + pea_ref[...]).astype(BF16), wa_ref[...], preferred_element_type=F32)
    p1 = jnp.dot((xs + peb_ref[...]).astype(BF16), wb_ref[...], preferred_element_type=F32)
    hid = _gelu_tanh(p0 + pltpu.roll(p1, nsub - 1, 0))
    out = jnp.dot(hid.astype(BF16), w2_ref[...], preferred_element_type=F32)
    bones2 = jnp.where(_same_head_mask(LANES), 1.0, 0.0)
    ms = _dot_sel(out * out, bones2, 2) * (1.0 / HEAD_DIM)
    is_k = _iota(out.shape, 1) < HEAD_DIM
    kc_ref[0, 0] = jnp.where(is_k, out * lax.rsqrt(ms + NORM_EPS) * kg_ref[...], 0.0).astype(BF16)
    vc_ref[0, 0] = jnp.where(is_k, 0.0, out).astype(BF16)


def _nsa_compress(proj, pn, B, T):
    G = NSA_KV_HEADS
    nsub = T // CMP_STRIDE
    kv_blk = COL_KV // LANES
    KW = CMP_STRIDE * LANES
    full = lambda shape: pl.BlockSpec(shape, lambda b, g: tuple(0 for _ in shape))
    vm = 2 * (T * LANES * 4 + 2 * KW * LANES * 2 + nsub * LANES * 2) + 6 * nsub * KW * 4
    return pl.pallas_call(
        _nsa_compress_kernel,
        grid=(B, G),
        in_specs=[
            pl.BlockSpec((T, LANES), lambda b, g: (b, kv_blk + 3 * g)),
            full((1, KW)), full((1, KW)), full((KW, LANES)), full((KW, LANES)), full((LANES, LANES)),
            full((1, LANES)),
        ],
        out_specs=[pl.BlockSpec((1, 1, nsub, LANES), lambda b, g: (b, g, 0, 0))] * 2,
        out_shape=[jax.ShapeDtypeStruct((B, G, nsub, LANES), BF16)] * 2,
        compiler_params=_cparams(("parallel", "parallel"), vm),
        name="nsa_compress",
    )(proj, pn["pe_a"], pn["pe_b"], pn["cw_a"], pn["cw_b"], pn["cw2"], pn["kg_cmp"])


NSA_GROUPS_PER_STEP = 4
NSA_KPAD = WINDOW
SEL_CHUNK = 4 * QUERY_BLOCK
WIN_KEYS = WINDOW + QUERY_BLOCK


def _lane_tile_max(s):
    tiles = [s[:, j * LANES:(j + 1) * LANES] for j in range(s.shape[1] // LANES)]
    while len(tiles) > 1:
        tiles = [jnp.maximum(a, b) for a, b in zip(tiles[0::2], tiles[1::2])] + ([tiles[-1]] if len(tiles) % 2 else [])
    return tiles[0]


def _nsa_attn_kernel(q_ref, kc_ref, vc_ref, ks_ref, vs_ref, kw_ref, vw_ref, gt_ref, stab_ref, wtab_ref, ctab_ref,
                     selmt_ref, gather_ref, grep_ref, o_ref, sbuf, mx_scr, acc_scr, sc_scr, *, n_sel):
    i = pl.program_id(2)
    QB = QUERY_BLOCK
    HG = NSA_GROUP
    R = HG * QB
    t0 = i * QB
    groups = range(kc_ref.shape[1])
    each = lambda f: [f(g) for g in groups]
    tile4 = lambda z: jnp.concatenate([z, z, z, z], axis=0)
    heads = lambda ref, g: ref[HG * g:HG * (g + 1)]
    lane = _iota((QB, LANES), 1)
    pad_mask = jnp.where(lane == PADFLAG_LANE, NEG_INF, 0.0)
    q = each(lambda g: q_ref[0, HG * g:HG * (g + 1)].reshape(R, LANES))
    q32 = each(lambda g: q[g].astype(F32))

    ncp = kc_ref.shape[2]
    s = each(lambda g: _dotb_nt(q[g], kc_ref[0, g]) + heads(ctab_ref, g)[:, 0].reshape(R, ncp))
    m = each(lambda g: jnp.max(s[g], axis=-1, keepdims=True))
    p_c = each(lambda g: jnp.exp(s[g] - m[g]))
    lsum = each(lambda g: jnp.sum(p_c[g], axis=-1, keepdims=True))
    p_c = each(lambda g: p_c[g] * jnp.where(m[g] > 0.5 * NEG_INF, 1.0 / lsum[g], 0.0))
    o_c = each(lambda g: jnp.dot(p_c[g].astype(BF16), vc_ref[0, g], preferred_element_type=F32))

    wrows = pl.ds(pl.multiple_of(t0, QB), WIN_KEYS)
    q_win = each(lambda g: (q32[g] + tile4(pad_mask)).astype(BF16))
    s = each(lambda g: _dotb_nt(q_win[g], kw_ref[0, g, wrows, :]) + heads(wtab_ref, g).reshape(R, WIN_KEYS))
    p = each(lambda g: jnp.exp(s[g] - jnp.max(_lane_tile_max(s[g]), axis=-1, keepdims=True)))
    acc_w = each(lambda g: jnp.dot(p[g].astype(BF16), vw_ref[0, g, wrows, :], preferred_element_type=F32))

    psum = each(lambda g: p_c[g][0:QB] + p_c[g][QB:2 * QB] + p_c[g][2 * QB:3 * QB] + p_c[g][3 * QB:4 * QB])
    selmt = selmt_ref[...]
    parts = each(lambda g: _split_bf16(psum[g], 3))
    imp = each(lambda g: sum(lax.dot_general(selmt, part, (((1,), (1,)), ((), ())), preferred_element_type=F32)
                             for part in parts[g]))
    ns = selmt.shape[0]
    blk = _iota((ns, QB), 0)
    cur = (t0 + _iota((ns, QB), 1)) >> 6
    forced = (blk == 0) | (blk == cur) | (blk == cur - 1)
    score = each(lambda g: jnp.where(forced, FORCE_SCORE, jnp.where(blk <= cur, imp[g], -1.0)))
    for g in groups:
        sc_scr[g] = score[g]
    ranks = [[] for _ in groups]
    for j in range(ns):
        lower = jnp.where(blk > j, 1.0, 0.0)
        for g in groups:
            other = sc_scr[g, j:j + 1, :]
            ranks[g].append(jnp.where(other > score[g], 1.0, 0.0) + jnp.where(other == score[g], lower, 0.0))
    while len(ranks[0]) > 1:
        ranks = [[a + b for a, b in zip(r[0::2], r[1::2])] + ([r[-1]] if len(r) % 2 else []) for r in ranks]
    chosen_t = each(lambda g: jnp.where(ranks[g][0] < n_sel, 1.0, 0.0))
    zrows = lambda n: jnp.zeros((n, QB), F32)
    chosen = each(lambda g: jnp.concatenate([zrows(BLOCK_LANE0), chosen_t[g], zrows(LANES - BLOCK_LANE0 - ns)],
                                            axis=0).T)
    is_block_lane = (lane >= BLOCK_LANE0) & (lane < BLOCK_LANE0 + ns)
    q_sel = each(lambda g: (q32[g] + tile4(jnp.where(is_block_lane, (chosen[g] - 1.0) * (-NEG_INF), pad_mask))
                            ).astype(BF16))

    def chunk_rows(c):
        return pl.ds(pl.multiple_of((i - 4 * c + 1) * QB, QB), SEL_CHUNK)

    s0 = each(lambda g: _dotb_nt(q_sel[g], ks_ref[0, g, chunk_rows(0), :]) + heads(stab_ref, g).reshape(R, SEL_CHUNK))
    for g in groups:
        sbuf[g, 0] = s0[g]
        mx_scr[g] = _lane_tile_max(s0[g])

    def scores_body(c, carry):
        s = each(lambda g: _dotb_nt(q_sel[g], ks_ref[0, g, chunk_rows(c), :]))
        for g in groups:
            sbuf[g, c] = s[g]
            mx_scr[g] = jnp.maximum(mx_scr[g], _lane_tile_max(s[g]))
        return carry

    n_far = i // 4
    lax.fori_loop(1, n_far + 1, scores_body, 0)
    m_s = each(lambda g: jnp.max(mx_scr[g], axis=-1, keepdims=True))

    acc_scr[...] = jnp.zeros_like(acc_scr)

    def values_body(c, carry):
        p = each(lambda g: jnp.exp(sbuf[g, c] - m_s[g]).astype(BF16))
        for g in groups:
            acc_scr[g] += jnp.dot(p[g], vs_ref[0, g, chunk_rows(c), :], preferred_element_type=F32)
        return carry

    lax.fori_loop(0, n_far + 1, values_body, 0)

    is_value = _iota((R, LANES), 1) >= HEAD_DIM
    normalised = lambda acc: jnp.where(is_value, acc * (1.0 / pltpu.roll(acc, HEAD_DIM, 1)), 0.0)

    def natural(o):
        cat = jnp.concatenate([o[h * QB:(h + 1) * QB] for h in range(HG)], axis=1).astype(BF16)
        return jnp.dot(cat, gather_ref[...], preferred_element_type=F32)

    branches = each(lambda g: [o_c[g], normalised(acc_scr[g]), normalised(acc_w[g])])
    gate = each(lambda g: [_dot_sel(gt_ref[0, g], grep_ref[c], 1) for c in range(3)])
    W = HG * HEAD_DIM
    for g in groups:
        out = sum(gate[g][c] * natural(branches[g][c]) for c in range(3))
        o_ref[:, g * W:(g + 1) * W] = out.astype(BF16)


def _nsa_attn(qp, kc, vc, ks, vs, kw, vw, gates, pn, B, T):
    G = NSA_KV_HEADS
    QB = QUERY_BLOCK
    NQ = T // QB
    ncp = kc.shape[2]
    ns = T // SEL_BLOCK
    TP = T + NSA_KPAD
    R = NSA_GROUP * QB
    n_sel = min(N_SEL, ns)
    n_chunks = (NQ - 1) // 4 + 1
    NG = NSA_GROUPS_PER_STEP
    HB = NSA_GROUP * NG
    once = pl.Buffered(1)
    gtab = lambda w: pl.BlockSpec((HB, QB, w), lambda b, g, i: (g, 0, 0), pipeline_mode=once)
    slab = lambda rows: pl.BlockSpec((1, NG, rows, LANES), lambda b, g, i: (b, g, 0, 0), pipeline_mode=once)
    vm = (NG * (2 * ncp * LANES * 2 + 4 * TP * LANES * 2 + R * (SEL_CHUNK + WIN_KEYS) * 4)
          + 2 * NG * (R * LANES * 2 + QB * LANES * 4 + R * ncp * 4 + QB * NSA_GROUP * HEAD_DIM * 2)
          + NG * (n_chunks * R * SEL_CHUNK * 4 + 2 * R * LANES * 4 + 4 * R * WIN_KEYS * 4))
    return pl.pallas_call(
        functools.partial(_nsa_attn_kernel, n_sel=n_sel),
        grid=(B, G // NG, NQ),
        in_specs=[
            pl.BlockSpec((1, HB, QB, LANES), lambda b, g, i: (b, g, i, 0)),
            slab(ncp), slab(ncp), slab(TP), slab(TP), slab(TP), slab(TP),
            pl.BlockSpec((1, NG, QB, LANES), lambda b, g, i: (b, g, i, 0)),
            gtab(SEL_CHUNK), gtab(WIN_KEYS),
            pl.BlockSpec((HB, 1, QB, ncp), lambda b, g, i: (g, i, 0, 0)),
            pl.BlockSpec((ns, ncp), lambda b, g, i: (0, 0)),
            pl.BlockSpec((NSA_GROUP * LANES, NSA_GROUP * HEAD_DIM), lambda b, g, i: (0, 0)),
            pl.BlockSpec((3, LANES, NSA_GROUP * HEAD_DIM), lambda b, g, i: (0, 0, 0)),
        ],
        out_specs=pl.BlockSpec((QB, HB * HEAD_DIM), lambda b, g, i: (b * NQ + i, g)),
        out_shape=jax.ShapeDtypeStruct((B * T, NSA_WIDTH), BF16),
        scratch_shapes=[pltpu.VMEM((NG, n_chunks, R, SEL_CHUNK), F32), pltpu.VMEM((NG, R, LANES), F32),
                        pltpu.VMEM((NG, R, LANES), F32), pltpu.VMEM((NG, ns, QB), F32)],
        compiler_params=_cparams(("parallel", "parallel", "arbitrary"), vm),
        name="nsa_attn",
    )(qp, kc, vc, ks, vs, kw, vw, gates, pn["stab"], pn["wtab"], pn["ctab"], pn["sel_mt"], pn["gather"],
      pn["gate_rep"])


def _rel_bucket_table():
    n = np.arange(REL_MAX_DIST + 1)
    max_exact = REL_BUCKETS // 2
    nf = np.maximum(n, max_exact).astype(np.float32)
    large = max_exact + (np.log(nf / np.float32(max_exact)) / np.float32(math.log(REL_MAX_DIST / max_exact))
                         * np.float32(REL_BUCKETS - max_exact)).astype(np.int32)
    large = np.minimum(large, REL_BUCKETS - 1)
    return np.where(n < max_exact, n, large).astype(np.int32)


def _sel_to_cmp_matrix(T, ncp):
    nc = T // CMP_STRIDE - CMP_BLOCK // CMP_STRIDE + 1
    ns = T // SEL_BLOCK
    cs = np.arange(nc) * CMP_STRIDE
    ss = np.arange(ns) * SEL_BLOCK
    lo = np.maximum(cs[None, :], ss[:, None])
    hi = np.minimum(cs[None, :] + CMP_BLOCK, ss[:, None] + SEL_BLOCK)
    out = np.zeros((ns, ncp), np.float32)
    out[:, :nc] = np.maximum(hi - lo, 0) / CMP_BLOCK
    return out


def _bias_tables(rel_bias, rel):
    bucket = _rel_bucket_table()[np.clip(rel, 0, REL_MAX_DIST)]
    onehot = (jnp.asarray(bucket.reshape(1, -1)) == jnp.arange(REL_BUCKETS, dtype=jnp.int32)[:, None]).astype(F32)
    tab = jnp.einsum('bh,bn->hn', rel_bias, onehot, precision=HIGHEST)
    return tab.reshape((rel_bias.shape[1],) + rel.shape)


def _prep_in_proj_weight(w_in_all, l):
    D = w_in_all.shape[1]
    nsa0 = RWKV_COLS
    kv0 = nsa0 + NSA_WIDTH
    gates0 = kv0 + 6 * NSA_KV_WIDTH
    merge0 = RWKV_COLS + NSA_COLS
    cols = [(0, 3 * RWKV_WIDTH), (nsa0, NSA_WIDTH), (merge0, 2 * D_MODEL)]
    for g in range(NSA_KV_HEADS):
        for branch in range(3):
            for kv in (0, 1):
                cols.append((kv0 + (2 * branch + kv) * NSA_KV_WIDTH + g * HEAD_DIM, HEAD_DIM))
    cols += [(3 * RWKV_WIDTH, RWKV_COLS - 3 * RWKV_WIDTH), (gates0, 3 * NSA_HEADS)]
    n_pad = PROJ_COLS - sum(n for _, n in cols)
    n_in = w_in_all.shape[2]
    tr = 128

    def relayout_kernel(w_ref, o_ref):
        parts = [w_ref[0, :, s:s + n] for s, n in cols] + [jnp.zeros((tr, n_pad), F32)]
        o_ref[...] = jnp.concatenate(parts, axis=1).astype(BF16)

    return pl.pallas_call(
        relayout_kernel,
        grid=(D // tr,),
        in_specs=[pl.BlockSpec((1, tr, n_in), lambda i: (l, i, 0))],
        out_specs=pl.BlockSpec((tr, PROJ_COLS), lambda i: (i, 0)),
        out_shape=jax.ShapeDtypeStruct((D, PROJ_COLS), BF16),
        compiler_params=_cparams(("parallel",), 2 * tr * n_in * 4 + 2 * tr * PROJ_COLS * 2 + 2 * tr * PROJ_COLS * 4),
        name="w_in_relayout",
    )(w_in_all)


def _prep_rwkv_params(mu, w0, w2, a0, a2, g2, k_k, k_a, r_k, ln_w, ln_b):
    C = RWKV_WIDTH
    row = lambda z: z.reshape(1, -1).astype(F32)
    mu_s = jnp.concatenate([mu[3 * C:], jnp.zeros((SMALL_W - (RWKV_COLS - 3 * C),), F32)]).reshape(1, SMALL_W)
    zl = jnp.zeros((DECAY_LORA, C), F32)

    w2p = jnp.concatenate([w2, zl], axis=0)
    w2h = w2p.astype(BF16)
    return dict(
        mu_r=row(mu[0:C]), mu_k=row(mu[C:2 * C]), mu_v=row(mu[2 * C:3 * C]), mu_s=mu_s,
        w0=row(w0), a0=row(a0), k_k=row(k_k), k_a=row(k_a), r_k=row(r_k), ln_w=row(ln_w), ln_b=row(ln_b),
        w2h=w2h, w2l=(w2p - w2h.astype(F32)).astype(BF16),
        a2=jnp.concatenate([zl, a2], axis=0).astype(BF16),
        g2=jnp.concatenate([g2, jnp.zeros((256 - GATE_LORA, C), F32)], axis=0).astype(BF16),
    )


def _prep_nsa_params(pe_k, w1_k, w2_k, pe_v, w1_v, w2_v, q_g, k_g, rel_bias, T):
    hd = HEAD_DIM
    ones = jnp.ones((hd,), F32)
    ncp = T // CMP_STRIDE
    NQ = T // QUERY_BLOCK

    def blockdiag(a, b):
        lead = ((0, 0),) * (a.ndim - 2)
        return jnp.pad(a, lead + ((0, hd), (0, hd))) + jnp.pad(b, lead + ((hd, 0), (hd, 0)))

    w1 = blockdiag(w1_k.reshape(CMP_BLOCK, hd, hd), w1_v.reshape(CMP_BLOCK, hd, hd))
    pe = jnp.concatenate([pe_k, pe_v], axis=1)
    half = CMP_STRIDE

    spread = np.zeros((NSA_GROUP * hd, NSA_GROUP * LANES), np.float32)
    gather = np.zeros((NSA_GROUP * LANES, NSA_GROUP * hd), np.float32)
    for h in range(NSA_GROUP):
        for d in range(hd):
            spread[h * hd + d, h * LANES + d] = 1.0
            gather[h * LANES + hd + d, h * hd + d] = 1.0
    gate_sel = np.zeros((NSA_KV_HEADS, SMALL_W, LANES), np.float32)
    for g in range(NSA_KV_HEADS):
        for j in range(3 * NSA_GROUP):
            gate_sel[g, SM_GATES + 3 * NSA_GROUP * g + j, j] = 1.0
    gate_rep = np.zeros((3, LANES, NSA_GROUP * hd), np.float32)
    for c in range(3):
        for h in range(NSA_GROUP):
            gate_rep[c, 3 * h + c, h * hd:(h + 1) * hd] = 1.0

    qi = np.arange(QUERY_BLOCK)[:, None]
    rel_s = qi + (SEL_CHUNK - QUERY_BLOCK) - np.arange(SEL_CHUNK)[None, :]
    rel_w = qi + WINDOW - np.arange(WIN_KEYS)[None, :]
    rel_c = ((np.arange(NQ)[:, None, None] * QUERY_BLOCK + qi[None])
             - (np.arange(ncp)[None, None, :] * CMP_STRIDE + CMP_BLOCK - 1))
    far = np.full((QUERY_BLOCK, 1), REL_MAX_DIST)
    tabs = _bias_tables(rel_bias, np.concatenate([rel_s, rel_w, far], axis=1))
    o1, o2 = SEL_CHUNK, SEL_CHUNK + WIN_KEYS
    masked = lambda tab, ok: jnp.where(jnp.asarray(ok)[None], tab, NEG_INF)
    stab = masked(tabs[:, :, 0:o1] - tabs[:, :, o2:], rel_s >= 0)
    wtab = masked(tabs[:, :, o1:o2], (rel_w >= 0) & (rel_w < WINDOW))
    ctab = masked(_bias_tables(rel_bias, rel_c), rel_c >= 0)
    return dict(
        stab=stab, wtab=wtab, ctab=ctab,
        q_g=jnp.tile(q_g, NSA_GROUP).reshape(1, -1),
        kg_cmp=jnp.concatenate([k_g[0], ones]).reshape(1, LANES),
        kg_sel=jnp.concatenate([k_g[1], ones]).reshape(1, LANES),
        kg_win=jnp.concatenate([k_g[2], ones]).reshape(1, LANES),
        pe_a=pe[:half].reshape(1, half * LANES), pe_b=pe[half:].reshape(1, half * LANES),
        cw_a=w1[:half].reshape(half * LANES, LANES).astype(BF16),
        cw_b=w1[half:].reshape(half * LANES, LANES).astype(BF16),
        cw2=blockdiag(w2_k, w2_v).astype(BF16),
        q_spread=jnp.asarray(spread, BF16), gather=jnp.asarray(gather, BF16),
        gate_sel=jnp.asarray(gate_sel), gate_rep=jnp.asarray(gate_rep, BF16),
        sel_mt=jnp.asarray(_sel_to_cmp_matrix(T, ncp)),
    )


def _front_pad(slab, flagged):
    B, G, _, L = slab.shape
    row = jnp.zeros((L,), slab.dtype)
    if flagged:
        row = row.at[PADFLAG_LANE].set(1.0)
    return jnp.concatenate([jnp.broadcast_to(row, (B, G, NSA_KPAD, L)), slab], axis=2)


def kernel(x, c, w_ada, b_ada, norm1_g, norm2_g, w_in, rwkv_mu, rwkv_w0, rwkv_w2, rwkv_a0, rwkv_a2, rwkv_g2, rwkv_k_k, rwkv_k_a, rwkv_r_k, rwkv_ln_w, rwkv_ln_b, cmp_pe_k, cmp_w1_k, cmp_w2_k, cmp_pe_v, cmp_w1_v, cmp_w2_v, q_norm_g, k_norm_g, rel_bias, w_o_rwkv, w_o_nsa, w_out, w_up, w_down):
    B, T, D = x.shape
    depth = w_in.shape[0]
    x2 = x.reshape(B * T, D)
    for l in range(depth):
        mod6 = _ada_mod(c, w_ada[l], b_ada[l]).reshape(B * 6, 1, D)
        proj = _in_proj(x2, norm1_g[l].reshape(1, D), mod6, _prep_in_proj_weight(w_in, l), T)
        pr = _prep_rwkv_params(rwkv_mu[l], rwkv_w0[l], rwkv_w2[l], rwkv_a0[l], rwkv_a2[l], rwkv_g2[l],
                               rwkv_k_k[l], rwkv_k_a[l], rwkv_r_k[l], rwkv_ln_w[l], rwkv_ln_b[l])
        o_a = _rwkv_mix(proj, pr, B, T)
        pn = _prep_nsa_params(cmp_pe_k[l], cmp_w1_k[l], cmp_w2_k[l], cmp_pe_v[l], cmp_w1_v[l], cmp_w2_v[l],
                              q_norm_g[l], k_norm_g[l], rel_bias, T)
        qp, ks, vs, kw, vw, gates = _nsa_prep(proj, pn, B, T)
        ks, vs, kw, vw = _front_pad(ks, True), _front_pad(vs, False), _front_pad(kw, True), _front_pad(vw, False)
        kc, vc = _nsa_compress(proj, pn, B, T)
        o_b = _nsa_attn(qp, kc, vc, ks, vs, kw, vw, gates, pn, B, T)
        mixed = _merge(o_a, o_b, w_o_rwkv[l].astype(BF16), w_o_nsa[l].astype(BF16), proj)
        x1, h2 = _out_proj(mixed, w_out[l].astype(BF16), x2, mod6, norm2_g[l].reshape(1, D), T)
        x2 = _mlp(h2, w_up[l].astype(BF16), w_down[l].astype(BF16), x1, mod6, T)
    return x2.reshape(B, T, D)
```

```python
import functools
import math

import numpy as np
import jax
import jax.numpy as jnp
from jax import lax
from jax.experimental import pallas as pl
from jax.experimental.pallas import tpu as pltpu

F32 = jnp.float32
BF16 = jnp.bfloat16
HIGHEST = lax.Precision.HIGHEST

D_MODEL = 2048
HEAD_DIM = 64
RWKV_WIDTH = D_MODEL // 2
DECAY_LORA = 64
ICLR_LORA = 64
GATE_LORA = 160
RWKV_GN_EPS = 64e-5
NSA_WIDTH = D_MODEL // 2
NSA_HEADS = NSA_WIDTH // HEAD_DIM
NSA_KV_HEADS = 4
NSA_GROUP = NSA_HEADS // NSA_KV_HEADS
NSA_KV_WIDTH = NSA_KV_HEADS * HEAD_DIM
CMP_BLOCK = 32
CMP_STRIDE = 16
SEL_BLOCK = 64
N_SEL = 8
WINDOW = 512
QUERY_BLOCK = 128
REL_BUCKETS = 32
REL_MAX_DIST = 128
D_FF = 4 * D_MODEL
NORM_EPS = 1e-6
NEG_INF = -1e30
FORCE_SCORE = 1e4

RWKV_COLS = 3 * RWKV_WIDTH + DECAY_LORA + ICLR_LORA + GATE_LORA
NSA_COLS = NSA_WIDTH + 6 * NSA_KV_WIDTH + 3 * NSA_HEADS

V7X_VMEM_BYTES = 64 * 1024 * 1024
LANES = 128

COL_RKV = 0
COL_Q = 3 * RWKV_WIDTH
COL_MERGE = COL_Q + NSA_WIDTH
COL_KV = COL_MERGE + 2 * D_MODEL
COL_SMALL = COL_KV + 6 * NSA_KV_WIDTH
SMALL_W = 512
PROJ_COLS = COL_SMALL + SMALL_W
SM_XG = DECAY_LORA + ICLR_LORA
SM_GATES = SM_XG + GATE_LORA


def _vmem_limit(nbytes):
    return int(min(nbytes * 5 // 4 + (4 << 20), V7X_VMEM_BYTES - (8 << 20)))


def _cparams(sem, vmem_bytes):
    return pltpu.CompilerParams(dimension_semantics=sem, vmem_limit_bytes=_vmem_limit(vmem_bytes))


def _ada_kernel(c_ref, w_ref, b_ref, o_ref):
    c = c_ref[...]
    s = c * jax.nn.sigmoid(c)
    o_ref[...] = jnp.dot(s.astype(BF16), w_ref[...].astype(BF16), preferred_element_type=F32) + b_ref[...]


def _ada_mod(c, w_ada, b_ada):
    B, D = c.shape
    N = w_ada.shape[1]
    tn = 1024
    return pl.pallas_call(
        _ada_kernel,
        grid=(N // tn,),
        in_specs=[
            pl.BlockSpec((B, D), lambda j: (0, 0)),
            pl.BlockSpec((D, tn), lambda j: (0, j)),
            pl.BlockSpec((1, tn), lambda j: (0, j)),
        ],
        out_specs=pl.BlockSpec((B, tn), lambda j: (0, j)),
        out_shape=jax.ShapeDtypeStruct((B, N), F32),
        compiler_params=_cparams(("parallel",), 2 * D * tn * 4 + D * tn * 2),
        name="ada_mod",
    )(c, w_ada, b_ada.reshape(1, N))


def _modulated_norm(x, g, sc, sh):
    ms = jnp.mean(x * x, axis=-1, keepdims=True)
    return (x * lax.rsqrt(ms + NORM_EPS) * g) * (1.0 + sc) + sh


def _inproj_kernel(x_ref, g_ref, sh_ref, sc_ref, w_ref, o_ref, h_scr):
    @pl.when(pl.program_id(1) == 0)
    def _():
        h_scr[...] = _modulated_norm(x_ref[...], g_ref[...], sc_ref[0], sh_ref[0]).astype(BF16)

    o_ref[...] = lax.dot_general(h_scr[...], w_ref[...], (((1,), (1,)), ((), ())),
                                 preferred_element_type=F32)


def _in_proj(x2, g1, mod6, w_in_t, T):
    BT, D = x2.shape
    NP = w_in_t.shape[0]
    tm = min(1024, T)
    tn = 1024
    tpb = T // tm
    vm = 2 * tm * D * 4 + tm * D * 2 + 2 * D * tn * 2 + 2 * tm * tn * 4 + 2 * tm * D * 4
    return pl.pallas_call(
        _inproj_kernel,
        grid=(BT // tm, NP // tn),
        in_specs=[
            pl.BlockSpec((tm, D), lambda i, j: (i, 0)),
            pl.BlockSpec((1, D), lambda i, j: (0, 0)),
            pl.BlockSpec((1, 1, D), lambda i, j: ((i // tpb) * 6 + 0, 0, 0)),
            pl.BlockSpec((1, 1, D), lambda i, j: ((i // tpb) * 6 + 1, 0, 0)),
            pl.BlockSpec((tn, D), lambda i, j: (j, 0)),
        ],
        out_specs=pl.BlockSpec((tm, tn), lambda i, j: (i, j)),
        out_shape=jax.ShapeDtypeStruct((BT, NP), F32),
        scratch_shapes=[pltpu.VMEM((tm, D), BF16)],
        compiler_params=_cparams(("parallel", "arbitrary"), vm),
        name="in_proj",
    )(x2, g1, mod6, mod6, w_in_t)


def _merge_kernel(oa_ref, ob_ref, wa_ref, wb_ref, ga_ref, gb_ref, o_ref):
    ya = jnp.dot(oa_ref[...], wa_ref[...], preferred_element_type=F32)
    yb = jnp.dot(ob_ref[...], wb_ref[...], preferred_element_type=F32)
    o_ref[...] = (jax.nn.sigmoid(ga_ref[...]) * ya + jax.nn.sigmoid(gb_ref[...]) * yb).astype(BF16)


def _merge(o_a, o_b, w_oa, w_ob, proj):
    BT, W = o_a.shape
    D = w_oa.shape[1]
    tm, tn = 512, 1024
    ga0 = COL_MERGE // tn
    gb0 = (COL_MERGE + D) // tn
    vm = 2 * (2 * tm * W * 2 + 2 * W * tn * 2 + 2 * tm * tn * 4 + tm * tn * 2) + 3 * tm * tn * 4
    return pl.pallas_call(
        _merge_kernel,
        grid=(BT // tm, D // tn),
        in_specs=[
            pl.BlockSpec((tm, W), lambda i, j: (i, 0)),
            pl.BlockSpec((tm, W), lambda i, j: (i, 0)),
            pl.BlockSpec((W, tn), lambda i, j: (0, j)),
            pl.BlockSpec((W, tn), lambda i, j: (0, j)),
            pl.BlockSpec((tm, tn), lambda i, j: (i, ga0 + j)),
            pl.BlockSpec((tm, tn), lambda i, j: (i, gb0 + j)),
        ],
        out_specs=pl.BlockSpec((tm, tn), lambda i, j: (i, j)),
        out_shape=jax.ShapeDtypeStruct((BT, D), BF16),
        compiler_params=_cparams(("parallel", "parallel"), vm),
        name="merge",
    )(o_a, o_b, w_oa, w_ob, proj, proj)


def _outproj_kernel(m_ref, w_ref, x_ref, gt_ref, g_ref, sh_ref, sc_ref, x1_ref, h2_ref):
    y = jnp.dot(m_ref[...], w_ref[...], preferred_element_type=F32)
    x1 = x_ref[...] + gt_ref[0] * y
    x1_ref[...] = x1
    h2_ref[...] = _modulated_norm(x1, g_ref[...], sc_ref[0], sh_ref[0]).astype(BF16)


def _out_proj(mixed, w_out, x2, mod6, g2, T):
    BT, D = x2.shape
    tm = min(512, T)
    tpb = T // tm
    vm = 2 * (tm * D * 2 + D * D * 2 + tm * D * 4 + tm * D * 4 + tm * D * 2) + 3 * tm * D * 4
    mod_spec = lambda k: pl.BlockSpec((1, 1, D), lambda i: ((i // tpb) * 6 + k, 0, 0))
    return pl.pallas_call(
        _outproj_kernel,
        grid=(BT // tm,),
        in_specs=[
            pl.BlockSpec((tm, D), lambda i: (i, 0)),
            pl.BlockSpec((D, D), lambda i: (0, 0)),
            pl.BlockSpec((tm, D), lambda i: (i, 0)),
            mod_spec(2),
            pl.BlockSpec((1, D), lambda i: (0, 0)),
            mod_spec(3),
            mod_spec(4),
        ],
        out_specs=[pl.BlockSpec((tm, D), lambda i: (i, 0)), pl.BlockSpec((tm, D), lambda i: (i, 0))],
        out_shape=[jax.ShapeDtypeStruct((BT, D), F32), jax.ShapeDtypeStruct((BT, D), BF16)],
        compiler_params=_cparams(("parallel",), vm),
        name="out_proj",
    )(mixed, w_out, x2, mod6, g2, mod6, mod6)


def _mlp_kernel(h_ref, wu_ref, wd_ref, x_ref, gt_ref, o_ref, acc_ref):
    f = pl.program_id(1)
    u = jnp.dot(h_ref[...], wu_ref[...], preferred_element_type=F32)
    u = jnp.square(jnp.maximum(u, 0.0)).astype(BF16)
    part = jnp.dot(u, wd_ref[...], preferred_element_type=F32)

    @pl.when(f == 0)
    def _():
        acc_ref[...] = part

    @pl.when(f > 0)
    def _():
        acc_ref[...] += part

    @pl.when(f == pl.num_programs(1) - 1)
    def _():
        o_ref[...] = x_ref[...] + gt_ref[0] * acc_ref[...]


def _mlp(h2, w_up, w_down, x1, mod6, T):
    BT, D = x1.shape
    F = w_up.shape[1]
    tm = min(512, T)
    tf = 1024
    tpb = T // tm
    vm = 2 * (tm * D * 2 + 2 * D * tf * 2 + 2 * tm * D * 4) + tm * D * 4 + 2 * tm * tf * 4
    return pl.pallas_call(
        _mlp_kernel,
        grid=(BT // tm, F // tf),
        in_specs=[
            pl.BlockSpec((tm, D), lambda i, f: (i, 0)),
            pl.BlockSpec((D, tf), lambda i, f: (0, f)),
            pl.BlockSpec((tf, D), lambda i, f: (f, 0)),
            pl.BlockSpec((tm, D), lambda i, f: (i, 0)),
            pl.BlockSpec((1, 1, D), lambda i, f: ((i // tpb) * 6 + 5, 0, 0)),
        ],
        out_specs=pl.BlockSpec((tm, D), lambda i, f: (i, 0)),
        out_shape=jax.ShapeDtypeStruct((BT, D), F32),
        scratch_shapes=[pltpu.VMEM((tm, D), F32)],
        compiler_params=_cparams(("parallel", "arbitrary"), vm),
        name="mlp",
    )(h2, w_up, w_down, x1, mod6)


def _dotb(a, b):
    return jnp.dot(a.astype(BF16), b.astype(BF16), preferred_element_type=F32)


def _dotb_nt(a, b):
    return lax.dot_general(a.astype(BF16), b.astype(BF16), (((1,), (1,)), ((), ())),
                           preferred_element_type=F32)


def _split_bf16(x, terms):
    parts, rem = [], x
    for t in range(terms):
        p = rem.astype(BF16)
        parts.append(p)
        if t + 1 < terms:
            rem = rem - p.astype(F32)
    return parts


def _dot_sel(x, sel, terms):
    sel = sel.astype(BF16)
    return sum(jnp.dot(p, sel, preferred_element_type=F32) for p in _split_bf16(x, terms))


def _sel_dot(sel, x, terms):
    sel = sel.astype(BF16)
    return sum(jnp.dot(sel, p, preferred_element_type=F32) for p in _split_bf16(x, terms))


def _dot3(a, b_hi, b_lo):
    a_hi, a_lo = _split_bf16(a, 2)
    return (jnp.dot(a_hi, b_hi, preferred_element_type=F32) + jnp.dot(a_lo, b_hi, preferred_element_type=F32)
            + jnp.dot(a_hi, b_lo, preferred_element_type=F32))


def _iota(shape, axis):
    return lax.broadcasted_iota(jnp.int32, shape, axis)


def _same_head_mask(n):
    return (_iota((n, n), 0) >> 6) == (_iota((n, n), 1) >> 6)


RW_TC = 256
RW_C = 64
RW_LANES = 4 * HEAD_DIM


def _block_diag(x, bdmask):
    return jnp.where(bdmask, jnp.concatenate([x, x, x, x], axis=0), 0.0)


def _rwkv_kernel(r_ref, k_ref, v_ref, s_ref, mur_ref, muk_ref, muv_ref, mus_ref,
                 w0_ref, a0_ref, kk_ref, ka_ref, rk_ref, lnw_ref, lnb_ref,
                 w2h_ref, w2l_ref, a2_ref, g2_ref, o_ref,
                 pr_scr, pk_scr, pv_scr, ps_scr, state_scr):
    TC = r_ref.shape[0]
    C = RW_C
    W = RW_LANES

    @pl.when(pl.program_id(1) == 0)
    def _():
        pr_scr[...] = jnp.zeros_like(pr_scr)
        pk_scr[...] = jnp.zeros_like(pk_scr)
        pv_scr[...] = jnp.zeros_like(pv_scr)
        ps_scr[...] = jnp.zeros_like(ps_scr)
        state_scr[...] = jnp.zeros_like(state_scr)

    def shift_mix(p_ref, prev_scr, mu_ref):
        p = p_ref[...]
        rolled = pltpu.roll(p, 1, 0)
        first = jnp.where(_iota((8, p.shape[1]), 0) == 0, prev_scr[0:1, :], rolled[0:8])
        shifted = jnp.concatenate([first, rolled[8:]], axis=0)
        prev_scr[0:1, :] = p[TC - 1:TC, :]
        return p + (shifted - p) * mu_ref[...]

    CW = r_ref.shape[1]
    G = CW // W
    NQ = TC // C
    groups = lambda x: [x[:, g * W:(g + 1) * W] for g in range(G)]
    per_group = lambda f, x: jnp.concatenate([f(xg) for xg in groups(x)], axis=1)

    r = shift_mix(r_ref, pr_scr, mur_ref)
    k = shift_mix(k_ref, pk_scr, muk_ref)
    v = shift_mix(v_ref, pv_scr, muv_ref)
    sm = shift_mix(s_ref, ps_scr, mus_ref)
    xwa = sm[:, 0:SM_XG]

    bdmask = _same_head_mask(W)
    bones = jnp.where(bdmask, 1.0, 0.0)
    head_sum = lambda x: per_group(lambda xg: _dot_sel(xg, bones, 1), x)
    tri = jnp.where(_same_head_mask(TC) & (_iota((TC, TC), 1) <= _iota((TC, TC), 0)), 1.0, 0.0)
    lane = _iota((C, W), 1)
    row = _iota((C, W), 0)
    lanehead = lane >> 6
    strict = (lane & 63) < row
    incl = (lane & 63) <= row
    eye_cat = jnp.where((lane & 63) == row, 1.0, 0.0)

    wlin = w0_ref[...] + _dot3(jnp.tanh(xwa), w2h_ref[...], w2l_ref[...])
    a = jax.nn.sigmoid(a0_ref[...] + _dotb(xwa, a2_ref[...]))
    gate = _dotb(jax.nn.sigmoid(sm[:, SM_XG:SM_XG + 256]), g2_ref[...])
    z = -wlin
    softplus = jnp.maximum(z, 0.0) + jnp.log(1.0 + jnp.exp(-jnp.abs(z)))
    ld = -jnp.exp(-softplus - 0.5)
    cum = _sel_dot(tri, ld, 3)
    cum_last = jnp.concatenate(
        [jnp.broadcast_to(cum[(q + 1) * C - 1:(q + 1) * C, :], (C, CW)) for q in range(NQ)], axis=0)
    kk = k * kk_ref[...]
    kk = kk * lax.rsqrt(jnp.maximum(head_sum(kk * kk), 1e-24))
    k2 = k * (1.0 + (a - 1.0) * ka_ref[...])
    bvec = kk * a
    e_inv = jnp.exp(-cum)
    e_end = jnp.exp(cum_last - cum)
    At_f = -kk * jnp.exp(cum - ld)
    Rt_f = r * jnp.exp(cum)
    Bt_f = bvec * e_inv
    Kt_f = k2 * e_inv
    Bg_f = bvec * e_end
    Kg_f = k2 * e_end
    g_end = jnp.exp(cum_last)

    items = [(g, q) for q in range(NQ) for g in range(G)]
    blk = lambda x, g, q: x[q * C:(q + 1) * C, g * W:(g + 1) * W]
    bd01 = jnp.where(bdmask, 1.0, 0.0).astype(BF16)

    def bd(x):
        xb = x.astype(BF16)
        return jnp.concatenate([xb, xb, xb, xb], axis=0) * bd01

    At = {it: blk(At_f, *it) for it in items}
    Rt = {it: blk(Rt_f, *it) for it in items}
    Vq = {it: blk(v, *it) for it in items}
    AA = {}
    for it in items:
        bk = jnp.concatenate([jnp.where(lanehead == h, X, 0.0)
                              for X in (blk(Bt_f, *it), blk(Kt_f, *it)) for h in range(4)], axis=0)
        AA[it] = _dotb_nt(jnp.concatenate([At[it], Rt[it]], axis=0), bk)
    A_ab = {it: jnp.where(strict, AA[it][0:C, 0:W], 0.0) for it in items}
    A_ak = {it: jnp.where(strict, AA[it][0:C, W:2 * W], 0.0) for it in items}
    A_rb = {it: jnp.where(incl, AA[it][C:2 * C, 0:W], 0.0) for it in items}
    A_rk = {it: jnp.where(incl, AA[it][C:2 * C, W:2 * W], 0.0) for it in items}
    M = dict(A_ab)
    Tm = {it: eye_cat + A_ab[it] for it in items}
    for _ in range(5):
        M = {it: _dotb(M[it], bd(M[it])) for it in items}
        Tm = {it: Tm[it] + _dotb(M[it], bd(Tm[it])) for it in items}
    Vbd = {it: bd(Vq[it]) for it in items}
    akv = {it: _dotb(A_ak[it], Vbd[it]) for it in items}
    rkv = {it: _dotb(A_rk[it], Vbd[it]) for it in items}

    S = [state_scr[g] for g in range(G)]
    ys = []
    for q in range(NQ):
        its = [(g, q) for g in range(G)]
        rhs = [_dotb_nt(At[it], S[it[0]]) + akv[it] for it in its]
        U = [_dotb(Tm[it], bd(rhs[g])) for g, it in enumerate(its)]
        ys.append(jnp.concatenate(
            [_dotb_nt(Rt[it], S[g]) + _dotb(A_rb[it], bd(U[g])) + rkv[it] for g, it in enumerate(its)], axis=1))
        upd = [_dotb(jnp.concatenate([U[g], Vq[it]], axis=0).T,
                     jnp.concatenate([blk(Bg_f, *it), blk(Kg_f, *it)], axis=0)) for g, it in enumerate(its)]
        S = [S[g] * blk(g_end, g, q)[0:1, :] + jnp.where(bdmask, upd[g], 0.0) for g in range(G)]
    for g in range(G):
        state_scr[g] = S[g]
    y = jnp.concatenate(ys, axis=0)

    inv_n = 1.0 / HEAD_DIM
    d = y - head_sum(y) * inv_n
    var = head_sum(d * d) * inv_n
    yn = d * lax.rsqrt(var + RWKV_GN_EPS) * lnw_ref[...] + lnb_ref[...]
    bonus = head_sum(r * k2 * rk_ref[...]) * v
    o_ref[...] = ((yn + bonus) * gate).astype(BF16)


def _rwkv_mix(proj, pr, B, T):
    BT = proj.shape[0]
    TC = min(RW_TC, T)
    W = RW_LANES
    CW = RWKV_WIDTH
    nct = T // TC
    row = lambda b, c: b * nct + c
    full = lambda shape: pl.BlockSpec(shape, lambda b, c: (0, 0))
    vec = full((1, CW))
    in_specs = [
        pl.BlockSpec((TC, CW), lambda b, c: (row(b, c), 0)),
        pl.BlockSpec((TC, CW), lambda b, c: (row(b, c), 1)),
        pl.BlockSpec((TC, CW), lambda b, c: (row(b, c), 2)),
        pl.BlockSpec((TC, SMALL_W), lambda b, c: (row(b, c), COL_SMALL // SMALL_W)),
        vec, vec, vec,
        full((1, SMALL_W)),
        vec, vec, vec, vec, vec, vec, vec,
        full((SM_XG, CW)), full((SM_XG, CW)), full((SM_XG, CW)), full((256, CW)),
    ]
    vm = (2 * (3 * TC * CW * 4 + TC * SMALL_W * 4 + TC * CW * 2 + (3 * SM_XG + 256) * CW * 2)
          + 24 * TC * CW * 4)
    return pl.pallas_call(
        _rwkv_kernel,
        grid=(B, nct),
        in_specs=in_specs,
        out_specs=pl.BlockSpec((TC, CW), lambda b, c: (row(b, c), 0)),
        out_shape=jax.ShapeDtypeStruct((BT, CW), BF16),
        scratch_shapes=[pltpu.VMEM((8, CW), F32), pltpu.VMEM((8, CW), F32), pltpu.VMEM((8, CW), F32),
                        pltpu.VMEM((8, SMALL_W), F32), pltpu.VMEM((CW // W, W, W), F32)],
        compiler_params=_cparams(("parallel", "arbitrary"), vm),
        name="rwkv_mix",
    )(proj, proj, proj, proj, pr["mu_r"], pr["mu_k"], pr["mu_v"], pr["mu_s"],
      pr["w0"], pr["a0"], pr["k_k"], pr["k_a"], pr["r_k"], pr["ln_w"], pr["ln_b"],
      pr["w2h"], pr["w2l"], pr["a2"], pr["g2"])


NSA_TT = 256
BLOCK_LANE0 = HEAD_DIM
PADFLAG_LANE = BLOCK_LANE0 + 32


def _nsa_prep_kernel(q_ref, kv0_ref, kv1_ref, kv2_ref, s_ref, qg_ref, kgs_ref, kgw_ref, e_ref, gsel_ref,
                     qp_ref, ksl_ref, vsl_ref, kwl_ref, vwl_ref, gt_ref):
    tt = q_ref.shape[0]
    QW = NSA_GROUP * HEAD_DIM
    kv_refs = (kv0_ref, kv1_ref, kv2_ref)
    bones = jnp.where(_same_head_mask(QW), 1.0, 0.0)
    bones2 = jnp.where(_same_head_mask(LANES), 1.0, 0.0)
    lane = _iota((tt, LANES), 1)
    block_id = (pl.program_id(1) * tt + _iota((tt, LANES), 0)) >> 6
    onehot = jnp.where(lane == block_id + BLOCK_LANE0, 1.0, 0.0)
    small = _split_bf16(s_ref[...], 3)

    def pair(g, branch):
        off = (3 * g + branch) * LANES
        return kv_refs[off // SMALL_W][:, off % SMALL_W:off % SMALL_W + LANES]

    def slabs(g, x, gain, k_ref, v_ref):
        ms = _dot_sel(x * x, bones2, 1) * (1.0 / HEAD_DIM)
        k_ref[0, g] = jnp.where(lane < HEAD_DIM, x * lax.rsqrt(ms + NORM_EPS) * gain, onehot).astype(BF16)
        v_ref[0, g] = jnp.where(lane >= HEAD_DIM, x, 1.0).astype(BF16)

    for g in range(NSA_KV_HEADS):
        q = q_ref[:, g * QW:(g + 1) * QW]
        ms = _dot_sel(q * q, bones, 1) * (1.0 / HEAD_DIM)
        qn = (q * lax.rsqrt(ms + NORM_EPS) * qg_ref[...]) * (HEAD_DIM ** -0.5)
        qs = jnp.dot(qn.astype(BF16), e_ref[...], preferred_element_type=F32).astype(BF16)
        for h in range(NSA_GROUP):
            qp_ref[0, NSA_GROUP * g + h] = qs[:, h * LANES:(h + 1) * LANES]
        slabs(g, pair(g, 1), kgs_ref[...], ksl_ref, vsl_ref)
        slabs(g, pair(g, 2), kgw_ref[...], kwl_ref, vwl_ref)
        sel = gsel_ref[g].astype(BF16)
        gt_ref[0, g] = jax.nn.sigmoid(sum(jnp.dot(part, sel, preferred_element_type=F32) for part in small))


def _nsa_prep(proj, pn, B, T):
    tt = min(NSA_TT, T)
    ntt = T // tt
    G = NSA_KV_HEADS
    QW = NSA_GROUP * HEAD_DIM
    row = lambda b, t: b * ntt + t
    full = lambda shape: pl.BlockSpec(shape, lambda b, t: tuple(0 for _ in shape))
    kv_spec = lambda j: pl.BlockSpec((tt, SMALL_W), lambda b, t: (row(b, t), COL_KV // SMALL_W + j))
    vm = 2 * (tt * NSA_WIDTH * 4 + 4 * tt * SMALL_W * 4 + QW * 4 * LANES * 2 + G * SMALL_W * LANES * 4
              + 16 * tt * LANES * 2 + 16 * tt * LANES * 2 + 4 * tt * LANES * 4) + 24 * tt * QW * 4
    return pl.pallas_call(
        _nsa_prep_kernel,
        grid=(B, ntt),
        in_specs=[
            pl.BlockSpec((tt, NSA_WIDTH), lambda b, t: (row(b, t), COL_Q // NSA_WIDTH)),
            kv_spec(0), kv_spec(1), kv_spec(2),
            pl.BlockSpec((tt, SMALL_W), lambda b, t: (row(b, t), COL_SMALL // SMALL_W)),
            full((1, QW)), full((1, LANES)), full((1, LANES)), full((QW, NSA_GROUP * LANES)),
            full((G, SMALL_W, LANES)),
        ],
        out_specs=[pl.BlockSpec((1, NSA_HEADS, tt, LANES), lambda b, t: (b, 0, t, 0))]
        + [pl.BlockSpec((1, G, tt, LANES), lambda b, t: (b, 0, t, 0))] * 5,
        out_shape=[jax.ShapeDtypeStruct((B, NSA_HEADS, T, LANES), BF16)]
        + [jax.ShapeDtypeStruct((B, G, T, LANES), BF16)] * 4
        + [jax.ShapeDtypeStruct((B, G, T, LANES), F32)],
        compiler_params=_cparams(("parallel", "parallel"), vm),
        name="nsa_prep",
    )(proj, proj, proj, proj, proj, pn["q_g"], pn["kg_sel"], pn["kg_win"], pn["q_spread"], pn["gate_sel"])


def _gelu_tanh(x):
    return 0.5 * x * (1.0 + jnp.tanh(math.sqrt(2.0 / math.pi) * (x + 0.044715 * (x * x * x))))


def _nsa_compress_kernel(x_ref, pea_ref, peb_ref, wa_ref, wb_ref, w2_ref, kg_ref, kc_ref, vc_ref):
    nsub = x_ref.shape[0] // CMP_STRIDE
    xs = jnp.concatenate([x_ref[pl.ds(s, nsub, stride=CMP_STRIDE), :] for s in range(CMP_STRIDE)], axis=1)
    p0 = jnp.dot((xs + pea_ref[...]).astype(BF16), wa_ref[...], preferred_element_type=F32)
    p1 = jnp.dot((xs + peb_ref[...]).astype(BF16), wb_ref[...], preferred_element_type=F32)
    hid = _gelu_tanh(p0 + pltpu.roll(p1, nsub - 1, 0))
    out = jnp.dot(hid.astype(BF16), w2_ref[...], preferred_element_type=F32)
    bones2 = jnp.where(_same_head_mask(LANES), 1.0, 0.0)
    ms = _dot_sel(out * out, bones2, 2) * (1.0 / HEAD_DIM)
    is_k = _iota(out.shape, 1) < HEAD_DIM
    kc_ref[0, 0] = jnp.where(is_k, out * lax.rsqrt(ms + NORM_EPS) * kg_ref[...], 0.0).astype(BF16)
    vc_ref[0, 0] = jnp.where(is_k, 0.0, out).astype(BF16)


def _nsa_compress(proj, pn, B, T):
    G = NSA_KV_HEADS
    nsub = T // CMP_STRIDE
    kv_blk = COL_KV // LANES
    KW = CMP_STRIDE * LANES
    full = lambda shape: pl.BlockSpec(shape, lambda b, g: tuple(0 for _ in shape))
    vm = 2 * (T * LANES * 4 + 2 * KW * LANES * 2 + nsub * LANES * 2) + 6 * nsub * KW * 4
    return pl.pallas_call(
        _nsa_compress_kernel,
        grid=(B, G),
        in_specs=[
            pl.BlockSpec((T, LANES), lambda b, g: (b, kv_blk + 3 * g)),
            full((1, KW)), full((1, KW)), full((KW, LANES)), full((KW, LANES)), full((LANES, LANES)),
            full((1, LANES)),
        ],
        out_specs=[pl.BlockSpec((1, 1, nsub, LANES), lambda b, g: (b, g, 0, 0))] * 2,
        out_shape=[jax.ShapeDtypeStruct((B, G, nsub, LANES), BF16)] * 2,
        compiler_params=_cparams(("parallel", "parallel"), vm),
        name="nsa_compress",
    )(proj, pn["pe_a"], pn["pe_b"], pn["cw_a"], pn["cw_b"], pn["cw2"], pn["kg_cmp"])


NSA_GROUPS_PER_STEP = 4
NSA_KPAD = WINDOW
SEL_CHUNK = 4 * QUERY_BLOCK
WIN_KEYS = WINDOW + QUERY_BLOCK


def _lane_tile_max(s):
    tiles = [s[:, j * LANES:(j + 1) * LANES] for j in range(s.shape[1] // LANES)]
    while len(tiles) > 1:
        tiles = [jnp.maximum(a, b) for a, b in zip(tiles[0::2], tiles[1::2])] + ([tiles[-1]] if len(tiles) % 2 else [])
    return tiles[0]


def _nsa_attn_kernel(q_ref, kc_ref, vc_ref, ks_ref, vs_ref, kw_ref, vw_ref, gt_ref, stab_ref, wtab_ref, ctab_ref,
                     selmt_ref, gather_ref, grep_ref, o_ref, sbuf, mx_scr, acc_scr, sc_scr, *, n_sel):
    i = pl.program_id(2)
    QB = QUERY_BLOCK
    HG = NSA_GROUP
    R = HG * QB
    t0 = i * QB
    groups = range(kc_ref.shape[1])
    each = lambda f: [f(g) for g in groups]
    tile4 = lambda z: jnp.concatenate([z, z, z, z], axis=0)
    heads = lambda ref, g: ref[HG * g:HG * (g + 1)]
    lane = _iota((QB, LANES), 1)
    pad_mask = jnp.where(lane == PADFLAG_LANE, NEG_INF, 0.0)
    q = each(lambda g: q_ref[0, HG * g:HG * (g + 1)].reshape(R, LANES))
    q32 = each(lambda g: q[g].astype(F32))

    ncp = kc_ref.shape[2]
    s = each(lambda g: _dotb_nt(q[g], kc_ref[0, g]) + heads(ctab_ref, g)[:, 0].reshape(R, ncp))
    m = each(lambda g: jnp.max(s[g], axis=-1, keepdims=True))
    p_c = each(lambda g: jnp.exp(s[g] - m[g]))
    lsum = each(lambda g: jnp.sum(p_c[g], axis=-1, keepdims=True))
    p_c = each(lambda g: p_c[g] * jnp.where(m[g] > 0.5 * NEG_INF, 1.0 / lsum[g], 0.0))
    o_c = each(lambda g: jnp.dot(p_c[g].astype(BF16), vc_ref[0, g], preferred_element_type=F32))

    wrows = pl.ds(pl.multiple_of(t0, QB), WIN_KEYS)
    q_win = each(lambda g: (q32[g] + tile4(pad_mask)).astype(BF16))
    s = each(lambda g: _dotb_nt(q_win[g], kw_ref[0, g, wrows, :]) + heads(wtab_ref, g).reshape(R, WIN_KEYS))
    p = each(lambda g: jnp.exp(s[g] - jnp.max(_lane_tile_max(s[g]), axis=-1, keepdims=True)))
    acc_w = each(lambda g: jnp.dot(p[g].astype(BF16), vw_ref[0, g, wrows, :], preferred_element_type=F32))

    psum = each(lambda g: p_c[g][0:QB] + p_c[g][QB:2 * QB] + p_c[g][2 * QB:3 * QB] + p_c[g][3 * QB:4 * QB])
    selmt = selmt_ref[...]
    parts = each(lambda g: _split_bf16(psum[g], 3))
    imp = each(lambda g: sum(lax.dot_general(selmt, part, (((1,), (1,)), ((), ())), preferred_element_type=F32)
                             for part in parts[g]))
    ns = selmt.shape[0]
    blk = _iota((ns, QB), 0)
    cur = (t0 + _iota((ns, QB), 1)) >> 6
    forced = (blk == 0) | (blk == cur) | (blk == cur - 1)
    score = each(lambda g: jnp.where(forced, FORCE_SCORE, jnp.where(blk <= cur, imp[g], -1.0)))
    for g in groups:
        sc_scr[g] = score[g]
    ranks = [[] for _ in groups]
    for j in range(ns):
        lower = jnp.where(blk > j, 1.0, 0.0)
        for g in groups:
            other = sc_scr[g, j:j + 1, :]
            ranks[g].append(jnp.where(other > score[g], 1.0, 0.0) + jnp.where(other == score[g], lower, 0.0))
    while len(ranks[0]) > 1:
        ranks = [[a + b for a, b in zip(r[0::2], r[1::2])] + ([r[-1]] if len(r) % 2 else []) for r in ranks]
    chosen_t = each(lambda g: jnp.where(ranks[g][0] < n_sel, 1.0, 0.0))
    zrows = lambda n: jnp.zeros((n, QB), F32)
    chosen = each(lambda g: jnp.concatenate([zrows(BLOCK_LANE0), chosen_t[g], zrows(LANES - BLOCK_LANE0 - ns)],
                                            axis=0).T)
    is_block_lane = (lane >= BLOCK_LANE0) & (lane < BLOCK_LANE0 + ns)
    q_sel = each(lambda g: (q32[g] + tile4(jnp.where(is_block_lane, (chosen[g] - 1.0) * (-NEG_INF), pad_mask))
                            ).astype(BF16))

    def chunk_rows(c):
        return pl.ds(pl.multiple_of((i - 4 * c + 1) * QB, QB), SEL_CHUNK)

    s0 = each(lambda g: _dotb_nt(q_sel[g], ks_ref[0, g, chunk_rows(0), :]) + heads(stab_ref, g).reshape(R, SEL_CHUNK))
    for g in groups:
        sbuf[g, 0] = s0[g]
        mx_scr[g] = _lane_tile_max(s0[g])

    def scores_body(c, carry):
        s = each(lambda g: _dotb_nt(q_sel[g], ks_ref[0, g, chunk_rows(c), :]))
        for g in groups:
            sbuf[g, c] = s[g]
            mx_scr[g] = jnp.maximum(mx_scr[g], _lane_tile_max(s[g]))
        return carry

    n_far = i // 4
    lax.fori_loop(1, n_far + 1, scores_body, 0)
    m_s = each(lambda g: jnp.max(mx_scr[g], axis=-1, keepdims=True))

    acc_scr[...] = jnp.zeros_like(acc_scr)

    def values_body(c, carry):
        p = each(lambda g: jnp.exp(sbuf[g, c] - m_s[g]).astype(BF16))
        for g in groups:
            acc_scr[g] += jnp.dot(p[g], vs_ref[0, g, chunk_rows(c), :], preferred_element_type=F32)
        return carry

    lax.fori_loop(0, n_far + 1, values_body, 0)

    is_value = _iota((R, LANES), 1) >= HEAD_DIM
    normalised = lambda acc: jnp.where(is_value, acc * (1.0 / pltpu.roll(acc, HEAD_DIM, 1)), 0.0)

    def natural(o):
        cat = jnp.concatenate([o[h * QB:(h + 1) * QB] for h in range(HG)], axis=1).astype(BF16)
        return jnp.dot(cat, gather_ref[...], preferred_element_type=F32)

    branches = each(lambda g: [o_c[g], normalised(acc_scr[g]), normalised(acc_w[g])])
    gate = each(lambda g: [_dot_sel(gt_ref[0, g], grep_ref[c], 1) for c in range(3)])
    W = HG * HEAD_DIM
    for g in groups:
        out = sum(gate[g][c] * natural(branches[g][c]) for c in range(3))
        o_ref[:, g * W:(g + 1) * W] = out.astype(BF16)


def _nsa_attn(qp, kc, vc, ks, vs, kw, vw, gates, pn, B, T):
    G = NSA_KV_HEADS
    QB = QUERY_BLOCK
    NQ = T // QB
    ncp = kc.shape[2]
    ns = T // SEL_BLOCK
    TP = T + NSA_KPAD
    R = NSA_GROUP * QB
    n_sel = min(N_SEL, ns)
    n_chunks = (NQ - 1) // 4 + 1
    NG = NSA_GROUPS_PER_STEP
    HB = NSA_GROUP * NG
    once = pl.Buffered(1)
    gtab = lambda w: pl.BlockSpec((HB, QB, w), lambda b, g, i: (g, 0, 0), pipeline_mode=once)
    slab = lambda rows: pl.BlockSpec((1, NG, rows, LANES), lambda b, g, i: (b, g, 0, 0), pipeline_mode=once)
    vm = (NG * (2 * ncp * LANES * 2 + 4 * TP * LANES * 2 + R * (SEL_CHUNK + WIN_KEYS) * 4)
          + 2 * NG * (R * LANES * 2 + QB * LANES * 4 + R * ncp * 4 + QB * NSA_GROUP * HEAD_DIM * 2)
          + NG * (n_chunks * R * SEL_CHUNK * 4 + 2 * R * LANES * 4 + 4 * R * WIN_KEYS * 4))
    return pl.pallas_call(
        functools.partial(_nsa_attn_kernel, n_sel=n_sel),
        grid=(B, G // NG, NQ),
        in_specs=[
            pl.BlockSpec((1, HB, QB, LANES), lambda b, g, i: (b, g, i, 0)),
            slab(ncp), slab(ncp), slab(TP), slab(TP), slab(TP), slab(TP),
            pl.BlockSpec((1, NG, QB, LANES), lambda b, g, i: (b, g, i, 0)),
            gtab(SEL_CHUNK), gtab(WIN_KEYS),
            pl.BlockSpec((HB, 1, QB, ncp), lambda b, g, i: (g, i, 0, 0)),
            pl.BlockSpec((ns, ncp), lambda b, g, i: (0, 0)),
            pl.BlockSpec((NSA_GROUP * LANES, NSA_GROUP * HEAD_DIM), lambda b, g, i: (0, 0)),
            pl.BlockSpec((3, LANES, NSA_GROUP * HEAD_DIM), lambda b, g, i: (0, 0, 0)),
        ],
        out_specs=pl.BlockSpec((QB, HB * HEAD_DIM), lambda b, g, i: (b * NQ + i, g)),
        out_shape=jax.ShapeDtypeStruct((B * T, NSA_WIDTH), BF16),
        scratch_shapes=[pltpu.VMEM((NG, n_chunks, R, SEL_CHUNK), F32), pltpu.VMEM((NG, R, LANES), F32),
                        pltpu.VMEM((NG, R, LANES), F32), pltpu.VMEM((NG, ns, QB), F32)],
        compiler_params=_cparams(("parallel", "parallel", "arbitrary"), vm),
        name="nsa_attn",
    )(qp, kc, vc, ks, vs, kw, vw, gates, pn["stab"], pn["wtab"], pn["ctab"], pn["sel_mt"], pn["gather"],
      pn["gate_rep"])


def _rel_bucket_table():
    n = np.arange(REL_MAX_DIST + 1)
    max_exact = REL_BUCKETS // 2
    nf = np.maximum(n, max_exact).astype(np.float32)
    large = max_exact + (np.log(nf / np.float32(max_exact)) / np.float32(math.log(REL_MAX_DIST / max_exact))
                         * np.float32(REL_BUCKETS - max_exact)).astype(np.int32)
    large = np.minimum(large, REL_BUCKETS - 1)
    return np.where(n < max_exact, n, large).astype(np.int32)


def _sel_to_cmp_matrix(T, ncp):
    nc = T // CMP_STRIDE - CMP_BLOCK // CMP_STRIDE + 1
    ns = T // SEL_BLOCK
    cs = np.arange(nc) * CMP_STRIDE
    ss = np.arange(ns) * SEL_BLOCK
    lo = np.maximum(cs[None, :], ss[:, None])
    hi = np.minimum(cs[None, :] + CMP_BLOCK, ss[:, None] + SEL_BLOCK)
    out = np.zeros((ns, ncp), np.float32)
    out[:, :nc] = np.maximum(hi - lo, 0) / CMP_BLOCK
    return out


def _bias_tables(rel_bias, rel):
    bucket = _rel_bucket_table()[np.clip(rel, 0, REL_MAX_DIST)]
    onehot = (jnp.asarray(bucket.reshape(1, -1)) == jnp.arange(REL_BUCKETS, dtype=jnp.int32)[:, None]).astype(F32)
    tab = jnp.einsum('bh,bn->hn', rel_bias, onehot, precision=HIGHEST)
    return tab.reshape((rel_bias.shape[1],) + rel.shape)


def _prep_in_proj_weight(w_in_all, l):
    D = w_in_all.shape[1]
    nsa0 = RWKV_COLS
    kv0 = nsa0 + NSA_WIDTH
    gates0 = kv0 + 6 * NSA_KV_WIDTH
    merge0 = RWKV_COLS + NSA_COLS
    wt = jnp.swapaxes(w_in_all[l], 0, 1)
    kv = wt[kv0:gates0].reshape(3, 2, NSA_KV_HEADS, HEAD_DIM, D)
    kv = jnp.transpose(kv, (2, 0, 1, 3, 4)).reshape(6 * NSA_KV_WIDTH, D)
    pad = jnp.zeros((SMALL_W - (SM_GATES + 3 * NSA_HEADS), D), wt.dtype)
    return jnp.concatenate([
        wt[0:3 * RWKV_WIDTH],
        wt[nsa0:kv0],
        wt[merge0:merge0 + 2 * D_MODEL],
        kv,
        wt[3 * RWKV_WIDTH:RWKV_COLS],
        wt[gates0:merge0],
        pad,
    ], axis=0).astype(BF16)


def _prep_rwkv_params(mu, w0, w2, a0, a2, g2, k_k, k_a, r_k, ln_w, ln_b):
    C = RWKV_WIDTH
    row = lambda z: z.reshape(1, -1).astype(F32)
    mu_s = jnp.concatenate([mu[3 * C:], jnp.zeros((SMALL_W - (RWKV_COLS - 3 * C),), F32)]).reshape(1, SMALL_W)
    zl = jnp.zeros((DECAY_LORA, C), F32)

    w2p = jnp.concatenate([w2, zl], axis=0)
    w2h = w2p.astype(BF16)
    return dict(
        mu_r=row(mu[0:C]), mu_k=row(mu[C:2 * C]), mu_v=row(mu[2 * C:3 * C]), mu_s=mu_s,
        w0=row(w0), a0=row(a0), k_k=row(k_k), k_a=row(k_a), r_k=row(r_k), ln_w=row(ln_w), ln_b=row(ln_b),
        w2h=w2h, w2l=(w2p - w2h.astype(F32)).astype(BF16),
        a2=jnp.concatenate([zl, a2], axis=0).astype(BF16),
        g2=jnp.concatenate([g2, jnp.zeros((256 - GATE_LORA, C), F32)], axis=0).astype(BF16),
    )


def _prep_nsa_params(pe_k, w1_k, w2_k, pe_v, w1_v, w2_v, q_g, k_g, rel_bias, T):
    hd = HEAD_DIM
    ones = jnp.ones((hd,), F32)
    ncp = T // CMP_STRIDE
    NQ = T // QUERY_BLOCK

    def blockdiag(a, b):
        lead = ((0, 0),) * (a.ndim - 2)
        return jnp.pad(a, lead + ((0, hd), (0, hd))) + jnp.pad(b, lead + ((hd, 0), (hd, 0)))

    w1 = blockdiag(w1_k.reshape(CMP_BLOCK, hd, hd), w1_v.reshape(CMP_BLOCK, hd, hd))
    pe = jnp.concatenate([pe_k, pe_v], axis=1)
    half = CMP_STRIDE

    spread = np.zeros((NSA_GROUP * hd, NSA_GROUP * LANES), np.float32)
    gather = np.zeros((NSA_GROUP * LANES, NSA_GROUP * hd), np.float32)
    for h in range(NSA_GROUP):
        for d in range(hd):
            spread[h * hd + d, h * LANES + d] = 1.0
            gather[h * LANES + hd + d, h * hd + d] = 1.0
    gate_sel = np.zeros((NSA_KV_HEADS, SMALL_W, LANES), np.float32)
    for g in range(NSA_KV_HEADS):
        for j in range(3 * NSA_GROUP):
            gate_sel[g, SM_GATES + 3 * NSA_GROUP * g + j, j] = 1.0
    gate_rep = np.zeros((3, LANES, NSA_GROUP * hd), np.float32)
    for c in range(3):
        for h in range(NSA_GROUP):
            gate_rep[c, 3 * h + c, h * hd:(h + 1) * hd] = 1.0

    qi = np.arange(QUERY_BLOCK)[:, None]
    rel_s = qi + (SEL_CHUNK - QUERY_BLOCK) - np.arange(SEL_CHUNK)[None, :]
    rel_w = qi + WINDOW - np.arange(WIN_KEYS)[None, :]
    rel_c = ((np.arange(NQ)[:, None, None] * QUERY_BLOCK + qi[None])
             - (np.arange(ncp)[None, None, :] * CMP_STRIDE + CMP_BLOCK - 1))
    far = np.full((QUERY_BLOCK, 1), REL_MAX_DIST)
    tabs = _bias_tables(rel_bias, np.concatenate([rel_s, rel_w, far], axis=1))
    o1, o2 = SEL_CHUNK, SEL_CHUNK + WIN_KEYS
    masked = lambda tab, ok: jnp.where(jnp.asarray(ok)[None], tab, NEG_INF)
    stab = masked(tabs[:, :, 0:o1] - tabs[:, :, o2:], rel_s >= 0)
    wtab = masked(tabs[:, :, o1:o2], (rel_w >= 0) & (rel_w < WINDOW))
    ctab = masked(_bias_tables(rel_bias, rel_c), rel_c >= 0)
    return dict(
        stab=stab, wtab=wtab, ctab=ctab,
        q_g=jnp.tile(q_g, NSA_GROUP).reshape(1, -1),
        kg_cmp=jnp.concatenate([k_g[0], ones]).reshape(1, LANES),
        kg_sel=jnp.concatenate([k_g[1], ones]).reshape(1, LANES),
        kg_win=jnp.concatenate([k_g[2], ones]).reshape(1, LANES),
        pe_a=pe[:half].reshape(1, half * LANES), pe_b=pe[half:].reshape(1, half * LANES),
        cw_a=w1[:half].reshape(half * LANES, LANES).astype(BF16),
        cw_b=w1[half:].reshape(half * LANES, LANES).astype(BF16),
        cw2=blockdiag(w2_k, w2_v).astype(BF16),
        q_spread=jnp.asarray(spread, BF16), gather=jnp.asarray(gather, BF16),
        gate_sel=jnp.asarray(gate_sel), gate_rep=jnp.asarray(gate_rep, BF16),
        sel_mt=jnp.asarray(_sel_to_cmp_matrix(T, ncp)),
    )


def _front_pad(slab, flagged):
    B, G, _, L = slab.shape
    row = jnp.zeros((L,), slab.dtype)
    if flagged:
        row = row.at[PADFLAG_LANE].set(1.0)
    return jnp.concatenate([jnp.broadcast_to(row, (B, G, NSA_KPAD, L)), slab], axis=2)


def kernel(x, c, w_ada, b_ada, norm1_g, norm2_g, w_in, rwkv_mu, rwkv_w0, rwkv_w2, rwkv_a0, rwkv_a2, rwkv_g2, rwkv_k_k, rwkv_k_a, rwkv_r_k, rwkv_ln_w, rwkv_ln_b, cmp_pe_k, cmp_w1_k, cmp_w2_k, cmp_pe_v, cmp_w1_v, cmp_w2_v, q_norm_g, k_norm_g, rel_bias, w_o_rwkv, w_o_nsa, w_out, w_up, w_down):
    B, T, D = x.shape
    depth = w_in.shape[0]
    x2 = x.reshape(B * T, D)
    for l in range(depth):
        mod6 = _ada_mod(c, w_ada[l], b_ada[l]).reshape(B * 6, 1, D)
        proj = _in_proj(x2, norm1_g[l].reshape(1, D), mod6, _prep_in_proj_weight(w_in, l), T)
        pr = _prep_rwkv_params(rwkv_mu[l], rwkv_w0[l], rwkv_w2[l], rwkv_a0[l], rwkv_a2[l], rwkv_g2[l],
                               rwkv_k_k[l], rwkv_k_a[l], rwkv_r_k[l], rwkv_ln_w[l], rwkv_ln_b[l])
        o_a = _rwkv_mix(proj, pr, B, T)
        pn = _prep_nsa_params(cmp_pe_k[l], cmp_w1_k[l], cmp_w2_k[l], cmp_pe_v[l], cmp_w1_v[l], cmp_w2_v[l],
                              q_norm_g[l], k_norm_g[l], rel_bias, T)
        qp, ks, vs, kw, vw, gates = _nsa_prep(proj, pn, B, T)
        ks, vs, kw, vw = _front_pad(ks, True), _front_pad(vs, False), _front_pad(kw, True), _front_pad(vw, False)
        kc, vc = _nsa_compress(proj, pn, B, T)
        o_b = _nsa_attn(qp, kc, vc, ks, vs, kw, vw, gates, pn, B, T)
        mixed = _merge(o_a, o_b, w_o_rwkv[l].astype(BF16), w_o_nsa[l].astype(BF16), proj)
        x1, h2 = _out_proj(mixed, w_out[l].astype(BF16), x2, mod6, norm2_g[l].reshape(1, D), T)
        x2 = _mlp(h2, w_up[l].astype(BF16), w_down[l].astype(BF16), x1, mod6, T)
    return x2.reshape(B, T, D)
```

```python
import functools
import math

import numpy as np
import jax
import jax.numpy as jnp
from jax import lax
from jax.experimental import pallas as pl
from jax.experimental.pallas import tpu as pltpu

F32 = jnp.float32
BF16 = jnp.bfloat16
HIGHEST = lax.Precision.HIGHEST

D_MODEL = 2048
HEAD_DIM = 64
RWKV_WIDTH = D_MODEL // 2
DECAY_LORA = 64
ICLR_LORA = 64
GATE_LORA = 160
RWKV_GN_EPS = 64e-5
NSA_WIDTH = D_MODEL // 2
NSA_HEADS = NSA_WIDTH // HEAD_DIM
NSA_KV_HEADS = 4
NSA_GROUP = NSA_HEADS // NSA_KV_HEADS
NSA_KV_WIDTH = NSA_KV_HEADS * HEAD_DIM
CMP_BLOCK = 32
CMP_STRIDE = 16
SEL_BLOCK = 64
N_SEL = 8
WINDOW = 512
QUERY_BLOCK = 128
REL_BUCKETS = 32
REL_MAX_DIST = 128
D_FF = 4 * D_MODEL
NORM_EPS = 1e-6
NEG_INF = -1e30
FORCE_SCORE = 1e4

RWKV_COLS = 3 * RWKV_WIDTH + DECAY_LORA + ICLR_LORA + GATE_LORA
NSA_COLS = NSA_WIDTH + 6 * NSA_KV_WIDTH + 3 * NSA_HEADS

V7X_VMEM_BYTES = 64 * 1024 * 1024
LANES = 128

COL_RKV = 0
COL_Q = 3 * RWKV_WIDTH
COL_MERGE = COL_Q + NSA_WIDTH
COL_KV = COL_MERGE + 2 * D_MODEL
COL_SMALL = COL_KV + 6 * NSA_KV_WIDTH
SMALL_W = 512
PROJ_COLS = COL_SMALL + SMALL_W
SM_XG = DECAY_LORA + ICLR_LORA
SM_GATES = SM_XG + GATE_LORA


def _vmem_limit(nbytes):
    return int(min(nbytes * 5 // 4 + (4 << 20), V7X_VMEM_BYTES - (8 << 20)))


def _cparams(sem, vmem_bytes):
    return pltpu.CompilerParams(dimension_semantics=sem, vmem_limit_bytes=_vmem_limit(vmem_bytes))


def _ada_kernel(c_ref, w_ref, b_ref, o_ref):
    c = c_ref[...]
    s = c * jax.nn.sigmoid(c)
    o_ref[...] = jnp.dot(s.astype(BF16), w_ref[...].astype(BF16), preferred_element_type=F32) + b_ref[...]


def _ada_mod(c, w_ada, b_ada):
    B, D = c.shape
    N = w_ada.shape[1]
    tn = 1024
    return pl.pallas_call(
        _ada_kernel,
        grid=(N // tn,),
        in_specs=[
            pl.BlockSpec((B, D), lambda j: (0, 0)),
            pl.BlockSpec((D, tn), lambda j: (0, j)),
            pl.BlockSpec((1, tn), lambda j: (0, j)),
        ],
        out_specs=pl.BlockSpec((B, tn), lambda j: (0, j)),
        out_shape=jax.ShapeDtypeStruct((B, N), F32),
        compiler_params=_cparams(("parallel",), 2 * D * tn * 4 + D * tn * 2),
        name="ada_mod",
    )(c, w_ada, b_ada.reshape(1, N))


def _modulated_norm(x, g, sc, sh):
    ms = jnp.mean(x * x, axis=-1, keepdims=True)
    return (x * lax.rsqrt(ms + NORM_EPS) * g) * (1.0 + sc) + sh


def _inproj_kernel(x_ref, g_ref, sh_ref, sc_ref, w_ref, o_ref, h_scr):
    @pl.when(pl.program_id(1) == 0)
    def _():
        h_scr[...] = _modulated_norm(x_ref[...], g_ref[...], sc_ref[0], sh_ref[0]).astype(BF16)

    o_ref[...] = lax.dot_general(h_scr[...], w_ref[...], (((1,), (1,)), ((), ())),
                                 preferred_element_type=F32)


def _in_proj(x2, g1, mod6, w_in_t, T):
    BT, D = x2.shape
    NP = w_in_t.shape[0]
    tm = min(1024, T)
    tn = 1024
    tpb = T // tm
    vm = 2 * tm * D * 4 + tm * D * 2 + 2 * D * tn * 2 + 2 * tm * tn * 4 + 2 * tm * D * 4
    return pl.pallas_call(
        _inproj_kernel,
        grid=(BT // tm, NP // tn),
        in_specs=[
            pl.BlockSpec((tm, D), lambda i, j: (i, 0)),
            pl.BlockSpec((1, D), lambda i, j: (0, 0)),
            pl.BlockSpec((1, 1, D), lambda i, j: ((i // tpb) * 6 + 0, 0, 0)),
            pl.BlockSpec((1, 1, D), lambda i, j: ((i // tpb) * 6 + 1, 0, 0)),
            pl.BlockSpec((tn, D), lambda i, j: (j, 0)),
        ],
        out_specs=pl.BlockSpec((tm, tn), lambda i, j: (i, j)),
        out_shape=jax.ShapeDtypeStruct((BT, NP), F32),
        scratch_shapes=[pltpu.VMEM((tm, D), BF16)],
        compiler_params=_cparams(("parallel", "arbitrary"), vm),
        name="in_proj",
    )(x2, g1, mod6, mod6, w_in_t)


def _merge_kernel(oa_ref, ob_ref, wa_ref, wb_ref, ga_ref, gb_ref, o_ref):
    ya = jnp.dot(oa_ref[...], wa_ref[...], preferred_element_type=F32)
    yb = jnp.dot(ob_ref[...], wb_ref[...], preferred_element_type=F32)
    o_ref[...] = (jax.nn.sigmoid(ga_ref[...]) * ya + jax.nn.sigmoid(gb_ref[...]) * yb).astype(BF16)


def _merge(o_a, o_b, w_oa, w_ob, proj):
    BT, W = o_a.shape
    D = w_oa.shape[1]
    tm, tn = 512, 1024
    ga0 = COL_MERGE // tn
    gb0 = (COL_MERGE + D) // tn
    vm = 2 * (2 * tm * W * 2 + 2 * W * tn * 2 + 2 * tm * tn * 4 + tm * tn * 2) + 3 * tm * tn * 4
    return pl.pallas_call(
        _merge_kernel,
        grid=(BT // tm, D // tn),
        in_specs=[
            pl.BlockSpec((tm, W), lambda i, j: (i, 0)),
            pl.BlockSpec((tm, W), lambda i, j: (i, 0)),
            pl.BlockSpec((W, tn), lambda i, j: (0, j)),
            pl.BlockSpec((W, tn), lambda i, j: (0, j)),
            pl.BlockSpec((tm, tn), lambda i, j: (i, ga0 + j)),
            pl.BlockSpec((tm, tn), lambda i, j: (i, gb0 + j)),
        ],
        out_specs=pl.BlockSpec((tm, tn), lambda i, j: (i, j)),
        out_shape=jax.ShapeDtypeStruct((BT, D), BF16),
        compiler_params=_cparams(("parallel", "parallel"), vm),
        name="merge",
    )(o_a, o_b, w_oa, w_ob, proj, proj)


def _outproj_kernel(m_ref, w_ref, x_ref, gt_ref, g_ref, sh_ref, sc_ref, x1_ref, h2_ref):
    y = jnp.dot(m_ref[...], w_ref[...], preferred_element_type=F32)
    x1 = x_ref[...] + gt_ref[0] * y
    x1_ref[...] = x1
    h2_ref[...] = _modulated_norm(x1, g_ref[...], sc_ref[0], sh_ref[0]).astype(BF16)


def _out_proj(mixed, w_out, x2, mod6, g2, T):
    BT, D = x2.shape
    tm = min(512, T)
    tpb = T // tm
    vm = 2 * (tm * D * 2 + D * D * 2 + tm * D * 4 + tm * D * 4 + tm * D * 2) + 3 * tm * D * 4
    mod_spec = lambda k: pl.BlockSpec((1, 1, D), lambda i: ((i // tpb) * 6 + k, 0, 0))
    return pl.pallas_call(
        _outproj_kernel,
        grid=(BT // tm,),
        in_specs=[
            pl.BlockSpec((tm, D), lambda i: (i, 0)),
            pl.BlockSpec((D, D), lambda i: (0, 0)),
            pl.BlockSpec((tm, D), lambda i: (i, 0)),
            mod_spec(2),
            pl.BlockSpec((1, D), lambda i: (0, 0)),
            mod_spec(3),
            mod_spec(4),
        ],
        out_specs=[pl.BlockSpec((tm, D), lambda i: (i, 0)), pl.BlockSpec((tm, D), lambda i: (i, 0))],
        out_shape=[jax.ShapeDtypeStruct((BT, D), F32), jax.ShapeDtypeStruct((BT, D), BF16)],
        compiler_params=_cparams(("parallel",), vm),
        name="out_proj",
    )(mixed, w_out, x2, mod6, g2, mod6, mod6)


OUT_CHUNK = 256


def _mlp_kernel(h_ref, wu_ref, wd_ref, x_ref, gt_ref, o_ref, acc_ref):
    f = pl.program_id(1)

    @pl.when(f == 0)
    def _():
        acc_ref[...] = jnp.zeros_like(acc_ref)

    u = jnp.dot(h_ref[...], wu_ref[...], preferred_element_type=F32)
    u = jnp.square(jnp.maximum(u, 0.0)).astype(BF16)
    for n in range(acc_ref.shape[1] // OUT_CHUNK):
        cs = slice(n * OUT_CHUNK, (n + 1) * OUT_CHUNK)
        acc_ref[:, cs] += jnp.dot(u, wd_ref[:, cs], preferred_element_type=F32)

    @pl.when(f == pl.num_programs(1) - 1)
    def _():
        o_ref[...] = x_ref[...] + gt_ref[0] * acc_ref[...]


def _mlp(h2, w_up, w_down, x1, mod6, T):
    BT, D = x1.shape
    F = w_up.shape[1]
    tm = min(512, T)
    tf = 1024
    tpb = T // tm
    vm = 2 * (tm * D * 2 + 2 * D * tf * 2 + 2 * tm * D * 4) + tm * D * 4 + 2 * tm * tf * 4
    return pl.pallas_call(
        _mlp_kernel,
        grid=(BT // tm, F // tf),
        in_specs=[
            pl.BlockSpec((tm, D), lambda i, f: (i, 0)),
            pl.BlockSpec((D, tf), lambda i, f: (0, f)),
            pl.BlockSpec((tf, D), lambda i, f: (f, 0)),
            pl.BlockSpec((tm, D), lambda i, f: (i, 0)),
            pl.BlockSpec((1, 1, D), lambda i, f: ((i // tpb) * 6 + 5, 0, 0)),
        ],
        out_specs=pl.BlockSpec((tm, D), lambda i, f: (i, 0)),
        out_shape=jax.ShapeDtypeStruct((BT, D), F32),
        scratch_shapes=[pltpu.VMEM((tm, D), F32)],
        compiler_params=_cparams(("parallel", "arbitrary"), vm),
        name="mlp",
    )(h2, w_up, w_down, x1, mod6)


def _dotb(a, b):
    return jnp.dot(a.astype(BF16), b.astype(BF16), preferred_element_type=F32)


def _dotb_nt(a, b):
    return lax.dot_general(a.astype(BF16), b.astype(BF16), (((1,), (1,)), ((), ())),
                           preferred_element_type=F32)


def _split_bf16(x, terms):
    parts, rem = [], x
    for t in range(terms):
        p = rem.astype(BF16)
        parts.append(p)
        if t + 1 < terms:
            rem = rem - p.astype(F32)
    return parts


def _dot_sel(x, sel, terms):
    sel = sel.astype(BF16)
    return sum(jnp.dot(p, sel, preferred_element_type=F32) for p in _split_bf16(x, terms))


def _sel_dot(sel, x, terms):
    sel = sel.astype(BF16)
    return sum(jnp.dot(sel, p, preferred_element_type=F32) for p in _split_bf16(x, terms))


def _dot3(a, b_hi, b_lo):
    a_hi, a_lo = _split_bf16(a, 2)
    return (jnp.dot(a_hi, b_hi, preferred_element_type=F32) + jnp.dot(a_lo, b_hi, preferred_element_type=F32)
            + jnp.dot(a_hi, b_lo, preferred_element_type=F32))


def _iota(shape, axis):
    return lax.broadcasted_iota(jnp.int32, shape, axis)


def _same_head_mask(n):
    return (_iota((n, n), 0) >> 6) == (_iota((n, n), 1) >> 6)


RW_TC = 256
RW_C = 64
RW_LANES = 4 * HEAD_DIM


def _block_diag(x, bdmask):
    return jnp.where(bdmask, jnp.concatenate([x, x, x, x], axis=0), 0.0)


def _rwkv_kernel(r_ref, k_ref, v_ref, s_ref, mur_ref, muk_ref, muv_ref, mus_ref,
                 w0_ref, a0_ref, kk_ref, ka_ref, rk_ref, lnw_ref, lnb_ref,
                 w2h_ref, w2l_ref, a2_ref, g2_ref, o_ref,
                 pr_scr, pk_scr, pv_scr, ps_scr, state_scr):
    TC = r_ref.shape[0]
    C = RW_C
    W = RW_LANES

    @pl.when(pl.program_id(1) == 0)
    def _():
        pr_scr[...] = jnp.zeros_like(pr_scr)
        pk_scr[...] = jnp.zeros_like(pk_scr)
        pv_scr[...] = jnp.zeros_like(pv_scr)
        ps_scr[...] = jnp.zeros_like(ps_scr)
        state_scr[...] = jnp.zeros_like(state_scr)

    def shift_mix(p_ref, prev_scr, mu_ref):
        p = p_ref[...]
        rolled = pltpu.roll(p, 1, 0)
        first = jnp.where(_iota((8, p.shape[1]), 0) == 0, prev_scr[0:1, :], rolled[0:8])
        shifted = jnp.concatenate([first, rolled[8:]], axis=0)
        prev_scr[0:1, :] = p[TC - 1:TC, :]
        return p + (shifted - p) * mu_ref[...]

    CW = r_ref.shape[1]
    G = CW // W
    NQ = TC // C
    groups = lambda x: [x[:, g * W:(g + 1) * W] for g in range(G)]
    per_group = lambda f, x: jnp.concatenate([f(xg) for xg in groups(x)], axis=1)

    r = shift_mix(r_ref, pr_scr, mur_ref)
    k = shift_mix(k_ref, pk_scr, muk_ref)
    v = shift_mix(v_ref, pv_scr, muv_ref)
    sm = shift_mix(s_ref, ps_scr, mus_ref)
    xwa = sm[:, 0:SM_XG]

    bdmask = _same_head_mask(W)
    bones = jnp.where(bdmask, 1.0, 0.0)
    head_sum = lambda x: per_group(lambda xg: _dot_sel(xg, bones, 1), x)
    tri = jnp.where(_same_head_mask(TC) & (_iota((TC, TC), 1) <= _iota((TC, TC), 0)), 1.0, 0.0)
    lane = _iota((C, W), 1)
    row = _iota((C, W), 0)
    lanehead = lane >> 6
    strict = (lane & 63) < row
    incl = (lane & 63) <= row
    eye_cat = jnp.where((lane & 63) == row, 1.0, 0.0)

    wlin = w0_ref[...] + _dot3(jnp.tanh(xwa), w2h_ref[...], w2l_ref[...])
    a = jax.nn.sigmoid(a0_ref[...] + _dotb(xwa, a2_ref[...]))
    gate = _dotb(jax.nn.sigmoid(sm[:, SM_XG:SM_XG + 256]), g2_ref[...])
    z = -wlin
    softplus = jnp.maximum(z, 0.0) + jnp.log(1.0 + jnp.exp(-jnp.abs(z)))
    ld = -jnp.exp(-softplus - 0.5)
    cum = _sel_dot(tri, ld, 3)
    cum_last = jnp.concatenate(
        [jnp.broadcast_to(cum[(q + 1) * C - 1:(q + 1) * C, :], (C, CW)) for q in range(NQ)], axis=0)
    kk = k * kk_ref[...]
    kk = kk * lax.rsqrt(jnp.maximum(head_sum(kk * kk), 1e-24))
    k2 = k * (1.0 + (a - 1.0) * ka_ref[...])
    bvec = kk * a
    e_inv = jnp.exp(-cum)
    e_end = jnp.exp(cum_last - cum)
    At_f = -kk * jnp.exp(cum - ld)
    Rt_f = r * jnp.exp(cum)
    Bt_f = bvec * e_inv
    Kt_f = k2 * e_inv
    Bg_f = bvec * e_end
    Kg_f = k2 * e_end
    g_end = jnp.exp(cum_last)

    items = [(g, q) for q in range(NQ) for g in range(G)]
    blk = lambda x, g, q: x[q * C:(q + 1) * C, g * W:(g + 1) * W]
    bd01 = jnp.where(bdmask, 1.0, 0.0).astype(BF16)

    def bd(x):
        xb = x.astype(BF16)
        return jnp.concatenate([xb, xb, xb, xb], axis=0) * bd01

    At = {it: blk(At_f, *it) for it in items}
    Rt = {it: blk(Rt_f, *it) for it in items}
    Vq = {it: blk(v, *it) for it in items}
    AA = {}
    for it in items:
        bk = jnp.concatenate([jnp.where(lanehead == h, X, 0.0)
                              for X in (blk(Bt_f, *it), blk(Kt_f, *it)) for h in range(4)], axis=0)
        AA[it] = _dotb_nt(jnp.concatenate([At[it], Rt[it]], axis=0), bk)
    A_ab = {it: jnp.where(strict, AA[it][0:C, 0:W], 0.0) for it in items}
    A_ak = {it: jnp.where(strict, AA[it][0:C, W:2 * W], 0.0) for it in items}
    A_rb = {it: jnp.where(incl, AA[it][C:2 * C, 0:W], 0.0) for it in items}
    A_rk = {it: jnp.where(incl, AA[it][C:2 * C, W:2 * W], 0.0) for it in items}
    M = dict(A_ab)
    Tm = {it: eye_cat + A_ab[it] for it in items}
    for _ in range(5):
        M = {it: _dotb(M[it], bd(M[it])) for it in items}
        Tm = {it: Tm[it] + _dotb(M[it], bd(Tm[it])) for it in items}
    Vbd = {it: bd(Vq[it]) for it in items}
    akv = {it: _dotb(A_ak[it], Vbd[it]) for it in items}
    rkv = {it: _dotb(A_rk[it], Vbd[it]) for it in items}

    S = [state_scr[g] for g in range(G)]
    ys = []
    for q in range(NQ):
        its = [(g, q) for g in range(G)]
        rhs = [_dotb_nt(At[it], S[it[0]]) + akv[it] for it in its]
        U = [_dotb(Tm[it], bd(rhs[g])) for g, it in enumerate(its)]
        ys.append(jnp.concatenate(
            [_dotb_nt(Rt[it], S[g]) + _dotb(A_rb[it], bd(U[g])) + rkv[it] for g, it in enumerate(its)], axis=1))
        upd = [_dotb(jnp.concatenate([U[g], Vq[it]], axis=0).T,
                     jnp.concatenate([blk(Bg_f, *it), blk(Kg_f, *it)], axis=0)) for g, it in enumerate(its)]
        S = [S[g] * blk(g_end, g, q)[0:1, :] + jnp.where(bdmask, upd[g], 0.0) for g in range(G)]
    for g in range(G):
        state_scr[g] = S[g]
    y = jnp.concatenate(ys, axis=0)

    inv_n = 1.0 / HEAD_DIM
    d = y - head_sum(y) * inv_n
    var = head_sum(d * d) * inv_n
    yn = d * lax.rsqrt(var + RWKV_GN_EPS) * lnw_ref[...] + lnb_ref[...]
    bonus = head_sum(r * k2 * rk_ref[...]) * v
    o_ref[...] = ((yn + bonus) * gate).astype(BF16)


def _rwkv_mix(proj, pr, B, T):
    BT = proj.shape[0]
    TC = min(RW_TC, T)
    W = RW_LANES
    CW = RWKV_WIDTH
    nct = T // TC
    row = lambda b, c: b * nct + c
    full = lambda shape: pl.BlockSpec(shape, lambda b, c: (0, 0))
    vec = full((1, CW))
    in_specs = [
        pl.BlockSpec((TC, CW), lambda b, c: (row(b, c), 0)),
        pl.BlockSpec((TC, CW), lambda b, c: (row(b, c), 1)),
        pl.BlockSpec((TC, CW), lambda b, c: (row(b, c), 2)),
        pl.BlockSpec((TC, SMALL_W), lambda b, c: (row(b, c), COL_SMALL // SMALL_W)),
        vec, vec, vec,
        full((1, SMALL_W)),
        vec, vec, vec, vec, vec, vec, vec,
        full((SM_XG, CW)), full((SM_XG, CW)), full((SM_XG, CW)), full((256, CW)),
    ]
    vm = (2 * (3 * TC * CW * 4 + TC * SMALL_W * 4 + TC * CW * 2 + (3 * SM_XG + 256) * CW * 2)
          + 24 * TC * CW * 4)
    return pl.pallas_call(
        _rwkv_kernel,
        grid=(B, nct),
        in_specs=in_specs,
        out_specs=pl.BlockSpec((TC, CW), lambda b, c: (row(b, c), 0)),
        out_shape=jax.ShapeDtypeStruct((BT, CW), BF16),
        scratch_shapes=[pltpu.VMEM((8, CW), F32), pltpu.VMEM((8, CW), F32), pltpu.VMEM((8, CW), F32),
                        pltpu.VMEM((8, SMALL_W), F32), pltpu.VMEM((CW // W, W, W), F32)],
        compiler_params=_cparams(("parallel", "arbitrary"), vm),
        name="rwkv_mix",
    )(proj, proj, proj, proj, pr["mu_r"], pr["mu_k"], pr["mu_v"], pr["mu_s"],
      pr["w0"], pr["a0"], pr["k_k"], pr["k_a"], pr["r_k"], pr["ln_w"], pr["ln_b"],
      pr["w2h"], pr["w2l"], pr["a2"], pr["g2"])


NSA_TT = 256
BLOCK_LANE0 = HEAD_DIM
PADFLAG_LANE = BLOCK_LANE0 + 32


def _nsa_prep_kernel(q_ref, kv0_ref, kv1_ref, kv2_ref, s_ref, qg_ref, kgs_ref, kgw_ref, e_ref, gsel_ref,
                     qp_ref, ksl_ref, vsl_ref, kwl_ref, vwl_ref, gt_ref):
    tt = q_ref.shape[0]
    QW = NSA_GROUP * HEAD_DIM
    kv_refs = (kv0_ref, kv1_ref, kv2_ref)
    bones = jnp.where(_same_head_mask(QW), 1.0, 0.0)
    bones2 = jnp.where(_same_head_mask(LANES), 1.0, 0.0)
    lane = _iota((tt, LANES), 1)
    block_id = (pl.program_id(1) * tt + _iota((tt, LANES), 0)) >> 6
    onehot = jnp.where(lane == block_id + BLOCK_LANE0, 1.0, 0.0)
    small = _split_bf16(s_ref[...], 3)

    def pair(g, branch):
        off = (3 * g + branch) * LANES
        return kv_refs[off // SMALL_W][:, off % SMALL_W:off % SMALL_W + LANES]

    def slabs(g, x, gain, k_ref, v_ref):
        ms = _dot_sel(x * x, bones2, 1) * (1.0 / HEAD_DIM)
        k_ref[0, g] = jnp.where(lane < HEAD_DIM, x * lax.rsqrt(ms + NORM_EPS) * gain, onehot).astype(BF16)
        v_ref[0, g] = jnp.where(lane >= HEAD_DIM, x, 1.0).astype(BF16)

    for g in range(NSA_KV_HEADS):
        q = q_ref[:, g * QW:(g + 1) * QW]
        ms = _dot_sel(q * q, bones, 1) * (1.0 / HEAD_DIM)
        qn = (q * lax.rsqrt(ms + NORM_EPS) * qg_ref[...]) * (HEAD_DIM ** -0.5)
        qs = jnp.dot(qn.astype(BF16), e_ref[...], preferred_element_type=F32).astype(BF16)
        for h in range(NSA_GROUP):
            qp_ref[0, NSA_GROUP * g + h] = qs[:, h * LANES:(h + 1) * LANES]
        slabs(g, pair(g, 1), kgs_ref[...], ksl_ref, vsl_ref)
        slabs(g, pair(g, 2), kgw_ref[...], kwl_ref, vwl_ref)
        sel = gsel_ref[g].astype(BF16)
        gt_ref[0, g] = jax.nn.sigmoid(sum(jnp.dot(part, sel, preferred_element_type=F32) for part in small))


def _nsa_prep(proj, pn, B, T):
    tt = min(NSA_TT, T)
    ntt = T // tt
    G = NSA_KV_HEADS
    QW = NSA_GROUP * HEAD_DIM
    row = lambda b, t: b * ntt + t
    full = lambda shape: pl.BlockSpec(shape, lambda b, t: tuple(0 for _ in shape))
    kv_spec = lambda j: pl.BlockSpec((tt, SMALL_W), lambda b, t: (row(b, t), COL_KV // SMALL_W + j))
    vm = 2 * (tt * NSA_WIDTH * 4 + 4 * tt * SMALL_W * 4 + QW * 4 * LANES * 2 + G * SMALL_W * LANES * 4
              + 16 * tt * LANES * 2 + 16 * tt * LANES * 2 + 4 * tt * LANES * 4) + 24 * tt * QW * 4
    return pl.pallas_call(
        _nsa_prep_kernel,
        grid=(B, ntt),
        in_specs=[
            pl.BlockSpec((tt, NSA_WIDTH), lambda b, t: (row(b, t), COL_Q // NSA_WIDTH)),
            kv_spec(0), kv_spec(1), kv_spec(2),
            pl.BlockSpec((tt, SMALL_W), lambda b, t: (row(b, t), COL_SMALL // SMALL_W)),
            full((1, QW)), full((1, LANES)), full((1, LANES)), full((QW, NSA_GROUP * LANES)),
            full((G, SMALL_W, LANES)),
        ],
        out_specs=[pl.BlockSpec((1, NSA_HEADS, tt, LANES), lambda b, t: (b, 0, t, 0))]
        + [pl.BlockSpec((1, G, tt, LANES), lambda b, t: (b, 0, t, 0))] * 5,
        out_shape=[jax.ShapeDtypeStruct((B, NSA_HEADS, T, LANES), BF16)]
        + [jax.ShapeDtypeStruct((B, G, T, LANES), BF16)] * 4
        + [jax.ShapeDtypeStruct((B, G, T, LANES), F32)],
        compiler_params=_cparams(("parallel", "parallel"), vm),
        name="nsa_prep",
    )(proj, proj, proj, proj, proj, pn["q_g"], pn["kg_sel"], pn["kg_win"], pn["q_spread"], pn["gate_sel"])


def _gelu_tanh(x):
    return 0.5 * x * (1.0 + jnp.tanh(math.sqrt(2.0 / math.pi) * (x + 0.044715 * (x * x * x))))


def _nsa_compress_kernel(x_ref, pea_ref, peb_ref, wa_ref, wb_ref, w2_ref, kg_ref, kc_ref, vc_ref):
    nsub = x_ref.shape[0] // CMP_STRIDE
    xs = jnp.concatenate([x_ref[pl.ds(s, nsub, stride=CMP_STRIDE), :] for s in range(CMP_STRIDE)], axis=1)
    p0 = jnp.dot((xs + pea_ref[...]).astype(BF16), wa_ref[...], preferred_element_type=F32)
    p1 = jnp.dot((xs + peb_ref[...]).astype(BF16), wb_ref[...], preferred_element_type=F32)
    hid = _gelu_tanh(p0 + pltpu.roll(p1, nsub - 1, 0))
    out = jnp.dot(hid.astype(BF16), w2_ref[...], preferred_element_type=F32)
    bones2 = jnp.where(_same_head_mask(LANES), 1.0, 0.0)
    ms = _dot_sel(out * out, bones2, 2) * (1.0 / HEAD_DIM)
    is_k = _iota(out.shape, 1) < HEAD_DIM
    kc_ref[0, 0] = jnp.where(is_k, out * lax.rsqrt(ms + NORM_EPS) * kg_ref[...], 0.0).astype(BF16)
    vc_ref[0, 0] = jnp.where(is_k, 0.0, out).astype(BF16)


def _nsa_compress(proj, pn, B, T):
    G = NSA_KV_HEADS
    nsub = T // CMP_STRIDE
    kv_blk = COL_KV // LANES
    KW = CMP_STRIDE * LANES
    full = lambda shape: pl.BlockSpec(shape, lambda b, g: tuple(0 for _ in shape))
    vm = 2 * (T * LANES * 4 + 2 * KW * LANES * 2 + nsub * LANES * 2) + 6 * nsub * KW * 4
    return pl.pallas_call(
        _nsa_compress_kernel,
        grid=(B, G),
        in_specs=[
            pl.BlockSpec((T, LANES), lambda b, g: (b, kv_blk + 3 * g)),
            full((1, KW)), full((1, KW)), full((KW, LANES)), full((KW, LANES)), full((LANES, LANES)),
            full((1, LANES)),
        ],
        out_specs=[pl.BlockSpec((1, 1, nsub, LANES), lambda b, g: (b, g, 0, 0))] * 2,
        out_shape=[jax.ShapeDtypeStruct((B, G, nsub, LANES), BF16)] * 2,
        compiler_params=_cparams(("parallel", "parallel"), vm),
        name="nsa_compress",
    )(proj, pn["pe_a"], pn["pe_b"], pn["cw_a"], pn["cw_b"], pn["cw2"], pn["kg_cmp"])


NSA_GROUPS_PER_STEP = 4
NSA_KPAD = WINDOW
SEL_CHUNK = 4 * QUERY_BLOCK
WIN_KEYS = WINDOW + QUERY_BLOCK


def _lane_tile_max(s):
    tiles = [s[:, j * LANES:(j + 1) * LANES] for j in range(s.shape[1] // LANES)]
    while len(tiles) > 1:
        tiles = [jnp.maximum(a, b) for a, b in zip(tiles[0::2], tiles[1::2])] + ([tiles[-1]] if len(tiles) % 2 else [])
    return tiles[0]


def _nsa_attn_kernel(q_ref, kc_ref, vc_ref, ks_ref, vs_ref, kw_ref, vw_ref, gt_ref, stab_ref, wtab_ref, ctab_ref,
                     selmt_ref, gather_ref, grep_ref, o_ref, sbuf, mx_scr, acc_scr, sc_scr, *, n_sel):
    i = pl.program_id(2)
    QB = QUERY_BLOCK
    HG = NSA_GROUP
    R = HG * QB
    t0 = i * QB
    groups = range(kc_ref.shape[1])
    each = lambda f: [f(g) for g in groups]
    tile4 = lambda z: jnp.concatenate([z, z, z, z], axis=0)
    heads = lambda ref, g: ref[HG * g:HG * (g + 1)]
    lane = _iota((QB, LANES), 1)
    pad_mask = jnp.where(lane == PADFLAG_LANE, NEG_INF, 0.0)
    q = each(lambda g: q_ref[0, HG * g:HG * (g + 1)].reshape(R, LANES))
    q32 = each(lambda g: q[g].astype(F32))

    ncp = kc_ref.shape[2]
    s = each(lambda g: _dotb_nt(q[g], kc_ref[0, g]) + heads(ctab_ref, g)[:, 0].reshape(R, ncp))
    m = each(lambda g: jnp.max(s[g], axis=-1, keepdims=True))
    p_c = each(lambda g: jnp.exp(s[g] - m[g]))
    lsum = each(lambda g: jnp.sum(p_c[g], axis=-1, keepdims=True))
    p_c = each(lambda g: p_c[g] * jnp.where(m[g] > 0.5 * NEG_INF, 1.0 / lsum[g], 0.0))
    o_c = each(lambda g: jnp.dot(p_c[g].astype(BF16), vc_ref[0, g], preferred_element_type=F32))

    wrows = pl.ds(pl.multiple_of(t0, QB), WIN_KEYS)
    q_win = each(lambda g: (q32[g] + tile4(pad_mask)).astype(BF16))
    s = each(lambda g: _dotb_nt(q_win[g], kw_ref[0, g, wrows, :]) + heads(wtab_ref, g).reshape(R, WIN_KEYS))
    p = each(lambda g: jnp.exp(s[g] - jnp.max(_lane_tile_max(s[g]), axis=-1, keepdims=True)))
    acc_w = each(lambda g: jnp.dot(p[g].astype(BF16), vw_ref[0, g, wrows, :], preferred_element_type=F32))

    psum = each(lambda g: p_c[g][0:QB] + p_c[g][QB:2 * QB] + p_c[g][2 * QB:3 * QB] + p_c[g][3 * QB:4 * QB])
    selmt = selmt_ref[...]
    parts = each(lambda g: _split_bf16(psum[g], 3))
    imp = each(lambda g: sum(lax.dot_general(selmt, part, (((1,), (1,)), ((), ())), preferred_element_type=F32)
                             for part in parts[g]))
    ns = selmt.shape[0]
    blk = _iota((ns, QB), 0)
    cur = (t0 + _iota((ns, QB), 1)) >> 6
    forced = (blk == 0) | (blk == cur) | (blk == cur - 1)
    score = each(lambda g: jnp.where(forced, FORCE_SCORE, jnp.where(blk <= cur, imp[g], -1.0)))
    for g in groups:
        sc_scr[g] = score[g]
    ranks = [[] for _ in groups]
    for j in range(ns):
        lower = jnp.where(blk > j, 1.0, 0.0)
        for g in groups:
            other = sc_scr[g, j:j + 1, :]
            ranks[g].append(jnp.where(other > score[g], 1.0, 0.0) + jnp.where(other == score[g], lower, 0.0))
    while len(ranks[0]) > 1:
        ranks = [[a + b for a, b in zip(r[0::2], r[1::2])] + ([r[-1]] if len(r) % 2 else []) for r in ranks]
    chosen_t = each(lambda g: jnp.where(ranks[g][0] < n_sel, 1.0, 0.0))
    zrows = lambda n: jnp.zeros((n, QB), F32)
    chosen = each(lambda g: jnp.concatenate([zrows(BLOCK_LANE0), chosen_t[g], zrows(LANES - BLOCK_LANE0 - ns)],
                                            axis=0).T)
    is_block_lane = (lane >= BLOCK_LANE0) & (lane < BLOCK_LANE0 + ns)
    q_sel = each(lambda g: (q32[g] + tile4(jnp.where(is_block_lane, (chosen[g] - 1.0) * (-NEG_INF), pad_mask))
                            ).astype(BF16))

    def chunk_rows(c):
        return pl.ds(pl.multiple_of((i - 4 * c + 1) * QB, QB), SEL_CHUNK)

    s0 = each(lambda g: _dotb_nt(q_sel[g], ks_ref[0, g, chunk_rows(0), :]) + heads(stab_ref, g).reshape(R, SEL_CHUNK))
    for g in groups:
        sbuf[g, 0] = s0[g]
        mx_scr[g] = _lane_tile_max(s0[g])

    def scores_body(c, carry):
        s = each(lambda g: _dotb_nt(q_sel[g], ks_ref[0, g, chunk_rows(c), :]))
        for g in groups:
            sbuf[g, c] = s[g]
            mx_scr[g] = jnp.maximum(mx_scr[g], _lane_tile_max(s[g]))
        return carry

    n_far = i // 4
    lax.fori_loop(1, n_far + 1, scores_body, 0)
    m_s = each(lambda g: jnp.max(mx_scr[g], axis=-1, keepdims=True))

    acc_scr[...] = jnp.zeros_like(acc_scr)

    def values_body(c, carry):
        p = each(lambda g: jnp.exp(sbuf[g, c] - m_s[g]).astype(BF16))
        for g in groups:
            acc_scr[g] += jnp.dot(p[g], vs_ref[0, g, chunk_rows(c), :], preferred_element_type=F32)
        return carry

    lax.fori_loop(0, n_far + 1, values_body, 0)

    is_value = _iota((R, LANES), 1) >= HEAD_DIM
    normalised = lambda acc: jnp.where(is_value, acc * (1.0 / pltpu.roll(acc, HEAD_DIM, 1)), 0.0)

    def natural(o):
        cat = jnp.concatenate([o[h * QB:(h + 1) * QB] for h in range(HG)], axis=1).astype(BF16)
        return jnp.dot(cat, gather_ref[...], preferred_element_type=F32)

    branches = each(lambda g: [o_c[g], normalised(acc_scr[g]), normalised(acc_w[g])])
    gate = each(lambda g: [_dot_sel(gt_ref[0, g], grep_ref[c], 1) for c in range(3)])
    W = HG * HEAD_DIM
    for g in groups:
        out = sum(gate[g][c] * natural(branches[g][c]) for c in range(3))
        o_ref[:, g * W:(g + 1) * W] = out.astype(BF16)


def _nsa_attn(qp, kc, vc, ks, vs, kw, vw, gates, pn, B, T):
    G = NSA_KV_HEADS
    QB = QUERY_BLOCK
    NQ = T // QB
    ncp = kc.shape[2]
    ns = T // SEL_BLOCK
    TP = T + NSA_KPAD
    R = NSA_GROUP * QB
    n_sel = min(N_SEL, ns)
    n_chunks = (NQ - 1) // 4 + 1
    NG = NSA_GROUPS_PER_STEP
    HB = NSA_GROUP * NG
    once = pl.Buffered(1)
    gtab = lambda w: pl.BlockSpec((HB, QB, w), lambda b, g, i: (g, 0, 0), pipeline_mode=once)
    slab = lambda rows: pl.BlockSpec((1, NG, rows, LANES), lambda b, g, i: (b, g, 0, 0), pipeline_mode=once)
    vm = (NG * (2 * ncp * LANES * 2 + 4 * TP * LANES * 2 + R * (SEL_CHUNK + WIN_KEYS) * 4)
          + 2 * NG * (R * LANES * 2 + QB * LANES * 4 + R * ncp * 4 + QB * NSA_GROUP * HEAD_DIM * 2)
          + NG * (n_chunks * R * SEL_CHUNK * 4 + 2 * R * LANES * 4 + 4 * R * WIN_KEYS * 4))
    return pl.pallas_call(
        functools.partial(_nsa_attn_kernel, n_sel=n_sel),
        grid=(B, G // NG, NQ),
        in_specs=[
            pl.BlockSpec((1, HB, QB, LANES), lambda b, g, i: (b, g, i, 0)),
            slab(ncp), slab(ncp), slab(TP), slab(TP), slab(TP), slab(TP),
            pl.BlockSpec((1, NG, QB, LANES), lambda b, g, i: (b, g, i, 0)),
            gtab(SEL_CHUNK), gtab(WIN_KEYS),
            pl.BlockSpec((HB, 1, QB, ncp), lambda b, g, i: (g, i, 0, 0)),
            pl.BlockSpec((ns, ncp), lambda b, g, i: (0, 0)),
            pl.BlockSpec((NSA_GROUP * LANES, NSA_GROUP * HEAD_DIM), lambda b, g, i: (0, 0)),
            pl.BlockSpec((3, LANES, NSA_GROUP * HEAD_DIM), lambda b, g, i: (0, 0, 0)),
        ],
        out_specs=pl.BlockSpec((QB, HB * HEAD_DIM), lambda b, g, i: (b * NQ + i, g)),
        out_shape=jax.ShapeDtypeStruct((B * T, NSA_WIDTH), BF16),
        scratch_shapes=[pltpu.VMEM((NG, n_chunks, R, SEL_CHUNK), F32), pltpu.VMEM((NG, R, LANES), F32),
                        pltpu.VMEM((NG, R, LANES), F32), pltpu.VMEM((NG, ns, QB), F32)],
        compiler_params=_cparams(("parallel", "parallel", "arbitrary"), vm),
        name="nsa_attn",
    )(qp, kc, vc, ks, vs, kw, vw, gates, pn["stab"], pn["wtab"], pn["ctab"], pn["sel_mt"], pn["gather"],
      pn["gate_rep"])


def _rel_bucket_table():
    n = np.arange(REL_MAX_DIST + 1)
    max_exact = REL_BUCKETS // 2
    nf = np.maximum(n, max_exact).astype(np.float32)
    large = max_exact + (np.log(nf / np.float32(max_exact)) / np.float32(math.log(REL_MAX_DIST / max_exact))
                         * np.float32(REL_BUCKETS - max_exact)).astype(np.int32)
    large = np.minimum(large, REL_BUCKETS - 1)
    return np.where(n < max_exact, n, large).astype(np.int32)


def _sel_to_cmp_matrix(T, ncp):
    nc = T // CMP_STRIDE - CMP_BLOCK // CMP_STRIDE + 1
    ns = T // SEL_BLOCK
    cs = np.arange(nc) * CMP_STRIDE
    ss = np.arange(ns) * SEL_BLOCK
    lo = np.maximum(cs[None, :], ss[:, None])
    hi = np.minimum(cs[None, :] + CMP_BLOCK, ss[:, None] + SEL_BLOCK)
    out = np.zeros((ns, ncp), np.float32)
    out[:, :nc] = np.maximum(hi - lo, 0) / CMP_BLOCK
    return out


def _bias_tables(rel_bias, rel):
    bucket = _rel_bucket_table()[np.clip(rel, 0, REL_MAX_DIST)]
    onehot = (jnp.asarray(bucket.reshape(1, -1)) == jnp.arange(REL_BUCKETS, dtype=jnp.int32)[:, None]).astype(F32)
    tab = jnp.einsum('bh,bn->hn', rel_bias, onehot, precision=HIGHEST)
    return tab.reshape((rel_bias.shape[1],) + rel.shape)


def _prep_in_proj_weight(w_in_all, l):
    D = w_in_all.shape[1]
    nsa0 = RWKV_COLS
    kv0 = nsa0 + NSA_WIDTH
    gates0 = kv0 + 6 * NSA_KV_WIDTH
    merge0 = RWKV_COLS + NSA_COLS
    wt = jnp.swapaxes(w_in_all[l], 0, 1)
    kv = wt[kv0:gates0].reshape(3, 2, NSA_KV_HEADS, HEAD_DIM, D)
    kv = jnp.transpose(kv, (2, 0, 1, 3, 4)).reshape(6 * NSA_KV_WIDTH, D)
    pad = jnp.zeros((SMALL_W - (SM_GATES + 3 * NSA_HEADS), D), wt.dtype)
    return jnp.concatenate([
        wt[0:3 * RWKV_WIDTH],
        wt[nsa0:kv0],
        wt[merge0:merge0 + 2 * D_MODEL],
        kv,
        wt[3 * RWKV_WIDTH:RWKV_COLS],
        wt[gates0:merge0],
        pad,
    ], axis=0).astype(BF16)


def _prep_rwkv_params(mu, w0, w2, a0, a2, g2, k_k, k_a, r_k, ln_w, ln_b):
    C = RWKV_WIDTH
    row = lambda z: z.reshape(1, -1).astype(F32)
    mu_s = jnp.concatenate([mu[3 * C:], jnp.zeros((SMALL_W - (RWKV_COLS - 3 * C),), F32)]).reshape(1, SMALL_W)
    zl = jnp.zeros((DECAY_LORA, C), F32)

    w2p = jnp.concatenate([w2, zl], axis=0)
    w2h = w2p.astype(BF16)
    return dict(
        mu_r=row(mu[0:C]), mu_k=row(mu[C:2 * C]), mu_v=row(mu[2 * C:3 * C]), mu_s=mu_s,
        w0=row(w0), a0=row(a0), k_k=row(k_k), k_a=row(k_a), r_k=row(r_k), ln_w=row(ln_w), ln_b=row(ln_b),
        w2h=w2h, w2l=(w2p - w2h.astype(F32)).astype(BF16),
        a2=jnp.concatenate([zl, a2], axis=0).astype(BF16),
        g2=jnp.concatenate([g2, jnp.zeros((256 - GATE_LORA, C), F32)], axis=0).astype(BF16),
    )


def _prep_nsa_params(pe_k, w1_k, w2_k, pe_v, w1_v, w2_v, q_g, k_g, rel_bias, T):
    hd = HEAD_DIM
    ones = jnp.ones((hd,), F32)
    ncp = T // CMP_STRIDE
    NQ = T // QUERY_BLOCK

    def blockdiag(a, b):
        lead = ((0, 0),) * (a.ndim - 2)
        return jnp.pad(a, lead + ((0, hd), (0, hd))) + jnp.pad(b, lead + ((hd, 0), (hd, 0)))

    w1 = blockdiag(w1_k.reshape(CMP_BLOCK, hd, hd), w1_v.reshape(CMP_BLOCK, hd, hd))
    pe = jnp.concatenate([pe_k, pe_v], axis=1)
    half = CMP_STRIDE

    spread = np.zeros((NSA_GROUP * hd, NSA_GROUP * LANES), np.float32)
    gather = np.zeros((NSA_GROUP * LANES, NSA_GROUP * hd), np.float32)
    for h in range(NSA_GROUP):
        for d in range(hd):
            spread[h * hd + d, h * LANES + d] = 1.0
            gather[h * LANES + hd + d, h * hd + d] = 1.0
    gate_sel = np.zeros((NSA_KV_HEADS, SMALL_W, LANES), np.float32)
    for g in range(NSA_KV_HEADS):
        for j in range(3 * NSA_GROUP):
            gate_sel[g, SM_GATES + 3 * NSA_GROUP * g + j, j] = 1.0
    gate_rep = np.zeros((3, LANES, NSA_GROUP * hd), np.float32)
    for c in range(3):
        for h in range(NSA_GROUP):
            gate_rep[c, 3 * h + c, h * hd:(h + 1) * hd] = 1.0

    qi = np.arange(QUERY_BLOCK)[:, None]
    rel_s = qi + (SEL_CHUNK - QUERY_BLOCK) - np.arange(SEL_CHUNK)[None, :]
    rel_w = qi + WINDOW - np.arange(WIN_KEYS)[None, :]
    rel_c = ((np.arange(NQ)[:, None, None] * QUERY_BLOCK + qi[None])
             - (np.arange(ncp)[None, None, :] * CMP_STRIDE + CMP_BLOCK - 1))
    far = np.full((QUERY_BLOCK, 1), REL_MAX_DIST)
    tabs = _bias_tables(rel_bias, np.concatenate([rel_s, rel_w, far], axis=1))
    o1, o2 = SEL_CHUNK, SEL_CHUNK + WIN_KEYS
    masked = lambda tab, ok: jnp.where(jnp.asarray(ok)[None], tab, NEG_INF)
    stab = masked(tabs[:, :, 0:o1] - tabs[:, :, o2:], rel_s >= 0)
    wtab = masked(tabs[:, :, o1:o2], (rel_w >= 0) & (rel_w < WINDOW))
    ctab = masked(_bias_tables(rel_bias, rel_c), rel_c >= 0)
    return dict(
        stab=stab, wtab=wtab, ctab=ctab,
        q_g=jnp.tile(q_g, NSA_GROUP).reshape(1, -1),
        kg_cmp=jnp.concatenate([k_g[0], ones]).reshape(1, LANES),
        kg_sel=jnp.concatenate([k_g[1], ones]).reshape(1, LANES),
        kg_win=jnp.concatenate([k_g[2], ones]).reshape(1, LANES),
        pe_a=pe[:half].reshape(1, half * LANES), pe_b=pe[half:].reshape(1, half * LANES),
        cw_a=w1[:half].reshape(half * LANES, LANES).astype(BF16),
        cw_b=w1[half:].reshape(half * LANES, LANES).astype(BF16),
        cw2=blockdiag(w2_k, w2_v).astype(BF16),
        q_spread=jnp.asarray(spread, BF16), gather=jnp.asarray(gather, BF16),
        gate_sel=jnp.asarray(gate_sel), gate_rep=jnp.asarray(gate_rep, BF16),
        sel_mt=jnp.asarray(_sel_to_cmp_matrix(T, ncp)),
    )


def _front_pad(slab, flagged):
    B, G, _, L = slab.shape
    row = jnp.zeros((L,), slab.dtype)
    if flagged:
        row = row.at[PADFLAG_LANE].set(1.0)
    return jnp.concatenate([jnp.broadcast_to(row, (B, G, NSA_KPAD, L)), slab], axis=2)


def kernel(x, c, w_ada, b_ada, norm1_g, norm2_g, w_in, rwkv_mu, rwkv_w0, rwkv_w2, rwkv_a0, rwkv_a2, rwkv_g2, rwkv_k_k, rwkv_k_a, rwkv_r_k, rwkv_ln_w, rwkv_ln_b, cmp_pe_k, cmp_w1_k, cmp_w2_k, cmp_pe_v, cmp_w1_v, cmp_w2_v, q_norm_g, k_norm_g, rel_bias, w_o_rwkv, w_o_nsa, w_out, w_up, w_down):
    B, T, D = x.shape
    depth = w_in.shape[0]
    x2 = x.reshape(B * T, D)
    for l in range(depth):
        mod6 = _ada_mod(c, w_ada[l], b_ada[l]).reshape(B * 6, 1, D)
        proj = _in_proj(x2, norm1_g[l].reshape(1, D), mod6, _prep_in_proj_weight(w_in, l), T)
        pr = _prep_rwkv_params(rwkv_mu[l], rwkv_w0[l], rwkv_w2[l], rwkv_a0[l], rwkv_a2[l], rwkv_g2[l],
                               rwkv_k_k[l], rwkv_k_a[l], rwkv_r_k[l], rwkv_ln_w[l], rwkv_ln_b[l])
        o_a = _rwkv_mix(proj, pr, B, T)
        pn = _prep_nsa_params(cmp_pe_k[l], cmp_w1_k[l], cmp_w2_k[l], cmp_pe_v[l], cmp_w1_v[l], cmp_w2_v[l],
                              q_norm_g[l], k_norm_g[l], rel_bias, T)
        qp, ks, vs, kw, vw, gates = _nsa_prep(proj, pn, B, T)
        ks, vs, kw, vw = _front_pad(ks, True), _front_pad(vs, False), _front_pad(kw, True), _front_pad(vw, False)
        kc, vc = _nsa_compress(proj, pn, B, T)
        o_b = _nsa_attn(qp, kc, vc, ks, vs, kw, vw, gates, pn, B, T)
        mixed = _merge(o_a, o_b, w_o_rwkv[l].astype(BF16), w_o_nsa[l].astype(BF16), proj)
        x1, h2 = _out_proj(mixed, w_out[l].astype(BF16), x2, mod6, norm2_g[l].reshape(1, D), T)
        x2 = _mlp(h2, w_up[l].astype(BF16), w_down[l].astype(BF16), x1, mod6, T)
    return x2.reshape(B, T, D)
```

```python
import functools
import math

import numpy as np
import jax
import jax.numpy as jnp
from jax import lax
from jax.experimental import pallas as pl
from jax.experimental.pallas import tpu as pltpu

F32 = jnp.float32
BF16 = jnp.bfloat16
HIGHEST = lax.Precision.HIGHEST

D_MODEL = 2048
HEAD_DIM = 64
RWKV_WIDTH = D_MODEL // 2
DECAY_LORA = 64
ICLR_LORA = 64
GATE_LORA = 160
RWKV_GN_EPS = 64e-5
NSA_WIDTH = D_MODEL // 2
NSA_HEADS = NSA_WIDTH // HEAD_DIM
NSA_KV_HEADS = 4
NSA_GROUP = NSA_HEADS // NSA_KV_HEADS
NSA_KV_WIDTH = NSA_KV_HEADS * HEAD_DIM
CMP_BLOCK = 32
CMP_STRIDE = 16
SEL_BLOCK = 64
N_SEL = 8
WINDOW = 512
QUERY_BLOCK = 128
REL_BUCKETS = 32
REL_MAX_DIST = 128
D_FF = 4 * D_MODEL
NORM_EPS = 1e-6
NEG_INF = -1e30
FORCE_SCORE = 1e4

RWKV_COLS = 3 * RWKV_WIDTH + DECAY_LORA + ICLR_LORA + GATE_LORA
NSA_COLS = NSA_WIDTH + 6 * NSA_KV_WIDTH + 3 * NSA_HEADS

V7X_VMEM_BYTES = 64 * 1024 * 1024
LANES = 128

COL_RKV = 0
COL_Q = 3 * RWKV_WIDTH
COL_MERGE = COL_Q + NSA_WIDTH
COL_KV = COL_MERGE + 2 * D_MODEL
COL_SMALL = COL_KV + 6 * NSA_KV_WIDTH
SMALL_W = 512
PROJ_COLS = COL_SMALL + SMALL_W
SM_XG = DECAY_LORA + ICLR_LORA
SM_GATES = SM_XG + GATE_LORA


def _vmem_limit(nbytes):
    return int(min(nbytes * 5 // 4 + (4 << 20), V7X_VMEM_BYTES - (8 << 20)))


def _cparams(sem, vmem_bytes):
    return pltpu.CompilerParams(dimension_semantics=sem, vmem_limit_bytes=_vmem_limit(vmem_bytes))


def _ada_kernel(c_ref, w_ref, b_ref, o_ref):
    c = c_ref[...]
    s = c * jax.nn.sigmoid(c)
    o_ref[...] = jnp.dot(s.astype(BF16), w_ref[...].astype(BF16), preferred_element_type=F32) + b_ref[...]


def _ada_mod(c, w_ada, b_ada):
    B, D = c.shape
    N = w_ada.shape[1]
    tn = 1024
    return pl.pallas_call(
        _ada_kernel,
        grid=(N // tn,),
        in_specs=[
            pl.BlockSpec((B, D), lambda j: (0, 0)),
            pl.BlockSpec((D, tn), lambda j: (0, j)),
            pl.BlockSpec((1, tn), lambda j: (0, j)),
        ],
        out_specs=pl.BlockSpec((B, tn), lambda j: (0, j)),
        out_shape=jax.ShapeDtypeStruct((B, N), F32),
        compiler_params=_cparams(("parallel",), 2 * D * tn * 4 + D * tn * 2),
        name="ada_mod",
    )(c, w_ada, b_ada.reshape(1, N))


def _modulated_norm(x, g, sc, sh):
    ms = jnp.mean(x * x, axis=-1, keepdims=True)
    return (x * lax.rsqrt(ms + NORM_EPS) * g) * (1.0 + sc) + sh


def _inproj_kernel(x_ref, g_ref, sh_ref, sc_ref, w_ref, o_ref, small_ref, h_scr):
    j = pl.program_id(1)

    @pl.when(j == 0)
    def _():
        h_scr[...] = _modulated_norm(x_ref[...], g_ref[...], sc_ref[0], sh_ref[0]).astype(BF16)

    y = lax.dot_general(h_scr[...], w_ref[...], (((1,), (1,)), ((), ())),
                        preferred_element_type=F32)
    o_ref[...] = y.astype(BF16)

    @pl.when(j == pl.num_programs(1) - 1)
    def _():
        small_ref[...] = y[:, y.shape[1] - SMALL_W:]


def _in_proj(x2, g1, mod6, w_in_t, T):
    BT, D = x2.shape
    NP = w_in_t.shape[0]
    tm = min(1024, T)
    tn = 1024
    tpb = T // tm
    assert NP - SMALL_W == COL_SMALL and tn >= SMALL_W
    vm = (2 * tm * D * 4 + tm * D * 2 + 2 * D * tn * 2 + 2 * tm * tn * 2 + 2 * tm * SMALL_W * 4
          + tm * tn * 4 + 2 * tm * D * 4)
    return pl.pallas_call(
        _inproj_kernel,
        grid=(BT // tm, NP // tn),
        in_specs=[
            pl.BlockSpec((tm, D), lambda i, j: (i, 0)),
            pl.BlockSpec((1, D), lambda i, j: (0, 0)),
            pl.BlockSpec((1, 1, D), lambda i, j: ((i // tpb) * 6 + 0, 0, 0)),
            pl.BlockSpec((1, 1, D), lambda i, j: ((i // tpb) * 6 + 1, 0, 0)),
            pl.BlockSpec((tn, D), lambda i, j: (j, 0)),
        ],
        out_specs=[pl.BlockSpec((tm, tn), lambda i, j: (i, j)), pl.BlockSpec((tm, SMALL_W), lambda i, j: (i, 0))],
        out_shape=[jax.ShapeDtypeStruct((BT, NP), BF16), jax.ShapeDtypeStruct((BT, SMALL_W), F32)],
        scratch_shapes=[pltpu.VMEM((tm, D), BF16)],
        compiler_params=_cparams(("parallel", "arbitrary"), vm),
        name="in_proj",
    )(x2, g1, mod6, mod6, w_in_t)


def _merge_kernel(oa_ref, ob_ref, wa_ref, wb_ref, ga_ref, gb_ref, o_ref):
    ya = jnp.dot(oa_ref[...], wa_ref[...], preferred_element_type=F32)
    yb = jnp.dot(ob_ref[...], wb_ref[...], preferred_element_type=F32)
    ga, gb = ga_ref[...].astype(F32), gb_ref[...].astype(F32)
    o_ref[...] = (jax.nn.sigmoid(ga) * ya + jax.nn.sigmoid(gb) * yb).astype(BF16)


def _merge(o_a, o_b, w_oa, w_ob, proj):
    BT, W = o_a.shape
    D = w_oa.shape[1]
    tm, tn = 512, 1024
    ga0 = COL_MERGE // tn
    gb0 = (COL_MERGE + D) // tn
    vm = 2 * (2 * tm * W * 2 + 2 * W * tn * 2 + 2 * tm * tn * 4 + tm * tn * 2) + 3 * tm * tn * 4
    return pl.pallas_call(
        _merge_kernel,
        grid=(BT // tm, D // tn),
        in_specs=[
            pl.BlockSpec((tm, W), lambda i, j: (i, 0)),
            pl.BlockSpec((tm, W), lambda i, j: (i, 0)),
            pl.BlockSpec((W, tn), lambda i, j: (0, j)),
            pl.BlockSpec((W, tn), lambda i, j: (0, j)),
            pl.BlockSpec((tm, tn), lambda i, j: (i, ga0 + j)),
            pl.BlockSpec((tm, tn), lambda i, j: (i, gb0 + j)),
        ],
        out_specs=pl.BlockSpec((tm, tn), lambda i, j: (i, j)),
        out_shape=jax.ShapeDtypeStruct((BT, D), BF16),
        compiler_params=_cparams(("parallel", "parallel"), vm),
        name="merge",
    )(o_a, o_b, w_oa, w_ob, proj, proj)


def _outproj_kernel(m_ref, w_ref, x_ref, gt_ref, g_ref, sh_ref, sc_ref, x1_ref, h2_ref):
    y = jnp.dot(m_ref[...], w_ref[...], preferred_element_type=F32)
    x1 = x_ref[...] + gt_ref[0] * y
    x1_ref[...] = x1
    h2_ref[...] = _modulated_norm(x1, g_ref[...], sc_ref[0], sh_ref[0]).astype(BF16)


def _out_proj(mixed, w_out, x2, mod6, g2, T):
    BT, D = x2.shape
    tm = min(512, T)
    tpb = T // tm
    vm = 2 * (tm * D * 2 + D * D * 2 + tm * D * 4 + tm * D * 4 + tm * D * 2) + 3 * tm * D * 4
    mod_spec = lambda k: pl.BlockSpec((1, 1, D), lambda i: ((i // tpb) * 6 + k, 0, 0))
    return pl.pallas_call(
        _outproj_kernel,
        grid=(BT // tm,),
        in_specs=[
            pl.BlockSpec((tm, D), lambda i: (i, 0)),
            pl.BlockSpec((D, D), lambda i: (0, 0)),
            pl.BlockSpec((tm, D), lambda i: (i, 0)),
            mod_spec(2),
            pl.BlockSpec((1, D), lambda i: (0, 0)),
            mod_spec(3),
            mod_spec(4),
        ],
        out_specs=[pl.BlockSpec((tm, D), lambda i: (i, 0)), pl.BlockSpec((tm, D), lambda i: (i, 0))],
        out_shape=[jax.ShapeDtypeStruct((BT, D), F32), jax.ShapeDtypeStruct((BT, D), BF16)],
        compiler_params=_cparams(("parallel",), vm),
        name="out_proj",
    )(mixed, w_out, x2, mod6, g2, mod6, mod6)


OUT_CHUNK = 256


def _mlp_kernel(h_ref, wu_ref, wd_ref, x_ref, gt_ref, o_ref, acc_ref):
    f = pl.program_id(1)

    @pl.when(f == 0)
    def _():
        acc_ref[...] = jnp.zeros_like(acc_ref)

    u = jnp.dot(h_ref[...], wu_ref[...], preferred_element_type=F32)
    u = jnp.square(jnp.maximum(u, 0.0)).astype(BF16)
    for n in range(acc_ref.shape[1] // OUT_CHUNK):
        cs = slice(n * OUT_CHUNK, (n + 1) * OUT_CHUNK)
        acc_ref[:, cs] += jnp.dot(u, wd_ref[:, cs], preferred_element_type=F32)

    @pl.when(f == pl.num_programs(1) - 1)
    def _():
        o_ref[...] = x_ref[...] + gt_ref[0] * acc_ref[...]


def _mlp(h2, w_up, w_down, x1, mod6, T):
    BT, D = x1.shape
    F = w_up.shape[1]
    tm = min(512, T)
    tf = 1024
    tpb = T // tm
    vm = 2 * (tm * D * 2 + 2 * D * tf * 2 + 2 * tm * D * 4) + tm * D * 4 + 2 * tm * tf * 4
    return pl.pallas_call(
        _mlp_kernel,
        grid=(BT // tm, F // tf),
        in_specs=[
            pl.BlockSpec((tm, D), lambda i, f: (i, 0)),
            pl.BlockSpec((D, tf), lambda i, f: (0, f)),
            pl.BlockSpec((tf, D), lambda i, f: (f, 0)),
            pl.BlockSpec((tm, D), lambda i, f: (i, 0)),
            pl.BlockSpec((1, 1, D), lambda i, f: ((i // tpb) * 6 + 5, 0, 0)),
        ],
        out_specs=pl.BlockSpec((tm, D), lambda i, f: (i, 0)),
        out_shape=jax.ShapeDtypeStruct((BT, D), F32),
        scratch_shapes=[pltpu.VMEM((tm, D), F32)],
        compiler_params=_cparams(("parallel", "arbitrary"), vm),
        name="mlp",
    )(h2, w_up, w_down, x1, mod6)


def _dotb(a, b):
    return jnp.dot(a.astype(BF16), b.astype(BF16), preferred_element_type=F32)


def _dotb_nt(a, b):
    return lax.dot_general(a.astype(BF16), b.astype(BF16), (((1,), (1,)), ((), ())),
                           preferred_element_type=F32)


def _split_bf16(x, terms):
    parts, rem = [], x
    for t in range(terms):
        p = rem.astype(BF16)
        parts.append(p)
        if t + 1 < terms:
            rem = rem - p.astype(F32)
    return parts


def _dot_sel(x, sel, terms):
    sel = sel.astype(BF16)
    return sum(jnp.dot(p, sel, preferred_element_type=F32) for p in _split_bf16(x, terms))


def _sel_dot(sel, x, terms):
    sel = sel.astype(BF16)
    return sum(jnp.dot(sel, p, preferred_element_type=F32) for p in _split_bf16(x, terms))


def _dot3(a, b_hi, b_lo):
    a_hi, a_lo = _split_bf16(a, 2)
    return (jnp.dot(a_hi, b_hi, preferred_element_type=F32) + jnp.dot(a_lo, b_hi, preferred_element_type=F32)
            + jnp.dot(a_hi, b_lo, preferred_element_type=F32))


def _iota(shape, axis):
    return lax.broadcasted_iota(jnp.int32, shape, axis)


def _same_head_mask(n):
    return (_iota((n, n), 0) >> 6) == (_iota((n, n), 1) >> 6)


RW_TC = 256
RW_C = 64
RW_LANES = 4 * HEAD_DIM


def _block_diag(x, bdmask):
    return jnp.where(bdmask, jnp.concatenate([x, x, x, x], axis=0), 0.0)


def _rwkv_kernel(r_ref, k_ref, v_ref, s_ref, mur_ref, muk_ref, muv_ref, mus_ref,
                 w0_ref, a0_ref, kk_ref, ka_ref, rk_ref, lnw_ref, lnb_ref,
                 w2h_ref, w2l_ref, a2_ref, g2_ref, o_ref,
                 pr_scr, pk_scr, pv_scr, ps_scr, state_scr):
    TC = r_ref.shape[0]
    C = RW_C
    W = RW_LANES

    @pl.when(pl.program_id(1) == 0)
    def _():
        pr_scr[...] = jnp.zeros_like(pr_scr)
        pk_scr[...] = jnp.zeros_like(pk_scr)
        pv_scr[...] = jnp.zeros_like(pv_scr)
        ps_scr[...] = jnp.zeros_like(ps_scr)
        state_scr[...] = jnp.zeros_like(state_scr)

    def shift_mix(p_ref, prev_scr, mu_ref):
        p = p_ref[...].astype(F32)
        rolled = pltpu.roll(p, 1, 0)
        first = jnp.where(_iota((8, p.shape[1]), 0) == 0, prev_scr[0:1, :], rolled[0:8])
        shifted = jnp.concatenate([first, rolled[8:]], axis=0)
        prev_scr[0:1, :] = p[TC - 1:TC, :]
        return p + (shifted - p) * mu_ref[...]

    CW = r_ref.shape[1]
    G = CW // W
    NQ = TC // C
    groups = lambda x: [x[:, g * W:(g + 1) * W] for g in range(G)]
    per_group = lambda f, x: jnp.concatenate([f(xg) for xg in groups(x)], axis=1)

    r = shift_mix(r_ref, pr_scr, mur_ref)
    k = shift_mix(k_ref, pk_scr, muk_ref)
    v = shift_mix(v_ref, pv_scr, muv_ref)
    sm = shift_mix(s_ref, ps_scr, mus_ref)
    xwa = sm[:, 0:SM_XG]

    bdmask = _same_head_mask(W)
    bones = jnp.where(bdmask, 1.0, 0.0)
    head_sum = lambda x: per_group(lambda xg: _dot_sel(xg, bones, 1), x)
    tri = jnp.where(_same_head_mask(TC) & (_iota((TC, TC), 1) <= _iota((TC, TC), 0)), 1.0, 0.0)
    lane = _iota((C, W), 1)
    row = _iota((C, W), 0)
    lanehead = lane >> 6
    strict = (lane & 63) < row
    incl = (lane & 63) <= row
    eye_cat = jnp.where((lane & 63) == row, 1.0, 0.0)

    wlin = w0_ref[...] + _dot3(jnp.tanh(xwa), w2h_ref[...], w2l_ref[...])
    a = jax.nn.sigmoid(a0_ref[...] + _dotb(xwa, a2_ref[...]))
    gate = _dotb(jax.nn.sigmoid(sm[:, SM_XG:SM_XG + 256]), g2_ref[...])
    z = -wlin
    softplus = jnp.maximum(z, 0.0) + jnp.log(1.0 + jnp.exp(-jnp.abs(z)))
    ld = -jnp.exp(-softplus - 0.5)
    cum = _sel_dot(tri, ld, 3)
    cum_last = jnp.concatenate(
        [jnp.broadcast_to(cum[(q + 1) * C - 1:(q + 1) * C, :], (C, CW)) for q in range(NQ)], axis=0)
    kk = k * kk_ref[...]
    kk = kk * lax.rsqrt(jnp.maximum(head_sum(kk * kk), 1e-24))
    k2 = k * (1.0 + (a - 1.0) * ka_ref[...])
    bvec = kk * a
    e_inv = jnp.exp(-cum)
    e_end = jnp.exp(cum_last - cum)
    At_f = -kk * jnp.exp(cum - ld)
    Rt_f = r * jnp.exp(cum)
    Bt_f = bvec * e_inv
    Kt_f = k2 * e_inv
    Bg_f = bvec * e_end
    Kg_f = k2 * e_end
    g_end = jnp.exp(cum_last)

    items = [(g, q) for q in range(NQ) for g in range(G)]
    blk = lambda x, g, q: x[q * C:(q + 1) * C, g * W:(g + 1) * W]
    bd01 = jnp.where(bdmask, 1.0, 0.0).astype(BF16)

    def bd(x):
        xb = x.astype(BF16)
        return jnp.concatenate([xb, xb, xb, xb], axis=0) * bd01

    At = {it: blk(At_f, *it) for it in items}
    Rt = {it: blk(Rt_f, *it) for it in items}
    Vq = {it: blk(v, *it) for it in items}
    AA = {}
    for it in items:
        bk = jnp.concatenate([jnp.where(lanehead == h, X, 0.0)
                              for X in (blk(Bt_f, *it), blk(Kt_f, *it)) for h in range(4)], axis=0)
        AA[it] = _dotb_nt(jnp.concatenate([At[it], Rt[it]], axis=0), bk)
    A_ab = {it: jnp.where(strict, AA[it][0:C, 0:W], 0.0) for it in items}
    A_ak = {it: jnp.where(strict, AA[it][0:C, W:2 * W], 0.0) for it in items}
    A_rb = {it: jnp.where(incl, AA[it][C:2 * C, 0:W], 0.0) for it in items}
    A_rk = {it: jnp.where(incl, AA[it][C:2 * C, W:2 * W], 0.0) for it in items}
    M = dict(A_ab)
    Tm = {it: eye_cat + A_ab[it] for it in items}
    for _ in range(5):
        M = {it: _dotb(M[it], bd(M[it])) for it in items}
        Tm = {it: Tm[it] + _dotb(M[it], bd(Tm[it])) for it in items}
    Vbd = {it: bd(Vq[it]) for it in items}
    akv = {it: _dotb(A_ak[it], Vbd[it]) for it in items}
    rkv = {it: _dotb(A_rk[it], Vbd[it]) for it in items}

    S = [state_scr[g] for g in range(G)]
    ys = []
    for q in range(NQ):
        its = [(g, q) for g in range(G)]
        rhs = [_dotb_nt(At[it], S[it[0]]) + akv[it] for it in its]
        U = [_dotb(Tm[it], bd(rhs[g])) for g, it in enumerate(its)]
        ys.append(jnp.concatenate(
            [_dotb_nt(Rt[it], S[g]) + _dotb(A_rb[it], bd(U[g])) + rkv[it] for g, it in enumerate(its)], axis=1))
        upd = [_dotb(jnp.concatenate([U[g], Vq[it]], axis=0).T,
                     jnp.concatenate([blk(Bg_f, *it), blk(Kg_f, *it)], axis=0)) for g, it in enumerate(its)]
        S = [S[g] * blk(g_end, g, q)[0:1, :] + jnp.where(bdmask, upd[g], 0.0) for g in range(G)]
    for g in range(G):
        state_scr[g] = S[g]
    y = jnp.concatenate(ys, axis=0)

    inv_n = 1.0 / HEAD_DIM
    d = y - head_sum(y) * inv_n
    var = head_sum(d * d) * inv_n
    yn = d * lax.rsqrt(var + RWKV_GN_EPS) * lnw_ref[...] + lnb_ref[...]
    bonus = head_sum(r * k2 * rk_ref[...]) * v
    o_ref[...] = ((yn + bonus) * gate).astype(BF16)


def _rwkv_mix(proj, small, pr, B, T):
    BT = proj.shape[0]
    TC = min(RW_TC, T)
    W = RW_LANES
    CW = RWKV_WIDTH
    nct = T // TC
    row = lambda b, c: b * nct + c
    full = lambda shape: pl.BlockSpec(shape, lambda b, c: (0, 0))
    vec = full((1, CW))
    in_specs = [
        pl.BlockSpec((TC, CW), lambda b, c: (row(b, c), 0)),
        pl.BlockSpec((TC, CW), lambda b, c: (row(b, c), 1)),
        pl.BlockSpec((TC, CW), lambda b, c: (row(b, c), 2)),
        pl.BlockSpec((TC, SMALL_W), lambda b, c: (row(b, c), 0)),
        vec, vec, vec,
        full((1, SMALL_W)),
        vec, vec, vec, vec, vec, vec, vec,
        full((SM_XG, CW)), full((SM_XG, CW)), full((SM_XG, CW)), full((256, CW)),
    ]
    vm = (2 * (3 * TC * CW * 4 + TC * SMALL_W * 4 + TC * CW * 2 + (3 * SM_XG + 256) * CW * 2)
          + 24 * TC * CW * 4)
    return pl.pallas_call(
        _rwkv_kernel,
        grid=(B, nct),
        in_specs=in_specs,
        out_specs=pl.BlockSpec((TC, CW), lambda b, c: (row(b, c), 0)),
        out_shape=jax.ShapeDtypeStruct((BT, CW), BF16),
        scratch_shapes=[pltpu.VMEM((8, CW), F32), pltpu.VMEM((8, CW), F32), pltpu.VMEM((8, CW), F32),
                        pltpu.VMEM((8, SMALL_W), F32), pltpu.VMEM((CW // W, W, W), F32)],
        compiler_params=_cparams(("parallel", "arbitrary"), vm),
        name="rwkv_mix",
    )(proj, proj, proj, small, pr["mu_r"], pr["mu_k"], pr["mu_v"], pr["mu_s"],
      pr["w0"], pr["a0"], pr["k_k"], pr["k_a"], pr["r_k"], pr["ln_w"], pr["ln_b"],
      pr["w2h"], pr["w2l"], pr["a2"], pr["g2"])


NSA_TT = 256
BLOCK_LANE0 = HEAD_DIM
PADFLAG_LANE = BLOCK_LANE0 + 32


def _nsa_prep_kernel(q_ref, kv0_ref, kv1_ref, kv2_ref, s_ref, qg_ref, kgs_ref, kgw_ref, e_ref, gsel_ref,
                     qp_ref, ksl_ref, vsl_ref, kwl_ref, vwl_ref, gt_ref):
    tt = q_ref.shape[0]
    QW = NSA_GROUP * HEAD_DIM
    kv_refs = (kv0_ref, kv1_ref, kv2_ref)
    bones = jnp.where(_same_head_mask(QW), 1.0, 0.0)
    bones2 = jnp.where(_same_head_mask(LANES), 1.0, 0.0)
    lane = _iota((tt, LANES), 1)
    block_id = (pl.program_id(1) * tt + _iota((tt, LANES), 0)) >> 6
    onehot = jnp.where(lane == block_id + BLOCK_LANE0, 1.0, 0.0)
    small = _split_bf16(s_ref[...], 3)

    def pair(g, branch):
        off = (3 * g + branch) * LANES
        return kv_refs[off // SMALL_W][:, off % SMALL_W:off % SMALL_W + LANES].astype(F32)

    def slabs(g, x, gain, k_ref, v_ref):
        ms = _dot_sel(x * x, bones2, 1) * (1.0 / HEAD_DIM)
        k_ref[0, g] = jnp.where(lane < HEAD_DIM, x * lax.rsqrt(ms + NORM_EPS) * gain, onehot).astype(BF16)
        v_ref[0, g] = jnp.where(lane >= HEAD_DIM, x, 1.0).astype(BF16)

    for g in range(NSA_KV_HEADS):
        q = q_ref[:, g * QW:(g + 1) * QW].astype(F32)
        ms = _dot_sel(q * q, bones, 1) * (1.0 / HEAD_DIM)
        qn = (q * lax.rsqrt(ms + NORM_EPS) * qg_ref[...]) * (HEAD_DIM ** -0.5)
        qs = jnp.dot(qn.astype(BF16), e_ref[...], preferred_element_type=F32).astype(BF16)
        for h in range(NSA_GROUP):
            qp_ref[0, NSA_GROUP * g + h] = qs[:, h * LANES:(h + 1) * LANES]
        slabs(g, pair(g, 1), kgs_ref[...], ksl_ref, vsl_ref)
        slabs(g, pair(g, 2), kgw_ref[...], kwl_ref, vwl_ref)
        sel = gsel_ref[g].astype(BF16)
        gt_ref[0, g] = jax.nn.sigmoid(sum(jnp.dot(part, sel, preferred_element_type=F32) for part in small))


def _nsa_prep(proj, small, pn, B, T):
    tt = min(NSA_TT, T)
    ntt = T // tt
    G = NSA_KV_HEADS
    QW = NSA_GROUP * HEAD_DIM
    row = lambda b, t: b * ntt + t
    full = lambda shape: pl.BlockSpec(shape, lambda b, t: tuple(0 for _ in shape))
    kv_spec = lambda j: pl.BlockSpec((tt, SMALL_W), lambda b, t: (row(b, t), COL_KV // SMALL_W + j))
    vm = 2 * (tt * NSA_WIDTH * 4 + 4 * tt * SMALL_W * 4 + QW * 4 * LANES * 2 + G * SMALL_W * LANES * 4
              + 16 * tt * LANES * 2 + 16 * tt * LANES * 2 + 4 * tt * LANES * 4) + 24 * tt * QW * 4
    return pl.pallas_call(
        _nsa_prep_kernel,
        grid=(B, ntt),
        in_specs=[
            pl.BlockSpec((tt, NSA_WIDTH), lambda b, t: (row(b, t), COL_Q // NSA_WIDTH)),
            kv_spec(0), kv_spec(1), kv_spec(2),
            pl.BlockSpec((tt, SMALL_W), lambda b, t: (row(b, t), 0)),
            full((1, QW)), full((1, LANES)), full((1, LANES)), full((QW, NSA_GROUP * LANES)),
            full((G, SMALL_W, LANES)),
        ],
        out_specs=[pl.BlockSpec((1, NSA_HEADS, tt, LANES), lambda b, t: (b, 0, t, 0))]
        + [pl.BlockSpec((1, G, tt, LANES), lambda b, t: (b, 0, t, 0))] * 5,
        out_shape=[jax.ShapeDtypeStruct((B, NSA_HEADS, T, LANES), BF16)]
        + [jax.ShapeDtypeStruct((B, G, T, LANES), BF16)] * 4
        + [jax.ShapeDtypeStruct((B, G, T, LANES), F32)],
        compiler_params=_cparams(("parallel", "parallel"), vm),
        name="nsa_prep",
    )(proj, proj, proj, proj, small, pn["q_g"], pn["kg_sel"], pn["kg_win"], pn["q_spread"], pn["gate_sel"])


def _gelu_tanh(x):
    return 0.5 * x * (1.0 + jnp.tanh(math.sqrt(2.0 / math.pi) * (x + 0.044715 * (x * x * x))))


def _nsa_compress_kernel(x_ref, pea_ref, peb_ref, wa_ref, wb_ref, w2_ref, kg_ref, kc_ref, vc_ref, x_scr):
    nsub = x_ref.shape[0] // CMP_STRIDE
    x_scr[...] = x_ref[...].astype(F32)
    xs = jnp.concatenate([x_scr[pl.ds(s, nsub, stride=CMP_STRIDE), :] for s in range(CMP_STRIDE)], axis=1)
    p0 = jnp.dot((xs + pea_ref[...]).astype(BF16), wa_ref[...], preferred_element_type=F32)
    p1 = jnp.dot((xs + peb_ref[...]).astype(BF16), wb_ref[...], preferred_element_type=F32)
    hid = _gelu_tanh(p0 + pltpu.roll(p1, nsub - 1, 0))
    out = jnp.dot(hid.astype(BF16), w2_ref[...], preferred_element_type=F32)
    bones2 = jnp.where(_same_head_mask(LANES), 1.0, 0.0)
    ms = _dot_sel(out * out, bones2, 2) * (1.0 / HEAD_DIM)
    is_k = _iota(out.shape, 1) < HEAD_DIM
    kc_ref[0, 0] = jnp.where(is_k, out * lax.rsqrt(ms + NORM_EPS) * kg_ref[...], 0.0).astype(BF16)
    vc_ref[0, 0] = jnp.where(is_k, 0.0, out).astype(BF16)


def _nsa_compress(proj, pn, B, T):
    G = NSA_KV_HEADS
    nsub = T // CMP_STRIDE
    kv_blk = COL_KV // LANES
    KW = CMP_STRIDE * LANES
    full = lambda shape: pl.BlockSpec(shape, lambda b, g: tuple(0 for _ in shape))
    vm = 2 * (T * LANES * 4 + 2 * KW * LANES * 2 + nsub * LANES * 2) + 6 * nsub * KW * 4
    return pl.pallas_call(
        _nsa_compress_kernel,
        grid=(B, G),
        in_specs=[
            pl.BlockSpec((T, LANES), lambda b, g: (b, kv_blk + 3 * g)),
            full((1, KW)), full((1, KW)), full((KW, LANES)), full((KW, LANES)), full((LANES, LANES)),
            full((1, LANES)),
        ],
        out_specs=[pl.BlockSpec((1, 1, nsub, LANES), lambda b, g: (b, g, 0, 0))] * 2,
        out_shape=[jax.ShapeDtypeStruct((B, G, nsub, LANES), BF16)] * 2,
        scratch_shapes=[pltpu.VMEM((T, LANES), F32)],
        compiler_params=_cparams(("parallel", "parallel"), vm),
        name="nsa_compress",
    )(proj, pn["pe_a"], pn["pe_b"], pn["cw_a"], pn["cw_b"], pn["cw2"], pn["kg_cmp"])


NSA_GROUPS_PER_STEP = 4
NSA_KPAD = WINDOW
SEL_CHUNK = 4 * QUERY_BLOCK
WIN_KEYS = WINDOW + QUERY_BLOCK


def _lane_tile_max(s):
    tiles = [s[:, j * LANES:(j + 1) * LANES] for j in range(s.shape[1] // LANES)]
    while len(tiles) > 1:
        tiles = [jnp.maximum(a, b) for a, b in zip(tiles[0::2], tiles[1::2])] + ([tiles[-1]] if len(tiles) % 2 else [])
    return tiles[0]


def _nsa_attn_kernel(q_ref, kc_ref, vc_ref, ks_ref, vs_ref, kw_ref, vw_ref, gt_ref, stab_ref, wtab_ref, ctab_ref,
                     selmt_ref, gather_ref, grep_ref, o_ref, sbuf, mx_scr, acc_scr, sc_scr, *, n_sel):
    i = pl.program_id(2)
    QB = QUERY_BLOCK
    HG = NSA_GROUP
    R = HG * QB
    t0 = i * QB
    groups = range(kc_ref.shape[1])
    each = lambda f: [f(g) for g in groups]
    tile4 = lambda z: jnp.concatenate([z, z, z, z], axis=0)
    heads = lambda ref, g: ref[HG * g:HG * (g + 1)]
    lane = _iota((QB, LANES), 1)
    pad_mask = jnp.where(lane == PADFLAG_LANE, NEG_INF, 0.0)
    q = each(lambda g: q_ref[0, HG * g:HG * (g + 1)].reshape(R, LANES))
    q32 = each(lambda g: q[g].astype(F32))

    ncp = kc_ref.shape[2]
    s = each(lambda g: _dotb_nt(q[g], kc_ref[0, g]) + heads(ctab_ref, g)[:, 0].reshape(R, ncp))
    m = each(lambda g: jnp.max(s[g], axis=-1, keepdims=True))
    p_c = each(lambda g: jnp.exp(s[g] - m[g]))
    lsum = each(lambda g: jnp.sum(p_c[g], axis=-1, keepdims=True))
    p_c = each(lambda g: p_c[g] * jnp.where(m[g] > 0.5 * NEG_INF, 1.0 / lsum[g], 0.0))
    o_c = each(lambda g: jnp.dot(p_c[g].astype(BF16), vc_ref[0, g], preferred_element_type=F32))

    wrows = pl.ds(pl.multiple_of(t0, QB), WIN_KEYS)
    q_win = each(lambda g: (q32[g] + tile4(pad_mask)).astype(BF16))
    s = each(lambda g: _dotb_nt(q_win[g], kw_ref[0, g, wrows, :]) + heads(wtab_ref, g).reshape(R, WIN_KEYS))
    p = each(lambda g: jnp.exp(s[g] - jnp.max(_lane_tile_max(s[g]), axis=-1, keepdims=True)))
    acc_w = each(lambda g: jnp.dot(p[g].astype(BF16), vw_ref[0, g, wrows, :], preferred_element_type=F32))

    psum = each(lambda g: p_c[g][0:QB] + p_c[g][QB:2 * QB] + p_c[g][2 * QB:3 * QB] + p_c[g][3 * QB:4 * QB])
    selmt = selmt_ref[...]
    parts = each(lambda g: _split_bf16(psum[g], 3))
    imp = each(lambda g: sum(lax.dot_general(selmt, part, (((1,), (1,)), ((), ())), preferred_element_type=F32)
                             for part in parts[g]))
    ns = selmt.shape[0]
    blk = _iota((ns, QB), 0)
    cur = (t0 + _iota((ns, QB), 1)) >> 6
    forced = (blk == 0) | (blk == cur) | (blk == cur - 1)
    score = each(lambda g: jnp.where(forced, FORCE_SCORE, jnp.where(blk <= cur, imp[g], -1.0)))
    for g in groups:
        sc_scr[g] = score[g]
    ranks = [[] for _ in groups]
    for j in range(ns):
        lower = jnp.where(blk > j, 1.0, 0.0)
        for g in groups:
            other = sc_scr[g, j:j + 1, :]
            ranks[g].append(jnp.where(other > score[g], 1.0, 0.0) + jnp.where(other == score[g], lower, 0.0))
    while len(ranks[0]) > 1:
        ranks = [[a + b for a, b in zip(r[0::2], r[1::2])] + ([r[-1]] if len(r) % 2 else []) for r in ranks]
    chosen_t = each(lambda g: jnp.where(ranks[g][0] < n_sel, 1.0, 0.0))
    zrows = lambda n: jnp.zeros((n, QB), F32)
    chosen = each(lambda g: jnp.concatenate([zrows(BLOCK_LANE0), chosen_t[g], zrows(LANES - BLOCK_LANE0 - ns)],
                                            axis=0).T)
    is_block_lane = (lane >= BLOCK_LANE0) & (lane < BLOCK_LANE0 + ns)
    q_sel = each(lambda g: (q32[g] + tile4(jnp.where(is_block_lane, (chosen[g] - 1.0) * (-NEG_INF), pad_mask))
                            ).astype(BF16))

    def chunk_rows(c):
        return pl.ds(pl.multiple_of((i - 4 * c + 1) * QB, QB), SEL_CHUNK)

    s0 = each(lambda g: _dotb_nt(q_sel[g], ks_ref[0, g, chunk_rows(0), :]) + heads(stab_ref, g).reshape(R, SEL_CHUNK))
    for g in groups:
        sbuf[g, 0] = s0[g]
        mx_scr[g] = _lane_tile_max(s0[g])

    def scores_body(c, carry):
        s = each(lambda g: _dotb_nt(q_sel[g], ks_ref[0, g, chunk_rows(c), :]))
        for g in groups:
            sbuf[g, c] = s[g]
            mx_scr[g] = jnp.maximum(mx_scr[g], _lane_tile_max(s[g]))
        return carry

    n_far = i // 4
    lax.fori_loop(1, n_far + 1, scores_body, 0)
    m_s = each(lambda g: jnp.max(mx_scr[g], axis=-1, keepdims=True))

    acc_scr[...] = jnp.zeros_like(acc_scr)

    def values_body(c, carry):
        p = each(lambda g: jnp.exp(sbuf[g, c] - m_s[g]).astype(BF16))
        for g in groups:
            acc_scr[g] += jnp.dot(p[g], vs_ref[0, g, chunk_rows(c), :], preferred_element_type=F32)
        return carry

    lax.fori_loop(0, n_far + 1, values_body, 0)

    is_value = _iota((R, LANES), 1) >= HEAD_DIM
    normalised = lambda acc: jnp.where(is_value, acc * (1.0 / pltpu.roll(acc, HEAD_DIM, 1)), 0.0)

    def natural(o):
        cat = jnp.concatenate([o[h * QB:(h + 1) * QB] for h in range(HG)], axis=1).astype(BF16)
        return jnp.dot(cat, gather_ref[...], preferred_element_type=F32)

    branches = each(lambda g: [o_c[g], normalised(acc_scr[g]), normalised(acc_w[g])])
    gate = each(lambda g: [_dot_sel(gt_ref[0, g], grep_ref[c], 1) for c in range(3)])
    W = HG * HEAD_DIM
    for g in groups:
        out = sum(gate[g][c] * natural(branches[g][c]) for c in range(3))
        o_ref[:, g * W:(g + 1) * W] = out.astype(BF16)


def _nsa_attn(qp, kc, vc, ks, vs, kw, vw, gates, pn, B, T):
    G = NSA_KV_HEADS
    QB = QUERY_BLOCK
    NQ = T // QB
    ncp = kc.shape[2]
    ns = T // SEL_BLOCK
    TP = T + NSA_KPAD
    R = NSA_GROUP * QB
    n_sel = min(N_SEL, ns)
    n_chunks = (NQ - 1) // 4 + 1
    NG = NSA_GROUPS_PER_STEP
    HB = NSA_GROUP * NG
    once = pl.Buffered(1)
    gtab = lambda w: pl.BlockSpec((HB, QB, w), lambda b, g, i: (g, 0, 0), pipeline_mode=once)
    slab = lambda rows: pl.BlockSpec((1, NG, rows, LANES), lambda b, g, i: (b, g, 0, 0), pipeline_mode=once)
    vm = (NG * (2 * ncp * LANES * 2 + 4 * TP * LANES * 2 + R * (SEL_CHUNK + WIN_KEYS) * 4)
          + 2 * NG * (R * LANES * 2 + QB * LANES * 4 + R * ncp * 4 + QB * NSA_GROUP * HEAD_DIM * 2)
          + NG * (n_chunks * R * SEL_CHUNK * 4 + 2 * R * LANES * 4 + 4 * R * WIN_KEYS * 4))
    return pl.pallas_call(
        functools.partial(_nsa_attn_kernel, n_sel=n_sel),
        grid=(B, G // NG, NQ),
        in_specs=[
            pl.BlockSpec((1, HB, QB, LANES), lambda b, g, i: (b, g, i, 0)),
            slab(ncp), slab(ncp), slab(TP), slab(TP), slab(TP), slab(TP),
            pl.BlockSpec((1, NG, QB, LANES), lambda b, g, i: (b, g, i, 0)),
            gtab(SEL_CHUNK), gtab(WIN_KEYS),
            pl.BlockSpec((HB, 1, QB, ncp), lambda b, g, i: (g, i, 0, 0)),
            pl.BlockSpec((ns, ncp), lambda b, g, i: (0, 0)),
            pl.BlockSpec((NSA_GROUP * LANES, NSA_GROUP * HEAD_DIM), lambda b, g, i: (0, 0)),
            pl.BlockSpec((3, LANES, NSA_GROUP * HEAD_DIM), lambda b, g, i: (0, 0, 0)),
        ],
        out_specs=pl.BlockSpec((QB, HB * HEAD_DIM), lambda b, g, i: (b * NQ + i, g)),
        out_shape=jax.ShapeDtypeStruct((B * T, NSA_WIDTH), BF16),
        scratch_shapes=[pltpu.VMEM((NG, n_chunks, R, SEL_CHUNK), F32), pltpu.VMEM((NG, R, LANES), F32),
                        pltpu.VMEM((NG, R, LANES), F32), pltpu.VMEM((NG, ns, QB), F32)],
        compiler_params=_cparams(("parallel", "parallel", "arbitrary"), vm),
        name="nsa_attn",
    )(qp, kc, vc, ks, vs, kw, vw, gates, pn["stab"], pn["wtab"], pn["ctab"], pn["sel_mt"], pn["gather"],
      pn["gate_rep"])


def _rel_bucket_table():
    n = np.arange(REL_MAX_DIST + 1)
    max_exact = REL_BUCKETS // 2
    nf = np.maximum(n, max_exact).astype(np.float32)
    large = max_exact + (np.log(nf / np.float32(max_exact)) / np.float32(math.log(REL_MAX_DIST / max_exact))
                         * np.float32(REL_BUCKETS - max_exact)).astype(np.int32)
    large = np.minimum(large, REL_BUCKETS - 1)
    return np.where(n < max_exact, n, large).astype(np.int32)


def _sel_to_cmp_matrix(T, ncp):
    nc = T // CMP_STRIDE - CMP_BLOCK // CMP_STRIDE + 1
    ns = T // SEL_BLOCK
    cs = np.arange(nc) * CMP_STRIDE
    ss = np.arange(ns) * SEL_BLOCK
    lo = np.maximum(cs[None, :], ss[:, None])
    hi = np.minimum(cs[None, :] + CMP_BLOCK, ss[:, None] + SEL_BLOCK)
    out = np.zeros((ns, ncp), np.float32)
    out[:, :nc] = np.maximum(hi - lo, 0) / CMP_BLOCK
    return out


def _bias_tables(rel_bias, rel):
    bucket = _rel_bucket_table()[np.clip(rel, 0, REL_MAX_DIST)]
    onehot = (jnp.asarray(bucket.reshape(1, -1)) == jnp.arange(REL_BUCKETS, dtype=jnp.int32)[:, None]).astype(F32)
    tab = jnp.einsum('bh,bn->hn', rel_bias, onehot, precision=HIGHEST)
    return tab.reshape((rel_bias.shape[1],) + rel.shape)


def _prep_in_proj_weight(w_in_all, l):
    D = w_in_all.shape[1]
    nsa0 = RWKV_COLS
    kv0 = nsa0 + NSA_WIDTH
    gates0 = kv0 + 6 * NSA_KV_WIDTH
    merge0 = RWKV_COLS + NSA_COLS
    wt = jnp.swapaxes(w_in_all[l], 0, 1)
    kv = wt[kv0:gates0].reshape(3, 2, NSA_KV_HEADS, HEAD_DIM, D)
    kv = jnp.transpose(kv, (2, 0, 1, 3, 4)).reshape(6 * NSA_KV_WIDTH, D)
    pad = jnp.zeros((SMALL_W - (SM_GATES + 3 * NSA_HEADS), D), wt.dtype)
    return jnp.concatenate([
        wt[0:3 * RWKV_WIDTH],
        wt[nsa0:kv0],
        wt[merge0:merge0 + 2 * D_MODEL],
        kv,
        wt[3 * RWKV_WIDTH:RWKV_COLS],
        wt[gates0:merge0],
        pad,
    ], axis=0).astype(BF16)


def _prep_rwkv_params(mu, w0, w2, a0, a2, g2, k_k, k_a, r_k, ln_w, ln_b):
    C = RWKV_WIDTH
    row = lambda z: z.reshape(1, -1).astype(F32)
    mu_s = jnp.concatenate([mu[3 * C:], jnp.zeros((SMALL_W - (RWKV_COLS - 3 * C),), F32)]).reshape(1, SMALL_W)
    zl = jnp.zeros((DECAY_LORA, C), F32)

    w2p = jnp.concatenate([w2, zl], axis=0)
    w2h = w2p.astype(BF16)
    return dict(
        mu_r=row(mu[0:C]), mu_k=row(mu[C:2 * C]), mu_v=row(mu[2 * C:3 * C]), mu_s=mu_s,
        w0=row(w0), a0=row(a0), k_k=row(k_k), k_a=row(k_a), r_k=row(r_k), ln_w=row(ln_w), ln_b=row(ln_b),
        w2h=w2h, w2l=(w2p - w2h.astype(F32)).astype(BF16),
        a2=jnp.concatenate([zl, a2], axis=0).astype(BF16),
        g2=jnp.concatenate([g2, jnp.zeros((256 - GATE_LORA, C), F32)], axis=0).astype(BF16),
    )


def _prep_nsa_params(pe_k, w1_k, w2_k, pe_v, w1_v, w2_v, q_g, k_g, rel_bias, T):
    hd = HEAD_DIM
    ones = jnp.ones((hd,), F32)
    ncp = T // CMP_STRIDE
    NQ = T // QUERY_BLOCK

    def blockdiag(a, b):
        lead = ((0, 0),) * (a.ndim - 2)
        return jnp.pad(a, lead + ((0, hd), (0, hd))) + jnp.pad(b, lead + ((hd, 0), (hd, 0)))

    w1 = blockdiag(w1_k.reshape(CMP_BLOCK, hd, hd), w1_v.reshape(CMP_BLOCK, hd, hd))
    pe = jnp.concatenate([pe_k, pe_v], axis=1)
    half = CMP_STRIDE

    spread = np.zeros((NSA_GROUP * hd, NSA_GROUP * LANES), np.float32)
    gather = np.zeros((NSA_GROUP * LANES, NSA_GROUP * hd), np.float32)
    for h in range(NSA_GROUP):
        for d in range(hd):
            spread[h * hd + d, h * LANES + d] = 1.0
            gather[h * LANES + hd + d, h * hd + d] = 1.0
    gate_sel = np.zeros((NSA_KV_HEADS, SMALL_W, LANES), np.float32)
    for g in range(NSA_KV_HEADS):
        for j in range(3 * NSA_GROUP):
            gate_sel[g, SM_GATES + 3 * NSA_GROUP * g + j, j] = 1.0
    gate_rep = np.zeros((3, LANES, NSA_GROUP * hd), np.float32)
    for c in range(3):
        for h in range(NSA_GROUP):
            gate_rep[c, 3 * h + c, h * hd:(h + 1) * hd] = 1.0

    qi = np.arange(QUERY_BLOCK)[:, None]
    rel_s = qi + (SEL_CHUNK - QUERY_BLOCK) - np.arange(SEL_CHUNK)[None, :]
    rel_w = qi + WINDOW - np.arange(WIN_KEYS)[None, :]
    rel_c = ((np.arange(NQ)[:, None, None] * QUERY_BLOCK + qi[None])
             - (np.arange(ncp)[None, None, :] * CMP_STRIDE + CMP_BLOCK - 1))
    far = np.full((QUERY_BLOCK, 1), REL_MAX_DIST)
    tabs = _bias_tables(rel_bias, np.concatenate([rel_s, rel_w, far], axis=1))
    o1, o2 = SEL_CHUNK, SEL_CHUNK + WIN_KEYS
    masked = lambda tab, ok: jnp.where(jnp.asarray(ok)[None], tab, NEG_INF)
    stab = masked(tabs[:, :, 0:o1] - tabs[:, :, o2:], rel_s >= 0)
    wtab = masked(tabs[:, :, o1:o2], (rel_w >= 0) & (rel_w < WINDOW))
    ctab = masked(_bias_tables(rel_bias, rel_c), rel_c >= 0)
    return dict(
        stab=stab, wtab=wtab, ctab=ctab,
        q_g=jnp.tile(q_g, NSA_GROUP).reshape(1, -1),
        kg_cmp=jnp.concatenate([k_g[0], ones]).reshape(1, LANES),
        kg_sel=jnp.concatenate([k_g[1], ones]).reshape(1, LANES),
        kg_win=jnp.concatenate([k_g[2], ones]).reshape(1, LANES),
        pe_a=pe[:half].reshape(1, half * LANES), pe_b=pe[half:].reshape(1, half * LANES),
        cw_a=w1[:half].reshape(half * LANES, LANES).astype(BF16),
        cw_b=w1[half:].reshape(half * LANES, LANES).astype(BF16),
        cw2=blockdiag(w2_k, w2_v).astype(BF16),
        q_spread=jnp.asarray(spread, BF16), gather=jnp.asarray(gather, BF16),
        gate_sel=jnp.asarray(gate_sel), gate_rep=jnp.asarray(gate_rep, BF16),
        sel_mt=jnp.asarray(_sel_to_cmp_matrix(T, ncp)),
    )


def _front_pad(slab, flagged):
    B, G, _, L = slab.shape
    row = jnp.zeros((L,), slab.dtype)
    if flagged:
        row = row.at[PADFLAG_LANE].set(1.0)
    return jnp.concatenate([jnp.broadcast_to(row, (B, G, NSA_KPAD, L)), slab], axis=2)


def kernel(x, c, w_ada, b_ada, norm1_g, norm2_g, w_in, rwkv_mu, rwkv_w0, rwkv_w2, rwkv_a0, rwkv_a2, rwkv_g2, rwkv_k_k, rwkv_k_a, rwkv_r_k, rwkv_ln_w, rwkv_ln_b, cmp_pe_k, cmp_w1_k, cmp_w2_k, cmp_pe_v, cmp_w1_v, cmp_w2_v, q_norm_g, k_norm_g, rel_bias, w_o_rwkv, w_o_nsa, w_out, w_up, w_down):
    B, T, D = x.shape
    depth = w_in.shape[0]
    x2 = x.reshape(B * T, D)
    for l in range(depth):
        mod6 = _ada_mod(c, w_ada[l], b_ada[l]).reshape(B * 6, 1, D)
        proj, small = _in_proj(x2, norm1_g[l].reshape(1, D), mod6, _prep_in_proj_weight(w_in, l), T)
        pr = _prep_rwkv_params(rwkv_mu[l], rwkv_w0[l], rwkv_w2[l], rwkv_a0[l], rwkv_a2[l], rwkv_g2[l],
                               rwkv_k_k[l], rwkv_k_a[l], rwkv_r_k[l], rwkv_ln_w[l], rwkv_ln_b[l])
        o_a = _rwkv_mix(proj, small, pr, B, T)
        pn = _prep_nsa_params(cmp_pe_k[l], cmp_w1_k[l], cmp_w2_k[l], cmp_pe_v[l], cmp_w1_v[l], cmp_w2_v[l],
                              q_norm_g[l], k_norm_g[l], rel_bias, T)
        qp, ks, vs, kw, vw, gates = _nsa_prep(proj, small, pn, B, T)
        ks, vs, kw, vw = _front_pad(ks, True), _front_pad(vs, False), _front_pad(kw, True), _front_pad(vw, False)
        kc, vc = _nsa_compress(proj, pn, B, T)
        o_b = _nsa_attn(qp, kc, vc, ks, vs, kw, vw, gates, pn, B, T)
        mixed = _merge(o_a, o_b, w_o_rwkv[l].astype(BF16), w_o_nsa[l].astype(BF16), proj)
        x1, h2 = _out_proj(mixed, w_out[l].astype(BF16), x2, mod6, norm2_g[l].reshape(1, D), T)
        x2 = _mlp(h2, w_up[l].astype(BF16), w_down[l].astype(BF16), x1, mod6, T)
    return x2.reshape(B, T, D)
```

```python
import functools
import math

import numpy as np
import jax
import jax.numpy as jnp
from jax import lax
from jax.experimental import pallas as pl
from jax.experimental.pallas import tpu as pltpu

F32 = jnp.float32
BF16 = jnp.bfloat16
HIGHEST = lax.Precision.HIGHEST

D_MODEL = 2048
HEAD_DIM = 64
RWKV_WIDTH = D_MODEL // 2
DECAY_LORA = 64
ICLR_LORA = 64
GATE_LORA = 160
RWKV_GN_EPS = 64e-5
NSA_WIDTH = D_MODEL // 2
NSA_HEADS = NSA_WIDTH // HEAD_DIM
NSA_KV_HEADS = 4
NSA_GROUP = NSA_HEADS // NSA_KV_HEADS
NSA_KV_WIDTH = NSA_KV_HEADS * HEAD_DIM
CMP_BLOCK = 32
CMP_STRIDE = 16
SEL_BLOCK = 64
N_SEL = 8
WINDOW = 512
QUERY_BLOCK = 128
REL_BUCKETS = 32
REL_MAX_DIST = 128
D_FF = 4 * D_MODEL
NORM_EPS = 1e-6
NEG_INF = -1e30
FORCE_SCORE = 1e4

RWKV_COLS = 3 * RWKV_WIDTH + DECAY_LORA + ICLR_LORA + GATE_LORA
NSA_COLS = NSA_WIDTH + 6 * NSA_KV_WIDTH + 3 * NSA_HEADS

V7X_VMEM_BYTES = 64 * 1024 * 1024
LANES = 128

COL_RKV = 0
COL_Q = 3 * RWKV_WIDTH
COL_MERGE = COL_Q + NSA_WIDTH
COL_KV = COL_MERGE + 2 * D_MODEL
COL_SMALL = COL_KV + 6 * NSA_KV_WIDTH
SMALL_W = 512
PROJ_COLS = COL_SMALL + SMALL_W
SM_XG = DECAY_LORA + ICLR_LORA
SM_GATES = SM_XG + GATE_LORA


def _vmem_limit(nbytes):
    return int(min(nbytes * 5 // 4 + (4 << 20), V7X_VMEM_BYTES - (8 << 20)))


def _cparams(sem, vmem_bytes):
    return pltpu.CompilerParams(dimension_semantics=sem, vmem_limit_bytes=_vmem_limit(vmem_bytes))


def _ada_kernel(c_ref, w_ref, b_ref, o_ref):
    c = c_ref[...]
    s = c * jax.nn.sigmoid(c)
    o_ref[...] = jnp.dot(s.astype(BF16), w_ref[...].astype(BF16), preferred_element_type=F32) + b_ref[...]


def _ada_mod(c, w_ada, b_ada):
    B, D = c.shape
    N = w_ada.shape[1]
    tn = 1024
    return pl.pallas_call(
        _ada_kernel,
        grid=(N // tn,),
        in_specs=[
            pl.BlockSpec((B, D), lambda j: (0, 0)),
            pl.BlockSpec((D, tn), lambda j: (0, j)),
            pl.BlockSpec((1, tn), lambda j: (0, j)),
        ],
        out_specs=pl.BlockSpec((B, tn), lambda j: (0, j)),
        out_shape=jax.ShapeDtypeStruct((B, N), F32),
        compiler_params=_cparams(("parallel",), 2 * D * tn * 4 + D * tn * 2),
        name="ada_mod",
    )(c, w_ada, b_ada.reshape(1, N))


def _modulated_norm(x, g, sc, sh):
    ms = jnp.mean(x * x, axis=-1, keepdims=True)
    return (x * lax.rsqrt(ms + NORM_EPS) * g) * (1.0 + sc) + sh


def _inproj_kernel(x_ref, g_ref, sh_ref, sc_ref, w_ref, o_ref, small_ref, h_scr):
    j = pl.program_id(1)

    @pl.when(j == 0)
    def _():
        h_scr[...] = _modulated_norm(x_ref[...], g_ref[...], sc_ref[0], sh_ref[0]).astype(BF16)

    y = lax.dot_general(h_scr[...], w_ref[...], (((1,), (1,)), ((), ())),
                        preferred_element_type=F32)
    o_ref[...] = y.astype(BF16)

    @pl.when(j == pl.num_programs(1) - 1)
    def _():
        small_ref[...] = y[:, y.shape[1] - SMALL_W:]


def _in_proj(x2, g1, mod6, w_in_t, T):
    BT, D = x2.shape
    NP = w_in_t.shape[0]
    tm = min(1024, T)
    tn = 1024
    tpb = T // tm
    assert NP - SMALL_W == COL_SMALL and tn >= SMALL_W
    vm = (2 * tm * D * 4 + tm * D * 2 + 2 * D * tn * 2 + 2 * tm * tn * 2 + 2 * tm * SMALL_W * 4
          + tm * tn * 4 + 2 * tm * D * 4)
    return pl.pallas_call(
        _inproj_kernel,
        grid=(BT // tm, NP // tn),
        in_specs=[
            pl.BlockSpec((tm, D), lambda i, j: (i, 0)),
            pl.BlockSpec((1, D), lambda i, j: (0, 0)),
            pl.BlockSpec((1, 1, D), lambda i, j: ((i // tpb) * 6 + 0, 0, 0)),
            pl.BlockSpec((1, 1, D), lambda i, j: ((i // tpb) * 6 + 1, 0, 0)),
            pl.BlockSpec((tn, D), lambda i, j: (j, 0)),
        ],
        out_specs=[pl.BlockSpec((tm, tn), lambda i, j: (i, j)), pl.BlockSpec((tm, SMALL_W), lambda i, j: (i, 0))],
        out_shape=[jax.ShapeDtypeStruct((BT, NP), BF16), jax.ShapeDtypeStruct((BT, SMALL_W), F32)],
        scratch_shapes=[pltpu.VMEM((tm, D), BF16)],
        compiler_params=_cparams(("parallel", "arbitrary"), vm),
        name="in_proj",
    )(x2, g1, mod6, mod6, w_in_t)


def _merge_kernel(oa_ref, ob_ref, wa_ref, wb_ref, ga_ref, gb_ref, o_ref):
    ya = jnp.dot(oa_ref[...], wa_ref[...], preferred_element_type=F32)
    yb = jnp.dot(ob_ref[...], wb_ref[...], preferred_element_type=F32)
    ga, gb = ga_ref[...].astype(F32), gb_ref[...].astype(F32)
    o_ref[...] = (jax.nn.sigmoid(ga) * ya + jax.nn.sigmoid(gb) * yb).astype(BF16)


def _merge(o_a, o_b, w_oa, w_ob, proj):
    BT, W = o_a.shape
    D = w_oa.shape[1]
    tm, tn = 512, 1024
    ga0 = COL_MERGE // tn
    gb0 = (COL_MERGE + D) // tn
    vm = 2 * (2 * tm * W * 2 + 2 * W * tn * 2 + 2 * tm * tn * 4 + tm * tn * 2) + 3 * tm * tn * 4
    return pl.pallas_call(
        _merge_kernel,
        grid=(BT // tm, D // tn),
        in_specs=[
            pl.BlockSpec((tm, W), lambda i, j: (i, 0)),
            pl.BlockSpec((tm, W), lambda i, j: (i, 0)),
            pl.BlockSpec((W, tn), lambda i, j: (0, j)),
            pl.BlockSpec((W, tn), lambda i, j: (0, j)),
            pl.BlockSpec((tm, tn), lambda i, j: (i, ga0 + j)),
            pl.BlockSpec((tm, tn), lambda i, j: (i, gb0 + j)),
        ],
        out_specs=pl.BlockSpec((tm, tn), lambda i, j: (i, j)),
        out_shape=jax.ShapeDtypeStruct((BT, D), BF16),
        compiler_params=_cparams(("parallel", "parallel"), vm),
        name="merge",
    )(o_a, o_b, w_oa, w_ob, proj, proj)


def _outproj_kernel(m_ref, w_ref, x_ref, gt_ref, g_ref, sh_ref, sc_ref, x1_ref, h2_ref):
    y = jnp.dot(m_ref[...], w_ref[...], preferred_element_type=F32)
    x1 = x_ref[...] + gt_ref[0] * y
    x1_ref[...] = x1
    h2_ref[...] = _modulated_norm(x1, g_ref[...], sc_ref[0], sh_ref[0]).astype(BF16)


def _out_proj(mixed, w_out, x2, mod6, g2, T):
    BT, D = x2.shape
    tm = min(512, T)
    tpb = T // tm
    vm = 2 * (tm * D * 2 + D * D * 2 + tm * D * 4 + tm * D * 4 + tm * D * 2) + 3 * tm * D * 4
    mod_spec = lambda k: pl.BlockSpec((1, 1, D), lambda i: ((i // tpb) * 6 + k, 0, 0))
    return pl.pallas_call(
        _outproj_kernel,
        grid=(BT // tm,),
        in_specs=[
            pl.BlockSpec((tm, D), lambda i: (i, 0)),
            pl.BlockSpec((D, D), lambda i: (0, 0)),
            pl.BlockSpec((tm, D), lambda i: (i, 0)),
            mod_spec(2),
            pl.BlockSpec((1, D), lambda i: (0, 0)),
            mod_spec(3),
            mod_spec(4),
        ],
        out_specs=[pl.BlockSpec((tm, D), lambda i: (i, 0)), pl.BlockSpec((tm, D), lambda i: (i, 0))],
        out_shape=[jax.ShapeDtypeStruct((BT, D), F32), jax.ShapeDtypeStruct((BT, D), BF16)],
        compiler_params=_cparams(("parallel",), vm),
        name="out_proj",
    )(mixed, w_out, x2, mod6, g2, mod6, mod6)


OUT_CHUNK = 256


def _mlp_kernel(h_ref, wu_ref, wd_ref, x_ref, gt_ref, o_ref, acc_ref):
    f = pl.program_id(1)

    @pl.when(f == 0)
    def _():
        acc_ref[...] = jnp.zeros_like(acc_ref)

    u = jnp.dot(h_ref[...], wu_ref[...], preferred_element_type=F32)
    u = jnp.square(jnp.maximum(u, 0.0)).astype(BF16)
    for n in range(acc_ref.shape[1] // OUT_CHUNK):
        cs = slice(n * OUT_CHUNK, (n + 1) * OUT_CHUNK)
        acc_ref[:, cs] += jnp.dot(u, wd_ref[:, cs], preferred_element_type=F32)

    @pl.when(f == pl.num_programs(1) - 1)
    def _():
        o_ref[...] = x_ref[...] + gt_ref[0] * acc_ref[...]


def _mlp(h2, w_up, w_down, x1, mod6, T):
    BT, D = x1.shape
    F = w_up.shape[1]
    tm = min(512, T)
    tf = 1024
    tpb = T // tm
    vm = 2 * (tm * D * 2 + 2 * D * tf * 2 + 2 * tm * D * 4) + tm * D * 4 + 2 * tm * tf * 4
    return pl.pallas_call(
        _mlp_kernel,
        grid=(BT // tm, F // tf),
        in_specs=[
            pl.BlockSpec((tm, D), lambda i, f: (i, 0)),
            pl.BlockSpec((D, tf), lambda i, f: (0, f)),
            pl.BlockSpec((tf, D), lambda i, f: (f, 0)),
            pl.BlockSpec((tm, D), lambda i, f: (i, 0)),
            pl.BlockSpec((1, 1, D), lambda i, f: ((i // tpb) * 6 + 5, 0, 0)),
        ],
        out_specs=pl.BlockSpec((tm, D), lambda i, f: (i, 0)),
        out_shape=jax.ShapeDtypeStruct((BT, D), F32),
        scratch_shapes=[pltpu.VMEM((tm, D), F32)],
        compiler_params=_cparams(("parallel", "arbitrary"), vm),
        name="mlp",
    )(h2, w_up, w_down, x1, mod6)


def _dotb(a, b):
    return jnp.dot(a.astype(BF16), b.astype(BF16), preferred_element_type=F32)


def _dotb_nt(a, b):
    return lax.dot_general(a.astype(BF16), b.astype(BF16), (((1,), (1,)), ((), ())),
                           preferred_element_type=F32)


def _split_bf16(x, terms):
    parts, rem = [], x
    for t in range(terms):
        p = rem.astype(BF16)
        parts.append(p)
        if t + 1 < terms:
            rem = rem - p.astype(F32)
    return parts


def _dot_sel(x, sel, terms):
    sel = sel.astype(BF16)
    return sum(jnp.dot(p, sel, preferred_element_type=F32) for p in _split_bf16(x, terms))


def _sel_dot(sel, x, terms):
    sel = sel.astype(BF16)
    return sum(jnp.dot(sel, p, preferred_element_type=F32) for p in _split_bf16(x, terms))


def _dot3(a, b_hi, b_lo):
    a_hi, a_lo = _split_bf16(a, 2)
    return (jnp.dot(a_hi, b_hi, preferred_element_type=F32) + jnp.dot(a_lo, b_hi, preferred_element_type=F32)
            + jnp.dot(a_hi, b_lo, preferred_element_type=F32))


def _iota(shape, axis):
    return lax.broadcasted_iota(jnp.int32, shape, axis)


def _same_head_mask(n):
    return (_iota((n, n), 0) >> 6) == (_iota((n, n), 1) >> 6)


RW_TC = 256
RW_C = 64
RW_LANES = 4 * HEAD_DIM


def _block_diag(x, bdmask):
    return jnp.where(bdmask, jnp.concatenate([x, x, x, x], axis=0), 0.0)


def _rwkv_kernel(r_ref, k_ref, v_ref, s_ref, mur_ref, muk_ref, muv_ref, mus_ref,
                 w0_ref, a0_ref, kk_ref, ka_ref, rk_ref, lnw_ref, lnb_ref,
                 w2h_ref, w2l_ref, a2_ref, g2_ref, o_ref,
                 pr_scr, pk_scr, pv_scr, ps_scr, state_scr):
    TC = r_ref.shape[0]
    C = RW_C
    W = RW_LANES

    @pl.when(pl.program_id(1) == 0)
    def _():
        pr_scr[...] = jnp.zeros_like(pr_scr)
        pk_scr[...] = jnp.zeros_like(pk_scr)
        pv_scr[...] = jnp.zeros_like(pv_scr)
        ps_scr[...] = jnp.zeros_like(ps_scr)
        state_scr[...] = jnp.zeros_like(state_scr)

    def shift_mix(p_ref, prev_scr, mu_ref):
        p = p_ref[...].astype(F32)
        rolled = pltpu.roll(p, 1, 0)
        first = jnp.where(_iota((8, p.shape[1]), 0) == 0, prev_scr[0:1, :], rolled[0:8])
        shifted = jnp.concatenate([first, rolled[8:]], axis=0)
        prev_scr[0:1, :] = p[TC - 1:TC, :]
        return p + (shifted - p) * mu_ref[...]

    CW = r_ref.shape[1]
    G = CW // W
    NQ = TC // C
    groups = lambda x: [x[:, g * W:(g + 1) * W] for g in range(G)]
    per_group = lambda f, x: jnp.concatenate([f(xg) for xg in groups(x)], axis=1)

    r = shift_mix(r_ref, pr_scr, mur_ref)
    k = shift_mix(k_ref, pk_scr, muk_ref)
    v = shift_mix(v_ref, pv_scr, muv_ref)
    sm = shift_mix(s_ref, ps_scr, mus_ref)
    xwa = sm[:, 0:SM_XG]

    bdmask = _same_head_mask(W)
    bones = jnp.where(bdmask, 1.0, 0.0)
    head_sum = lambda x: per_group(lambda xg: _dot_sel(xg, bones, 1), x)
    tri = jnp.where(_same_head_mask(TC) & (_iota((TC, TC), 1) <= _iota((TC, TC), 0)), 1.0, 0.0)
    lane = _iota((C, W), 1)
    row = _iota((C, W), 0)
    lanehead = lane >> 6
    strict = (lane & 63) < row
    incl = (lane & 63) <= row
    eye_cat = jnp.where((lane & 63) == row, 1.0, 0.0)

    wlin = w0_ref[...] + _dot3(jnp.tanh(xwa), w2h_ref[...], w2l_ref[...])
    a = jax.nn.sigmoid(a0_ref[...] + _dotb(xwa, a2_ref[...]))
    gate = _dotb(jax.nn.sigmoid(sm[:, SM_XG:SM_XG + 256]), g2_ref[...])
    z = -wlin
    softplus = jnp.maximum(z, 0.0) + jnp.log(1.0 + jnp.exp(-jnp.abs(z)))
    ld = -jnp.exp(-softplus - 0.5)
    cum = _sel_dot(tri, ld, 3)
    cum_last = jnp.concatenate(
        [jnp.broadcast_to(cum[(q + 1) * C - 1:(q + 1) * C, :], (C, CW)) for q in range(NQ)], axis=0)
    kk = k * kk_ref[...]
    kk = kk * lax.rsqrt(jnp.maximum(head_sum(kk * kk), 1e-24))
    k2 = k * (1.0 + (a - 1.0) * ka_ref[...])
    bvec = kk * a
    e_inv = jnp.exp(-cum)
    e_end = jnp.exp(cum_last - cum)
    At_f = -kk * jnp.exp(cum - ld)
    Rt_f = r * jnp.exp(cum)
    Bt_f = bvec * e_inv
    Kt_f = k2 * e_inv
    Bg_f = bvec * e_end
    Kg_f = k2 * e_end
    g_end = jnp.exp(cum_last)

    items = [(g, q) for q in range(NQ) for g in range(G)]
    blk = lambda x, g, q: x[q * C:(q + 1) * C, g * W:(g + 1) * W]
    bd01 = jnp.where(bdmask, 1.0, 0.0).astype(BF16)

    def bd(x):
        xb = x.astype(BF16)
        return jnp.concatenate([xb, xb, xb, xb], axis=0) * bd01

    At = {it: blk(At_f, *it) for it in items}
    Rt = {it: blk(Rt_f, *it) for it in items}
    Vq = {it: blk(v, *it) for it in items}
    AA = {}
    for it in items:
        bk = jnp.concatenate([jnp.where(lanehead == h, X, 0.0)
                              for X in (blk(Bt_f, *it), blk(Kt_f, *it)) for h in range(4)], axis=0)
        AA[it] = _dotb_nt(jnp.concatenate([At[it], Rt[it]], axis=0), bk)
    A_ab = {it: jnp.where(strict, AA[it][0:C, 0:W], 0.0) for it in items}
    A_ak = {it: jnp.where(strict, AA[it][0:C, W:2 * W], 0.0) for it in items}
    A_rb = {it: jnp.where(incl, AA[it][C:2 * C, 0:W], 0.0) for it in items}
    A_rk = {it: jnp.where(incl, AA[it][C:2 * C, W:2 * W], 0.0) for it in items}
    M = dict(A_ab)
    Tm = {it: eye_cat + A_ab[it] for it in items}
    for _ in range(5):
        M = {it: _dotb(M[it], bd(M[it])) for it in items}
        Tm = {it: Tm[it] + _dotb(M[it], bd(Tm[it])) for it in items}
    Vbd = {it: bd(Vq[it]) for it in items}
    akv = {it: _dotb(A_ak[it], Vbd[it]) for it in items}
    rkv = {it: _dotb(A_rk[it], Vbd[it]) for it in items}

    S = [state_scr[g] for g in range(G)]
    ys = []
    for q in range(NQ):
        its = [(g, q) for g in range(G)]
        rhs = [_dotb_nt(At[it], S[it[0]]) + akv[it] for it in its]
        U = [_dotb(Tm[it], bd(rhs[g])) for g, it in enumerate(its)]
        ys.append(jnp.concatenate(
            [_dotb_nt(Rt[it], S[g]) + _dotb(A_rb[it], bd(U[g])) + rkv[it] for g, it in enumerate(its)], axis=1))
        upd = [_dotb(jnp.concatenate([U[g], Vq[it]], axis=0).T,
                     jnp.concatenate([blk(Bg_f, *it), blk(Kg_f, *it)], axis=0)) for g, it in enumerate(its)]
        S = [S[g] * blk(g_end, g, q)[0:1, :] + jnp.where(bdmask, upd[g], 0.0) for g in range(G)]
    for g in range(G):
        state_scr[g] = S[g]
    y = jnp.concatenate(ys, axis=0)

    inv_n = 1.0 / HEAD_DIM
    d = y - head_sum(y) * inv_n
    var = head_sum(d * d) * inv_n
    yn = d * lax.rsqrt(var + RWKV_GN_EPS) * lnw_ref[...] + lnb_ref[...]
    bonus = head_sum(r * k2 * rk_ref[...]) * v
    o_ref[...] = ((yn + bonus) * gate).astype(BF16)


def _rwkv_mix(proj, small, pr, B, T):
    BT = proj.shape[0]
    TC = min(RW_TC, T)
    W = RW_LANES
    CW = RWKV_WIDTH
    nct = T // TC
    row = lambda b, c: b * nct + c
    full = lambda shape: pl.BlockSpec(shape, lambda b, c: (0, 0))
    vec = full((1, CW))
    in_specs = [
        pl.BlockSpec((TC, CW), lambda b, c: (row(b, c), 0)),
        pl.BlockSpec((TC, CW), lambda b, c: (row(b, c), 1)),
        pl.BlockSpec((TC, CW), lambda b, c: (row(b, c), 2)),
        pl.BlockSpec((TC, SMALL_W), lambda b, c: (row(b, c), 0)),
        vec, vec, vec,
        full((1, SMALL_W)),
        vec, vec, vec, vec, vec, vec, vec,
        full((SM_XG, CW)), full((SM_XG, CW)), full((SM_XG, CW)), full((256, CW)),
    ]
    vm = (2 * (3 * TC * CW * 4 + TC * SMALL_W * 4 + TC * CW * 2 + (3 * SM_XG + 256) * CW * 2)
          + 24 * TC * CW * 4)
    return pl.pallas_call(
        _rwkv_kernel,
        grid=(B, nct),
        in_specs=in_specs,
        out_specs=pl.BlockSpec((TC, CW), lambda b, c: (row(b, c), 0)),
        out_shape=jax.ShapeDtypeStruct((BT, CW), BF16),
        scratch_shapes=[pltpu.VMEM((8, CW), F32), pltpu.VMEM((8, CW), F32), pltpu.VMEM((8, CW), F32),
                        pltpu.VMEM((8, SMALL_W), F32), pltpu.VMEM((CW // W, W, W), F32)],
        compiler_params=_cparams(("parallel", "arbitrary"), vm),
        name="rwkv_mix",
    )(proj, proj, proj, small, pr["mu_r"], pr["mu_k"], pr["mu_v"], pr["mu_s"],
      pr["w0"], pr["a0"], pr["k_k"], pr["k_a"], pr["r_k"], pr["ln_w"], pr["ln_b"],
      pr["w2h"], pr["w2l"], pr["a2"], pr["g2"])


NSA_TT = 256
BLOCK_LANE0 = HEAD_DIM
PADFLAG_LANE = BLOCK_LANE0 + 32


def _nsa_prep_kernel(q_ref, kv0_ref, kv1_ref, kv2_ref, s_ref, qg_ref, kgs_ref, kgw_ref, e_ref, gsel_ref,
                     qp_ref, ksl_ref, vsl_ref, kwl_ref, vwl_ref, gt_ref):
    tt = q_ref.shape[0]
    QW = NSA_GROUP * HEAD_DIM
    kv_refs = (kv0_ref, kv1_ref, kv2_ref)
    bones = jnp.where(_same_head_mask(QW), 1.0, 0.0)
    bones2 = jnp.where(_same_head_mask(LANES), 1.0, 0.0)
    lane = _iota((tt, LANES), 1)
    block_id = (pl.program_id(1) * tt + _iota((tt, LANES), 0)) >> 6
    onehot = jnp.where(lane == block_id + BLOCK_LANE0, 1.0, 0.0)
    small = _split_bf16(s_ref[...], 3)

    def pair(g, branch):
        off = (3 * g + branch) * LANES
        return kv_refs[off // SMALL_W][:, off % SMALL_W:off % SMALL_W + LANES].astype(F32)

    def slabs(g, x, gain, k_ref, v_ref):
        ms = _dot_sel(x * x, bones2, 1) * (1.0 / HEAD_DIM)
        k_ref[0, g] = jnp.where(lane < HEAD_DIM, x * lax.rsqrt(ms + NORM_EPS) * gain, onehot).astype(BF16)
        v_ref[0, g] = jnp.where(lane >= HEAD_DIM, x, 1.0).astype(BF16)

    for g in range(NSA_KV_HEADS):
        q = q_ref[:, g * QW:(g + 1) * QW].astype(F32)
        ms = _dot_sel(q * q, bones, 1) * (1.0 / HEAD_DIM)
        qn = (q * lax.rsqrt(ms + NORM_EPS) * qg_ref[...]) * (HEAD_DIM ** -0.5)
        qs = jnp.dot(qn.astype(BF16), e_ref[...], preferred_element_type=F32).astype(BF16)
        for h in range(NSA_GROUP):
            qp_ref[0, NSA_GROUP * g + h] = qs[:, h * LANES:(h + 1) * LANES]
        slabs(g, pair(g, 1), kgs_ref[...], ksl_ref, vsl_ref)
        slabs(g, pair(g, 2), kgw_ref[...], kwl_ref, vwl_ref)
        sel = gsel_ref[g].astype(BF16)
        gt_ref[0, g] = jax.nn.sigmoid(sum(jnp.dot(part, sel, preferred_element_type=F32) for part in small))


def _nsa_prep(proj, small, pn, B, T):
    tt = min(NSA_TT, T)
    ntt = T // tt
    G = NSA_KV_HEADS
    QW = NSA_GROUP * HEAD_DIM
    row = lambda b, t: b * ntt + t
    full = lambda shape: pl.BlockSpec(shape, lambda b, t: tuple(0 for _ in shape))
    kv_spec = lambda j: pl.BlockSpec((tt, SMALL_W), lambda b, t: (row(b, t), COL_KV // SMALL_W + j))
    vm = 2 * (tt * NSA_WIDTH * 4 + 4 * tt * SMALL_W * 4 + QW * 4 * LANES * 2 + G * SMALL_W * LANES * 4
              + 16 * tt * LANES * 2 + 16 * tt * LANES * 2 + 4 * tt * LANES * 4) + 24 * tt * QW * 4
    return pl.pallas_call(
        _nsa_prep_kernel,
        grid=(B, ntt),
        in_specs=[
            pl.BlockSpec((tt, NSA_WIDTH), lambda b, t: (row(b, t), COL_Q // NSA_WIDTH)),
            kv_spec(0), kv_spec(1), kv_spec(2),
            pl.BlockSpec((tt, SMALL_W), lambda b, t: (row(b, t), 0)),
            full((1, QW)), full((1, LANES)), full((1, LANES)), full((QW, NSA_GROUP * LANES)),
            full((G, SMALL_W, LANES)),
        ],
        out_specs=[pl.BlockSpec((1, NSA_HEADS, tt, LANES), lambda b, t: (b, 0, t, 0))]
        + [pl.BlockSpec((1, G, tt, LANES), lambda b, t: (b, 0, t, 0))] * 5,
        out_shape=[jax.ShapeDtypeStruct((B, NSA_HEADS, T, LANES), BF16)]
        + [jax.ShapeDtypeStruct((B, G, T, LANES), BF16)] * 4
        + [jax.ShapeDtypeStruct((B, G, T, LANES), F32)],
        compiler_params=_cparams(("parallel", "parallel"), vm),
        name="nsa_prep",
    )(proj, proj, proj, proj, small, pn["q_g"], pn["kg_sel"], pn["kg_win"], pn["q_spread"], pn["gate_sel"])


def _gelu_tanh(x):
    return 0.5 * x * (1.0 + jnp.tanh(math.sqrt(2.0 / math.pi) * (x + 0.044715 * (x * x * x))))


def _nsa_compress_kernel(x_ref, pea_ref, peb_ref, wa_ref, wb_ref, w2_ref, kg_ref, kc_ref, vc_ref, x_scr):
    nsub = x_ref.shape[0] // CMP_STRIDE
    x_scr[...] = x_ref[...].astype(F32)
    xs = jnp.concatenate([x_scr[pl.ds(s, nsub, stride=CMP_STRIDE), :] for s in range(CMP_STRIDE)], axis=1)
    p0 = jnp.dot((xs + pea_ref[...]).astype(BF16), wa_ref[...], preferred_element_type=F32)
    p1 = jnp.dot((xs + peb_ref[...]).astype(BF16), wb_ref[...], preferred_element_type=F32)
    hid = _gelu_tanh(p0 + pltpu.roll(p1, nsub - 1, 0))
    out = jnp.dot(hid.astype(BF16), w2_ref[...], preferred_element_type=F32)
    bones2 = jnp.where(_same_head_mask(LANES), 1.0, 0.0)
    ms = _dot_sel(out * out, bones2, 2) * (1.0 / HEAD_DIM)
    is_k = _iota(out.shape, 1) < HEAD_DIM
    kc_ref[0, 0] = jnp.where(is_k, out * lax.rsqrt(ms + NORM_EPS) * kg_ref[...], 0.0).astype(BF16)
    vc_ref[0, 0] = jnp.where(is_k, 0.0, out).astype(BF16)


def _nsa_compress(proj, pn, B, T):
    G = NSA_KV_HEADS
    nsub = T // CMP_STRIDE
    kv_blk = COL_KV // LANES
    KW = CMP_STRIDE * LANES
    full = lambda shape: pl.BlockSpec(shape, lambda b, g: tuple(0 for _ in shape))
    vm = 2 * (T * LANES * 4 + 2 * KW * LANES * 2 + nsub * LANES * 2) + 6 * nsub * KW * 4
    return pl.pallas_call(
        _nsa_compress_kernel,
        grid=(B, G),
        in_specs=[
            pl.BlockSpec((T, LANES), lambda b, g: (b, kv_blk + 3 * g)),
            full((1, KW)), full((1, KW)), full((KW, LANES)), full((KW, LANES)), full((LANES, LANES)),
            full((1, LANES)),
        ],
        out_specs=[pl.BlockSpec((1, 1, nsub, LANES), lambda b, g: (b, g, 0, 0))] * 2,
        out_shape=[jax.ShapeDtypeStruct((B, G, nsub, LANES), BF16)] * 2,
        scratch_shapes=[pltpu.VMEM((T, LANES), F32)],
        compiler_params=_cparams(("parallel", "parallel"), vm),
        name="nsa_compress",
    )(proj, pn["pe_a"], pn["pe_b"], pn["cw_a"], pn["cw_b"], pn["cw2"], pn["kg_cmp"])


NSA_GROUPS_PER_STEP = 4
NSA_KPAD = WINDOW
SEL_CHUNK = 4 * QUERY_BLOCK
WIN_KEYS = WINDOW + QUERY_BLOCK


def _lane_tile_max(s):
    tiles = [s[:, j * LANES:(j + 1) * LANES] for j in range(s.shape[1] // LANES)]
    while len(tiles) > 1:
        tiles = [jnp.maximum(a, b) for a, b in zip(tiles[0::2], tiles[1::2])] + ([tiles[-1]] if len(tiles) % 2 else [])
    return tiles[0]


def _nsa_attn_kernel(q_ref, kc_ref, vc_ref, ks_ref, vs_ref, kw_ref, vw_ref, gt_ref, stab_ref, wtab_ref, ctab_ref,
                     selmt_ref, gather_ref, grep_ref, o_ref, sbuf, mx_scr, acc_scr, sc_scr, *, n_sel):
    i = pl.program_id(2)
    QB = QUERY_BLOCK
    HG = NSA_GROUP
    R = HG * QB
    t0 = i * QB
    groups = range(kc_ref.shape[1])
    each = lambda f: [f(g) for g in groups]
    tile4 = lambda z: jnp.concatenate([z, z, z, z], axis=0)
    heads = lambda ref, g: ref[HG * g:HG * (g + 1)]
    lane = _iota((QB, LANES), 1)
    pad_mask = jnp.where(lane == PADFLAG_LANE, NEG_INF, 0.0)
    q = each(lambda g: q_ref[0, HG * g:HG * (g + 1)].reshape(R, LANES))
    q32 = each(lambda g: q[g].astype(F32))

    ncp = kc_ref.shape[2]
    s = each(lambda g: _dotb_nt(q[g], kc_ref[0, g]) + heads(ctab_ref, g)[:, 0].reshape(R, ncp))
    m = each(lambda g: jnp.max(s[g], axis=-1, keepdims=True))
    p_c = each(lambda g: jnp.exp(s[g] - m[g]))
    lsum = each(lambda g: jnp.sum(p_c[g], axis=-1, keepdims=True))
    p_c = each(lambda g: p_c[g] * jnp.where(m[g] > 0.5 * NEG_INF, 1.0 / lsum[g], 0.0))
    o_c = each(lambda g: jnp.dot(p_c[g].astype(BF16), vc_ref[0, g], preferred_element_type=F32))

    wrows = pl.ds(pl.multiple_of(t0, QB), WIN_KEYS)
    q_win = each(lambda g: (q32[g] + tile4(pad_mask)).astype(BF16))
    s = each(lambda g: _dotb_nt(q_win[g], kw_ref[0, g, wrows, :]) + heads(wtab_ref, g).reshape(R, WIN_KEYS))
    p = each(lambda g: jnp.exp((s[g] - jnp.max(_lane_tile_max(s[g]), axis=-1, keepdims=True)).astype(BF16)))
    acc_w = each(lambda g: jnp.dot(p[g], vw_ref[0, g, wrows, :], preferred_element_type=F32))

    psum = each(lambda g: p_c[g][0:QB] + p_c[g][QB:2 * QB] + p_c[g][2 * QB:3 * QB] + p_c[g][3 * QB:4 * QB])
    selmt = selmt_ref[...]
    parts = each(lambda g: _split_bf16(psum[g], 3))
    imp = each(lambda g: sum(lax.dot_general(selmt, part, (((1,), (1,)), ((), ())), preferred_element_type=F32)
                             for part in parts[g]))
    ns = selmt.shape[0]
    blk = _iota((ns, QB), 0)
    cur = (t0 + _iota((ns, QB), 1)) >> 6
    forced = (blk == 0) | (blk == cur) | (blk == cur - 1)
    score = each(lambda g: jnp.where(forced, FORCE_SCORE, jnp.where(blk <= cur, imp[g], -1.0)))
    for g in groups:
        sc_scr[g] = score[g]
    ranks = [[] for _ in groups]
    for j in range(ns):
        lower = jnp.where(blk > j, 1.0, 0.0)
        for g in groups:
            other = sc_scr[g, j:j + 1, :]
            ranks[g].append(jnp.where(other > score[g], 1.0, 0.0) + jnp.where(other == score[g], lower, 0.0))
    while len(ranks[0]) > 1:
        ranks = [[a + b for a, b in zip(r[0::2], r[1::2])] + ([r[-1]] if len(r) % 2 else []) for r in ranks]
    chosen_t = each(lambda g: jnp.where(ranks[g][0] < n_sel, 1.0, 0.0))
    zrows = lambda n: jnp.zeros((n, QB), F32)
    chosen = each(lambda g: jnp.concatenate([zrows(BLOCK_LANE0), chosen_t[g], zrows(LANES - BLOCK_LANE0 - ns)],
                                            axis=0).T)
    is_block_lane = (lane >= BLOCK_LANE0) & (lane < BLOCK_LANE0 + ns)
    q_sel = each(lambda g: (q32[g] + tile4(jnp.where(is_block_lane, (chosen[g] - 1.0) * (-NEG_INF), pad_mask))
                            ).astype(BF16))

    def chunk_rows(c):
        return pl.ds(pl.multiple_of((i - 4 * c + 1) * QB, QB), SEL_CHUNK)

    s0 = each(lambda g: _dotb_nt(q_sel[g], ks_ref[0, g, chunk_rows(0), :]) + heads(stab_ref, g).reshape(R, SEL_CHUNK))
    for g in groups:
        sbuf[g, 0] = s0[g]
        mx_scr[g] = _lane_tile_max(s0[g])

    def scores_body(c, carry):
        s = each(lambda g: _dotb_nt(q_sel[g], ks_ref[0, g, chunk_rows(c), :]))
        for g in groups:
            sbuf[g, c] = s[g]
            mx_scr[g] = jnp.maximum(mx_scr[g], _lane_tile_max(s[g]))
        return carry

    n_far = i // 4
    lax.fori_loop(1, n_far + 1, scores_body, 0)
    m_s = each(lambda g: jnp.max(mx_scr[g], axis=-1, keepdims=True))

    acc_scr[...] = jnp.zeros_like(acc_scr)

    def values_body(c, carry):
        p = each(lambda g: jnp.exp((sbuf[g, c] - m_s[g]).astype(BF16)))
        for g in groups:
            acc_scr[g] += jnp.dot(p[g], vs_ref[0, g, chunk_rows(c), :], preferred_element_type=F32)
        return carry

    lax.fori_loop(0, n_far + 1, values_body, 0)

    is_value = _iota((R, LANES), 1) >= HEAD_DIM
    normalised = lambda acc: jnp.where(is_value, acc * (1.0 / pltpu.roll(acc, HEAD_DIM, 1)), 0.0)

    def natural(o):
        cat = jnp.concatenate([o[h * QB:(h + 1) * QB] for h in range(HG)], axis=1).astype(BF16)
        return jnp.dot(cat, gather_ref[...], preferred_element_type=F32)

    branches = each(lambda g: [o_c[g], normalised(acc_scr[g]), normalised(acc_w[g])])
    gate = each(lambda g: [_dot_sel(gt_ref[0, g], grep_ref[c], 1) for c in range(3)])
    W = HG * HEAD_DIM
    for g in groups:
        out = sum(gate[g][c] * natural(branches[g][c]) for c in range(3))
        o_ref[:, g * W:(g + 1) * W] = out.astype(BF16)


def _nsa_attn(qp, kc, vc, ks, vs, kw, vw, gates, pn, B, T):
    G = NSA_KV_HEADS
    QB = QUERY_BLOCK
    NQ = T // QB
    ncp = kc.shape[2]
    ns = T // SEL_BLOCK
    TP = T + NSA_KPAD
    R = NSA_GROUP * QB
    n_sel = min(N_SEL, ns)
    n_chunks = (NQ - 1) // 4 + 1
    NG = NSA_GROUPS_PER_STEP
    HB = NSA_GROUP * NG
    once = pl.Buffered(1)
    gtab = lambda w: pl.BlockSpec((HB, QB, w), lambda b, g, i: (g, 0, 0), pipeline_mode=once)
    slab = lambda rows: pl.BlockSpec((1, NG, rows, LANES), lambda b, g, i: (b, g, 0, 0), pipeline_mode=once)
    vm = (NG * (2 * ncp * LANES * 2 + 4 * TP * LANES * 2 + R * (SEL_CHUNK + WIN_KEYS) * 4)
          + 2 * NG * (R * LANES * 2 + QB * LANES * 4 + R * ncp * 4 + QB * NSA_GROUP * HEAD_DIM * 2)
          + NG * (n_chunks * R * SEL_CHUNK * 4 + 2 * R * LANES * 4 + 4 * R * WIN_KEYS * 4))
    return pl.pallas_call(
        functools.partial(_nsa_attn_kernel, n_sel=n_sel),
        grid=(B, G // NG, NQ),
        in_specs=[
            pl.BlockSpec((1, HB, QB, LANES), lambda b, g, i: (b, g, i, 0)),
            slab(ncp), slab(ncp), slab(TP), slab(TP), slab(TP), slab(TP),
            pl.BlockSpec((1, NG, QB, LANES), lambda b, g, i: (b, g, i, 0)),
            gtab(SEL_CHUNK), gtab(WIN_KEYS),
            pl.BlockSpec((HB, 1, QB, ncp), lambda b, g, i: (g, i, 0, 0)),
            pl.BlockSpec((ns, ncp), lambda b, g, i: (0, 0)),
            pl.BlockSpec((NSA_GROUP * LANES, NSA_GROUP * HEAD_DIM), lambda b, g, i: (0, 0)),
            pl.BlockSpec((3, LANES, NSA_GROUP * HEAD_DIM), lambda b, g, i: (0, 0, 0)),
        ],
        out_specs=pl.BlockSpec((QB, HB * HEAD_DIM), lambda b, g, i: (b * NQ + i, g)),
        out_shape=jax.ShapeDtypeStruct((B * T, NSA_WIDTH), BF16),
        scratch_shapes=[pltpu.VMEM((NG, n_chunks, R, SEL_CHUNK), F32), pltpu.VMEM((NG, R, LANES), F32),
                        pltpu.VMEM((NG, R, LANES), F32), pltpu.VMEM((NG, ns, QB), F32)],
        compiler_params=_cparams(("parallel", "parallel", "arbitrary"), vm),
        name="nsa_attn",
    )(qp, kc, vc, ks, vs, kw, vw, gates, pn["stab"], pn["wtab"], pn["ctab"], pn["sel_mt"], pn["gather"],
      pn["gate_rep"])


def _rel_bucket_table():
    n = np.arange(REL_MAX_DIST + 1)
    max_exact = REL_BUCKETS // 2
    nf = np.maximum(n, max_exact).astype(np.float32)
    large = max_exact + (np.log(nf / np.float32(max_exact)) / np.float32(math.log(REL_MAX_DIST / max_exact))
                         * np.float32(REL_BUCKETS - max_exact)).astype(np.int32)
    large = np.minimum(large, REL_BUCKETS - 1)
    return np.where(n < max_exact, n, large).astype(np.int32)


def _sel_to_cmp_matrix(T, ncp):
    nc = T // CMP_STRIDE - CMP_BLOCK // CMP_STRIDE + 1
    ns = T // SEL_BLOCK
    cs = np.arange(nc) * CMP_STRIDE
    ss = np.arange(ns) * SEL_BLOCK
    lo = np.maximum(cs[None, :], ss[:, None])
    hi = np.minimum(cs[None, :] + CMP_BLOCK, ss[:, None] + SEL_BLOCK)
    out = np.zeros((ns, ncp), np.float32)
    out[:, :nc] = np.maximum(hi - lo, 0) / CMP_BLOCK
    return out


def _bias_tables(rel_bias, rel):
    bucket = _rel_bucket_table()[np.clip(rel, 0, REL_MAX_DIST)]
    onehot = (jnp.asarray(bucket.reshape(1, -1)) == jnp.arange(REL_BUCKETS, dtype=jnp.int32)[:, None]).astype(F32)
    tab = jnp.einsum('bh,bn->hn', rel_bias, onehot, precision=HIGHEST)
    return tab.reshape((rel_bias.shape[1],) + rel.shape)


def _prep_in_proj_weight(w_in_all, l):
    D = w_in_all.shape[1]
    nsa0 = RWKV_COLS
    kv0 = nsa0 + NSA_WIDTH
    gates0 = kv0 + 6 * NSA_KV_WIDTH
    merge0 = RWKV_COLS + NSA_COLS
    wt = jnp.swapaxes(w_in_all[l], 0, 1)
    kv = wt[kv0:gates0].reshape(3, 2, NSA_KV_HEADS, HEAD_DIM, D)
    kv = jnp.transpose(kv, (2, 0, 1, 3, 4)).reshape(6 * NSA_KV_WIDTH, D)
    pad = jnp.zeros((SMALL_W - (SM_GATES + 3 * NSA_HEADS), D), wt.dtype)
    return jnp.concatenate([
        wt[0:3 * RWKV_WIDTH],
        wt[nsa0:kv0],
        wt[merge0:merge0 + 2 * D_MODEL],
        kv,
        wt[3 * RWKV_WIDTH:RWKV_COLS],
        wt[gates0:merge0],
        pad,
    ], axis=0).astype(BF16)


def _prep_rwkv_params(mu, w0, w2, a0, a2, g2, k_k, k_a, r_k, ln_w, ln_b):
    C = RWKV_WIDTH
    row = lambda z: z.reshape(1, -1).astype(F32)
    mu_s = jnp.concatenate([mu[3 * C:], jnp.zeros((SMALL_W - (RWKV_COLS - 3 * C),), F32)]).reshape(1, SMALL_W)
    zl = jnp.zeros((DECAY_LORA, C), F32)

    w2p = jnp.concatenate([w2, zl], axis=0)
    w2h = w2p.astype(BF16)
    return dict(
        mu_r=row(mu[0:C]), mu_k=row(mu[C:2 * C]), mu_v=row(mu[2 * C:3 * C]), mu_s=mu_s,
        w0=row(w0), a0=row(a0), k_k=row(k_k), k_a=row(k_a), r_k=row(r_k), ln_w=row(ln_w), ln_b=row(ln_b),
        w2h=w2h, w2l=(w2p - w2h.astype(F32)).astype(BF16),
        a2=jnp.concatenate([zl, a2], axis=0).astype(BF16),
        g2=jnp.concatenate([g2, jnp.zeros((256 - GATE_LORA, C), F32)], axis=0).astype(BF16),
    )


def _prep_nsa_params(pe_k, w1_k, w2_k, pe_v, w1_v, w2_v, q_g, k_g, rel_bias, T):
    hd = HEAD_DIM
    ones = jnp.ones((hd,), F32)
    ncp = T // CMP_STRIDE
    NQ = T // QUERY_BLOCK

    def blockdiag(a, b):
        lead = ((0, 0),) * (a.ndim - 2)
        return jnp.pad(a, lead + ((0, hd), (0, hd))) + jnp.pad(b, lead + ((hd, 0), (hd, 0)))

    w1 = blockdiag(w1_k.reshape(CMP_BLOCK, hd, hd), w1_v.reshape(CMP_BLOCK, hd, hd))
    pe = jnp.concatenate([pe_k, pe_v], axis=1)
    half = CMP_STRIDE

    spread = np.zeros((NSA_GROUP * hd, NSA_GROUP * LANES), np.float32)
    gather = np.zeros((NSA_GROUP * LANES, NSA_GROUP * hd), np.float32)
    for h in range(NSA_GROUP):
        for d in range(hd):
            spread[h * hd + d, h * LANES + d] = 1.0
            gather[h * LANES + hd + d, h * hd + d] = 1.0
    gate_sel = np.zeros((NSA_KV_HEADS, SMALL_W, LANES), np.float32)
    for g in range(NSA_KV_HEADS):
        for j in range(3 * NSA_GROUP):
            gate_sel[g, SM_GATES + 3 * NSA_GROUP * g + j, j] = 1.0
    gate_rep = np.zeros((3, LANES, NSA_GROUP * hd), np.float32)
    for c in range(3):
        for h in range(NSA_GROUP):
            gate_rep[c, 3 * h + c, h * hd:(h + 1) * hd] = 1.0

    qi = np.arange(QUERY_BLOCK)[:, None]
    rel_s = qi + (SEL_CHUNK - QUERY_BLOCK) - np.arange(SEL_CHUNK)[None, :]
    rel_w = qi + WINDOW - np.arange(WIN_KEYS)[None, :]
    rel_c = ((np.arange(NQ)[:, None, None] * QUERY_BLOCK + qi[None])
             - (np.arange(ncp)[None, None, :] * CMP_STRIDE + CMP_BLOCK - 1))
    far = np.full((QUERY_BLOCK, 1), REL_MAX_DIST)
    tabs = _bias_tables(rel_bias, np.concatenate([rel_s, rel_w, far], axis=1))
    o1, o2 = SEL_CHUNK, SEL_CHUNK + WIN_KEYS
    masked = lambda tab, ok: jnp.where(jnp.asarray(ok)[None], tab, NEG_INF)
    stab = masked(tabs[:, :, 0:o1] - tabs[:, :, o2:], rel_s >= 0)
    wtab = masked(tabs[:, :, o1:o2], (rel_w >= 0) & (rel_w < WINDOW))
    ctab = masked(_bias_tables(rel_bias, rel_c), rel_c >= 0)
    return dict(
        stab=stab, wtab=wtab, ctab=ctab,
        q_g=jnp.tile(q_g, NSA_GROUP).reshape(1, -1),
        kg_cmp=jnp.concatenate([k_g[0], ones]).reshape(1, LANES),
        kg_sel=jnp.concatenate([k_g[1], ones]).reshape(1, LANES),
        kg_win=jnp.concatenate([k_g[2], ones]).reshape(1, LANES),
        pe_a=pe[:half].reshape(1, half * LANES), pe_b=pe[half:].reshape(1, half * LANES),
        cw_a=w1[:half].reshape(half * LANES, LANES).astype(BF16),
        cw_b=w1[half:].reshape(half * LANES, LANES).astype(BF16),
        cw2=blockdiag(w2_k, w2_v).astype(BF16),
        q_spread=jnp.asarray(spread, BF16), gather=jnp.asarray(gather, BF16),
        gate_sel=jnp.asarray(gate_sel), gate_rep=jnp.asarray(gate_rep, BF16),
        sel_mt=jnp.asarray(_sel_to_cmp_matrix(T, ncp)),
    )


def _front_pad(slab, flagged):
    B, G, _, L = slab.shape
    row = jnp.zeros((L,), slab.dtype)
    if flagged:
        row = row.at[PADFLAG_LANE].set(1.0)
    return jnp.concatenate([jnp.broadcast_to(row, (B, G, NSA_KPAD, L)), slab], axis=2)


def kernel(x, c, w_ada, b_ada, norm1_g, norm2_g, w_in, rwkv_mu, rwkv_w0, rwkv_w2, rwkv_a0, rwkv_a2, rwkv_g2, rwkv_k_k, rwkv_k_a, rwkv_r_k, rwkv_ln_w, rwkv_ln_b, cmp_pe_k, cmp_w1_k, cmp_w2_k, cmp_pe_v, cmp_w1_v, cmp_w2_v, q_norm_g, k_norm_g, rel_bias, w_o_rwkv, w_o_nsa, w_out, w_up, w_down):
    B, T, D = x.shape
    depth = w_in.shape[0]
    x2 = x.reshape(B * T, D)
    for l in range(depth):
        mod6 = _ada_mod(c, w_ada[l], b_ada[l]).reshape(B * 6, 1, D)
        proj, small = _in_proj(x2, norm1_g[l].reshape(1, D), mod6, _prep_in_proj_weight(w_in, l), T)
        pr = _prep_rwkv_params(rwkv_mu[l], rwkv_w0[l], rwkv_w2[l], rwkv_a0[l], rwkv_a2[l], rwkv_g2[l],
                               rwkv_k_k[l], rwkv_k_a[l], rwkv_r_k[l], rwkv_ln_w[l], rwkv_ln_b[l])
        o_a = _rwkv_mix(proj, small, pr, B, T)
        pn = _prep_nsa_params(cmp_pe_k[l], cmp_w1_k[l], cmp_w2_k[l], cmp_pe_v[l], cmp_w1_v[l], cmp_w2_v[l],
                              q_norm_g[l], k_norm_g[l], rel_bias, T)
        qp, ks, vs, kw, vw, gates = _nsa_prep(proj, small, pn, B, T)
        ks, vs, kw, vw = _front_pad(ks, True), _front_pad(vs, False), _front_pad(kw, True), _front_pad(vw, False)
        kc, vc = _nsa_compress(proj, pn, B, T)
        o_b = _nsa_attn(qp, kc, vc, ks, vs, kw, vw, gates, pn, B, T)
        mixed = _merge(o_a, o_b, w_o_rwkv[l].astype(BF16), w_o_nsa[l].astype(BF16), proj)
        x1, h2 = _out_proj(mixed, w_out[l].astype(BF16), x2, mod6, norm2_g[l].reshape(1, D), T)
        x2 = _mlp(h2, w_up[l].astype(BF16), w_down[l].astype(BF16), x1, mod6, T)
    return x2.reshape(B, T, D)
```

```python
import functools
import math

import numpy as np
import jax
import jax.numpy as jnp
from jax import lax
from jax.experimental import pallas as pl
from jax.experimental.pallas import tpu as pltpu

F32 = jnp.float32
BF16 = jnp.bfloat16
HIGHEST = lax.Precision.HIGHEST

D_MODEL = 2048
HEAD_DIM = 64
RWKV_WIDTH = D_MODEL // 2
DECAY_LORA = 64
ICLR_LORA = 64
GATE_LORA = 160
RWKV_GN_EPS = 64e-5
NSA_WIDTH = D_MODEL // 2
NSA_HEADS = NSA_WIDTH // HEAD_DIM
NSA_KV_HEADS = 4
NSA_GROUP = NSA_HEADS // NSA_KV_HEADS
NSA_KV_WIDTH = NSA_KV_HEADS * HEAD_DIM
CMP_BLOCK = 32
CMP_STRIDE = 16
SEL_BLOCK = 64
N_SEL = 8
WINDOW = 512
QUERY_BLOCK = 128
REL_BUCKETS = 32
REL_MAX_DIST = 128
D_FF = 4 * D_MODEL
NORM_EPS = 1e-6
NEG_INF = -1e30
FORCE_SCORE = 1e4

RWKV_COLS = 3 * RWKV_WIDTH + DECAY_LORA + ICLR_LORA + GATE_LORA
NSA_COLS = NSA_WIDTH + 6 * NSA_KV_WIDTH + 3 * NSA_HEADS

V7X_VMEM_BYTES = 64 * 1024 * 1024
LANES = 128

COL_RKV = 0
COL_Q = 3 * RWKV_WIDTH
COL_MERGE = COL_Q + NSA_WIDTH
COL_KV = COL_MERGE + 2 * D_MODEL
COL_SMALL = COL_KV + 6 * NSA_KV_WIDTH
SMALL_W = 512
PROJ_COLS = COL_SMALL + SMALL_W
SM_XG = DECAY_LORA + ICLR_LORA
SM_GATES = SM_XG + GATE_LORA


def _vmem_limit(nbytes):
    return int(min(nbytes * 5 // 4 + (4 << 20), V7X_VMEM_BYTES - (8 << 20)))


def _cparams(sem, vmem_bytes):
    return pltpu.CompilerParams(dimension_semantics=sem, vmem_limit_bytes=_vmem_limit(vmem_bytes))


def _ada_kernel(c_ref, w_ref, b_ref, o_ref):
    c = c_ref[...]
    s = c * jax.nn.sigmoid(c)
    o_ref[...] = jnp.dot(s.astype(BF16), w_ref[...].astype(BF16), preferred_element_type=F32) + b_ref[...]


def _ada_mod(c, w_ada, b_ada):
    B, D = c.shape
    N = w_ada.shape[1]
    tn = 1024
    return pl.pallas_call(
        _ada_kernel,
        grid=(N // tn,),
        in_specs=[
            pl.BlockSpec((B, D), lambda j: (0, 0)),
            pl.BlockSpec((D, tn), lambda j: (0, j)),
            pl.BlockSpec((1, tn), lambda j: (0, j)),
        ],
        out_specs=pl.BlockSpec((B, tn), lambda j: (0, j)),
        out_shape=jax.ShapeDtypeStruct((B, N), F32),
        compiler_params=_cparams(("parallel",), 2 * D * tn * 4 + D * tn * 2),
        name="ada_mod",
    )(c, w_ada, b_ada.reshape(1, N))


def _modulated_norm(x, g, sc, sh):
    ms = jnp.mean(x * x, axis=-1, keepdims=True)
    return (x * lax.rsqrt(ms + NORM_EPS) * g) * (1.0 + sc) + sh


def _inproj_kernel(x_ref, g_ref, sh_ref, sc_ref, w_ref, o_ref, small_ref, h_scr):
    j = pl.program_id(1)

    @pl.when(j == 0)
    def _():
        h_scr[...] = _modulated_norm(x_ref[...], g_ref[...], sc_ref[0], sh_ref[0]).astype(BF16)

    y = lax.dot_general(h_scr[...], w_ref[...], (((1,), (1,)), ((), ())),
                        preferred_element_type=F32)
    o_ref[...] = y.astype(BF16)

    @pl.when(j == pl.num_programs(1) - 1)
    def _():
        small_ref[...] = y[:, y.shape[1] - SMALL_W:]


def _in_proj(x2, g1, mod6, w_in_t, T):
    BT, D = x2.shape
    NP = w_in_t.shape[0]
    tm = min(1024, T)
    tn = 1024
    tpb = T // tm
    assert NP - SMALL_W == COL_SMALL and tn >= SMALL_W
    vm = (2 * tm * D * 4 + tm * D * 2 + 2 * D * tn * 2 + 2 * tm * tn * 2 + 2 * tm * SMALL_W * 4
          + tm * tn * 4 + 2 * tm * D * 4)
    return pl.pallas_call(
        _inproj_kernel,
        grid=(BT // tm, NP // tn),
        in_specs=[
            pl.BlockSpec((tm, D), lambda i, j: (i, 0)),
            pl.BlockSpec((1, D), lambda i, j: (0, 0)),
            pl.BlockSpec((1, 1, D), lambda i, j: ((i // tpb) * 6 + 0, 0, 0)),
            pl.BlockSpec((1, 1, D), lambda i, j: ((i // tpb) * 6 + 1, 0, 0)),
            pl.BlockSpec((tn, D), lambda i, j: (j, 0)),
        ],
        out_specs=[pl.BlockSpec((tm, tn), lambda i, j: (i, j)), pl.BlockSpec((tm, SMALL_W), lambda i, j: (i, 0))],
        out_shape=[jax.ShapeDtypeStruct((BT, NP), BF16), jax.ShapeDtypeStruct((BT, SMALL_W), F32)],
        scratch_shapes=[pltpu.VMEM((tm, D), BF16)],
        compiler_params=_cparams(("parallel", "arbitrary"), vm),
        name="in_proj",
    )(x2, g1, mod6, mod6, w_in_t)


def _merge_kernel(oa_ref, ob_ref, wa_ref, wb_ref, ga_ref, gb_ref, o_ref):
    ya = jnp.dot(oa_ref[...], wa_ref[...], preferred_element_type=F32)
    yb = jnp.dot(ob_ref[...], wb_ref[...], preferred_element_type=F32)
    ga, gb = ga_ref[...].astype(F32), gb_ref[...].astype(F32)
    o_ref[...] = (jax.nn.sigmoid(ga) * ya + jax.nn.sigmoid(gb) * yb).astype(BF16)


def _merge(o_a, o_b, w_oa, w_ob, proj):
    BT, W = o_a.shape
    D = w_oa.shape[1]
    tm, tn = 512, 1024
    ga0 = COL_MERGE // tn
    gb0 = (COL_MERGE + D) // tn
    vm = 2 * (2 * tm * W * 2 + 2 * W * tn * 2 + 2 * tm * tn * 4 + tm * tn * 2) + 3 * tm * tn * 4
    return pl.pallas_call(
        _merge_kernel,
        grid=(BT // tm, D // tn),
        in_specs=[
            pl.BlockSpec((tm, W), lambda i, j: (i, 0)),
            pl.BlockSpec((tm, W), lambda i, j: (i, 0)),
            pl.BlockSpec((W, tn), lambda i, j: (0, j)),
            pl.BlockSpec((W, tn), lambda i, j: (0, j)),
            pl.BlockSpec((tm, tn), lambda i, j: (i, ga0 + j)),
            pl.BlockSpec((tm, tn), lambda i, j: (i, gb0 + j)),
        ],
        out_specs=pl.BlockSpec((tm, tn), lambda i, j: (i, j)),
        out_shape=jax.ShapeDtypeStruct((BT, D), BF16),
        compiler_params=_cparams(("parallel", "parallel"), vm),
        name="merge",
    )(o_a, o_b, w_oa, w_ob, proj, proj)


def _outproj_kernel(m_ref, w_ref, x_ref, gt_ref, g_ref, sh_ref, sc_ref, x1_ref, h2_ref):
    y = jnp.dot(m_ref[...], w_ref[...], preferred_element_type=F32)
    x1 = x_ref[...] + gt_ref[0] * y
    x1_ref[...] = x1
    h2_ref[...] = _modulated_norm(x1, g_ref[...], sc_ref[0], sh_ref[0]).astype(BF16)


def _out_proj(mixed, w_out, x2, mod6, g2, T):
    BT, D = x2.shape
    tm = min(512, T)
    tpb = T // tm
    vm = 2 * (tm * D * 2 + D * D * 2 + tm * D * 4 + tm * D * 4 + tm * D * 2) + 3 * tm * D * 4
    mod_spec = lambda k: pl.BlockSpec((1, 1, D), lambda i: ((i // tpb) * 6 + k, 0, 0))
    return pl.pallas_call(
        _outproj_kernel,
        grid=(BT // tm,),
        in_specs=[
            pl.BlockSpec((tm, D), lambda i: (i, 0)),
            pl.BlockSpec((D, D), lambda i: (0, 0)),
            pl.BlockSpec((tm, D), lambda i: (i, 0)),
            mod_spec(2),
            pl.BlockSpec((1, D), lambda i: (0, 0)),
            mod_spec(3),
            mod_spec(4),
        ],
        out_specs=[pl.BlockSpec((tm, D), lambda i: (i, 0)), pl.BlockSpec((tm, D), lambda i: (i, 0))],
        out_shape=[jax.ShapeDtypeStruct((BT, D), F32), jax.ShapeDtypeStruct((BT, D), BF16)],
        compiler_params=_cparams(("parallel",), vm),
        name="out_proj",
    )(mixed, w_out, x2, mod6, g2, mod6, mod6)


OUT_CHUNK = 256


def _mlp_kernel(h_ref, wu_ref, wd_ref, x_ref, gt_ref, o_ref, acc_ref):
    f = pl.program_id(1)

    @pl.when(f == 0)
    def _():
        acc_ref[...] = jnp.zeros_like(acc_ref)

    u = jnp.dot(h_ref[...], wu_ref[...], preferred_element_type=F32)
    u = jnp.square(jnp.maximum(u, 0.0)).astype(BF16)
    for n in range(acc_ref.shape[1] // OUT_CHUNK):
        cs = slice(n * OUT_CHUNK, (n + 1) * OUT_CHUNK)
        acc_ref[:, cs] += jnp.dot(u, wd_ref[:, cs], preferred_element_type=F32)

    @pl.when(f == pl.num_programs(1) - 1)
    def _():
        o_ref[...] = x_ref[...] + gt_ref[0] * acc_ref[...]


def _mlp(h2, w_up, w_down, x1, mod6, T):
    BT, D = x1.shape
    F = w_up.shape[1]
    tm = min(512, T)
    tf = 1024
    tpb = T // tm
    vm = 2 * (tm * D * 2 + 2 * D * tf * 2 + 2 * tm * D * 4) + tm * D * 4 + 2 * tm * tf * 4
    return pl.pallas_call(
        _mlp_kernel,
        grid=(BT // tm, F // tf),
        in_specs=[
            pl.BlockSpec((tm, D), lambda i, f: (i, 0)),
            pl.BlockSpec((D, tf), lambda i, f: (0, f)),
            pl.BlockSpec((tf, D), lambda i, f: (f, 0)),
            pl.BlockSpec((tm, D), lambda i, f: (i, 0)),
            pl.BlockSpec((1, 1, D), lambda i, f: ((i // tpb) * 6 + 5, 0, 0)),
        ],
        out_specs=pl.BlockSpec((tm, D), lambda i, f: (i, 0)),
        out_shape=jax.ShapeDtypeStruct((BT, D), F32),
        scratch_shapes=[pltpu.VMEM((tm, D), F32)],
        compiler_params=_cparams(("parallel", "arbitrary"), vm),
        name="mlp",
    )(h2, w_up, w_down, x1, mod6)


def _dotb(a, b):
    return jnp.dot(a.astype(BF16), b.astype(BF16), preferred_element_type=F32)


def _dotb_nt(a, b):
    return lax.dot_general(a.astype(BF16), b.astype(BF16), (((1,), (1,)), ((), ())),
                           preferred_element_type=F32)


def _split_bf16(x, terms):
    parts, rem = [], x
    for t in range(terms):
        p = rem.astype(BF16)
        parts.append(p)
        if t + 1 < terms:
            rem = rem - p.astype(F32)
    return parts


def _dot_sel(x, sel, terms):
    sel = sel.astype(BF16)
    return sum(jnp.dot(p, sel, preferred_element_type=F32) for p in _split_bf16(x, terms))


def _sel_dot(sel, x, terms):
    sel = sel.astype(BF16)
    return sum(jnp.dot(sel, p, preferred_element_type=F32) for p in _split_bf16(x, terms))


def _dot3(a, b_hi, b_lo):
    a_hi, a_lo = _split_bf16(a, 2)
    return (jnp.dot(a_hi, b_hi, preferred_element_type=F32) + jnp.dot(a_lo, b_hi, preferred_element_type=F32)
            + jnp.dot(a_hi, b_lo, preferred_element_type=F32))


def _iota(shape, axis):
    return lax.broadcasted_iota(jnp.int32, shape, axis)


def _same_head_mask(n):
    return (_iota((n, n), 0) >> 6) == (_iota((n, n), 1) >> 6)


RW_TC = 256
RW_C = 64
RW_LANES = 4 * HEAD_DIM


def _block_diag(x, bdmask):
    return jnp.where(bdmask, jnp.concatenate([x, x, x, x], axis=0), 0.0)


def _rwkv_kernel(r_ref, k_ref, v_ref, s_ref, mur_ref, muk_ref, muv_ref, mus_ref,
                 w0_ref, a0_ref, kk_ref, ka_ref, rk_ref, lnw_ref, lnb_ref,
                 w2h_ref, w2l_ref, a2_ref, g2_ref, o_ref,
                 pr_scr, pk_scr, pv_scr, ps_scr, state_scr):
    TC = r_ref.shape[0]
    C = RW_C
    W = RW_LANES

    @pl.when(pl.program_id(1) == 0)
    def _():
        pr_scr[...] = jnp.zeros_like(pr_scr)
        pk_scr[...] = jnp.zeros_like(pk_scr)
        pv_scr[...] = jnp.zeros_like(pv_scr)
        ps_scr[...] = jnp.zeros_like(ps_scr)
        state_scr[...] = jnp.zeros_like(state_scr)

    def shift_mix(p_ref, prev_scr, mu_ref):
        p = p_ref[...].astype(F32)
        rolled = pltpu.roll(p, 1, 0)
        first = jnp.where(_iota((8, p.shape[1]), 0) == 0, prev_scr[0:1, :], rolled[0:8])
        shifted = jnp.concatenate([first, rolled[8:]], axis=0)
        prev_scr[0:1, :] = p[TC - 1:TC, :]
        return p + (shifted - p) * mu_ref[...]

    CW = r_ref.shape[1]
    G = CW // W
    NQ = TC // C
    groups = lambda x: [x[:, g * W:(g + 1) * W] for g in range(G)]
    per_group = lambda f, x: jnp.concatenate([f(xg) for xg in groups(x)], axis=1)

    r = shift_mix(r_ref, pr_scr, mur_ref)
    k = shift_mix(k_ref, pk_scr, muk_ref)
    v = shift_mix(v_ref, pv_scr, muv_ref)
    sm = shift_mix(s_ref, ps_scr, mus_ref)
    xwa = sm[:, 0:SM_XG]

    bdmask = _same_head_mask(W)
    bones = jnp.where(bdmask, 1.0, 0.0)
    head_sum = lambda x: per_group(lambda xg: _dot_sel(xg, bones, 1), x)
    tri = jnp.where(_same_head_mask(TC) & (_iota((TC, TC), 1) <= _iota((TC, TC), 0)), 1.0, 0.0)
    lane = _iota((C, W), 1)
    row = _iota((C, W), 0)
    lanehead = lane >> 6
    strict = (lane & 63) < row
    incl = (lane & 63) <= row
    eye_cat = jnp.where((lane & 63) == row, 1.0, 0.0)

    wlin = w0_ref[...] + _dot3(jnp.tanh(xwa), w2h_ref[...], w2l_ref[...])
    a = jax.nn.sigmoid(a0_ref[...] + _dotb(xwa, a2_ref[...]))
    gate = _dotb(jax.nn.sigmoid(sm[:, SM_XG:SM_XG + 256]), g2_ref[...])
    z = -wlin
    softplus = jnp.maximum(z, 0.0) + jnp.log(1.0 + jnp.exp(-jnp.abs(z)))
    ld = -jnp.exp(-softplus - 0.5)
    cum = _sel_dot(tri, ld, 3)
    cum_last = jnp.concatenate(
        [jnp.broadcast_to(cum[(q + 1) * C - 1:(q + 1) * C, :], (C, CW)) for q in range(NQ)], axis=0)
    kk = k * kk_ref[...]
    kk = kk * lax.rsqrt(jnp.maximum(head_sum(kk * kk), 1e-24))
    k2 = k * (1.0 + (a - 1.0) * ka_ref[...])
    bvec = kk * a
    e_inv = jnp.exp(-cum)
    e_end = jnp.exp(cum_last - cum)
    At_f = -kk * jnp.exp(cum - ld)
    Rt_f = r * jnp.exp(cum)
    Bt_f = bvec * e_inv
    Kt_f = k2 * e_inv
    Bg_f = bvec * e_end
    Kg_f = k2 * e_end
    g_end = jnp.exp(cum_last)

    items = [(g, q) for q in range(NQ) for g in range(G)]
    blk = lambda x, g, q: x[q * C:(q + 1) * C, g * W:(g + 1) * W]
    bd01 = jnp.where(bdmask, 1.0, 0.0).astype(BF16)

    def bd(x):
        xb = x.astype(BF16)
        return jnp.concatenate([xb, xb, xb, xb], axis=0) * bd01

    At = {it: blk(At_f, *it) for it in items}
    Rt = {it: blk(Rt_f, *it) for it in items}
    Vq = {it: blk(v, *it) for it in items}
    AA = {}
    for it in items:
        bk = jnp.concatenate([jnp.where(lanehead == h, X, 0.0)
                              for X in (blk(Bt_f, *it), blk(Kt_f, *it)) for h in range(4)], axis=0)
        AA[it] = _dotb_nt(jnp.concatenate([At[it], Rt[it]], axis=0), bk)
    A_ab = {it: jnp.where(strict, AA[it][0:C, 0:W], 0.0) for it in items}
    A_ak = {it: jnp.where(strict, AA[it][0:C, W:2 * W], 0.0) for it in items}
    A_rb = {it: jnp.where(incl, AA[it][C:2 * C, 0:W], 0.0) for it in items}
    A_rk = {it: jnp.where(incl, AA[it][C:2 * C, W:2 * W], 0.0) for it in items}
    M = dict(A_ab)
    Tm = {it: eye_cat + A_ab[it] for it in items}
    for _ in range(5):
        M = {it: _dotb(M[it], bd(M[it])) for it in items}
        Tm = {it: Tm[it] + _dotb(M[it], bd(Tm[it])) for it in items}
    Vbd = {it: bd(Vq[it]) for it in items}
    akv = {it: _dotb(A_ak[it], Vbd[it]) for it in items}
    rkv = {it: _dotb(A_rk[it], Vbd[it]) for it in items}

    S = [state_scr[g] for g in range(G)]
    ys = []
    for q in range(NQ):
        its = [(g, q) for g in range(G)]
        rhs = [_dotb_nt(At[it], S[it[0]]) + akv[it] for it in its]
        U = [_dotb(Tm[it], bd(rhs[g])) for g, it in enumerate(its)]
        ys.append(jnp.concatenate(
            [_dotb_nt(Rt[it], S[g]) + _dotb(A_rb[it], bd(U[g])) + rkv[it] for g, it in enumerate(its)], axis=1))
        upd = [_dotb(jnp.concatenate([U[g], Vq[it]], axis=0).T,
                     jnp.concatenate([blk(Bg_f, *it), blk(Kg_f, *it)], axis=0)) for g, it in enumerate(its)]
        S = [S[g] * blk(g_end, g, q)[0:1, :] + jnp.where(bdmask, upd[g], 0.0) for g in range(G)]
    for g in range(G):
        state_scr[g] = S[g]
    y = jnp.concatenate(ys, axis=0)

    inv_n = 1.0 / HEAD_DIM
    d = y - head_sum(y) * inv_n
    var = head_sum(d * d) * inv_n
    yn = d * lax.rsqrt(var + RWKV_GN_EPS) * lnw_ref[...] + lnb_ref[...]
    bonus = head_sum(r * k2 * rk_ref[...]) * v
    o_ref[...] = ((yn + bonus) * gate).astype(BF16)


def _rwkv_mix(proj, small, pr, B, T):
    BT = proj.shape[0]
    TC = min(RW_TC, T)
    W = RW_LANES
    CW = RWKV_WIDTH
    nct = T // TC
    row = lambda b, c: b * nct + c
    full = lambda shape: pl.BlockSpec(shape, lambda b, c: (0, 0))
    vec = full((1, CW))
    in_specs = [
        pl.BlockSpec((TC, CW), lambda b, c: (row(b, c), 0)),
        pl.BlockSpec((TC, CW), lambda b, c: (row(b, c), 1)),
        pl.BlockSpec((TC, CW), lambda b, c: (row(b, c), 2)),
        pl.BlockSpec((TC, SMALL_W), lambda b, c: (row(b, c), 0)),
        vec, vec, vec,
        full((1, SMALL_W)),
        vec, vec, vec, vec, vec, vec, vec,
        full((SM_XG, CW)), full((SM_XG, CW)), full((SM_XG, CW)), full((256, CW)),
    ]
    vm = (2 * (3 * TC * CW * 4 + TC * SMALL_W * 4 + TC * CW * 2 + (3 * SM_XG + 256) * CW * 2)
          + 24 * TC * CW * 4)
    return pl.pallas_call(
        _rwkv_kernel,
        grid=(B, nct),
        in_specs=in_specs,
        out_specs=pl.BlockSpec((TC, CW), lambda b, c: (row(b, c), 0)),
        out_shape=jax.ShapeDtypeStruct((BT, CW), BF16),
        scratch_shapes=[pltpu.VMEM((8, CW), F32), pltpu.VMEM((8, CW), F32), pltpu.VMEM((8, CW), F32),
                        pltpu.VMEM((8, SMALL_W), F32), pltpu.VMEM((CW // W, W, W), F32)],
        compiler_params=_cparams(("parallel", "arbitrary"), vm),
        name="rwkv_mix",
    )(proj, proj, proj, small, pr["mu_r"], pr["mu_k"], pr["mu_v"], pr["mu_s"],
      pr["w0"], pr["a0"], pr["k_k"], pr["k_a"], pr["r_k"], pr["ln_w"], pr["ln_b"],
      pr["w2h"], pr["w2l"], pr["a2"], pr["g2"])


NSA_TT = 256
BLOCK_LANE0 = HEAD_DIM
PADFLAG_LANE = BLOCK_LANE0 + 32


def _nsa_prep_kernel(q_ref, kv0_ref, kv1_ref, kv2_ref, s_ref, qg_ref, kgs_ref, kgw_ref, e_ref, gsel_ref,
                     qp_ref, ksl_ref, vsl_ref, kwl_ref, vwl_ref, gt_ref):
    tt = q_ref.shape[0]
    n_pad = NSA_KPAD // tt
    step = pl.program_id(1)
    lane = _iota((tt, LANES), 1)

    @pl.when(step < n_pad)
    def _():
        flag = jnp.broadcast_to(jnp.where(lane == PADFLAG_LANE, 1.0, 0.0).astype(BF16), ksl_ref.shape[1:])
        ksl_ref[0] = flag
        kwl_ref[0] = flag
        vsl_ref[0] = jnp.zeros_like(flag)
        vwl_ref[0] = jnp.zeros_like(flag)

    @pl.when(step >= n_pad)
    def _():
        _nsa_prep_tile(step - n_pad, lane, q_ref, (kv0_ref, kv1_ref, kv2_ref), s_ref, qg_ref, kgs_ref, kgw_ref,
                       e_ref, gsel_ref, qp_ref, ksl_ref, vsl_ref, kwl_ref, vwl_ref, gt_ref)


def _nsa_prep_tile(tile, lane, q_ref, kv_refs, s_ref, qg_ref, kgs_ref, kgw_ref, e_ref, gsel_ref,
                   qp_ref, ksl_ref, vsl_ref, kwl_ref, vwl_ref, gt_ref):
    tt = q_ref.shape[0]
    QW = NSA_GROUP * HEAD_DIM
    bones = jnp.where(_same_head_mask(QW), 1.0, 0.0)
    bones2 = jnp.where(_same_head_mask(LANES), 1.0, 0.0)
    block_id = (tile * tt + _iota((tt, LANES), 0)) >> 6
    onehot = jnp.where(lane == block_id + BLOCK_LANE0, 1.0, 0.0)
    small = _split_bf16(s_ref[...], 3)

    def pair(g, branch):
        off = (3 * g + branch) * LANES
        return kv_refs[off // SMALL_W][:, off % SMALL_W:off % SMALL_W + LANES].astype(F32)

    def slabs(g, x, gain, k_ref, v_ref):
        ms = _dot_sel(x * x, bones2, 1) * (1.0 / HEAD_DIM)
        k_ref[0, g] = jnp.where(lane < HEAD_DIM, x * lax.rsqrt(ms + NORM_EPS) * gain, onehot).astype(BF16)
        v_ref[0, g] = jnp.where(lane >= HEAD_DIM, x, 1.0).astype(BF16)

    for g in range(NSA_KV_HEADS):
        q = q_ref[:, g * QW:(g + 1) * QW].astype(F32)
        ms = _dot_sel(q * q, bones, 1) * (1.0 / HEAD_DIM)
        qn = (q * lax.rsqrt(ms + NORM_EPS) * qg_ref[...]) * (HEAD_DIM ** -0.5)
        qs = jnp.dot(qn.astype(BF16), e_ref[...], preferred_element_type=F32).astype(BF16)
        for h in range(NSA_GROUP):
            qp_ref[0, NSA_GROUP * g + h] = qs[:, h * LANES:(h + 1) * LANES]
        slabs(g, pair(g, 1), kgs_ref[...], ksl_ref, vsl_ref)
        slabs(g, pair(g, 2), kgw_ref[...], kwl_ref, vwl_ref)
        sel = gsel_ref[g].astype(BF16)
        gt_ref[0, g] = jax.nn.sigmoid(sum(jnp.dot(part, sel, preferred_element_type=F32) for part in small))


def _nsa_prep(proj, small, pn, B, T):
    tt = min(NSA_TT, T)
    ntt = T // tt
    G = NSA_KV_HEADS
    QW = NSA_GROUP * HEAD_DIM
    n_pad = NSA_KPAD // tt
    tile = lambda t: jnp.maximum(t - n_pad, 0)
    row = lambda b, t: b * ntt + tile(t)
    full = lambda shape: pl.BlockSpec(shape, lambda b, t: tuple(0 for _ in shape))
    kv_spec = lambda j: pl.BlockSpec((tt, SMALL_W), lambda b, t: (row(b, t), COL_KV // SMALL_W + j))
    slab = pl.BlockSpec((1, G, tt, LANES), lambda b, t: (b, 0, t, 0))
    vm = 2 * (tt * NSA_WIDTH * 4 + 4 * tt * SMALL_W * 4 + QW * 4 * LANES * 2 + G * SMALL_W * LANES * 4
              + 16 * tt * LANES * 2 + 16 * tt * LANES * 2 + 4 * tt * LANES * 4) + 24 * tt * QW * 4
    return pl.pallas_call(
        _nsa_prep_kernel,
        grid=(B, ntt + n_pad),
        in_specs=[
            pl.BlockSpec((tt, NSA_WIDTH), lambda b, t: (row(b, t), COL_Q // NSA_WIDTH)),
            kv_spec(0), kv_spec(1), kv_spec(2),
            pl.BlockSpec((tt, SMALL_W), lambda b, t: (row(b, t), 0)),
            full((1, QW)), full((1, LANES)), full((1, LANES)), full((QW, NSA_GROUP * LANES)),
            full((G, SMALL_W, LANES)),
        ],
        out_specs=[pl.BlockSpec((1, NSA_HEADS, tt, LANES), lambda b, t: (b, 0, tile(t), 0)),
                   slab, slab, slab, slab,
                   pl.BlockSpec((1, G, tt, LANES), lambda b, t: (b, 0, tile(t), 0))],
        out_shape=[jax.ShapeDtypeStruct((B, NSA_HEADS, T, LANES), BF16)]
        + [jax.ShapeDtypeStruct((B, G, T + NSA_KPAD, LANES), BF16)] * 4
        + [jax.ShapeDtypeStruct((B, G, T, LANES), F32)],
        compiler_params=_cparams(("parallel", "arbitrary"), vm),
        name="nsa_prep",
    )(proj, proj, proj, proj, small, pn["q_g"], pn["kg_sel"], pn["kg_win"], pn["q_spread"], pn["gate_sel"])


def _gelu_tanh(x):
    return 0.5 * x * (1.0 + jnp.tanh(math.sqrt(2.0 / math.pi) * (x + 0.044715 * (x * x * x))))


def _nsa_compress_kernel(x_ref, pea_ref, peb_ref, wa_ref, wb_ref, w2_ref, kg_ref, kc_ref, vc_ref, x_scr):
    nsub = x_ref.shape[0] // CMP_STRIDE
    x_scr[...] = x_ref[...].astype(F32)
    xs = jnp.concatenate([x_scr[pl.ds(s, nsub, stride=CMP_STRIDE), :] for s in range(CMP_STRIDE)], axis=1)
    p0 = jnp.dot((xs + pea_ref[...]).astype(BF16), wa_ref[...], preferred_element_type=F32)
    p1 = jnp.dot((xs + peb_ref[...]).astype(BF16), wb_ref[...], preferred_element_type=F32)
    hid = _gelu_tanh(p0 + pltpu.roll(p1, nsub - 1, 0))
    out = jnp.dot(hid.astype(BF16), w2_ref[...], preferred_element_type=F32)
    bones2 = jnp.where(_same_head_mask(LANES), 1.0, 0.0)
    ms = _dot_sel(out * out, bones2, 2) * (1.0 / HEAD_DIM)
    is_k = _iota(out.shape, 1) < HEAD_DIM
    kc_ref[0, 0] = jnp.where(is_k, out * lax.rsqrt(ms + NORM_EPS) * kg_ref[...], 0.0).astype(BF16)
    vc_ref[0, 0] = jnp.where(is_k, 0.0, out).astype(BF16)


def _nsa_compress(proj, pn, B, T):
    G = NSA_KV_HEADS
    nsub = T // CMP_STRIDE
    kv_blk = COL_KV // LANES
    KW = CMP_STRIDE * LANES
    full = lambda shape: pl.BlockSpec(shape, lambda b, g: tuple(0 for _ in shape))
    vm = 2 * (T * LANES * 4 + 2 * KW * LANES * 2 + nsub * LANES * 2) + 6 * nsub * KW * 4
    return pl.pallas_call(
        _nsa_compress_kernel,
        grid=(B, G),
        in_specs=[
            pl.BlockSpec((T, LANES), lambda b, g: (b, kv_blk + 3 * g)),
            full((1, KW)), full((1, KW)), full((KW, LANES)), full((KW, LANES)), full((LANES, LANES)),
            full((1, LANES)),
        ],
        out_specs=[pl.BlockSpec((1, 1, nsub, LANES), lambda b, g: (b, g, 0, 0))] * 2,
        out_shape=[jax.ShapeDtypeStruct((B, G, nsub, LANES), BF16)] * 2,
        scratch_shapes=[pltpu.VMEM((T, LANES), F32)],
        compiler_params=_cparams(("parallel", "parallel"), vm),
        name="nsa_compress",
    )(proj, pn["pe_a"], pn["pe_b"], pn["cw_a"], pn["cw_b"], pn["cw2"], pn["kg_cmp"])


NSA_GROUPS_PER_STEP = 4
NSA_KPAD = WINDOW
SEL_CHUNK = 4 * QUERY_BLOCK
WIN_KEYS = WINDOW + QUERY_BLOCK


def _lane_tile_max(s):
    tiles = [s[:, j * LANES:(j + 1) * LANES] for j in range(s.shape[1] // LANES)]
    while len(tiles) > 1:
        tiles = [jnp.maximum(a, b) for a, b in zip(tiles[0::2], tiles[1::2])] + ([tiles[-1]] if len(tiles) % 2 else [])
    return tiles[0]


def _nsa_attn_kernel(q_ref, kc_ref, vc_ref, ks_ref, vs_ref, kw_ref, vw_ref, gt_ref, stab_ref, wtab_ref, ctab_ref,
                     selmt_ref, gather_ref, grep_ref, o_ref, sbuf, mx_scr, acc_scr, sc_scr, *, n_sel):
    i = pl.program_id(2)
    QB = QUERY_BLOCK
    HG = NSA_GROUP
    R = HG * QB
    t0 = i * QB
    groups = range(kc_ref.shape[1])
    each = lambda f: [f(g) for g in groups]
    tile4 = lambda z: jnp.concatenate([z, z, z, z], axis=0)
    heads = lambda ref, g: ref[HG * g:HG * (g + 1)]
    lane = _iota((QB, LANES), 1)
    pad_mask = jnp.where(lane == PADFLAG_LANE, NEG_INF, 0.0)
    q = each(lambda g: q_ref[0, HG * g:HG * (g + 1)].reshape(R, LANES))
    q32 = each(lambda g: q[g].astype(F32))

    ncp = kc_ref.shape[2]
    s = each(lambda g: _dotb_nt(q[g], kc_ref[0, g]) + heads(ctab_ref, g)[:, 0].reshape(R, ncp))
    m = each(lambda g: jnp.max(s[g], axis=-1, keepdims=True))
    p_c = each(lambda g: jnp.exp(s[g] - m[g]))
    lsum = each(lambda g: jnp.sum(p_c[g], axis=-1, keepdims=True))
    p_c = each(lambda g: p_c[g] * jnp.where(m[g] > 0.5 * NEG_INF, 1.0 / lsum[g], 0.0))
    o_c = each(lambda g: jnp.dot(p_c[g].astype(BF16), vc_ref[0, g], preferred_element_type=F32))

    wrows = pl.ds(pl.multiple_of(t0, QB), WIN_KEYS)
    q_win = each(lambda g: (q32[g] + tile4(pad_mask)).astype(BF16))
    s = each(lambda g: _dotb_nt(q_win[g], kw_ref[0, g, wrows, :]) + heads(wtab_ref, g).reshape(R, WIN_KEYS))
    p = each(lambda g: jnp.exp(s[g] - jnp.max(_lane_tile_max(s[g]), axis=-1, keepdims=True)))
    acc_w = each(lambda g: jnp.dot(p[g].astype(BF16), vw_ref[0, g, wrows, :], preferred_element_type=F32))

    psum = each(lambda g: p_c[g][0:QB] + p_c[g][QB:2 * QB] + p_c[g][2 * QB:3 * QB] + p_c[g][3 * QB:4 * QB])
    selmt = selmt_ref[...]
    parts = each(lambda g: _split_bf16(psum[g], 3))
    imp = each(lambda g: sum(lax.dot_general(selmt, part, (((1,), (1,)), ((), ())), preferred_element_type=F32)
                             for part in parts[g]))
    ns = selmt.shape[0]
    blk = _iota((ns, QB), 0)
    cur = (t0 + _iota((ns, QB), 1)) >> 6
    forced = (blk == 0) | (blk == cur) | (blk == cur - 1)
    score = each(lambda g: jnp.where(forced, FORCE_SCORE, jnp.where(blk <= cur, imp[g], -1.0)))
    for g in groups:
        sc_scr[g] = score[g]
    ranks = [[] for _ in groups]
    for j in range(ns):
        lower = jnp.where(blk > j, 1.0, 0.0)
        for g in groups:
            other = sc_scr[g, j:j + 1, :]
            ranks[g].append(jnp.where(other > score[g], 1.0, 0.0) + jnp.where(other == score[g], lower, 0.0))
    while len(ranks[0]) > 1:
        ranks = [[a + b for a, b in zip(r[0::2], r[1::2])] + ([r[-1]] if len(r) % 2 else []) for r in ranks]
    chosen_t = each(lambda g: jnp.where(ranks[g][0] < n_sel, 1.0, 0.0))
    zrows = lambda n: jnp.zeros((n, QB), F32)
    chosen = each(lambda g: jnp.concatenate([zrows(BLOCK_LANE0), chosen_t[g], zrows(LANES - BLOCK_LANE0 - ns)],
                                            axis=0).T)
    is_block_lane = (lane >= BLOCK_LANE0) & (lane < BLOCK_LANE0 + ns)
    q_sel = each(lambda g: (q32[g] + tile4(jnp.where(is_block_lane, (chosen[g] - 1.0) * (-NEG_INF), pad_mask))
                            ).astype(BF16))

    def chunk_rows(c):
        return pl.ds(pl.multiple_of((i - 4 * c + 1) * QB, QB), SEL_CHUNK)

    s0 = each(lambda g: _dotb_nt(q_sel[g], ks_ref[0, g, chunk_rows(0), :]) + heads(stab_ref, g).reshape(R, SEL_CHUNK))
    for g in groups:
        sbuf[g, 0] = s0[g]
        mx_scr[g] = _lane_tile_max(s0[g])

    def scores_body(c, carry):
        s = each(lambda g: _dotb_nt(q_sel[g], ks_ref[0, g, chunk_rows(c), :]))
        for g in groups:
            sbuf[g, c] = s[g]
            mx_scr[g] = jnp.maximum(mx_scr[g], _lane_tile_max(s[g]))
        return carry

    n_far = i // 4
    lax.fori_loop(1, n_far + 1, scores_body, 0)
    m_s = each(lambda g: jnp.max(mx_scr[g], axis=-1, keepdims=True))

    acc_scr[...] = jnp.zeros_like(acc_scr)

    def values_body(c, carry):
        p = each(lambda g: jnp.exp(sbuf[g, c] - m_s[g]).astype(BF16))
        for g in groups:
            acc_scr[g] += jnp.dot(p[g], vs_ref[0, g, chunk_rows(c), :], preferred_element_type=F32)
        return carry

    lax.fori_loop(0, n_far + 1, values_body, 0)

    is_value = _iota((R, LANES), 1) >= HEAD_DIM
    normalised = lambda acc: jnp.where(is_value, acc * (1.0 / pltpu.roll(acc, HEAD_DIM, 1)), 0.0)

    def natural(o):
        cat = jnp.concatenate([o[h * QB:(h + 1) * QB] for h in range(HG)], axis=1).astype(BF16)
        return jnp.dot(cat, gather_ref[...], preferred_element_type=F32)

    branches = each(lambda g: [o_c[g], normalised(acc_scr[g]), normalised(acc_w[g])])
    gate = each(lambda g: [_dot_sel(gt_ref[0, g], grep_ref[c], 1) for c in range(3)])
    W = HG * HEAD_DIM
    for g in groups:
        out = sum(gate[g][c] * natural(branches[g][c]) for c in range(3))
        o_ref[:, g * W:(g + 1) * W] = out.astype(BF16)


def _nsa_attn(qp, kc, vc, ks, vs, kw, vw, gates, pn, B, T):
    G = NSA_KV_HEADS
    QB = QUERY_BLOCK
    NQ = T // QB
    ncp = kc.shape[2]
    ns = T // SEL_BLOCK
    TP = T + NSA_KPAD
    R = NSA_GROUP * QB
    n_sel = min(N_SEL, ns)
    n_chunks = (NQ - 1) // 4 + 1
    NG = NSA_GROUPS_PER_STEP
    HB = NSA_GROUP * NG
    once = pl.Buffered(1)
    gtab = lambda w: pl.BlockSpec((HB, QB, w), lambda b, g, i: (g, 0, 0), pipeline_mode=once)
    slab = lambda rows: pl.BlockSpec((1, NG, rows, LANES), lambda b, g, i: (b, g, 0, 0), pipeline_mode=once)
    vm = (NG * (2 * ncp * LANES * 2 + 4 * TP * LANES * 2 + R * (SEL_CHUNK + WIN_KEYS) * 4)
          + 2 * NG * (R * LANES * 2 + QB * LANES * 4 + R * ncp * 4 + QB * NSA_GROUP * HEAD_DIM * 2)
          + NG * (n_chunks * R * SEL_CHUNK * 4 + 2 * R * LANES * 4 + 4 * R * WIN_KEYS * 4))
    return pl.pallas_call(
        functools.partial(_nsa_attn_kernel, n_sel=n_sel),
        grid=(B, G // NG, NQ),
        in_specs=[
            pl.BlockSpec((1, HB, QB, LANES), lambda b, g, i: (b, g, i, 0)),
            slab(ncp), slab(ncp), slab(TP), slab(TP), slab(TP), slab(TP),
            pl.BlockSpec((1, NG, QB, LANES), lambda b, g, i: (b, g, i, 0)),
            gtab(SEL_CHUNK), gtab(WIN_KEYS),
            pl.BlockSpec((HB, 1, QB, ncp), lambda b, g, i: (g, i, 0, 0)),
            pl.BlockSpec((ns, ncp), lambda b, g, i: (0, 0)),
            pl.BlockSpec((NSA_GROUP * LANES, NSA_GROUP * HEAD_DIM), lambda b, g, i: (0, 0)),
            pl.BlockSpec((3, LANES, NSA_GROUP * HEAD_DIM), lambda b, g, i: (0, 0, 0)),
        ],
        out_specs=pl.BlockSpec((QB, HB * HEAD_DIM), lambda b, g, i: (b * NQ + i, g)),
        out_shape=jax.ShapeDtypeStruct((B * T, NSA_WIDTH), BF16),
        scratch_shapes=[pltpu.VMEM((NG, n_chunks, R, SEL_CHUNK), F32), pltpu.VMEM((NG, R, LANES), F32),
                        pltpu.VMEM((NG, R, LANES), F32), pltpu.VMEM((NG, ns, QB), F32)],
        compiler_params=_cparams(("parallel", "parallel", "arbitrary"), vm),
        name="nsa_attn",
    )(qp, kc, vc, ks, vs, kw, vw, gates, pn["stab"], pn["wtab"], pn["ctab"], pn["sel_mt"], pn["gather"],
      pn["gate_rep"])


def _rel_bucket_table():
    n = np.arange(REL_MAX_DIST + 1)
    max_exact = REL_BUCKETS // 2
    nf = np.maximum(n, max_exact).astype(np.float32)
    large = max_exact + (np.log(nf / np.float32(max_exact)) / np.float32(math.log(REL_MAX_DIST / max_exact))
                         * np.float32(REL_BUCKETS - max_exact)).astype(np.int32)
    large = np.minimum(large, REL_BUCKETS - 1)
    return np.where(n < max_exact, n, large).astype(np.int32)


def _sel_to_cmp_matrix(T, ncp):
    nc = T // CMP_STRIDE - CMP_BLOCK // CMP_STRIDE + 1
    ns = T // SEL_BLOCK
    cs = np.arange(nc) * CMP_STRIDE
    ss = np.arange(ns) * SEL_BLOCK
    lo = np.maximum(cs[None, :], ss[:, None])
    hi = np.minimum(cs[None, :] + CMP_BLOCK, ss[:, None] + SEL_BLOCK)
    out = np.zeros((ns, ncp), np.float32)
    out[:, :nc] = np.maximum(hi - lo, 0) / CMP_BLOCK
    return out


def _bias_tables(rel_bias, rel, attend, shift=None):
    bucket = np.where(attend, _rel_bucket_table()[np.clip(rel, 0, REL_MAX_DIST)], REL_BUCKETS)
    rows = rel_bias if shift is None else rel_bias - shift[None, :]
    rows = jnp.concatenate([rows, jnp.full((1, rel_bias.shape[1]), NEG_INF, rel_bias.dtype)], axis=0)
    onehot = (jnp.asarray(bucket.reshape(1, -1)) == jnp.arange(REL_BUCKETS + 1, dtype=jnp.int32)[:, None]).astype(F32)
    tab = jnp.einsum('bh,bn->hn', rows, onehot, precision=HIGHEST)
    return tab.reshape((rel_bias.shape[1],) + rel.shape)


def _prep_in_proj_weight(w_in_all, l):
    D = w_in_all.shape[1]
    nsa0 = RWKV_COLS
    kv0 = nsa0 + NSA_WIDTH
    gates0 = kv0 + 6 * NSA_KV_WIDTH
    merge0 = RWKV_COLS + NSA_COLS
    wt = jnp.swapaxes(w_in_all[l], 0, 1)
    kv = wt[kv0:gates0].reshape(3, 2, NSA_KV_HEADS, HEAD_DIM, D)
    kv = jnp.transpose(kv, (2, 0, 1, 3, 4)).reshape(6 * NSA_KV_WIDTH, D)
    pad = jnp.zeros((SMALL_W - (SM_GATES + 3 * NSA_HEADS), D), wt.dtype)
    return jnp.concatenate([
        wt[0:3 * RWKV_WIDTH],
        wt[nsa0:kv0],
        wt[merge0:merge0 + 2 * D_MODEL],
        kv,
        wt[3 * RWKV_WIDTH:RWKV_COLS],
        wt[gates0:merge0],
        pad,
    ], axis=0).astype(BF16)


def _prep_rwkv_params(mu, w0, w2, a0, a2, g2, k_k, k_a, r_k, ln_w, ln_b):
    C = RWKV_WIDTH
    row = lambda z: z.reshape(1, -1).astype(F32)
    mu_s = jnp.concatenate([mu[3 * C:], jnp.zeros((SMALL_W - (RWKV_COLS - 3 * C),), F32)]).reshape(1, SMALL_W)
    zl = jnp.zeros((DECAY_LORA, C), F32)

    w2p = jnp.concatenate([w2, zl], axis=0)
    w2h = w2p.astype(BF16)
    return dict(
        mu_r=row(mu[0:C]), mu_k=row(mu[C:2 * C]), mu_v=row(mu[2 * C:3 * C]), mu_s=mu_s,
        w0=row(w0), a0=row(a0), k_k=row(k_k), k_a=row(k_a), r_k=row(r_k), ln_w=row(ln_w), ln_b=row(ln_b),
        w2h=w2h, w2l=(w2p - w2h.astype(F32)).astype(BF16),
        a2=jnp.concatenate([zl, a2], axis=0).astype(BF16),
        g2=jnp.concatenate([g2, jnp.zeros((256 - GATE_LORA, C), F32)], axis=0).astype(BF16),
    )


def _prep_nsa_params(pe_k, w1_k, w2_k, pe_v, w1_v, w2_v, q_g, k_g, rel_bias, T):
    hd = HEAD_DIM
    ones = jnp.ones((hd,), F32)
    ncp = T // CMP_STRIDE
    NQ = T // QUERY_BLOCK

    def blockdiag(a, b):
        lead = ((0, 0),) * (a.ndim - 2)
        return jnp.pad(a, lead + ((0, hd), (0, hd))) + jnp.pad(b, lead + ((hd, 0), (hd, 0)))

    w1 = blockdiag(w1_k.reshape(CMP_BLOCK, hd, hd), w1_v.reshape(CMP_BLOCK, hd, hd))
    pe = jnp.concatenate([pe_k, pe_v], axis=1)
    half = CMP_STRIDE

    spread = np.zeros((NSA_GROUP * hd, NSA_GROUP * LANES), np.float32)
    gather = np.zeros((NSA_GROUP * LANES, NSA_GROUP * hd), np.float32)
    for h in range(NSA_GROUP):
        for d in range(hd):
            spread[h * hd + d, h * LANES + d] = 1.0
            gather[h * LANES + hd + d, h * hd + d] = 1.0
    gate_sel = np.zeros((NSA_KV_HEADS, SMALL_W, LANES), np.float32)
    for g in range(NSA_KV_HEADS):
        for j in range(3 * NSA_GROUP):
            gate_sel[g, SM_GATES + 3 * NSA_GROUP * g + j, j] = 1.0
    gate_rep = np.zeros((3, LANES, NSA_GROUP * hd), np.float32)
    for c in range(3):
        for h in range(NSA_GROUP):
            gate_rep[c, 3 * h + c, h * hd:(h + 1) * hd] = 1.0

    qi = np.arange(QUERY_BLOCK)[:, None]
    rel_s = qi + (SEL_CHUNK - QUERY_BLOCK) - np.arange(SEL_CHUNK)[None, :]
    rel_w = qi + WINDOW - np.arange(WIN_KEYS)[None, :]
    rel_c = ((np.arange(NQ)[:, None, None] * QUERY_BLOCK + qi[None])
             - (np.arange(ncp)[None, None, :] * CMP_STRIDE + CMP_BLOCK - 1))
    far_bias = rel_bias[int(_rel_bucket_table()[REL_MAX_DIST])]
    stab = _bias_tables(rel_bias, rel_s, rel_s >= 0, shift=far_bias)
    wtab = _bias_tables(rel_bias, rel_w, (rel_w >= 0) & (rel_w < WINDOW))
    ctab = _bias_tables(rel_bias, rel_c, rel_c >= 0)
    return dict(
        stab=stab, wtab=wtab, ctab=ctab,
        q_g=jnp.tile(q_g, NSA_GROUP).reshape(1, -1),
        kg_cmp=jnp.concatenate([k_g[0], ones]).reshape(1, LANES),
        kg_sel=jnp.concatenate([k_g[1], ones]).reshape(1, LANES),
        kg_win=jnp.concatenate([k_g[2], ones]).reshape(1, LANES),
        pe_a=pe[:half].reshape(1, half * LANES), pe_b=pe[half:].reshape(1, half * LANES),
        cw_a=w1[:half].reshape(half * LANES, LANES).astype(BF16),
        cw_b=w1[half:].reshape(half * LANES, LANES).astype(BF16),
        cw2=blockdiag(w2_k, w2_v).astype(BF16),
        q_spread=jnp.asarray(spread, BF16), gather=jnp.asarray(gather, BF16),
        gate_sel=jnp.asarray(gate_sel), gate_rep=jnp.asarray(gate_rep, BF16),
        sel_mt=jnp.asarray(_sel_to_cmp_matrix(T, ncp)),
    )


def kernel(x, c, w_ada, b_ada, norm1_g, norm2_g, w_in, rwkv_mu, rwkv_w0, rwkv_w2, rwkv_a0, rwkv_a2, rwkv_g2, rwkv_k_k, rwkv_k_a, rwkv_r_k, rwkv_ln_w, rwkv_ln_b, cmp_pe_k, cmp_w1_k, cmp_w2_k, cmp_pe_v, cmp_w1_v, cmp_w2_v, q_norm_g, k_norm_g, rel_bias, w_o_rwkv, w_o_nsa, w_out, w_up, w_down):
    B, T, D = x.shape
    depth = w_in.shape[0]
    x2 = x.reshape(B * T, D)
    for l in range(depth):
        mod6 = _ada_mod(c, w_ada[l], b_ada[l]).reshape(B * 6, 1, D)
        proj, small = _in_proj(x2, norm1_g[l].reshape(1, D), mod6, _prep_in_proj_weight(w_in, l), T)
        pr = _prep_rwkv_params(rwkv_mu[l], rwkv_w0[l], rwkv_w2[l], rwkv_a0[l], rwkv_a2[l], rwkv_g2[l],
                               rwkv_k_k[l], rwkv_k_a[l], rwkv_r_k[l], rwkv_ln_w[l], rwkv_ln_b[l])
        o_a = _rwkv_mix(proj, small, pr, B, T)
        pn = _prep_nsa_params(cmp_pe_k[l], cmp_w1_k[l], cmp_w2_k[l], cmp_pe_v[l], cmp_w1_v[l], cmp_w2_v[l],
                              q_norm_g[l], k_norm_g[l], rel_bias, T)
        qp, ks, vs, kw, vw, gates = _nsa_prep(proj, small, pn, B, T)
        kc, vc = _nsa_compress(proj, pn, B, T)
        o_b = _nsa_attn(qp, kc, vc, ks, vs, kw, vw, gates, pn, B, T)
        mixed = _merge(o_a, o_b, w_o_rwkv[l].astype(BF16), w_o_nsa[l].astype(BF16), proj)
        x1, h2 = _out_proj(mixed, w_out[l].astype(BF16), x2, mod6, norm2_g[l].reshape(1, D), T)
        x2 = _mlp(h2, w_up[l].astype(BF16), w_down[l].astype(BF16), x1, mod6, T)
    return x2.reshape(B, T, D)
```

```python
import functools
import math

import numpy as np
import jax
import jax.numpy as jnp
from jax import lax
from jax.experimental import pallas as pl
from jax.experimental.pallas import tpu as pltpu

F32 = jnp.float32
BF16 = jnp.bfloat16
HIGHEST = lax.Precision.HIGHEST

D_MODEL = 2048
HEAD_DIM = 64
RWKV_WIDTH = D_MODEL // 2
DECAY_LORA = 64
ICLR_LORA = 64
GATE_LORA = 160
RWKV_GN_EPS = 64e-5
NSA_WIDTH = D_MODEL // 2
NSA_HEADS = NSA_WIDTH // HEAD_DIM
NSA_KV_HEADS = 4
NSA_GROUP = NSA_HEADS // NSA_KV_HEADS
NSA_KV_WIDTH = NSA_KV_HEADS * HEAD_DIM
CMP_BLOCK = 32
CMP_STRIDE = 16
SEL_BLOCK = 64
N_SEL = 8
WINDOW = 512
QUERY_BLOCK = 128
REL_BUCKETS = 32
REL_MAX_DIST = 128
D_FF = 4 * D_MODEL
NORM_EPS = 1e-6
NEG_INF = -1e30
FORCE_SCORE = 1e4

RWKV_COLS = 3 * RWKV_WIDTH + DECAY_LORA + ICLR_LORA + GATE_LORA
NSA_COLS = NSA_WIDTH + 6 * NSA_KV_WIDTH + 3 * NSA_HEADS

V7X_VMEM_BYTES = 64 * 1024 * 1024
LANES = 128

COL_RKV = 0
COL_Q = 3 * RWKV_WIDTH
COL_MERGE = COL_Q + NSA_WIDTH
COL_KV = COL_MERGE + 2 * D_MODEL
COL_SMALL = COL_KV + 6 * NSA_KV_WIDTH
SMALL_W = 512
PROJ_COLS = COL_SMALL + SMALL_W
SM_XG = DECAY_LORA + ICLR_LORA
SM_GATES = SM_XG + GATE_LORA


def _vmem_limit(nbytes):
    return int(min(nbytes * 5 // 4 + (4 << 20), V7X_VMEM_BYTES - (8 << 20)))


def _cparams(sem, vmem_bytes):
    return pltpu.CompilerParams(dimension_semantics=sem, vmem_limit_bytes=_vmem_limit(vmem_bytes))


def _ada_kernel(c_ref, w_ref, b_ref, o_ref):
    c = c_ref[...]
    s = c * jax.nn.sigmoid(c)
    o_ref[...] = jnp.dot(s.astype(BF16), w_ref[...].astype(BF16), preferred_element_type=F32) + b_ref[...]


def _ada_mod(c, w_ada, b_ada):
    B, D = c.shape
    N = w_ada.shape[1]
    tn = 1024
    return pl.pallas_call(
        _ada_kernel,
        grid=(N // tn,),
        in_specs=[
            pl.BlockSpec((B, D), lambda j: (0, 0)),
            pl.BlockSpec((D, tn), lambda j: (0, j)),
            pl.BlockSpec((1, tn), lambda j: (0, j)),
        ],
        out_specs=pl.BlockSpec((B, tn), lambda j: (0, j)),
        out_shape=jax.ShapeDtypeStruct((B, N), F32),
        compiler_params=_cparams(("parallel",), 2 * D * tn * 4 + D * tn * 2),
        name="ada_mod",
    )(c, w_ada, b_ada.reshape(1, N))


def _modulated_norm(x, g, sc, sh):
    ms = jnp.mean(x * x, axis=-1, keepdims=True)
    return (x * lax.rsqrt(ms + NORM_EPS) * g) * (1.0 + sc) + sh


def _inproj_kernel(x_ref, g_ref, sh_ref, sc_ref, w_ref, o_ref, small_ref, h_scr):
    j = pl.program_id(1)

    @pl.when(j == 0)
    def _():
        h_scr[...] = _modulated_norm(x_ref[...], g_ref[...], sc_ref[0], sh_ref[0]).astype(BF16)

    y = lax.dot_general(h_scr[...], w_ref[...], (((1,), (1,)), ((), ())),
                        preferred_element_type=F32)
    o_ref[...] = y.astype(BF16)

    @pl.when(j == pl.num_programs(1) - 1)
    def _():
        small_ref[...] = y[:, y.shape[1] - SMALL_W:]


def _in_proj(x2, g1, mod6, w_in_t, T):
    BT, D = x2.shape
    NP = w_in_t.shape[0]
    tm = min(1024, T)
    tn = 1024
    tpb = T // tm
    assert NP - SMALL_W == COL_SMALL and tn >= SMALL_W
    vm = (2 * tm * D * 4 + tm * D * 2 + 2 * D * tn * 2 + 2 * tm * tn * 2 + 2 * tm * SMALL_W * 4
          + tm * tn * 4 + 2 * tm * D * 4)
    return pl.pallas_call(
        _inproj_kernel,
        grid=(BT // tm, NP // tn),
        in_specs=[
            pl.BlockSpec((tm, D), lambda i, j: (i, 0)),
            pl.BlockSpec((1, D), lambda i, j: (0, 0)),
            pl.BlockSpec((1, 1, D), lambda i, j: ((i // tpb) * 6 + 0, 0, 0)),
            pl.BlockSpec((1, 1, D), lambda i, j: ((i // tpb) * 6 + 1, 0, 0)),
            pl.BlockSpec((tn, D), lambda i, j: (j, 0)),
        ],
        out_specs=[pl.BlockSpec((tm, tn), lambda i, j: (i, j)), pl.BlockSpec((tm, SMALL_W), lambda i, j: (i, 0))],
        out_shape=[jax.ShapeDtypeStruct((BT, NP), BF16), jax.ShapeDtypeStruct((BT, SMALL_W), F32)],
        scratch_shapes=[pltpu.VMEM((tm, D), BF16)],
        compiler_params=_cparams(("parallel", "arbitrary"), vm),
        name="in_proj",
    )(x2, g1, mod6, mod6, w_in_t)


def _merge_kernel(oa_ref, ob_ref, wa_ref, wb_ref, ga_ref, gb_ref, o_ref):
    ya = jnp.dot(oa_ref[...], wa_ref[...], preferred_element_type=F32)
    yb = jnp.dot(ob_ref[...], wb_ref[...], preferred_element_type=F32)
    ga, gb = ga_ref[...].astype(F32), gb_ref[...].astype(F32)
    o_ref[...] = (jax.nn.sigmoid(ga) * ya + jax.nn.sigmoid(gb) * yb).astype(BF16)


def _merge(o_a, o_b, w_oa, w_ob, proj):
    BT, W = o_a.shape
    D = w_oa.shape[1]
    tm, tn = 512, 1024
    ga0 = COL_MERGE // tn
    gb0 = (COL_MERGE + D) // tn
    vm = 2 * (2 * tm * W * 2 + 2 * W * tn * 2 + 2 * tm * tn * 4 + tm * tn * 2) + 3 * tm * tn * 4
    return pl.pallas_call(
        _merge_kernel,
        grid=(BT // tm, D // tn),
        in_specs=[
            pl.BlockSpec((tm, W), lambda i, j: (i, 0)),
            pl.BlockSpec((tm, W), lambda i, j: (i, 0)),
            pl.BlockSpec((W, tn), lambda i, j: (0, j)),
            pl.BlockSpec((W, tn), lambda i, j: (0, j)),
            pl.BlockSpec((tm, tn), lambda i, j: (i, ga0 + j)),
            pl.BlockSpec((tm, tn), lambda i, j: (i, gb0 + j)),
        ],
        out_specs=pl.BlockSpec((tm, tn), lambda i, j: (i, j)),
        out_shape=jax.ShapeDtypeStruct((BT, D), BF16),
        compiler_params=_cparams(("parallel", "parallel"), vm),
        name="merge",
    )(o_a, o_b, w_oa, w_ob, proj, proj)


def _outproj_kernel(m_ref, w_ref, x_ref, gt_ref, g_ref, sh_ref, sc_ref, x1_ref, h2_ref):
    y = jnp.dot(m_ref[...], w_ref[...], preferred_element_type=F32)
    x1 = x_ref[...] + gt_ref[0] * y
    x1_ref[...] = x1
    h2_ref[...] = _modulated_norm(x1, g_ref[...], sc_ref[0], sh_ref[0]).astype(BF16)


def _out_proj(mixed, w_out, x2, mod6, g2, T):
    BT, D = x2.shape
    tm = min(512, T)
    tpb = T // tm
    vm = 2 * (tm * D * 2 + D * D * 2 + tm * D * 4 + tm * D * 4 + tm * D * 2) + 3 * tm * D * 4
    mod_spec = lambda k: pl.BlockSpec((1, 1, D), lambda i: ((i // tpb) * 6 + k, 0, 0))
    return pl.pallas_call(
        _outproj_kernel,
        grid=(BT // tm,),
        in_specs=[
            pl.BlockSpec((tm, D), lambda i: (i, 0)),
            pl.BlockSpec((D, D), lambda i: (0, 0)),
            pl.BlockSpec((tm, D), lambda i: (i, 0)),
            mod_spec(2),
            pl.BlockSpec((1, D), lambda i: (0, 0)),
            mod_spec(3),
            mod_spec(4),
        ],
        out_specs=[pl.BlockSpec((tm, D), lambda i: (i, 0)), pl.BlockSpec((tm, D), lambda i: (i, 0))],
        out_shape=[jax.ShapeDtypeStruct((BT, D), F32), jax.ShapeDtypeStruct((BT, D), BF16)],
        compiler_params=_cparams(("parallel",), vm),
        name="out_proj",
    )(mixed, w_out, x2, mod6, g2, mod6, mod6)


OUT_CHUNK = 256


def _mlp_kernel(h_ref, wu_ref, wd_ref, x_ref, gt_ref, o_ref, acc_ref):
    f = pl.program_id(1)

    @pl.when(f == 0)
    def _():
        acc_ref[...] = jnp.zeros_like(acc_ref)

    u = jnp.dot(h_ref[...], wu_ref[...], preferred_element_type=F32)
    u = jnp.square(jnp.maximum(u, 0.0)).astype(BF16)
    for n in range(acc_ref.shape[1] // OUT_CHUNK):
        cs = slice(n * OUT_CHUNK, (n + 1) * OUT_CHUNK)
        acc_ref[:, cs] += jnp.dot(u, wd_ref[:, cs], preferred_element_type=F32)

    @pl.when(f == pl.num_programs(1) - 1)
    def _():
        o_ref[...] = x_ref[...] + gt_ref[0] * acc_ref[...]


def _mlp(h2, w_up, w_down, x1, mod6, T):
    BT, D = x1.shape
    F = w_up.shape[1]
    tm = min(512, T)
    tf = 1024
    tpb = T // tm
    vm = 2 * (tm * D * 2 + 2 * D * tf * 2 + 2 * tm * D * 4) + tm * D * 4 + 2 * tm * tf * 4
    return pl.pallas_call(
        _mlp_kernel,
        grid=(BT // tm, F // tf),
        in_specs=[
            pl.BlockSpec((tm, D), lambda i, f: (i, 0)),
            pl.BlockSpec((D, tf), lambda i, f: (0, f)),
            pl.BlockSpec((tf, D), lambda i, f: (f, 0)),
            pl.BlockSpec((tm, D), lambda i, f: (i, 0)),
            pl.BlockSpec((1, 1, D), lambda i, f: ((i // tpb) * 6 + 5, 0, 0)),
        ],
        out_specs=pl.BlockSpec((tm, D), lambda i, f: (i, 0)),
        out_shape=jax.ShapeDtypeStruct((BT, D), F32),
        scratch_shapes=[pltpu.VMEM((tm, D), F32)],
        compiler_params=_cparams(("parallel", "arbitrary"), vm),
        name="mlp",
    )(h2, w_up, w_down, x1, mod6)


def _dotb(a, b):
    return jnp.dot(a.astype(BF16), b.astype(BF16), preferred_element_type=F32)


def _dotb_nt(a, b):
    return lax.dot_general(a.astype(BF16), b.astype(BF16), (((1,), (1,)), ((), ())),
                           preferred_element_type=F32)


def _split_bf16(x, terms):
    parts, rem = [], x
    for t in range(terms):
        p = rem.astype(BF16)
        parts.append(p)
        if t + 1 < terms:
            rem = rem - p.astype(F32)
    return parts


def _dot_sel(x, sel, terms):
    sel = sel.astype(BF16)
    return sum(jnp.dot(p, sel, preferred_element_type=F32) for p in _split_bf16(x, terms))


def _sel_dot(sel, x, terms):
    sel = sel.astype(BF16)
    return sum(jnp.dot(sel, p, preferred_element_type=F32) for p in _split_bf16(x, terms))


def _dot3(a, b_hi, b_lo):
    a_hi, a_lo = _split_bf16(a, 2)
    return (jnp.dot(a_hi, b_hi, preferred_element_type=F32) + jnp.dot(a_lo, b_hi, preferred_element_type=F32)
            + jnp.dot(a_hi, b_lo, preferred_element_type=F32))


def _iota(shape, axis):
    return lax.broadcasted_iota(jnp.int32, shape, axis)


def _same_head_mask(n):
    return (_iota((n, n), 0) >> 6) == (_iota((n, n), 1) >> 6)


RW_TC = 256
RW_C = 64
RW_LANES = 4 * HEAD_DIM


def _block_diag(x, bdmask):
    return jnp.where(bdmask, jnp.concatenate([x, x, x, x], axis=0), 0.0)


def _rwkv_kernel(r_ref, k_ref, v_ref, s_ref, mur_ref, muk_ref, muv_ref, mus_ref,
                 w0_ref, a0_ref, kk_ref, ka_ref, rk_ref, lnw_ref, lnb_ref,
                 w2h_ref, w2l_ref, a2_ref, g2_ref, o_ref,
                 pr_scr, pk_scr, pv_scr, ps_scr, state_scr):
    TC = r_ref.shape[0]
    C = RW_C
    W = RW_LANES

    @pl.when(pl.program_id(1) == 0)
    def _():
        pr_scr[...] = jnp.zeros_like(pr_scr)
        pk_scr[...] = jnp.zeros_like(pk_scr)
        pv_scr[...] = jnp.zeros_like(pv_scr)
        ps_scr[...] = jnp.zeros_like(ps_scr)
        state_scr[...] = jnp.zeros_like(state_scr)

    def shift_mix(p_ref, prev_scr, mu_ref):
        p = p_ref[...].astype(F32)
        rolled = pltpu.roll(p, 1, 0)
        first = jnp.where(_iota((8, p.shape[1]), 0) == 0, prev_scr[0:1, :], rolled[0:8])
        shifted = jnp.concatenate([first, rolled[8:]], axis=0)
        prev_scr[0:1, :] = p[TC - 1:TC, :]
        return p + (shifted - p) * mu_ref[...]

    CW = r_ref.shape[1]
    G = CW // W
    NQ = TC // C
    groups = lambda x: [x[:, g * W:(g + 1) * W] for g in range(G)]
    per_group = lambda f, x: jnp.concatenate([f(xg) for xg in groups(x)], axis=1)

    r = shift_mix(r_ref, pr_scr, mur_ref)
    k = shift_mix(k_ref, pk_scr, muk_ref)
    v = shift_mix(v_ref, pv_scr, muv_ref)
    sm = shift_mix(s_ref, ps_scr, mus_ref)
    xwa = sm[:, 0:SM_XG]

    bdmask = _same_head_mask(W)
    bones = jnp.where(bdmask, 1.0, 0.0)
    head_sum = lambda x: per_group(lambda xg: _dot_sel(xg, bones, 1), x)
    tri = jnp.where(_same_head_mask(TC) & (_iota((TC, TC), 1) <= _iota((TC, TC), 0)), 1.0, 0.0)
    lane = _iota((C, W), 1)
    row = _iota((C, W), 0)
    lanehead = lane >> 6
    strict = (lane & 63) < row
    incl = (lane & 63) <= row
    eye_cat = jnp.where((lane & 63) == row, 1.0, 0.0)

    wlin = w0_ref[...] + _dot3(jnp.tanh(xwa), w2h_ref[...], w2l_ref[...])
    a = jax.nn.sigmoid(a0_ref[...] + _dotb(xwa, a2_ref[...]))
    gate = _dotb(jax.nn.sigmoid(sm[:, SM_XG:SM_XG + 256]), g2_ref[...])
    z = -wlin
    softplus = jnp.maximum(z, 0.0) + jnp.log(1.0 + jnp.exp(-jnp.abs(z)))
    ld = -jnp.exp(-softplus - 0.5)
    cum = _sel_dot(tri, ld, 3)
    cum_last = jnp.concatenate(
        [jnp.broadcast_to(cum[(q + 1) * C - 1:(q + 1) * C, :], (C, CW)) for q in range(NQ)], axis=0)
    kk = k * kk_ref[...]
    kk = kk * lax.rsqrt(jnp.maximum(head_sum(kk * kk), 1e-24))
    k2 = k * (1.0 + (a - 1.0) * ka_ref[...])
    bvec = kk * a
    e_inv = jnp.exp(-cum)
    e_end = jnp.exp(cum_last - cum)
    At_f = -kk * jnp.exp(cum - ld)
    Rt_f = r * jnp.exp(cum)
    Bt_f = bvec * e_inv
    Kt_f = k2 * e_inv
    Bg_f = bvec * e_end
    Kg_f = k2 * e_end
    g_end = jnp.exp(cum_last)

    items = [(g, q) for q in range(NQ) for g in range(G)]
    blk = lambda x, g, q: x[q * C:(q + 1) * C, g * W:(g + 1) * W]
    bd01 = jnp.where(bdmask, 1.0, 0.0).astype(BF16)

    def bd(x):
        xb = x.astype(BF16)
        return jnp.concatenate([xb, xb, xb, xb], axis=0) * bd01

    At = {it: blk(At_f, *it) for it in items}
    Rt = {it: blk(Rt_f, *it) for it in items}
    Vq = {it: blk(v, *it) for it in items}
    AA = {}
    for it in items:
        bk = jnp.concatenate([jnp.where(lanehead == h, X, 0.0)
                              for X in (blk(Bt_f, *it), blk(Kt_f, *it)) for h in range(4)], axis=0)
        AA[it] = _dotb_nt(jnp.concatenate([At[it], Rt[it]], axis=0), bk)
    A_ab = {it: jnp.where(strict, AA[it][0:C, 0:W], 0.0) for it in items}
    A_ak = {it: jnp.where(strict, AA[it][0:C, W:2 * W], 0.0) for it in items}
    A_rb = {it: jnp.where(incl, AA[it][C:2 * C, 0:W], 0.0) for it in items}
    A_rk = {it: jnp.where(incl, AA[it][C:2 * C, W:2 * W], 0.0) for it in items}
    M = dict(A_ab)
    Tm = {it: eye_cat + A_ab[it] for it in items}
    for _ in range(5):
        M = {it: _dotb(M[it], bd(M[it])) for it in items}
        Tm = {it: Tm[it] + _dotb(M[it], bd(Tm[it])) for it in items}
    Vbd = {it: bd(Vq[it]) for it in items}
    akv = {it: _dotb(A_ak[it], Vbd[it]) for it in items}
    rkv = {it: _dotb(A_rk[it], Vbd[it]) for it in items}

    S = [state_scr[g] for g in range(G)]
    ys = []
    for q in range(NQ):
        its = [(g, q) for g in range(G)]
        rhs = [_dotb_nt(At[it], S[it[0]]) + akv[it] for it in its]
        U = [_dotb(Tm[it], bd(rhs[g])) for g, it in enumerate(its)]
        ys.append(jnp.concatenate(
            [_dotb_nt(Rt[it], S[g]) + _dotb(A_rb[it], bd(U[g])) + rkv[it] for g, it in enumerate(its)], axis=1))
        upd = [_dotb(jnp.concatenate([U[g], Vq[it]], axis=0).T,
                     jnp.concatenate([blk(Bg_f, *it), blk(Kg_f, *it)], axis=0)) for g, it in enumerate(its)]
        S = [S[g] * blk(g_end, g, q)[0:1, :] + jnp.where(bdmask, upd[g], 0.0) for g in range(G)]
    for g in range(G):
        state_scr[g] = S[g]
    y = jnp.concatenate(ys, axis=0)

    inv_n = 1.0 / HEAD_DIM
    d = y - head_sum(y) * inv_n
    var = head_sum(d * d) * inv_n
    yn = d * lax.rsqrt(var + RWKV_GN_EPS) * lnw_ref[...] + lnb_ref[...]
    bonus = head_sum(r * k2 * rk_ref[...]) * v
    o_ref[...] = ((yn + bonus) * gate).astype(BF16)


def _rwkv_mix(proj, small, pr, B, T):
    BT = proj.shape[0]
    TC = min(RW_TC, T)
    W = RW_LANES
    CW = RWKV_WIDTH
    nct = T // TC
    row = lambda b, c: b * nct + c
    full = lambda shape: pl.BlockSpec(shape, lambda b, c: (0, 0))
    vec = full((1, CW))
    in_specs = [
        pl.BlockSpec((TC, CW), lambda b, c: (row(b, c), 0)),
        pl.BlockSpec((TC, CW), lambda b, c: (row(b, c), 1)),
        pl.BlockSpec((TC, CW), lambda b, c: (row(b, c), 2)),
        pl.BlockSpec((TC, SMALL_W), lambda b, c: (row(b, c), 0)),
        vec, vec, vec,
        full((1, SMALL_W)),
        vec, vec, vec, vec, vec, vec, vec,
        full((SM_XG, CW)), full((SM_XG, CW)), full((SM_XG, CW)), full((256, CW)),
    ]
    vm = (2 * (3 * TC * CW * 4 + TC * SMALL_W * 4 + TC * CW * 2 + (3 * SM_XG + 256) * CW * 2)
          + 24 * TC * CW * 4)
    return pl.pallas_call(
        _rwkv_kernel,
        grid=(B, nct),
        in_specs=in_specs,
        out_specs=pl.BlockSpec((TC, CW), lambda b, c: (row(b, c), 0)),
        out_shape=jax.ShapeDtypeStruct((BT, CW), BF16),
        scratch_shapes=[pltpu.VMEM((8, CW), F32), pltpu.VMEM((8, CW), F32), pltpu.VMEM((8, CW), F32),
                        pltpu.VMEM((8, SMALL_W), F32), pltpu.VMEM((CW // W, W, W), F32)],
        compiler_params=_cparams(("parallel", "arbitrary"), vm),
        name="rwkv_mix",
    )(proj, proj, proj, small, pr["mu_r"], pr["mu_k"], pr["mu_v"], pr["mu_s"],
      pr["w0"], pr["a0"], pr["k_k"], pr["k_a"], pr["r_k"], pr["ln_w"], pr["ln_b"],
      pr["w2h"], pr["w2l"], pr["a2"], pr["g2"])


NSA_TT = 256
BLOCK_LANE0 = HEAD_DIM
PADFLAG_LANE = BLOCK_LANE0 + 32


def _nsa_prep_kernel(q_ref, kv0_ref, kv1_ref, kv2_ref, s_ref, qg_ref, kgs_ref, kgw_ref, e_ref, gsel_ref,
                     qp_ref, ksl_ref, vsl_ref, kwl_ref, vwl_ref, gt_ref):
    tt = q_ref.shape[0]
    n_pad = NSA_KPAD // tt
    step = pl.program_id(1)
    lane = _iota((tt, LANES), 1)

    @pl.when(step < n_pad)
    def _():
        flag = jnp.broadcast_to(jnp.where(lane == PADFLAG_LANE, 1.0, 0.0).astype(BF16), ksl_ref.shape[1:])
        ksl_ref[0] = flag
        kwl_ref[0] = flag
        vsl_ref[0] = jnp.zeros_like(flag)
        vwl_ref[0] = jnp.zeros_like(flag)

    @pl.when(step >= n_pad)
    def _():
        _nsa_prep_tile(step - n_pad, lane, q_ref, (kv0_ref, kv1_ref, kv2_ref), s_ref, qg_ref, kgs_ref, kgw_ref,
                       e_ref, gsel_ref, qp_ref, ksl_ref, vsl_ref, kwl_ref, vwl_ref, gt_ref)


def _nsa_prep_tile(tile, lane, q_ref, kv_refs, s_ref, qg_ref, kgs_ref, kgw_ref, e_ref, gsel_ref,
                   qp_ref, ksl_ref, vsl_ref, kwl_ref, vwl_ref, gt_ref):
    tt = q_ref.shape[0]
    QW = NSA_GROUP * HEAD_DIM
    bones = jnp.where(_same_head_mask(QW), 1.0, 0.0)
    bones2 = jnp.where(_same_head_mask(LANES), 1.0, 0.0)
    block_id = (tile * tt + _iota((tt, LANES), 0)) >> 6
    onehot = jnp.where(lane == block_id + BLOCK_LANE0, 1.0, 0.0)
    small = _split_bf16(s_ref[...], 3)

    def pair(g, branch):
        off = (3 * g + branch) * LANES
        return kv_refs[off // SMALL_W][:, off % SMALL_W:off % SMALL_W + LANES].astype(F32)

    def slabs(g, x, gain, k_ref, v_ref):
        ms = _dot_sel(x * x, bones2, 1) * (1.0 / HEAD_DIM)
        k_ref[0, g] = jnp.where(lane < HEAD_DIM, x * lax.rsqrt(ms + NORM_EPS) * gain, onehot).astype(BF16)
        v_ref[0, g] = jnp.where(lane >= HEAD_DIM, x, 1.0).astype(BF16)

    for g in range(NSA_KV_HEADS):
        q = q_ref[:, g * QW:(g + 1) * QW].astype(F32)
        ms = _dot_sel(q * q, bones, 1) * (1.0 / HEAD_DIM)
        qn = (q * lax.rsqrt(ms + NORM_EPS) * qg_ref[...]) * (HEAD_DIM ** -0.5)
        qs = jnp.dot(qn.astype(BF16), e_ref[...], preferred_element_type=F32).astype(BF16)
        for h in range(NSA_GROUP):
            qp_ref[0, NSA_GROUP * g + h] = qs[:, h * LANES:(h + 1) * LANES]
        slabs(g, pair(g, 1), kgs_ref[...], ksl_ref, vsl_ref)
        slabs(g, pair(g, 2), kgw_ref[...], kwl_ref, vwl_ref)
        sel = gsel_ref[g].astype(BF16)
        gt_ref[0, g] = jax.nn.sigmoid(sum(jnp.dot(part, sel, preferred_element_type=F32) for part in small))


def _nsa_prep(proj, small, pn, B, T):
    tt = min(NSA_TT, T)
    ntt = T // tt
    G = NSA_KV_HEADS
    QW = NSA_GROUP * HEAD_DIM
    n_pad = NSA_KPAD // tt
    tile = lambda t: jnp.maximum(t - n_pad, 0)
    row = lambda b, t: b * ntt + tile(t)
    full = lambda shape: pl.BlockSpec(shape, lambda b, t: tuple(0 for _ in shape))
    kv_spec = lambda j: pl.BlockSpec((tt, SMALL_W), lambda b, t: (row(b, t), COL_KV // SMALL_W + j))
    slab = pl.BlockSpec((1, G, tt, LANES), lambda b, t: (b, 0, t, 0))
    vm = 2 * (tt * NSA_WIDTH * 4 + 4 * tt * SMALL_W * 4 + QW * 4 * LANES * 2 + G * SMALL_W * LANES * 4
              + 16 * tt * LANES * 2 + 16 * tt * LANES * 2 + 4 * tt * LANES * 4) + 24 * tt * QW * 4
    return pl.pallas_call(
        _nsa_prep_kernel,
        grid=(B, ntt + n_pad),
        in_specs=[
            pl.BlockSpec((tt, NSA_WIDTH), lambda b, t: (row(b, t), COL_Q // NSA_WIDTH)),
            kv_spec(0), kv_spec(1), kv_spec(2),
            pl.BlockSpec((tt, SMALL_W), lambda b, t: (row(b, t), 0)),
            full((1, QW)), full((1, LANES)), full((1, LANES)), full((QW, NSA_GROUP * LANES)),
            full((G, SMALL_W, LANES)),
        ],
        out_specs=[pl.BlockSpec((1, NSA_HEADS, tt, LANES), lambda b, t: (b, 0, tile(t), 0)),
                   slab, slab, slab, slab,
                   pl.BlockSpec((1, G, tt, LANES), lambda b, t: (b, 0, tile(t), 0))],
        out_shape=[jax.ShapeDtypeStruct((B, NSA_HEADS, T, LANES), BF16)]
        + [jax.ShapeDtypeStruct((B, G, T + NSA_KPAD, LANES), BF16)] * 4
        + [jax.ShapeDtypeStruct((B, G, T, LANES), F32)],
        compiler_params=_cparams(("parallel", "arbitrary"), vm),
        name="nsa_prep",
    )(proj, proj, proj, proj, small, pn["q_g"], pn["kg_sel"], pn["kg_win"], pn["q_spread"], pn["gate_sel"])


def _gelu_tanh(x):
    return 0.5 * x * (1.0 + jnp.tanh(math.sqrt(2.0 / math.pi) * (x + 0.044715 * (x * x * x))))


def _nsa_compress_kernel(x_ref, pea_ref, peb_ref, wa_ref, wb_ref, w2_ref, kg_ref, kc_ref, vc_ref, x_scr):
    nsub = x_ref.shape[0] // CMP_STRIDE
    x_scr[...] = x_ref[...].astype(F32)
    xs = jnp.concatenate([x_scr[pl.ds(s, nsub, stride=CMP_STRIDE), :] for s in range(CMP_STRIDE)], axis=1)
    p0 = jnp.dot((xs + pea_ref[...]).astype(BF16), wa_ref[...], preferred_element_type=F32)
    p1 = jnp.dot((xs + peb_ref[...]).astype(BF16), wb_ref[...], preferred_element_type=F32)
    hid = _gelu_tanh(p0 + pltpu.roll(p1, nsub - 1, 0))
    out = jnp.dot(hid.astype(BF16), w2_ref[...], preferred_element_type=F32)
    bones2 = jnp.where(_same_head_mask(LANES), 1.0, 0.0)
    ms = _dot_sel(out * out, bones2, 2) * (1.0 / HEAD_DIM)
    is_k = _iota(out.shape, 1) < HEAD_DIM
    kc_ref[0, 0] = jnp.where(is_k, out * lax.rsqrt(ms + NORM_EPS) * kg_ref[...], 0.0).astype(BF16)
    vc_ref[0, 0] = jnp.where(is_k, 0.0, out).astype(BF16)


def _nsa_compress(proj, pn, B, T):
    G = NSA_KV_HEADS
    nsub = T // CMP_STRIDE
    kv_blk = COL_KV // LANES
    KW = CMP_STRIDE * LANES
    full = lambda shape: pl.BlockSpec(shape, lambda b, g: tuple(0 for _ in shape))
    vm = 2 * (T * LANES * 4 + 2 * KW * LANES * 2 + nsub * LANES * 2) + 6 * nsub * KW * 4
    return pl.pallas_call(
        _nsa_compress_kernel,
        grid=(B, G),
        in_specs=[
            pl.BlockSpec((T, LANES), lambda b, g: (b, kv_blk + 3 * g)),
            full((1, KW)), full((1, KW)), full((KW, LANES)), full((KW, LANES)), full((LANES, LANES)),
            full((1, LANES)),
        ],
        out_specs=[pl.BlockSpec((1, 1, nsub, LANES), lambda b, g: (b, g, 0, 0))] * 2,
        out_shape=[jax.ShapeDtypeStruct((B, G, nsub, LANES), BF16)] * 2,
        scratch_shapes=[pltpu.VMEM((T, LANES), F32)],
        compiler_params=_cparams(("parallel", "parallel"), vm),
        name="nsa_compress",
    )(proj, pn["pe_a"], pn["pe_b"], pn["cw_a"], pn["cw_b"], pn["cw2"], pn["kg_cmp"])


NSA_GROUPS_PER_STEP = 4
NSA_KPAD = WINDOW
SEL_CHUNK = 4 * QUERY_BLOCK
WIN_KEYS = WINDOW + QUERY_BLOCK


def _lane_tile_max(s):
    tiles = [s[:, j * LANES:(j + 1) * LANES] for j in range(s.shape[1] // LANES)]
    while len(tiles) > 1:
        tiles = [jnp.maximum(a, b) for a, b in zip(tiles[0::2], tiles[1::2])] + ([tiles[-1]] if len(tiles) % 2 else [])
    return tiles[0]


def _nsa_attn_kernel(q_ref, kc_ref, vc_ref, ks_ref, vs_ref, kw_ref, vw_ref, gt_ref, stab_ref, wtab_ref, ctab_ref,
                     selmt_ref, gather_ref, grep_ref, o_ref, sbuf, mx_scr, acc_scr, sc_scr, *, n_sel):
    i = pl.program_id(2)
    QB = QUERY_BLOCK
    HG = NSA_GROUP
    R = HG * QB
    t0 = i * QB
    groups = range(kc_ref.shape[1])
    each = lambda f: [f(g) for g in groups]
    tile4 = lambda z: jnp.concatenate([z, z, z, z], axis=0)
    heads = lambda ref, g: ref[HG * g:HG * (g + 1)]
    lane = _iota((QB, LANES), 1)
    pad_mask = jnp.where(lane == PADFLAG_LANE, NEG_INF, 0.0)
    q = each(lambda g: q_ref[0, HG * g:HG * (g + 1)].reshape(R, LANES))
    q32 = each(lambda g: q[g].astype(F32))

    ncp = kc_ref.shape[2]
    s = each(lambda g: _dotb_nt(q[g], kc_ref[0, g]) + heads(ctab_ref, g)[:, 0].reshape(R, ncp))
    m = each(lambda g: jnp.max(s[g], axis=-1, keepdims=True))
    p_c = each(lambda g: jnp.exp(s[g] - m[g]))
    lsum = each(lambda g: jnp.sum(p_c[g], axis=-1, keepdims=True))
    p_c = each(lambda g: p_c[g] * jnp.where(m[g] > 0.5 * NEG_INF, 1.0 / lsum[g], 0.0))
    o_c = each(lambda g: jnp.dot(p_c[g].astype(BF16), vc_ref[0, g], preferred_element_type=F32))

    wrows = pl.ds(pl.multiple_of(t0, QB), WIN_KEYS)
    q_win = each(lambda g: (q32[g] + tile4(pad_mask)).astype(BF16))
    s = each(lambda g: _dotb_nt(q_win[g], kw_ref[0, g, wrows, :]) + heads(wtab_ref, g).reshape(R, WIN_KEYS))
    p = each(lambda g: jnp.exp(s[g] - jnp.max(_lane_tile_max(s[g]), axis=-1, keepdims=True)))
    acc_w = each(lambda g: jnp.dot(p[g].astype(BF16), vw_ref[0, g, wrows, :], preferred_element_type=F32))

    psum = each(lambda g: p_c[g][0:QB] + p_c[g][QB:2 * QB] + p_c[g][2 * QB:3 * QB] + p_c[g][3 * QB:4 * QB])
    selmt = selmt_ref[...]
    parts = each(lambda g: _split_bf16(psum[g], 3))
    imp = each(lambda g: sum(lax.dot_general(selmt, part, (((1,), (1,)), ((), ())), preferred_element_type=F32)
                             for part in parts[g]))
    ns = selmt.shape[0]
    blk = _iota((ns, QB), 0)
    cur = (t0 + _iota((ns, QB), 1)) >> 6
    forced = (blk == 0) | (blk == cur) | (blk == cur - 1)
    score = each(lambda g: jnp.where(forced, FORCE_SCORE, jnp.where(blk <= cur, imp[g], -1.0)))
    for g in groups:
        sc_scr[g] = score[g]
    ranks = [[] for _ in groups]
    for j in range(ns):
        lower = jnp.where(blk > j, 1.0, 0.0)
        for g in groups:
            other = sc_scr[g, j:j + 1, :]
            ranks[g].append(jnp.where(other > score[g], 1.0, 0.0) + jnp.where(other == score[g], lower, 0.0))
    while len(ranks[0]) > 1:
        ranks = [[a + b for a, b in zip(r[0::2], r[1::2])] + ([r[-1]] if len(r) % 2 else []) for r in ranks]
    chosen_t = each(lambda g: jnp.where(ranks[g][0] < n_sel, 1.0, 0.0))
    zrows = lambda n: jnp.zeros((n, QB), F32)
    chosen = each(lambda g: jnp.concatenate([zrows(BLOCK_LANE0), chosen_t[g], zrows(LANES - BLOCK_LANE0 - ns)],
                                            axis=0).T)
    is_block_lane = (lane >= BLOCK_LANE0) & (lane < BLOCK_LANE0 + ns)
    q_sel = each(lambda g: (q32[g] + tile4(jnp.where(is_block_lane, (chosen[g] - 1.0) * (-NEG_INF), pad_mask))
                            ).astype(BF16))

    def chunk_rows(c):
        return pl.ds(pl.multiple_of((i - 4 * c + 1) * QB, QB), SEL_CHUNK)

    s0 = each(lambda g: _dotb_nt(q_sel[g], ks_ref[0, g, chunk_rows(0), :]) + heads(stab_ref, g).reshape(R, SEL_CHUNK))
    for g in groups:
        sbuf[g, 0] = s0[g]
        mx_scr[g] = _lane_tile_max(s0[g])

    n_far = i // 4
    max_far = sbuf.shape[1] - 1
    for k in range(1, max_far + 1):
        @pl.when(n_far == k)
        def _(k=k):
            for c in range(1, k + 1):
                s = each(lambda g: _dotb_nt(q_sel[g], ks_ref[0, g, chunk_rows(c), :]))
                for g in groups:
                    sbuf[g, c] = s[g]
                    mx_scr[g] = jnp.maximum(mx_scr[g], _lane_tile_max(s[g]))

    m_s = each(lambda g: jnp.max(mx_scr[g], axis=-1, keepdims=True))

    for k in range(max_far + 1):
        @pl.when(n_far == k)
        def _(k=k):
            for g in groups:
                acc_scr[g] = sum(jnp.dot(jnp.exp(sbuf[g, c] - m_s[g]).astype(BF16), vs_ref[0, g, chunk_rows(c), :],
                                         preferred_element_type=F32) for c in range(k + 1))

    is_value = _iota((R, LANES), 1) >= HEAD_DIM
    normalised = lambda acc: jnp.where(is_value, acc * (1.0 / pltpu.roll(acc, HEAD_DIM, 1)), 0.0)

    def natural(o):
        cat = jnp.concatenate([o[h * QB:(h + 1) * QB] for h in range(HG)], axis=1).astype(BF16)
        return jnp.dot(cat, gather_ref[...], preferred_element_type=F32)

    branches = each(lambda g: [o_c[g], normalised(acc_scr[g]), normalised(acc_w[g])])
    gate = each(lambda g: [_dot_sel(gt_ref[0, g], grep_ref[c], 1) for c in range(3)])
    W = HG * HEAD_DIM
    for g in groups:
        out = sum(gate[g][c] * natural(branches[g][c]) for c in range(3))
        o_ref[:, g * W:(g + 1) * W] = out.astype(BF16)


def _nsa_attn(qp, kc, vc, ks, vs, kw, vw, gates, pn, B, T):
    G = NSA_KV_HEADS
    QB = QUERY_BLOCK
    NQ = T // QB
    ncp = kc.shape[2]
    ns = T // SEL_BLOCK
    TP = T + NSA_KPAD
    R = NSA_GROUP * QB
    n_sel = min(N_SEL, ns)
    n_chunks = (NQ - 1) // 4 + 1
    NG = NSA_GROUPS_PER_STEP
    HB = NSA_GROUP * NG
    once = pl.Buffered(1)
    gtab = lambda w: pl.BlockSpec((HB, QB, w), lambda b, g, i: (g, 0, 0), pipeline_mode=once)
    slab = lambda rows: pl.BlockSpec((1, NG, rows, LANES), lambda b, g, i: (b, g, 0, 0), pipeline_mode=once)
    vm = (NG * (2 * ncp * LANES * 2 + 4 * TP * LANES * 2 + R * (SEL_CHUNK + WIN_KEYS) * 4)
          + 2 * NG * (R * LANES * 2 + QB * LANES * 4 + R * ncp * 4 + QB * NSA_GROUP * HEAD_DIM * 2)
          + NG * (n_chunks * R * SEL_CHUNK * 4 + 2 * R * LANES * 4 + 4 * R * WIN_KEYS * 4))
    return pl.pallas_call(
        functools.partial(_nsa_attn_kernel, n_sel=n_sel),
        grid=(B, G // NG, NQ),
        in_specs=[
            pl.BlockSpec((1, HB, QB, LANES), lambda b, g, i: (b, g, i, 0)),
            slab(ncp), slab(ncp), slab(TP), slab(TP), slab(TP), slab(TP),
            pl.BlockSpec((1, NG, QB, LANES), lambda b, g, i: (b, g, i, 0)),
            gtab(SEL_CHUNK), gtab(WIN_KEYS),
            pl.BlockSpec((HB, 1, QB, ncp), lambda b, g, i: (g, i, 0, 0)),
            pl.BlockSpec((ns, ncp), lambda b, g, i: (0, 0)),
            pl.BlockSpec((NSA_GROUP * LANES, NSA_GROUP * HEAD_DIM), lambda b, g, i: (0, 0)),
            pl.BlockSpec((3, LANES, NSA_GROUP * HEAD_DIM), lambda b, g, i: (0, 0, 0)),
        ],
        out_specs=pl.BlockSpec((QB, HB * HEAD_DIM), lambda b, g, i: (b * NQ + i, g)),
        out_shape=jax.ShapeDtypeStruct((B * T, NSA_WIDTH), BF16),
        scratch_shapes=[pltpu.VMEM((NG, n_chunks, R, SEL_CHUNK), F32), pltpu.VMEM((NG, R, LANES), F32),
                        pltpu.VMEM((NG, R, LANES), F32), pltpu.VMEM((NG, ns, QB), F32)],
        compiler_params=_cparams(("parallel", "parallel", "arbitrary"), vm),
        name="nsa_attn",
    )(qp, kc, vc, ks, vs, kw, vw, gates, pn["stab"], pn["wtab"], pn["ctab"], pn["sel_mt"], pn["gather"],
      pn["gate_rep"])


def _rel_bucket_table():
    n = np.arange(REL_MAX_DIST + 1)
    max_exact = REL_BUCKETS // 2
    nf = np.maximum(n, max_exact).astype(np.float32)
    large = max_exact + (np.log(nf / np.float32(max_exact)) / np.float32(math.log(REL_MAX_DIST / max_exact))
                         * np.float32(REL_BUCKETS - max_exact)).astype(np.int32)
    large = np.minimum(large, REL_BUCKETS - 1)
    return np.where(n < max_exact, n, large).astype(np.int32)


def _sel_to_cmp_matrix(T, ncp):
    nc = T // CMP_STRIDE - CMP_BLOCK // CMP_STRIDE + 1
    ns = T // SEL_BLOCK
    cs = np.arange(nc) * CMP_STRIDE
    ss = np.arange(ns) * SEL_BLOCK
    lo = np.maximum(cs[None, :], ss[:, None])
    hi = np.minimum(cs[None, :] + CMP_BLOCK, ss[:, None] + SEL_BLOCK)
    out = np.zeros((ns, ncp), np.float32)
    out[:, :nc] = np.maximum(hi - lo, 0) / CMP_BLOCK
    return out


def _bias_tables(rel_bias, rel, attend, shift=None):
    bucket = np.where(attend, _rel_bucket_table()[np.clip(rel, 0, REL_MAX_DIST)], REL_BUCKETS)
    rows = rel_bias if shift is None else rel_bias - shift[None, :]
    rows = jnp.concatenate([rows, jnp.full((1, rel_bias.shape[1]), NEG_INF, rel_bias.dtype)], axis=0)
    onehot = (jnp.asarray(bucket.reshape(1, -1)) == jnp.arange(REL_BUCKETS + 1, dtype=jnp.int32)[:, None]).astype(F32)
    tab = jnp.einsum('bh,bn->hn', rows, onehot, precision=HIGHEST)
    return tab.reshape((rel_bias.shape[1],) + rel.shape)


def _prep_in_proj_weight(w_in_all, l):
    D = w_in_all.shape[1]
    nsa0 = RWKV_COLS
    kv0 = nsa0 + NSA_WIDTH
    gates0 = kv0 + 6 * NSA_KV_WIDTH
    merge0 = RWKV_COLS + NSA_COLS
    wt = jnp.swapaxes(w_in_all[l], 0, 1)
    kv = wt[kv0:gates0].reshape(3, 2, NSA_KV_HEADS, HEAD_DIM, D)
    kv = jnp.transpose(kv, (2, 0, 1, 3, 4)).reshape(6 * NSA_KV_WIDTH, D)
    pad = jnp.zeros((SMALL_W - (SM_GATES + 3 * NSA_HEADS), D), wt.dtype)
    return jnp.concatenate([
        wt[0:3 * RWKV_WIDTH],
        wt[nsa0:kv0],
        wt[merge0:merge0 + 2 * D_MODEL],
        kv,
        wt[3 * RWKV_WIDTH:RWKV_COLS],
        wt[gates0:merge0],
        pad,
    ], axis=0).astype(BF16)


def _prep_rwkv_params(mu, w0, w2, a0, a2, g2, k_k, k_a, r_k, ln_w, ln_b):
    C = RWKV_WIDTH
    row = lambda z: z.reshape(1, -1).astype(F32)
    mu_s = jnp.concatenate([mu[3 * C:], jnp.zeros((SMALL_W - (RWKV_COLS - 3 * C),), F32)]).reshape(1, SMALL_W)
    zl = jnp.zeros((DECAY_LORA, C), F32)

    w2p = jnp.concatenate([w2, zl], axis=0)
    w2h = w2p.astype(BF16)
    return dict(
        mu_r=row(mu[0:C]), mu_k=row(mu[C:2 * C]), mu_v=row(mu[2 * C:3 * C]), mu_s=mu_s,
        w0=row(w0), a0=row(a0), k_k=row(k_k), k_a=row(k_a), r_k=row(r_k), ln_w=row(ln_w), ln_b=row(ln_b),
        w2h=w2h, w2l=(w2p - w2h.astype(F32)).astype(BF16),
        a2=jnp.concatenate([zl, a2], axis=0).astype(BF16),
        g2=jnp.concatenate([g2, jnp.zeros((256 - GATE_LORA, C), F32)], axis=0).astype(BF16),
    )


def _prep_nsa_params(pe_k, w1_k, w2_k, pe_v, w1_v, w2_v, q_g, k_g, rel_bias, T):
    hd = HEAD_DIM
    ones = jnp.ones((hd,), F32)
    ncp = T // CMP_STRIDE
    NQ = T // QUERY_BLOCK

    def blockdiag(a, b):
        lead = ((0, 0),) * (a.ndim - 2)
        return jnp.pad(a, lead + ((0, hd), (0, hd))) + jnp.pad(b, lead + ((hd, 0), (hd, 0)))

    w1 = blockdiag(w1_k.reshape(CMP_BLOCK, hd, hd), w1_v.reshape(CMP_BLOCK, hd, hd))
    pe = jnp.concatenate([pe_k, pe_v], axis=1)
    half = CMP_STRIDE

    spread = np.zeros((NSA_GROUP * hd, NSA_GROUP * LANES), np.float32)
    gather = np.zeros((NSA_GROUP * LANES, NSA_GROUP * hd), np.float32)
    for h in range(NSA_GROUP):
        for d in range(hd):
            spread[h * hd + d, h * LANES + d] = 1.0
            gather[h * LANES + hd + d, h * hd + d] = 1.0
    gate_sel = np.zeros((NSA_KV_HEADS, SMALL_W, LANES), np.float32)
    for g in range(NSA_KV_HEADS):
        for j in range(3 * NSA_GROUP):
            gate_sel[g, SM_GATES + 3 * NSA_GROUP * g + j, j] = 1.0
    gate_rep = np.zeros((3, LANES, NSA_GROUP * hd), np.float32)
    for c in range(3):
        for h in range(NSA_GROUP):
            gate_rep[c, 3 * h + c, h * hd:(h + 1) * hd] = 1.0

    qi = np.arange(QUERY_BLOCK)[:, None]
    rel_s = qi + (SEL_CHUNK - QUERY_BLOCK) - np.arange(SEL_CHUNK)[None, :]
    rel_w = qi + WINDOW - np.arange(WIN_KEYS)[None, :]
    rel_c = ((np.arange(NQ)[:, None, None] * QUERY_BLOCK + qi[None])
             - (np.arange(ncp)[None, None, :] * CMP_STRIDE + CMP_BLOCK - 1))
    far_bias = rel_bias[int(_rel_bucket_table()[REL_MAX_DIST])]
    stab = _bias_tables(rel_bias, rel_s, rel_s >= 0, shift=far_bias)
    wtab = _bias_tables(rel_bias, rel_w, (rel_w >= 0) & (rel_w < WINDOW))
    ctab = _bias_tables(rel_bias, rel_c, rel_c >= 0)
    return dict(
        stab=stab, wtab=wtab, ctab=ctab,
        q_g=jnp.tile(q_g, NSA_GROUP).reshape(1, -1),
        kg_cmp=jnp.concatenate([k_g[0], ones]).reshape(1, LANES),
        kg_sel=jnp.concatenate([k_g[1], ones]).reshape(1, LANES),
        kg_win=jnp.concatenate([k_g[2], ones]).reshape(1, LANES),
        pe_a=pe[:half].reshape(1, half * LANES), pe_b=pe[half:].reshape(1, half * LANES),
        cw_a=w1[:half].reshape(half * LANES, LANES).astype(BF16),
        cw_b=w1[half:].reshape(half * LANES, LANES).astype(BF16),
        cw2=blockdiag(w2_k, w2_v).astype(BF16),
        q_spread=jnp.asarray(spread, BF16), gather=jnp.asarray(gather, BF16),
        gate_sel=jnp.asarray(gate_sel), gate_rep=jnp.asarray(gate_rep, BF16),
        sel_mt=jnp.asarray(_sel_to_cmp_matrix(T, ncp)),
    )


def kernel(x, c, w_ada, b_ada, norm1_g, norm2_g, w_in, rwkv_mu, rwkv_w0, rwkv_w2, rwkv_a0, rwkv_a2, rwkv_g2, rwkv_k_k, rwkv_k_a, rwkv_r_k, rwkv_ln_w, rwkv_ln_b, cmp_pe_k, cmp_w1_k, cmp_w2_k, cmp_pe_v, cmp_w1_v, cmp_w2_v, q_norm_g, k_norm_g, rel_bias, w_o_rwkv, w_o_nsa, w_out, w_up, w_down):
    B, T, D = x.shape
    depth = w_in.shape[0]
    x2 = x.reshape(B * T, D)
    for l in range(depth):
        mod6 = _ada_mod(c, w_ada[l], b_ada[l]).reshape(B * 6, 1, D)
        proj, small = _in_proj(x2, norm1_g[l].reshape(1, D), mod6, _prep_in_proj_weight(w_in, l), T)
        pr = _prep_rwkv_params(rwkv_mu[l], rwkv_w0[l], rwkv_w2[l], rwkv_a0[l], rwkv_a2[l], rwkv_g2[l],
                               rwkv_k_k[l], rwkv_k_a[l], rwkv_r_k[l], rwkv_ln_w[l], rwkv_ln_b[l])
        o_a = _rwkv_mix(proj, small, pr, B, T)
        pn = _prep_nsa_params(cmp_pe_k[l], cmp_w1_k[l], cmp_w2_k[l], cmp_pe_v[l], cmp_w1_v[l], cmp_w2_v[l],
                              q_norm_g[l], k_norm_g[l], rel_bias, T)
        qp, ks, vs, kw, vw, gates = _nsa_prep(proj, small, pn, B, T)
        kc, vc = _nsa_compress(proj, pn, B, T)
        o_b = _nsa_attn(qp, kc, vc, ks, vs, kw, vw, gates, pn, B, T)
        mixed = _merge(o_a, o_b, w_o_rwkv[l].astype(BF16), w_o_nsa[l].astype(BF16), proj)
        x1, h2 = _out_proj(mixed, w_out[l].astype(BF16), x2, mod6, norm2_g[l].reshape(1, D), T)
        x2 = _mlp(h2, w_up[l].astype(BF16), w_down[l].astype(BF16), x1, mod6, T)
    return x2.reshape(B, T, D)
```

```python
import functools
import math

import numpy as np
import jax
import jax.numpy as jnp
from jax import lax
from jax.experimental import pallas as pl
from jax.experimental.pallas import tpu as pltpu

F32 = jnp.float32
BF16 = jnp.bfloat16
HIGHEST = lax.Precision.HIGHEST

D_MODEL = 2048
HEAD_DIM = 64
RWKV_WIDTH = D_MODEL // 2
DECAY_LORA = 64
ICLR_LORA = 64
GATE_LORA = 160
RWKV_GN_EPS = 64e-5
NSA_WIDTH = D_MODEL // 2
NSA_HEADS = NSA_WIDTH // HEAD_DIM
NSA_KV_HEADS = 4
NSA_GROUP = NSA_HEADS // NSA_KV_HEADS
NSA_KV_WIDTH = NSA_KV_HEADS * HEAD_DIM
CMP_BLOCK = 32
CMP_STRIDE = 16
SEL_BLOCK = 64
N_SEL = 8
WINDOW = 512
QUERY_BLOCK = 128
REL_BUCKETS = 32
REL_MAX_DIST = 128
D_FF = 4 * D_MODEL
NORM_EPS = 1e-6
NEG_INF = -1e30
FORCE_SCORE = 1e4

RWKV_COLS = 3 * RWKV_WIDTH + DECAY_LORA + ICLR_LORA + GATE_LORA
NSA_COLS = NSA_WIDTH + 6 * NSA_KV_WIDTH + 3 * NSA_HEADS

V7X_VMEM_BYTES = 64 * 1024 * 1024
LANES = 128

COL_RKV = 0
COL_Q = 3 * RWKV_WIDTH
COL_MERGE = COL_Q + NSA_WIDTH
COL_KV = COL_MERGE + 2 * D_MODEL
COL_SMALL = COL_KV + 6 * NSA_KV_WIDTH
SMALL_W = 512
PROJ_COLS = COL_SMALL + SMALL_W
SM_XG = DECAY_LORA + ICLR_LORA
SM_GATES = SM_XG + GATE_LORA


def _vmem_limit(nbytes):
    return int(min(nbytes * 5 // 4 + (4 << 20), V7X_VMEM_BYTES - (8 << 20)))


def _cparams(sem, vmem_bytes):
    return pltpu.CompilerParams(dimension_semantics=sem, vmem_limit_bytes=_vmem_limit(vmem_bytes))


def _ada_kernel(c_ref, w_ref, b_ref, o_ref):
    c = c_ref[...]
    s = c * jax.nn.sigmoid(c)
    o_ref[...] = jnp.dot(s.astype(BF16), w_ref[...].astype(BF16), preferred_element_type=F32) + b_ref[...]


def _ada_mod(c, w_ada, b_ada):
    B, D = c.shape
    N = w_ada.shape[1]
    tn = 1024
    return pl.pallas_call(
        _ada_kernel,
        grid=(N // tn,),
        in_specs=[
            pl.BlockSpec((B, D), lambda j: (0, 0)),
            pl.BlockSpec((D, tn), lambda j: (0, j)),
            pl.BlockSpec((1, tn), lambda j: (0, j)),
        ],
        out_specs=pl.BlockSpec((B, tn), lambda j: (0, j)),
        out_shape=jax.ShapeDtypeStruct((B, N), F32),
        compiler_params=_cparams(("parallel",), 2 * D * tn * 4 + D * tn * 2),
        name="ada_mod",
    )(c, w_ada, b_ada.reshape(1, N))


def _modulated_norm(x, g, sc, sh):
    ms = jnp.mean(x * x, axis=-1, keepdims=True)
    return (x * lax.rsqrt(ms + NORM_EPS) * g) * (1.0 + sc) + sh


def _inproj_kernel(x_ref, g_ref, sh_ref, sc_ref, w_ref, o_ref, small_ref, h_scr):
    j = pl.program_id(1)

    @pl.when(j == 0)
    def _():
        h_scr[...] = _modulated_norm(x_ref[...], g_ref[...], sc_ref[0], sh_ref[0]).astype(BF16)

    y = lax.dot_general(h_scr[...], w_ref[...], (((1,), (1,)), ((), ())),
                        preferred_element_type=F32)
    o_ref[...] = y.astype(BF16)

    @pl.when(j == pl.num_programs(1) - 1)
    def _():
        small_ref[...] = y[:, y.shape[1] - SMALL_W:]


def _in_proj(x2, g1, mod6, w_in_t, T):
    BT, D = x2.shape
    NP = w_in_t.shape[0]
    tm = min(1024, T)
    tn = 1024
    tpb = T // tm
    assert NP - SMALL_W == COL_SMALL and tn >= SMALL_W
    vm = (2 * tm * D * 4 + tm * D * 2 + 2 * D * tn * 2 + 2 * tm * tn * 2 + 2 * tm * SMALL_W * 4
          + tm * tn * 4 + 2 * tm * D * 4)
    return pl.pallas_call(
        _inproj_kernel,
        grid=(BT // tm, NP // tn),
        in_specs=[
            pl.BlockSpec((tm, D), lambda i, j: (i, 0)),
            pl.BlockSpec((1, D), lambda i, j: (0, 0)),
            pl.BlockSpec((1, 1, D), lambda i, j: ((i // tpb) * 6 + 0, 0, 0)),
            pl.BlockSpec((1, 1, D), lambda i, j: ((i // tpb) * 6 + 1, 0, 0)),
            pl.BlockSpec((tn, D), lambda i, j: (j, 0)),
        ],
        out_specs=[pl.BlockSpec((tm, tn), lambda i, j: (i, j)), pl.BlockSpec((tm, SMALL_W), lambda i, j: (i, 0))],
        out_shape=[jax.ShapeDtypeStruct((BT, NP), BF16), jax.ShapeDtypeStruct((BT, SMALL_W), F32)],
        scratch_shapes=[pltpu.VMEM((tm, D), BF16)],
        compiler_params=_cparams(("parallel", "arbitrary"), vm),
        name="in_proj",
    )(x2, g1, mod6, mod6, w_in_t)


def _merge_kernel(oa_ref, ob_ref, wa_ref, wb_ref, ga_ref, gb_ref, o_ref):
    ya = jnp.dot(oa_ref[...], wa_ref[...], preferred_element_type=F32)
    yb = jnp.dot(ob_ref[...], wb_ref[...], preferred_element_type=F32)
    ga, gb = ga_ref[...].astype(F32), gb_ref[...].astype(F32)
    o_ref[...] = (jax.nn.sigmoid(ga) * ya + jax.nn.sigmoid(gb) * yb).astype(BF16)


def _merge(o_a, o_b, w_oa, w_ob, proj):
    BT, W = o_a.shape
    D = w_oa.shape[1]
    tm, tn = 512, 1024
    ga0 = COL_MERGE // tn
    gb0 = (COL_MERGE + D) // tn
    vm = 2 * (2 * tm * W * 2 + 2 * W * tn * 2 + 2 * tm * tn * 4 + tm * tn * 2) + 3 * tm * tn * 4
    return pl.pallas_call(
        _merge_kernel,
        grid=(BT // tm, D // tn),
        in_specs=[
            pl.BlockSpec((tm, W), lambda i, j: (i, 0)),
            pl.BlockSpec((tm, W), lambda i, j: (i, 0)),
            pl.BlockSpec((W, tn), lambda i, j: (0, j)),
            pl.BlockSpec((W, tn), lambda i, j: (0, j)),
            pl.BlockSpec((tm, tn), lambda i, j: (i, ga0 + j)),
            pl.BlockSpec((tm, tn), lambda i, j: (i, gb0 + j)),
        ],
        out_specs=pl.BlockSpec((tm, tn), lambda i, j: (i, j)),
        out_shape=jax.ShapeDtypeStruct((BT, D), BF16),
        compiler_params=_cparams(("parallel", "parallel"), vm),
        name="merge",
    )(o_a, o_b, w_oa, w_ob, proj, proj)


def _outproj_kernel(m_ref, w_ref, x_ref, gt_ref, g_ref, sh_ref, sc_ref, x1_ref, h2_ref):
    y = jnp.dot(m_ref[...], w_ref[...], preferred_element_type=F32)
    x1 = x_ref[...] + gt_ref[0] * y
    x1_ref[...] = x1
    h2_ref[...] = _modulated_norm(x1, g_ref[...], sc_ref[0], sh_ref[0]).astype(BF16)


def _out_proj(mixed, w_out, x2, mod6, g2, T):
    BT, D = x2.shape
    tm = min(512, T)
    tpb = T // tm
    vm = 2 * (tm * D * 2 + D * D * 2 + tm * D * 4 + tm * D * 4 + tm * D * 2) + 3 * tm * D * 4
    mod_spec = lambda k: pl.BlockSpec((1, 1, D), lambda i: ((i // tpb) * 6 + k, 0, 0))
    return pl.pallas_call(
        _outproj_kernel,
        grid=(BT // tm,),
        in_specs=[
            pl.BlockSpec((tm, D), lambda i: (i, 0)),
            pl.BlockSpec((D, D), lambda i: (0, 0)),
            pl.BlockSpec((tm, D), lambda i: (i, 0)),
            mod_spec(2),
            pl.BlockSpec((1, D), lambda i: (0, 0)),
            mod_spec(3),
            mod_spec(4),
        ],
        out_specs=[pl.BlockSpec((tm, D), lambda i: (i, 0)), pl.BlockSpec((tm, D), lambda i: (i, 0))],
        out_shape=[jax.ShapeDtypeStruct((BT, D), F32), jax.ShapeDtypeStruct((BT, D), BF16)],
        compiler_params=_cparams(("parallel",), vm),
        name="out_proj",
    )(mixed, w_out, x2, mod6, g2, mod6, mod6)


OUT_CHUNK = 256


def _mlp_kernel(h_ref, wu_ref, wd_ref, x_ref, gt_ref, o_ref, acc_ref):
    f = pl.program_id(1)

    @pl.when(f == 0)
    def _():
        acc_ref[...] = jnp.zeros_like(acc_ref)

    u = jnp.dot(h_ref[...], wu_ref[...], preferred_element_type=F32)
    u = jnp.square(jnp.maximum(u, 0.0)).astype(BF16)
    for n in range(acc_ref.shape[1] // OUT_CHUNK):
        cs = slice(n * OUT_CHUNK, (n + 1) * OUT_CHUNK)
        acc_ref[:, cs] += jnp.dot(u, wd_ref[:, cs], preferred_element_type=F32)

    @pl.when(f == pl.num_programs(1) - 1)
    def _():
        o_ref[...] = x_ref[...] + gt_ref[0] * acc_ref[...]


def _mlp(h2, w_up, w_down, x1, mod6, T):
    BT, D = x1.shape
    F = w_up.shape[1]
    tm = min(512, T)
    tf = 1024
    tpb = T // tm
    vm = 2 * (tm * D * 2 + 2 * D * tf * 2 + 2 * tm * D * 4) + tm * D * 4 + 2 * tm * tf * 4
    return pl.pallas_call(
        _mlp_kernel,
        grid=(BT // tm, F // tf),
        in_specs=[
            pl.BlockSpec((tm, D), lambda i, f: (i, 0)),
            pl.BlockSpec((D, tf), lambda i, f: (0, f)),
            pl.BlockSpec((tf, D), lambda i, f: (f, 0)),
            pl.BlockSpec((tm, D), lambda i, f: (i, 0)),
            pl.BlockSpec((1, 1, D), lambda i, f: ((i // tpb) * 6 + 5, 0, 0)),
        ],
        out_specs=pl.BlockSpec((tm, D), lambda i, f: (i, 0)),
        out_shape=jax.ShapeDtypeStruct((BT, D), F32),
        scratch_shapes=[pltpu.VMEM((tm, D), F32)],
        compiler_params=_cparams(("parallel", "arbitrary"), vm),
        name="mlp",
    )(h2, w_up, w_down, x1, mod6)


def _dotb(a, b):
    return jnp.dot(a.astype(BF16), b.astype(BF16), preferred_element_type=F32)


def _dotb_nt(a, b):
    return lax.dot_general(a.astype(BF16), b.astype(BF16), (((1,), (1,)), ((), ())),
                           preferred_element_type=F32)


def _split_bf16(x, terms):
    parts, rem = [], x
    for t in range(terms):
        p = rem.astype(BF16)
        parts.append(p)
        if t + 1 < terms:
            rem = rem - p.astype(F32)
    return parts


def _dot_sel(x, sel, terms):
    sel = sel.astype(BF16)
    return sum(jnp.dot(p, sel, preferred_element_type=F32) for p in _split_bf16(x, terms))


def _sel_dot(sel, x, terms):
    sel = sel.astype(BF16)
    return sum(jnp.dot(sel, p, preferred_element_type=F32) for p in _split_bf16(x, terms))


def _dot3(a, b_hi, b_lo):
    a_hi, a_lo = _split_bf16(a, 2)
    return (jnp.dot(a_hi, b_hi, preferred_element_type=F32) + jnp.dot(a_lo, b_hi, preferred_element_type=F32)
            + jnp.dot(a_hi, b_lo, preferred_element_type=F32))


def _iota(shape, axis):
    return lax.broadcasted_iota(jnp.int32, shape, axis)


def _same_head_mask(n):
    return (_iota((n, n), 0) >> 6) == (_iota((n, n), 1) >> 6)


RW_TC = 256
RW_C = 64
RW_LANES = 4 * HEAD_DIM


def _block_diag(x, bdmask):
    return jnp.where(bdmask, jnp.concatenate([x, x, x, x], axis=0), 0.0)


def _rwkv_kernel(r_ref, k_ref, v_ref, s_ref, mur_ref, muk_ref, muv_ref, mus_ref,
                 w0_ref, a0_ref, kk_ref, ka_ref, rk_ref, lnw_ref, lnb_ref,
                 w2h_ref, w2l_ref, a2_ref, g2_ref, o_ref,
                 pr_scr, pk_scr, pv_scr, ps_scr, state_scr):
    TC = r_ref.shape[0]
    C = RW_C
    W = RW_LANES

    @pl.when(pl.program_id(1) == 0)
    def _():
        pr_scr[...] = jnp.zeros_like(pr_scr)
        pk_scr[...] = jnp.zeros_like(pk_scr)
        pv_scr[...] = jnp.zeros_like(pv_scr)
        ps_scr[...] = jnp.zeros_like(ps_scr)
        state_scr[...] = jnp.zeros_like(state_scr)

    def shift_mix(p_ref, prev_scr, mu_ref):
        p = p_ref[...].astype(F32)
        rolled = pltpu.roll(p, 1, 0)
        first = jnp.where(_iota((8, p.shape[1]), 0) == 0, prev_scr[0:1, :], rolled[0:8])
        shifted = jnp.concatenate([first, rolled[8:]], axis=0)
        prev_scr[0:1, :] = p[TC - 1:TC, :]
        return p + (shifted - p) * mu_ref[...]

    CW = r_ref.shape[1]
    G = CW // W
    NQ = TC // C
    groups = lambda x: [x[:, g * W:(g + 1) * W] for g in range(G)]
    per_group = lambda f, x: jnp.concatenate([f(xg) for xg in groups(x)], axis=1)

    r = shift_mix(r_ref, pr_scr, mur_ref)
    k = shift_mix(k_ref, pk_scr, muk_ref)
    v = shift_mix(v_ref, pv_scr, muv_ref)
    sm = shift_mix(s_ref, ps_scr, mus_ref)
    xwa = sm[:, 0:SM_XG]

    bdmask = _same_head_mask(W)
    bones = jnp.where(bdmask, 1.0, 0.0)
    head_sum = lambda x: per_group(lambda xg: _dot_sel(xg, bones, 1), x)
    tri = jnp.where(_same_head_mask(TC) & (_iota((TC, TC), 1) <= _iota((TC, TC), 0)), 1.0, 0.0)
    lane = _iota((C, W), 1)
    row = _iota((C, W), 0)
    lanehead = lane >> 6
    strict = (lane & 63) < row
    incl = (lane & 63) <= row
    eye_cat = jnp.where((lane & 63) == row, 1.0, 0.0)

    wlin = w0_ref[...] + _dot3(jnp.tanh(xwa), w2h_ref[...], w2l_ref[...])
    a = jax.nn.sigmoid(a0_ref[...] + _dotb(xwa, a2_ref[...]))
    gate = _dotb(jax.nn.sigmoid(sm[:, SM_XG:SM_XG + 256]), g2_ref[...])
    z = -wlin
    softplus = jnp.maximum(z, 0.0) + jnp.log(1.0 + jnp.exp(-jnp.abs(z)))
    ld = -jnp.exp(-softplus - 0.5)
    cum = _sel_dot(tri, ld, 3)
    cum_last = jnp.concatenate(
        [jnp.broadcast_to(cum[(q + 1) * C - 1:(q + 1) * C, :], (C, CW)) for q in range(NQ)], axis=0)
    kk = k * kk_ref[...]
    kk = kk * lax.rsqrt(jnp.maximum(head_sum(kk * kk), 1e-24))
    k2 = k * (1.0 + (a - 1.0) * ka_ref[...])
    bvec = kk * a
    e_inv = jnp.exp(-cum)
    e_end = jnp.exp(cum_last - cum)
    At_f = -kk * jnp.exp(cum - ld)
    Rt_f = r * jnp.exp(cum)
    Bt_f = bvec * e_inv
    Kt_f = k2 * e_inv
    Bg_f = bvec * e_end
    Kg_f = k2 * e_end
    g_end = jnp.exp(cum_last)

    items = [(g, q) for q in range(NQ) for g in range(G)]
    blk = lambda x, g, q: x[q * C:(q + 1) * C, g * W:(g + 1) * W]
    bd01 = jnp.where(bdmask, 1.0, 0.0).astype(BF16)

    def bd(x):
        xb = x.astype(BF16)
        return jnp.concatenate([xb, xb, xb, xb], axis=0) * bd01

    At = {it: blk(At_f, *it) for it in items}
    Rt = {it: blk(Rt_f, *it) for it in items}
    Vq = {it: blk(v, *it) for it in items}
    AA = {}
    for it in items:
        bk = jnp.concatenate([jnp.where(lanehead == h, X, 0.0)
                              for X in (blk(Bt_f, *it), blk(Kt_f, *it)) for h in range(4)], axis=0)
        AA[it] = _dotb_nt(jnp.concatenate([At[it], Rt[it]], axis=0), bk)
    A_ab = {it: jnp.where(strict, AA[it][0:C, 0:W], 0.0) for it in items}
    A_ak = {it: jnp.where(strict, AA[it][0:C, W:2 * W], 0.0) for it in items}
    A_rb = {it: jnp.where(incl, AA[it][C:2 * C, 0:W], 0.0) for it in items}
    A_rk = {it: jnp.where(incl, AA[it][C:2 * C, W:2 * W], 0.0) for it in items}
    M = dict(A_ab)
    Tm = {it: eye_cat + A_ab[it] for it in items}
    for _ in range(5):
        M = {it: _dotb(M[it], bd(M[it])) for it in items}
        Tm = {it: Tm[it] + _dotb(M[it], bd(Tm[it])) for it in items}
    Vbd = {it: bd(Vq[it]) for it in items}
    akv = {it: _dotb(A_ak[it], Vbd[it]) for it in items}
    rkv = {it: _dotb(A_rk[it], Vbd[it]) for it in items}

    S = [state_scr[g] for g in range(G)]
    ys = []
    for q in range(NQ):
        its = [(g, q) for g in range(G)]
        rhs = [_dotb_nt(At[it], S[it[0]]) + akv[it] for it in its]
        U = [_dotb(Tm[it], bd(rhs[g])) for g, it in enumerate(its)]
        ys.append(jnp.concatenate(
            [_dotb_nt(Rt[it], S[g]) + _dotb(A_rb[it], bd(U[g])) + rkv[it] for g, it in enumerate(its)], axis=1))
        upd = [_dotb(jnp.concatenate([U[g], Vq[it]], axis=0).T,
                     jnp.concatenate([blk(Bg_f, *it), blk(Kg_f, *it)], axis=0)) for g, it in enumerate(its)]
        S = [S[g] * blk(g_end, g, q)[0:1, :] + jnp.where(bdmask, upd[g], 0.0) for g in range(G)]
    for g in range(G):
        state_scr[g] = S[g]
    y = jnp.concatenate(ys, axis=0)

    inv_n = 1.0 / HEAD_DIM
    d = y - head_sum(y) * inv_n
    var = head_sum(d * d) * inv_n
    yn = d * lax.rsqrt(var + RWKV_GN_EPS) * lnw_ref[...] + lnb_ref[...]
    bonus = head_sum(r * k2 * rk_ref[...]) * v
    o_ref[...] = ((yn + bonus) * gate).astype(BF16)


def _rwkv_mix(proj, small, pr, B, T):
    BT = proj.shape[0]
    TC = min(RW_TC, T)
    W = RW_LANES
    CW = RWKV_WIDTH
    nct = T // TC
    row = lambda b, c: b * nct + c
    full = lambda shape: pl.BlockSpec(shape, lambda b, c: (0, 0))
    vec = full((1, CW))
    in_specs = [
        pl.BlockSpec((TC, CW), lambda b, c: (row(b, c), 0)),
        pl.BlockSpec((TC, CW), lambda b, c: (row(b, c), 1)),
        pl.BlockSpec((TC, CW), lambda b, c: (row(b, c), 2)),
        pl.BlockSpec((TC, SMALL_W), lambda b, c: (row(b, c), 0)),
        vec, vec, vec,
        full((1, SMALL_W)),
        vec, vec, vec, vec, vec, vec, vec,
        full((SM_XG, CW)), full((SM_XG, CW)), full((SM_XG, CW)), full((256, CW)),
    ]
    vm = (2 * (3 * TC * CW * 4 + TC * SMALL_W * 4 + TC * CW * 2 + (3 * SM_XG + 256) * CW * 2)
          + 24 * TC * CW * 4)
    return pl.pallas_call(
        _rwkv_kernel,
        grid=(B, nct),
        in_specs=in_specs,
        out_specs=pl.BlockSpec((TC, CW), lambda b, c: (row(b, c), 0)),
        out_shape=jax.ShapeDtypeStruct((BT, CW), BF16),
        scratch_shapes=[pltpu.VMEM((8, CW), F32), pltpu.VMEM((8, CW), F32), pltpu.VMEM((8, CW), F32),
                        pltpu.VMEM((8, SMALL_W), F32), pltpu.VMEM((CW // W, W, W), F32)],
        compiler_params=_cparams(("parallel", "arbitrary"), vm),
        name="rwkv_mix",
    )(proj, proj, proj, small, pr["mu_r"], pr["mu_k"], pr["mu_v"], pr["mu_s"],
      pr["w0"], pr["a0"], pr["k_k"], pr["k_a"], pr["r_k"], pr["ln_w"], pr["ln_b"],
      pr["w2h"], pr["w2l"], pr["a2"], pr["g2"])


NSA_TT = 256
BLOCK_LANE0 = HEAD_DIM
PADFLAG_LANE = BLOCK_LANE0 + 32


def _nsa_prep_kernel(q_ref, kv0_ref, kv1_ref, kv2_ref, s_ref, qg_ref, kgs_ref, kgw_ref, e_ref, gsel_ref,
                     qp_ref, ksl_ref, vsl_ref, kwl_ref, vwl_ref, gt_ref):
    tt = q_ref.shape[0]
    n_pad = NSA_KPAD // tt
    step = pl.program_id(1)
    lane = _iota((tt, LANES), 1)

    @pl.when(step < n_pad)
    def _():
        flag = jnp.broadcast_to(jnp.where(lane == PADFLAG_LANE, 1.0, 0.0).astype(BF16), ksl_ref.shape[1:])
        ksl_ref[0] = flag
        kwl_ref[0] = flag
        vsl_ref[0] = jnp.zeros_like(flag)
        vwl_ref[0] = jnp.zeros_like(flag)

    @pl.when(step >= n_pad)
    def _():
        _nsa_prep_tile(step - n_pad, lane, q_ref, (kv0_ref, kv1_ref, kv2_ref), s_ref, qg_ref, kgs_ref, kgw_ref,
                       e_ref, gsel_ref, qp_ref, ksl_ref, vsl_ref, kwl_ref, vwl_ref, gt_ref)


def _nsa_prep_tile(tile, lane, q_ref, kv_refs, s_ref, qg_ref, kgs_ref, kgw_ref, e_ref, gsel_ref,
                   qp_ref, ksl_ref, vsl_ref, kwl_ref, vwl_ref, gt_ref):
    tt = q_ref.shape[0]
    QW = NSA_GROUP * HEAD_DIM
    bones = jnp.where(_same_head_mask(QW), 1.0, 0.0)
    bones2 = jnp.where(_same_head_mask(LANES), 1.0, 0.0)
    block_id = (tile * tt + _iota((tt, LANES), 0)) >> 6
    onehot = jnp.where(lane == block_id + BLOCK_LANE0, 1.0, 0.0)
    small = _split_bf16(s_ref[...], 3)

    def pair(g, branch):
        off = (3 * g + branch) * LANES
        return kv_refs[off // SMALL_W][:, off % SMALL_W:off % SMALL_W + LANES].astype(F32)

    def slabs(g, x, gain, k_ref, v_ref):
        ms = _dot_sel(x * x, bones2, 1) * (1.0 / HEAD_DIM)
        k_ref[0, g] = jnp.where(lane < HEAD_DIM, x * lax.rsqrt(ms + NORM_EPS) * gain, onehot).astype(BF16)
        v_ref[0, g] = jnp.where(lane >= HEAD_DIM, x, 1.0).astype(BF16)

    for g in range(NSA_KV_HEADS):
        q = q_ref[:, g * QW:(g + 1) * QW].astype(F32)
        ms = _dot_sel(q * q, bones, 1) * (1.0 / HEAD_DIM)
        qn = (q * lax.rsqrt(ms + NORM_EPS) * qg_ref[...]) * (HEAD_DIM ** -0.5)
        qs = jnp.dot(qn.astype(BF16), e_ref[...], preferred_element_type=F32).astype(BF16)
        for h in range(NSA_GROUP):
            qp_ref[0, NSA_GROUP * g + h] = qs[:, h * LANES:(h + 1) * LANES]
        slabs(g, pair(g, 1), kgs_ref[...], ksl_ref, vsl_ref)
        slabs(g, pair(g, 2), kgw_ref[...], kwl_ref, vwl_ref)
        sel = gsel_ref[g].astype(BF16)
        gt_ref[0, g] = jax.nn.sigmoid(sum(jnp.dot(part, sel, preferred_element_type=F32) for part in small))


def _nsa_prep(proj, small, pn, B, T):
    tt = min(NSA_TT, T)
    ntt = T // tt
    G = NSA_KV_HEADS
    QW = NSA_GROUP * HEAD_DIM
    n_pad = NSA_KPAD // tt
    tile = lambda t: jnp.maximum(t - n_pad, 0)
    row = lambda b, t: b * ntt + tile(t)
    full = lambda shape: pl.BlockSpec(shape, lambda b, t: tuple(0 for _ in shape))
    kv_spec = lambda j: pl.BlockSpec((tt, SMALL_W), lambda b, t: (row(b, t), COL_KV // SMALL_W + j))
    slab = pl.BlockSpec((1, G, tt, LANES), lambda b, t: (b, 0, t, 0))
    vm = 2 * (tt * NSA_WIDTH * 4 + 4 * tt * SMALL_W * 4 + QW * 4 * LANES * 2 + G * SMALL_W * LANES * 4
              + 16 * tt * LANES * 2 + 16 * tt * LANES * 2 + 4 * tt * LANES * 4) + 24 * tt * QW * 4
    return pl.pallas_call(
        _nsa_prep_kernel,
        grid=(B, ntt + n_pad),
        in_specs=[
            pl.BlockSpec((tt, NSA_WIDTH), lambda b, t: (row(b, t), COL_Q // NSA_WIDTH)),
            kv_spec(0), kv_spec(1), kv_spec(2),
            pl.BlockSpec((tt, SMALL_W), lambda b, t: (row(b, t), 0)),
            full((1, QW)), full((1, LANES)), full((1, LANES)), full((QW, NSA_GROUP * LANES)),
            full((G, SMALL_W, LANES)),
        ],
        out_specs=[pl.BlockSpec((1, NSA_HEADS, tt, LANES), lambda b, t: (b, 0, tile(t), 0)),
                   slab, slab, slab, slab,
                   pl.BlockSpec((1, G, tt, LANES), lambda b, t: (b, 0, tile(t), 0))],
        out_shape=[jax.ShapeDtypeStruct((B, NSA_HEADS, T, LANES), BF16)]
        + [jax.ShapeDtypeStruct((B, G, T + NSA_KPAD, LANES), BF16)] * 4
        + [jax.ShapeDtypeStruct((B, G, T, LANES), F32)],
        compiler_params=_cparams(("parallel", "arbitrary"), vm),
        name="nsa_prep",
    )(proj, proj, proj, proj, small, pn["q_g"], pn["kg_sel"], pn["kg_win"], pn["q_spread"], pn["gate_sel"])


def _gelu_tanh(x):
    return 0.5 * x * (1.0 + jnp.tanh(math.sqrt(2.0 / math.pi) * (x + 0.044715 * (x * x * x))))


def _nsa_compress_kernel(x_ref, pea_ref, peb_ref, wa_ref, wb_ref, w2_ref, kg_ref, kc_ref, vc_ref, x_scr):
    nsub = x_ref.shape[0] // CMP_STRIDE
    x_scr[...] = x_ref[...].astype(F32)
    xs = jnp.concatenate([x_scr[pl.ds(s, nsub, stride=CMP_STRIDE), :] for s in range(CMP_STRIDE)], axis=1)
    p0 = jnp.dot((xs + pea_ref[...]).astype(BF16), wa_ref[...], preferred_element_type=F32)
    p1 = jnp.dot((xs + peb_ref[...]).astype(BF16), wb_ref[...], preferred_element_type=F32)
    hid = _gelu_tanh(p0 + pltpu.roll(p1, nsub - 1, 0))
    out = jnp.dot(hid.astype(BF16), w2_ref[...], preferred_element_type=F32)
    bones2 = jnp.where(_same_head_mask(LANES), 1.0, 0.0)
    ms = _dot_sel(out * out, bones2, 2) * (1.0 / HEAD_DIM)
    is_k = _iota(out.shape, 1) < HEAD_DIM
    kc_ref[0, 0] = jnp.where(is_k, out * lax.rsqrt(ms + NORM_EPS) * kg_ref[...], 0.0).astype(BF16)
    vc_ref[0, 0] = jnp.where(is_k, 0.0, out).astype(BF16)


def _nsa_compress(proj, pn, B, T):
    G = NSA_KV_HEADS
    nsub = T // CMP_STRIDE
    kv_blk = COL_KV // LANES
    KW = CMP_STRIDE * LANES
    full = lambda shape: pl.BlockSpec(shape, lambda b, g: tuple(0 for _ in shape))
    vm = 2 * (T * LANES * 4 + 2 * KW * LANES * 2 + nsub * LANES * 2) + 6 * nsub * KW * 4
    return pl.pallas_call(
        _nsa_compress_kernel,
        grid=(B, G),
        in_specs=[
            pl.BlockSpec((T, LANES), lambda b, g: (b, kv_blk + 3 * g)),
            full((1, KW)), full((1, KW)), full((KW, LANES)), full((KW, LANES)), full((LANES, LANES)),
            full((1, LANES)),
        ],
        out_specs=[pl.BlockSpec((1, 1, nsub, LANES), lambda b, g: (b, g, 0, 0))] * 2,
        out_shape=[jax.ShapeDtypeStruct((B, G, nsub, LANES), BF16)] * 2,
        scratch_shapes=[pltpu.VMEM((T, LANES), F32)],
        compiler_params=_cparams(("parallel", "parallel"), vm),
        name="nsa_compress",
    )(proj, pn["pe_a"], pn["pe_b"], pn["cw_a"], pn["cw_b"], pn["cw2"], pn["kg_cmp"])


NSA_GROUPS_PER_STEP = 4
NSA_KPAD = WINDOW
SEL_CHUNK = 4 * QUERY_BLOCK
WIN_KEYS = WINDOW + QUERY_BLOCK


def _lane_tile_max(s):
    tiles = [s[:, j * LANES:(j + 1) * LANES] for j in range(s.shape[1] // LANES)]
    while len(tiles) > 1:
        tiles = [jnp.maximum(a, b) for a, b in zip(tiles[0::2], tiles[1::2])] + ([tiles[-1]] if len(tiles) % 2 else [])
    return tiles[0]


def _nsa_attn_kernel(*refs, n_sel, max_far):
    n_far = pl.program_id(2) // 4
    for k in range(max_far + 1):
        pl.when(n_far == k)(functools.partial(_nsa_attn_step, k, *refs, n_sel=n_sel))


def _nsa_attn_step(n_far, q_ref, kc_ref, vc_ref, ks_ref, vs_ref, kw_ref, vw_ref, gt_ref, stab_ref, wtab_ref, ctab_ref,
                   selmt_ref, gather_ref, grep_ref, o_ref, sbuf, sc_scr, *, n_sel):
    i = pl.program_id(2)
    QB = QUERY_BLOCK
    HG = NSA_GROUP
    R = HG * QB
    t0 = i * QB
    groups = range(kc_ref.shape[1])
    each = lambda f: [f(g) for g in groups]
    tile4 = lambda z: jnp.concatenate([z, z, z, z], axis=0)
    heads = lambda ref, g: ref[HG * g:HG * (g + 1)]
    lane = _iota((QB, LANES), 1)
    pad_mask = jnp.where(lane == PADFLAG_LANE, NEG_INF, 0.0)
    q = each(lambda g: q_ref[0, HG * g:HG * (g + 1)].reshape(R, LANES))
    q32 = each(lambda g: q[g].astype(F32))

    ncp = kc_ref.shape[2]
    s = each(lambda g: _dotb_nt(q[g], kc_ref[0, g]) + heads(ctab_ref, g)[:, 0].reshape(R, ncp))
    m = each(lambda g: jnp.max(s[g], axis=-1, keepdims=True))
    p_c = each(lambda g: jnp.exp(s[g] - m[g]))
    lsum = each(lambda g: jnp.sum(p_c[g], axis=-1, keepdims=True))
    p_c = each(lambda g: p_c[g] * jnp.where(m[g] > 0.5 * NEG_INF, 1.0 / lsum[g], 0.0))
    o_c = each(lambda g: jnp.dot(p_c[g].astype(BF16), vc_ref[0, g], preferred_element_type=F32))

    wrows = pl.ds(pl.multiple_of(t0, QB), WIN_KEYS)
    q_win = each(lambda g: (q32[g] + tile4(pad_mask)).astype(BF16))
    s = each(lambda g: _dotb_nt(q_win[g], kw_ref[0, g, wrows, :]) + heads(wtab_ref, g).reshape(R, WIN_KEYS))
    p = each(lambda g: jnp.exp(s[g] - jnp.max(_lane_tile_max(s[g]), axis=-1, keepdims=True)))
    acc_w = each(lambda g: jnp.dot(p[g].astype(BF16), vw_ref[0, g, wrows, :], preferred_element_type=F32))

    psum = each(lambda g: p_c[g][0:QB] + p_c[g][QB:2 * QB] + p_c[g][2 * QB:3 * QB] + p_c[g][3 * QB:4 * QB])
    selmt = selmt_ref[...]
    parts = each(lambda g: _split_bf16(psum[g], 3))
    imp = each(lambda g: sum(lax.dot_general(selmt, part, (((1,), (1,)), ((), ())), preferred_element_type=F32)
                             for part in parts[g]))
    ns = selmt.shape[0]
    blk = _iota((ns, QB), 0)
    cur = (t0 + _iota((ns, QB), 1)) >> 6
    forced = (blk == 0) | (blk == cur) | (blk == cur - 1)
    score = each(lambda g: jnp.where(forced, FORCE_SCORE, jnp.where(blk <= cur, imp[g], -1.0)))
    for g in groups:
        sc_scr[g] = score[g]
    ranks = [[] for _ in groups]
    for j in range(ns):
        lower = jnp.where(blk > j, 1.0, 0.0)
        for g in groups:
            other = sc_scr[g, j:j + 1, :]
            ranks[g].append(jnp.where(other > score[g], 1.0, 0.0) + jnp.where(other == score[g], lower, 0.0))
    while len(ranks[0]) > 1:
        ranks = [[a + b for a, b in zip(r[0::2], r[1::2])] + ([r[-1]] if len(r) % 2 else []) for r in ranks]
    chosen_t = each(lambda g: jnp.where(ranks[g][0] < n_sel, 1.0, 0.0))
    zrows = lambda n: jnp.zeros((n, QB), F32)
    chosen = each(lambda g: jnp.concatenate([zrows(BLOCK_LANE0), chosen_t[g], zrows(LANES - BLOCK_LANE0 - ns)],
                                            axis=0).T)
    is_block_lane = (lane >= BLOCK_LANE0) & (lane < BLOCK_LANE0 + ns)
    q_sel = each(lambda g: (q32[g] + tile4(jnp.where(is_block_lane, (chosen[g] - 1.0) * (-NEG_INF), pad_mask))
                            ).astype(BF16))

    def chunk_rows(c):
        return pl.ds(pl.multiple_of((i - 4 * c + 1) * QB, QB), SEL_CHUNK)

    s0 = each(lambda g: _dotb_nt(q_sel[g], ks_ref[0, g, chunk_rows(0), :]) + heads(stab_ref, g).reshape(R, SEL_CHUNK))
    tile_max = each(lambda g: _lane_tile_max(s0[g]))
    for c in range(1, n_far + 1):
        s = each(lambda g: _dotb_nt(q_sel[g], ks_ref[0, g, chunk_rows(c), :]))
        for g in groups:
            sbuf[g, c - 1] = s[g]
        tile_max = each(lambda g: jnp.maximum(tile_max[g], _lane_tile_max(s[g])))
    m_s = each(lambda g: jnp.max(tile_max[g], axis=-1, keepdims=True))

    def pv(g, c):
        s = s0[g] if c == 0 else sbuf[g, c - 1]
        return jnp.dot(jnp.exp(s - m_s[g]).astype(BF16), vs_ref[0, g, chunk_rows(c), :], preferred_element_type=F32)

    acc_s = each(lambda g: sum(pv(g, c) for c in range(n_far + 1)))

    is_value = _iota((R, LANES), 1) >= HEAD_DIM
    normalised = lambda acc: jnp.where(is_value, acc * (1.0 / pltpu.roll(acc, HEAD_DIM, 1)), 0.0)

    def natural(o):
        cat = jnp.concatenate([o[h * QB:(h + 1) * QB] for h in range(HG)], axis=1).astype(BF16)
        return jnp.dot(cat, gather_ref[...], preferred_element_type=F32)

    branches = each(lambda g: [o_c[g], normalised(acc_s[g]), normalised(acc_w[g])])
    gate = each(lambda g: [_dot_sel(gt_ref[0, g], grep_ref[c], 1) for c in range(3)])
    W = HG * HEAD_DIM
    for g in groups:
        out = sum(gate[g][c] * natural(branches[g][c]) for c in range(3))
        o_ref[:, g * W:(g + 1) * W] = out.astype(BF16)


def _nsa_attn(qp, kc, vc, ks, vs, kw, vw, gates, pn, B, T):
    G = NSA_KV_HEADS
    QB = QUERY_BLOCK
    NQ = T // QB
    ncp = kc.shape[2]
    ns = T // SEL_BLOCK
    TP = T + NSA_KPAD
    R = NSA_GROUP * QB
    n_sel = min(N_SEL, ns)
    max_far = (NQ - 1) // 4
    NG = NSA_GROUPS_PER_STEP
    HB = NSA_GROUP * NG
    once = pl.Buffered(1)
    gtab = lambda w: pl.BlockSpec((HB, QB, w), lambda b, g, i: (g, 0, 0), pipeline_mode=once)
    slab = lambda rows: pl.BlockSpec((1, NG, rows, LANES), lambda b, g, i: (b, g, 0, 0), pipeline_mode=once)
    vm = (NG * (2 * ncp * LANES * 2 + 4 * TP * LANES * 2 + R * (SEL_CHUNK + WIN_KEYS) * 4)
          + 2 * NG * (R * LANES * 2 + QB * LANES * 4 + R * ncp * 4 + QB * NSA_GROUP * HEAD_DIM * 2)
          + NG * ((max_far + 1) * R * SEL_CHUNK * 4 + 2 * R * LANES * 4 + 4 * R * WIN_KEYS * 4))
    return pl.pallas_call(
        functools.partial(_nsa_attn_kernel, n_sel=n_sel, max_far=max_far),
        grid=(B, G // NG, NQ),
        in_specs=[
            pl.BlockSpec((1, HB, QB, LANES), lambda b, g, i: (b, g, i, 0)),
            slab(ncp), slab(ncp), slab(TP), slab(TP), slab(TP), slab(TP),
            pl.BlockSpec((1, NG, QB, LANES), lambda b, g, i: (b, g, i, 0)),
            gtab(SEL_CHUNK), gtab(WIN_KEYS),
            pl.BlockSpec((HB, 1, QB, ncp), lambda b, g, i: (g, i, 0, 0)),
            pl.BlockSpec((ns, ncp), lambda b, g, i: (0, 0)),
            pl.BlockSpec((NSA_GROUP * LANES, NSA_GROUP * HEAD_DIM), lambda b, g, i: (0, 0)),
            pl.BlockSpec((3, LANES, NSA_GROUP * HEAD_DIM), lambda b, g, i: (0, 0, 0)),
        ],
        out_specs=pl.BlockSpec((QB, HB * HEAD_DIM), lambda b, g, i: (b * NQ + i, g)),
        out_shape=jax.ShapeDtypeStruct((B * T, NSA_WIDTH), BF16),
        scratch_shapes=[pltpu.VMEM((NG, max(max_far, 1), R, SEL_CHUNK), F32), pltpu.VMEM((NG, ns, QB), F32)],
        compiler_params=_cparams(("parallel", "parallel", "arbitrary"), vm),
        name="nsa_attn",
    )(qp, kc, vc, ks, vs, kw, vw, gates, pn["stab"], pn["wtab"], pn["ctab"], pn["sel_mt"], pn["gather"],
      pn["gate_rep"])


def _rel_bucket_table():
    n = np.arange(REL_MAX_DIST + 1)
    max_exact = REL_BUCKETS // 2
    nf = np.maximum(n, max_exact).astype(np.float32)
    large = max_exact + (np.log(nf / np.float32(max_exact)) / np.float32(math.log(REL_MAX_DIST / max_exact))
                         * np.float32(REL_BUCKETS - max_exact)).astype(np.int32)
    large = np.minimum(large, REL_BUCKETS - 1)
    return np.where(n < max_exact, n, large).astype(np.int32)


def _sel_to_cmp_matrix(T, ncp):
    nc = T // CMP_STRIDE - CMP_BLOCK // CMP_STRIDE + 1
    ns = T // SEL_BLOCK
    cs = np.arange(nc) * CMP_STRIDE
    ss = np.arange(ns) * SEL_BLOCK
    lo = np.maximum(cs[None, :], ss[:, None])
    hi = np.minimum(cs[None, :] + CMP_BLOCK, ss[:, None] + SEL_BLOCK)
    out = np.zeros((ns, ncp), np.float32)
    out[:, :nc] = np.maximum(hi - lo, 0) / CMP_BLOCK
    return out


def _bias_tables(rel_bias, rel, attend, shift=None):
    bucket = np.where(attend, _rel_bucket_table()[np.clip(rel, 0, REL_MAX_DIST)], REL_BUCKETS)
    rows = rel_bias if shift is None else rel_bias - shift[None, :]
    rows = jnp.concatenate([rows, jnp.full((1, rel_bias.shape[1]), NEG_INF, rel_bias.dtype)], axis=0)
    onehot = (jnp.asarray(bucket.reshape(1, -1)) == jnp.arange(REL_BUCKETS + 1, dtype=jnp.int32)[:, None]).astype(F32)
    tab = jnp.einsum('bh,bn->hn', rows, onehot, precision=HIGHEST)
    return tab.reshape((rel_bias.shape[1],) + rel.shape)


def _prep_in_proj_weight(w_in_all, l):
    D = w_in_all.shape[1]
    nsa0 = RWKV_COLS
    kv0 = nsa0 + NSA_WIDTH
    gates0 = kv0 + 6 * NSA_KV_WIDTH
    merge0 = RWKV_COLS + NSA_COLS
    wt = jnp.swapaxes(w_in_all[l], 0, 1)
    kv = wt[kv0:gates0].reshape(3, 2, NSA_KV_HEADS, HEAD_DIM, D)
    kv = jnp.transpose(kv, (2, 0, 1, 3, 4)).reshape(6 * NSA_KV_WIDTH, D)
    pad = jnp.zeros((SMALL_W - (SM_GATES + 3 * NSA_HEADS), D), wt.dtype)
    return jnp.concatenate([
        wt[0:3 * RWKV_WIDTH],
        wt[nsa0:kv0],
        wt[merge0:merge0 + 2 * D_MODEL],
        kv,
        wt[3 * RWKV_WIDTH:RWKV_COLS],
        wt[gates0:merge0],
        pad,
    ], axis=0).astype(BF16)


def _prep_rwkv_params(mu, w0, w2, a0, a2, g2, k_k, k_a, r_k, ln_w, ln_b):
    C = RWKV_WIDTH
    row = lambda z: z.reshape(1, -1).astype(F32)
    mu_s = jnp.concatenate([mu[3 * C:], jnp.zeros((SMALL_W - (RWKV_COLS - 3 * C),), F32)]).reshape(1, SMALL_W)
    zl = jnp.zeros((DECAY_LORA, C), F32)

    w2p = jnp.concatenate([w2, zl], axis=0)
    w2h = w2p.astype(BF16)
    return dict(
        mu_r=row(mu[0:C]), mu_k=row(mu[C:2 * C]), mu_v=row(mu[2 * C:3 * C]), mu_s=mu_s,
        w0=row(w0), a0=row(a0), k_k=row(k_k), k_a=row(k_a), r_k=row(r_k), ln_w=row(ln_w), ln_b=row(ln_b),
        w2h=w2h, w2l=(w2p - w2h.astype(F32)).astype(BF16),
        a2=jnp.concatenate([zl, a2], axis=0).astype(BF16),
        g2=jnp.concatenate([g2, jnp.zeros((256 - GATE_LORA, C), F32)], axis=0).astype(BF16),
    )


def _prep_nsa_params(pe_k, w1_k, w2_k, pe_v, w1_v, w2_v, q_g, k_g, rel_bias, T):
    hd = HEAD_DIM
    ones = jnp.ones((hd,), F32)
    ncp = T // CMP_STRIDE
    NQ = T // QUERY_BLOCK

    def blockdiag(a, b):
        lead = ((0, 0),) * (a.ndim - 2)
        return jnp.pad(a, lead + ((0, hd), (0, hd))) + jnp.pad(b, lead + ((hd, 0), (hd, 0)))

    w1 = blockdiag(w1_k.reshape(CMP_BLOCK, hd, hd), w1_v.reshape(CMP_BLOCK, hd, hd))
    pe = jnp.concatenate([pe_k, pe_v], axis=1)
    half = CMP_STRIDE

    spread = np.zeros((NSA_GROUP * hd, NSA_GROUP * LANES), np.float32)
    gather = np.zeros((NSA_GROUP * LANES, NSA_GROUP * hd), np.float32)
    for h in range(NSA_GROUP):
        for d in range(hd):
            spread[h * hd + d, h * LANES + d] = 1.0
            gather[h * LANES + hd + d, h * hd + d] = 1.0
    gate_sel = np.zeros((NSA_KV_HEADS, SMALL_W, LANES), np.float32)
    for g in range(NSA_KV_HEADS):
        for j in range(3 * NSA_GROUP):
            gate_sel[g, SM_GATES + 3 * NSA_GROUP * g + j, j] = 1.0
    gate_rep = np.zeros((3, LANES, NSA_GROUP * hd), np.float32)
    for c in range(3):
        for h in range(NSA_GROUP):
            gate_rep[c, 3 * h + c, h * hd:(h + 1) * hd] = 1.0

    qi = np.arange(QUERY_BLOCK)[:, None]
    rel_s = qi + (SEL_CHUNK - QUERY_BLOCK) - np.arange(SEL_CHUNK)[None, :]
    rel_w = qi + WINDOW - np.arange(WIN_KEYS)[None, :]
    rel_c = ((np.arange(NQ)[:, None, None] * QUERY_BLOCK + qi[None])
             - (np.arange(ncp)[None, None, :] * CMP_STRIDE + CMP_BLOCK - 1))
    far_bias = rel_bias[int(_rel_bucket_table()[REL_MAX_DIST])]
    stab = _bias_tables(rel_bias, rel_s, rel_s >= 0, shift=far_bias)
    wtab = _bias_tables(rel_bias, rel_w, (rel_w >= 0) & (rel_w < WINDOW))
    ctab = _bias_tables(rel_bias, rel_c, rel_c >= 0)
    return dict(
        stab=stab, wtab=wtab, ctab=ctab,
        q_g=jnp.tile(q_g, NSA_GROUP).reshape(1, -1),
        kg_cmp=jnp.concatenate([k_g[0], ones]).reshape(1, LANES),
        kg_sel=jnp.concatenate([k_g[1], ones]).reshape(1, LANES),
        kg_win=jnp.concatenate([k_g[2], ones]).reshape(1, LANES),
        pe_a=pe[:half].reshape(1, half * LANES), pe_b=pe[half:].reshape(1, half * LANES),
        cw_a=w1[:half].reshape(half * LANES, LANES).astype(BF16),
        cw_b=w1[half:].reshape(half * LANES, LANES).astype(BF16),
        cw2=blockdiag(w2_k, w2_v).astype(BF16),
        q_spread=jnp.asarray(spread, BF16), gather=jnp.asarray(gather, BF16),
        gate_sel=jnp.asarray(gate_sel), gate_rep=jnp.asarray(gate_rep, BF16),
        sel_mt=jnp.asarray(_sel_to_cmp_matrix(T, ncp)),
    )


def kernel(x, c, w_ada, b_ada, norm1_g, norm2_g, w_in, rwkv_mu, rwkv_w0, rwkv_w2, rwkv_a0, rwkv_a2, rwkv_g2, rwkv_k_k, rwkv_k_a, rwkv_r_k, rwkv_ln_w, rwkv_ln_b, cmp_pe_k, cmp_w1_k, cmp_w2_k, cmp_pe_v, cmp_w1_v, cmp_w2_v, q_norm_g, k_norm_g, rel_bias, w_o_rwkv, w_o_nsa, w_out, w_up, w_down):
    B, T, D = x.shape
    depth = w_in.shape[0]
    x2 = x.reshape(B * T, D)
    for l in range(depth):
        mod6 = _ada_mod(c, w_ada[l], b_ada[l]).reshape(B * 6, 1, D)
        proj, small = _in_proj(x2, norm1_g[l].reshape(1, D), mod6, _prep_in_proj_weight(w_in, l), T)
        pr = _prep_rwkv_params(rwkv_mu[l], rwkv_w0[l], rwkv_w2[l], rwkv_a0[l], rwkv_a2[l], rwkv_g2[l],
                               rwkv_k_k[l], rwkv_k_a[l], rwkv_r_k[l], rwkv_ln_w[l], rwkv_ln_b[l])
        o_a = _rwkv_mix(proj, small, pr, B, T)
        pn = _prep_nsa_params(cmp_pe_k[l], cmp_w1_k[l], cmp_w2_k[l], cmp_pe_v[l], cmp_w1_v[l], cmp_w2_v[l],
                              q_norm_g[l], k_norm_g[l], rel_bias, T)
        qp, ks, vs, kw, vw, gates = _nsa_prep(proj, small, pn, B, T)
        kc, vc = _nsa_compress(proj, pn, B, T)
        o_b = _nsa_attn(qp, kc, vc, ks, vs, kw, vw, gates, pn, B, T)
        mixed = _merge(o_a, o_b, w_o_rwkv[l].astype(BF16), w_o_nsa[l].astype(BF16), proj)
        x1, h2 = _out_proj(mixed, w_out[l].astype(BF16), x2, mod6, norm2_g[l].reshape(1, D), T)
        x2 = _mlp(h2, w_up[l].astype(BF16), w_down[l].astype(BF16), x1, mod6, T)
    return x2.reshape(B, T, D)
```

```python
import functools
import math

import numpy as np
import jax
import jax.numpy as jnp
from jax import lax
from jax.experimental import pallas as pl
from jax.experimental.pallas import tpu as pltpu

F32 = jnp.float32
BF16 = jnp.bfloat16
HIGHEST = lax.Precision.HIGHEST

D_MODEL = 2048
HEAD_DIM = 64
RWKV_WIDTH = D_MODEL // 2
DECAY_LORA = 64
ICLR_LORA = 64
GATE_LORA = 160
RWKV_GN_EPS = 64e-5
NSA_WIDTH = D_MODEL // 2
NSA_HEADS = NSA_WIDTH // HEAD_DIM
NSA_KV_HEADS = 4
NSA_GROUP = NSA_HEADS // NSA_KV_HEADS
NSA_KV_WIDTH = NSA_KV_HEADS * HEAD_DIM
CMP_BLOCK = 32
CMP_STRIDE = 16
SEL_BLOCK = 64
N_SEL = 8
WINDOW = 512
QUERY_BLOCK = 128
REL_BUCKETS = 32
REL_MAX_DIST = 128
D_FF = 4 * D_MODEL
NORM_EPS = 1e-6
NEG_INF = -1e30
FORCE_SCORE = 1e4

RWKV_COLS = 3 * RWKV_WIDTH + DECAY_LORA + ICLR_LORA + GATE_LORA
NSA_COLS = NSA_WIDTH + 6 * NSA_KV_WIDTH + 3 * NSA_HEADS

V7X_VMEM_BYTES = 64 * 1024 * 1024
LANES = 128

COL_RKV = 0
COL_Q = 3 * RWKV_WIDTH
COL_MERGE = COL_Q + NSA_WIDTH
COL_KV = COL_MERGE + 2 * D_MODEL
COL_SMALL = COL_KV + 6 * NSA_KV_WIDTH
SMALL_W = 512
PROJ_COLS = COL_SMALL + SMALL_W
SM_XG = DECAY_LORA + ICLR_LORA
SM_GATES = SM_XG + GATE_LORA


def _vmem_limit(nbytes):
    return int(min(nbytes * 5 // 4 + (4 << 20), V7X_VMEM_BYTES - (8 << 20)))


def _cparams(sem, vmem_bytes):
    return pltpu.CompilerParams(dimension_semantics=sem, vmem_limit_bytes=_vmem_limit(vmem_bytes))


def _ada_kernel(c_ref, w_ref, b_ref, o_ref):
    c = c_ref[...]
    s = c * jax.nn.sigmoid(c)
    o_ref[...] = jnp.dot(s.astype(BF16), w_ref[...].astype(BF16), preferred_element_type=F32) + b_ref[...]


def _ada_mod(c, w_ada, b_ada):
    B, D = c.shape
    N = w_ada.shape[1]
    tn = 1024
    return pl.pallas_call(
        _ada_kernel,
        grid=(N // tn,),
        in_specs=[
            pl.BlockSpec((B, D), lambda j: (0, 0)),
            pl.BlockSpec((D, tn), lambda j: (0, j)),
            pl.BlockSpec((1, tn), lambda j: (0, j)),
        ],
        out_specs=pl.BlockSpec((B, tn), lambda j: (0, j)),
        out_shape=jax.ShapeDtypeStruct((B, N), F32),
        compiler_params=_cparams(("parallel",), 2 * D * tn * 4 + D * tn * 2),
        name="ada_mod",
    )(c, w_ada, b_ada.reshape(1, N))


def _modulated_norm(x, g, sc, sh):
    ms = jnp.mean(x * x, axis=-1, keepdims=True)
    return (x * lax.rsqrt(ms + NORM_EPS) * g) * (1.0 + sc) + sh


def _inproj_kernel(x_ref, g_ref, sh_ref, sc_ref, w_ref, o_ref, small_ref, h_scr):
    j = pl.program_id(1)
    last = pl.num_programs(1) - 1

    def project(h):
        y = lax.dot_general(h, w_ref[...], (((1,), (1,)), ((), ())),
                            preferred_element_type=F32)
        o_ref[...] = y.astype(BF16)
        return y

    @pl.when(j == 0)
    def _():
        h = _modulated_norm(x_ref[...], g_ref[...], sc_ref[0], sh_ref[0]).astype(BF16)
        h_scr[...] = h
        project(h)

    @pl.when((j > 0) & (j < last))
    def _():
        project(h_scr[...])

    @pl.when(j == last)
    def _():
        y = project(h_scr[...])
        small_ref[...] = y[:, y.shape[1] - SMALL_W:]


def _in_proj(x2, g1, mod6, w_in_t, T):
    BT, D = x2.shape
    NP = w_in_t.shape[0]
    tm = min(1024, T)
    tn = 1024
    tpb = T // tm
    assert NP - SMALL_W == COL_SMALL and tn >= SMALL_W
    vm = (2 * tm * D * 4 + tm * D * 2 + 2 * D * tn * 2 + 2 * tm * tn * 2 + 2 * tm * SMALL_W * 4
          + tm * tn * 4 + 2 * tm * D * 4)
    return pl.pallas_call(
        _inproj_kernel,
        grid=(BT // tm, NP // tn),
        in_specs=[
            pl.BlockSpec((tm, D), lambda i, j: (i, 0)),
            pl.BlockSpec((1, D), lambda i, j: (0, 0)),
            pl.BlockSpec((1, 1, D), lambda i, j: ((i // tpb) * 6 + 0, 0, 0)),
            pl.BlockSpec((1, 1, D), lambda i, j: ((i // tpb) * 6 + 1, 0, 0)),
            pl.BlockSpec((tn, D), lambda i, j: (j, 0)),
        ],
        out_specs=[pl.BlockSpec((tm, tn), lambda i, j: (i, j)), pl.BlockSpec((tm, SMALL_W), lambda i, j: (i, 0))],
        out_shape=[jax.ShapeDtypeStruct((BT, NP), BF16), jax.ShapeDtypeStruct((BT, SMALL_W), F32)],
        scratch_shapes=[pltpu.VMEM((tm, D), BF16)],
        compiler_params=_cparams(("parallel", "arbitrary"), vm),
        name="in_proj",
    )(x2, g1, mod6, mod6, w_in_t)


def _merge_kernel(oa_ref, ob_ref, wa_ref, wb_ref, ga_ref, gb_ref, o_ref):
    ya = jnp.dot(oa_ref[...], wa_ref[...], preferred_element_type=F32)
    yb = jnp.dot(ob_ref[...], wb_ref[...], preferred_element_type=F32)
    ga, gb = ga_ref[...].astype(F32), gb_ref[...].astype(F32)
    o_ref[...] = (jax.nn.sigmoid(ga) * ya + jax.nn.sigmoid(gb) * yb).astype(BF16)


def _merge(o_a, o_b, w_oa, w_ob, proj):
    BT, W = o_a.shape
    D = w_oa.shape[1]
    tm, tn = 512, 1024
    ga0 = COL_MERGE // tn
    gb0 = (COL_MERGE + D) // tn
    vm = 2 * (2 * tm * W * 2 + 2 * W * tn * 2 + 2 * tm * tn * 4 + tm * tn * 2) + 3 * tm * tn * 4
    return pl.pallas_call(
        _merge_kernel,
        grid=(BT // tm, D // tn),
        in_specs=[
            pl.BlockSpec((tm, W), lambda i, j: (i, 0)),
            pl.BlockSpec((tm, W), lambda i, j: (i, 0)),
            pl.BlockSpec((W, tn), lambda i, j: (0, j)),
            pl.BlockSpec((W, tn), lambda i, j: (0, j)),
            pl.BlockSpec((tm, tn), lambda i, j: (i, ga0 + j)),
            pl.BlockSpec((tm, tn), lambda i, j: (i, gb0 + j)),
        ],
        out_specs=pl.BlockSpec((tm, tn), lambda i, j: (i, j)),
        out_shape=jax.ShapeDtypeStruct((BT, D), BF16),
        compiler_params=_cparams(("parallel", "parallel"), vm),
        name="merge",
    )(o_a, o_b, w_oa, w_ob, proj, proj)


def _outproj_kernel(m_ref, w_ref, x_ref, gt_ref, g_ref, sh_ref, sc_ref, x1_ref, h2_ref):
    y = jnp.dot(m_ref[...], w_ref[...], preferred_element_type=F32)
    x1 = x_ref[...] + gt_ref[0] * y
    x1_ref[...] = x1
    h2_ref[...] = _modulated_norm(x1, g_ref[...], sc_ref[0], sh_ref[0]).astype(BF16)


def _out_proj(mixed, w_out, x2, mod6, g2, T):
    BT, D = x2.shape
    tm = min(512, T)
    tpb = T // tm
    vm = 2 * (tm * D * 2 + D * D * 2 + tm * D * 4 + tm * D * 4 + tm * D * 2) + 3 * tm * D * 4
    mod_spec = lambda k: pl.BlockSpec((1, 1, D), lambda i: ((i // tpb) * 6 + k, 0, 0))
    return pl.pallas_call(
        _outproj_kernel,
        grid=(BT // tm,),
        in_specs=[
            pl.BlockSpec((tm, D), lambda i: (i, 0)),
            pl.BlockSpec((D, D), lambda i: (0, 0)),
            pl.BlockSpec((tm, D), lambda i: (i, 0)),
            mod_spec(2),
            pl.BlockSpec((1, D), lambda i: (0, 0)),
            mod_spec(3),
            mod_spec(4),
        ],
        out_specs=[pl.BlockSpec((tm, D), lambda i: (i, 0)), pl.BlockSpec((tm, D), lambda i: (i, 0))],
        out_shape=[jax.ShapeDtypeStruct((BT, D), F32), jax.ShapeDtypeStruct((BT, D), BF16)],
        compiler_params=_cparams(("parallel",), vm),
        name="out_proj",
    )(mixed, w_out, x2, mod6, g2, mod6, mod6)


OUT_CHUNK = 256


def _mlp_kernel(h_ref, wu_ref, wd_ref, x_ref, gt_ref, o_ref, acc_ref):
    f = pl.program_id(1)
    last = pl.num_programs(1) - 1

    def step(first, final):
        u = jnp.dot(h_ref[...], wu_ref[...], preferred_element_type=F32)
        u = jnp.square(jnp.maximum(u, 0.0)).astype(BF16)
        gt = gt_ref[0]
        for n in range(acc_ref.shape[1] // OUT_CHUNK):
            cs = slice(n * OUT_CHUNK, (n + 1) * OUT_CHUNK)
            part = jnp.dot(u, wd_ref[:, cs], preferred_element_type=F32)
            if first:
                acc_ref[:, cs] = part
            elif final:
                o_ref[:, cs] = x_ref[:, cs] + gt[:, cs] * (acc_ref[:, cs] + part)
            else:
                acc_ref[:, cs] += part

    pl.when(f == 0)(functools.partial(step, True, False))
    pl.when((f > 0) & (f < last))(functools.partial(step, False, False))
    pl.when(f == last)(functools.partial(step, False, True))


def _mlp(h2, w_up, w_down, x1, mod6, T):
    BT, D = x1.shape
    F = w_up.shape[1]
    tm = min(512, T)
    tf = 1024
    tpb = T // tm
    vm = 2 * (tm * D * 2 + 2 * D * tf * 2 + 2 * tm * D * 4) + tm * D * 4 + 2 * tm * tf * 4
    return pl.pallas_call(
        _mlp_kernel,
        grid=(BT // tm, F // tf),
        in_specs=[
            pl.BlockSpec((tm, D), lambda i, f: (i, 0)),
            pl.BlockSpec((D, tf), lambda i, f: (0, f)),
            pl.BlockSpec((tf, D), lambda i, f: (f, 0)),
            pl.BlockSpec((tm, D), lambda i, f: (i, 0)),
            pl.BlockSpec((1, 1, D), lambda i, f: ((i // tpb) * 6 + 5, 0, 0)),
        ],
        out_specs=pl.BlockSpec((tm, D), lambda i, f: (i, 0)),
        out_shape=jax.ShapeDtypeStruct((BT, D), F32),
        scratch_shapes=[pltpu.VMEM((tm, D), F32)],
        compiler_params=_cparams(("parallel", "arbitrary"), vm),
        name="mlp",
    )(h2, w_up, w_down, x1, mod6)


def _dotb(a, b):
    return jnp.dot(a.astype(BF16), b.astype(BF16), preferred_element_type=F32)


def _dotb_nt(a, b):
    return lax.dot_general(a.astype(BF16), b.astype(BF16), (((1,), (1,)), ((), ())),
                           preferred_element_type=F32)


def _split_bf16(x, terms):
    parts, rem = [], x
    for t in range(terms):
        p = rem.astype(BF16)
        parts.append(p)
        if t + 1 < terms:
            rem = rem - p.astype(F32)
    return parts


def _dot_sel(x, sel, terms):
    sel = sel.astype(BF16)
    return sum(jnp.dot(p, sel, preferred_element_type=F32) for p in _split_bf16(x, terms))


def _sel_dot(sel, x, terms):
    sel = sel.astype(BF16)
    return sum(jnp.dot(sel, p, preferred_element_type=F32) for p in _split_bf16(x, terms))


def _dot3(a, b_hi, b_lo):
    a_hi, a_lo = _split_bf16(a, 2)
    return (jnp.dot(a_hi, b_hi, preferred_element_type=F32) + jnp.dot(a_lo, b_hi, preferred_element_type=F32)
            + jnp.dot(a_hi, b_lo, preferred_element_type=F32))


def _iota(shape, axis):
    return lax.broadcasted_iota(jnp.int32, shape, axis)


def _same_head_mask(n):
    return (_iota((n, n), 0) >> 6) == (_iota((n, n), 1) >> 6)


RW_TC = 256
RW_C = 64
RW_LANES = 4 * HEAD_DIM


def _block_diag(x, bdmask):
    return jnp.where(bdmask, jnp.concatenate([x, x, x, x], axis=0), 0.0)


def _rwkv_kernel(r_ref, k_ref, v_ref, s_ref, mur_ref, muk_ref, muv_ref, mus_ref,
                 w0_ref, a0_ref, kk_ref, ka_ref, rk_ref, lnw_ref, lnb_ref,
                 w2h_ref, w2l_ref, a2_ref, g2_ref, o_ref,
                 pr_scr, pk_scr, pv_scr, ps_scr, state_scr):
    TC = r_ref.shape[0]
    C = RW_C
    W = RW_LANES

    @pl.when(pl.program_id(1) == 0)
    def _():
        pr_scr[...] = jnp.zeros_like(pr_scr)
        pk_scr[...] = jnp.zeros_like(pk_scr)
        pv_scr[...] = jnp.zeros_like(pv_scr)
        ps_scr[...] = jnp.zeros_like(ps_scr)
        state_scr[...] = jnp.zeros_like(state_scr)

    def shift_mix(p_ref, prev_scr, mu_ref):
        p = p_ref[...].astype(F32)
        rolled = pltpu.roll(p, 1, 0)
        first = jnp.where(_iota((8, p.shape[1]), 0) == 0, prev_scr[0:1, :], rolled[0:8])
        shifted = jnp.concatenate([first, rolled[8:]], axis=0)
        prev_scr[0:1, :] = p[TC - 1:TC, :]
        return p + (shifted - p) * mu_ref[...]

    CW = r_ref.shape[1]
    G = CW // W
    NQ = TC // C
    groups = lambda x: [x[:, g * W:(g + 1) * W] for g in range(G)]
    per_group = lambda f, x: jnp.concatenate([f(xg) for xg in groups(x)], axis=1)

    r = shift_mix(r_ref, pr_scr, mur_ref)
    k = shift_mix(k_ref, pk_scr, muk_ref)
    v = shift_mix(v_ref, pv_scr, muv_ref)
    sm = shift_mix(s_ref, ps_scr, mus_ref)
    xwa = sm[:, 0:SM_XG]

    bdmask = _same_head_mask(W)
    bones = jnp.where(bdmask, 1.0, 0.0)
    head_sum = lambda x: per_group(lambda xg: _dot_sel(xg, bones, 1), x)
    tri = jnp.where(_same_head_mask(TC) & (_iota((TC, TC), 1) <= _iota((TC, TC), 0)), 1.0, 0.0)
    lane = _iota((C, W), 1)
    row = _iota((C, W), 0)
    lanehead = lane >> 6
    strict = (lane & 63) < row
    incl = (lane & 63) <= row
    eye_cat = jnp.where((lane & 63) == row, 1.0, 0.0)

    wlin = w0_ref[...] + _dot3(jnp.tanh(xwa), w2h_ref[...], w2l_ref[...])
    a = jax.nn.sigmoid(a0_ref[...] + _dotb(xwa, a2_ref[...]))
    gate = _dotb(jax.nn.sigmoid(sm[:, SM_XG:SM_XG + 256]), g2_ref[...])
    z = -wlin
    softplus = jnp.maximum(z, 0.0) + jnp.log(1.0 + jnp.exp(-jnp.abs(z)))
    ld = -jnp.exp(-softplus - 0.5)
    cum = _sel_dot(tri, ld, 3)
    cum_last = jnp.concatenate(
        [jnp.broadcast_to(cum[(q + 1) * C - 1:(q + 1) * C, :], (C, CW)) for q in range(NQ)], axis=0)
    kk = k * kk_ref[...]
    kk = kk * lax.rsqrt(jnp.maximum(head_sum(kk * kk), 1e-24))
    k2 = k * (1.0 + (a - 1.0) * ka_ref[...])
    bvec = kk * a
    e_inv = jnp.exp(-cum)
    e_end = jnp.exp(cum_last - cum)
    At_f = -kk * jnp.exp(cum - ld)
    Rt_f = r * jnp.exp(cum)
    Bt_f = bvec * e_inv
    Kt_f = k2 * e_inv
    Bg_f = bvec * e_end
    Kg_f = k2 * e_end
    g_end = jnp.exp(cum_last)

    items = [(g, q) for q in range(NQ) for g in range(G)]
    blk = lambda x, g, q: x[q * C:(q + 1) * C, g * W:(g + 1) * W]
    bd01 = jnp.where(bdmask, 1.0, 0.0).astype(BF16)

    def bd(x):
        xb = x.astype(BF16)
        return jnp.concatenate([xb, xb, xb, xb], axis=0) * bd01

    At = {it: blk(At_f, *it) for it in items}
    Rt = {it: blk(Rt_f, *it) for it in items}
    Vq = {it: blk(v, *it) for it in items}
    AA = {}
    for it in items:
        bk = jnp.concatenate([jnp.where(lanehead == h, X, 0.0)
                              for X in (blk(Bt_f, *it), blk(Kt_f, *it)) for h in range(4)], axis=0)
        AA[it] = _dotb_nt(jnp.concatenate([At[it], Rt[it]], axis=0), bk)
    A_ab = {it: jnp.where(strict, AA[it][0:C, 0:W], 0.0) for it in items}
    A_ak = {it: jnp.where(strict, AA[it][0:C, W:2 * W], 0.0) for it in items}
    A_rb = {it: jnp.where(incl, AA[it][C:2 * C, 0:W], 0.0) for it in items}
    A_rk = {it: jnp.where(incl, AA[it][C:2 * C, W:2 * W], 0.0) for it in items}
    M = dict(A_ab)
    Tm = {it: eye_cat + A_ab[it] for it in items}
    for _ in range(5):
        M = {it: _dotb(M[it], bd(M[it])) for it in items}
        Tm = {it: Tm[it] + _dotb(M[it], bd(Tm[it])) for it in items}
    Vbd = {it: bd(Vq[it]) for it in items}
    akv = {it: _dotb(A_ak[it], Vbd[it]) for it in items}
    rkv = {it: _dotb(A_rk[it], Vbd[it]) for it in items}

    S = [state_scr[g] for g in range(G)]
    ys = []
    for q in range(NQ):
        its = [(g, q) for g in range(G)]
        rhs = [_dotb_nt(At[it], S[it[0]]) + akv[it] for it in its]
        U = [_dotb(Tm[it], bd(rhs[g])) for g, it in enumerate(its)]
        ys.append(jnp.concatenate(
            [_dotb_nt(Rt[it], S[g]) + _dotb(A_rb[it], bd(U[g])) + rkv[it] for g, it in enumerate(its)], axis=1))
        upd = [_dotb(jnp.concatenate([U[g], Vq[it]], axis=0).T,
                     jnp.concatenate([blk(Bg_f, *it), blk(Kg_f, *it)], axis=0)) for g, it in enumerate(its)]
        S = [S[g] * blk(g_end, g, q)[0:1, :] + jnp.where(bdmask, upd[g], 0.0) for g in range(G)]
    for g in range(G):
        state_scr[g] = S[g]
    y = jnp.concatenate(ys, axis=0)

    inv_n = 1.0 / HEAD_DIM
    d = y - head_sum(y) * inv_n
    var = head_sum(d * d) * inv_n
    yn = d * lax.rsqrt(var + RWKV_GN_EPS) * lnw_ref[...] + lnb_ref[...]
    bonus = head_sum(r * k2 * rk_ref[...]) * v
    o_ref[...] = ((yn + bonus) * gate).astype(BF16)


def _rwkv_mix(proj, small, pr, B, T):
    BT = proj.shape[0]
    TC = min(RW_TC, T)
    W = RW_LANES
    CW = RWKV_WIDTH
    nct = T // TC
    row = lambda b, c: b * nct + c
    full = lambda shape: pl.BlockSpec(shape, lambda b, c: (0, 0))
    vec = full((1, CW))
    in_specs = [
        pl.BlockSpec((TC, CW), lambda b, c: (row(b, c), 0)),
        pl.BlockSpec((TC, CW), lambda b, c: (row(b, c), 1)),
        pl.BlockSpec((TC, CW), lambda b, c: (row(b, c), 2)),
        pl.BlockSpec((TC, SMALL_W), lambda b, c: (row(b, c), 0)),
        vec, vec, vec,
        full((1, SMALL_W)),
        vec, vec, vec, vec, vec, vec, vec,
        full((SM_XG, CW)), full((SM_XG, CW)), full((SM_XG, CW)), full((256, CW)),
    ]
    vm = (2 * (3 * TC * CW * 4 + TC * SMALL_W * 4 + TC * CW * 2 + (3 * SM_XG + 256) * CW * 2)
          + 24 * TC * CW * 4)
    return pl.pallas_call(
        _rwkv_kernel,
        grid=(B, nct),
        in_specs=in_specs,
        out_specs=pl.BlockSpec((TC, CW), lambda b, c: (row(b, c), 0)),
        out_shape=jax.ShapeDtypeStruct((BT, CW), BF16),
        scratch_shapes=[pltpu.VMEM((8, CW), F32), pltpu.VMEM((8, CW), F32), pltpu.VMEM((8, CW), F32),
                        pltpu.VMEM((8, SMALL_W), F32), pltpu.VMEM((CW // W, W, W), F32)],
        compiler_params=_cparams(("parallel", "arbitrary"), vm),
        name="rwkv_mix",
    )(proj, proj, proj, small, pr["mu_r"], pr["mu_k"], pr["mu_v"], pr["mu_s"],
      pr["w0"], pr["a0"], pr["k_k"], pr["k_a"], pr["r_k"], pr["ln_w"], pr["ln_b"],
      pr["w2h"], pr["w2l"], pr["a2"], pr["g2"])


NSA_TT = 256
BLOCK_LANE0 = HEAD_DIM
PADFLAG_LANE = BLOCK_LANE0 + 32


def _nsa_prep_kernel(q_ref, kv0_ref, kv1_ref, kv2_ref, s_ref, qg_ref, kgs_ref, kgw_ref, e_ref, gsel_ref,
                     qp_ref, ksl_ref, vsl_ref, kwl_ref, vwl_ref, gt_ref):
    tt = q_ref.shape[0]
    n_pad = NSA_KPAD // tt
    step = pl.program_id(1)
    lane = _iota((tt, LANES), 1)

    @pl.when(step < n_pad)
    def _():
        flag = jnp.broadcast_to(jnp.where(lane == PADFLAG_LANE, 1.0, 0.0).astype(BF16), ksl_ref.shape[1:])
        ksl_ref[0] = flag
        kwl_ref[0] = flag
        vsl_ref[0] = jnp.zeros_like(flag)
        vwl_ref[0] = jnp.zeros_like(flag)

    @pl.when(step >= n_pad)
    def _():
        _nsa_prep_tile(step - n_pad, lane, q_ref, (kv0_ref, kv1_ref, kv2_ref), s_ref, qg_ref, kgs_ref, kgw_ref,
                       e_ref, gsel_ref, qp_ref, ksl_ref, vsl_ref, kwl_ref, vwl_ref, gt_ref)


def _nsa_prep_tile(tile, lane, q_ref, kv_refs, s_ref, qg_ref, kgs_ref, kgw_ref, e_ref, gsel_ref,
                   qp_ref, ksl_ref, vsl_ref, kwl_ref, vwl_ref, gt_ref):
    tt = q_ref.shape[0]
    QW = NSA_GROUP * HEAD_DIM
    bones = jnp.where(_same_head_mask(QW), 1.0, 0.0)
    bones2 = jnp.where(_same_head_mask(LANES), 1.0, 0.0)
    block_id = (tile * tt + _iota((tt, LANES), 0)) >> 6
    onehot = jnp.where(lane == block_id + BLOCK_LANE0, 1.0, 0.0)
    small = _split_bf16(s_ref[...], 3)

    def pair(g, branch):
        off = (3 * g + branch) * LANES
        return kv_refs[off // SMALL_W][:, off % SMALL_W:off % SMALL_W + LANES].astype(F32)

    def slabs(g, x, gain, k_ref, v_ref):
        ms = _dot_sel(x * x, bones2, 1) * (1.0 / HEAD_DIM)
        k_ref[0, g] = jnp.where(lane < HEAD_DIM, x * lax.rsqrt(ms + NORM_EPS) * gain, onehot).astype(BF16)
        v_ref[0, g] = jnp.where(lane >= HEAD_DIM, x, 1.0).astype(BF16)

    for g in range(NSA_KV_HEADS):
        q = q_ref[:, g * QW:(g + 1) * QW].astype(F32)
        ms = _dot_sel(q * q, bones, 1) * (1.0 / HEAD_DIM)
        qn = (q * lax.rsqrt(ms + NORM_EPS) * qg_ref[...]) * (HEAD_DIM ** -0.5)
        qs = jnp.dot(qn.astype(BF16), e_ref[...], preferred_element_type=F32).astype(BF16)
        for h in range(NSA_GROUP):
            qp_ref[0, NSA_GROUP * g + h] = qs[:, h * LANES:(h + 1) * LANES]
        slabs(g, pair(g, 1), kgs_ref[...], ksl_ref, vsl_ref)
        slabs(g, pair(g, 2), kgw_ref[...], kwl_ref, vwl_ref)
        sel = gsel_ref[g].astype(BF16)
        gt_ref[0, g] = jax.nn.sigmoid(sum(jnp.dot(part, sel, preferred_element_type=F32) for part in small))


def _nsa_prep(proj, small, pn, B, T):
    tt = min(NSA_TT, T)
    ntt = T // tt
    G = NSA_KV_HEADS
    QW = NSA_GROUP * HEAD_DIM
    n_pad = NSA_KPAD // tt
    tile = lambda t: jnp.maximum(t - n_pad, 0)
    row = lambda b, t: b * ntt + tile(t)
    full = lambda shape: pl.BlockSpec(shape, lambda b, t: tuple(0 for _ in shape))
    kv_spec = lambda j: pl.BlockSpec((tt, SMALL_W), lambda b, t: (row(b, t), COL_KV // SMALL_W + j))
    slab = pl.BlockSpec((1, G, tt, LANES), lambda b, t: (b, 0, t, 0))
    vm = 2 * (tt * NSA_WIDTH * 4 + 4 * tt * SMALL_W * 4 + QW * 4 * LANES * 2 + G * SMALL_W * LANES * 4
              + 16 * tt * LANES * 2 + 16 * tt * LANES * 2 + 4 * tt * LANES * 4) + 24 * tt * QW * 4
    return pl.pallas_call(
        _nsa_prep_kernel,
        grid=(B, ntt + n_pad),
        in_specs=[
            pl.BlockSpec((tt, NSA_WIDTH), lambda b, t: (row(b, t), COL_Q // NSA_WIDTH)),
            kv_spec(0), kv_spec(1), kv_spec(2),
            pl.BlockSpec((tt, SMALL_W), lambda b, t: (row(b, t), 0)),
            full((1, QW)), full((1, LANES)), full((1, LANES)), full((QW, NSA_GROUP * LANES)),
            full((G, SMALL_W, LANES)),
        ],
        out_specs=[pl.BlockSpec((1, NSA_HEADS, tt, LANES), lambda b, t: (b, 0, tile(t), 0)),
                   slab, slab, slab, slab,
                   pl.BlockSpec((1, G, tt, LANES), lambda b, t: (b, 0, tile(t), 0))],
        out_shape=[jax.ShapeDtypeStruct((B, NSA_HEADS, T, LANES), BF16)]
        + [jax.ShapeDtypeStruct((B, G, T + NSA_KPAD, LANES), BF16)] * 4
        + [jax.ShapeDtypeStruct((B, G, T, LANES), F32)],
        compiler_params=_cparams(("parallel", "arbitrary"), vm),
        name="nsa_prep",
    )(proj, proj, proj, proj, small, pn["q_g"], pn["kg_sel"], pn["kg_win"], pn["q_spread"], pn["gate_sel"])


def _gelu_tanh(x):
    return 0.5 * x * (1.0 + jnp.tanh(math.sqrt(2.0 / math.pi) * (x + 0.044715 * (x * x * x))))


def _nsa_compress_kernel(x_ref, pea_ref, peb_ref, wa_ref, wb_ref, w2_ref, kg_ref, kc_ref, vc_ref, x_scr):
    nsub = x_ref.shape[0] // CMP_STRIDE
    x_scr[...] = x_ref[...].astype(F32)
    xs = jnp.concatenate([x_scr[pl.ds(s, nsub, stride=CMP_STRIDE), :] for s in range(CMP_STRIDE)], axis=1)
    p0 = jnp.dot((xs + pea_ref[...]).astype(BF16), wa_ref[...], preferred_element_type=F32)
    p1 = jnp.dot((xs + peb_ref[...]).astype(BF16), wb_ref[...], preferred_element_type=F32)
    hid = _gelu_tanh(p0 + pltpu.roll(p1, nsub - 1, 0))
    out = jnp.dot(hid.astype(BF16), w2_ref[...], preferred_element_type=F32)
    bones2 = jnp.where(_same_head_mask(LANES), 1.0, 0.0)
    ms = _dot_sel(out * out, bones2, 2) * (1.0 / HEAD_DIM)
    is_k = _iota(out.shape, 1) < HEAD_DIM
    kc_ref[0, 0] = jnp.where(is_k, out * lax.rsqrt(ms + NORM_EPS) * kg_ref[...], 0.0).astype(BF16)
    vc_ref[0, 0] = jnp.where(is_k, 0.0, out).astype(BF16)


def _nsa_compress(proj, pn, B, T):
    G = NSA_KV_HEADS
    nsub = T // CMP_STRIDE
    kv_blk = COL_KV // LANES
    KW = CMP_STRIDE * LANES
    full = lambda shape: pl.BlockSpec(shape, lambda b, g: tuple(0 for _ in shape))
    vm = 2 * (T * LANES * 4 + 2 * KW * LANES * 2 + nsub * LANES * 2) + 6 * nsub * KW * 4
    return pl.pallas_call(
        _nsa_compress_kernel,
        grid=(B, G),
        in_specs=[
            pl.BlockSpec((T, LANES), lambda b, g: (b, kv_blk + 3 * g)),
            full((1, KW)), full((1, KW)), full((KW, LANES)), full((KW, LANES)), full((LANES, LANES)),
            full((1, LANES)),
        ],
        out_specs=[pl.BlockSpec((1, 1, nsub, LANES), lambda b, g: (b, g, 0, 0))] * 2,
        out_shape=[jax.ShapeDtypeStruct((B, G, nsub, LANES), BF16)] * 2,
        scratch_shapes=[pltpu.VMEM((T, LANES), F32)],
        compiler_params=_cparams(("parallel", "parallel"), vm),
        name="nsa_compress",
    )(proj, pn["pe_a"], pn["pe_b"], pn["cw_a"], pn["cw_b"], pn["cw2"], pn["kg_cmp"])


NSA_GROUPS_PER_STEP = 4
NSA_KPAD = WINDOW
SEL_CHUNK = 4 * QUERY_BLOCK
WIN_KEYS = WINDOW + QUERY_BLOCK


def _lane_tile_max(s):
    tiles = [s[:, j * LANES:(j + 1) * LANES] for j in range(s.shape[1] // LANES)]
    while len(tiles) > 1:
        tiles = [jnp.maximum(a, b) for a, b in zip(tiles[0::2], tiles[1::2])] + ([tiles[-1]] if len(tiles) % 2 else [])
    return tiles[0]


def _nsa_attn_kernel(*refs, n_sel, max_far):
    n_far = pl.program_id(2) // 4
    for k in range(max_far + 1):
        pl.when(n_far == k)(functools.partial(_nsa_attn_step, k, *refs, n_sel=n_sel))


def _nsa_attn_step(n_far, q_ref, kc_ref, vc_ref, ks_ref, vs_ref, kw_ref, vw_ref, gt_ref, stab_ref, wtab_ref, ctab_ref,
                   selmt_ref, gather_ref, grep_ref, o_ref, sbuf, sc_scr, *, n_sel):
    i = pl.program_id(2)
    QB = QUERY_BLOCK
    HG = NSA_GROUP
    R = HG * QB
    t0 = i * QB
    groups = range(kc_ref.shape[1])
    each = lambda f: [f(g) for g in groups]
    tile4 = lambda z: jnp.concatenate([z, z, z, z], axis=0)
    heads = lambda ref, g: ref[HG * g:HG * (g + 1)]
    lane = _iota((QB, LANES), 1)
    pad_mask = jnp.where(lane == PADFLAG_LANE, NEG_INF, 0.0)
    q = each(lambda g: q_ref[0, HG * g:HG * (g + 1)].reshape(R, LANES))
    q32 = each(lambda g: q[g].astype(F32))

    ncp = kc_ref.shape[2]
    s = each(lambda g: _dotb_nt(q[g], kc_ref[0, g]) + heads(ctab_ref, g)[:, 0].reshape(R, ncp))
    m = each(lambda g: jnp.max(s[g], axis=-1, keepdims=True))
    p_c = each(lambda g: jnp.exp(s[g] - m[g]))
    lsum = each(lambda g: jnp.sum(p_c[g], axis=-1, keepdims=True))
    p_c = each(lambda g: p_c[g] * jnp.where(m[g] > 0.5 * NEG_INF, 1.0 / lsum[g], 0.0))
    o_c = each(lambda g: jnp.dot(p_c[g].astype(BF16), vc_ref[0, g], preferred_element_type=F32))

    wrows = pl.ds(pl.multiple_of(t0, QB), WIN_KEYS)
    q_win = each(lambda g: (q32[g] + tile4(pad_mask)).astype(BF16))
    s = each(lambda g: _dotb_nt(q_win[g], kw_ref[0, g, wrows, :]) + heads(wtab_ref, g).reshape(R, WIN_KEYS))
    p = each(lambda g: jnp.exp(s[g] - jnp.max(_lane_tile_max(s[g]), axis=-1, keepdims=True)))
    acc_w = each(lambda g: jnp.dot(p[g].astype(BF16), vw_ref[0, g, wrows, :], preferred_element_type=F32))

    psum = each(lambda g: p_c[g][0:QB] + p_c[g][QB:2 * QB] + p_c[g][2 * QB:3 * QB] + p_c[g][3 * QB:4 * QB])
    selmt = selmt_ref[...]
    parts = each(lambda g: _split_bf16(psum[g], 3))
    imp = each(lambda g: sum(lax.dot_general(selmt, part, (((1,), (1,)), ((), ())), preferred_element_type=F32)
                             for part in parts[g]))
    ns = selmt.shape[0]
    blk = _iota((ns, QB), 0)
    cur = (t0 + _iota((ns, QB), 1)) >> 6
    forced = (blk == 0) | (blk == cur) | (blk == cur - 1)
    score = each(lambda g: jnp.where(forced, FORCE_SCORE, jnp.where(blk <= cur, imp[g], -1.0)))
    for g in groups:
        sc_scr[g] = score[g]
    ranks = [[] for _ in groups]
    for j in range(ns):
        lower = jnp.where(blk > j, 1.0, 0.0)
        for g in groups:
            other = sc_scr[g, j:j + 1, :]
            ranks[g].append(jnp.where(other > score[g], 1.0, 0.0) + jnp.where(other == score[g], lower, 0.0))
    while len(ranks[0]) > 1:
        ranks = [[a + b for a, b in zip(r[0::2], r[1::2])] + ([r[-1]] if len(r) % 2 else []) for r in ranks]
    chosen_t = each(lambda g: jnp.where(ranks[g][0] < n_sel, 1.0, 0.0))
    zrows = lambda n: jnp.zeros((n, QB), F32)
    chosen = each(lambda g: jnp.concatenate([zrows(BLOCK_LANE0), chosen_t[g], zrows(LANES - BLOCK_LANE0 - ns)],
                                            axis=0).T)
    is_block_lane = (lane >= BLOCK_LANE0) & (lane < BLOCK_LANE0 + ns)
    q_sel = each(lambda g: (q32[g] + tile4(jnp.where(is_block_lane, (chosen[g] - 1.0) * (-NEG_INF), pad_mask))
                            ).astype(BF16))

    def chunk_rows(c):
        return pl.ds(pl.multiple_of((i - 4 * c + 1) * QB, QB), SEL_CHUNK)

    s0 = each(lambda g: _dotb_nt(q_sel[g], ks_ref[0, g, chunk_rows(0), :]) + heads(stab_ref, g).reshape(R, SEL_CHUNK))
    tile_max = each(lambda g: _lane_tile_max(s0[g]))
    for c in range(1, n_far + 1):
        s = each(lambda g: _dotb_nt(q_sel[g], ks_ref[0, g, chunk_rows(c), :]))
        for g in groups:
            sbuf[g, c - 1] = s[g]
        tile_max = each(lambda g: jnp.maximum(tile_max[g], _lane_tile_max(s[g])))
    m_s = each(lambda g: jnp.max(tile_max[g], axis=-1, keepdims=True))

    def pv(g, c):
        s = s0[g] if c == 0 else sbuf[g, c - 1]
        return jnp.dot(jnp.exp(s - m_s[g]).astype(BF16), vs_ref[0, g, chunk_rows(c), :], preferred_element_type=F32)

    acc_s = each(lambda g: sum(pv(g, c) for c in range(n_far + 1)))

    is_value = _iota((R, LANES), 1) >= HEAD_DIM
    normalised = lambda acc: jnp.where(is_value, acc * (1.0 / pltpu.roll(acc, HEAD_DIM, 1)), 0.0)

    def natural(o):
        cat = jnp.concatenate([o[h * QB:(h + 1) * QB] for h in range(HG)], axis=1).astype(BF16)
        return jnp.dot(cat, gather_ref[...], preferred_element_type=F32)

    branches = each(lambda g: [o_c[g], normalised(acc_s[g]), normalised(acc_w[g])])
    gate = each(lambda g: [_dot_sel(gt_ref[0, g], grep_ref[c], 1) for c in range(3)])
    W = HG * HEAD_DIM
    for g in groups:
        out = sum(gate[g][c] * natural(branches[g][c]) for c in range(3))
        o_ref[:, g * W:(g + 1) * W] = out.astype(BF16)


def _nsa_attn(qp, kc, vc, ks, vs, kw, vw, gates, pn, B, T):
    G = NSA_KV_HEADS
    QB = QUERY_BLOCK
    NQ = T // QB
    ncp = kc.shape[2]
    ns = T // SEL_BLOCK
    TP = T + NSA_KPAD
    R = NSA_GROUP * QB
    n_sel = min(N_SEL, ns)
    max_far = (NQ - 1) // 4
    NG = NSA_GROUPS_PER_STEP
    HB = NSA_GROUP * NG
    once = pl.Buffered(1)
    gtab = lambda w: pl.BlockSpec((HB, QB, w), lambda b, g, i: (g, 0, 0), pipeline_mode=once)
    slab = lambda rows: pl.BlockSpec((1, NG, rows, LANES), lambda b, g, i: (b, g, 0, 0), pipeline_mode=once)
    vm = (NG * (2 * ncp * LANES * 2 + 4 * TP * LANES * 2 + R * (SEL_CHUNK + WIN_KEYS) * 4)
          + 2 * NG * (R * LANES * 2 + QB * LANES * 4 + R * ncp * 4 + QB * NSA_GROUP * HEAD_DIM * 2)
          + NG * ((max_far + 1) * R * SEL_CHUNK * 4 + 2 * R * LANES * 4 + 4 * R * WIN_KEYS * 4))
    return pl.pallas_call(
        functools.partial(_nsa_attn_kernel, n_sel=n_sel, max_far=max_far),
        grid=(B, G // NG, NQ),
        in_specs=[
            pl.BlockSpec((1, HB, QB, LANES), lambda b, g, i: (b, g, i, 0)),
            slab(ncp), slab(ncp), slab(TP), slab(TP), slab(TP), slab(TP),
            pl.BlockSpec((1, NG, QB, LANES), lambda b, g, i: (b, g, i, 0)),
            gtab(SEL_CHUNK), gtab(WIN_KEYS),
            pl.BlockSpec((HB, 1, QB, ncp), lambda b, g, i: (g, i, 0, 0)),
            pl.BlockSpec((ns, ncp), lambda b, g, i: (0, 0)),
            pl.BlockSpec((NSA_GROUP * LANES, NSA_GROUP * HEAD_DIM), lambda b, g, i: (0, 0)),
            pl.BlockSpec((3, LANES, NSA_GROUP * HEAD_DIM), lambda b, g, i: (0, 0, 0)),
        ],
        out_specs=pl.BlockSpec((QB, HB * HEAD_DIM), lambda b, g, i: (b * NQ + i, g)),
        out_shape=jax.ShapeDtypeStruct((B * T, NSA_WIDTH), BF16),
        scratch_shapes=[pltpu.VMEM((NG, max(max_far, 1), R, SEL_CHUNK), F32), pltpu.VMEM((NG, ns, QB), F32)],
        compiler_params=_cparams(("parallel", "parallel", "arbitrary"), vm),
        name="nsa_attn",
    )(qp, kc, vc, ks, vs, kw, vw, gates, pn["stab"], pn["wtab"], pn["ctab"], pn["sel_mt"], pn["gather"],
      pn["gate_rep"])


def _rel_bucket_table():
    n = np.arange(REL_MAX_DIST + 1)
    max_exact = REL_BUCKETS // 2
    nf = np.maximum(n, max_exact).astype(np.float32)
    large = max_exact + (np.log(nf / np.float32(max_exact)) / np.float32(math.log(REL_MAX_DIST / max_exact))
                         * np.float32(REL_BUCKETS - max_exact)).astype(np.int32)
    large = np.minimum(large, REL_BUCKETS - 1)
    return np.where(n < max_exact, n, large).astype(np.int32)


def _sel_to_cmp_matrix(T, ncp):
    nc = T // CMP_STRIDE - CMP_BLOCK // CMP_STRIDE + 1
    ns = T // SEL_BLOCK
    cs = np.arange(nc) * CMP_STRIDE
    ss = np.arange(ns) * SEL_BLOCK
    lo = np.maximum(cs[None, :], ss[:, None])
    hi = np.minimum(cs[None, :] + CMP_BLOCK, ss[:, None] + SEL_BLOCK)
    out = np.zeros((ns, ncp), np.float32)
    out[:, :nc] = np.maximum(hi - lo, 0) / CMP_BLOCK
    return out


def _bias_tables(rel_bias, rel, attend, shift=None):
    bucket = np.where(attend, _rel_bucket_table()[np.clip(rel, 0, REL_MAX_DIST)], REL_BUCKETS)
    rows = rel_bias if shift is None else rel_bias - shift[None, :]
    rows = jnp.concatenate([rows, jnp.full((1, rel_bias.shape[1]), NEG_INF, rel_bias.dtype)], axis=0)
    onehot = (jnp.asarray(bucket.reshape(1, -1)) == jnp.arange(REL_BUCKETS + 1, dtype=jnp.int32)[:, None]).astype(F32)
    tab = jnp.einsum('bh,bn->hn', rows, onehot, precision=HIGHEST)
    return tab.reshape((rel_bias.shape[1],) + rel.shape)


def _prep_in_proj_weight(w_in_all, l):
    D = w_in_all.shape[1]
    nsa0 = RWKV_COLS
    kv0 = nsa0 + NSA_WIDTH
    gates0 = kv0 + 6 * NSA_KV_WIDTH
    merge0 = RWKV_COLS + NSA_COLS
    wt = jnp.swapaxes(w_in_all[l], 0, 1)
    kv = wt[kv0:gates0].reshape(3, 2, NSA_KV_HEADS, HEAD_DIM, D)
    kv = jnp.transpose(kv, (2, 0, 1, 3, 4)).reshape(6 * NSA_KV_WIDTH, D)
    pad = jnp.zeros((SMALL_W - (SM_GATES + 3 * NSA_HEADS), D), wt.dtype)
    return jnp.concatenate([
        wt[0:3 * RWKV_WIDTH],
        wt[nsa0:kv0],
        wt[merge0:merge0 + 2 * D_MODEL],
        kv,
        wt[3 * RWKV_WIDTH:RWKV_COLS],
        wt[gates0:merge0],
        pad,
    ], axis=0).astype(BF16)


def _prep_rwkv_params(mu, w0, w2, a0, a2, g2, k_k, k_a, r_k, ln_w, ln_b):
    C = RWKV_WIDTH
    row = lambda z: z.reshape(1, -1).astype(F32)
    mu_s = jnp.concatenate([mu[3 * C:], jnp.zeros((SMALL_W - (RWKV_COLS - 3 * C),), F32)]).reshape(1, SMALL_W)
    zl = jnp.zeros((DECAY_LORA, C), F32)

    w2p = jnp.concatenate([w2, zl], axis=0)
    w2h = w2p.astype(BF16)
    return dict(
        mu_r=row(mu[0:C]), mu_k=row(mu[C:2 * C]), mu_v=row(mu[2 * C:3 * C]), mu_s=mu_s,
        w0=row(w0), a0=row(a0), k_k=row(k_k), k_a=row(k_a), r_k=row(r_k), ln_w=row(ln_w), ln_b=row(ln_b),
        w2h=w2h, w2l=(w2p - w2h.astype(F32)).astype(BF16),
        a2=jnp.concatenate([zl, a2], axis=0).astype(BF16),
        g2=jnp.concatenate([g2, jnp.zeros((256 - GATE_LORA, C), F32)], axis=0).astype(BF16),
    )


def _prep_nsa_params(pe_k, w1_k, w2_k, pe_v, w1_v, w2_v, q_g, k_g, rel_bias, T):
    hd = HEAD_DIM
    ones = jnp.ones((hd,), F32)
    ncp = T // CMP_STRIDE
    NQ = T // QUERY_BLOCK

    def blockdiag(a, b):
        lead = ((0, 0),) * (a.ndim - 2)
        return jnp.pad(a, lead + ((0, hd), (0, hd))) + jnp.pad(b, lead + ((hd, 0), (hd, 0)))

    w1 = blockdiag(w1_k.reshape(CMP_BLOCK, hd, hd), w1_v.reshape(CMP_BLOCK, hd, hd))
    pe = jnp.concatenate([pe_k, pe_v], axis=1)
    half = CMP_STRIDE

    spread = np.zeros((NSA_GROUP * hd, NSA_GROUP * LANES), np.float32)
    gather = np.zeros((NSA_GROUP * LANES, NSA_GROUP * hd), np.float32)
    for h in range(NSA_GROUP):
        for d in range(hd):
            spread[h * hd + d, h * LANES + d] = 1.0
            gather[h * LANES + hd + d, h * hd + d] = 1.0
    gate_sel = np.zeros((NSA_KV_HEADS, SMALL_W, LANES), np.float32)
    for g in range(NSA_KV_HEADS):
        for j in range(3 * NSA_GROUP):
            gate_sel[g, SM_GATES + 3 * NSA_GROUP * g + j, j] = 1.0
    gate_rep = np.zeros((3, LANES, NSA_GROUP * hd), np.float32)
    for c in range(3):
        for h in range(NSA_GROUP):
            gate_rep[c, 3 * h + c, h * hd:(h + 1) * hd] = 1.0

    qi = np.arange(QUERY_BLOCK)[:, None]
    rel_s = qi + (SEL_CHUNK - QUERY_BLOCK) - np.arange(SEL_CHUNK)[None, :]
    rel_w = qi + WINDOW - np.arange(WIN_KEYS)[None, :]
    rel_c = ((np.arange(NQ)[:, None, None] * QUERY_BLOCK + qi[None])
             - (np.arange(ncp)[None, None, :] * CMP_STRIDE + CMP_BLOCK - 1))
    far_bias = rel_bias[int(_rel_bucket_table()[REL_MAX_DIST])]
    stab = _bias_tables(rel_bias, rel_s, rel_s >= 0, shift=far_bias)
    wtab = _bias_tables(rel_bias, rel_w, (rel_w >= 0) & (rel_w < WINDOW))
    ctab = _bias_tables(rel_bias, rel_c, rel_c >= 0)
    return dict(
        stab=stab, wtab=wtab, ctab=ctab,
        q_g=jnp.tile(q_g, NSA_GROUP).reshape(1, -1),
        kg_cmp=jnp.concatenate([k_g[0], ones]).reshape(1, LANES),
        kg_sel=jnp.concatenate([k_g[1], ones]).reshape(1, LANES),
        kg_win=jnp.concatenate([k_g[2], ones]).reshape(1, LANES),
        pe_a=pe[:half].reshape(1, half * LANES), pe_b=pe[half:].reshape(1, half * LANES),
        cw_a=w1[:half].reshape(half * LANES, LANES).astype(BF16),
        cw_b=w1[half:].reshape(half * LANES, LANES).astype(BF16),
        cw2=blockdiag(w2_k, w2_v).astype(BF16),
        q_spread=jnp.asarray(spread, BF16), gather=jnp.asarray(gather, BF16),
        gate_sel=jnp.asarray(gate_sel), gate_rep=jnp.asarray(gate_rep, BF16),
        sel_mt=jnp.asarray(_sel_to_cmp_matrix(T, ncp)),
    )


def kernel(x, c, w_ada, b_ada, norm1_g, norm2_g, w_in, rwkv_mu, rwkv_w0, rwkv_w2, rwkv_a0, rwkv_a2, rwkv_g2, rwkv_k_k, rwkv_k_a, rwkv_r_k, rwkv_ln_w, rwkv_ln_b, cmp_pe_k, cmp_w1_k, cmp_w2_k, cmp_pe_v, cmp_w1_v, cmp_w2_v, q_norm_g, k_norm_g, rel_bias, w_o_rwkv, w_o_nsa, w_out, w_up, w_down):
    B, T, D = x.shape
    depth = w_in.shape[0]
    x2 = x.reshape(B * T, D)
    for l in range(depth):
        mod6 = _ada_mod(c, w_ada[l], b_ada[l]).reshape(B * 6, 1, D)
        proj, small = _in_proj(x2, norm1_g[l].reshape(1, D), mod6, _prep_in_proj_weight(w_in, l), T)
        pr = _prep_rwkv_params(rwkv_mu[l], rwkv_w0[l], rwkv_w2[l], rwkv_a0[l], rwkv_a2[l], rwkv_g2[l],
                               rwkv_k_k[l], rwkv_k_a[l], rwkv_r_k[l], rwkv_ln_w[l], rwkv_ln_b[l])
        o_a = _rwkv_mix(proj, small, pr, B, T)
        pn = _prep_nsa_params(cmp_pe_k[l], cmp_w1_k[l], cmp_w2_k[l], cmp_pe_v[l], cmp_w1_v[l], cmp_w2_v[l],
                              q_norm_g[l], k_norm_g[l], rel_bias, T)
        qp, ks, vs, kw, vw, gates = _nsa_prep(proj, small, pn, B, T)
        kc, vc = _nsa_compress(proj, pn, B, T)
        o_b = _nsa_attn(qp, kc, vc, ks, vs, kw, vw, gates, pn, B, T)
        mixed = _merge(o_a, o_b, w_o_rwkv[l].astype(BF16), w_o_nsa[l].astype(BF16), proj)
        x1, h2 = _out_proj(mixed, w_out[l].astype(BF16), x2, mod6, norm2_g[l].reshape(1, D), T)
        x2 = _mlp(h2, w_up[l].astype(BF16), w_down[l].astype(BF16), x1, mod6, T)
    return x2.reshape(B, T, D)
```

```python
import functools
import math

import numpy as np
import jax
import jax.numpy as jnp
from jax import lax
from jax.experimental import pallas as pl
from jax.experimental.pallas import tpu as pltpu

F32 = jnp.float32
BF16 = jnp.bfloat16
HIGHEST = lax.Precision.HIGHEST

D_MODEL = 2048
HEAD_DIM = 64
RWKV_WIDTH = D_MODEL // 2
DECAY_LORA = 64
ICLR_LORA = 64
GATE_LORA = 160
RWKV_GN_EPS = 64e-5
NSA_WIDTH = D_MODEL // 2
NSA_HEADS = NSA_WIDTH // HEAD_DIM
NSA_KV_HEADS = 4
NSA_GROUP = NSA_HEADS // NSA_KV_HEADS
NSA_KV_WIDTH = NSA_KV_HEADS * HEAD_DIM
CMP_BLOCK = 32
CMP_STRIDE = 16
SEL_BLOCK = 64
N_SEL = 8
WINDOW = 512
QUERY_BLOCK = 128
REL_BUCKETS = 32
REL_MAX_DIST = 128
D_FF = 4 * D_MODEL
NORM_EPS = 1e-6
NEG_INF = -1e30
FORCE_SCORE = 1e4

RWKV_COLS = 3 * RWKV_WIDTH + DECAY_LORA + ICLR_LORA + GATE_LORA
NSA_COLS = NSA_WIDTH + 6 * NSA_KV_WIDTH + 3 * NSA_HEADS

V7X_VMEM_BYTES = 64 * 1024 * 1024
LANES = 128

COL_RKV = 0
COL_Q = 3 * RWKV_WIDTH
COL_MERGE = COL_Q + NSA_WIDTH
COL_KV = COL_MERGE + 2 * D_MODEL
COL_SMALL = COL_KV + 6 * NSA_KV_WIDTH
SMALL_W = 512
PROJ_COLS = COL_SMALL + SMALL_W
SM_XG = DECAY_LORA + ICLR_LORA
SM_GATES = SM_XG + GATE_LORA


def _vmem_limit(nbytes):
    return int(min(nbytes * 5 // 4 + (4 << 20), V7X_VMEM_BYTES - (8 << 20)))


def _cparams(sem, vmem_bytes):
    return pltpu.CompilerParams(dimension_semantics=sem, vmem_limit_bytes=_vmem_limit(vmem_bytes))


def _ada_kernel(c_ref, w_ref, b_ref, o_ref):
    c = c_ref[...]
    s = c * jax.nn.sigmoid(c)
    o_ref[...] = jnp.dot(s.astype(BF16), w_ref[...].astype(BF16), preferred_element_type=F32) + b_ref[...]


def _ada_mod(c, w_ada, b_ada):
    B, D = c.shape
    N = w_ada.shape[1]
    tn = 1024
    return pl.pallas_call(
        _ada_kernel,
        grid=(N // tn,),
        in_specs=[
            pl.BlockSpec((B, D), lambda j: (0, 0)),
            pl.BlockSpec((D, tn), lambda j: (0, j)),
            pl.BlockSpec((1, tn), lambda j: (0, j)),
        ],
        out_specs=pl.BlockSpec((B, tn), lambda j: (0, j)),
        out_shape=jax.ShapeDtypeStruct((B, N), F32),
        compiler_params=_cparams(("parallel",), 2 * D * tn * 4 + D * tn * 2),
        name="ada_mod",
    )(c, w_ada, b_ada.reshape(1, N))


def _modulated_norm(x, g, sc, sh):
    ms = jnp.mean(x * x, axis=-1, keepdims=True)
    return (x * lax.rsqrt(ms + NORM_EPS) * g) * (1.0 + sc) + sh


def _inproj_kernel(x_ref, g_ref, sh_ref, sc_ref, w_ref, o_ref, small_ref, h_scr):
    j = pl.program_id(1)
    last = pl.num_programs(1) - 1

    def project(h):
        y = lax.dot_general(h, w_ref[...], (((1,), (1,)), ((), ())),
                            preferred_element_type=F32)
        o_ref[...] = y.astype(BF16)
        return y

    @pl.when(j == 0)
    def _():
        h = _modulated_norm(x_ref[...], g_ref[...], sc_ref[0], sh_ref[0]).astype(BF16)
        h_scr[...] = h
        project(h)

    @pl.when((j > 0) & (j < last))
    def _():
        project(h_scr[...])

    @pl.when(j == last)
    def _():
        y = project(h_scr[...])
        small_ref[...] = y[:, y.shape[1] - SMALL_W:]


def _in_proj(x2, g1, mod6, w_in_t, T):
    BT, D = x2.shape
    NP = w_in_t.shape[0]
    tm = min(1024, T)
    tn = 1024
    tpb = T // tm
    assert NP - SMALL_W == COL_SMALL and tn >= SMALL_W
    vm = (2 * tm * D * 4 + tm * D * 2 + 2 * D * tn * 2 + 2 * tm * tn * 2 + 2 * tm * SMALL_W * 4
          + tm * tn * 4 + 2 * tm * D * 4)
    return pl.pallas_call(
        _inproj_kernel,
        grid=(BT // tm, NP // tn),
        in_specs=[
            pl.BlockSpec((tm, D), lambda i, j: (i, 0)),
            pl.BlockSpec((1, D), lambda i, j: (0, 0)),
            pl.BlockSpec((1, 1, D), lambda i, j: ((i // tpb) * 6 + 0, 0, 0)),
            pl.BlockSpec((1, 1, D), lambda i, j: ((i // tpb) * 6 + 1, 0, 0)),
            pl.BlockSpec((tn, D), lambda i, j: (j, 0)),
        ],
        out_specs=[pl.BlockSpec((tm, tn), lambda i, j: (i, j)), pl.BlockSpec((tm, SMALL_W), lambda i, j: (i, 0))],
        out_shape=[jax.ShapeDtypeStruct((BT, NP), BF16), jax.ShapeDtypeStruct((BT, SMALL_W), F32)],
        scratch_shapes=[pltpu.VMEM((tm, D), BF16)],
        compiler_params=_cparams(("parallel", "arbitrary"), vm),
        name="in_proj",
    )(x2, g1, mod6, mod6, w_in_t)


def _merge_kernel(oa_ref, ob_ref, wa_ref, wb_ref, ga_ref, gb_ref, o_ref):
    ya = jnp.dot(oa_ref[...], wa_ref[...], preferred_element_type=F32)
    yb = jnp.dot(ob_ref[...], wb_ref[...], preferred_element_type=F32)
    ga, gb = ga_ref[...].astype(F32), gb_ref[...].astype(F32)
    o_ref[...] = (jax.nn.sigmoid(ga) * ya + jax.nn.sigmoid(gb) * yb).astype(BF16)


def _merge(o_a, o_b, w_oa, w_ob, proj):
    BT, W = o_a.shape
    D = w_oa.shape[1]
    tm, tn = 512, 1024
    ga0 = COL_MERGE // tn
    gb0 = (COL_MERGE + D) // tn
    vm = 2 * (2 * tm * W * 2 + 2 * W * tn * 2 + 2 * tm * tn * 4 + tm * tn * 2) + 3 * tm * tn * 4
    return pl.pallas_call(
        _merge_kernel,
        grid=(BT // tm, D // tn),
        in_specs=[
            pl.BlockSpec((tm, W), lambda i, j: (i, 0)),
            pl.BlockSpec((tm, W), lambda i, j: (i, 0)),
            pl.BlockSpec((W, tn), lambda i, j: (0, j)),
            pl.BlockSpec((W, tn), lambda i, j: (0, j)),
            pl.BlockSpec((tm, tn), lambda i, j: (i, ga0 + j)),
            pl.BlockSpec((tm, tn), lambda i, j: (i, gb0 + j)),
        ],
        out_specs=pl.BlockSpec((tm, tn), lambda i, j: (i, j)),
        out_shape=jax.ShapeDtypeStruct((BT, D), BF16),
        compiler_params=_cparams(("parallel", "parallel"), vm),
        name="merge",
    )(o_a, o_b, w_oa, w_ob, proj, proj)


def _outproj_kernel(m_ref, w_ref, x_ref, gt_ref, g_ref, sh_ref, sc_ref, x1_ref, h2_ref):
    y = jnp.dot(m_ref[...], w_ref[...], preferred_element_type=F32)
    x1 = x_ref[...] + gt_ref[0] * y
    x1_ref[...] = x1
    h2_ref[...] = _modulated_norm(x1, g_ref[...], sc_ref[0], sh_ref[0]).astype(BF16)


def _out_proj(mixed, w_out, x2, mod6, g2, T):
    BT, D = x2.shape
    tm = min(512, T)
    tpb = T // tm
    vm = 2 * (tm * D * 2 + D * D * 2 + tm * D * 4 + tm * D * 4 + tm * D * 2) + 3 * tm * D * 4
    mod_spec = lambda k: pl.BlockSpec((1, 1, D), lambda i: ((i // tpb) * 6 + k, 0, 0))
    return pl.pallas_call(
        _outproj_kernel,
        grid=(BT // tm,),
        in_specs=[
            pl.BlockSpec((tm, D), lambda i: (i, 0)),
            pl.BlockSpec((D, D), lambda i: (0, 0)),
            pl.BlockSpec((tm, D), lambda i: (i, 0)),
            mod_spec(2),
            pl.BlockSpec((1, D), lambda i: (0, 0)),
            mod_spec(3),
            mod_spec(4),
        ],
        out_specs=[pl.BlockSpec((tm, D), lambda i: (i, 0)), pl.BlockSpec((tm, D), lambda i: (i, 0))],
        out_shape=[jax.ShapeDtypeStruct((BT, D), F32), jax.ShapeDtypeStruct((BT, D), BF16)],
        compiler_params=_cparams(("parallel",), vm),
        name="out_proj",
    )(mixed, w_out, x2, mod6, g2, mod6, mod6)


OUT_CHUNK = 256


def _mlp_kernel(h_ref, wu_ref, wd_ref, x_ref, gt_ref, o_ref, acc_ref):
    f = pl.program_id(1)
    last = pl.num_programs(1) - 1

    def step(first, final):
        u = jnp.dot(h_ref[...], wu_ref[...], preferred_element_type=F32)
        u = jnp.square(jnp.maximum(u, 0.0)).astype(BF16)
        gt = gt_ref[0]
        for n in range(acc_ref.shape[1] // OUT_CHUNK):
            cs = slice(n * OUT_CHUNK, (n + 1) * OUT_CHUNK)
            part = jnp.dot(u, wd_ref[:, cs], preferred_element_type=F32)
            if first:
                acc_ref[:, cs] = part
            elif final:
                o_ref[:, cs] = x_ref[:, cs] + gt[:, cs] * (acc_ref[:, cs] + part)
            else:
                acc_ref[:, cs] += part

    pl.when(f == 0)(functools.partial(step, True, False))
    pl.when((f > 0) & (f < last))(functools.partial(step, False, False))
    pl.when(f == last)(functools.partial(step, False, True))


def _mlp(h2, w_up, w_down, x1, mod6, T):
    BT, D = x1.shape
    F = w_up.shape[1]
    tm = min(512, T)
    tf = 1024
    tpb = T // tm
    vm = 2 * (tm * D * 2 + 2 * D * tf * 2 + 2 * tm * D * 4) + tm * D * 4 + 2 * tm * tf * 4
    return pl.pallas_call(
        _mlp_kernel,
        grid=(BT // tm, F // tf),
        in_specs=[
            pl.BlockSpec((tm, D), lambda i, f: (i, 0)),
            pl.BlockSpec((D, tf), lambda i, f: (0, f)),
            pl.BlockSpec((tf, D), lambda i, f: (f, 0)),
            pl.BlockSpec((tm, D), lambda i, f: (i, 0)),
            pl.BlockSpec((1, 1, D), lambda i, f: ((i // tpb) * 6 + 5, 0, 0)),
        ],
        out_specs=pl.BlockSpec((tm, D), lambda i, f: (i, 0)),
        out_shape=jax.ShapeDtypeStruct((BT, D), F32),
        scratch_shapes=[pltpu.VMEM((tm, D), F32)],
        compiler_params=_cparams(("parallel", "arbitrary"), vm),
        name="mlp",
    )(h2, w_up, w_down, x1, mod6)


def _dotb(a, b):
    return jnp.dot(a.astype(BF16), b.astype(BF16), preferred_element_type=F32)


def _dotb_nt(a, b):
    return lax.dot_general(a.astype(BF16), b.astype(BF16), (((1,), (1,)), ((), ())),
                           preferred_element_type=F32)


def _split_bf16(x, terms):
    parts, rem = [], x
    for t in range(terms):
        p = rem.astype(BF16)
        parts.append(p)
        if t + 1 < terms:
            rem = rem - p.astype(F32)
    return parts


def _dot_sel(x, sel, terms):
    sel = sel.astype(BF16)
    return sum(jnp.dot(p, sel, preferred_element_type=F32) for p in _split_bf16(x, terms))


def _sel_dot(sel, x, terms):
    sel = sel.astype(BF16)
    return sum(jnp.dot(sel, p, preferred_element_type=F32) for p in _split_bf16(x, terms))


def _dot3(a, b_hi, b_lo):
    a_hi, a_lo = _split_bf16(a, 2)
    return (jnp.dot(a_hi, b_hi, preferred_element_type=F32) + jnp.dot(a_lo, b_hi, preferred_element_type=F32)
            + jnp.dot(a_hi, b_lo, preferred_element_type=F32))


def _iota(shape, axis):
    return lax.broadcasted_iota(jnp.int32, shape, axis)


def _same_head_mask(n):
    return (_iota((n, n), 0) >> 6) == (_iota((n, n), 1) >> 6)


RW_TC = 256
RW_C = 64
RW_LANES = 4 * HEAD_DIM


def _block_diag(x, bdmask):
    return jnp.where(bdmask, jnp.concatenate([x, x, x, x], axis=0), 0.0)


def _rwkv_kernel(r_ref, k_ref, v_ref, s_ref, mur_ref, muk_ref, muv_ref, mus_ref,
                 w0_ref, a0_ref, kk_ref, ka_ref, rk_ref, lnw_ref, lnb_ref,
                 w2h_ref, w2l_ref, a2_ref, g2_ref, o_ref,
                 pr_scr, pk_scr, pv_scr, ps_scr, state_scr):
    TC = r_ref.shape[0]
    C = RW_C
    W = RW_LANES

    @pl.when(pl.program_id(1) == 0)
    def _():
        pr_scr[...] = jnp.zeros_like(pr_scr)
        pk_scr[...] = jnp.zeros_like(pk_scr)
        pv_scr[...] = jnp.zeros_like(pv_scr)
        ps_scr[...] = jnp.zeros_like(ps_scr)
        state_scr[...] = jnp.zeros_like(state_scr)

    def shift_mix(p_ref, prev_scr, mu_ref):
        p = p_ref[...].astype(F32)
        rolled = pltpu.roll(p, 1, 0)
        first = jnp.where(_iota((8, p.shape[1]), 0) == 0, prev_scr[0:1, :], rolled[0:8])
        shifted = jnp.concatenate([first, rolled[8:]], axis=0)
        prev_scr[0:1, :] = p[TC - 1:TC, :]
        return p + (shifted - p) * mu_ref[...]

    CW = r_ref.shape[1]
    G = CW // W
    NQ = TC // C
    groups = lambda x: [x[:, g * W:(g + 1) * W] for g in range(G)]
    per_group = lambda f, x: jnp.concatenate([f(xg) for xg in groups(x)], axis=1)

    r = shift_mix(r_ref, pr_scr, mur_ref)
    k = shift_mix(k_ref, pk_scr, muk_ref)
    v = shift_mix(v_ref, pv_scr, muv_ref)
    sm = shift_mix(s_ref, ps_scr, mus_ref)
    xwa = sm[:, 0:SM_XG]

    bdmask = _same_head_mask(W)
    bones = jnp.where(bdmask, 1.0, 0.0)
    head_sum = lambda x: per_group(lambda xg: _dot_sel(xg, bones, 1), x)
    tri = jnp.where(_same_head_mask(TC) & (_iota((TC, TC), 1) <= _iota((TC, TC), 0)), 1.0, 0.0)
    lane = _iota((C, W), 1)
    row = _iota((C, W), 0)
    lanehead = lane >> 6
    strict = (lane & 63) < row
    incl = (lane & 63) <= row
    eye_cat = jnp.where((lane & 63) == row, 1.0, 0.0)

    wlin = w0_ref[...] + _dot3(jnp.tanh(xwa), w2h_ref[...], w2l_ref[...])
    a = jax.nn.sigmoid(a0_ref[...] + _dotb(xwa, a2_ref[...]))
    gate = _dotb(jax.nn.sigmoid(sm[:, SM_XG:SM_XG + 256]), g2_ref[...])
    z = -wlin
    softplus = jnp.maximum(z, 0.0) + jnp.log(1.0 + jnp.exp(-jnp.abs(z)))
    ld = -jnp.exp(-softplus - 0.5)
    cum = _sel_dot(tri, ld, 3)
    cum_last = jnp.concatenate(
        [jnp.broadcast_to(cum[(q + 1) * C - 1:(q + 1) * C, :], (C, CW)) for q in range(NQ)], axis=0)
    kk = k * kk_ref[...]
    kk = kk * lax.rsqrt(jnp.maximum(head_sum(kk * kk), 1e-24))
    k2 = k * (1.0 + (a - 1.0) * ka_ref[...])
    bvec = kk * a
    e_inv = jnp.exp(-cum)
    e_end = jnp.exp(cum_last - cum)
    At_f = -kk * jnp.exp(cum - ld)
    Rt_f = r * jnp.exp(cum)
    Bt_f = bvec * e_inv
    Kt_f = k2 * e_inv
    Bg_f = bvec * e_end
    Kg_f = k2 * e_end
    g_end = jnp.exp(cum_last)

    items = [(g, q) for q in range(NQ) for g in range(G)]
    blk = lambda x, g, q: x[q * C:(q + 1) * C, g * W:(g + 1) * W]
    bd01 = jnp.where(bdmask, 1.0, 0.0).astype(BF16)

    def bd(x):
        xb = x.astype(BF16)
        return jnp.concatenate([xb, xb, xb, xb], axis=0) * bd01

    At = {it: blk(At_f, *it) for it in items}
    Rt = {it: blk(Rt_f, *it) for it in items}
    Vq = {it: blk(v, *it) for it in items}
    AA = {}
    for it in items:
        bk = jnp.concatenate([jnp.where(lanehead == h, X, 0.0)
                              for X in (blk(Bt_f, *it), blk(Kt_f, *it)) for h in range(4)], axis=0)
        AA[it] = _dotb_nt(jnp.concatenate([At[it], Rt[it]], axis=0), bk)
    A_ab = {it: jnp.where(strict, AA[it][0:C, 0:W], 0.0) for it in items}
    A_ak = {it: jnp.where(strict, AA[it][0:C, W:2 * W], 0.0) for it in items}
    A_rb = {it: jnp.where(incl, AA[it][C:2 * C, 0:W], 0.0) for it in items}
    A_rk = {it: jnp.where(incl, AA[it][C:2 * C, W:2 * W], 0.0) for it in items}
    M = dict(A_ab)
    Tm = {it: eye_cat + A_ab[it] for it in items}
    for _ in range(5):
        M = {it: _dotb(M[it], bd(M[it])) for it in items}
        Tm = {it: Tm[it] + _dotb(M[it], bd(Tm[it])) for it in items}
    Vbd = {it: bd(Vq[it]) for it in items}
    akv = {it: _dotb(A_ak[it], Vbd[it]) for it in items}
    rkv = {it: _dotb(A_rk[it], Vbd[it]) for it in items}

    S = [state_scr[g] for g in range(G)]
    ys = []
    for q in range(NQ):
        its = [(g, q) for g in range(G)]
        rhs = [_dotb_nt(At[it], S[it[0]]) + akv[it] for it in its]
        U = [_dotb(Tm[it], bd(rhs[g])) for g, it in enumerate(its)]
        ys.append(jnp.concatenate(
            [_dotb_nt(Rt[it], S[g]) + _dotb(A_rb[it], bd(U[g])) + rkv[it] for g, it in enumerate(its)], axis=1))
        upd = [_dotb(jnp.concatenate([U[g], Vq[it]], axis=0).T,
                     jnp.concatenate([blk(Bg_f, *it), blk(Kg_f, *it)], axis=0)) for g, it in enumerate(its)]
        S = [S[g] * blk(g_end, g, q)[0:1, :] + jnp.where(bdmask, upd[g], 0.0) for g in range(G)]
    for g in range(G):
        state_scr[g] = S[g]
    y = jnp.concatenate(ys, axis=0)

    inv_n = 1.0 / HEAD_DIM
    d = y - head_sum(y) * inv_n
    var = head_sum(d * d) * inv_n
    yn = d * lax.rsqrt(var + RWKV_GN_EPS) * lnw_ref[...] + lnb_ref[...]
    bonus = head_sum(r * k2 * rk_ref[...]) * v
    o_ref[...] = ((yn + bonus) * gate).astype(BF16)


def _rwkv_mix(proj, small, pr, B, T):
    BT = proj.shape[0]
    TC = min(RW_TC, T)
    W = RW_LANES
    CW = RWKV_WIDTH
    nct = T // TC
    row = lambda b, c: b * nct + c
    full = lambda shape: pl.BlockSpec(shape, lambda b, c: (0, 0))
    vec = full((1, CW))
    in_specs = [
        pl.BlockSpec((TC, CW), lambda b, c: (row(b, c), 0)),
        pl.BlockSpec((TC, CW), lambda b, c: (row(b, c), 1)),
        pl.BlockSpec((TC, CW), lambda b, c: (row(b, c), 2)),
        pl.BlockSpec((TC, SMALL_W), lambda b, c: (row(b, c), 0)),
        vec, vec, vec,
        full((1, SMALL_W)),
        vec, vec, vec, vec, vec, vec, vec,
        full((SM_XG, CW)), full((SM_XG, CW)), full((SM_XG, CW)), full((256, CW)),
    ]
    vm = (2 * (3 * TC * CW * 4 + TC * SMALL_W * 4 + TC * CW * 2 + (3 * SM_XG + 256) * CW * 2)
          + 24 * TC * CW * 4)
    return pl.pallas_call(
        _rwkv_kernel,
        grid=(B, nct),
        in_specs=in_specs,
        out_specs=pl.BlockSpec((TC, CW), lambda b, c: (row(b, c), 0)),
        out_shape=jax.ShapeDtypeStruct((BT, CW), BF16),
        scratch_shapes=[pltpu.VMEM((8, CW), F32), pltpu.VMEM((8, CW), F32), pltpu.VMEM((8, CW), F32),
                        pltpu.VMEM((8, SMALL_W), F32), pltpu.VMEM((CW // W, W, W), F32)],
        compiler_params=_cparams(("parallel", "arbitrary"), vm),
        name="rwkv_mix",
    )(proj, proj, proj, small, pr["mu_r"], pr["mu_k"], pr["mu_v"], pr["mu_s"],
      pr["w0"], pr["a0"], pr["k_k"], pr["k_a"], pr["r_k"], pr["ln_w"], pr["ln_b"],
      pr["w2h"], pr["w2l"], pr["a2"], pr["g2"])


NSA_TT = 256
BLOCK_LANE0 = HEAD_DIM
PADFLAG_LANE = BLOCK_LANE0 + 32


def _nsa_prep_kernel(q_ref, kv0_ref, kv1_ref, kv2_ref, s_ref, qg_ref, kgs_ref, kgw_ref, e_ref, gsel_ref,
                     qp_ref, ksl_ref, vsl_ref, kwl_ref, vwl_ref, gt_ref):
    tt = q_ref.shape[0]
    n_pad = NSA_KPAD // tt
    step = pl.program_id(1)
    lane = _iota((tt, LANES), 1)

    @pl.when(step < n_pad)
    def _():
        flag = jnp.broadcast_to(jnp.where(lane == PADFLAG_LANE, 1.0, 0.0).astype(BF16), ksl_ref.shape[1:])
        ksl_ref[0] = flag
        kwl_ref[0] = flag
        vsl_ref[0] = jnp.zeros_like(flag)
        vwl_ref[0] = jnp.zeros_like(flag)

    @pl.when(step >= n_pad)
    def _():
        _nsa_prep_tile(step - n_pad, lane, q_ref, (kv0_ref, kv1_ref, kv2_ref), s_ref, qg_ref, kgs_ref, kgw_ref,
                       e_ref, gsel_ref, qp_ref, ksl_ref, vsl_ref, kwl_ref, vwl_ref, gt_ref)


def _nsa_prep_tile(tile, lane, q_ref, kv_refs, s_ref, qg_ref, kgs_ref, kgw_ref, e_ref, gsel_ref,
                   qp_ref, ksl_ref, vsl_ref, kwl_ref, vwl_ref, gt_ref):
    tt = q_ref.shape[0]
    QW = NSA_GROUP * HEAD_DIM
    bones = jnp.where(_same_head_mask(QW), 1.0, 0.0)
    bones2 = jnp.where(_same_head_mask(LANES), 1.0, 0.0)
    block_id = (tile * tt + _iota((tt, LANES), 0)) >> 6
    onehot = jnp.where(lane == block_id + BLOCK_LANE0, 1.0, 0.0)
    gates = jax.nn.sigmoid(_dot_sel(s_ref[...], gsel_ref[...], 2))

    def pair(g, branch):
        off = (3 * g + branch) * LANES
        return kv_refs[off // SMALL_W][:, off % SMALL_W:off % SMALL_W + LANES].astype(F32)

    def slabs(g, x, gain, k_ref, v_ref):
        ms = _dot_sel(x * x, bones2, 1) * (1.0 / HEAD_DIM)
        k_ref[0, g] = jnp.where(lane < HEAD_DIM, x * lax.rsqrt(ms + NORM_EPS) * gain, onehot).astype(BF16)
        v_ref[0, g] = jnp.where(lane >= HEAD_DIM, x, 1.0).astype(BF16)

    for g in range(NSA_KV_HEADS):
        q = q_ref[:, g * QW:(g + 1) * QW].astype(F32)
        ms = _dot_sel(q * q, bones, 1) * (1.0 / HEAD_DIM)
        qn = (q * lax.rsqrt(ms + NORM_EPS) * qg_ref[...]) * (HEAD_DIM ** -0.5)
        qs = jnp.dot(qn.astype(BF16), e_ref[...], preferred_element_type=F32).astype(BF16)
        for h in range(NSA_GROUP):
            qp_ref[0, NSA_GROUP * g + h] = qs[:, h * LANES:(h + 1) * LANES]
        slabs(g, pair(g, 1), kgs_ref[...], ksl_ref, vsl_ref)
        slabs(g, pair(g, 2), kgw_ref[...], kwl_ref, vwl_ref)
        gt_ref[0, g] = gates[:, g * LANES:(g + 1) * LANES]


def _nsa_prep(proj, small, pn, B, T):
    tt = min(NSA_TT, T)
    ntt = T // tt
    G = NSA_KV_HEADS
    QW = NSA_GROUP * HEAD_DIM
    n_pad = NSA_KPAD // tt
    tile = lambda t: jnp.maximum(t - n_pad, 0)
    row = lambda b, t: b * ntt + tile(t)
    full = lambda shape: pl.BlockSpec(shape, lambda b, t: tuple(0 for _ in shape))
    kv_spec = lambda j: pl.BlockSpec((tt, SMALL_W), lambda b, t: (row(b, t), COL_KV // SMALL_W + j))
    slab = pl.BlockSpec((1, G, tt, LANES), lambda b, t: (b, 0, t, 0))
    vm = 2 * (tt * NSA_WIDTH * 4 + 4 * tt * SMALL_W * 4 + QW * 4 * LANES * 2 + G * SMALL_W * LANES * 4
              + 16 * tt * LANES * 2 + 16 * tt * LANES * 2 + 4 * tt * LANES * 4) + 24 * tt * QW * 4
    return pl.pallas_call(
        _nsa_prep_kernel,
        grid=(B, ntt + n_pad),
        in_specs=[
            pl.BlockSpec((tt, NSA_WIDTH), lambda b, t: (row(b, t), COL_Q // NSA_WIDTH)),
            kv_spec(0), kv_spec(1), kv_spec(2),
            pl.BlockSpec((tt, SMALL_W), lambda b, t: (row(b, t), 0)),
            full((1, QW)), full((1, LANES)), full((1, LANES)), full((QW, NSA_GROUP * LANES)),
            full((SMALL_W, G * LANES)),
        ],
        out_specs=[pl.BlockSpec((1, NSA_HEADS, tt, LANES), lambda b, t: (b, 0, tile(t), 0)),
                   slab, slab, slab, slab,
                   pl.BlockSpec((1, G, tt, LANES), lambda b, t: (b, 0, tile(t), 0))],
        out_shape=[jax.ShapeDtypeStruct((B, NSA_HEADS, T, LANES), BF16)]
        + [jax.ShapeDtypeStruct((B, G, T + NSA_KPAD, LANES), BF16)] * 4
        + [jax.ShapeDtypeStruct((B, G, T, LANES), F32)],
        compiler_params=_cparams(("parallel", "arbitrary"), vm),
        name="nsa_prep",
    )(proj, proj, proj, proj, small, pn["q_g"], pn["kg_sel"], pn["kg_win"], pn["q_spread"], pn["gate_sel"])


def _gelu_tanh(x):
    return 0.5 * x * (1.0 + jnp.tanh(math.sqrt(2.0 / math.pi) * (x + 0.044715 * (x * x * x))))


def _nsa_compress_kernel(x_ref, pea_ref, peb_ref, wa_ref, wb_ref, w2_ref, kg_ref, kc_ref, vc_ref, x_scr):
    nsub = x_ref.shape[0] // CMP_STRIDE
    x_scr[...] = x_ref[...].astype(F32)
    xs = jnp.concatenate([x_scr[pl.ds(s, nsub, stride=CMP_STRIDE), :] for s in range(CMP_STRIDE)], axis=1)
    p0 = jnp.dot((xs + pea_ref[...]).astype(BF16), wa_ref[...], preferred_element_type=F32)
    p1 = jnp.dot((xs + peb_ref[...]).astype(BF16), wb_ref[...], preferred_element_type=F32)
    hid = _gelu_tanh(p0 + pltpu.roll(p1, nsub - 1, 0))
    out = jnp.dot(hid.astype(BF16), w2_ref[...], preferred_element_type=F32)
    bones2 = jnp.where(_same_head_mask(LANES), 1.0, 0.0)
    ms = _dot_sel(out * out, bones2, 2) * (1.0 / HEAD_DIM)
    is_k = _iota(out.shape, 1) < HEAD_DIM
    kc_ref[0, 0] = jnp.where(is_k, out * lax.rsqrt(ms + NORM_EPS) * kg_ref[...], 0.0).astype(BF16)
    vc_ref[0, 0] = jnp.where(is_k, 0.0, out).astype(BF16)


def _nsa_compress(proj, pn, B, T):
    G = NSA_KV_HEADS
    nsub = T // CMP_STRIDE
    kv_blk = COL_KV // LANES
    KW = CMP_STRIDE * LANES
    full = lambda shape: pl.BlockSpec(shape, lambda b, g: tuple(0 for _ in shape))
    vm = 2 * (T * LANES * 4 + 2 * KW * LANES * 2 + nsub * LANES * 2) + 6 * nsub * KW * 4
    return pl.pallas_call(
        _nsa_compress_kernel,
        grid=(B, G),
        in_specs=[
            pl.BlockSpec((T, LANES), lambda b, g: (b, kv_blk + 3 * g)),
            full((1, KW)), full((1, KW)), full((KW, LANES)), full((KW, LANES)), full((LANES, LANES)),
            full((1, LANES)),
        ],
        out_specs=[pl.BlockSpec((1, 1, nsub, LANES), lambda b, g: (b, g, 0, 0))] * 2,
        out_shape=[jax.ShapeDtypeStruct((B, G, nsub, LANES), BF16)] * 2,
        scratch_shapes=[pltpu.VMEM((T, LANES), F32)],
        compiler_params=_cparams(("parallel", "parallel"), vm),
        name="nsa_compress",
    )(proj, pn["pe_a"], pn["pe_b"], pn["cw_a"], pn["cw_b"], pn["cw2"], pn["kg_cmp"])


NSA_GROUPS_PER_STEP = 4
NSA_KPAD = WINDOW
SEL_CHUNK = 4 * QUERY_BLOCK
WIN_KEYS = WINDOW + QUERY_BLOCK


def _lane_tile_max(s):
    tiles = [s[:, j * LANES:(j + 1) * LANES] for j in range(s.shape[1] // LANES)]
    while len(tiles) > 1:
        tiles = [jnp.maximum(a, b) for a, b in zip(tiles[0::2], tiles[1::2])] + ([tiles[-1]] if len(tiles) % 2 else [])
    return tiles[0]


def _nsa_attn_kernel(*refs, n_sel, max_far):
    n_far = pl.program_id(2) // 4
    for k in range(max_far + 1):
        pl.when(n_far == k)(functools.partial(_nsa_attn_step, k, *refs, n_sel=n_sel))


def _nsa_attn_step(n_far, q_ref, kc_ref, vc_ref, ks_ref, vs_ref, kw_ref, vw_ref, gt_ref, stab_ref, wtab_ref, ctab_ref,
                   selmt_ref, gather_ref, grep_ref, o_ref, sbuf, sc_scr, *, n_sel):
    i = pl.program_id(2)
    QB = QUERY_BLOCK
    HG = NSA_GROUP
    R = HG * QB
    t0 = i * QB
    groups = range(kc_ref.shape[1])
    each = lambda f: [f(g) for g in groups]
    tile4 = lambda z: jnp.concatenate([z, z, z, z], axis=0)
    heads = lambda ref, g: ref[HG * g:HG * (g + 1)]
    lane = _iota((QB, LANES), 1)
    pad_mask = jnp.where(lane == PADFLAG_LANE, NEG_INF, 0.0)
    q = each(lambda g: q_ref[0, HG * g:HG * (g + 1)].reshape(R, LANES))
    q32 = each(lambda g: q[g].astype(F32))

    ncp = kc_ref.shape[2]
    s = each(lambda g: _dotb_nt(q[g], kc_ref[0, g]) + heads(ctab_ref, g)[:, 0].reshape(R, ncp))
    m = each(lambda g: jnp.max(s[g], axis=-1, keepdims=True))
    p_c = each(lambda g: jnp.exp(s[g] - m[g]))
    lsum = each(lambda g: jnp.sum(p_c[g], axis=-1, keepdims=True))
    p_c = each(lambda g: p_c[g] * jnp.where(m[g] > 0.5 * NEG_INF, 1.0 / lsum[g], 0.0))
    o_c = each(lambda g: jnp.dot(p_c[g].astype(BF16), vc_ref[0, g], preferred_element_type=F32))

    wrows = pl.ds(pl.multiple_of(t0, QB), WIN_KEYS)
    q_win = each(lambda g: (q32[g] + tile4(pad_mask)).astype(BF16))
    s = each(lambda g: _dotb_nt(q_win[g], kw_ref[0, g, wrows, :]) + heads(wtab_ref, g).reshape(R, WIN_KEYS))
    p = each(lambda g: jnp.exp(s[g] - jnp.max(_lane_tile_max(s[g]), axis=-1, keepdims=True)))
    acc_w = each(lambda g: jnp.dot(p[g].astype(BF16), vw_ref[0, g, wrows, :], preferred_element_type=F32))

    psum = each(lambda g: p_c[g][0:QB] + p_c[g][QB:2 * QB] + p_c[g][2 * QB:3 * QB] + p_c[g][3 * QB:4 * QB])
    selmt = selmt_ref[...]
    parts = each(lambda g: _split_bf16(psum[g], 3))
    imp = each(lambda g: sum(lax.dot_general(selmt, part, (((1,), (1,)), ((), ())), preferred_element_type=F32)
                             for part in parts[g]))
    ns = selmt.shape[0]
    blk = _iota((ns, QB), 0)
    cur = (t0 + _iota((ns, QB), 1)) >> 6
    forced = (blk == 0) | (blk == cur) | (blk == cur - 1)
    score = each(lambda g: jnp.where(forced, FORCE_SCORE, jnp.where(blk <= cur, imp[g], -1.0)))
    for g in groups:
        sc_scr[g] = score[g]
    ranks = [[] for _ in groups]
    for j in range(ns):
        lower = jnp.where(blk > j, 1.0, 0.0)
        for g in groups:
            other = sc_scr[g, j:j + 1, :]
            ranks[g].append(jnp.where(other > score[g], 1.0, 0.0) + jnp.where(other == score[g], lower, 0.0))
    while len(ranks[0]) > 1:
        ranks = [[a + b for a, b in zip(r[0::2], r[1::2])] + ([r[-1]] if len(r) % 2 else []) for r in ranks]
    chosen_t = each(lambda g: jnp.where(ranks[g][0] < n_sel, 1.0, 0.0))
    zrows = lambda n: jnp.zeros((n, QB), F32)
    chosen = each(lambda g: jnp.concatenate([zrows(BLOCK_LANE0), chosen_t[g], zrows(LANES - BLOCK_LANE0 - ns)],
                                            axis=0).T)
    is_block_lane = (lane >= BLOCK_LANE0) & (lane < BLOCK_LANE0 + ns)
    q_sel = each(lambda g: (q32[g] + tile4(jnp.where(is_block_lane, (chosen[g] - 1.0) * (-NEG_INF), pad_mask))
                            ).astype(BF16))

    def chunk_rows(c):
        return pl.ds(pl.multiple_of((i - 4 * c + 1) * QB, QB), SEL_CHUNK)

    s0 = each(lambda g: _dotb_nt(q_sel[g], ks_ref[0, g, chunk_rows(0), :]) + heads(stab_ref, g).reshape(R, SEL_CHUNK))
    tile_max = each(lambda g: _lane_tile_max(s0[g]))
    for c in range(1, n_far + 1):
        s = each(lambda g: _dotb_nt(q_sel[g], ks_ref[0, g, chunk_rows(c), :]))
        for g in groups:
            sbuf[g, c - 1] = s[g]
        tile_max = each(lambda g: jnp.maximum(tile_max[g], _lane_tile_max(s[g])))
    m_s = each(lambda g: jnp.max(tile_max[g], axis=-1, keepdims=True))

    def pv(g, c):
        s = s0[g] if c == 0 else sbuf[g, c - 1]
        return jnp.dot(jnp.exp(s - m_s[g]).astype(BF16), vs_ref[0, g, chunk_rows(c), :], preferred_element_type=F32)

    acc_s = each(lambda g: sum(pv(g, c) for c in range(n_far + 1)))

    is_value = _iota((R, LANES), 1) >= HEAD_DIM
    normalised = lambda acc: jnp.where(is_value, acc * (1.0 / pltpu.roll(acc, HEAD_DIM, 1)), 0.0)

    def natural(o):
        cat = jnp.concatenate([o[h * QB:(h + 1) * QB] for h in range(HG)], axis=1).astype(BF16)
        return jnp.dot(cat, gather_ref[...], preferred_element_type=F32)

    branches = each(lambda g: [o_c[g], normalised(acc_s[g]), normalised(acc_w[g])])
    gate = each(lambda g: [_dot_sel(gt_ref[0, g], grep_ref[c], 1) for c in range(3)])
    W = HG * HEAD_DIM
    for g in groups:
        out = sum(gate[g][c] * natural(branches[g][c]) for c in range(3))
        o_ref[:, g * W:(g + 1) * W] = out.astype(BF16)


def _nsa_attn(qp, kc, vc, ks, vs, kw, vw, gates, pn, B, T):
    G = NSA_KV_HEADS
    QB = QUERY_BLOCK
    NQ = T // QB
    ncp = kc.shape[2]
    ns = T // SEL_BLOCK
    TP = T + NSA_KPAD
    R = NSA_GROUP * QB
    n_sel = min(N_SEL, ns)
    max_far = (NQ - 1) // 4
    NG = NSA_GROUPS_PER_STEP
    HB = NSA_GROUP * NG
    once = pl.Buffered(1)
    gtab = lambda w: pl.BlockSpec((HB, QB, w), lambda b, g, i: (g, 0, 0), pipeline_mode=once)
    slab = lambda rows: pl.BlockSpec((1, NG, rows, LANES), lambda b, g, i: (b, g, 0, 0), pipeline_mode=once)
    vm = (NG * (2 * ncp * LANES * 2 + 4 * TP * LANES * 2 + R * (SEL_CHUNK + WIN_KEYS) * 4)
          + 2 * NG * (R * LANES * 2 + QB * LANES * 4 + R * ncp * 4 + QB * NSA_GROUP * HEAD_DIM * 2)
          + NG * ((max_far + 1) * R * SEL_CHUNK * 4 + 2 * R * LANES * 4 + 4 * R * WIN_KEYS * 4))
    return pl.pallas_call(
        functools.partial(_nsa_attn_kernel, n_sel=n_sel, max_far=max_far),
        grid=(B, G // NG, NQ),
        in_specs=[
            pl.BlockSpec((1, HB, QB, LANES), lambda b, g, i: (b, g, i, 0)),
            slab(ncp), slab(ncp), slab(TP), slab(TP), slab(TP), slab(TP),
            pl.BlockSpec((1, NG, QB, LANES), lambda b, g, i: (b, g, i, 0)),
            gtab(SEL_CHUNK), gtab(WIN_KEYS),
            pl.BlockSpec((HB, 1, QB, ncp), lambda b, g, i: (g, i, 0, 0)),
            pl.BlockSpec((ns, ncp), lambda b, g, i: (0, 0)),
            pl.BlockSpec((NSA_GROUP * LANES, NSA_GROUP * HEAD_DIM), lambda b, g, i: (0, 0)),
            pl.BlockSpec((3, LANES, NSA_GROUP * HEAD_DIM), lambda b, g, i: (0, 0, 0)),
        ],
        out_specs=pl.BlockSpec((QB, HB * HEAD_DIM), lambda b, g, i: (b * NQ + i, g)),
        out_shape=jax.ShapeDtypeStruct((B * T, NSA_WIDTH), BF16),
        scratch_shapes=[pltpu.VMEM((NG, max(max_far, 1), R, SEL_CHUNK), F32), pltpu.VMEM((NG, ns, QB), F32)],
        compiler_params=_cparams(("parallel", "parallel", "arbitrary"), vm),
        name="nsa_attn",
    )(qp, kc, vc, ks, vs, kw, vw, gates, pn["stab"], pn["wtab"], pn["ctab"], pn["sel_mt"], pn["gather"],
      pn["gate_rep"])


def _rel_bucket_table():
    n = np.arange(REL_MAX_DIST + 1)
    max_exact = REL_BUCKETS // 2
    nf = np.maximum(n, max_exact).astype(np.float32)
    large = max_exact + (np.log(nf / np.float32(max_exact)) / np.float32(math.log(REL_MAX_DIST / max_exact))
                         * np.float32(REL_BUCKETS - max_exact)).astype(np.int32)
    large = np.minimum(large, REL_BUCKETS - 1)
    return np.where(n < max_exact, n, large).astype(np.int32)


def _sel_to_cmp_matrix(T, ncp):
    nc = T // CMP_STRIDE - CMP_BLOCK // CMP_STRIDE + 1
    ns = T // SEL_BLOCK
    cs = np.arange(nc) * CMP_STRIDE
    ss = np.arange(ns) * SEL_BLOCK
    lo = np.maximum(cs[None, :], ss[:, None])
    hi = np.minimum(cs[None, :] + CMP_BLOCK, ss[:, None] + SEL_BLOCK)
    out = np.zeros((ns, ncp), np.float32)
    out[:, :nc] = np.maximum(hi - lo, 0) / CMP_BLOCK
    return out


def _bias_tables(rel_bias, rel, attend, shift=None):
    bucket = np.where(attend, _rel_bucket_table()[np.clip(rel, 0, REL_MAX_DIST)], REL_BUCKETS)
    rows = rel_bias if shift is None else rel_bias - shift[None, :]
    rows = jnp.concatenate([rows, jnp.full((1, rel_bias.shape[1]), NEG_INF, rel_bias.dtype)], axis=0)
    onehot = (jnp.asarray(bucket.reshape(1, -1)) == jnp.arange(REL_BUCKETS + 1, dtype=jnp.int32)[:, None]).astype(F32)
    tab = jnp.einsum('bh,bn->hn', rows, onehot, precision=HIGHEST)
    return tab.reshape((rel_bias.shape[1],) + rel.shape)


def _prep_in_proj_weight(w_in_all, l):
    D = w_in_all.shape[1]
    nsa0 = RWKV_COLS
    kv0 = nsa0 + NSA_WIDTH
    gates0 = kv0 + 6 * NSA_KV_WIDTH
    merge0 = RWKV_COLS + NSA_COLS
    wt = jnp.swapaxes(w_in_all[l], 0, 1)
    kv = wt[kv0:gates0].reshape(3, 2, NSA_KV_HEADS, HEAD_DIM, D)
    kv = jnp.transpose(kv, (2, 0, 1, 3, 4)).reshape(6 * NSA_KV_WIDTH, D)
    pad = jnp.zeros((SMALL_W - (SM_GATES + 3 * NSA_HEADS), D), wt.dtype)
    return jnp.concatenate([
        wt[0:3 * RWKV_WIDTH],
        wt[nsa0:kv0],
        wt[merge0:merge0 + 2 * D_MODEL],
        kv,
        wt[3 * RWKV_WIDTH:RWKV_COLS],
        wt[gates0:merge0],
        pad,
    ], axis=0).astype(BF16)


def _prep_rwkv_params(mu, w0, w2, a0, a2, g2, k_k, k_a, r_k, ln_w, ln_b):
    C = RWKV_WIDTH
    row = lambda z: z.reshape(1, -1).astype(F32)
    mu_s = jnp.concatenate([mu[3 * C:], jnp.zeros((SMALL_W - (RWKV_COLS - 3 * C),), F32)]).reshape(1, SMALL_W)
    zl = jnp.zeros((DECAY_LORA, C), F32)

    w2p = jnp.concatenate([w2, zl], axis=0)
    w2h = w2p.astype(BF16)
    return dict(
        mu_r=row(mu[0:C]), mu_k=row(mu[C:2 * C]), mu_v=row(mu[2 * C:3 * C]), mu_s=mu_s,
        w0=row(w0), a0=row(a0), k_k=row(k_k), k_a=row(k_a), r_k=row(r_k), ln_w=row(ln_w), ln_b=row(ln_b),
        w2h=w2h, w2l=(w2p - w2h.astype(F32)).astype(BF16),
        a2=jnp.concatenate([zl, a2], axis=0).astype(BF16),
        g2=jnp.concatenate([g2, jnp.zeros((256 - GATE_LORA, C), F32)], axis=0).astype(BF16),
    )


def _prep_nsa_params(pe_k, w1_k, w2_k, pe_v, w1_v, w2_v, q_g, k_g, rel_bias, T):
    hd = HEAD_DIM
    ones = jnp.ones((hd,), F32)
    ncp = T // CMP_STRIDE
    NQ = T // QUERY_BLOCK

    def blockdiag(a, b):
        lead = ((0, 0),) * (a.ndim - 2)
        return jnp.pad(a, lead + ((0, hd), (0, hd))) + jnp.pad(b, lead + ((hd, 0), (hd, 0)))

    w1 = blockdiag(w1_k.reshape(CMP_BLOCK, hd, hd), w1_v.reshape(CMP_BLOCK, hd, hd))
    pe = jnp.concatenate([pe_k, pe_v], axis=1)
    half = CMP_STRIDE

    spread = np.zeros((NSA_GROUP * hd, NSA_GROUP * LANES), np.float32)
    gather = np.zeros((NSA_GROUP * LANES, NSA_GROUP * hd), np.float32)
    for h in range(NSA_GROUP):
        for d in range(hd):
            spread[h * hd + d, h * LANES + d] = 1.0
            gather[h * LANES + hd + d, h * hd + d] = 1.0
    gate_sel = np.zeros((SMALL_W, NSA_KV_HEADS * LANES), np.float32)
    for g in range(NSA_KV_HEADS):
        for j in range(3 * NSA_GROUP):
            gate_sel[SM_GATES + 3 * NSA_GROUP * g + j, g * LANES + j] = 1.0
    gate_rep = np.zeros((3, LANES, NSA_GROUP * hd), np.float32)
    for c in range(3):
        for h in range(NSA_GROUP):
            gate_rep[c, 3 * h + c, h * hd:(h + 1) * hd] = 1.0

    qi = np.arange(QUERY_BLOCK)[:, None]
    rel_s = qi + (SEL_CHUNK - QUERY_BLOCK) - np.arange(SEL_CHUNK)[None, :]
    rel_w = qi + WINDOW - np.arange(WIN_KEYS)[None, :]
    rel_c = ((np.arange(NQ)[:, None, None] * QUERY_BLOCK + qi[None])
             - (np.arange(ncp)[None, None, :] * CMP_STRIDE + CMP_BLOCK - 1))
    far_bias = rel_bias[int(_rel_bucket_table()[REL_MAX_DIST])]
    stab = _bias_tables(rel_bias, rel_s, rel_s >= 0, shift=far_bias)
    wtab = _bias_tables(rel_bias, rel_w, (rel_w >= 0) & (rel_w < WINDOW))
    ctab = _bias_tables(rel_bias, rel_c, rel_c >= 0)
    return dict(
        stab=stab, wtab=wtab, ctab=ctab,
        q_g=jnp.tile(q_g, NSA_GROUP).reshape(1, -1),
        kg_cmp=jnp.concatenate([k_g[0], ones]).reshape(1, LANES),
        kg_sel=jnp.concatenate([k_g[1], ones]).reshape(1, LANES),
        kg_win=jnp.concatenate([k_g[2], ones]).reshape(1, LANES),
        pe_a=pe[:half].reshape(1, half * LANES), pe_b=pe[half:].reshape(1, half * LANES),
        cw_a=w1[:half].reshape(half * LANES, LANES).astype(BF16),
        cw_b=w1[half:].reshape(half * LANES, LANES).astype(BF16),
        cw2=blockdiag(w2_k, w2_v).astype(BF16),
        q_spread=jnp.asarray(spread, BF16), gather=jnp.asarray(gather, BF16),
        gate_sel=jnp.asarray(gate_sel), gate_rep=jnp.asarray(gate_rep, BF16),
        sel_mt=jnp.asarray(_sel_to_cmp_matrix(T, ncp)),
    )


def kernel(x, c, w_ada, b_ada, norm1_g, norm2_g, w_in, rwkv_mu, rwkv_w0, rwkv_w2, rwkv_a0, rwkv_a2, rwkv_g2, rwkv_k_k, rwkv_k_a, rwkv_r_k, rwkv_ln_w, rwkv_ln_b, cmp_pe_k, cmp_w1_k, cmp_w2_k, cmp_pe_v, cmp_w1_v, cmp_w2_v, q_norm_g, k_norm_g, rel_bias, w_o_rwkv, w_o_nsa, w_out, w_up, w_down):
    B, T, D = x.shape
    depth = w_in.shape[0]
    x2 = x.reshape(B * T, D)
    for l in range(depth):
        mod6 = _ada_mod(c, w_ada[l], b_ada[l]).reshape(B * 6, 1, D)
        proj, small = _in_proj(x2, norm1_g[l].reshape(1, D), mod6, _prep_in_proj_weight(w_in, l), T)
        pr = _prep_rwkv_params(rwkv_mu[l], rwkv_w0[l], rwkv_w2[l], rwkv_a0[l], rwkv_a2[l], rwkv_g2[l],
                               rwkv_k_k[l], rwkv_k_a[l], rwkv_r_k[l], rwkv_ln_w[l], rwkv_ln_b[l])
        o_a = _rwkv_mix(proj, small, pr, B, T)
        pn = _prep_nsa_params(cmp_pe_k[l], cmp_w1_k[l], cmp_w2_k[l], cmp_pe_v[l], cmp_w1_v[l], cmp_w2_v[l],
                              q_norm_g[l], k_norm_g[l], rel_bias, T)
        qp, ks, vs, kw, vw, gates = _nsa_prep(proj, small, pn, B, T)
        kc, vc = _nsa_compress(proj, pn, B, T)
        o_b = _nsa_attn(qp, kc, vc, ks, vs, kw, vw, gates, pn, B, T)
        mixed = _merge(o_a, o_b, w_o_rwkv[l].astype(BF16), w_o_nsa[l].astype(BF16), proj)
        x1, h2 = _out_proj(mixed, w_out[l].astype(BF16), x2, mod6, norm2_g[l].reshape(1, D), T)
        x2 = _mlp(h2, w_up[l].astype(BF16), w_down[l].astype(BF16), x1, mod6, T)
    return x2.reshape(B, T, D)
```

```python
import functools
import math

import numpy as np
import jax
import jax.numpy as jnp
from jax import lax
from jax.experimental import pallas as pl
from jax.experimental.pallas import tpu as pltpu

F32 = jnp.float32
BF16 = jnp.bfloat16
HIGHEST = lax.Precision.HIGHEST

D_MODEL = 2048
HEAD_DIM = 64
RWKV_WIDTH = D_MODEL // 2
DECAY_LORA = 64
ICLR_LORA = 64
GATE_LORA = 160
RWKV_GN_EPS = 64e-5
NSA_WIDTH = D_MODEL // 2
NSA_HEADS = NSA_WIDTH // HEAD_DIM
NSA_KV_HEADS = 4
NSA_GROUP = NSA_HEADS // NSA_KV_HEADS
NSA_KV_WIDTH = NSA_KV_HEADS * HEAD_DIM
CMP_BLOCK = 32
CMP_STRIDE = 16
SEL_BLOCK = 64
N_SEL = 8
WINDOW = 512
QUERY_BLOCK = 128
REL_BUCKETS = 32
REL_MAX_DIST = 128
D_FF = 4 * D_MODEL
NORM_EPS = 1e-6
NEG_INF = -1e30
FORCE_SCORE = 1e4

RWKV_COLS = 3 * RWKV_WIDTH + DECAY_LORA + ICLR_LORA + GATE_LORA
NSA_COLS = NSA_WIDTH + 6 * NSA_KV_WIDTH + 3 * NSA_HEADS

V7X_VMEM_BYTES = 64 * 1024 * 1024
LANES = 128

COL_RKV = 0
COL_Q = 3 * RWKV_WIDTH
COL_MERGE = COL_Q + NSA_WIDTH
COL_KV = COL_MERGE + 2 * D_MODEL
COL_SMALL = COL_KV + 6 * NSA_KV_WIDTH
SMALL_W = 512
PROJ_COLS = COL_SMALL + SMALL_W
SM_XG = DECAY_LORA + ICLR_LORA
SM_GATES = SM_XG + GATE_LORA


def _vmem_limit(nbytes):
    return int(min(nbytes * 5 // 4 + (4 << 20), V7X_VMEM_BYTES - (8 << 20)))


def _cparams(sem, vmem_bytes):
    return pltpu.CompilerParams(dimension_semantics=sem, vmem_limit_bytes=_vmem_limit(vmem_bytes))


def _ada_kernel(c_ref, w_ref, b_ref, o_ref):
    c = c_ref[...]
    s = c * jax.nn.sigmoid(c)
    o_ref[...] = jnp.dot(s.astype(BF16), w_ref[...].astype(BF16), preferred_element_type=F32) + b_ref[...]


def _ada_mod(c, w_ada, b_ada):
    B, D = c.shape
    N = w_ada.shape[1]
    tn = 1024
    return pl.pallas_call(
        _ada_kernel,
        grid=(N // tn,),
        in_specs=[
            pl.BlockSpec((B, D), lambda j: (0, 0)),
            pl.BlockSpec((D, tn), lambda j: (0, j)),
            pl.BlockSpec((1, tn), lambda j: (0, j)),
        ],
        out_specs=pl.BlockSpec((B, tn), lambda j: (0, j)),
        out_shape=jax.ShapeDtypeStruct((B, N), F32),
        compiler_params=_cparams(("parallel",), 2 * D * tn * 4 + D * tn * 2),
        name="ada_mod",
    )(c, w_ada, b_ada.reshape(1, N))


def _modulated_norm(x, g, sc, sh):
    ms = jnp.mean(x * x, axis=-1, keepdims=True)
    return (x * lax.rsqrt(ms + NORM_EPS) * g) * (1.0 + sc) + sh


def _inproj_kernel(x_ref, g_ref, sh_ref, sc_ref, w_ref, o_ref, small_ref, h_scr):
    j = pl.program_id(1)
    last = pl.num_programs(1) - 1

    def project(h):
        y = lax.dot_general(h, w_ref[...], (((1,), (1,)), ((), ())),
                            preferred_element_type=F32)
        o_ref[...] = y.astype(BF16)
        return y

    @pl.when(j == 0)
    def _():
        h = _modulated_norm(x_ref[...], g_ref[...], sc_ref[0], sh_ref[0]).astype(BF16)
        h_scr[...] = h
        project(h)

    @pl.when((j > 0) & (j < last))
    def _():
        project(h_scr[...])

    @pl.when(j == last)
    def _():
        y = project(h_scr[...])
        small_ref[...] = y[:, y.shape[1] - SMALL_W:]


def _in_proj(x2, g1, mod6, w_in_t, T):
    BT, D = x2.shape
    NP = w_in_t.shape[0]
    tm = min(1024, T)
    tn = 1024
    tpb = T // tm
    assert NP - SMALL_W == COL_SMALL and tn >= SMALL_W
    vm = (2 * tm * D * 4 + tm * D * 2 + 2 * D * tn * 2 + 2 * tm * tn * 2 + 2 * tm * SMALL_W * 4
          + tm * tn * 4 + 2 * tm * D * 4)
    return pl.pallas_call(
        _inproj_kernel,
        grid=(BT // tm, NP // tn),
        in_specs=[
            pl.BlockSpec((tm, D), lambda i, j: (i, 0)),
            pl.BlockSpec((1, D), lambda i, j: (0, 0)),
            pl.BlockSpec((1, 1, D), lambda i, j: ((i // tpb) * 6 + 0, 0, 0)),
            pl.BlockSpec((1, 1, D), lambda i, j: ((i // tpb) * 6 + 1, 0, 0)),
            pl.BlockSpec((tn, D), lambda i, j: (j, 0)),
        ],
        out_specs=[pl.BlockSpec((tm, tn), lambda i, j: (i, j)), pl.BlockSpec((tm, SMALL_W), lambda i, j: (i, 0))],
        out_shape=[jax.ShapeDtypeStruct((BT, NP), BF16), jax.ShapeDtypeStruct((BT, SMALL_W), F32)],
        scratch_shapes=[pltpu.VMEM((tm, D), BF16)],
        compiler_params=_cparams(("parallel", "arbitrary"), vm),
        name="in_proj",
    )(x2, g1, mod6, mod6, w_in_t)


def _merge_kernel(oa_ref, ob_ref, wa_ref, wb_ref, ga_ref, gb_ref, o_ref):
    ya = jnp.dot(oa_ref[...], wa_ref[...], preferred_element_type=F32)
    yb = jnp.dot(ob_ref[...], wb_ref[...], preferred_element_type=F32)
    ga, gb = ga_ref[...].astype(F32), gb_ref[...].astype(F32)
    o_ref[...] = (jax.nn.sigmoid(ga) * ya + jax.nn.sigmoid(gb) * yb).astype(BF16)


def _merge(o_a, o_b, w_oa, w_ob, proj):
    BT, W = o_a.shape
    D = w_oa.shape[1]
    tm, tn = 512, 1024
    ga0 = COL_MERGE // tn
    gb0 = (COL_MERGE + D) // tn
    vm = 2 * (2 * tm * W * 2 + 2 * W * tn * 2 + 2 * tm * tn * 4 + tm * tn * 2) + 3 * tm * tn * 4
    return pl.pallas_call(
        _merge_kernel,
        grid=(BT // tm, D // tn),
        in_specs=[
            pl.BlockSpec((tm, W), lambda i, j: (i, 0)),
            pl.BlockSpec((tm, W), lambda i, j: (i, 0)),
            pl.BlockSpec((W, tn), lambda i, j: (0, j)),
            pl.BlockSpec((W, tn), lambda i, j: (0, j)),
            pl.BlockSpec((tm, tn), lambda i, j: (i, ga0 + j)),
            pl.BlockSpec((tm, tn), lambda i, j: (i, gb0 + j)),
        ],
        out_specs=pl.BlockSpec((tm, tn), lambda i, j: (i, j)),
        out_shape=jax.ShapeDtypeStruct((BT, D), BF16),
        compiler_params=_cparams(("parallel", "parallel"), vm),
        name="merge",
    )(o_a, o_b, w_oa, w_ob, proj, proj)


def _outproj_kernel(m_ref, w_ref, x_ref, gt_ref, g_ref, sh_ref, sc_ref, x1_ref, h2_ref):
    y = jnp.dot(m_ref[...], w_ref[...], preferred_element_type=F32)
    x1 = x_ref[...] + gt_ref[0] * y
    x1_ref[...] = x1
    h2_ref[...] = _modulated_norm(x1, g_ref[...], sc_ref[0], sh_ref[0]).astype(BF16)


def _merge_outproj_kernel(oa_ref, ob_ref, wa_ref, wb_ref, ga_ref, gb_ref, w_ref, x_ref, gt_ref, g_ref, sh_ref, sc_ref,
                          x1_ref, h2_ref):
    ya = jnp.dot(oa_ref[...], wa_ref[...], preferred_element_type=F32)
    yb = jnp.dot(ob_ref[...], wb_ref[...], preferred_element_type=F32)
    ga, gb = ga_ref[...].astype(F32), gb_ref[...].astype(F32)
    mixed = (jax.nn.sigmoid(ga) * ya + jax.nn.sigmoid(gb) * yb).astype(BF16)
    y = jnp.dot(mixed, w_ref[...], preferred_element_type=F32)
    x1 = x_ref[...] + gt_ref[0] * y
    x1_ref[...] = x1
    h2_ref[...] = _modulated_norm(x1, g_ref[...], sc_ref[0], sh_ref[0]).astype(BF16)


def _merge_out_proj(o_a, o_b, w_oa, w_ob, proj, w_out, x2, mod6, g2, T):
    BT, D = x2.shape
    W = o_a.shape[1]
    tm = min(256, T)
    tpb = T // tm
    once = pl.Buffered(1)
    vm = (2 * (2 * tm * W * 2 + 2 * tm * D * 2 + tm * D * 4 + tm * D * 4 + tm * D * 2)
          + 2 * W * D * 2 + D * D * 2 + 6 * tm * D * 4)
    mod_spec = lambda k: pl.BlockSpec((1, 1, D), lambda i: ((i // tpb) * 6 + k, 0, 0))
    return pl.pallas_call(
        _merge_outproj_kernel,
        grid=(BT // tm,),
        in_specs=[
            pl.BlockSpec((tm, W), lambda i: (i, 0)),
            pl.BlockSpec((tm, W), lambda i: (i, 0)),
            pl.BlockSpec((W, D), lambda i: (0, 0), pipeline_mode=once),
            pl.BlockSpec((W, D), lambda i: (0, 0), pipeline_mode=once),
            pl.BlockSpec((tm, D), lambda i: (i, COL_MERGE // D)),
            pl.BlockSpec((tm, D), lambda i: (i, COL_MERGE // D + 1)),
            pl.BlockSpec((D, D), lambda i: (0, 0), pipeline_mode=once),
            pl.BlockSpec((tm, D), lambda i: (i, 0)),
            mod_spec(2),
            pl.BlockSpec((1, D), lambda i: (0, 0)),
            mod_spec(3),
            mod_spec(4),
        ],
        out_specs=[pl.BlockSpec((tm, D), lambda i: (i, 0)), pl.BlockSpec((tm, D), lambda i: (i, 0))],
        out_shape=[jax.ShapeDtypeStruct((BT, D), F32), jax.ShapeDtypeStruct((BT, D), BF16)],
        compiler_params=_cparams(("parallel",), vm),
        name="merge_out_proj",
    )(o_a, o_b, w_oa, w_ob, proj, proj, w_out, x2, mod6, g2, mod6, mod6)


def _out_proj(mixed, w_out, x2, mod6, g2, T):
    BT, D = x2.shape
    tm = min(512, T)
    tpb = T // tm
    vm = 2 * (tm * D * 2 + D * D * 2 + tm * D * 4 + tm * D * 4 + tm * D * 2) + 3 * tm * D * 4
    mod_spec = lambda k: pl.BlockSpec((1, 1, D), lambda i: ((i // tpb) * 6 + k, 0, 0))
    return pl.pallas_call(
        _outproj_kernel,
        grid=(BT // tm,),
        in_specs=[
            pl.BlockSpec((tm, D), lambda i: (i, 0)),
            pl.BlockSpec((D, D), lambda i: (0, 0)),
            pl.BlockSpec((tm, D), lambda i: (i, 0)),
            mod_spec(2),
            pl.BlockSpec((1, D), lambda i: (0, 0)),
            mod_spec(3),
            mod_spec(4),
        ],
        out_specs=[pl.BlockSpec((tm, D), lambda i: (i, 0)), pl.BlockSpec((tm, D), lambda i: (i, 0))],
        out_shape=[jax.ShapeDtypeStruct((BT, D), F32), jax.ShapeDtypeStruct((BT, D), BF16)],
        compiler_params=_cparams(("parallel",), vm),
        name="out_proj",
    )(mixed, w_out, x2, mod6, g2, mod6, mod6)


OUT_CHUNK = 256


def _mlp_kernel(h_ref, wu_ref, wd_ref, x_ref, gt_ref, o_ref, acc_ref):
    f = pl.program_id(1)
    last = pl.num_programs(1) - 1

    def step(first, final):
        u = jnp.dot(h_ref[...], wu_ref[...], preferred_element_type=F32)
        u = jnp.square(jnp.maximum(u, 0.0)).astype(BF16)
        gt = gt_ref[0]
        for n in range(acc_ref.shape[1] // OUT_CHUNK):
            cs = slice(n * OUT_CHUNK, (n + 1) * OUT_CHUNK)
            part = jnp.dot(u, wd_ref[:, cs], preferred_element_type=F32)
            if first:
                acc_ref[:, cs] = part
            elif final:
                o_ref[:, cs] = x_ref[:, cs] + gt[:, cs] * (acc_ref[:, cs] + part)
            else:
                acc_ref[:, cs] += part

    pl.when(f == 0)(functools.partial(step, True, False))
    pl.when((f > 0) & (f < last))(functools.partial(step, False, False))
    pl.when(f == last)(functools.partial(step, False, True))


def _mlp(h2, w_up, w_down, x1, mod6, T):
    BT, D = x1.shape
    F = w_up.shape[1]
    tm = min(512, T)
    tf = 1024
    tpb = T // tm
    vm = 2 * (tm * D * 2 + 2 * D * tf * 2 + 2 * tm * D * 4) + tm * D * 4 + 2 * tm * tf * 4
    return pl.pallas_call(
        _mlp_kernel,
        grid=(BT // tm, F // tf),
        in_specs=[
            pl.BlockSpec((tm, D), lambda i, f: (i, 0)),
            pl.BlockSpec((D, tf), lambda i, f: (0, f)),
            pl.BlockSpec((tf, D), lambda i, f: (f, 0)),
            pl.BlockSpec((tm, D), lambda i, f: (i, 0)),
            pl.BlockSpec((1, 1, D), lambda i, f: ((i // tpb) * 6 + 5, 0, 0)),
        ],
        out_specs=pl.BlockSpec((tm, D), lambda i, f: (i, 0)),
        out_shape=jax.ShapeDtypeStruct((BT, D), F32),
        scratch_shapes=[pltpu.VMEM((tm, D), F32)],
        compiler_params=_cparams(("parallel", "arbitrary"), vm),
        name="mlp",
    )(h2, w_up, w_down, x1, mod6)


def _dotb(a, b):
    return jnp.dot(a.astype(BF16), b.astype(BF16), preferred_element_type=F32)


def _dotb_nt(a, b):
    return lax.dot_general(a.astype(BF16), b.astype(BF16), (((1,), (1,)), ((), ())),
                           preferred_element_type=F32)


def _split_bf16(x, terms):
    parts, rem = [], x
    for t in range(terms):
        p = rem.astype(BF16)
        parts.append(p)
        if t + 1 < terms:
            rem = rem - p.astype(F32)
    return parts


def _dot_sel(x, sel, terms):
    sel = sel.astype(BF16)
    return sum(jnp.dot(p, sel, preferred_element_type=F32) for p in _split_bf16(x, terms))


def _sel_dot(sel, x, terms):
    sel = sel.astype(BF16)
    return sum(jnp.dot(sel, p, preferred_element_type=F32) for p in _split_bf16(x, terms))


def _dot3(a, b_hi, b_lo):
    a_hi, a_lo = _split_bf16(a, 2)
    return (jnp.dot(a_hi, b_hi, preferred_element_type=F32) + jnp.dot(a_lo, b_hi, preferred_element_type=F32)
            + jnp.dot(a_hi, b_lo, preferred_element_type=F32))


def _iota(shape, axis):
    return lax.broadcasted_iota(jnp.int32, shape, axis)


def _same_head_mask(n):
    return (_iota((n, n), 0) >> 6) == (_iota((n, n), 1) >> 6)


RW_TC = 256
RW_C = 64
RW_LANES = 4 * HEAD_DIM


def _block_diag(x, bdmask):
    return jnp.where(bdmask, jnp.concatenate([x, x, x, x], axis=0), 0.0)


def _rwkv_kernel(r_ref, k_ref, v_ref, s_ref, mur_ref, muk_ref, muv_ref, mus_ref,
                 w0_ref, a0_ref, kk_ref, ka_ref, rk_ref, lnw_ref, lnb_ref,
                 w2h_ref, w2l_ref, a2_ref, g2_ref, o_ref,
                 pr_scr, pk_scr, pv_scr, ps_scr, state_scr):
    TC = r_ref.shape[0]
    C = RW_C
    W = RW_LANES

    @pl.when(pl.program_id(1) == 0)
    def _():
        pr_scr[...] = jnp.zeros_like(pr_scr)
        pk_scr[...] = jnp.zeros_like(pk_scr)
        pv_scr[...] = jnp.zeros_like(pv_scr)
        ps_scr[...] = jnp.zeros_like(ps_scr)
        state_scr[...] = jnp.zeros_like(state_scr)

    def shift_mix(p_ref, prev_scr, mu_ref):
        p = p_ref[...].astype(F32)
        rolled = pltpu.roll(p, 1, 0)
        first = jnp.where(_iota((8, p.shape[1]), 0) == 0, prev_scr[0:1, :], rolled[0:8])
        shifted = jnp.concatenate([first, rolled[8:]], axis=0)
        prev_scr[0:1, :] = p[TC - 1:TC, :]
        return p + (shifted - p) * mu_ref[...]

    CW = r_ref.shape[1]
    G = CW // W
    NQ = TC // C
    groups = lambda x: [x[:, g * W:(g + 1) * W] for g in range(G)]
    per_group = lambda f, x: jnp.concatenate([f(xg) for xg in groups(x)], axis=1)

    r = shift_mix(r_ref, pr_scr, mur_ref)
    k = shift_mix(k_ref, pk_scr, muk_ref)
    v = shift_mix(v_ref, pv_scr, muv_ref)
    sm = shift_mix(s_ref, ps_scr, mus_ref)
    xwa = sm[:, 0:SM_XG]

    bdmask = _same_head_mask(W)
    bones = jnp.where(bdmask, 1.0, 0.0)
    head_sum = lambda x: per_group(lambda xg: _dot_sel(xg, bones, 1), x)
    tri = jnp.where(_same_head_mask(TC) & (_iota((TC, TC), 1) <= _iota((TC, TC), 0)), 1.0, 0.0)
    lane = _iota((C, W), 1)
    row = _iota((C, W), 0)
    lanehead = lane >> 6
    strict = (lane & 63) < row
    incl = (lane & 63) <= row
    eye_cat = jnp.where((lane & 63) == row, 1.0, 0.0)

    wlin = w0_ref[...] + _dot3(jnp.tanh(xwa), w2h_ref[...], w2l_ref[...])
    a = jax.nn.sigmoid(a0_ref[...] + _dotb(xwa, a2_ref[...]))
    gate = _dotb(jax.nn.sigmoid(sm[:, SM_XG:SM_XG + 256]), g2_ref[...])
    z = -wlin
    softplus = jnp.maximum(z, 0.0) + jnp.log(1.0 + jnp.exp(-jnp.abs(z)))
    ld = -jnp.exp(-softplus - 0.5)
    cum = _sel_dot(tri, ld, 3)
    cum_last = jnp.concatenate(
        [jnp.broadcast_to(cum[(q + 1) * C - 1:(q + 1) * C, :], (C, CW)) for q in range(NQ)], axis=0)
    kk = k * kk_ref[...]
    kk = kk * lax.rsqrt(jnp.maximum(head_sum(kk * kk), 1e-24))
    k2 = k * (1.0 + (a - 1.0) * ka_ref[...])
    bvec = kk * a
    e_inv = jnp.exp(-cum)
    e_end = jnp.exp(cum_last - cum)
    At_f = -kk * jnp.exp(cum - ld)
    Rt_f = r * jnp.exp(cum)
    Bt_f = bvec * e_inv
    Kt_f = k2 * e_inv
    Bg_f = bvec * e_end
    Kg_f = k2 * e_end
    g_end = jnp.exp(cum_last)

    items = [(g, q) for q in range(NQ) for g in range(G)]
    blk = lambda x, g, q: x[q * C:(q + 1) * C, g * W:(g + 1) * W]
    bd01 = jnp.where(bdmask, 1.0, 0.0).astype(BF16)

    def bd(x):
        xb = x.astype(BF16)
        return jnp.concatenate([xb, xb, xb, xb], axis=0) * bd01

    At = {it: blk(At_f, *it) for it in items}
    Rt = {it: blk(Rt_f, *it) for it in items}
    Vq = {it: blk(v, *it) for it in items}
    AA = {}
    for it in items:
        bk = jnp.concatenate([jnp.where(lanehead == h, X, 0.0)
                              for X in (blk(Bt_f, *it), blk(Kt_f, *it)) for h in range(4)], axis=0)
        AA[it] = _dotb_nt(jnp.concatenate([At[it], Rt[it]], axis=0), bk)
    A_ab = {it: jnp.where(strict, AA[it][0:C, 0:W], 0.0) for it in items}
    A_ak = {it: jnp.where(strict, AA[it][0:C, W:2 * W], 0.0) for it in items}
    A_rb = {it: jnp.where(incl, AA[it][C:2 * C, 0:W], 0.0) for it in items}
    A_rk = {it: jnp.where(incl, AA[it][C:2 * C, W:2 * W], 0.0) for it in items}
    M = dict(A_ab)
    Tm = {it: eye_cat + A_ab[it] for it in items}
    for _ in range(5):
        M = {it: _dotb(M[it], bd(M[it])) for it in items}
        Tm = {it: Tm[it] + _dotb(M[it], bd(Tm[it])) for it in items}
    Vbd = {it: bd(Vq[it]) for it in items}
    akv = {it: _dotb(A_ak[it], Vbd[it]) for it in items}
    rkv = {it: _dotb(A_rk[it], Vbd[it]) for it in items}

    S = [state_scr[g] for g in range(G)]
    ys = []
    for q in range(NQ):
        its = [(g, q) for g in range(G)]
        rhs = [_dotb_nt(At[it], S[it[0]]) + akv[it] for it in its]
        U = [_dotb(Tm[it], bd(rhs[g])) for g, it in enumerate(its)]
        ys.append(jnp.concatenate(
            [_dotb_nt(Rt[it], S[g]) + _dotb(A_rb[it], bd(U[g])) + rkv[it] for g, it in enumerate(its)], axis=1))
        upd = [_dotb(jnp.concatenate([U[g], Vq[it]], axis=0).T,
                     jnp.concatenate([blk(Bg_f, *it), blk(Kg_f, *it)], axis=0)) for g, it in enumerate(its)]
        S = [S[g] * blk(g_end, g, q)[0:1, :] + jnp.where(bdmask, upd[g], 0.0) for g in range(G)]
    for g in range(G):
        state_scr[g] = S[g]
    y = jnp.concatenate(ys, axis=0)

    inv_n = 1.0 / HEAD_DIM
    d = y - head_sum(y) * inv_n
    var = head_sum(d * d) * inv_n
    yn = d * lax.rsqrt(var + RWKV_GN_EPS) * lnw_ref[...] + lnb_ref[...]
    bonus = head_sum(r * k2 * rk_ref[...]) * v
    o_ref[...] = ((yn + bonus) * gate).astype(BF16)


def _rwkv_mix(proj, small, pr, B, T):
    BT = proj.shape[0]
    TC = min(RW_TC, T)
    W = RW_LANES
    CW = RWKV_WIDTH
    nct = T // TC
    row = lambda b, c: b * nct + c
    full = lambda shape: pl.BlockSpec(shape, lambda b, c: (0, 0))
    vec = full((1, CW))
    in_specs = [
        pl.BlockSpec((TC, CW), lambda b, c: (row(b, c), 0)),
        pl.BlockSpec((TC, CW), lambda b, c: (row(b, c), 1)),
        pl.BlockSpec((TC, CW), lambda b, c: (row(b, c), 2)),
        pl.BlockSpec((TC, SMALL_W), lambda b, c: (row(b, c), 0)),
        vec, vec, vec,
        full((1, SMALL_W)),
        vec, vec, vec, vec, vec, vec, vec,
        full((SM_XG, CW)), full((SM_XG, CW)), full((SM_XG, CW)), full((256, CW)),
    ]
    vm = (2 * (3 * TC * CW * 4 + TC * SMALL_W * 4 + TC * CW * 2 + (3 * SM_XG + 256) * CW * 2)
          + 24 * TC * CW * 4)
    return pl.pallas_call(
        _rwkv_kernel,
        grid=(B, nct),
        in_specs=in_specs,
        out_specs=pl.BlockSpec((TC, CW), lambda b, c: (row(b, c), 0)),
        out_shape=jax.ShapeDtypeStruct((BT, CW), BF16),
        scratch_shapes=[pltpu.VMEM((8, CW), F32), pltpu.VMEM((8, CW), F32), pltpu.VMEM((8, CW), F32),
                        pltpu.VMEM((8, SMALL_W), F32), pltpu.VMEM((CW // W, W, W), F32)],
        compiler_params=_cparams(("parallel", "arbitrary"), vm),
        name="rwkv_mix",
    )(proj, proj, proj, small, pr["mu_r"], pr["mu_k"], pr["mu_v"], pr["mu_s"],
      pr["w0"], pr["a0"], pr["k_k"], pr["k_a"], pr["r_k"], pr["ln_w"], pr["ln_b"],
      pr["w2h"], pr["w2l"], pr["a2"], pr["g2"])


NSA_TT = 256
BLOCK_LANE0 = HEAD_DIM
PADFLAG_LANE = BLOCK_LANE0 + 32


def _nsa_prep_kernel(q_ref, kv0_ref, kv1_ref, kv2_ref, s_ref, qg_ref, kgs_ref, kgw_ref, e_ref, gsel_ref,
                     qp_ref, ksl_ref, vsl_ref, kwl_ref, vwl_ref, gt_ref):
    tt = q_ref.shape[0]
    n_pad = NSA_KPAD // tt
    step = pl.program_id(1)
    lane = _iota((tt, LANES), 1)

    @pl.when(step < n_pad)
    def _():
        flag = jnp.broadcast_to(jnp.where(lane == PADFLAG_LANE, 1.0, 0.0).astype(BF16), ksl_ref.shape[1:])
        ksl_ref[0] = flag
        kwl_ref[0] = flag
        vsl_ref[0] = jnp.zeros_like(flag)
        vwl_ref[0] = jnp.zeros_like(flag)

    @pl.when(step >= n_pad)
    def _():
        _nsa_prep_tile(step - n_pad, lane, q_ref, (kv0_ref, kv1_ref, kv2_ref), s_ref, qg_ref, kgs_ref, kgw_ref,
                       e_ref, gsel_ref, qp_ref, ksl_ref, vsl_ref, kwl_ref, vwl_ref, gt_ref)


def _nsa_prep_tile(tile, lane, q_ref, kv_refs, s_ref, qg_ref, kgs_ref, kgw_ref, e_ref, gsel_ref,
                   qp_ref, ksl_ref, vsl_ref, kwl_ref, vwl_ref, gt_ref):
    tt = q_ref.shape[0]
    QW = NSA_GROUP * HEAD_DIM
    bones = jnp.where(_same_head_mask(QW), 1.0, 0.0)
    bones2 = jnp.where(_same_head_mask(LANES), 1.0, 0.0)
    block_id = (tile * tt + _iota((tt, LANES), 0)) >> 6
    onehot = jnp.where(lane == block_id + BLOCK_LANE0, 1.0, 0.0)
    gates = jax.nn.sigmoid(_dot_sel(s_ref[...], gsel_ref[...], 2))

    def pair(g, branch):
        off = (3 * g + branch) * LANES
        return kv_refs[off // SMALL_W][:, off % SMALL_W:off % SMALL_W + LANES].astype(F32)

    def slabs(g, x, gain, k_ref, v_ref):
        ms = _dot_sel(x * x, bones2, 1) * (1.0 / HEAD_DIM)
        k_ref[0, g] = jnp.where(lane < HEAD_DIM, x * lax.rsqrt(ms + NORM_EPS) * gain, onehot).astype(BF16)
        v_ref[0, g] = jnp.where(lane >= HEAD_DIM, x, 1.0).astype(BF16)

    for g in range(NSA_KV_HEADS):
        q = q_ref[:, g * QW:(g + 1) * QW].astype(F32)
        ms = _dot_sel(q * q, bones, 1) * (1.0 / HEAD_DIM)
        qn = (q * lax.rsqrt(ms + NORM_EPS) * qg_ref[...]) * (HEAD_DIM ** -0.5)
        qs = jnp.dot(qn.astype(BF16), e_ref[...], preferred_element_type=F32).astype(BF16)
        for h in range(NSA_GROUP):
            qp_ref[0, NSA_GROUP * g + h] = qs[:, h * LANES:(h + 1) * LANES]
        slabs(g, pair(g, 1), kgs_ref[...], ksl_ref, vsl_ref)
        slabs(g, pair(g, 2), kgw_ref[...], kwl_ref, vwl_ref)
        gt_ref[0, g] = gates[:, g * LANES:(g + 1) * LANES]


def _nsa_prep(proj, small, pn, B, T):
    tt = min(NSA_TT, T)
    ntt = T // tt
    G = NSA_KV_HEADS
    QW = NSA_GROUP * HEAD_DIM
    n_pad = NSA_KPAD // tt
    tile = lambda t: jnp.maximum(t - n_pad, 0)
    row = lambda b, t: b * ntt + tile(t)
    full = lambda shape: pl.BlockSpec(shape, lambda b, t: tuple(0 for _ in shape))
    kv_spec = lambda j: pl.BlockSpec((tt, SMALL_W), lambda b, t: (row(b, t), COL_KV // SMALL_W + j))
    slab = pl.BlockSpec((1, G, tt, LANES), lambda b, t: (b, 0, t, 0))
    vm = 2 * (tt * NSA_WIDTH * 4 + 4 * tt * SMALL_W * 4 + QW * 4 * LANES * 2 + G * SMALL_W * LANES * 4
              + 16 * tt * LANES * 2 + 16 * tt * LANES * 2 + 4 * tt * LANES * 4) + 24 * tt * QW * 4
    return pl.pallas_call(
        _nsa_prep_kernel,
        grid=(B, ntt + n_pad),
        in_specs=[
            pl.BlockSpec((tt, NSA_WIDTH), lambda b, t: (row(b, t), COL_Q // NSA_WIDTH)),
            kv_spec(0), kv_spec(1), kv_spec(2),
            pl.BlockSpec((tt, SMALL_W), lambda b, t: (row(b, t), 0)),
            full((1, QW)), full((1, LANES)), full((1, LANES)), full((QW, NSA_GROUP * LANES)),
            full((SMALL_W, G * LANES)),
        ],
        out_specs=[pl.BlockSpec((1, NSA_HEADS, tt, LANES), lambda b, t: (b, 0, tile(t), 0)),
                   slab, slab, slab, slab,
                   pl.BlockSpec((1, G, tt, LANES), lambda b, t: (b, 0, tile(t), 0))],
        out_shape=[jax.ShapeDtypeStruct((B, NSA_HEADS, T, LANES), BF16)]
        + [jax.ShapeDtypeStruct((B, G, T + NSA_KPAD, LANES), BF16)] * 4
        + [jax.ShapeDtypeStruct((B, G, T, LANES), F32)],
        compiler_params=_cparams(("parallel", "arbitrary"), vm),
        name="nsa_prep",
    )(proj, proj, proj, proj, small, pn["q_g"], pn["kg_sel"], pn["kg_win"], pn["q_spread"], pn["gate_sel"])


def _gelu_tanh(x):
    return 0.5 * x * (1.0 + jnp.tanh(math.sqrt(2.0 / math.pi) * (x + 0.044715 * (x * x * x))))


def _nsa_compress_kernel(x_ref, pea_ref, peb_ref, wa_ref, wb_ref, w2_ref, kg_ref, kc_ref, vc_ref, x_scr):
    nsub = x_ref.shape[0] // CMP_STRIDE
    x_scr[...] = x_ref[...].astype(F32)
    xs = jnp.concatenate([x_scr[pl.ds(s, nsub, stride=CMP_STRIDE), :] for s in range(CMP_STRIDE)], axis=1)
    p0 = jnp.dot((xs + pea_ref[...]).astype(BF16), wa_ref[...], preferred_element_type=F32)
    p1 = jnp.dot((xs + peb_ref[...]).astype(BF16), wb_ref[...], preferred_element_type=F32)
    hid = _gelu_tanh(p0 + pltpu.roll(p1, nsub - 1, 0))
    out = jnp.dot(hid.astype(BF16), w2_ref[...], preferred_element_type=F32)
    bones2 = jnp.where(_same_head_mask(LANES), 1.0, 0.0)
    ms = _dot_sel(out * out, bones2, 2) * (1.0 / HEAD_DIM)
    is_k = _iota(out.shape, 1) < HEAD_DIM
    kc_ref[0, 0] = jnp.where(is_k, out * lax.rsqrt(ms + NORM_EPS) * kg_ref[...], 0.0).astype(BF16)
    vc_ref[0, 0] = jnp.where(is_k, 0.0, out).astype(BF16)


def _nsa_compress(proj, pn, B, T):
    G = NSA_KV_HEADS
    nsub = T // CMP_STRIDE
    kv_blk = COL_KV // LANES
    KW = CMP_STRIDE * LANES
    full = lambda shape: pl.BlockSpec(shape, lambda b, g: tuple(0 for _ in shape))
    vm = 2 * (T * LANES * 4 + 2 * KW * LANES * 2 + nsub * LANES * 2) + 6 * nsub * KW * 4
    return pl.pallas_call(
        _nsa_compress_kernel,
        grid=(B, G),
        in_specs=[
            pl.BlockSpec((T, LANES), lambda b, g: (b, kv_blk + 3 * g)),
            full((1, KW)), full((1, KW)), full((KW, LANES)), full((KW, LANES)), full((LANES, LANES)),
            full((1, LANES)),
        ],
        out_specs=[pl.BlockSpec((1, 1, nsub, LANES), lambda b, g: (b, g, 0, 0))] * 2,
        out_shape=[jax.ShapeDtypeStruct((B, G, nsub, LANES), BF16)] * 2,
        scratch_shapes=[pltpu.VMEM((T, LANES), F32)],
        compiler_params=_cparams(("parallel", "parallel"), vm),
        name="nsa_compress",
    )(proj, pn["pe_a"], pn["pe_b"], pn["cw_a"], pn["cw_b"], pn["cw2"], pn["kg_cmp"])


NSA_GROUPS_PER_STEP = 4
NSA_KPAD = WINDOW
SEL_CHUNK = 4 * QUERY_BLOCK
WIN_KEYS = WINDOW + QUERY_BLOCK


def _lane_tile_max(s):
    tiles = [s[:, j * LANES:(j + 1) * LANES] for j in range(s.shape[1] // LANES)]
    while len(tiles) > 1:
        tiles = [jnp.maximum(a, b) for a, b in zip(tiles[0::2], tiles[1::2])] + ([tiles[-1]] if len(tiles) % 2 else [])
    return tiles[0]


def _nsa_attn_kernel(*refs, n_sel, max_far):
    n_far = pl.program_id(2) // 4
    for k in range(max_far + 1):
        pl.when(n_far == k)(functools.partial(_nsa_attn_step, k, *refs, n_sel=n_sel))


def _nsa_attn_step(n_far, q_ref, kc_ref, vc_ref, ks_ref, vs_ref, kw_ref, vw_ref, gt_ref, stab_ref, wtab_ref, ctab_ref,
                   selmt_ref, gather_ref, grep_ref, o_ref, sbuf, sc_scr, *, n_sel):
    i = pl.program_id(2)
    QB = QUERY_BLOCK
    HG = NSA_GROUP
    R = HG * QB
    t0 = i * QB
    groups = range(kc_ref.shape[1])
    each = lambda f: [f(g) for g in groups]
    tile4 = lambda z: jnp.concatenate([z, z, z, z], axis=0)
    heads = lambda ref, g: ref[HG * g:HG * (g + 1)]
    lane = _iota((QB, LANES), 1)
    pad_mask = jnp.where(lane == PADFLAG_LANE, NEG_INF, 0.0)
    q = each(lambda g: q_ref[0, HG * g:HG * (g + 1)].reshape(R, LANES))
    q32 = each(lambda g: q[g].astype(F32))

    ncp = kc_ref.shape[2]
    s = each(lambda g: _dotb_nt(q[g], kc_ref[0, g]) + heads(ctab_ref, g)[:, 0].reshape(R, ncp))
    m = each(lambda g: jnp.max(s[g], axis=-1, keepdims=True))
    p_c = each(lambda g: jnp.exp(s[g] - m[g]))
    lsum = each(lambda g: jnp.sum(p_c[g], axis=-1, keepdims=True))
    p_c = each(lambda g: p_c[g] * jnp.where(m[g] > 0.5 * NEG_INF, 1.0 / lsum[g], 0.0))
    o_c = each(lambda g: jnp.dot(p_c[g].astype(BF16), vc_ref[0, g], preferred_element_type=F32))

    wrows = pl.ds(pl.multiple_of(t0, QB), WIN_KEYS)
    q_win = each(lambda g: (q32[g] + tile4(pad_mask)).astype(BF16))
    s = each(lambda g: _dotb_nt(q_win[g], kw_ref[0, g, wrows, :]) + heads(wtab_ref, g).reshape(R, WIN_KEYS))
    p = each(lambda g: jnp.exp(s[g] - jnp.max(_lane_tile_max(s[g]), axis=-1, keepdims=True)))
    acc_w = each(lambda g: jnp.dot(p[g].astype(BF16), vw_ref[0, g, wrows, :], preferred_element_type=F32))

    psum = each(lambda g: p_c[g][0:QB] + p_c[g][QB:2 * QB] + p_c[g][2 * QB:3 * QB] + p_c[g][3 * QB:4 * QB])
    selmt = selmt_ref[...]
    parts = each(lambda g: _split_bf16(psum[g], 3))
    imp = each(lambda g: sum(lax.dot_general(selmt, part, (((1,), (1,)), ((), ())), preferred_element_type=F32)
                             for part in parts[g]))
    ns = selmt.shape[0]
    blk = _iota((ns, QB), 0)
    cur = (t0 + _iota((ns, QB), 1)) >> 6
    forced = (blk == 0) | (blk == cur) | (blk == cur - 1)
    score = each(lambda g: jnp.where(forced, FORCE_SCORE, jnp.where(blk <= cur, imp[g], -1.0)))
    for g in groups:
        sc_scr[g] = score[g]
    ranks = [[] for _ in groups]
    for j in range(ns):
        lower = jnp.where(blk > j, 1.0, 0.0)
        for g in groups:
            other = sc_scr[g, j:j + 1, :]
            ranks[g].append(jnp.where(other > score[g], 1.0, 0.0) + jnp.where(other == score[g], lower, 0.0))
    while len(ranks[0]) > 1:
        ranks = [[a + b for a, b in zip(r[0::2], r[1::2])] + ([r[-1]] if len(r) % 2 else []) for r in ranks]
    chosen_t = each(lambda g: jnp.where(ranks[g][0] < n_sel, 1.0, 0.0))
    zrows = lambda n: jnp.zeros((n, QB), F32)
    chosen = each(lambda g: jnp.concatenate([zrows(BLOCK_LANE0), chosen_t[g], zrows(LANES - BLOCK_LANE0 - ns)],
                                            axis=0).T)
    is_block_lane = (lane >= BLOCK_LANE0) & (lane < BLOCK_LANE0 + ns)
    q_sel = each(lambda g: (q32[g] + tile4(jnp.where(is_block_lane, (chosen[g] - 1.0) * (-NEG_INF), pad_mask))
                            ).astype(BF16))

    def chunk_rows(c):
        return pl.ds(pl.multiple_of((i - 4 * c + 1) * QB, QB), SEL_CHUNK)

    s0 = each(lambda g: _dotb_nt(q_sel[g], ks_ref[0, g, chunk_rows(0), :]) + heads(stab_ref, g).reshape(R, SEL_CHUNK))
    tile_max = each(lambda g: _lane_tile_max(s0[g]))
    for c in range(1, n_far + 1):
        s = each(lambda g: _dotb_nt(q_sel[g], ks_ref[0, g, chunk_rows(c), :]))
        for g in groups:
            sbuf[g, c - 1] = s[g]
        tile_max = each(lambda g: jnp.maximum(tile_max[g], _lane_tile_max(s[g])))
    m_s = each(lambda g: jnp.max(tile_max[g], axis=-1, keepdims=True))

    def pv(g, c):
        s = s0[g] if c == 0 else sbuf[g, c - 1]
        return jnp.dot(jnp.exp(s - m_s[g]).astype(BF16), vs_ref[0, g, chunk_rows(c), :], preferred_element_type=F32)

    acc_s = each(lambda g: sum(pv(g, c) for c in range(n_far + 1)))

    is_value = _iota((R, LANES), 1) >= HEAD_DIM
    normalised = lambda acc: jnp.where(is_value, acc * (1.0 / pltpu.roll(acc, HEAD_DIM, 1)), 0.0)

    def natural(o):
        cat = jnp.concatenate([o[h * QB:(h + 1) * QB] for h in range(HG)], axis=1).astype(BF16)
        return jnp.dot(cat, gather_ref[...], preferred_element_type=F32)

    branches = each(lambda g: [o_c[g], normalised(acc_s[g]), normalised(acc_w[g])])
    gate = each(lambda g: [_dot_sel(gt_ref[0, g], grep_ref[c], 1) for c in range(3)])
    W = HG * HEAD_DIM
    for g in groups:
        out = sum(gate[g][c] * natural(branches[g][c]) for c in range(3))
        o_ref[:, g * W:(g + 1) * W] = out.astype(BF16)


def _nsa_attn(qp, kc, vc, ks, vs, kw, vw, gates, pn, B, T):
    G = NSA_KV_HEADS
    QB = QUERY_BLOCK
    NQ = T // QB
    ncp = kc.shape[2]
    ns = T // SEL_BLOCK
    TP = T + NSA_KPAD
    R = NSA_GROUP * QB
    n_sel = min(N_SEL, ns)
    max_far = (NQ - 1) // 4
    NG = NSA_GROUPS_PER_STEP
    HB = NSA_GROUP * NG
    once = pl.Buffered(1)
    gtab = lambda w: pl.BlockSpec((HB, QB, w), lambda b, g, i: (g, 0, 0), pipeline_mode=once)
    slab = lambda rows: pl.BlockSpec((1, NG, rows, LANES), lambda b, g, i: (b, g, 0, 0), pipeline_mode=once)
    vm = (NG * (2 * ncp * LANES * 2 + 4 * TP * LANES * 2 + R * (SEL_CHUNK + WIN_KEYS) * 4)
          + 2 * NG * (R * LANES * 2 + QB * LANES * 4 + R * ncp * 4 + QB * NSA_GROUP * HEAD_DIM * 2)
          + NG * ((max_far + 1) * R * SEL_CHUNK * 4 + 2 * R * LANES * 4 + 4 * R * WIN_KEYS * 4))
    return pl.pallas_call(
        functools.partial(_nsa_attn_kernel, n_sel=n_sel, max_far=max_far),
        grid=(B, G // NG, NQ),
        in_specs=[
            pl.BlockSpec((1, HB, QB, LANES), lambda b, g, i: (b, g, i, 0)),
            slab(ncp), slab(ncp), slab(TP), slab(TP), slab(TP), slab(TP),
            pl.BlockSpec((1, NG, QB, LANES), lambda b, g, i: (b, g, i, 0)),
            gtab(SEL_CHUNK), gtab(WIN_KEYS),
            pl.BlockSpec((HB, 1, QB, ncp), lambda b, g, i: (g, i, 0, 0)),
            pl.BlockSpec((ns, ncp), lambda b, g, i: (0, 0)),
            pl.BlockSpec((NSA_GROUP * LANES, NSA_GROUP * HEAD_DIM), lambda b, g, i: (0, 0)),
            pl.BlockSpec((3, LANES, NSA_GROUP * HEAD_DIM), lambda b, g, i: (0, 0, 0)),
        ],
        out_specs=pl.BlockSpec((QB, HB * HEAD_DIM), lambda b, g, i: (b * NQ + i, g)),
        out_shape=jax.ShapeDtypeStruct((B * T, NSA_WIDTH), BF16),
        scratch_shapes=[pltpu.VMEM((NG, max(max_far, 1), R, SEL_CHUNK), F32), pltpu.VMEM((NG, ns, QB), F32)],
        compiler_params=_cparams(("parallel", "parallel", "arbitrary"), vm),
        name="nsa_attn",
    )(qp, kc, vc, ks, vs, kw, vw, gates, pn["stab"], pn["wtab"], pn["ctab"], pn["sel_mt"], pn["gather"],
      pn["gate_rep"])


def _rel_bucket_table():
    n = np.arange(REL_MAX_DIST + 1)
    max_exact = REL_BUCKETS // 2
    nf = np.maximum(n, max_exact).astype(np.float32)
    large = max_exact + (np.log(nf / np.float32(max_exact)) / np.float32(math.log(REL_MAX_DIST / max_exact))
                         * np.float32(REL_BUCKETS - max_exact)).astype(np.int32)
    large = np.minimum(large, REL_BUCKETS - 1)
    return np.where(n < max_exact, n, large).astype(np.int32)


def _sel_to_cmp_matrix(T, ncp):
    nc = T // CMP_STRIDE - CMP_BLOCK // CMP_STRIDE + 1
    ns = T // SEL_BLOCK
    cs = np.arange(nc) * CMP_STRIDE
    ss = np.arange(ns) * SEL_BLOCK
    lo = np.maximum(cs[None, :], ss[:, None])
    hi = np.minimum(cs[None, :] + CMP_BLOCK, ss[:, None] + SEL_BLOCK)
    out = np.zeros((ns, ncp), np.float32)
    out[:, :nc] = np.maximum(hi - lo, 0) / CMP_BLOCK
    return out


def _bias_tables(rel_bias, rel, attend, shift=None):
    bucket = np.where(attend, _rel_bucket_table()[np.clip(rel, 0, REL_MAX_DIST)], REL_BUCKETS)
    rows = rel_bias if shift is None else rel_bias - shift[None, :]
    rows = jnp.concatenate([rows, jnp.full((1, rel_bias.shape[1]), NEG_INF, rel_bias.dtype)], axis=0)
    onehot = (jnp.asarray(bucket.reshape(1, -1)) == jnp.arange(REL_BUCKETS + 1, dtype=jnp.int32)[:, None]).astype(F32)
    tab = jnp.einsum('bh,bn->hn', rows, onehot, precision=HIGHEST)
    return tab.reshape((rel_bias.shape[1],) + rel.shape)


def _prep_in_proj_weight(w_in_all, l):
    D = w_in_all.shape[1]
    nsa0 = RWKV_COLS
    kv0 = nsa0 + NSA_WIDTH
    gates0 = kv0 + 6 * NSA_KV_WIDTH
    merge0 = RWKV_COLS + NSA_COLS
    wt = jnp.swapaxes(w_in_all[l], 0, 1)
    kv = wt[kv0:gates0].reshape(3, 2, NSA_KV_HEADS, HEAD_DIM, D)
    kv = jnp.transpose(kv, (2, 0, 1, 3, 4)).reshape(6 * NSA_KV_WIDTH, D)
    pad = jnp.zeros((SMALL_W - (SM_GATES + 3 * NSA_HEADS), D), wt.dtype)
    return jnp.concatenate([
        wt[0:3 * RWKV_WIDTH],
        wt[nsa0:kv0],
        wt[merge0:merge0 + 2 * D_MODEL],
        kv,
        wt[3 * RWKV_WIDTH:RWKV_COLS],
        wt[gates0:merge0],
        pad,
    ], axis=0).astype(BF16)


def _prep_rwkv_params(mu, w0, w2, a0, a2, g2, k_k, k_a, r_k, ln_w, ln_b):
    C = RWKV_WIDTH
    row = lambda z: z.reshape(1, -1).astype(F32)
    mu_s = jnp.concatenate([mu[3 * C:], jnp.zeros((SMALL_W - (RWKV_COLS - 3 * C),), F32)]).reshape(1, SMALL_W)
    zl = jnp.zeros((DECAY_LORA, C), F32)

    w2p = jnp.concatenate([w2, zl], axis=0)
    w2h = w2p.astype(BF16)
    return dict(
        mu_r=row(mu[0:C]), mu_k=row(mu[C:2 * C]), mu_v=row(mu[2 * C:3 * C]), mu_s=mu_s,
        w0=row(w0), a0=row(a0), k_k=row(k_k), k_a=row(k_a), r_k=row(r_k), ln_w=row(ln_w), ln_b=row(ln_b),
        w2h=w2h, w2l=(w2p - w2h.astype(F32)).astype(BF16),
        a2=jnp.concatenate([zl, a2], axis=0).astype(BF16),
        g2=jnp.concatenate([g2, jnp.zeros((256 - GATE_LORA, C), F32)], axis=0).astype(BF16),
    )


def _prep_nsa_params(pe_k, w1_k, w2_k, pe_v, w1_v, w2_v, q_g, k_g, rel_bias, T):
    hd = HEAD_DIM
    ones = jnp.ones((hd,), F32)
    ncp = T // CMP_STRIDE
    NQ = T // QUERY_BLOCK

    def blockdiag(a, b):
        lead = ((0, 0),) * (a.ndim - 2)
        return jnp.pad(a, lead + ((0, hd), (0, hd))) + jnp.pad(b, lead + ((hd, 0), (hd, 0)))

    w1 = blockdiag(w1_k.reshape(CMP_BLOCK, hd, hd), w1_v.reshape(CMP_BLOCK, hd, hd))
    pe = jnp.concatenate([pe_k, pe_v], axis=1)
    half = CMP_STRIDE

    spread = np.zeros((NSA_GROUP * hd, NSA_GROUP * LANES), np.float32)
    gather = np.zeros((NSA_GROUP * LANES, NSA_GROUP * hd), np.float32)
    for h in range(NSA_GROUP):
        for d in range(hd):
            spread[h * hd + d, h * LANES + d] = 1.0
            gather[h * LANES + hd + d, h * hd + d] = 1.0
    gate_sel = np.zeros((SMALL_W, NSA_KV_HEADS * LANES), np.float32)
    for g in range(NSA_KV_HEADS):
        for j in range(3 * NSA_GROUP):
            gate_sel[SM_GATES + 3 * NSA_GROUP * g + j, g * LANES + j] = 1.0
    gate_rep = np.zeros((3, LANES, NSA_GROUP * hd), np.float32)
    for c in range(3):
        for h in range(NSA_GROUP):
            gate_rep[c, 3 * h + c, h * hd:(h + 1) * hd] = 1.0

    qi = np.arange(QUERY_BLOCK)[:, None]
    rel_s = qi + (SEL_CHUNK - QUERY_BLOCK) - np.arange(SEL_CHUNK)[None, :]
    rel_w = qi + WINDOW - np.arange(WIN_KEYS)[None, :]
    rel_c = ((np.arange(NQ)[:, None, None] * QUERY_BLOCK + qi[None])
             - (np.arange(ncp)[None, None, :] * CMP_STRIDE + CMP_BLOCK - 1))
    far_bias = rel_bias[int(_rel_bucket_table()[REL_MAX_DIST])]
    stab = _bias_tables(rel_bias, rel_s, rel_s >= 0, shift=far_bias)
    wtab = _bias_tables(rel_bias, rel_w, (rel_w >= 0) & (rel_w < WINDOW))
    ctab = _bias_tables(rel_bias, rel_c, rel_c >= 0)
    return dict(
        stab=stab, wtab=wtab, ctab=ctab,
        q_g=jnp.tile(q_g, NSA_GROUP).reshape(1, -1),
        kg_cmp=jnp.concatenate([k_g[0], ones]).reshape(1, LANES),
        kg_sel=jnp.concatenate([k_g[1], ones]).reshape(1, LANES),
        kg_win=jnp.concatenate([k_g[2], ones]).reshape(1, LANES),
        pe_a=pe[:half].reshape(1, half * LANES), pe_b=pe[half:].reshape(1, half * LANES),
        cw_a=w1[:half].reshape(half * LANES, LANES).astype(BF16),
        cw_b=w1[half:].reshape(half * LANES, LANES).astype(BF16),
        cw2=blockdiag(w2_k, w2_v).astype(BF16),
        q_spread=jnp.asarray(spread, BF16), gather=jnp.asarray(gather, BF16),
        gate_sel=jnp.asarray(gate_sel), gate_rep=jnp.asarray(gate_rep, BF16),
        sel_mt=jnp.asarray(_sel_to_cmp_matrix(T, ncp)),
    )


def kernel(x, c, w_ada, b_ada, norm1_g, norm2_g, w_in, rwkv_mu, rwkv_w0, rwkv_w2, rwkv_a0, rwkv_a2, rwkv_g2, rwkv_k_k, rwkv_k_a, rwkv_r_k, rwkv_ln_w, rwkv_ln_b, cmp_pe_k, cmp_w1_k, cmp_w2_k, cmp_pe_v, cmp_w1_v, cmp_w2_v, q_norm_g, k_norm_g, rel_bias, w_o_rwkv, w_o_nsa, w_out, w_up, w_down):
    B, T, D = x.shape
    depth = w_in.shape[0]
    x2 = x.reshape(B * T, D)
    for l in range(depth):
        mod6 = _ada_mod(c, w_ada[l], b_ada[l]).reshape(B * 6, 1, D)
        proj, small = _in_proj(x2, norm1_g[l].reshape(1, D), mod6, _prep_in_proj_weight(w_in, l), T)
        pr = _prep_rwkv_params(rwkv_mu[l], rwkv_w0[l], rwkv_w2[l], rwkv_a0[l], rwkv_a2[l], rwkv_g2[l],
                               rwkv_k_k[l], rwkv_k_a[l], rwkv_r_k[l], rwkv_ln_w[l], rwkv_ln_b[l])
        o_a = _rwkv_mix(proj, small, pr, B, T)
        pn = _prep_nsa_params(cmp_pe_k[l], cmp_w1_k[l], cmp_w2_k[l], cmp_pe_v[l], cmp_w1_v[l], cmp_w2_v[l],
                              q_norm_g[l], k_norm_g[l], rel_bias, T)
        qp, ks, vs, kw, vw, gates = _nsa_prep(proj, small, pn, B, T)
        kc, vc = _nsa_compress(proj, pn, B, T)
        o_b = _nsa_attn(qp, kc, vc, ks, vs, kw, vw, gates, pn, B, T)
        x1, h2 = _merge_out_proj(o_a, o_b, w_o_rwkv[l].astype(BF16), w_o_nsa[l].astype(BF16), proj,
                                 w_out[l].astype(BF16), x2, mod6, norm2_g[l].reshape(1, D), T)
        x2 = _mlp(h2, w_up[l].astype(BF16), w_down[l].astype(BF16), x1, mod6, T)
    return x2.reshape(B, T, D)
```
